```python
import jax, jax.numpy as jnp
from jax import lax
import numpy as np

D_MODEL = 2048
BATCH = 8
SEQ = 8192
DEPTH = 1

GDN_HEADS = 8
GDN_DK = D_MODEL // 16
GDN_DV = D_MODEL // GDN_HEADS
GDN_QK = GDN_HEADS * GDN_DK
GDN_V = GDN_HEADS * GDN_DV
CONV_WIDTH = 4
RET_HEADS = 8
RET_DK = D_MODEL // 16
RET_DV = D_MODEL // RET_HEADS
RET_QK = RET_HEADS * RET_DK
RET_V = RET_HEADS * RET_DV
ROPE_BASE = 10000.0
CHUNK = 64
D_FF = ((8 * D_MODEL // 3 + 255) // 256) * 256
EPS = 1e-6
IN_SPLITS = (2 * GDN_QK + GDN_V, GDN_V, GDN_HEADS, GDN_HEADS,
             RET_QK, RET_QK, RET_V, RET_V, D_MODEL, D_MODEL)
N_IN = sum(IN_SPLITS)

kernel_name = "hybrid_gdn_retention_gated_merge"


def rmsnorm(x, w):
    xf = x.astype(jnp.float32)
    xf = xf * lax.rsqrt(jnp.mean(xf * xf, axis=-1, keepdims=True) + EPS)
    return xf.astype(x.dtype) * w


def group_norm_heads(o, w):
    mu = jnp.mean(o, axis=-1, keepdims=True)
    var = jnp.mean(jnp.square(o - mu), axis=-1, keepdims=True)
    return (o - mu) * lax.rsqrt(var + EPS) * w.reshape(o.shape[2], o.shape[3]).astype(jnp.float32)


def l2norm(x):
    return x * lax.rsqrt(jnp.sum(x * x, axis=-1, keepdims=True) + EPS)


def split_cols(p):
    offs = [int(o) for o in np.cumsum(IN_SPLITS)[:-1]]
    return jnp.split(p, offs, axis=-1)


def causal_conv(x, w):
    width, s = w.shape[0], x.shape[1]
    xp = jnp.pad(x, ((0, 0), (width - 1, 0), (0, 0)))
    out = xp[:, 0:s] * w[0]
    for j in range(1, width):
        out = out + xp[:, j:j + s] * w[j]
    return out


def rotary(x):
    d, s = x.shape[-1], x.shape[1]
    inv = ROPE_BASE ** (-jnp.arange(0, d, 2, dtype=jnp.float32) / d)
    ang = jnp.arange(s, dtype=jnp.float32)[:, None] * inv[None, :]
    cos = jnp.cos(ang)[None, :, None, :]
    sin = jnp.sin(ang)[None, :, None, :]
    x1, x2 = x[..., : d // 2], x[..., d // 2:]
    return jnp.concatenate([x1 * cos - x2 * sin, x2 * cos + x1 * sin], axis=-1)


def to_chunks(t):
    b, s, h, d = t.shape
    return t.reshape(b, s // CHUNK, CHUNK, h, d).transpose(0, 3, 1, 2, 4)


def to_chunks_scalar(t):
    b, s, h = t.shape
    return t.reshape(b, s // CHUNK, CHUNK, h).transpose(0, 3, 1, 2)


def from_scan(o):
    n, b, h, c, d = o.shape
    return o.transpose(1, 0, 3, 2, 4).reshape(b, n * c, h, d)


def gated_delta_rule(q, k, v, beta, g):
    c = q.shape[-2]
    incl = jnp.tril(jnp.ones((c, c), dtype=bool))
    strict = jnp.tril(jnp.ones((c, c), dtype=bool), -1)
    g = jnp.cumsum(g, axis=-1)
    decay = jnp.exp(jnp.where(incl, g[..., :, None] - g[..., None, :], -jnp.inf))
    kb = k * beta[..., None]
    a = jnp.where(strict, jnp.einsum('bhnid,bhnjd->bhnij', kb, k) * decay, 0.0)
    t = a + jnp.eye(c, dtype=a.dtype)
    u = lax.linalg.triangular_solve(t, v * beta[..., None], left_side=True, lower=True, unit_diagonal=True)
    w = lax.linalg.triangular_solve(t, kb * jnp.exp(g)[..., None], left_side=True, lower=True, unit_diagonal=True)
    attn = jnp.einsum('bhnid,bhnjd->bhnij', q, k) * decay

    def step(state, xs):
        q_c, k_c, u_c, w_c, g_c, attn_c = xs
        v_new = u_c - jnp.einsum('bhcd,bhde->bhce', w_c, state)
        o_c = (jnp.einsum('bhcd,bhde->bhce', q_c * jnp.exp(g_c)[..., None], state)
               + jnp.einsum('bhij,bhje->bhie', attn_c, v_new))
        g_last = g_c[..., -1:]
        state = (state * jnp.exp(g_last)[..., None]
                 + jnp.einsum('bhcd,bhce->bhde', k_c * jnp.exp(g_last - g_c)[..., None], v_new))
        return state, o_c

    b, h, _, _, dk = q.shape
    dv = v.shape[-1]
    s0 = jnp.zeros((b, h, dk, dv), jnp.float32)
    xs = tuple(jnp.moveaxis(z, 2, 0) for z in (q, k, u, w, g, attn))
    _, o = lax.scan(step, s0, xs)
    return from_scan(o)


def retention_chunked(q, k, v, log_gamma):
    c = q.shape[-2]
    pos = jnp.arange(c, dtype=jnp.float32)
    dist = pos[:, None] - pos[None, :]
    dmat = jnp.exp(jnp.where(dist >= 0, dist * log_gamma[:, None, None], -jnp.inf))
    scores = jnp.einsum('bhnid,bhnjd->bhnij', q, k) * dmat[:, None]
    intra = jnp.einsum('bhnij,bhnje->bhnie', scores, v)
    xi = jnp.exp((pos + 1.0) * log_gamma[:, None])[:, :, None]
    zeta = jnp.exp((c - 1.0 - pos) * log_gamma[:, None])[:, :, None]
    gamma_c = jnp.exp(c * log_gamma)[:, None, None]

    def step(state, xs):
        q_c, k_c, v_c = xs
        o_c = jnp.einsum('bhcd,bhde->bhce', q_c, state) * xi
        state = state * gamma_c + jnp.einsum('bhcd,bhce->bhde', k_c * zeta, v_c)
        return state, o_c

    b, h, _, _, dk = q.shape
    dv = v.shape[-1]
    s0 = jnp.zeros((b, h, dk, dv), jnp.float32)
    xs = tuple(jnp.moveaxis(z, 2, 0) for z in (q, k, v))
    _, inter = lax.scan(step, s0, xs)
    return from_scan(jnp.moveaxis(intra, 2, 0) + inter)


def _fwd_setup_inputs(seed: int = 0) -> dict:
    key = jax.random.key(seed)
    ks = jax.random.split(key, 16)
    f32 = jnp.float32
    x = jax.random.normal(ks[0], (BATCH, SEQ, D_MODEL), f32)
    norm1_w = 1.0 + 0.02 * jax.random.normal(ks[1], (DEPTH, D_MODEL), f32)
    w_in = jax.random.normal(ks[2], (DEPTH, D_MODEL, N_IN), f32) * D_MODEL ** -0.5
    conv_w = jax.random.normal(ks[3], (DEPTH, CONV_WIDTH, 2 * GDN_QK + GDN_V), f32) * CONV_WIDTH ** -0.5
    a_log = jnp.log(jax.random.uniform(ks[4], (DEPTH, GDN_HEADS), f32, 1.0, 16.0))
    dt = jnp.exp(jax.random.uniform(ks[5], (DEPTH, GDN_HEADS), f32, float(np.log(1e-3)), float(np.log(1e-1))))
    dt_bias = dt + jnp.log(-jnp.expm1(-dt))
    gdn_norm_w = 1.0 + 0.02 * jax.random.normal(ks[6], (DEPTH, GDN_DV), f32)
    ret_norm_w = 1.0 + 0.02 * jax.random.normal(ks[7], (DEPTH, RET_V), f32)
    w_out = jax.random.normal(ks[8], (DEPTH, D_MODEL, D_MODEL), f32) * D_MODEL ** -0.5
    norm2_w = 1.0 + 0.02 * jax.random.normal(ks[9], (DEPTH, D_MODEL), f32)
    w_gate = jax.random.normal(ks[10], (DEPTH, D_MODEL, D_FF), f32) * D_MODEL ** -0.5
    w_up = jax.random.normal(ks[11], (DEPTH, D_MODEL, D_FF), f32) * D_MODEL ** -0.5
    w_down = jax.random.normal(ks[12], (DEPTH, D_FF, D_MODEL), f32) * D_FF ** -0.5
    norm_f_w = 1.0 + 0.02 * jax.random.normal(ks[13], (D_MODEL,), f32)
    return {"x": x, "norm1_w": norm1_w, "w_in": w_in, "conv_w": conv_w, "a_log": a_log,
            "dt_bias": dt_bias, "gdn_norm_w": gdn_norm_w, "ret_norm_w": ret_norm_w,
            "w_out": w_out, "norm2_w": norm2_w, "w_gate": w_gate, "w_up": w_up,
            "w_down": w_down, "norm_f_w": norm_f_w}


def _fwd_reference(x, norm1_w, w_in, conv_w, a_log, dt_bias, gdn_norm_w, ret_norm_w,
              w_out, norm2_w, w_gate, w_up, w_down, norm_f_w):
    b, s, _ = x.shape
    f32 = jnp.float32
    log_gamma = jnp.log1p(-jnp.exp2(-5.0 - jnp.arange(RET_HEADS, dtype=f32)))
    h = x
    for l in range(DEPTH):
        u = rmsnorm(h, norm1_w[l])
        proj = u @ w_in[l]
        a_qkv, a_z, a_b, a_a, r_q, r_k, r_v, r_g, gate_a, gate_b = split_cols(proj)

        qkv = jax.nn.silu(causal_conv(a_qkv, conv_w[l])).astype(f32)
        q_a = l2norm(qkv[..., :GDN_QK].reshape(b, s, GDN_HEADS, GDN_DK)) * GDN_DK ** -0.5
        k_a = l2norm(qkv[..., GDN_QK:2 * GDN_QK].reshape(b, s, GDN_HEADS, GDN_DK))
        v_a = qkv[..., 2 * GDN_QK:].reshape(b, s, GDN_HEADS, GDN_DV)
        beta = jax.nn.sigmoid(a_b.astype(f32))
        g = -jnp.exp(a_log[l].astype(f32)) * jax.nn.softplus(a_a.astype(f32) + dt_bias[l].astype(f32))
        o_a = gated_delta_rule(to_chunks(q_a), to_chunks(k_a), to_chunks(v_a),
                               to_chunks_scalar(beta), to_chunks_scalar(g))
        o_a = rmsnorm(o_a, gdn_norm_w[l].astype(f32)) * jax.nn.silu(a_z.astype(f32).reshape(b, s, GDN_HEADS, GDN_DV))
        o_a = o_a.reshape(b, s, GDN_V).astype(x.dtype)

        q_b = rotary(r_q.astype(f32).reshape(b, s, RET_HEADS, RET_DK))
        k_b = rotary(r_k.astype(f32).reshape(b, s, RET_HEADS, RET_DK)) * RET_DK ** -0.5
        v_b = r_v.astype(f32).reshape(b, s, RET_HEADS, RET_DV)
        o_b = retention_chunked(to_chunks(q_b), to_chunks(k_b), to_chunks(v_b), log_gamma)
        o_b = group_norm_heads(o_b, ret_norm_w[l]).reshape(b, s, RET_V) * jax.nn.silu(r_g.astype(f32))
        o_b = o_b.astype(x.dtype)

        mixed = jax.nn.sigmoid(gate_a) * o_a + jax.nn.sigmoid(gate_b) * o_b
        h = h + mixed @ w_out[l]

        hn = rmsnorm(h, norm2_w[l])
        h = h + (jax.nn.silu(hn @ w_gate[l]) * (hn @ w_up[l])) @ w_down[l]
    return rmsnorm(h, norm_f_w)


import jax as _jax
import jax.numpy as _jnp

TWIN_FORMAT = 'train_step'
FWD_PARAMS = ['x', 'norm1_w', 'w_in', 'conv_w', 'a_log', 'dt_bias', 'gdn_norm_w', 'ret_norm_w', 'w_out', 'norm2_w', 'w_gate', 'w_up', 'w_down', 'norm_f_w']
TWIN_WEIGHTS = ['norm1_w', 'w_in', 'conv_w', 'a_log', 'dt_bias', 'gdn_norm_w', 'ret_norm_w', 'w_out', 'norm2_w', 'w_gate', 'w_up', 'w_down', 'norm_f_w']
TWIN_DIFF_INPUT = 'x'
TWIN_INPUTS = ['x', 'norm1_w', 'w_in', 'conv_w', 'a_log', 'dt_bias', 'gdn_norm_w', 'ret_norm_w', 'w_out', 'norm2_w', 'w_gate', 'w_up', 'w_down', 'norm_f_w', 'loss_target', 'm_norm1_w', 'm_w_in', 'm_conv_w', 'm_a_log', 'm_dt_bias', 'm_gdn_norm_w', 'm_ret_norm_w', 'm_w_out', 'm_norm2_w', 'm_w_gate', 'm_w_up', 'm_w_down', 'm_norm_f_w', 'v_norm1_w', 'v_w_in', 'v_conv_w', 'v_a_log', 'v_dt_bias', 'v_gdn_norm_w', 'v_ret_norm_w', 'v_w_out', 'v_norm2_w', 'v_w_gate', 'v_w_up', 'v_w_down', 'v_norm_f_w']
TWIN_OUTPUTS = ['loss', 'grad_x', 'grad_norm1_w', 'grad_w_in', 'grad_conv_w', 'grad_a_log', 'grad_dt_bias', 'grad_gdn_norm_w', 'grad_ret_norm_w', 'grad_w_out', 'grad_norm2_w', 'grad_w_gate', 'grad_w_up', 'grad_w_down', 'grad_norm_f_w', 'delta_norm1_w', 'delta_w_in', 'delta_conv_w', 'delta_a_log', 'delta_dt_bias', 'delta_gdn_norm_w', 'delta_ret_norm_w', 'delta_w_out', 'delta_norm2_w', 'delta_w_gate', 'delta_w_up', 'delta_w_down', 'delta_norm_f_w', 'new_m_norm1_w', 'new_m_w_in', 'new_m_conv_w', 'new_m_a_log', 'new_m_dt_bias', 'new_m_gdn_norm_w', 'new_m_ret_norm_w', 'new_m_w_out', 'new_m_norm2_w', 'new_m_w_gate', 'new_m_w_up', 'new_m_w_down', 'new_m_norm_f_w', 'new_v_norm1_w', 'new_v_w_in', 'new_v_conv_w', 'new_v_a_log', 'new_v_dt_bias', 'new_v_gdn_norm_w', 'new_v_ret_norm_w', 'new_v_w_out', 'new_v_norm2_w', 'new_v_w_gate', 'new_v_w_up', 'new_v_w_down', 'new_v_norm_f_w']
TWIN_LEAF_KINDS = {'loss': 'loss', 'grad_x': 'grad_x', 'grad_norm1_w': 'grad_w', 'grad_w_in': 'grad_w', 'grad_conv_w': 'grad_w', 'grad_a_log': 'grad_w', 'grad_dt_bias': 'grad_w', 'grad_gdn_norm_w': 'grad_w', 'grad_ret_norm_w': 'grad_w', 'grad_w_out': 'grad_w', 'grad_norm2_w': 'grad_w', 'grad_w_gate': 'grad_w', 'grad_w_up': 'grad_w', 'grad_w_down': 'grad_w', 'grad_norm_f_w': 'grad_w', 'delta_norm1_w': 'delta_w', 'delta_w_in': 'delta_w', 'delta_conv_w': 'delta_w', 'delta_a_log': 'delta_w', 'delta_dt_bias': 'delta_w', 'delta_gdn_norm_w': 'delta_w', 'delta_ret_norm_w': 'delta_w', 'delta_w_out': 'delta_w', 'delta_norm2_w': 'delta_w', 'delta_w_gate': 'delta_w', 'delta_w_up': 'delta_w', 'delta_w_down': 'delta_w', 'delta_norm_f_w': 'delta_w', 'new_m_norm1_w': 'new_m', 'new_m_w_in': 'new_m', 'new_m_conv_w': 'new_m', 'new_m_a_log': 'new_m', 'new_m_dt_bias': 'new_m', 'new_m_gdn_norm_w': 'new_m', 'new_m_ret_norm_w': 'new_m', 'new_m_w_out': 'new_m', 'new_m_norm2_w': 'new_m', 'new_m_w_gate': 'new_m', 'new_m_w_up': 'new_m', 'new_m_w_down': 'new_m', 'new_m_norm_f_w': 'new_m', 'new_v_norm1_w': 'new_v', 'new_v_w_in': 'new_v', 'new_v_conv_w': 'new_v', 'new_v_a_log': 'new_v', 'new_v_dt_bias': 'new_v', 'new_v_gdn_norm_w': 'new_v', 'new_v_ret_norm_w': 'new_v', 'new_v_w_out': 'new_v', 'new_v_norm2_w': 'new_v', 'new_v_w_gate': 'new_v', 'new_v_w_up': 'new_v', 'new_v_w_down': 'new_v', 'new_v_norm_f_w': 'new_v'}


def _forward(args):
    return _fwd_reference(*[args[k] for k in FWD_PARAMS])


def _output_shape():
    def fwd():
        inp = _fwd_setup_inputs(0)
        return _fwd_reference(*[inp[k] for k in FWD_PARAMS])
    out = _jax.eval_shape(fwd)
    return out.shape, out.dtype

N_MICROBATCH = 1
ADAM_LR = 0.001
ADAM_B1 = 0.9
ADAM_B2 = 0.999
ADAM_EPS = 1e-08
ADAM_WD = 0.01
ADAM_STEP = 10
PER_EXAMPLE_BATCH_AXIS = {'x': 0, 'loss_target': 0}
SHARED_INPUTS = []
_WEIGHT_DTYPES = {'norm1_w': _jnp.float32, 'w_in': _jnp.float32, 'conv_w': _jnp.float32, 'a_log': _jnp.float32, 'dt_bias': _jnp.float32, 'gdn_norm_w': _jnp.float32, 'ret_norm_w': _jnp.float32, 'w_out': _jnp.float32, 'norm2_w': _jnp.float32, 'w_gate': _jnp.float32, 'w_up': _jnp.float32, 'w_down': _jnp.float32, 'norm_f_w': _jnp.float32}
MOMENT_SCALE = {'norm1_w': 1.117017e-01, 'w_in': 3.875264e-02, 'conv_w': 3.722139e-02, 'a_log': 3.093816e-01, 'dt_bias': 2.931174e-01, 'gdn_norm_w': 1.209666e-01, 'ret_norm_w': 4.269199e-02, 'w_out': 5.773238e-02, 'norm2_w': 8.627858e-02, 'w_gate': 3.719093e-02, 'w_up': 3.605402e-02, 'w_down': 5.979873e-02, 'norm_f_w': 3.197881e+01}


def _to_microbatches(a, axis):
    t = _jnp.moveaxis(a, axis, 0)
    t = t.reshape((N_MICROBATCH, t.shape[0] // N_MICROBATCH) + t.shape[1:])
    return _jnp.moveaxis(t, 1, axis + 1)


def setup_inputs(seed: int = 0) -> dict:
    inp = _fwd_setup_inputs(seed)
    key = _jax.random.fold_in(_jax.random.key(seed), 7919)
    shape, _ = _output_shape()
    out = dict(inp)
    out["loss_target"] = _jax.random.normal(_jax.random.fold_in(key, 0), shape, _jnp.float32)
    for i, name in enumerate(TWIN_WEIGHTS):
        w = inp[name].astype(_jnp.float32)
        if MOMENT_SCALE is None:
            s = _jnp.sqrt(_jnp.mean(_jnp.square(w)) + 1e-30)
        else:
            s = MOMENT_SCALE[name]
        km, kv = _jax.random.split(_jax.random.fold_in(key, i + 1))
        out[name] = w
        out["m_" + name] = s * _jax.random.normal(km, w.shape, _jnp.float32)
        out["v_" + name] = (s * s) * _jax.random.uniform(kv, w.shape, _jnp.float32, 0.5, 1.5)
    if N_MICROBATCH > 1:
        for name, axis in PER_EXAMPLE_BATCH_AXIS.items():
            out[name] = _to_microbatches(out[name], axis)
    return {'x': out['x'], 'norm1_w': out['norm1_w'], 'w_in': out['w_in'], 'conv_w': out['conv_w'], 'a_log': out['a_log'], 'dt_bias': out['dt_bias'], 'gdn_norm_w': out['gdn_norm_w'], 'ret_norm_w': out['ret_norm_w'], 'w_out': out['w_out'], 'norm2_w': out['norm2_w'], 'w_gate': out['w_gate'], 'w_up': out['w_up'], 'w_down': out['w_down'], 'norm_f_w': out['norm_f_w'], 'loss_target': out['loss_target'], 'm_norm1_w': out['m_norm1_w'], 'm_w_in': out['m_w_in'], 'm_conv_w': out['m_conv_w'], 'm_a_log': out['m_a_log'], 'm_dt_bias': out['m_dt_bias'], 'm_gdn_norm_w': out['m_gdn_norm_w'], 'm_ret_norm_w': out['m_ret_norm_w'], 'm_w_out': out['m_w_out'], 'm_norm2_w': out['m_norm2_w'], 'm_w_gate': out['m_w_gate'], 'm_w_up': out['m_w_up'], 'm_w_down': out['m_w_down'], 'm_norm_f_w': out['m_norm_f_w'], 'v_norm1_w': out['v_norm1_w'], 'v_w_in': out['v_w_in'], 'v_conv_w': out['v_conv_w'], 'v_a_log': out['v_a_log'], 'v_dt_bias': out['v_dt_bias'], 'v_gdn_norm_w': out['v_gdn_norm_w'], 'v_ret_norm_w': out['v_ret_norm_w'], 'v_w_out': out['v_w_out'], 'v_norm2_w': out['v_norm2_w'], 'v_w_gate': out['v_w_gate'], 'v_w_up': out['v_w_up'], 'v_w_down': out['v_w_down'], 'v_norm_f_w': out['v_norm_f_w']}


def _loss(weights, diff, rest, loss_target):
    with _jax.named_scope("forward"):
        args = {**rest, TWIN_DIFF_INPUT: diff, **{k: w.astype(_WEIGHT_DTYPES[k]) for k, w in weights.items()}}
        y = _forward(args)
    with _jax.named_scope("loss_head"):
        err = _jnp.square(y.astype(_jnp.float32) - loss_target)
        return 0.5 * _jnp.sum(_jnp.mean(err, axis=-1)) if err.ndim else 0.5 * err


def _adamw(w, g, m, v):
    m = ADAM_B1 * m + (1.0 - ADAM_B1) * g
    v = ADAM_B2 * v + (1.0 - ADAM_B2) * _jnp.square(g)
    m_hat = m / (1.0 - ADAM_B1 ** ADAM_STEP)
    v_hat = v / (1.0 - ADAM_B2 ** ADAM_STEP)
    delta = -ADAM_LR * (m_hat / (_jnp.sqrt(v_hat) + ADAM_EPS) + ADAM_WD * w)
    return delta, m, v


def reference(x, norm1_w, w_in, conv_w, a_log, dt_bias, gdn_norm_w, ret_norm_w, w_out, norm2_w, w_gate, w_up, w_down, norm_f_w, loss_target, m_norm1_w, m_w_in, m_conv_w, m_a_log, m_dt_bias, m_gdn_norm_w, m_ret_norm_w, m_w_out, m_norm2_w, m_w_gate, m_w_up, m_w_down, m_norm_f_w, v_norm1_w, v_w_in, v_conv_w, v_a_log, v_dt_bias, v_gdn_norm_w, v_ret_norm_w, v_w_out, v_norm2_w, v_w_gate, v_w_up, v_w_down, v_norm_f_w):
    given = dict(x=x, norm1_w=norm1_w, w_in=w_in, conv_w=conv_w, a_log=a_log, dt_bias=dt_bias, gdn_norm_w=gdn_norm_w, ret_norm_w=ret_norm_w, w_out=w_out, norm2_w=norm2_w, w_gate=w_gate, w_up=w_up, w_down=w_down, norm_f_w=norm_f_w, loss_target=loss_target, m_norm1_w=m_norm1_w, m_w_in=m_w_in, m_conv_w=m_conv_w, m_a_log=m_a_log, m_dt_bias=m_dt_bias, m_gdn_norm_w=m_gdn_norm_w, m_ret_norm_w=m_ret_norm_w, m_w_out=m_w_out, m_norm2_w=m_norm2_w, m_w_gate=m_w_gate, m_w_up=m_w_up, m_w_down=m_w_down, m_norm_f_w=m_norm_f_w, v_norm1_w=v_norm1_w, v_w_in=v_w_in, v_conv_w=v_conv_w, v_a_log=v_a_log, v_dt_bias=v_dt_bias, v_gdn_norm_w=v_gdn_norm_w, v_ret_norm_w=v_ret_norm_w, v_w_out=v_w_out, v_norm2_w=v_norm2_w, v_w_gate=v_w_gate, v_w_up=v_w_up, v_w_down=v_w_down, v_norm_f_w=v_norm_f_w)
    weights = {n: given[n] for n in TWIN_WEIGHTS}
    shared = {n: given[n] for n in SHARED_INPUTS}
    per_example = {n: given[n] for n in ['x']}
    grad_fn = _jax.value_and_grad(_loss, argnums=(0, 1))

    def one_microbatch(ex, loss_target):
        ex = dict(ex)
        diff = ex.pop(TWIN_DIFF_INPUT)
        return grad_fn(weights, diff, {**shared, **ex}, loss_target)

    if N_MICROBATCH == 1:
        loss, (grad_w, grad_x) = one_microbatch(per_example, given["loss_target"])
    else:
        def body(carry, xs):
            loss_sum, grad_sum = carry
            l_k, (gw_k, gx_k) = one_microbatch(xs[0], xs[1])
            with _jax.named_scope("update"):
                return (loss_sum + l_k, _jax.tree.map(_jnp.add, grad_sum, gw_k)), gx_k

        init = (_jnp.zeros((), _jnp.float32), _jax.tree.map(_jnp.zeros_like, weights))
        (loss, grad_w), grad_x = _jax.lax.scan(body, init, (per_example, given["loss_target"]))
    with _jax.named_scope("update"):
        delta_w, new_m, new_v = {}, {}, {}
        for n in TWIN_WEIGHTS:
            delta_w[n], new_m[n], new_v[n] = _adamw(weights[n], grad_w[n], given["m_" + n], given["v_" + n])
    return (loss, grad_x, *[grad_w[n] for n in TWIN_WEIGHTS], *[delta_w[n] for n in TWIN_WEIGHTS],
            *[new_m[n] for n in TWIN_WEIGHTS], *[new_v[n] for n in TWIN_WEIGHTS])
```

```python
import functools

import jax
import jax.numpy as jnp
import numpy as np
from jax import lax
from jax.experimental import pallas as pl
from jax.experimental.pallas import tpu as pltpu

F32 = jnp.float32
BF16 = jnp.bfloat16
MESH = pl.DeviceIdType.MESH

HEADS = 8
DK = 128
DV = 256
CHUNK = 64
CONV_W = 4
EPS = 1e-6
ROPE_BASE = 10000.0
ADAM_LR, ADAM_B1, ADAM_B2, ADAM_EPS, ADAM_WD, ADAM_STEP = 0.001, 0.9, 0.999, 1e-08, 0.01, 10
N_CHIPS = 4
N_DEV = 8
LANES = 128
HALO = 8
VMEM_LIMIT = 48 * 1024 * 1024


def _pick(n, cands):
    for c in cands:
        if n % c == 0:
            return c
    raise ValueError(f"no tile for {n} in {cands}")


def _params(sem=None):
    return pltpu.CompilerParams(dimension_semantics=sem, vmem_limit_bytes=VMEM_LIMIT)


def _dot(a, b):
    return jnp.dot(a.astype(BF16), b.astype(BF16), preferred_element_type=F32)


def _dot_nt(a, b):
    return lax.dot_general(a.astype(BF16), b.astype(BF16), (((1,), (1,)), ((), ())), preferred_element_type=F32)


def _dot_tn(a, b):
    return lax.dot_general(a.astype(BF16), b.astype(BF16), (((0,), (0,)), ((), ())), preferred_element_type=F32)


def _dot_hi(a, b):
    return jnp.dot(a, b, precision=lax.Precision.HIGHEST, preferred_element_type=F32)


def _dot_tn_hi(a, b):
    return lax.dot_general(a, b, (((0,), (0,)), ((), ())), precision=lax.Precision.HIGHEST,
                           preferred_element_type=F32)


def _sigmoid(x):
    return 1.0 / (1.0 + jnp.exp(-x))


def _iota2(n):
    return lax.broadcasted_iota(jnp.int32, (n, n), 0), lax.broadcasted_iota(jnp.int32, (n, n), 1)


def _row_to_col(row):
    n = row.shape[1]
    r, c = _iota2(n)
    return jnp.sum(jnp.where(r == c, jnp.broadcast_to(row, (n, n)), 0.0), axis=1, keepdims=True)


def _col_to_row(col):
    n = col.shape[0]
    r, c = _iota2(n)
    return jnp.sum(jnp.where(r == c, jnp.broadcast_to(col, (n, n)), 0.0), axis=0, keepdims=True)


def _mm(a, b, *, name, ta=False, tb=False, out_dtype=F32, res=None):
    m, k = (a.shape[1], a.shape[0]) if ta else a.shape
    n = b.shape[0] if tb else b.shape[1]
    tm = _pick(m, (512, 256, 128))
    tn = _pick(n, (1024, 512, 256, 128))
    tk = _pick(k, (1024, 512, 256, 128))
    nk = k // tk
    dn = (((0 if ta else 1,), (1 if tb else 0,)), ((), ()))

    def body(*refs):
        if res is None:
            a_ref, b_ref, o_ref, acc = refs
        else:
            a_ref, b_ref, r_ref, o_ref, acc = refs
        kk = pl.program_id(2)

        @pl.when(kk == 0)
        def _():
            acc[...] = jnp.zeros_like(acc)

        acc[...] += lax.dot_general(a_ref[...], b_ref[...], dn, preferred_element_type=F32)

        @pl.when(kk == nk - 1)
        def _():
            r = acc[...]
            if res is not None:
                r = r + r_ref[...]
            o_ref[...] = r.astype(out_dtype)

    a_spec = pl.BlockSpec((tk, tm), lambda i, j, kk: (kk, i)) if ta else pl.BlockSpec((tm, tk), lambda i, j, kk: (i, kk))
    b_spec = pl.BlockSpec((tn, tk), lambda i, j, kk: (j, kk)) if tb else pl.BlockSpec((tk, tn), lambda i, j, kk: (kk, j))
    o_spec = pl.BlockSpec((tm, tn), lambda i, j, kk: (i, j))
    in_specs, args = [a_spec, b_spec], [a, b]
    if res is not None:
        in_specs.append(o_spec)
        args.append(res)
    return pl.pallas_call(
        body, name=name, grid=(m // tm, n // tn, nk),
        in_specs=in_specs, out_specs=o_spec,
        out_shape=jax.ShapeDtypeStruct((m, n), out_dtype),
        scratch_shapes=[pltpu.VMEM((tm, tn), F32)],
        compiler_params=_params(("parallel", "parallel", "arbitrary")),
    )(*args)


def _rms_fwd(x, w, *, name):
    t, d = x.shape
    tt = _pick(t, (512, 256))

    def body(x_ref, w_ref, o_ref):
        xv = x_ref[...]
        r = lax.rsqrt(jnp.mean(xv * xv, axis=-1, keepdims=True) + EPS)
        o_ref[...] = (xv * r * w_ref[...]).astype(BF16)

    return pl.pallas_call(
        body, name=name, grid=(t // tt,),
        in_specs=[pl.BlockSpec((tt, d), lambda i: (i, 0)), pl.BlockSpec((1, d), lambda i: (0, 0))],
        out_specs=pl.BlockSpec((tt, d), lambda i: (i, 0)),
        out_shape=jax.ShapeDtypeStruct((t, d), BF16),
        compiler_params=_params(("parallel",)),
    )(x, w)


def _rms_bwd(dn, x, w, dres, *, name):
    t, d = x.shape
    tt = _pick(t, (256,))

    def body(dn_ref, x_ref, w_ref, dres_ref, dx_ref, dxb_ref, dw_ref):
        xv, g = x_ref[...], dn_ref[...]
        r = lax.rsqrt(jnp.mean(xv * xv, axis=-1, keepdims=True) + EPS)
        xh = xv * r
        gw = g * w_ref[...]
        dx = dres_ref[...] + r * (gw - xh * jnp.mean(gw * xh, axis=-1, keepdims=True))
        dx_ref[...] = dx
        dxb_ref[...] = dx.astype(BF16)

        @pl.when(pl.program_id(0) == 0)
        def _():
            dw_ref[...] = jnp.zeros_like(dw_ref)

        dw_ref[...] += jnp.sum(g * xh, axis=0, keepdims=True)

    blk = pl.BlockSpec((tt, d), lambda i: (i, 0))
    row = pl.BlockSpec((1, d), lambda i: (0, 0))
    return pl.pallas_call(
        body, name=name, grid=(t // tt,),
        in_specs=[blk, blk, row, blk], out_specs=[blk, blk, row],
        out_shape=[jax.ShapeDtypeStruct((t, d), F32), jax.ShapeDtypeStruct((t, d), BF16),
                   jax.ShapeDtypeStruct((1, d), F32)],
        compiler_params=_params(("arbitrary",)),
    )(dn, x, w, dres)


def _loss_head(h2, tgt, wf, *, name):
    t, d = h2.shape
    tt = _pick(t, (256,))

    def body(x_ref, t_ref, w_ref, loss_ref, dx_ref, dxb_ref, dw_ref):
        xv = x_ref[...]
        r = lax.rsqrt(jnp.mean(xv * xv, axis=-1, keepdims=True) + EPS)
        xh = xv * r
        err = xh * w_ref[...] - t_ref[...]
        lpart = 0.5 * jnp.sum(jnp.mean(err * err, axis=-1, keepdims=True), axis=0, keepdims=True)
        dy = err * (1.0 / d)
        gw = dy * w_ref[...]
        dx = r * (gw - xh * jnp.mean(gw * xh, axis=-1, keepdims=True))
        dx_ref[...] = dx
        dxb_ref[...] = dx.astype(BF16)

        @pl.when(pl.program_id(0) == 0)
        def _():
            dw_ref[...] = jnp.zeros_like(dw_ref)
            loss_ref[...] = jnp.zeros_like(loss_ref)

        dw_ref[...] += jnp.sum(dy * xh, axis=0, keepdims=True)
        loss_ref[...] += jnp.broadcast_to(lpart, loss_ref.shape)

    blk = pl.BlockSpec((tt, d), lambda i: (i, 0))
    row = pl.BlockSpec((1, d), lambda i: (0, 0))
    lrow = pl.BlockSpec((1, LANES), lambda i: (0, 0))
    return pl.pallas_call(
        body, name=name, grid=(t // tt,),
        in_specs=[blk, blk, row], out_specs=[lrow, blk, blk, row],
        out_shape=[jax.ShapeDtypeStruct((1, LANES), F32), jax.ShapeDtypeStruct((t, d), F32),
                   jax.ShapeDtypeStruct((t, d), BF16), jax.ShapeDtypeStruct((1, d), F32)],
        compiler_params=_params(("arbitrary",)),
    )(h2, tgt, wf)


def _swiglu_fwd(gt, up, *, name):
    t, f = gt.shape
    tt, bn = _pick(t, (256,)), _pick(f, (512, 256, 128))

    def body(g_ref, u_ref, o_ref):
        g = g_ref[...]
        o_ref[...] = (g * _sigmoid(g) * u_ref[...]).astype(BF16)

    blk = pl.BlockSpec((tt, bn), lambda i, j: (i, j))
    return pl.pallas_call(
        body, name=name, grid=(t // tt, f // bn), in_specs=[blk, blk], out_specs=blk,
        out_shape=jax.ShapeDtypeStruct((t, f), BF16),
        compiler_params=_params(("parallel", "parallel")),
    )(gt, up)


def _swiglu_bwd(gt, up, dact, *, name):
    t, f = gt.shape
    tt, bn = _pick(t, (256,)), _pick(f, (512, 256, 128))

    def body(g_ref, u_ref, d_ref, dg_ref, du_ref):
        g, da = g_ref[...], d_ref[...]
        sg = _sigmoid(g)
        dg_ref[...] = (da * u_ref[...] * sg * (1.0 + g * (1.0 - sg))).astype(BF16)
        du_ref[...] = (da * g * sg).astype(BF16)

    blk = pl.BlockSpec((tt, bn), lambda i, j: (i, j))
    return pl.pallas_call(
        body, name=name, grid=(t // tt, f // bn), in_specs=[blk, blk, blk], out_specs=[blk, blk],
        out_shape=[jax.ShapeDtypeStruct((t, f), BF16)] * 2,
        compiler_params=_params(("parallel", "parallel")),
    )(gt, up, dact)


def _conv_cols(xs, cw_ref, cb, tt):
    cols = slice(cb * LANES, (cb + 1) * LANES)
    base = HALO - (CONV_W - 1)
    acc = xs[base:base + tt, cols] * cw_ref[0:1, cols]
    for j in range(1, CONV_W):
        acc = acc + xs[base + j:base + j + tt, cols] * cw_ref[j:j + 1, cols]
    return acc


def _fill_halo(xs, x_ref, xp_ref, tt):
    first = pl.program_id(0) == 0
    xs[0:HALO, :] = jnp.where(first, 0.0, xp_ref[...])
    xs[HALO:HALO + tt, :] = x_ref[...]


def _conv_specs(tt, ch):
    cur = pl.BlockSpec((tt, ch), lambda i: (i, 0))
    prev = pl.BlockSpec((HALO, ch), lambda i: (jnp.maximum(i * (tt // HALO) - 1, 0), 0))
    return cur, prev


def _conv_fwd(proj, conv_w, *, name):
    t = proj.shape[0]
    ch = conv_w.shape[1]
    nqk = HEADS * DK
    tt = _pick(t, (256,))
    cur, prev = _conv_specs(tt, ch)

    def body(x_ref, xp_ref, cw_ref, q_ref, k_ref, v_ref, xs):
        _fill_halo(xs, x_ref, xp_ref, tt)
        for cb in range(ch // LANES):
            c = _conv_cols(xs, cw_ref, cb, tt)
            s = c * _sigmoid(c)
            if cb < 2 * HEADS:
                s = s * lax.rsqrt(jnp.sum(s * s, axis=-1, keepdims=True) + EPS)
                if cb < HEADS:
                    q_ref[:, cb * LANES:(cb + 1) * LANES] = s * (DK ** -0.5)
                else:
                    k_ref[:, (cb - HEADS) * LANES:(cb - HEADS + 1) * LANES] = s
            else:
                v_ref[:, (cb - 2 * HEADS) * LANES:(cb - 2 * HEADS + 1) * LANES] = s

    return pl.pallas_call(
        body, name=name, grid=(t // tt,),
        in_specs=[cur, prev, pl.BlockSpec((CONV_W, ch), lambda i: (0, 0))],
        out_specs=[pl.BlockSpec((tt, nqk), lambda i: (i, 0)), pl.BlockSpec((tt, nqk), lambda i: (i, 0)),
                   pl.BlockSpec((tt, ch - 2 * nqk), lambda i: (i, 0))],
        out_shape=[jax.ShapeDtypeStruct((t, nqk), F32), jax.ShapeDtypeStruct((t, nqk), F32),
                   jax.ShapeDtypeStruct((t, ch - 2 * nqk), F32)],
        scratch_shapes=[pltpu.VMEM((HALO + tt, ch), F32)],
        compiler_params=_params(("arbitrary",)),
    )(proj, proj, conv_w)


def _conv_bwd_pre(proj, conv_w, dq, dk, dv, *, name):
    t = proj.shape[0]
    ch = conv_w.shape[1]
    nqk = HEADS * DK
    tt = _pick(t, (256,))
    cur, prev = _conv_specs(tt, ch)

    def body(x_ref, xp_ref, cw_ref, dq_ref, dk_ref, dv_ref, dc_ref, xs):
        _fill_halo(xs, x_ref, xp_ref, tt)
        for cb in range(ch // LANES):
            c = _conv_cols(xs, cw_ref, cb, tt)
            sg = _sigmoid(c)
            s = c * sg
            if cb < 2 * HEADS:
                if cb < HEADS:
                    d, scale = dq_ref[:, cb * LANES:(cb + 1) * LANES], DK ** -0.5
                else:
                    d, scale = dk_ref[:, (cb - HEADS) * LANES:(cb - HEADS + 1) * LANES], 1.0
                rinv = lax.rsqrt(jnp.sum(s * s, axis=-1, keepdims=True) + EPS)
                ds = scale * rinv * (d - s * (rinv * rinv) * jnp.sum(d * s, axis=-1, keepdims=True))
            else:
                ds = dv_ref[:, (cb - 2 * HEADS) * LANES:(cb - 2 * HEADS + 1) * LANES]
            dc_ref[:, cb * LANES:(cb + 1) * LANES] = ds * sg * (1.0 + c * (1.0 - sg))

    return pl.pallas_call(
        body, name=name, grid=(t // tt,),
        in_specs=[cur, prev, pl.BlockSpec((CONV_W, ch), lambda i: (0, 0)),
                  pl.BlockSpec((tt, nqk), lambda i: (i, 0)), pl.BlockSpec((tt, nqk), lambda i: (i, 0)),
                  pl.BlockSpec((tt, ch - 2 * nqk), lambda i: (i, 0))],
        out_specs=pl.BlockSpec((tt, ch), lambda i: (i, 0)),
        out_shape=jax.ShapeDtypeStruct((t, ch), F32),
        scratch_shapes=[pltpu.VMEM((HALO + tt, ch), F32)],
        compiler_params=_params(("arbitrary",)),
    )(proj, proj, conv_w, dq, dk, dv)


def _conv_bwd(proj, dc, conv_w, *, name):
    t = proj.shape[0]
    ch = conv_w.shape[1]
    tt = _pick(t, (256,))
    nt = t // tt
    cur, prev = _conv_specs(tt, ch)
    nxt = pl.BlockSpec((HALO, ch), lambda i: (jnp.minimum((i + 1) * (tt // HALO), t // HALO - 1), 0))

    def body(x_ref, xp_ref, d_ref, dn_ref, cw_ref, dx_ref, dw_ref, xs, ds):
        _fill_halo(xs, x_ref, xp_ref, tt)
        last = pl.program_id(0) == nt - 1
        ds[0:tt, :] = d_ref[...]
        ds[tt:tt + HALO, :] = jnp.where(last, 0.0, dn_ref[...])

        @pl.when(pl.program_id(0) == 0)
        def _():
            dw_ref[...] = jnp.zeros_like(dw_ref)

        base = HALO - (CONV_W - 1)
        for cb in range(ch // LANES):
            cols = slice(cb * LANES, (cb + 1) * LANES)
            d0 = ds[0:tt, cols]
            acc = None
            for j in range(CONV_W):
                sh = CONV_W - 1 - j
                term = ds[sh:sh + tt, cols] * cw_ref[j:j + 1, cols]
                acc = term if acc is None else acc + term
                dw_ref[j:j + 1, cols] += jnp.sum(d0 * xs[base + j:base + j + tt, cols], axis=0, keepdims=True)
            dx_ref[:, cols] = acc.astype(BF16)

    return pl.pallas_call(
        body, name=name, grid=(nt,),
        in_specs=[cur, prev, cur, nxt, pl.BlockSpec((CONV_W, ch), lambda i: (0, 0))],
        out_specs=[pl.BlockSpec((tt, ch), lambda i: (i, 0)), pl.BlockSpec((HALO, ch), lambda i: (0, 0))],
        out_shape=[jax.ShapeDtypeStruct((t, ch), BF16), jax.ShapeDtypeStruct((HALO, ch), F32)],
        scratch_shapes=[pltpu.VMEM((HALO + tt, ch), F32), pltpu.VMEM((tt + HALO, ch), F32)],
        compiler_params=_params(("arbitrary",)),
    )(proj, proj, dc, dc, conv_w)


def _bg_fwd(projs, alog_row, dtb_row, *, name):
    t = projs.shape[0]
    tt = _pick(t, (512, 256))

    def body(p_ref, al_ref, db_ref, o_ref):
        p = p_ref[...]
        lane = lax.broadcasted_iota(jnp.int32, p.shape, 1)
        z = p + db_ref[...]
        sp = jnp.maximum(z, 0.0) + jnp.log1p(jnp.exp(-jnp.abs(z)))
        g = -jnp.exp(al_ref[...]) * sp
        o_ref[...] = jnp.where(lane < HEADS, _sigmoid(p), jnp.where(lane < 2 * HEADS, g, 0.0))

    blk = pl.BlockSpec((tt, LANES), lambda i: (i, 0))
    row = pl.BlockSpec((1, LANES), lambda i: (0, 0))
    return pl.pallas_call(
        body, name=name, grid=(t // tt,), in_specs=[blk, row, row], out_specs=blk,
        out_shape=jax.ShapeDtypeStruct((t, LANES), F32), compiler_params=_params(("parallel",)),
    )(projs, alog_row, dtb_row)


def _bg_bwd(projs, dbg, alog_row, dtb_row, *, name):
    t = projs.shape[0]
    tt = _pick(t, (512, 256))

    def body(p_ref, d_ref, al_ref, db_ref, o_ref, dal_ref, ddb_ref):
        p, d = p_ref[...], d_ref[...]
        lane = lax.broadcasted_iota(jnp.int32, p.shape, 1)
        isg = (lane >= HEADS) & (lane < 2 * HEADS)
        be = _sigmoid(p)
        z = p + db_ref[...]
        sp = jnp.maximum(z, 0.0) + jnp.log1p(jnp.exp(-jnp.abs(z)))
        ea = jnp.exp(al_ref[...])
        d_aa = jnp.where(isg, d * (-ea) * _sigmoid(z), 0.0)
        o_ref[...] = jnp.where(lane < HEADS, d * be * (1.0 - be), d_aa).astype(BF16)

        @pl.when(pl.program_id(0) == 0)
        def _():
            dal_ref[...] = jnp.zeros_like(dal_ref)
            ddb_ref[...] = jnp.zeros_like(ddb_ref)

        dal_ref[...] += jnp.sum(jnp.where(isg, d * (-ea) * sp, 0.0), axis=0, keepdims=True)
        ddb_ref[...] += jnp.sum(d_aa, axis=0, keepdims=True)

    blk = pl.BlockSpec((tt, LANES), lambda i: (i, 0))
    row = pl.BlockSpec((1, LANES), lambda i: (0, 0))
    return pl.pallas_call(
        body, name=name, grid=(t // tt,), in_specs=[blk, blk, row, row], out_specs=[blk, row, row],
        out_shape=[jax.ShapeDtypeStruct((t, LANES), BF16), jax.ShapeDtypeStruct((1, LANES), F32),
                   jax.ShapeDtypeStruct((1, LANES), F32)],
        compiler_params=_params(("arbitrary",)),
    )(projs, dbg, alog_row, dtb_row)


def _hn_specs(n_chunks, rev=False):
    def nn(n):
        return n_chunks - 1 - n if rev else n
    tok = lambda w: pl.BlockSpec((CHUNK, w), lambda h, n: (nn(n), h))
    per = lambda a, b: pl.BlockSpec((None, None, a, b), lambda h, n: (h, nn(n), 0, 0))
    return tok, per


def _decay(gc_col, gc_row):
    r, c = _iota2(CHUNK)
    d = jnp.exp(jnp.minimum(gc_col - gc_row, 0.0))
    return jnp.where(r >= c, d, 0.0), jnp.where(r > c, d, 0.0)


def _gdn_prep(q, k, v, beta_r, g_r, *, name):
    t = q.shape[0]
    n_chunks = t // CHUNK
    tok, per = _hn_specs(n_chunks)

    def body(q_ref, k_ref, v_ref, b_ref, g_ref, gc_ref, ti_ref, u_ref, w_ref, p_ref):
        kk, g_row = k_ref[...], g_ref[...]
        r, c = _iota2(CHUNK)
        gc_col = jnp.sum(jnp.where(c <= r, jnp.broadcast_to(g_row, (CHUNK, CHUNK)), 0.0), axis=1, keepdims=True)
        gc_row = _col_to_row(gc_col)
        beta_col = _row_to_col(b_ref[...])
        d_incl, d_strict = _decay(gc_col, gc_row)
        kb = kk * beta_col
        a = _dot_nt(kb, kk) * d_strict
        eye = (r == c).astype(F32)
        x = eye - a
        pw = a
        for _ in range(5):
            pw = _dot_hi(pw, pw)
            x = _dot_hi(x, eye + pw)
        gc_ref[...] = gc_row
        ti_ref[...] = x
        u_ref[...] = _dot_hi(x, v_ref[...] * beta_col)
        w_ref[...] = _dot_hi(x, kb * jnp.exp(gc_col))
        p_ref[...] = _dot_nt(q_ref[...], kk) * d_incl

    return pl.pallas_call(
        body, name=name, grid=(HEADS, n_chunks),
        in_specs=[tok(DK), tok(DK), tok(DV), per(1, CHUNK), per(1, CHUNK)],
        out_specs=[per(1, CHUNK), per(CHUNK, CHUNK), tok(DV), tok(DK), per(CHUNK, CHUNK)],
        out_shape=[jax.ShapeDtypeStruct((HEADS, n_chunks, 1, CHUNK), F32),
                   jax.ShapeDtypeStruct((HEADS, n_chunks, CHUNK, CHUNK), F32),
                   jax.ShapeDtypeStruct((t, HEADS * DV), F32), jax.ShapeDtypeStruct((t, HEADS * DK), F32),
                   jax.ShapeDtypeStruct((HEADS, n_chunks, CHUNK, CHUNK), F32)],
        compiler_params=_params(("parallel", "parallel")),
    )(q, k, v, beta_r, g_r)


def _gdn_scan(q, k, u, w, p, gc_r, *, name):
    t = q.shape[0]
    n_chunks = t // CHUNK
    tok, per = _hn_specs(n_chunks)

    def body(q_ref, k_ref, u_ref, w_ref, p_ref, gc_ref, o_ref, s_ref, st):
        @pl.when(pl.program_id(1) == 0)
        def _():
            st[...] = jnp.zeros_like(st)

        s = st[...]
        s_ref[...] = s
        gc_row = gc_ref[...]
        gc_col = _row_to_col(gc_row)
        glast = gc_row[:, CHUNK - 1:CHUNK]
        vn = u_ref[...] - _dot(w_ref[...], s)
        o_ref[...] = _dot(q_ref[...] * jnp.exp(gc_col), s) + _dot(p_ref[...], vn)
        st[...] = s * jnp.exp(glast) + _dot_tn(k_ref[...] * jnp.exp(glast - gc_col), vn)

    return pl.pallas_call(
        body, name=name, grid=(HEADS, n_chunks),
        in_specs=[tok(DK), tok(DK), tok(DV), tok(DK), per(CHUNK, CHUNK), per(1, CHUNK)],
        out_specs=[tok(DV), per(DK, DV)],
        out_shape=[jax.ShapeDtypeStruct((t, HEADS * DV), F32),
                   jax.ShapeDtypeStruct((HEADS, n_chunks, DK, DV), F32)],
        scratch_shapes=[pltpu.VMEM((DK, DV), F32)],
        compiler_params=_params(("arbitrary", "arbitrary")),
    )(q, k, u, w, p, gc_r)


def _gdn_bwd(q, k, v, beta_r, gc_r, ti, u, w, s_all, do, *, name):
    t = q.shape[0]
    n_chunks = t // CHUNK
    tok, per = _hn_specs(n_chunks, rev=True)

    def body(q_ref, k_ref, v_ref, b_ref, gc_ref, ti_ref, u_ref, w_ref, s_ref, do_ref,
             dq_ref, dk_ref, dv_ref, db_ref, dg_ref, dst):
        @pl.when(pl.program_id(1) == 0)
        def _():
            dst[...] = jnp.zeros_like(dst)

        qq, kk, vv, uu, ww = q_ref[...], k_ref[...], v_ref[...], u_ref[...], w_ref[...]
        s, d_s, d_o, tinv = s_ref[...], dst[...], do_ref[...], ti_ref[...]
        r, c = _iota2(CHUNK)
        gc_row = gc_ref[...]
        gc_col = _row_to_col(gc_row)
        beta_col = _row_to_col(b_ref[...])
        eg = jnp.exp(gc_col)
        glast = gc_row[:, CHUNK - 1:CHUNK]
        eglast = jnp.exp(glast)
        e2 = jnp.exp(glast - gc_col)
        d_incl, d_strict = _decay(gc_col, gc_row)
        kb = kk * beta_col
        kkt = _dot_nt(kb, kk)
        a = kkt * d_strict
        qk = _dot_nt(qq, kk)
        pp = qk * d_incl
        vn = uu - _dot(ww, s)
        ke, qe = kk * e2, qq * eg

        dvn = _dot_tn(pp, d_o) + _dot(ke, d_s)
        dqk = _dot_nt(d_o, vn) * d_incl
        dqe = _dot_nt(d_o, s)
        dke = _dot_nt(vn, d_s)
        de2 = jnp.sum(dke * ke, axis=1, keepdims=True)
        dglast = jnp.sum(de2, axis=0, keepdims=True) + eglast * jnp.sum(
            jnp.sum(d_s * s, axis=1, keepdims=True), axis=0, keepdims=True)
        dst[...] = d_s * eglast + _dot_tn(qe, d_o) - _dot_tn(ww, dvn)

        dw = -_dot_nt(dvn, s)
        dvb = _dot_tn_hi(tinv, dvn)
        dkbe = _dot_tn_hi(tinv, dw)
        d_a = jnp.where(r > c, -(_dot_nt(dvb, uu) + _dot_nt(dkbe, ww)), 0.0)
        dm = d_a * d_strict
        dkb = _dot(dm, kk) + dkbe * eg
        e_mat = d_a * a + dqk * qk
        dgc = (jnp.sum(dqe * qe, axis=1, keepdims=True) - de2 + jnp.sum(dkbe * kb * eg, axis=1, keepdims=True)
               + jnp.sum(e_mat, axis=1, keepdims=True) - _row_to_col(jnp.sum(e_mat, axis=0, keepdims=True)))
        rows = lax.broadcasted_iota(jnp.int32, (CHUNK, 1), 0)
        dgc = dgc + jnp.where(rows == CHUNK - 1, dglast, 0.0)

        dq_ref[...] = dqe * eg + _dot(dqk, kk)
        dk_ref[...] = _dot_tn(dqk, qq) + dke * e2 + _dot_tn(dm, kb) + dkb * beta_col
        dv_ref[...] = dvb * beta_col
        dbeta = jnp.sum(dkb * kk, axis=1, keepdims=True) + jnp.sum(dvb * vv, axis=1, keepdims=True)
        db_ref[...] = _col_to_row(dbeta)
        dg_ref[...] = jnp.sum(jnp.where(r >= c, jnp.broadcast_to(dgc, (CHUNK, CHUNK)), 0.0), axis=0, keepdims=True)

    return pl.pallas_call(
        body, name=name, grid=(HEADS, n_chunks),
        in_specs=[tok(DK), tok(DK), tok(DV), per(1, CHUNK), per(1, CHUNK), per(CHUNK, CHUNK), tok(DV), tok(DK),
                  per(DK, DV), tok(DV)],
        out_specs=[tok(DK), tok(DK), tok(DV), per(1, CHUNK), per(1, CHUNK)],
        out_shape=[jax.ShapeDtypeStruct((t, HEADS * DK), F32), jax.ShapeDtypeStruct((t, HEADS * DK), F32),
                   jax.ShapeDtypeStruct((t, HEADS * DV), F32),
                   jax.ShapeDtypeStruct((HEADS, n_chunks, 1, CHUNK), F32),
                   jax.ShapeDtypeStruct((HEADS, n_chunks, 1, CHUNK), F32)],
        scratch_shapes=[pltpu.VMEM((DK, DV), F32)],
        compiler_params=_params(("arbitrary", "arbitrary")),
    )(q, k, v, beta_r, gc_r, ti, u, w, s_all, do)


def _rot(x, cos2, sin2):
    return x * cos2 + pltpu.roll(x, DK // 2, 1) * sin2


def _unrot(d, cos2, sin2):
    return d * cos2 + pltpu.roll(d * sin2, DK // 2, 1)


def _ret_consts(lg):
    r, c = _iota2(CHUNK)
    dm = jnp.where(r >= c, jnp.exp((r - c).astype(F32) * lg), 0.0)
    pos = lax.broadcasted_iota(jnp.int32, (CHUNK, 1), 0).astype(F32)
    return dm, jnp.exp((pos + 1.0) * lg), jnp.exp((CHUNK - 1.0 - pos) * lg), jnp.exp(CHUNK * lg)


def _ret_fwd(proj, cos2, sin2, lg_tab, col_q, col_k, col_v, *, name):
    t = proj.shape[0]
    n_chunks = t // CHUNK
    _, per = _hn_specs(n_chunks)
    bq, bk, bv = col_q // DK, col_k // DK, col_v // DV

    def body(q_ref, k_ref, v_ref, cos_ref, sin_ref, lg_ref, o_ref, s_ref, st):
        @pl.when(pl.program_id(1) == 0)
        def _():
            st[...] = jnp.zeros_like(st)

        s = st[...]
        s_ref[...] = s
        dm, xi, zeta, gam = _ret_consts(lg_ref[:, 0:1])
        qq = _rot(q_ref[...], cos_ref[...], sin_ref[...])
        kk = _rot(k_ref[...], cos_ref[...], sin_ref[...]) * (DK ** -0.5)
        vv = v_ref[...]
        o_ref[...] = _dot(_dot_nt(qq, kk) * dm, vv) + _dot(qq, s) * xi
        st[...] = s * gam + _dot_tn(kk * zeta, vv)

    return pl.pallas_call(
        body, name=name, grid=(HEADS, n_chunks),
        in_specs=[pl.BlockSpec((CHUNK, DK), lambda h, n: (n, bq + h)),
                  pl.BlockSpec((CHUNK, DK), lambda h, n: (n, bk + h)),
                  pl.BlockSpec((CHUNK, DV), lambda h, n: (n, bv + h)),
                  pl.BlockSpec((CHUNK, DK), lambda h, n: (n, 0)), pl.BlockSpec((CHUNK, DK), lambda h, n: (n, 0)),
                  pl.BlockSpec((None, 1, LANES), lambda h, n: (h, 0, 0))],
        out_specs=[pl.BlockSpec((CHUNK, DV), lambda h, n: (n, h)), per(DK, DV)],
        out_shape=[jax.ShapeDtypeStruct((t, HEADS * DV), F32),
                   jax.ShapeDtypeStruct((HEADS, n_chunks, DK, DV), F32)],
        scratch_shapes=[pltpu.VMEM((DK, DV), F32)],
        compiler_params=_params(("arbitrary", "arbitrary")),
    )(proj, proj, proj, cos2, sin2, lg_tab)


def _ret_bwd(proj, cos2, sin2, lg_tab, s_all, do, col_q, col_k, col_v, *, name):
    t = proj.shape[0]
    n_chunks = t // CHUNK
    _, per = _hn_specs(n_chunks, rev=True)
    bq, bk, bv = col_q // DK, col_k // DK, col_v // DV
    rv = lambda n: n_chunks - 1 - n

    def body(q_ref, k_ref, v_ref, cos_ref, sin_ref, lg_ref, s_ref, do_ref, dq_ref, dk_ref, dv_ref, dst):
        @pl.when(pl.program_id(1) == 0)
        def _():
            dst[...] = jnp.zeros_like(dst)

        s, d_s, d_o = s_ref[...], dst[...], do_ref[...]
        dm, xi, zeta, gam = _ret_consts(lg_ref[:, 0:1])
        cos2v, sin2v = cos_ref[...], sin_ref[...]
        qq = _rot(q_ref[...], cos2v, sin2v)
        kk = _rot(k_ref[...], cos2v, sin2v) * (DK ** -0.5)
        vv = v_ref[...]
        sc = _dot_nt(qq, kk) * dm
        dxo = d_o * xi
        dsc = _dot_nt(d_o, vv) * dm
        dv_ref[...] = (_dot_tn(sc, d_o) + _dot(kk * zeta, d_s)).astype(BF16)
        dq = _dot(dsc, kk) + _dot_nt(dxo, s)
        dk = _dot_tn(dsc, qq) + _dot_nt(vv, d_s) * zeta
        dst[...] = d_s * gam + _dot_tn(qq, dxo)
        dq_ref[...] = _unrot(dq, cos2v, sin2v).astype(BF16)
        dk_ref[...] = (_unrot(dk, cos2v, sin2v) * (DK ** -0.5)).astype(BF16)

    return pl.pallas_call(
        body, name=name, grid=(HEADS, n_chunks),
        in_specs=[pl.BlockSpec((CHUNK, DK), lambda h, n: (rv(n), bq + h)),
                  pl.BlockSpec((CHUNK, DK), lambda h, n: (rv(n), bk + h)),
                  pl.BlockSpec((CHUNK, DV), lambda h, n: (rv(n), bv + h)),
                  pl.BlockSpec((CHUNK, DK), lambda h, n: (rv(n), 0)),
                  pl.BlockSpec((CHUNK, DK), lambda h, n: (rv(n), 0)),
                  pl.BlockSpec((None, 1, LANES), lambda h, n: (h, 0, 0)),
                  per(DK, DV), pl.BlockSpec((CHUNK, DV), lambda h, n: (rv(n), h))],
        out_specs=[pl.BlockSpec((CHUNK, DK), lambda h, n: (rv(n), h)),
                   pl.BlockSpec((CHUNK, DK), lambda h, n: (rv(n), h)),
                   pl.BlockSpec((CHUNK, DV), lambda h, n: (rv(n), h))],
        out_shape=[jax.ShapeDtypeStruct((t, HEADS * DK), BF16), jax.ShapeDtypeStruct((t, HEADS * DK), BF16),
                   jax.ShapeDtypeStruct((t, HEADS * DV), BF16)],
        scratch_shapes=[pltpu.VMEM((DK, DV), F32)],
        compiler_params=_params(("arbitrary", "arbitrary")),
    )(proj, proj, proj, cos2, sin2, lg_tab, s_all, do)


def _merge_parts(oa, ob, z, rg, ga, gb, wa, wb, h):
    ra = lax.rsqrt(jnp.mean(oa * oa, axis=-1, keepdims=True) + EPS)
    xa = oa * ra
    mu = jnp.mean(ob, axis=-1, keepdims=True)
    cen = ob - mu
    rb = lax.rsqrt(jnp.mean(cen * cen, axis=-1, keepdims=True) + EPS)
    xb = cen * rb
    sz, sr = _sigmoid(z), _sigmoid(rg)
    return ra, xa, rb, xb, sz, sr, _sigmoid(ga), _sigmoid(gb)


def _merge_specs(tt, d, cz, crg, cga, cgb):
    blk = pl.BlockSpec((tt, d), lambda i: (i, 0))
    pcol = lambda c: pl.BlockSpec((tt, d), lambda i: (i, c // d))
    row = pl.BlockSpec((1, d), lambda i: (0, 0))
    return blk, [blk, blk, pcol(cz), pcol(crg), pcol(cga), pcol(cgb), row, row], row


def _merge_fwd(oa, ob, proj, wa, wb, cz, crg, cga, cgb, *, name):
    t, d = oa.shape
    tt = _pick(t, (256,))
    blk, in_specs, _ = _merge_specs(tt, d, cz, crg, cga, cgb)

    def body(oa_ref, ob_ref, z_ref, rg_ref, ga_ref, gb_ref, wa_ref, wb_ref, o_ref):
        for h in range(HEADS):
            cols = slice(h * DV, (h + 1) * DV)
            z, rg = z_ref[:, cols], rg_ref[:, cols]
            _, xa, _, xb, sz, sr, sga, sgb = _merge_parts(
                oa_ref[:, cols], ob_ref[:, cols], z, rg, ga_ref[:, cols], gb_ref[:, cols], None, None, h)
            o_a = xa * wa_ref[:, cols] * (z * sz)
            o_b = xb * wb_ref[:, cols] * (rg * sr)
            o_ref[:, cols] = (sga * o_a + sgb * o_b).astype(BF16)

    return pl.pallas_call(
        body, name=name, grid=(t // tt,), in_specs=in_specs, out_specs=blk,
        out_shape=jax.ShapeDtypeStruct((t, d), BF16), compiler_params=_params(("parallel",)),
    )(oa, ob, proj, proj, proj, proj, wa, wb)


def _merge_bwd(dmix, oa, ob, proj, wa, wb, cz, crg, cga, cgb, *, name):
    t, d = oa.shape
    tt = _pick(t, (256,))
    blk, in_specs, row = _merge_specs(tt, d, cz, crg, cga, cgb)

    def body(dm_ref, oa_ref, ob_ref, z_ref, rg_ref, ga_ref, gb_ref, wa_ref, wb_ref,
             doa_ref, dob_ref, dz_ref, drg_ref, dga_ref, dgb_ref, dwa_ref, dwb_ref):
        @pl.when(pl.program_id(0) == 0)
        def _():
            dwa_ref[...] = jnp.zeros_like(dwa_ref)
            dwb_ref[...] = jnp.zeros_like(dwb_ref)

        for h in range(HEADS):
            cols = slice(h * DV, (h + 1) * DV)
            z, rg, wa_h, wb_h, dmx = z_ref[:, cols], rg_ref[:, cols], wa_ref[:, cols], wb_ref[:, cols], dm_ref[:, cols]
            ra, xa, rb, xb, sz, sr, sga, sgb = _merge_parts(
                oa_ref[:, cols], ob_ref[:, cols], z, rg, ga_ref[:, cols], gb_ref[:, cols], None, None, h)
            na, nb = xa * wa_h, xb * wb_h
            sil_z, sil_r = z * sz, rg * sr
            o_a, o_b = na * sil_z, nb * sil_r
            dga_ref[:, cols] = (dmx * o_a * sga * (1.0 - sga)).astype(BF16)
            dgb_ref[:, cols] = (dmx * o_b * sgb * (1.0 - sgb)).astype(BF16)
            d_oa, d_ob = dmx * sga, dmx * sgb
            dz_ref[:, cols] = (d_oa * na * sz * (1.0 + z * (1.0 - sz))).astype(BF16)
            drg_ref[:, cols] = (d_ob * nb * sr * (1.0 + rg * (1.0 - sr))).astype(BF16)
            dna, dnb = d_oa * sil_z, d_ob * sil_r
            dwa_ref[:, cols] += jnp.sum(dna * xa, axis=0, keepdims=True)
            dwb_ref[:, cols] += jnp.sum(dnb * xb, axis=0, keepdims=True)
            gwa, gwb = dna * wa_h, dnb * wb_h
            doa_ref[:, cols] = ra * (gwa - xa * jnp.mean(gwa * xa, axis=-1, keepdims=True))
            dob_ref[:, cols] = rb * (gwb - jnp.mean(gwb, axis=-1, keepdims=True)
                                     - xb * jnp.mean(gwb * xb, axis=-1, keepdims=True))

    return pl.pallas_call(
        body, name=name, grid=(t // tt,), in_specs=[blk] + in_specs,
        out_specs=[blk, blk, blk, blk, blk, blk, row, row],
        out_shape=[jax.ShapeDtypeStruct((t, d), F32)] * 2 + [jax.ShapeDtypeStruct((t, d), BF16)] * 4
        + [jax.ShapeDtypeStruct((1, d), F32)] * 2,
        compiler_params=_params(("arbitrary",)),
    )(dmix, oa, ob, proj, proj, proj, proj, wa, wb)


def _row_block(rows, cols, itemsize=4, target=1 << 20):
    for rb in (512, 256, 128, 64, 32, 16, 8):
        if rows % rb == 0 and rb * cols * itemsize <= target:
            return rb
    return rows


def _adamw(w, g, m, v, *, name):
    rows, cols = w.shape
    rb = _row_block(rows, cols)

    def body(w_ref, g_ref, m_ref, v_ref, d_ref, nm_ref, nv_ref):
        gg = g_ref[...]
        mm = ADAM_B1 * m_ref[...] + (1.0 - ADAM_B1) * gg
        vv = ADAM_B2 * v_ref[...] + (1.0 - ADAM_B2) * (gg * gg)
        m_hat = mm / (1.0 - ADAM_B1 ** ADAM_STEP)
        v_hat = vv / (1.0 - ADAM_B2 ** ADAM_STEP)
        d_ref[...] = -ADAM_LR * (m_hat / (jnp.sqrt(v_hat) + ADAM_EPS) + ADAM_WD * w_ref[...])
        nm_ref[...] = mm
        nv_ref[...] = vv

    blk = pl.BlockSpec((rb, cols), lambda i: (i, 0))
    return pl.pallas_call(
        body, name=name, grid=(rows // rb,), in_specs=[blk] * 4, out_specs=[blk] * 3,
        out_shape=[jax.ShapeDtypeStruct((rows, cols), F32)] * 3, compiler_params=_params(("parallel",)),
    )(w, g, m, v)


def _pair_add(g, rsib, c_idx, *, name):
    _, _, hr, cols = g.shape
    rb = _row_block(hr, cols, 2)

    def body(c_ref, g_ref, r_ref, o_ref):
        o_ref[...] = (g_ref[...].astype(F32) + r_ref[...].astype(F32)).astype(BF16)

    return pl.pallas_call(
        body, name=name,
        grid_spec=pltpu.PrefetchScalarGridSpec(
            num_scalar_prefetch=1, grid=(N_CHIPS, hr // rb),
            in_specs=[pl.BlockSpec((None, None, rb, cols), lambda j, i, c: (j, c[0], i, 0)),
                      pl.BlockSpec((None, rb, cols), lambda j, i, c: (j, i, 0))],
            out_specs=pl.BlockSpec((None, rb, cols), lambda j, i, c: (j, i, 0))),
        out_shape=jax.ShapeDtypeStruct((N_CHIPS, hr, cols), BF16),
        compiler_params=_params(("parallel", "parallel")),
    )(c_idx, g, rsib)


def _chip_sum(qb, *, name):
    _, hr, cols = qb.shape
    rb = _row_block(hr, cols, 2, 1 << 19)

    def body(q_ref, o_ref):
        acc = q_ref[0].astype(F32)
        for i in range(1, N_CHIPS):
            acc = acc + q_ref[i].astype(F32)
        o_ref[...] = acc

    return pl.pallas_call(
        body, name=name, grid=(hr // rb,),
        in_specs=[pl.BlockSpec((N_CHIPS, rb, cols), lambda i: (0, i, 0))],
        out_specs=pl.BlockSpec((rb, cols), lambda i: (i, 0)),
        out_shape=jax.ShapeDtypeStruct((hr, cols), F32), compiler_params=_params(("parallel",)),
    )(qb)


def _place():
    x, y, c = lax.axis_index("x"), lax.axis_index("y"), lax.axis_index("c")
    return x, y, c, [(1 - x, y), (x, 1 - y), (1 - x, 1 - y)]


ANY = pl.BlockSpec(memory_space=pl.ANY)


def _gather_big(shards, *, name):
    nw = len(shards)

    def body(*refs):
        ins, outs = refs[:nw], refs[nw:2 * nw]
        lsem, ssem, rsem = refs[2 * nw:]
        x, y, c, chips = _place()
        s_me = 2 * x + y
        locals_ = [pltpu.make_async_copy(ins[w], outs[w].at[s_me], lsem.at[w]) for w in range(nw)]
        for cp in locals_:
            cp.start()

        def rows(w, half):
            hr = shards[w].shape[0] // 2
            return pl.ds(pl.multiple_of(half * hr, 16), hr)

        def rcopy(w, k, src, slot, half, to):
            return pltpu.make_async_remote_copy(
                src_ref=src, dst_ref=outs[w].at[slot, rows(w, half)],
                send_sem=ssem.at[w * 6 + k], recv_sem=rsem.at[w * 6 + k], device_id=to, device_id_type=MESH)

        sent = []
        for w in range(nw):
            for j, chip in enumerate(chips):
                cp = rcopy(w, j, ins[w].at[rows(w, c)], s_me, c, (chip[0], chip[1], c))
                cp.start()
                sent.append(cp)
        for w in range(nw):
            for j, chip in enumerate(chips):
                s_j = 2 * chip[0] + chip[1]
                rcopy(w, j, ins[w].at[rows(w, c)], s_j, c, (x, y, c)).wait_recv()
                cp = rcopy(w, 3 + j, outs[w].at[s_j, rows(w, c)], s_j, c, (x, y, 1 - c))
                cp.start()
                sent.append(cp)
        for w in range(nw):
            for j, chip in enumerate(chips):
                s_j = 2 * chip[0] + chip[1]
                rcopy(w, 3 + j, ins[w].at[rows(w, c)], s_j, 1 - c, (x, y, c)).wait_recv()
        for cp in sent:
            cp.wait_send()
        for cp in locals_:
            cp.wait()

    return pl.pallas_call(
        body, name=name, in_specs=[ANY] * nw, out_specs=[ANY] * nw,
        out_shape=[jax.ShapeDtypeStruct((N_CHIPS,) + s.shape, s.dtype) for s in shards],
        scratch_shapes=[pltpu.SemaphoreType.DMA((nw,)), pltpu.SemaphoreType.DMA((nw * 6,)),
                        pltpu.SemaphoreType.DMA((nw * 6,))],
        compiler_params=pltpu.CompilerParams(has_side_effects=True),
    )(*shards)


def _pair_exchange(gs, *, name):
    nw = len(gs)

    def body(*refs):
        ins, outs = refs[:nw], refs[nw:2 * nw]
        ssem, rsem = refs[2 * nw:]
        x, y, c, _ = _place()
        cps = []
        for w in range(nw):
            cp = pltpu.make_async_remote_copy(
                src_ref=ins[w].at[:, 1 - c], dst_ref=outs[w], send_sem=ssem.at[w], recv_sem=rsem.at[w],
                device_id=(x, y, 1 - c), device_id_type=MESH)
            cp.start()
            cps.append(cp)
        for cp in cps:
            cp.wait()

    return pl.pallas_call(
        body, name=name, in_specs=[ANY] * nw, out_specs=[ANY] * nw,
        out_shape=[jax.ShapeDtypeStruct((g.shape[0],) + g.shape[2:], g.dtype) for g in gs],
        scratch_shapes=[pltpu.SemaphoreType.DMA((nw,)), pltpu.SemaphoreType.DMA((nw,))],
        compiler_params=pltpu.CompilerParams(has_side_effects=True),
    )(*gs)


def _chip_scatter(ps, *, name):
    nw = len(ps)

    def body(*refs):
        ins, outs = refs[:nw], refs[nw:2 * nw]
        lsem, ssem, rsem = refs[2 * nw:]
        x, y, c, chips = _place()
        s_me = 2 * x + y
        locals_ = [pltpu.make_async_copy(ins[w].at[s_me], outs[w].at[s_me], lsem.at[w]) for w in range(nw)]
        for cp in locals_:
            cp.start()
        cps = []
        for w in range(nw):
            for j, chip in enumerate(chips):
                s_j = 2 * chip[0] + chip[1]
                cp = pltpu.make_async_remote_copy(
                    src_ref=ins[w].at[s_j], dst_ref=outs[w].at[s_me],
                    send_sem=ssem.at[w * 3 + j], recv_sem=rsem.at[w * 3 + j],
                    device_id=(chip[0], chip[1], c), device_id_type=MESH)
                cp.start()
                cps.append(cp)
        for cp in cps:
            cp.wait()
        for cp in locals_:
            cp.wait()

    return pl.pallas_call(
        body, name=name, in_specs=[ANY] * nw, out_specs=[ANY] * nw,
        out_shape=[jax.ShapeDtypeStruct(p.shape, p.dtype) for p in ps],
        scratch_shapes=[pltpu.SemaphoreType.DMA((nw,)), pltpu.SemaphoreType.DMA((nw * 3,)),
                        pltpu.SemaphoreType.DMA((nw * 3,))],
        compiler_params=pltpu.CompilerParams(has_side_effects=True),
    )(*ps)


def _pair_share(hs, *, name):
    nw = len(hs)

    def body(*refs):
        ins, outs = refs[:nw], refs[nw:2 * nw]
        lsem, ssem, rsem = refs[2 * nw:]
        x, y, c, _ = _place()
        cps = []
        for w in range(nw):
            lc = pltpu.make_async_copy(ins[w], outs[w].at[c], lsem.at[w])
            lc.start()
            cp = pltpu.make_async_remote_copy(
                src_ref=ins[w], dst_ref=outs[w].at[c], send_sem=ssem.at[w], recv_sem=rsem.at[w],
                device_id=(x, y, 1 - c), device_id_type=MESH)
            cp.start()
            cps += [lc, cp]
        for cp in cps:
            cp.wait()

    return pl.pallas_call(
        body, name=name, in_specs=[ANY] * nw, out_specs=[ANY] * nw,
        out_shape=[jax.ShapeDtypeStruct((2,) + h.shape, h.dtype) for h in hs],
        scratch_shapes=[pltpu.SemaphoreType.DMA((nw,)), pltpu.SemaphoreType.DMA((nw,)),
                        pltpu.SemaphoreType.DMA((nw,))],
        compiler_params=pltpu.CompilerParams(has_side_effects=True),
    )(*hs)


def _gather_small(a, *, name):
    def body(a_ref, o_ref, ssem, rsem):
        x, y, c, chips = _place()
        s_me = 2 * x + y
        o_ref[s_me] = a_ref[...]
        cps = []
        for j, chip in enumerate(chips):
            cp = pltpu.make_async_remote_copy(
                src_ref=a_ref, dst_ref=o_ref.at[s_me], send_sem=ssem.at[j], recv_sem=rsem.at[j],
                device_id=(chip[0], chip[1], c), device_id_type=MESH)
            cp.start()
            cps.append(cp)
        for cp in cps:
            cp.wait()

    vm = pl.BlockSpec(memory_space=pltpu.VMEM)
    return pl.pallas_call(
        body, name=name, in_specs=[vm], out_specs=vm,
        out_shape=jax.ShapeDtypeStruct((N_CHIPS,) + a.shape, a.dtype),
        scratch_shapes=[pltpu.SemaphoreType.DMA((3,)), pltpu.SemaphoreType.DMA((3,))],
    )(a)


def _allreduce_small(p, *, name):
    def body(p_ref, o_ref, buf, ssem, rsem):
        x, y, c, _ = _place()
        me = 4 * x + 2 * y + c
        buf[me] = p_ref[...]
        cps = []
        for k in range(1, N_DEV):
            fx, fy, fc = (k >> 2) & 1, (k >> 1) & 1, k & 1
            peer = (x + fx - 2 * x * fx, y + fy - 2 * y * fy, c + fc - 2 * c * fc)
            cp = pltpu.make_async_remote_copy(
                src_ref=buf.at[me], dst_ref=buf.at[me], send_sem=ssem.at[k - 1], recv_sem=rsem.at[k - 1],
                device_id=peer, device_id_type=MESH)
            cp.start()
            cps.append(cp)
        for cp in cps:
            cp.wait()
        acc = buf[0]
        for i in range(1, N_DEV):
            acc = acc + buf[i]
        o_ref[...] = acc

    vm = pl.BlockSpec(memory_space=pltpu.VMEM)
    return pl.pallas_call(
        body, name=name, in_specs=[vm], out_specs=vm,
        out_shape=jax.ShapeDtypeStruct(p.shape, p.dtype),
        scratch_shapes=[pltpu.VMEM((N_DEV,) + p.shape, p.dtype), pltpu.SemaphoreType.DMA((N_DEV - 1,)),
                        pltpu.SemaphoreType.DMA((N_DEV - 1,))],
    )(p)


def _rows_to_tokens(r):
    h, n = r.shape[0], r.shape[1]
    return r.reshape(h, n * CHUNK).T


def _tokens_to_rows(a):
    t, h = a.shape
    return a.T.reshape(h, t // CHUNK, 1, CHUNK)


def kernel(x, norm1_w, w_in, conv_w, a_log, dt_bias, gdn_norm_w, ret_norm_w, w_out, norm2_w, w_gate, w_up, w_down, norm_f_w, loss_target, m_norm1_w, m_w_in, m_conv_w, m_a_log, m_dt_bias, m_gdn_norm_w, m_ret_norm_w, m_w_out, m_norm2_w, m_w_gate, m_w_up, m_w_down, m_norm_f_w, v_norm1_w, v_w_in, v_conv_w, v_a_log, v_dt_bias, v_gdn_norm_w, v_ret_norm_w, v_w_out, v_norm2_w, v_w_gate, v_w_up, v_w_down, v_norm_f_w):
    t, d = x.shape[1], x.shape[2]
    f = w_gate.shape[2] * N_CHIPS
    nqk, nv = HEADS * DK, HEADS * DV
    ncs = w_in.shape[2]
    c_idx = lax.axis_index("c")
    s_idx = 2 * lax.axis_index("x") + lax.axis_index("y")
    xs = x[0]
    tgt = loss_target[0]

    o_az = 2 * nqk + nv
    o_small = o_az + nv
    n_small = 2 * HEADS
    o_rq = o_small
    o_rk, o_rv = o_rq + nqk, o_rq + 2 * nqk
    o_rg = o_rv + nv
    o_ga = o_rg + nv
    o_gb = o_ga + d
    n_main = o_gb + d

    wg_in, wg_out, wg_gate, wg_up, wg_down = _gather_big(
        [w_in[0].astype(BF16), w_out[0].astype(BF16), w_gate[0].astype(BF16), w_up[0].astype(BF16),
         w_down[0].astype(BF16)], name="gather_weights")
    conv_full = _gather_small(conv_w[0], name="gather_conv_w")
    conv_full = jnp.concatenate([conv_full[i] for i in range(N_CHIPS)], axis=1)
    cuts = []
    lo_small = hi_small = None
    for s in range(N_CHIPS):
        g0, g1 = s * ncs, (s + 1) * ncs
        if g1 <= o_small or g0 >= o_small + n_small:
            cuts.append((s, 0, ncs))
        else:
            lo_small, hi_small = o_small - g0, o_small + n_small - g0
            small_shard = s
            cuts += [(s, 0, lo_small), (s, hi_small, ncs)]
    w_main = jnp.concatenate([wg_in[s][:, a:b] for s, a, b in cuts], axis=1)
    w_small = jnp.pad(wg_in[small_shard][:, lo_small:hi_small], ((0, 0), (0, LANES - n_small)))
    w_o = wg_out.reshape(d, d)
    w_g = jnp.concatenate([wg_gate[i] for i in range(N_CHIPS)], axis=1)
    w_u = jnp.concatenate([wg_up[i] for i in range(N_CHIPS)], axis=1)
    w_d = wg_down.reshape(f, d)

    pad16 = lambda a: jnp.pad(a, ((0, 0), (HEADS, LANES - 2 * HEADS)))
    alog_row, dtb_row = pad16(a_log), pad16(dt_bias)
    wa_row = jnp.tile(gdn_norm_w, (1, HEADS))
    inv = ROPE_BASE ** (-jnp.arange(0, DK, 2, dtype=F32) / DK)
    ang = jnp.arange(t, dtype=F32)[:, None] * inv[None, :]
    cos2 = jnp.concatenate([jnp.cos(ang), jnp.cos(ang)], axis=1)
    sin2 = jnp.concatenate([-jnp.sin(ang), jnp.sin(ang)], axis=1)
    lg = jnp.log1p(-jnp.exp2(-5.0 - jnp.arange(HEADS, dtype=F32)))
    lg_tab = jnp.broadcast_to(lg[:, None, None], (HEADS, 1, LANES))

    u1 = _rms_fwd(xs, norm1_w, name="rms1_fwd")
    proj = _mm(u1, w_main, name="mm_proj")
    projs = _mm(u1, w_small, name="mm_proj_small")
    q_a, k_a, v_a = _conv_fwd(proj, conv_full, name="conv_fwd")
    bg = _bg_fwd(projs, alog_row, dtb_row, name="bg_fwd")
    beta_r = _tokens_to_rows(bg[:, :HEADS])
    g_r = _tokens_to_rows(bg[:, HEADS:2 * HEADS])
    gc_r, tinv, u_a, w_a, p_a = _gdn_prep(q_a, k_a, v_a, beta_r, g_r, name="gdn_prep")
    o_a, s_a = _gdn_scan(q_a, k_a, u_a, w_a, p_a, gc_r, name="gdn_scan")
    o_b, s_b = _ret_fwd(proj, cos2, sin2, lg_tab, o_rq, o_rk, o_rv, name="ret_fwd")
    mixed = _merge_fwd(o_a, o_b, proj, wa_row, ret_norm_w, o_az, o_rg, o_ga, o_gb, name="merge_fwd")
    h1 = _mm(mixed, w_o, res=xs, name="mm_out")
    hn = _rms_fwd(h1, norm2_w, name="rms2_fwd")
    gt = _mm(hn, w_g, name="mm_gate")
    up = _mm(hn, w_u, name="mm_up")
    act = _swiglu_fwd(gt, up, name="swiglu_fwd")
    h2 = _mm(act, w_d, res=h1, name="mm_down")
    loss_row, dh2, dh2b, d_nf = _loss_head(h2, tgt, norm_f_w.reshape(1, d), name="loss_head")

    dact = _mm(dh2b, w_d, tb=True, name="mm_dact")
    g_down = _mm(act, dh2b, ta=True, out_dtype=BF16, name="mm_dw_down")
    dgt, dup = _swiglu_bwd(gt, up, dact, name="swiglu_bwd")
    dhn = _mm(dgt, w_g, tb=True, name="mm_dhn_gate")
    dhn = _mm(dup, w_u, tb=True, res=dhn, name="mm_dhn_up")
    g_gate = _mm(hn, dgt, ta=True, out_dtype=BF16, name="mm_dw_gate")
    g_up = _mm(hn, dup, ta=True, out_dtype=BF16, name="mm_dw_up")
    dh1, dh1b, d_n2 = _rms_bwd(dhn, h1, norm2_w, dh2, name="rms2_bwd")
    dmix = _mm(dh1b, w_o, tb=True, name="mm_dmix")
    g_out = _mm(mixed, dh1b, ta=True, out_dtype=BF16, name="mm_dw_out")
    do_a, do_b, d_az, d_rg, d_ga, d_gb, d_wa, d_wb = _merge_bwd(
        dmix, o_a, o_b, proj, wa_row, ret_norm_w, o_az, o_rg, o_ga, o_gb, name="merge_bwd")
    d_rq, d_rk, d_rv = _ret_bwd(proj, cos2, sin2, lg_tab, s_b, do_b, o_rq, o_rk, o_rv, name="ret_bwd")
    dq_a, dk_a, dv_a, dbeta_r, dg_r = _gdn_bwd(q_a, k_a, v_a, beta_r, gc_r, tinv, u_a, w_a, s_a, do_a, name="gdn_bwd")
    dc = _conv_bwd_pre(proj, conv_full, dq_a, dk_a, dv_a, name="conv_bwd_pre")
    d_aqkv, d_cw = _conv_bwd(proj, dc, conv_full, name="conv_bwd")
    dbg = jnp.pad(jnp.concatenate([_rows_to_tokens(dbeta_r), _rows_to_tokens(dg_r)], axis=1),
                  ((0, 0), (0, LANES - 2 * HEADS)))
    dprojs, d_alog, d_dtb = _bg_bwd(projs, dbg, alog_row, dtb_row, name="bg_bwd")
    dproj = jnp.concatenate([d_aqkv, d_az, d_rq, d_rk, d_rv, d_rg, d_ga, d_gb], axis=1)
    du = _mm(dproj, w_main, tb=True, name="mm_du")
    du = _mm(dprojs, w_small, tb=True, res=du, name="mm_du_small")
    g_main = _mm(u1, dproj, ta=True, out_dtype=BF16, name="mm_dw_in")
    g_small = _mm(u1, dprojs, ta=True, out_dtype=BF16, name="mm_dw_in_small")
    dx, _, d_n1 = _rms_bwd(du, xs, norm1_w, dh1, name="rms1_bwd")

    pieces, off = [], 0
    for s in range(N_CHIPS):
        if s == small_shard:
            a = g_main[:, off:off + lo_small]
            b = g_main[:, off + lo_small:off + lo_small + ncs - hi_small]
            pieces.append(jnp.concatenate([a, g_small[:, :n_small], b], axis=1))
            off += ncs - n_small
        else:
            pieces.append(g_main[:, off:off + ncs])
            off += ncs
    col_split = lambda g, wd: jnp.stack([g[:, i * wd:(i + 1) * wd] for i in range(N_CHIPS)])
    halves = lambda g: g.reshape(N_CHIPS, 2, g.shape[1] // 2, g.shape[2])
    gs = [halves(jnp.stack(pieces)), halves(g_out.reshape(N_CHIPS, d // N_CHIPS, d)),
          halves(col_split(g_gate, f // N_CHIPS)), halves(col_split(g_up, f // N_CHIPS)),
          halves(g_down.reshape(N_CHIPS, f // N_CHIPS, d))]
    names = ["w_in", "w_out", "w_gate", "w_up", "w_down"]
    rsib = _pair_exchange(gs, name="grad_pair_exchange")
    c_arr = jnp.reshape(c_idx, (1,)).astype(jnp.int32)
    ps = [_pair_add(g, r, c_arr, name=f"grad_pair_add_{nm}") for g, r, nm in zip(gs, rsib, names)]
    qs = _chip_scatter(ps, name="grad_chip_scatter")
    hs = [_chip_sum(q, name=f"grad_chip_sum_{nm}") for q, nm in zip(qs, names)]
    full = _pair_share(hs, name="grad_pair_share")
    big_w = [w_in[0], w_out[0], w_gate[0], w_up[0], w_down[0]]
    big_m = [m_w_in[0], m_w_out[0], m_w_gate[0], m_w_up[0], m_w_down[0]]
    big_v = [v_w_in[0], v_w_out[0], v_w_gate[0], v_w_up[0], v_w_down[0]]
    big = {}
    for nm, gf, w_, m_, v_ in zip(names, full, big_w, big_m, big_v):
        g = gf.reshape(w_.shape)
        dl, nm_, nv_ = _adamw(w_, g, m_, v_, name=f"adamw_{nm}")
        big[nm] = tuple(a[None] for a in (g, dl, nm_, nv_))

    d_wa_h = jnp.sum(d_wa.reshape(HEADS, DV), axis=0, keepdims=True)
    small = [d_n1, d_alog[:, HEADS:2 * HEADS], d_dtb[:, HEADS:2 * HEADS], d_wa_h, d_wb, d_n2, d_nf,
             d_cw[:CONV_W].reshape(1, -1)]
    sizes = [a.shape[1] for a in small]
    packed = jnp.concatenate(small, axis=1)
    n_pack = packed.shape[1]
    n_rows = -(-n_pack // LANES)
    n_rows = -(-n_rows // 8) * 8
    packed = jnp.pad(packed, ((0, 0), (0, n_rows * LANES - n_pack))).reshape(n_rows, LANES)
    red = _allreduce_small(packed, name="allreduce_small").reshape(1, -1)
    offs = np.cumsum([0] + sizes)
    g_n1, g_alog, g_dtb, g_wa, g_wb, g_n2, g_nf, g_cw = [red[:, offs[i]:offs[i + 1]] for i in range(len(sizes))]
    ncw = conv_w.shape[2]
    g_cw = lax.dynamic_slice(g_cw.reshape(CONV_W, -1), (0, s_idx * ncw), (CONV_W, ncw))

    def small_update(w_, g, m_, v_, nm):
        shape = w_.shape
        pad = (-w_.size) % LANES
        to2 = lambda a: jnp.pad(a.reshape(1, -1), ((0, 0), (0, pad)))
        outs = _adamw(to2(w_), to2(g), to2(m_), to2(v_), name=f"adamw_{nm}")
        return (g.reshape(shape),) + tuple(a[:, :w_.size].reshape(shape) for a in outs)

    res = {
        "norm1_w": small_update(norm1_w, g_n1, m_norm1_w, v_norm1_w, "norm1_w"),
        "w_in": big["w_in"],
        "conv_w": small_update(conv_w, g_cw, m_conv_w, v_conv_w, "conv_w"),
        "a_log": small_update(a_log, g_alog, m_a_log, v_a_log, "a_log"),
        "dt_bias": small_update(dt_bias, g_dtb, m_dt_bias, v_dt_bias, "dt_bias"),
        "gdn_norm_w": small_update(gdn_norm_w, g_wa, m_gdn_norm_w, v_gdn_norm_w, "gdn_norm_w"),
        "ret_norm_w": small_update(ret_norm_w, g_wb, m_ret_norm_w, v_ret_norm_w, "ret_norm_w"),
        "w_out": big["w_out"],
        "norm2_w": small_update(norm2_w, g_n2, m_norm2_w, v_norm2_w, "norm2_w"),
        "w_gate": big["w_gate"],
        "w_up": big["w_up"],
        "w_down": big["w_down"],
        "norm_f_w": small_update(norm_f_w, g_nf, m_norm_f_w, v_norm_f_w, "norm_f_w"),
    }
    order = ["norm1_w", "w_in", "conv_w", "a_log", "dt_bias", "gdn_norm_w", "ret_norm_w", "w_out", "norm2_w",
             "w_gate", "w_up", "w_down", "norm_f_w"]
    loss = lax.psum(loss_row[0, 0], ("x", "y", "c"))
    return (loss, dx[None], *[res[n][0] for n in order], *[res[n][1] for n in order],
            *[res[n][2] for n in order], *[res[n][3] for n in order])
```

```python
import functools

import jax
import jax.numpy as jnp
import numpy as np
from jax import lax
from jax.experimental import pallas as pl
from jax.experimental.pallas import tpu as pltpu

F32 = jnp.float32
BF16 = jnp.bfloat16
MESH = pl.DeviceIdType.MESH

HEADS = 8
DK = 128
DV = 256
CHUNK = 64
CONV_W = 4
EPS = 1e-6
ROPE_BASE = 10000.0
ADAM_LR, ADAM_B1, ADAM_B2, ADAM_EPS, ADAM_WD, ADAM_STEP = 0.001, 0.9, 0.999, 1e-08, 0.01, 10
N_CHIPS = 4
N_DEV = 8
LANES = 128
HALO = 8
VMEM_LIMIT = 56 * 1024 * 1024
HB = 8


def _pick(n, cands):
    for c in cands:
        if n % c == 0:
            return c
    raise ValueError(f"no tile for {n} in {cands}")


def _params(sem=None):
    return pltpu.CompilerParams(dimension_semantics=sem, vmem_limit_bytes=VMEM_LIMIT)


def _dot(a, b):
    return jnp.dot(a.astype(BF16), b.astype(BF16), preferred_element_type=F32)


def _dot_nt(a, b):
    return lax.dot_general(a.astype(BF16), b.astype(BF16), (((1,), (1,)), ((), ())), preferred_element_type=F32)


def _dot_tn(a, b):
    return lax.dot_general(a.astype(BF16), b.astype(BF16), (((0,), (0,)), ((), ())), preferred_element_type=F32)


def _sigmoid(x):
    return 1.0 / (1.0 + jnp.exp(-x))


def _iota2(n):
    return lax.broadcasted_iota(jnp.int32, (n, n), 0), lax.broadcasted_iota(jnp.int32, (n, n), 1)


def _row_to_col(row):
    n = row.shape[1]
    r, c = _iota2(n)
    return jnp.sum(jnp.where(r == c, jnp.broadcast_to(row, (n, n)), 0.0), axis=1, keepdims=True)


def _col_to_row(col):
    n = col.shape[0]
    r, c = _iota2(n)
    return jnp.sum(jnp.where(r == c, jnp.broadcast_to(col, (n, n)), 0.0), axis=0, keepdims=True)


def _mm(a, b, *, name, tm, tn, tk, ta=False, tb=False, out_dtype=F32, res=None):
    m, k = (a.shape[1], a.shape[0]) if ta else a.shape
    n = b.shape[0] if tb else b.shape[1]
    tm, tn, tk = min(tm, m), min(tn, n), min(tk, k)
    assert m % tm == 0 and n % tn == 0 and k % tk == 0, (name, m, n, k)
    nk = k // tk
    dn = (((0 if ta else 1,), (1 if tb else 0,)), ((), ()))

    def body(*refs):
        a_ref, b_ref = refs[0], refs[1]
        r_ref = refs[2] if res is not None else None
        o_ref = refs[3] if res is not None else refs[2]

        def finish(r):
            if res is not None:
                r = r + r_ref[...]
            o_ref[...] = r.astype(out_dtype)

        part = lax.dot_general(a_ref[...], b_ref[...], dn, preferred_element_type=F32)
        if nk == 1:
            finish(part)
            return
        acc = refs[-1]
        kk = pl.program_id(2)

        @pl.when(kk == 0)
        def _():
            acc[...] = part

        @pl.when((kk > 0) & (kk < nk - 1))
        def _():
            acc[...] += part

        @pl.when(kk == nk - 1)
        def _():
            finish(acc[...] + part)

    a_spec = pl.BlockSpec((tk, tm), lambda j, i, kk: (kk, i)) if ta else pl.BlockSpec((tm, tk), lambda j, i, kk: (i, kk))
    b_spec = pl.BlockSpec((tn, tk), lambda j, i, kk: (j, kk)) if tb else pl.BlockSpec((tk, tn), lambda j, i, kk: (kk, j))
    o_spec = pl.BlockSpec((tm, tn), lambda j, i, kk: (i, j))
    in_specs, args = [a_spec, b_spec], [a, b]
    if res is not None:
        in_specs.append(o_spec)
        args.append(res)
    return pl.pallas_call(
        body, name=name, grid=(n // tn, m // tm, nk),
        in_specs=in_specs, out_specs=o_spec,
        out_shape=jax.ShapeDtypeStruct((m, n), out_dtype),
        scratch_shapes=[pltpu.VMEM((tm, tn), F32)] if nk > 1 else [],
        compiler_params=_params(("parallel", "parallel", "arbitrary")),
    )(*args)


def _rms_fwd(x, w, *, name):
    t, d = x.shape
    tt = _pick(t, (512, 256))

    def body(x_ref, w_ref, o_ref):
        xv = x_ref[...]
        r = lax.rsqrt(jnp.mean(xv * xv, axis=-1, keepdims=True) + EPS)
        o_ref[...] = (xv * r * w_ref[...]).astype(BF16)

    return pl.pallas_call(
        body, name=name, grid=(t // tt,),
        in_specs=[pl.BlockSpec((tt, d), lambda i: (i, 0)), pl.BlockSpec((1, d), lambda i: (0, 0))],
        out_specs=pl.BlockSpec((tt, d), lambda i: (i, 0)),
        out_shape=jax.ShapeDtypeStruct((t, d), BF16),
        compiler_params=_params(("parallel",)),
    )(x, w)


def _rms_bwd(dn, x, w, dres, *, name):
    t, d = x.shape
    tt = _pick(t, (256,))

    def body(dn_ref, x_ref, w_ref, dres_ref, dx_ref, dxb_ref, dw_ref):
        xv, g = x_ref[...], dn_ref[...]
        r = lax.rsqrt(jnp.mean(xv * xv, axis=-1, keepdims=True) + EPS)
        xh = xv * r
        gw = g * w_ref[...]
        dx = dres_ref[...] + r * (gw - xh * jnp.mean(gw * xh, axis=-1, keepdims=True))
        dx_ref[...] = dx
        dxb_ref[...] = dx.astype(BF16)

        @pl.when(pl.program_id(0) == 0)
        def _():
            dw_ref[...] = jnp.zeros_like(dw_ref)

        dw_ref[...] += jnp.sum(g * xh, axis=0, keepdims=True)

    blk = pl.BlockSpec((tt, d), lambda i: (i, 0))
    row = pl.BlockSpec((1, d), lambda i: (0, 0))
    return pl.pallas_call(
        body, name=name, grid=(t // tt,),
        in_specs=[blk, blk, row, blk], out_specs=[blk, blk, row],
        out_shape=[jax.ShapeDtypeStruct((t, d), F32), jax.ShapeDtypeStruct((t, d), BF16),
                   jax.ShapeDtypeStruct((1, d), F32)],
        compiler_params=_params(("arbitrary",)),
    )(dn, x, w, dres)


def _loss_head(h2, tgt, wf, *, name):
    t, d = h2.shape
    tt = _pick(t, (256,))

    def body(x_ref, t_ref, w_ref, loss_ref, dx_ref, dxb_ref, dw_ref):
        xv = x_ref[...]
        r = lax.rsqrt(jnp.mean(xv * xv, axis=-1, keepdims=True) + EPS)
        xh = xv * r
        err = xh * w_ref[...] - t_ref[...]
        lpart = 0.5 * jnp.sum(jnp.mean(err * err, axis=-1, keepdims=True), axis=0, keepdims=True)
        dy = err * (1.0 / d)
        gw = dy * w_ref[...]
        dx = r * (gw - xh * jnp.mean(gw * xh, axis=-1, keepdims=True))
        dx_ref[...] = dx
        dxb_ref[...] = dx.astype(BF16)

        @pl.when(pl.program_id(0) == 0)
        def _():
            dw_ref[...] = jnp.zeros_like(dw_ref)
            loss_ref[...] = jnp.zeros_like(loss_ref)

        dw_ref[...] += jnp.sum(dy * xh, axis=0, keepdims=True)
        loss_ref[...] += jnp.broadcast_to(lpart, loss_ref.shape)

    blk = pl.BlockSpec((tt, d), lambda i: (i, 0))
    row = pl.BlockSpec((1, d), lambda i: (0, 0))
    lrow = pl.BlockSpec((1, LANES), lambda i: (0, 0))
    return pl.pallas_call(
        body, name=name, grid=(t // tt,),
        in_specs=[blk, blk, row], out_specs=[lrow, blk, blk, row],
        out_shape=[jax.ShapeDtypeStruct((1, LANES), F32), jax.ShapeDtypeStruct((t, d), F32),
                   jax.ShapeDtypeStruct((t, d), BF16), jax.ShapeDtypeStruct((1, d), F32)],
        compiler_params=_params(("arbitrary",)),
    )(h2, tgt, wf)


def _swiglu_fwd(gt, up, *, name):
    t, f = gt.shape
    tt, bn = _pick(t, (256,)), _pick(f, (512, 256, 128))

    def body(g_ref, u_ref, o_ref):
        g = g_ref[...]
        o_ref[...] = (g * _sigmoid(g) * u_ref[...]).astype(BF16)

    blk = pl.BlockSpec((tt, bn), lambda i, j: (i, j))
    return pl.pallas_call(
        body, name=name, grid=(t // tt, f // bn), in_specs=[blk, blk], out_specs=blk,
        out_shape=jax.ShapeDtypeStruct((t, f), BF16),
        compiler_params=_params(("parallel", "parallel")),
    )(gt, up)


def _swiglu_bwd(gt, up, dact, *, name):
    t, f = gt.shape
    tt, bn = _pick(t, (256,)), _pick(f, (512, 256, 128))

    def body(g_ref, u_ref, d_ref, dg_ref, du_ref):
        g, da = g_ref[...], d_ref[...]
        sg = _sigmoid(g)
        dg_ref[...] = (da * u_ref[...] * sg * (1.0 + g * (1.0 - sg))).astype(BF16)
        du_ref[...] = (da * g * sg).astype(BF16)

    blk = pl.BlockSpec((tt, bn), lambda i, j: (i, j))
    return pl.pallas_call(
        body, name=name, grid=(t // tt, f // bn), in_specs=[blk, blk, blk], out_specs=[blk, blk],
        out_shape=[jax.ShapeDtypeStruct((t, f), BF16)] * 2,
        compiler_params=_params(("parallel", "parallel")),
    )(gt, up, dact)


def _conv_cols(xs, cw_ref, cb, tt):
    cols = slice(cb * LANES, (cb + 1) * LANES)
    base = HALO - (CONV_W - 1)
    acc = xs[base:base + tt, cols] * cw_ref[0:1, cols]
    for j in range(1, CONV_W):
        acc = acc + xs[base + j:base + j + tt, cols] * cw_ref[j:j + 1, cols]
    return acc


def _fill_halo(xs, x_ref, xp_ref, tt):
    first = pl.program_id(0) == 0
    xs[0:HALO, :] = jnp.where(first, 0.0, xp_ref[...])
    xs[HALO:HALO + tt, :] = x_ref[...]


def _conv_specs(tt, ch):
    cur = pl.BlockSpec((tt, ch), lambda i: (i, 0))
    prev = pl.BlockSpec((HALO, ch), lambda i: (jnp.maximum(i * (tt // HALO) - 1, 0), 0))
    return cur, prev


def _conv_fwd(proj, conv_w, *, name):
    t = proj.shape[0]
    ch = conv_w.shape[1]
    nqk = HEADS * DK
    tt = _pick(t, (256,))
    cur, prev = _conv_specs(tt, ch)

    def body(x_ref, xp_ref, cw_ref, q_ref, k_ref, v_ref, xs):
        _fill_halo(xs, x_ref, xp_ref, tt)
        for cb in range(ch // LANES):
            c = _conv_cols(xs, cw_ref, cb, tt)
            s = c * _sigmoid(c)
            if cb < 2 * HEADS:
                s = s * lax.rsqrt(jnp.sum(s * s, axis=-1, keepdims=True) + EPS)
                if cb < HEADS:
                    q_ref[:, cb * LANES:(cb + 1) * LANES] = s * (DK ** -0.5)
                else:
                    k_ref[:, (cb - HEADS) * LANES:(cb - HEADS + 1) * LANES] = s
            else:
                v_ref[:, (cb - 2 * HEADS) * LANES:(cb - 2 * HEADS + 1) * LANES] = s

    return pl.pallas_call(
        body, name=name, grid=(t // tt,),
        in_specs=[cur, prev, pl.BlockSpec((CONV_W, ch), lambda i: (0, 0))],
        out_specs=[pl.BlockSpec((tt, nqk), lambda i: (i, 0)), pl.BlockSpec((tt, nqk), lambda i: (i, 0)),
                   pl.BlockSpec((tt, ch - 2 * nqk), lambda i: (i, 0))],
        out_shape=[jax.ShapeDtypeStruct((t, nqk), F32), jax.ShapeDtypeStruct((t, nqk), F32),
                   jax.ShapeDtypeStruct((t, ch - 2 * nqk), F32)],
        scratch_shapes=[pltpu.VMEM((HALO + tt, ch), F32)],
        compiler_params=_params(("arbitrary",)),
    )(proj, proj, conv_w)


def _conv_bwd_pre(proj, conv_w, dq, dk, dv, *, name):
    t = proj.shape[0]
    ch = conv_w.shape[1]
    nqk = HEADS * DK
    tt = _pick(t, (256,))
    cur, prev = _conv_specs(tt, ch)

    def body(x_ref, xp_ref, cw_ref, dq_ref, dk_ref, dv_ref, dc_ref, xs):
        _fill_halo(xs, x_ref, xp_ref, tt)
        for cb in range(ch // LANES):
            c = _conv_cols(xs, cw_ref, cb, tt)
            sg = _sigmoid(c)
            s = c * sg
            if cb < 2 * HEADS:
                if cb < HEADS:
                    d, scale = dq_ref[:, cb * LANES:(cb + 1) * LANES], DK ** -0.5
                else:
                    d, scale = dk_ref[:, (cb - HEADS) * LANES:(cb - HEADS + 1) * LANES], 1.0
                rinv = lax.rsqrt(jnp.sum(s * s, axis=-1, keepdims=True) + EPS)
                ds = scale * rinv * (d - s * (rinv * rinv) * jnp.sum(d * s, axis=-1, keepdims=True))
            else:
                ds = dv_ref[:, (cb - 2 * HEADS) * LANES:(cb - 2 * HEADS + 1) * LANES]
            dc_ref[:, cb * LANES:(cb + 1) * LANES] = ds * sg * (1.0 + c * (1.0 - sg))

    return pl.pallas_call(
        body, name=name, grid=(t // tt,),
        in_specs=[cur, prev, pl.BlockSpec((CONV_W, ch), lambda i: (0, 0)),
                  pl.BlockSpec((tt, nqk), lambda i: (i, 0)), pl.BlockSpec((tt, nqk), lambda i: (i, 0)),
                  pl.BlockSpec((tt, ch - 2 * nqk), lambda i: (i, 0))],
        out_specs=pl.BlockSpec((tt, ch), lambda i: (i, 0)),
        out_shape=jax.ShapeDtypeStruct((t, ch), F32),
        scratch_shapes=[pltpu.VMEM((HALO + tt, ch), F32)],
        compiler_params=_params(("arbitrary",)),
    )(proj, proj, conv_w, dq, dk, dv)


def _conv_bwd(proj, dc, conv_w, *, name):
    t = proj.shape[0]
    ch = conv_w.shape[1]
    tt = _pick(t, (256,))
    nt = t // tt
    cur, prev = _conv_specs(tt, ch)
    nxt = pl.BlockSpec((HALO, ch), lambda i: (jnp.minimum((i + 1) * (tt // HALO), t // HALO - 1), 0))

    def body(x_ref, xp_ref, d_ref, dn_ref, cw_ref, dx_ref, dw_ref, xs, ds):
        _fill_halo(xs, x_ref, xp_ref, tt)
        last = pl.program_id(0) == nt - 1
        ds[0:tt, :] = d_ref[...]
        ds[tt:tt + HALO, :] = jnp.where(last, 0.0, dn_ref[...])

        @pl.when(pl.program_id(0) == 0)
        def _():
            dw_ref[...] = jnp.zeros_like(dw_ref)

        base = HALO - (CONV_W - 1)
        for cb in range(ch // LANES):
            cols = slice(cb * LANES, (cb + 1) * LANES)
            d0 = ds[0:tt, cols]
            acc = None
            for j in range(CONV_W):
                sh = CONV_W - 1 - j
                term = ds[sh:sh + tt, cols] * cw_ref[j:j + 1, cols]
                acc = term if acc is None else acc + term
                dw_ref[j:j + 1, cols] += jnp.sum(d0 * xs[base + j:base + j + tt, cols], axis=0, keepdims=True)
            dx_ref[:, cols] = acc.astype(BF16)

    return pl.pallas_call(
        body, name=name, grid=(nt,),
        in_specs=[cur, prev, cur, nxt, pl.BlockSpec((CONV_W, ch), lambda i: (0, 0))],
        out_specs=[pl.BlockSpec((tt, ch), lambda i: (i, 0)), pl.BlockSpec((HALO, ch), lambda i: (0, 0))],
        out_shape=[jax.ShapeDtypeStruct((t, ch), BF16), jax.ShapeDtypeStruct((HALO, ch), F32)],
        scratch_shapes=[pltpu.VMEM((HALO + tt, ch), F32), pltpu.VMEM((tt + HALO, ch), F32)],
        compiler_params=_params(("arbitrary",)),
    )(proj, proj, dc, dc, conv_w)


def _bg_fwd(projs, alog_row, dtb_row, *, name):
    t = projs.shape[0]
    tt = _pick(t, (512, 256))

    def body(p_ref, al_ref, db_ref, o_ref):
        p = p_ref[...]
        lane = lax.broadcasted_iota(jnp.int32, p.shape, 1)
        z = p + db_ref[...]
        sp = jnp.maximum(z, 0.0) + jnp.log1p(jnp.exp(-jnp.abs(z)))
        g = -jnp.exp(al_ref[...]) * sp
        o_ref[...] = jnp.where(lane < HEADS, _sigmoid(p), jnp.where(lane < 2 * HEADS, g, 0.0))

    blk = pl.BlockSpec((tt, LANES), lambda i: (i, 0))
    row = pl.BlockSpec((1, LANES), lambda i: (0, 0))
    return pl.pallas_call(
        body, name=name, grid=(t // tt,), in_specs=[blk, row, row], out_specs=blk,
        out_shape=jax.ShapeDtypeStruct((t, LANES), F32), compiler_params=_params(("parallel",)),
    )(projs, alog_row, dtb_row)


def _bg_bwd(projs, dbg, alog_row, dtb_row, *, name):
    t = projs.shape[0]
    tt = _pick(t, (512, 256))

    def body(p_ref, d_ref, al_ref, db_ref, o_ref, dal_ref, ddb_ref):
        p, d = p_ref[...], d_ref[...]
        lane = lax.broadcasted_iota(jnp.int32, p.shape, 1)
        isg = (lane >= HEADS) & (lane < 2 * HEADS)
        be = _sigmoid(p)
        z = p + db_ref[...]
        sp = jnp.maximum(z, 0.0) + jnp.log1p(jnp.exp(-jnp.abs(z)))
        ea = jnp.exp(al_ref[...])
        d_aa = jnp.where(isg, d * (-ea) * _sigmoid(z), 0.0)
        o_ref[...] = jnp.where(lane < HEADS, d * be * (1.0 - be), d_aa).astype(BF16)

        @pl.when(pl.program_id(0) == 0)
        def _():
            dal_ref[...] = jnp.zeros_like(dal_ref)
            ddb_ref[...] = jnp.zeros_like(ddb_ref)

        dal_ref[...] += jnp.sum(jnp.where(isg, d * (-ea) * sp, 0.0), axis=0, keepdims=True)
        ddb_ref[...] += jnp.sum(d_aa, axis=0, keepdims=True)

    blk = pl.BlockSpec((tt, LANES), lambda i: (i, 0))
    row = pl.BlockSpec((1, LANES), lambda i: (0, 0))
    return pl.pallas_call(
        body, name=name, grid=(t // tt,), in_specs=[blk, blk, row, row], out_specs=[blk, row, row],
        out_shape=[jax.ShapeDtypeStruct((t, LANES), BF16), jax.ShapeDtypeStruct((1, LANES), F32),
                   jax.ShapeDtypeStruct((1, LANES), F32)],
        compiler_params=_params(("arbitrary",)),
    )(projs, dbg, alog_row, dtb_row)


def _hn_specs(n_chunks, rev=False):
    def nn(n):
        return n_chunks - 1 - n if rev else n
    tok = lambda w: pl.BlockSpec((CHUNK, HB * w), lambda g, n: (nn(n), g))
    per = lambda a, b: pl.BlockSpec((HB, None, a, b), lambda g, n: (g, nn(n), 0, 0))
    return tok, per


def _hcols(hh, w):
    return slice(hh * w, (hh + 1) * w)


def _decay(gc_col, gc_row):
    r, c = _iota2(CHUNK)
    d = jnp.exp(jnp.minimum(gc_col - gc_row, 0.0))
    return jnp.where(r >= c, d, 0.0), jnp.where(r > c, d, 0.0)


def _gdn_prep(q, k, v, beta_r, g_r, *, name):
    t = q.shape[0]
    n_chunks = t // CHUNK
    tok, per = _hn_specs(n_chunks)

    def body(q_ref, k_ref, v_ref, b_ref, g_ref, gc_ref, ti_ref, u_ref, w_ref, p_ref):
        r, c = _iota2(CHUNK)
        for hh in range(HB):
            kk, g_row = k_ref[:, _hcols(hh, DK)], g_ref[hh]
            gc_col = jnp.sum(jnp.where(c <= r, jnp.broadcast_to(g_row, (CHUNK, CHUNK)), 0.0), axis=1, keepdims=True)
            gc_row = _col_to_row(gc_col)
            beta_col = _row_to_col(b_ref[hh])
            d_incl, d_strict = _decay(gc_col, gc_row)
            kb = kk * beta_col
            a = _dot_nt(kb, kk) * d_strict
            y = -a
            pw = a
            for _ in range(5):
                pw = _dot(pw, pw)
                y = y + pw + _dot(y, pw)
            vb = v_ref[:, _hcols(hh, DV)] * beta_col
            kbe = kb * jnp.exp(gc_col)
            gc_ref[hh] = gc_row
            ti_ref[hh] = y
            u_ref[:, _hcols(hh, DV)] = vb + _dot(y, vb)
            w_ref[:, _hcols(hh, DK)] = kbe + _dot(y, kbe)
            p_ref[hh] = _dot_nt(q_ref[:, _hcols(hh, DK)], kk) * d_incl

    return pl.pallas_call(
        body, name=name, grid=(HEADS // HB, n_chunks),
        in_specs=[tok(DK), tok(DK), tok(DV), per(1, CHUNK), per(1, CHUNK)],
        out_specs=[per(1, CHUNK), per(CHUNK, CHUNK), tok(DV), tok(DK), per(CHUNK, CHUNK)],
        out_shape=[jax.ShapeDtypeStruct((HEADS, n_chunks, 1, CHUNK), F32),
                   jax.ShapeDtypeStruct((HEADS, n_chunks, CHUNK, CHUNK), F32),
                   jax.ShapeDtypeStruct((t, HEADS * DV), F32), jax.ShapeDtypeStruct((t, HEADS * DK), F32),
                   jax.ShapeDtypeStruct((HEADS, n_chunks, CHUNK, CHUNK), F32)],
        compiler_params=_params(("parallel", "parallel")),
    )(q, k, v, beta_r, g_r)


def _gdn_scan(q, k, u, w, p, gc_r, *, name):
    t = q.shape[0]
    n_chunks = t // CHUNK
    tok, per = _hn_specs(n_chunks)

    def body(q_ref, k_ref, u_ref, w_ref, p_ref, gc_ref, o_ref, s_ref, st):
        @pl.when(pl.program_id(1) == 0)
        def _():
            st[...] = jnp.zeros_like(st)

        for hh in range(HB):
            s = st[hh]
            s_ref[hh] = s
            gc_row = gc_ref[hh]
            gc_col = _row_to_col(gc_row)
            glast = gc_row[:, CHUNK - 1:CHUNK]
            vn = u_ref[:, _hcols(hh, DV)] - _dot(w_ref[:, _hcols(hh, DK)], s)
            o_ref[:, _hcols(hh, DV)] = _dot(q_ref[:, _hcols(hh, DK)] * jnp.exp(gc_col), s) + _dot(p_ref[hh], vn)
            st[hh] = s * jnp.exp(glast) + _dot_tn(k_ref[:, _hcols(hh, DK)] * jnp.exp(glast - gc_col), vn)

    return pl.pallas_call(
        body, name=name, grid=(HEADS // HB, n_chunks),
        in_specs=[tok(DK), tok(DK), tok(DV), tok(DK), per(CHUNK, CHUNK), per(1, CHUNK)],
        out_specs=[tok(DV), per(DK, DV)],
        out_shape=[jax.ShapeDtypeStruct((t, HEADS * DV), F32),
                   jax.ShapeDtypeStruct((HEADS, n_chunks, DK, DV), F32)],
        scratch_shapes=[pltpu.VMEM((HB, DK, DV), F32)],
        compiler_params=_params(("arbitrary", "arbitrary")),
    )(q, k, u, w, p, gc_r)


def _gdn_bwd(q, k, v, beta_r, gc_r, ti, u, w, s_all, do, *, name):
    t = q.shape[0]
    n_chunks = t // CHUNK
    tok, per = _hn_specs(n_chunks, rev=True)

    def body(q_ref, k_ref, v_ref, b_ref, gc_ref, ti_ref, u_ref, w_ref, s_ref, do_ref,
             dq_ref, dk_ref, dv_ref, db_ref, dg_ref, dst):
        @pl.when(pl.program_id(1) == 0)
        def _():
            dst[...] = jnp.zeros_like(dst)

        r, c = _iota2(CHUNK)
        for hh in range(HB):
            qq, kk, vv = q_ref[:, _hcols(hh, DK)], k_ref[:, _hcols(hh, DK)], v_ref[:, _hcols(hh, DV)]
            uu, ww = u_ref[:, _hcols(hh, DV)], w_ref[:, _hcols(hh, DK)]
            s, d_s, d_o, yy = s_ref[hh], dst[hh], do_ref[:, _hcols(hh, DV)], ti_ref[hh]
            gc_row = gc_ref[hh]
            gc_col = _row_to_col(gc_row)
            beta_col = _row_to_col(b_ref[hh])
            eg = jnp.exp(gc_col)
            glast = gc_row[:, CHUNK - 1:CHUNK]
            eglast = jnp.exp(glast)
            e2 = jnp.exp(glast - gc_col)
            d_incl, d_strict = _decay(gc_col, gc_row)
            kb = kk * beta_col
            kkt = _dot_nt(kb, kk)
            a = kkt * d_strict
            qk = _dot_nt(qq, kk)
            pp = qk * d_incl
            vn = uu - _dot(ww, s)
            ke, qe = kk * e2, qq * eg

            dvn = _dot_tn(pp, d_o) + _dot(ke, d_s)
            dqk = _dot_nt(d_o, vn) * d_incl
            dqe = _dot_nt(d_o, s)
            dke = _dot_nt(vn, d_s)
            de2 = jnp.sum(dke * ke, axis=1, keepdims=True)
            dglast = jnp.sum(de2, axis=0, keepdims=True) + eglast * jnp.sum(
                jnp.sum(d_s * s, axis=1, keepdims=True), axis=0, keepdims=True)
            dst[hh] = d_s * eglast + _dot_tn(qe, d_o) - _dot_tn(ww, dvn)

            dw = -_dot_nt(dvn, s)
            dvb = dvn + _dot_tn(yy, dvn)
            dkbe = dw + _dot_tn(yy, dw)
            d_a = jnp.where(r > c, -(_dot_nt(dvb, uu) + _dot_nt(dkbe, ww)), 0.0)
            dm = d_a * d_strict
            dkb = _dot(dm, kk) + dkbe * eg
            e_mat = d_a * a + dqk * qk
            dgc = (jnp.sum(dqe * qe, axis=1, keepdims=True) - de2 + jnp.sum(dkbe * kb * eg, axis=1, keepdims=True)
                   + jnp.sum(e_mat, axis=1, keepdims=True) - _row_to_col(jnp.sum(e_mat, axis=0, keepdims=True)))
            rows = lax.broadcasted_iota(jnp.int32, (CHUNK, 1), 0)
            dgc = dgc + jnp.where(rows == CHUNK - 1, dglast, 0.0)

            dq_ref[:, _hcols(hh, DK)] = dqe * eg + _dot(dqk, kk)
            dk_ref[:, _hcols(hh, DK)] = _dot_tn(dqk, qq) + dke * e2 + _dot_tn(dm, kb) + dkb * beta_col
            dv_ref[:, _hcols(hh, DV)] = dvb * beta_col
            dbeta = jnp.sum(dkb * kk, axis=1, keepdims=True) + jnp.sum(dvb * vv, axis=1, keepdims=True)
            db_ref[hh] = _col_to_row(dbeta)
            dg_ref[hh] = jnp.sum(jnp.where(r >= c, jnp.broadcast_to(dgc, (CHUNK, CHUNK)), 0.0), axis=0, keepdims=True)

    return pl.pallas_call(
        body, name=name, grid=(HEADS // HB, n_chunks),
        in_specs=[tok(DK), tok(DK), tok(DV), per(1, CHUNK), per(1, CHUNK), per(CHUNK, CHUNK), tok(DV), tok(DK),
                  per(DK, DV), tok(DV)],
        out_specs=[tok(DK), tok(DK), tok(DV), per(1, CHUNK), per(1, CHUNK)],
        out_shape=[jax.ShapeDtypeStruct((t, HEADS * DK), F32), jax.ShapeDtypeStruct((t, HEADS * DK), F32),
                   jax.ShapeDtypeStruct((t, HEADS * DV), F32),
                   jax.ShapeDtypeStruct((HEADS, n_chunks, 1, CHUNK), F32),
                   jax.ShapeDtypeStruct((HEADS, n_chunks, 1, CHUNK), F32)],
        scratch_shapes=[pltpu.VMEM((HB, DK, DV), F32)],
        compiler_params=_params(("arbitrary", "arbitrary")),
    )(q, k, v, beta_r, gc_r, ti, u, w, s_all, do)


def _rot(x, cos2, sin2):
    return x * cos2 + pltpu.roll(x, DK // 2, 1) * sin2


def _unrot(d, cos2, sin2):
    return d * cos2 + pltpu.roll(d * sin2, DK // 2, 1)


def _ret_consts(lg):
    r, c = _iota2(CHUNK)
    dm = jnp.where(r >= c, jnp.exp((r - c).astype(F32) * lg), 0.0)
    pos = lax.broadcasted_iota(jnp.int32, (CHUNK, 1), 0).astype(F32)
    return dm, jnp.exp((pos + 1.0) * lg), jnp.exp((CHUNK - 1.0 - pos) * lg), jnp.exp(CHUNK * lg)


def _ret_fwd(proj, cos2, sin2, lg_tab, col_q, col_k, col_v, *, name):
    t = proj.shape[0]
    n_chunks = t // CHUNK
    tok, per = _hn_specs(n_chunks)
    bq, bk, bv = col_q // (HB * DK), col_k // (HB * DK), col_v // (HB * DV)

    def body(q_ref, k_ref, v_ref, cos_ref, sin_ref, lg_ref, o_ref, s_ref, st):
        @pl.when(pl.program_id(1) == 0)
        def _():
            st[...] = jnp.zeros_like(st)

        cos2v, sin2v = cos_ref[...], sin_ref[...]
        for hh in range(HB):
            s = st[hh]
            s_ref[hh] = s
            dm, xi, zeta, gam = _ret_consts(lg_ref[hh][:, 0:1])
            qq = _rot(q_ref[:, _hcols(hh, DK)], cos2v, sin2v)
            kk = _rot(k_ref[:, _hcols(hh, DK)], cos2v, sin2v) * (DK ** -0.5)
            vv = v_ref[:, _hcols(hh, DV)]
            o_ref[:, _hcols(hh, DV)] = _dot(_dot_nt(qq, kk) * dm, vv) + _dot(qq, s) * xi
            st[hh] = s * gam + _dot_tn(kk * zeta, vv)

    return pl.pallas_call(
        body, name=name, grid=(HEADS // HB, n_chunks),
        in_specs=[pl.BlockSpec((CHUNK, HB * DK), lambda g, n: (n, bq + g)),
                  pl.BlockSpec((CHUNK, HB * DK), lambda g, n: (n, bk + g)),
                  pl.BlockSpec((CHUNK, HB * DV), lambda g, n: (n, bv + g)),
                  pl.BlockSpec((CHUNK, DK), lambda g, n: (n, 0)), pl.BlockSpec((CHUNK, DK), lambda g, n: (n, 0)),
                  pl.BlockSpec((HB, 1, LANES), lambda g, n: (g, 0, 0))],
        out_specs=[tok(DV), per(DK, DV)],
        out_shape=[jax.ShapeDtypeStruct((t, HEADS * DV), F32),
                   jax.ShapeDtypeStruct((HEADS, n_chunks, DK, DV), F32)],
        scratch_shapes=[pltpu.VMEM((HB, DK, DV), F32)],
        compiler_params=_params(("arbitrary", "arbitrary")),
    )(proj, proj, proj, cos2, sin2, lg_tab)


def _ret_bwd(proj, cos2, sin2, lg_tab, s_all, do, col_q, col_k, col_v, *, name):
    t = proj.shape[0]
    n_chunks = t // CHUNK
    tok, per = _hn_specs(n_chunks, rev=True)
    bq, bk, bv = col_q // (HB * DK), col_k // (HB * DK), col_v // (HB * DV)
    rv = lambda n: n_chunks - 1 - n

    def body(q_ref, k_ref, v_ref, cos_ref, sin_ref, lg_ref, s_ref, do_ref, dq_ref, dk_ref, dv_ref, dst):
        @pl.when(pl.program_id(1) == 0)
        def _():
            dst[...] = jnp.zeros_like(dst)

        cos2v, sin2v = cos_ref[...], sin_ref[...]
        for hh in range(HB):
            s, d_s, d_o = s_ref[hh], dst[hh], do_ref[:, _hcols(hh, DV)]
            dm, xi, zeta, gam = _ret_consts(lg_ref[hh][:, 0:1])
            qq = _rot(q_ref[:, _hcols(hh, DK)], cos2v, sin2v)
            kk = _rot(k_ref[:, _hcols(hh, DK)], cos2v, sin2v) * (DK ** -0.5)
            vv = v_ref[:, _hcols(hh, DV)]
            sc = _dot_nt(qq, kk) * dm
            dxo = d_o * xi
            dsc = _dot_nt(d_o, vv) * dm
            dv_ref[:, _hcols(hh, DV)] = (_dot_tn(sc, d_o) + _dot(kk * zeta, d_s)).astype(BF16)
            dq = _dot(dsc, kk) + _dot_nt(dxo, s)
            dk = _dot_tn(dsc, qq) + _dot_nt(vv, d_s) * zeta
            dst[hh] = d_s * gam + _dot_tn(qq, dxo)
            dq_ref[:, _hcols(hh, DK)] = _unrot(dq, cos2v, sin2v).astype(BF16)
            dk_ref[:, _hcols(hh, DK)] = (_unrot(dk, cos2v, sin2v) * (DK ** -0.5)).astype(BF16)

    return pl.pallas_call(
        body, name=name, grid=(HEADS // HB, n_chunks),
        in_specs=[pl.BlockSpec((CHUNK, HB * DK), lambda g, n: (rv(n), bq + g)),
                  pl.BlockSpec((CHUNK, HB * DK), lambda g, n: (rv(n), bk + g)),
                  pl.BlockSpec((CHUNK, HB * DV), lambda g, n: (rv(n), bv + g)),
                  pl.BlockSpec((CHUNK, DK), lambda g, n: (rv(n), 0)),
                  pl.BlockSpec((CHUNK, DK), lambda g, n: (rv(n), 0)),
                  pl.BlockSpec((HB, 1, LANES), lambda g, n: (g, 0, 0)),
                  per(DK, DV), tok(DV)],
        out_specs=[tok(DK), tok(DK), tok(DV)],
        out_shape=[jax.ShapeDtypeStruct((t, HEADS * DK), BF16), jax.ShapeDtypeStruct((t, HEADS * DK), BF16),
                   jax.ShapeDtypeStruct((t, HEADS * DV), BF16)],
        scratch_shapes=[pltpu.VMEM((HB, DK, DV), F32)],
        compiler_params=_params(("arbitrary", "arbitrary")),
    )(proj, proj, proj, cos2, sin2, lg_tab, s_all, do)


def _merge_parts(oa, ob, z, rg, ga, gb):
    ra = lax.rsqrt(jnp.mean(oa * oa, axis=-1, keepdims=True) + EPS)
    xa = oa * ra
    mu = jnp.mean(ob, axis=-1, keepdims=True)
    cen = ob - mu
    rb = lax.rsqrt(jnp.mean(cen * cen, axis=-1, keepdims=True) + EPS)
    xb = cen * rb
    sz, sr = _sigmoid(z), _sigmoid(rg)
    return ra, xa, rb, xb, sz, sr, _sigmoid(ga), _sigmoid(gb)


def _merge_specs(tt, d, cz, crg, cga, cgb):
    blk = pl.BlockSpec((tt, d), lambda i: (i, 0))
    pcol = lambda c: pl.BlockSpec((tt, d), lambda i: (i, c // d))
    row = pl.BlockSpec((1, d), lambda i: (0, 0))
    return blk, [blk, blk, pcol(cz), pcol(crg), pcol(cga), pcol(cgb), row, row], row


def _merge_fwd(oa, ob, proj, wa, wb, cz, crg, cga, cgb, *, name):
    t, d = oa.shape
    tt = _pick(t, (256,))
    blk, in_specs, _ = _merge_specs(tt, d, cz, crg, cga, cgb)

    def body(oa_ref, ob_ref, z_ref, rg_ref, ga_ref, gb_ref, wa_ref, wb_ref, o_ref):
        for h in range(HEADS):
            cols = slice(h * DV, (h + 1) * DV)
            z, rg = z_ref[:, cols], rg_ref[:, cols]
            _, xa, _, xb, sz, sr, sga, sgb = _merge_parts(
                oa_ref[:, cols], ob_ref[:, cols], z, rg, ga_ref[:, cols], gb_ref[:, cols])
            o_a = xa * wa_ref[:, cols] * (z * sz)
            o_b = xb * wb_ref[:, cols] * (rg * sr)
            o_ref[:, cols] = (sga * o_a + sgb * o_b).astype(BF16)

    return pl.pallas_call(
        body, name=name, grid=(t // tt,), in_specs=in_specs, out_specs=blk,
        out_shape=jax.ShapeDtypeStruct((t, d), BF16), compiler_params=_params(("parallel",)),
    )(oa, ob, proj, proj, proj, proj, wa, wb)


def _merge_bwd(dmix, oa, ob, proj, wa, wb, cz, crg, cga, cgb, *, name):
    t, d = oa.shape
    tt = _pick(t, (256,))
    blk, in_specs, row = _merge_specs(tt, d, cz, crg, cga, cgb)

    def body(dm_ref, oa_ref, ob_ref, z_ref, rg_ref, ga_ref, gb_ref, wa_ref, wb_ref,
             doa_ref, dob_ref, dz_ref, drg_ref, dga_ref, dgb_ref, dwa_ref, dwb_ref):
        @pl.when(pl.program_id(0) == 0)
        def _():
            dwa_ref[...] = jnp.zeros_like(dwa_ref)
            dwb_ref[...] = jnp.zeros_like(dwb_ref)

        for h in range(HEADS):
            cols = slice(h * DV, (h + 1) * DV)
            z, rg, wa_h, wb_h, dmx = z_ref[:, cols], rg_ref[:, cols], wa_ref[:, cols], wb_ref[:, cols], dm_ref[:, cols]
            ra, xa, rb, xb, sz, sr, sga, sgb = _merge_parts(
                oa_ref[:, cols], ob_ref[:, cols], z, rg, ga_ref[:, cols], gb_ref[:, cols])
            na, nb = xa * wa_h, xb * wb_h
            sil_z, sil_r = z * sz, rg * sr
            o_a, o_b = na * sil_z, nb * sil_r
            dga_ref[:, cols] = (dmx * o_a * sga * (1.0 - sga)).astype(BF16)
            dgb_ref[:, cols] = (dmx * o_b * sgb * (1.0 - sgb)).astype(BF16)
            d_oa, d_ob = dmx * sga, dmx * sgb
            dz_ref[:, cols] = (d_oa * na * sz * (1.0 + z * (1.0 - sz))).astype(BF16)
            drg_ref[:, cols] = (d_ob * nb * sr * (1.0 + rg * (1.0 - sr))).astype(BF16)
            dna, dnb = d_oa * sil_z, d_ob * sil_r
            dwa_ref[:, cols] += jnp.sum(dna * xa, axis=0, keepdims=True)
            dwb_ref[:, cols] += jnp.sum(dnb * xb, axis=0, keepdims=True)
            gwa, gwb = dna * wa_h, dnb * wb_h
            doa_ref[:, cols] = ra * (gwa - xa * jnp.mean(gwa * xa, axis=-1, keepdims=True))
            dob_ref[:, cols] = rb * (gwb - jnp.mean(gwb, axis=-1, keepdims=True)
                                     - xb * jnp.mean(gwb * xb, axis=-1, keepdims=True))

    return pl.pallas_call(
        body, name=name, grid=(t // tt,), in_specs=[blk] + in_specs,
        out_specs=[blk, blk, blk, blk, blk, blk, row, row],
        out_shape=[jax.ShapeDtypeStruct((t, d), F32)] * 2 + [jax.ShapeDtypeStruct((t, d), BF16)] * 4
        + [jax.ShapeDtypeStruct((1, d), F32)] * 2,
        compiler_params=_params(("arbitrary",)),
    )(dmix, oa, ob, proj, proj, proj, proj, wa, wb)


def _row_block(rows, cols, itemsize=4, target=1 << 20):
    for rb in (512, 256, 128, 64, 32, 16, 8):
        if rows % rb == 0 and rb * cols * itemsize <= target:
            return rb
    return rows


def _adamw(w, g, m, v, *, name):
    rows, cols = w.shape
    rb = _row_block(rows, cols)

    def body(w_ref, g_ref, m_ref, v_ref, d_ref, nm_ref, nv_ref):
        gg = g_ref[...]
        mm = ADAM_B1 * m_ref[...] + (1.0 - ADAM_B1) * gg
        vv = ADAM_B2 * v_ref[...] + (1.0 - ADAM_B2) * (gg * gg)
        m_hat = mm / (1.0 - ADAM_B1 ** ADAM_STEP)
        v_hat = vv / (1.0 - ADAM_B2 ** ADAM_STEP)
        d_ref[...] = -ADAM_LR * (m_hat / (jnp.sqrt(v_hat) + ADAM_EPS) + ADAM_WD * w_ref[...])
        nm_ref[...] = mm
        nv_ref[...] = vv

    blk = pl.BlockSpec((rb, cols), lambda i: (i, 0))
    return pl.pallas_call(
        body, name=name, grid=(rows // rb,), in_specs=[blk] * 4, out_specs=[blk] * 3,
        out_shape=[jax.ShapeDtypeStruct((rows, cols), F32)] * 3, compiler_params=_params(("parallel",)),
    )(w, g, m, v)


def _pair_add(g, rsib, c_idx, *, name):
    _, _, hr, cols = g.shape
    rb = _row_block(hr, cols, 2)

    def body(c_ref, g_ref, r_ref, o_ref):
        o_ref[...] = (g_ref[...].astype(F32) + r_ref[...].astype(F32)).astype(BF16)

    return pl.pallas_call(
        body, name=name,
        grid_spec=pltpu.PrefetchScalarGridSpec(
            num_scalar_prefetch=1, grid=(N_CHIPS, hr // rb),
            in_specs=[pl.BlockSpec((None, None, rb, cols), lambda j, i, c: (j, c[0], i, 0)),
                      pl.BlockSpec((None, rb, cols), lambda j, i, c: (j, i, 0))],
            out_specs=pl.BlockSpec((None, rb, cols), lambda j, i, c: (j, i, 0))),
        out_shape=jax.ShapeDtypeStruct((N_CHIPS, hr, cols), BF16),
        compiler_params=_params(("parallel", "parallel")),
    )(c_idx, g, rsib)


def _chip_sum(qb, *, name):
    _, hr, cols = qb.shape
    rb = _row_block(hr, cols, 2, 1 << 19)

    def body(q_ref, o_ref):
        acc = q_ref[0].astype(F32)
        for i in range(1, N_CHIPS):
            acc = acc + q_ref[i].astype(F32)
        o_ref[...] = acc

    return pl.pallas_call(
        body, name=name, grid=(hr // rb,),
        in_specs=[pl.BlockSpec((N_CHIPS, rb, cols), lambda i: (0, i, 0))],
        out_specs=pl.BlockSpec((rb, cols), lambda i: (i, 0)),
        out_shape=jax.ShapeDtypeStruct((hr, cols), F32), compiler_params=_params(("parallel",)),
    )(qb)


def _place():
    x, y, c = lax.axis_index("x"), lax.axis_index("y"), lax.axis_index("c")
    return x, y, c, [(1 - x, y), (x, 1 - y), (1 - x, 1 - y)]


ANY = pl.BlockSpec(memory_space=pl.ANY)


def _gather_big(shards, *, name):
    nw = len(shards)

    def body(*refs):
        ins, outs = refs[:nw], refs[nw:2 * nw]
        ssem, rsem = refs[2 * nw:]
        x, y, c, chips = _place()
        s_me = 2 * x + y

        def rows(w, half):
            hr = shards[w].shape[0] // 2
            return pl.ds(pl.multiple_of(half * hr, 16), hr)

        def rcopy(w, k, src, slot, half, to):
            return pltpu.make_async_remote_copy(
                src_ref=src, dst_ref=outs[w].at[slot, rows(w, half)],
                send_sem=ssem.at[w * 6 + k], recv_sem=rsem.at[w * 6 + k], device_id=to, device_id_type=MESH)

        sent = []
        for w in range(nw):
            for j, chip in enumerate(chips):
                cp = rcopy(w, j, ins[w].at[rows(w, c)], s_me, c, (chip[0], chip[1], c))
                cp.start()
                sent.append(cp)
        for w in range(nw):
            for j, chip in enumerate(chips):
                s_j = 2 * chip[0] + chip[1]
                rcopy(w, j, ins[w].at[rows(w, c)], s_j, c, (x, y, c)).wait_recv()
                cp = rcopy(w, 3 + j, outs[w].at[s_j, rows(w, c)], s_j, c, (x, y, 1 - c))
                cp.start()
                sent.append(cp)
        for w in range(nw):
            for j, chip in enumerate(chips):
                s_j = 2 * chip[0] + chip[1]
                rcopy(w, 3 + j, ins[w].at[rows(w, c)], s_j, 1 - c, (x, y, c)).wait_recv()
        for cp in sent:
            cp.wait_send()

    return pl.pallas_call(
        body, name=name, in_specs=[ANY] * nw, out_specs=[ANY] * nw,
        out_shape=[jax.ShapeDtypeStruct((N_CHIPS,) + s.shape, s.dtype) for s in shards],
        scratch_shapes=[pltpu.SemaphoreType.DMA((nw * 6,)), pltpu.SemaphoreType.DMA((nw * 6,))],
        compiler_params=pltpu.CompilerParams(has_side_effects=True),
    )(*shards)


def _pair_exchange(gs, *, name):
    nw = len(gs)

    def body(*refs):
        ins, outs = refs[:nw], refs[nw:2 * nw]
        ssem, rsem = refs[2 * nw:]
        x, y, c, _ = _place()
        cps = []
        for w in range(nw):
            cp = pltpu.make_async_remote_copy(
                src_ref=ins[w].at[:, 1 - c], dst_ref=outs[w], send_sem=ssem.at[w], recv_sem=rsem.at[w],
                device_id=(x, y, 1 - c), device_id_type=MESH)
            cp.start()
            cps.append(cp)
        for cp in cps:
            cp.wait()

    return pl.pallas_call(
        body, name=name, in_specs=[ANY] * nw, out_specs=[ANY] * nw,
        out_shape=[jax.ShapeDtypeStruct((g.shape[0],) + g.shape[2:], g.dtype) for g in gs],
        scratch_shapes=[pltpu.SemaphoreType.DMA((nw,)), pltpu.SemaphoreType.DMA((nw,))],
        compiler_params=pltpu.CompilerParams(has_side_effects=True),
    )(*gs)


def _chip_scatter(ps, *, name):
    nw = len(ps)

    def body(*refs):
        ins, outs = refs[:nw], refs[nw:2 * nw]
        ssem, rsem = refs[2 * nw:]
        x, y, c, chips = _place()
        s_me = 2 * x + y
        cps = []
        for w in range(nw):
            for j, chip in enumerate(chips):
                s_j = 2 * chip[0] + chip[1]
                cp = pltpu.make_async_remote_copy(
                    src_ref=ins[w].at[s_j], dst_ref=outs[w].at[s_me],
                    send_sem=ssem.at[w * 3 + j], recv_sem=rsem.at[w * 3 + j],
                    device_id=(chip[0], chip[1], c), device_id_type=MESH)
                cp.start()
                cps.append(cp)
        for cp in cps:
            cp.wait()

    return pl.pallas_call(
        body, name=name, in_specs=[ANY] * nw, out_specs=[ANY] * nw,
        out_shape=[jax.ShapeDtypeStruct(p.shape, p.dtype) for p in ps],
        scratch_shapes=[pltpu.SemaphoreType.DMA((nw * 3,)), pltpu.SemaphoreType.DMA((nw * 3,))],
        compiler_params=pltpu.CompilerParams(has_side_effects=True),
    )(*ps)


def _pair_share(hs, *, name):
    nw = len(hs)

    def body(*refs):
        ins, outs = refs[:nw], refs[nw:2 * nw]
        ssem, rsem = refs[2 * nw:]
        x, y, c, _ = _place()
        cps = []
        for w in range(nw):
            cp = pltpu.make_async_remote_copy(
                src_ref=ins[w], dst_ref=outs[w], send_sem=ssem.at[w], recv_sem=rsem.at[w],
                device_id=(x, y, 1 - c), device_id_type=MESH)
            cp.start()
            cps.append(cp)
        for cp in cps:
            cp.wait()

    return pl.pallas_call(
        body, name=name, in_specs=[ANY] * nw, out_specs=[ANY] * nw,
        out_shape=[jax.ShapeDtypeStruct(h.shape, h.dtype) for h in hs],
        scratch_shapes=[pltpu.SemaphoreType.DMA((nw,)), pltpu.SemaphoreType.DMA((nw,))],
        compiler_params=pltpu.CompilerParams(has_side_effects=True),
    )(*hs)


def _gather_small(a, *, name):
    def body(a_ref, o_ref, ssem, rsem):
        x, y, c, chips = _place()
        s_me = 2 * x + y
        o_ref[s_me] = a_ref[...]
        cps = []
        for j, chip in enumerate(chips):
            cp = pltpu.make_async_remote_copy(
                src_ref=a_ref, dst_ref=o_ref.at[s_me], send_sem=ssem.at[j], recv_sem=rsem.at[j],
                device_id=(chip[0], chip[1], c), device_id_type=MESH)
            cp.start()
            cps.append(cp)
        for cp in cps:
            cp.wait()

    vm = pl.BlockSpec(memory_space=pltpu.VMEM)
    return pl.pallas_call(
        body, name=name, in_specs=[vm], out_specs=vm,
        out_shape=jax.ShapeDtypeStruct((N_CHIPS,) + a.shape, a.dtype),
        scratch_shapes=[pltpu.SemaphoreType.DMA((3,)), pltpu.SemaphoreType.DMA((3,))],
    )(a)


def _allreduce_small(p, *, name):
    def body(p_ref, o_ref, buf, ssem, rsem):
        x, y, c, _ = _place()
        me = 4 * x + 2 * y + c
        buf[me] = p_ref[...]
        cps = []
        for k in range(1, N_DEV):
            fx, fy, fc = (k >> 2) & 1, (k >> 1) & 1, k & 1
            peer = (x + fx - 2 * x * fx, y + fy - 2 * y * fy, c + fc - 2 * c * fc)
            cp = pltpu.make_async_remote_copy(
                src_ref=buf.at[me], dst_ref=buf.at[me], send_sem=ssem.at[k - 1], recv_sem=rsem.at[k - 1],
                device_id=peer, device_id_type=MESH)
            cp.start()
            cps.append(cp)
        for cp in cps:
            cp.wait()
        acc = buf[0]
        for i in range(1, N_DEV):
            acc = acc + buf[i]
        o_ref[...] = acc

    vm = pl.BlockSpec(memory_space=pltpu.VMEM)
    return pl.pallas_call(
        body, name=name, in_specs=[vm], out_specs=vm,
        out_shape=jax.ShapeDtypeStruct(p.shape, p.dtype),
        scratch_shapes=[pltpu.VMEM((N_DEV,) + p.shape, p.dtype), pltpu.SemaphoreType.DMA((N_DEV - 1,)),
                        pltpu.SemaphoreType.DMA((N_DEV - 1,))],
    )(p)


def _rows_to_tokens(r):
    h, n = r.shape[0], r.shape[1]
    return r.reshape(h, n * CHUNK).T


def _tokens_to_rows(a):
    t, h = a.shape
    return a.T.reshape(h, t // CHUNK, 1, CHUNK)


def kernel(x, norm1_w, w_in, conv_w, a_log, dt_bias, gdn_norm_w, ret_norm_w, w_out, norm2_w, w_gate, w_up, w_down, norm_f_w, loss_target, m_norm1_w, m_w_in, m_conv_w, m_a_log, m_dt_bias, m_gdn_norm_w, m_ret_norm_w, m_w_out, m_norm2_w, m_w_gate, m_w_up, m_w_down, m_norm_f_w, v_norm1_w, v_w_in, v_conv_w, v_a_log, v_dt_bias, v_gdn_norm_w, v_ret_norm_w, v_w_out, v_norm2_w, v_w_gate, v_w_up, v_w_down, v_norm_f_w):
    t, d = x.shape[1], x.shape[2]
    f = w_gate.shape[2] * N_CHIPS
    nqk, nv = HEADS * DK, HEADS * DV
    ncs = w_in.shape[2]
    c_idx = lax.axis_index("c")
    s_idx = 2 * lax.axis_index("x") + lax.axis_index("y")
    xs = x[0]
    tgt = loss_target[0]

    o_az = 2 * nqk + nv
    o_small = o_az + nv
    n_small = 2 * HEADS
    o_rq = o_small
    o_rk, o_rv = o_rq + nqk, o_rq + 2 * nqk
    o_rg = o_rv + nv
    o_ga = o_rg + nv
    o_gb = o_ga + d
    n_main = o_gb + d

    own = [w_in[0].astype(BF16), w_out[0].astype(BF16), w_gate[0].astype(BF16), w_up[0].astype(BF16),
           w_down[0].astype(BF16)]
    put_own = lambda full, mine: lax.dynamic_update_slice(full, mine[None], (s_idx, 0, 0))
    wg_in, wg_out, wg_gate, wg_up, wg_down = [
        put_own(g, o) for g, o in zip(_gather_big(own, name="gather_weights"), own)]
    conv_full = _gather_small(conv_w[0], name="gather_conv_w")
    conv_full = jnp.concatenate([conv_full[i] for i in range(N_CHIPS)], axis=1)
    cuts = []
    lo_small = hi_small = None
    for s in range(N_CHIPS):
        g0, g1 = s * ncs, (s + 1) * ncs
        if g1 <= o_small or g0 >= o_small + n_small:
            cuts.append((s, 0, ncs))
        else:
            lo_small, hi_small = o_small - g0, o_small + n_small - g0
            small_shard = s
            cuts += [(s, 0, lo_small), (s, hi_small, ncs)]
    w_main = jnp.concatenate([wg_in[s][:, a:b] for s, a, b in cuts], axis=1)
    w_small = jnp.pad(wg_in[small_shard][:, lo_small:hi_small], ((0, 0), (0, LANES - n_small)))
    w_o = wg_out.reshape(d, d)
    w_g = jnp.concatenate([wg_gate[i] for i in range(N_CHIPS)], axis=1)
    w_u = jnp.concatenate([wg_up[i] for i in range(N_CHIPS)], axis=1)
    w_d = wg_down.reshape(f, d)

    pad16 = lambda a: jnp.pad(a, ((0, 0), (HEADS, LANES - 2 * HEADS)))
    alog_row, dtb_row = pad16(a_log), pad16(dt_bias)
    wa_row = jnp.tile(gdn_norm_w, (1, HEADS))
    inv = ROPE_BASE ** (-jnp.arange(0, DK, 2, dtype=F32) / DK)
    ang = jnp.arange(t, dtype=F32)[:, None] * inv[None, :]
    cos2 = jnp.concatenate([jnp.cos(ang), jnp.cos(ang)], axis=1)
    sin2 = jnp.concatenate([-jnp.sin(ang), jnp.sin(ang)], axis=1)
    lg = jnp.log1p(-jnp.exp2(-5.0 - jnp.arange(HEADS, dtype=F32)))
    lg_tab = jnp.broadcast_to(lg[:, None, None], (HEADS, 1, LANES))

    fq = f // N_CHIPS
    u1 = _rms_fwd(xs, norm1_w, name="rms1_fwd")
    proj = _mm(u1, w_main, tm=512, tn=2048, tk=d, name="mm_proj")
    projs = _mm(u1, w_small, tm=1024, tn=LANES, tk=d, name="mm_proj_small")
    q_a, k_a, v_a = _conv_fwd(proj, conv_full, name="conv_fwd")
    bg = _bg_fwd(projs, alog_row, dtb_row, name="bg_fwd")
    beta_r = _tokens_to_rows(bg[:, :HEADS])
    g_r = _tokens_to_rows(bg[:, HEADS:2 * HEADS])
    gc_r, tinv, u_a, w_a, p_a = _gdn_prep(q_a, k_a, v_a, beta_r, g_r, name="gdn_prep")
    o_a, s_a = _gdn_scan(q_a, k_a, u_a, w_a, p_a, gc_r, name="gdn_scan")
    o_b, s_b = _ret_fwd(proj, cos2, sin2, lg_tab, o_rq, o_rk, o_rv, name="ret_fwd")
    mixed = _merge_fwd(o_a, o_b, proj, wa_row, ret_norm_w, o_az, o_rg, o_ga, o_gb, name="merge_fwd")
    h1 = _mm(mixed, w_o, res=xs, tm=512, tn=d, tk=d, name="mm_out")
    hn = _rms_fwd(h1, norm2_w, name="rms2_fwd")
    gt = _mm(hn, w_g, tm=512, tn=fq, tk=d, name="mm_gate")
    up = _mm(hn, w_u, tm=512, tn=fq, tk=d, name="mm_up")
    act = _swiglu_fwd(gt, up, name="swiglu_fwd")
    h2 = _mm(act, w_d, res=h1, tm=512, tn=d, tk=fq, name="mm_down")
    loss_row, dh2, dh2b, d_nf = _loss_head(h2, tgt, norm_f_w.reshape(1, d), name="loss_head")

    dact = _mm(dh2b, w_d, tb=True, tm=512, tn=fq, tk=d, name="mm_dact")
    g_down = _mm(act, dh2b, ta=True, out_dtype=BF16, tm=fq, tn=d, tk=1024, name="mm_dw_down")
    dgt, dup = _swiglu_bwd(gt, up, dact, name="swiglu_bwd")
    dhn = _mm(dgt, w_g, tb=True, tm=512, tn=d, tk=fq, name="mm_dhn_gate")
    dhn = _mm(dup, w_u, tb=True, res=dhn, tm=512, tn=d, tk=fq, name="mm_dhn_up")
    g_gate = _mm(hn, dgt, ta=True, out_dtype=BF16, tm=1024, tn=fq, tk=1024, name="mm_dw_gate")
    g_up = _mm(hn, dup, ta=True, out_dtype=BF16, tm=1024, tn=fq, tk=1024, name="mm_dw_up")
    dh1, dh1b, d_n2 = _rms_bwd(dhn, h1, norm2_w, dh2, name="rms2_bwd")
    dmix = _mm(dh1b, w_o, tb=True, tm=512, tn=d, tk=d, name="mm_dmix")
    g_out = _mm(mixed, dh1b, ta=True, out_dtype=BF16, tm=1024, tn=d, tk=1024, name="mm_dw_out")
    do_a, do_b, d_az, d_rg, d_ga, d_gb, d_wa, d_wb = _merge_bwd(
        dmix, o_a, o_b, proj, wa_row, ret_norm_w, o_az, o_rg, o_ga, o_gb, name="merge_bwd")
    d_rq, d_rk, d_rv = _ret_bwd(proj, cos2, sin2, lg_tab, s_b, do_b, o_rq, o_rk, o_rv, name="ret_bwd")
    dq_a, dk_a, dv_a, dbeta_r, dg_r = _gdn_bwd(q_a, k_a, v_a, beta_r, gc_r, tinv, u_a, w_a, s_a, do_a, name="gdn_bwd")
    dc = _conv_bwd_pre(proj, conv_full, dq_a, dk_a, dv_a, name="conv_bwd_pre")
    d_aqkv, d_cw = _conv_bwd(proj, dc, conv_full, name="conv_bwd")
    dbg = jnp.pad(jnp.concatenate([_rows_to_tokens(dbeta_r), _rows_to_tokens(dg_r)], axis=1),
                  ((0, 0), (0, LANES - 2 * HEADS)))
    dprojs, d_alog, d_dtb = _bg_bwd(projs, dbg, alog_row, dtb_row, name="bg_bwd")
    dproj = jnp.concatenate([d_aqkv, d_az, d_rq, d_rk, d_rv, d_rg, d_ga, d_gb], axis=1)
    du = _mm(dproj, w_main, tb=True, tm=1024, tn=d, tk=1024, name="mm_du")
    du = _mm(dprojs, w_small, tb=True, res=du, tm=1024, tn=d, tk=LANES, name="mm_du_small")
    g_main = _mm(u1, dproj, ta=True, out_dtype=BF16, tm=1024, tn=2048, tk=1024, name="mm_dw_in")
    g_small = _mm(u1, dprojs, ta=True, out_dtype=BF16, tm=1024, tn=LANES, tk=1024, name="mm_dw_in_small")
    dx, _, d_n1 = _rms_bwd(du, xs, norm1_w, dh1, name="rms1_bwd")

    pieces, off = [], 0
    for s in range(N_CHIPS):
        if s == small_shard:
            a = g_main[:, off:off + lo_small]
            b = g_main[:, off + lo_small:off + lo_small + ncs - hi_small]
            pieces.append(jnp.concatenate([a, g_small[:, :n_small], b], axis=1))
            off += ncs - n_small
        else:
            pieces.append(g_main[:, off:off + ncs])
            off += ncs
    col_split = lambda g, wd: jnp.stack([g[:, i * wd:(i + 1) * wd] for i in range(N_CHIPS)])
    halves = lambda g: g.reshape(N_CHIPS, 2, g.shape[1] // 2, g.shape[2])
    gs = [halves(jnp.stack(pieces)), halves(g_out.reshape(N_CHIPS, d // N_CHIPS, d)),
          halves(col_split(g_gate, f // N_CHIPS)), halves(col_split(g_up, f // N_CHIPS)),
          halves(g_down.reshape(N_CHIPS, f // N_CHIPS, d))]
    names = ["w_in", "w_out", "w_gate", "w_up", "w_down"]
    rsib = _pair_exchange(gs, name="grad_pair_exchange")
    c_arr = jnp.reshape(c_idx, (1,)).astype(jnp.int32)
    ps = [_pair_add(g, r, c_arr, name=f"grad_pair_add_{nm}") for g, r, nm in zip(gs, rsib, names)]
    qs = _chip_scatter(ps, name="grad_chip_scatter")
    qs = [lax.dynamic_update_slice(q, lax.dynamic_slice(p, (s_idx, 0, 0), (1,) + p.shape[1:]), (s_idx, 0, 0))
          for q, p in zip(qs, ps)]
    hs = [_chip_sum(q, name=f"grad_chip_sum_{nm}") for q, nm in zip(qs, names)]
    theirs = _pair_share(hs, name="grad_pair_share")
    big_w = [w_in[0], w_out[0], w_gate[0], w_up[0], w_down[0]]
    big_m = [m_w_in[0], m_w_out[0], m_w_gate[0], m_w_up[0], m_w_down[0]]
    big_v = [v_w_in[0], v_w_out[0], v_w_gate[0], v_w_up[0], v_w_down[0]]
    big = {}
    for nm, mine, other, w_, m_, v_ in zip(names, hs, theirs, big_w, big_m, big_v):
        g = jnp.where(c_idx == 0, jnp.concatenate([mine, other]), jnp.concatenate([other, mine]))
        dl, nm_, nv_ = _adamw(w_, g, m_, v_, name=f"adamw_{nm}")
        big[nm] = tuple(a[None] for a in (g, dl, nm_, nv_))

    d_wa_h = jnp.sum(d_wa.reshape(HEADS, DV), axis=0, keepdims=True)
    small = [d_n1, d_alog[:, HEADS:2 * HEADS], d_dtb[:, HEADS:2 * HEADS], d_wa_h, d_wb, d_n2, d_nf,
             d_cw[:CONV_W].reshape(1, -1)]
    sizes = [a.shape[1] for a in small]
    packed = jnp.concatenate(small, axis=1)
    n_pack = packed.shape[1]
    n_rows = -(-n_pack // LANES)
    n_rows = -(-n_rows // 8) * 8
    packed = jnp.pad(packed, ((0, 0), (0, n_rows * LANES - n_pack))).reshape(n_rows, LANES)
    red = _allreduce_small(packed, name="allreduce_small").reshape(1, -1)
    offs = np.cumsum([0] + sizes)
    g_n1, g_alog, g_dtb, g_wa, g_wb, g_n2, g_nf, g_cw = [red[:, offs[i]:offs[i + 1]] for i in range(len(sizes))]
    ncw = conv_w.shape[2]
    g_cw = lax.dynamic_slice(g_cw.reshape(CONV_W, -1), (0, s_idx * ncw), (CONV_W, ncw))

    def small_update(w_, g, m_, v_, nm):
        shape = w_.shape
        pad = (-w_.size) % LANES
        to2 = lambda a: jnp.pad(a.reshape(1, -1), ((0, 0), (0, pad)))
        outs = _adamw(to2(w_), to2(g), to2(m_), to2(v_), name=f"adamw_{nm}")
        return (g.reshape(shape),) + tuple(a[:, :w_.size].reshape(shape) for a in outs)

    res = {
        "norm1_w": small_update(norm1_w, g_n1, m_norm1_w, v_norm1_w, "norm1_w"),
        "w_in": big["w_in"],
        "conv_w": small_update(conv_w, g_cw, m_conv_w, v_conv_w, "conv_w"),
        "a_log": small_update(a_log, g_alog, m_a_log, v_a_log, "a_log"),
        "dt_bias": small_update(dt_bias, g_dtb, m_dt_bias, v_dt_bias, "dt_bias"),
        "gdn_norm_w": small_update(gdn_norm_w, g_wa, m_gdn_norm_w, v_gdn_norm_w, "gdn_norm_w"),
        "ret_norm_w": small_update(ret_norm_w, g_wb, m_ret_norm_w, v_ret_norm_w, "ret_norm_w"),
        "w_out": big["w_out"],
        "norm2_w": small_update(norm2_w, g_n2, m_norm2_w, v_norm2_w, "norm2_w"),
        "w_gate": big["w_gate"],
        "w_up": big["w_up"],
        "w_down": big["w_down"],
        "norm_f_w": small_update(norm_f_w, g_nf, m_norm_f_w, v_norm_f_w, "norm_f_w"),
    }
    order = ["norm1_w", "w_in", "conv_w", "a_log", "dt_bias", "gdn_norm_w", "ret_norm_w", "w_out", "norm2_w",
             "w_gate", "w_up", "w_down", "norm_f_w"]
    loss = lax.psum(loss_row[0, 0], ("x", "y", "c"))
    return (loss, dx[None], *[res[n][0] for n in order], *[res[n][1] for n in order],
            *[res[n][2] for n in order], *[res[n][3] for n in order])
```

```python
import functools

import jax
import jax.numpy as jnp
import numpy as np
from jax import lax
from jax.experimental import pallas as pl
from jax.experimental.pallas import tpu as pltpu

F32 = jnp.float32
BF16 = jnp.bfloat16
MESH = pl.DeviceIdType.MESH

HEADS = 8
DK = 128
DV = 256
CHUNK = 64
CONV_W = 4
EPS = 1e-6
ROPE_BASE = 10000.0
ADAM_LR, ADAM_B1, ADAM_B2, ADAM_EPS, ADAM_WD, ADAM_STEP = 0.001, 0.9, 0.999, 1e-08, 0.01, 10
N_CHIPS = 4
N_DEV = 8
LANES = 128
HALO = 8
VMEM_LIMIT = 56 * 1024 * 1024
HB = 8


def _pick(n, cands):
    for c in cands:
        if n % c == 0:
            return c
    raise ValueError(f"no tile for {n} in {cands}")


def _params(sem=None):
    return pltpu.CompilerParams(dimension_semantics=sem, vmem_limit_bytes=VMEM_LIMIT)


def _dot(a, b):
    return jnp.dot(a.astype(BF16), b.astype(BF16), preferred_element_type=F32)


def _dot_nt(a, b):
    return lax.dot_general(a.astype(BF16), b.astype(BF16), (((1,), (1,)), ((), ())), preferred_element_type=F32)


def _dot_tn(a, b):
    return lax.dot_general(a.astype(BF16), b.astype(BF16), (((0,), (0,)), ((), ())), preferred_element_type=F32)


def _sigmoid(x):
    return 1.0 / (1.0 + jnp.exp(-x))


def _iota2(n):
    return lax.broadcasted_iota(jnp.int32, (n, n), 0), lax.broadcasted_iota(jnp.int32, (n, n), 1)


def _row_to_col(row):
    n = row.shape[1]
    r, c = _iota2(n)
    return jnp.sum(jnp.where(r == c, jnp.broadcast_to(row, (n, n)), 0.0), axis=1, keepdims=True)


def _col_to_row(col):
    n = col.shape[0]
    r, c = _iota2(n)
    return jnp.sum(jnp.where(r == c, jnp.broadcast_to(col, (n, n)), 0.0), axis=0, keepdims=True)


def _mm(a, b, *, name, tm, tn, tk, ta=False, tb=False, out_dtype=F32, res=None):
    m, k = (a.shape[1], a.shape[0]) if ta else a.shape
    n = b.shape[0] if tb else b.shape[1]
    tm, tn, tk = min(tm, m), min(tn, n), min(tk, k)
    assert m % tm == 0 and n % tn == 0 and k % tk == 0, (name, m, n, k)
    nk = k // tk
    dn = (((0 if ta else 1,), (1 if tb else 0,)), ((), ()))

    def body(*refs):
        a_ref, b_ref = refs[0], refs[1]
        r_ref = refs[2] if res is not None else None
        o_ref = refs[3] if res is not None else refs[2]

        def finish(r):
            if res is not None:
                r = r + r_ref[...]
            o_ref[...] = r.astype(out_dtype)

        part = lax.dot_general(a_ref[...], b_ref[...], dn, preferred_element_type=F32)
        if nk == 1:
            finish(part)
            return
        acc = refs[-1]
        kk = pl.program_id(2)

        @pl.when(kk == 0)
        def _():
            acc[...] = part

        @pl.when((kk > 0) & (kk < nk - 1))
        def _():
            acc[...] += part

        @pl.when(kk == nk - 1)
        def _():
            finish(acc[...] + part)

    a_spec = pl.BlockSpec((tk, tm), lambda j, i, kk: (kk, i)) if ta else pl.BlockSpec((tm, tk), lambda j, i, kk: (i, kk))
    b_spec = pl.BlockSpec((tn, tk), lambda j, i, kk: (j, kk)) if tb else pl.BlockSpec((tk, tn), lambda j, i, kk: (kk, j))
    o_spec = pl.BlockSpec((tm, tn), lambda j, i, kk: (i, j))
    in_specs, args = [a_spec, b_spec], [a, b]
    if res is not None:
        in_specs.append(o_spec)
        args.append(res)
    return pl.pallas_call(
        body, name=name, grid=(n // tn, m // tm, nk),
        in_specs=in_specs, out_specs=o_spec,
        out_shape=jax.ShapeDtypeStruct((m, n), out_dtype),
        scratch_shapes=[pltpu.VMEM((tm, tn), F32)] if nk > 1 else [],
        compiler_params=_params(("parallel", "parallel", "arbitrary")),
    )(*args)


def _rms_fwd(x, w, *, name):
    t, d = x.shape
    tt = _pick(t, (512, 256))

    def body(x_ref, w_ref, o_ref):
        xv = x_ref[...]
        r = lax.rsqrt(jnp.mean(xv * xv, axis=-1, keepdims=True) + EPS)
        o_ref[...] = (xv * r * w_ref[...]).astype(BF16)

    return pl.pallas_call(
        body, name=name, grid=(t // tt,),
        in_specs=[pl.BlockSpec((tt, d), lambda i: (i, 0)), pl.BlockSpec((1, d), lambda i: (0, 0))],
        out_specs=pl.BlockSpec((tt, d), lambda i: (i, 0)),
        out_shape=jax.ShapeDtypeStruct((t, d), BF16),
        compiler_params=_params(("parallel",)),
    )(x, w)


def _rms_bwd(dn, x, w, dres, *, name):
    t, d = x.shape
    tt = _pick(t, (256,))

    def body(dn_ref, x_ref, w_ref, dres_ref, dx_ref, dxb_ref, dw_ref):
        xv, g = x_ref[...], dn_ref[...]
        r = lax.rsqrt(jnp.mean(xv * xv, axis=-1, keepdims=True) + EPS)
        xh = xv * r
        gw = g * w_ref[...]
        dx = dres_ref[...] + r * (gw - xh * jnp.mean(gw * xh, axis=-1, keepdims=True))
        dx_ref[...] = dx
        dxb_ref[...] = dx.astype(BF16)

        @pl.when(pl.program_id(0) == 0)
        def _():
            dw_ref[...] = jnp.zeros_like(dw_ref)

        dw_ref[...] += jnp.sum(g * xh, axis=0, keepdims=True)

    blk = pl.BlockSpec((tt, d), lambda i: (i, 0))
    row = pl.BlockSpec((1, d), lambda i: (0, 0))
    return pl.pallas_call(
        body, name=name, grid=(t // tt,),
        in_specs=[blk, blk, row, blk], out_specs=[blk, blk, row],
        out_shape=[jax.ShapeDtypeStruct((t, d), F32), jax.ShapeDtypeStruct((t, d), BF16),
                   jax.ShapeDtypeStruct((1, d), F32)],
        compiler_params=_params(("arbitrary",)),
    )(dn, x, w, dres)


def _loss_head(h2, tgt, wf, *, name):
    t, d = h2.shape
    tt = _pick(t, (256,))

    def body(x_ref, t_ref, w_ref, loss_ref, dx_ref, dxb_ref, dw_ref):
        xv = x_ref[...]
        r = lax.rsqrt(jnp.mean(xv * xv, axis=-1, keepdims=True) + EPS)
        xh = xv * r
        err = xh * w_ref[...] - t_ref[...]
        lpart = 0.5 * jnp.sum(jnp.mean(err * err, axis=-1, keepdims=True), axis=0, keepdims=True)
        dy = err * (1.0 / d)
        gw = dy * w_ref[...]
        dx = r * (gw - xh * jnp.mean(gw * xh, axis=-1, keepdims=True))
        dx_ref[...] = dx
        dxb_ref[...] = dx.astype(BF16)

        @pl.when(pl.program_id(0) == 0)
        def _():
            dw_ref[...] = jnp.zeros_like(dw_ref)
            loss_ref[...] = jnp.zeros_like(loss_ref)

        dw_ref[...] += jnp.sum(dy * xh, axis=0, keepdims=True)
        loss_ref[...] += jnp.broadcast_to(lpart, loss_ref.shape)

    blk = pl.BlockSpec((tt, d), lambda i: (i, 0))
    row = pl.BlockSpec((1, d), lambda i: (0, 0))
    lrow = pl.BlockSpec((1, LANES), lambda i: (0, 0))
    return pl.pallas_call(
        body, name=name, grid=(t // tt,),
        in_specs=[blk, blk, row], out_specs=[lrow, blk, blk, row],
        out_shape=[jax.ShapeDtypeStruct((1, LANES), F32), jax.ShapeDtypeStruct((t, d), F32),
                   jax.ShapeDtypeStruct((t, d), BF16), jax.ShapeDtypeStruct((1, d), F32)],
        compiler_params=_params(("arbitrary",)),
    )(h2, tgt, wf)


def _ffn_in(hn, w_g, w_u, *, name, tm, tn):
    t, d = hn.shape
    f = w_g.shape[1]
    tm = min(tm, t)

    def body(a_ref, g_ref, u_ref, gt_ref, up_ref, act_ref):
        a = a_ref[...]
        g = jnp.dot(a, g_ref[...], preferred_element_type=F32)
        u = jnp.dot(a, u_ref[...], preferred_element_type=F32)
        gt_ref[...] = g.astype(BF16)
        up_ref[...] = u.astype(BF16)
        act_ref[...] = (g * _sigmoid(g) * u).astype(BF16)

    wblk = pl.BlockSpec((d, tn), lambda j, i: (0, j))
    oblk = pl.BlockSpec((tm, tn), lambda j, i: (i, j))
    return pl.pallas_call(
        body, name=name, grid=(f // tn, t // tm),
        in_specs=[pl.BlockSpec((tm, d), lambda j, i: (i, 0)), wblk, wblk], out_specs=[oblk] * 3,
        out_shape=[jax.ShapeDtypeStruct((t, f), BF16)] * 3,
        compiler_params=_params(("parallel", "parallel")),
    )(hn, w_g, w_u)


def _ffn_back(dh2b, w_d, gt, up, *, name, tm, tn):
    t, d = dh2b.shape
    f = w_d.shape[0]
    tm = min(tm, t)

    def body(a_ref, w_ref, g_ref, u_ref, dg_ref, du_ref):
        da = lax.dot_general(a_ref[...], w_ref[...], (((1,), (1,)), ((), ())), preferred_element_type=F32)
        g = g_ref[...].astype(F32)
        sg = _sigmoid(g)
        dg_ref[...] = (da * u_ref[...].astype(F32) * sg * (1.0 + g * (1.0 - sg))).astype(BF16)
        du_ref[...] = (da * g * sg).astype(BF16)

    oblk = pl.BlockSpec((tm, tn), lambda j, i: (i, j))
    return pl.pallas_call(
        body, name=name, grid=(f // tn, t // tm),
        in_specs=[pl.BlockSpec((tm, d), lambda j, i: (i, 0)), pl.BlockSpec((tn, d), lambda j, i: (j, 0)), oblk, oblk],
        out_specs=[oblk] * 2,
        out_shape=[jax.ShapeDtypeStruct((t, f), BF16)] * 2,
        compiler_params=_params(("parallel", "parallel")),
    )(dh2b, w_d, gt, up)


def _conv_cols(xs, cw_ref, cb, tt):
    cols = slice(cb * LANES, (cb + 1) * LANES)
    base = HALO - (CONV_W - 1)
    acc = xs[base:base + tt, cols] * cw_ref[0:1, cols]
    for j in range(1, CONV_W):
        acc = acc + xs[base + j:base + j + tt, cols] * cw_ref[j:j + 1, cols]
    return acc


def _fill_halo(xs, x_ref, xp_ref, tt):
    first = pl.program_id(0) == 0
    xs[0:HALO, :] = jnp.where(first, 0.0, xp_ref[...])
    xs[HALO:HALO + tt, :] = x_ref[...]


def _conv_specs(tt, ch):
    cur = pl.BlockSpec((tt, ch), lambda i: (i, 0))
    prev = pl.BlockSpec((HALO, ch), lambda i: (jnp.maximum(i * (tt // HALO) - 1, 0), 0))
    return cur, prev


def _conv_fwd(proj, conv_w, *, name):
    t = proj.shape[0]
    ch = conv_w.shape[1]
    nqk = HEADS * DK
    tt = _pick(t, (256,))
    cur, prev = _conv_specs(tt, ch)

    def body(x_ref, xp_ref, cw_ref, q_ref, k_ref, v_ref, xs):
        _fill_halo(xs, x_ref, xp_ref, tt)
        for cb in range(ch // LANES):
            c = _conv_cols(xs, cw_ref, cb, tt)
            s = c * _sigmoid(c)
            if cb < 2 * HEADS:
                s = s * lax.rsqrt(jnp.sum(s * s, axis=-1, keepdims=True) + EPS)
                if cb < HEADS:
                    q_ref[:, cb * LANES:(cb + 1) * LANES] = s * (DK ** -0.5)
                else:
                    k_ref[:, (cb - HEADS) * LANES:(cb - HEADS + 1) * LANES] = s
            else:
                v_ref[:, (cb - 2 * HEADS) * LANES:(cb - 2 * HEADS + 1) * LANES] = s

    return pl.pallas_call(
        body, name=name, grid=(t // tt,),
        in_specs=[cur, prev, pl.BlockSpec((CONV_W, ch), lambda i: (0, 0))],
        out_specs=[pl.BlockSpec((tt, nqk), lambda i: (i, 0)), pl.BlockSpec((tt, nqk), lambda i: (i, 0)),
                   pl.BlockSpec((tt, ch - 2 * nqk), lambda i: (i, 0))],
        out_shape=[jax.ShapeDtypeStruct((t, nqk), F32), jax.ShapeDtypeStruct((t, nqk), F32),
                   jax.ShapeDtypeStruct((t, ch - 2 * nqk), F32)],
        scratch_shapes=[pltpu.VMEM((HALO + tt, ch), F32)],
        compiler_params=_params(("arbitrary",)),
    )(proj, proj, conv_w)


def _conv_bwd_pre(proj, conv_w, dq, dk, dv, *, name):
    t = proj.shape[0]
    ch = conv_w.shape[1]
    nqk = HEADS * DK
    tt = _pick(t, (256,))
    cur, prev = _conv_specs(tt, ch)

    def body(x_ref, xp_ref, cw_ref, dq_ref, dk_ref, dv_ref, dc_ref, xs):
        _fill_halo(xs, x_ref, xp_ref, tt)
        for cb in range(ch // LANES):
            c = _conv_cols(xs, cw_ref, cb, tt)
            sg = _sigmoid(c)
            s = c * sg
            if cb < 2 * HEADS:
                if cb < HEADS:
                    d, scale = dq_ref[:, cb * LANES:(cb + 1) * LANES], DK ** -0.5
                else:
                    d, scale = dk_ref[:, (cb - HEADS) * LANES:(cb - HEADS + 1) * LANES], 1.0
                rinv = lax.rsqrt(jnp.sum(s * s, axis=-1, keepdims=True) + EPS)
                ds = scale * rinv * (d - s * (rinv * rinv) * jnp.sum(d * s, axis=-1, keepdims=True))
            else:
                ds = dv_ref[:, (cb - 2 * HEADS) * LANES:(cb - 2 * HEADS + 1) * LANES]
            dc_ref[:, cb * LANES:(cb + 1) * LANES] = ds * sg * (1.0 + c * (1.0 - sg))

    return pl.pallas_call(
        body, name=name, grid=(t // tt,),
        in_specs=[cur, prev, pl.BlockSpec((CONV_W, ch), lambda i: (0, 0)),
                  pl.BlockSpec((tt, nqk), lambda i: (i, 0)), pl.BlockSpec((tt, nqk), lambda i: (i, 0)),
                  pl.BlockSpec((tt, ch - 2 * nqk), lambda i: (i, 0))],
        out_specs=pl.BlockSpec((tt, ch), lambda i: (i, 0)),
        out_shape=jax.ShapeDtypeStruct((t, ch), F32),
        scratch_shapes=[pltpu.VMEM((HALO + tt, ch), F32)],
        compiler_params=_params(("arbitrary",)),
    )(proj, proj, conv_w, dq, dk, dv)


def _conv_bwd(proj, dc, conv_w, *, name):
    t = proj.shape[0]
    ch = conv_w.shape[1]
    tt = _pick(t, (256,))
    nt = t // tt
    cur, prev = _conv_specs(tt, ch)
    nxt = pl.BlockSpec((HALO, ch), lambda i: (jnp.minimum((i + 1) * (tt // HALO), t // HALO - 1), 0))

    def body(x_ref, xp_ref, d_ref, dn_ref, cw_ref, dx_ref, dw_ref, xs, ds):
        _fill_halo(xs, x_ref, xp_ref, tt)
        last = pl.program_id(0) == nt - 1
        ds[0:tt, :] = d_ref[...]
        ds[tt:tt + HALO, :] = jnp.where(last, 0.0, dn_ref[...])

        @pl.when(pl.program_id(0) == 0)
        def _():
            dw_ref[...] = jnp.zeros_like(dw_ref)

        base = HALO - (CONV_W - 1)
        for cb in range(ch // LANES):
            cols = slice(cb * LANES, (cb + 1) * LANES)
            d0 = ds[0:tt, cols]
            acc = None
            for j in range(CONV_W):
                sh = CONV_W - 1 - j
                term = ds[sh:sh + tt, cols] * cw_ref[j:j + 1, cols]
                acc = term if acc is None else acc + term
                dw_ref[j:j + 1, cols] += jnp.sum(d0 * xs[base + j:base + j + tt, cols], axis=0, keepdims=True)
            dx_ref[:, cols] = acc.astype(BF16)

    return pl.pallas_call(
        body, name=name, grid=(nt,),
        in_specs=[cur, prev, cur, nxt, pl.BlockSpec((CONV_W, ch), lambda i: (0, 0))],
        out_specs=[pl.BlockSpec((tt, ch), lambda i: (i, 0)), pl.BlockSpec((HALO, ch), lambda i: (0, 0))],
        out_shape=[jax.ShapeDtypeStruct((t, ch), BF16), jax.ShapeDtypeStruct((HALO, ch), F32)],
        scratch_shapes=[pltpu.VMEM((HALO + tt, ch), F32), pltpu.VMEM((tt + HALO, ch), F32)],
        compiler_params=_params(("arbitrary",)),
    )(proj, proj, dc, dc, conv_w)


def _bg_fwd(projs, alog_row, dtb_row, *, name):
    t = projs.shape[0]
    tt = _pick(t, (512, 256))

    def body(p_ref, al_ref, db_ref, o_ref):
        p = p_ref[...]
        lane = lax.broadcasted_iota(jnp.int32, p.shape, 1)
        z = p + db_ref[...]
        sp = jnp.maximum(z, 0.0) + jnp.log1p(jnp.exp(-jnp.abs(z)))
        g = -jnp.exp(al_ref[...]) * sp
        o_ref[...] = jnp.where(lane < HEADS, _sigmoid(p), jnp.where(lane < 2 * HEADS, g, 0.0))

    blk = pl.BlockSpec((tt, LANES), lambda i: (i, 0))
    row = pl.BlockSpec((1, LANES), lambda i: (0, 0))
    return pl.pallas_call(
        body, name=name, grid=(t // tt,), in_specs=[blk, row, row], out_specs=blk,
        out_shape=jax.ShapeDtypeStruct((t, LANES), F32), compiler_params=_params(("parallel",)),
    )(projs, alog_row, dtb_row)


def _bg_bwd(projs, dbg, alog_row, dtb_row, *, name):
    t = projs.shape[0]
    tt = _pick(t, (512, 256))

    def body(p_ref, d_ref, al_ref, db_ref, o_ref, dal_ref, ddb_ref):
        p, d = p_ref[...], d_ref[...]
        lane = lax.broadcasted_iota(jnp.int32, p.shape, 1)
        isg = (lane >= HEADS) & (lane < 2 * HEADS)
        be = _sigmoid(p)
        z = p + db_ref[...]
        sp = jnp.maximum(z, 0.0) + jnp.log1p(jnp.exp(-jnp.abs(z)))
        ea = jnp.exp(al_ref[...])
        d_aa = jnp.where(isg, d * (-ea) * _sigmoid(z), 0.0)
        o_ref[...] = jnp.where(lane < HEADS, d * be * (1.0 - be), d_aa).astype(BF16)

        @pl.when(pl.program_id(0) == 0)
        def _():
            dal_ref[...] = jnp.zeros_like(dal_ref)
            ddb_ref[...] = jnp.zeros_like(ddb_ref)

        dal_ref[...] += jnp.sum(jnp.where(isg, d * (-ea) * sp, 0.0), axis=0, keepdims=True)
        ddb_ref[...] += jnp.sum(d_aa, axis=0, keepdims=True)

    blk = pl.BlockSpec((tt, LANES), lambda i: (i, 0))
    row = pl.BlockSpec((1, LANES), lambda i: (0, 0))
    return pl.pallas_call(
        body, name=name, grid=(t // tt,), in_specs=[blk, blk, row, row], out_specs=[blk, row, row],
        out_shape=[jax.ShapeDtypeStruct((t, LANES), BF16), jax.ShapeDtypeStruct((1, LANES), F32),
                   jax.ShapeDtypeStruct((1, LANES), F32)],
        compiler_params=_params(("arbitrary",)),
    )(projs, dbg, alog_row, dtb_row)


def _hn_specs(n_chunks, rev=False):
    def nn(n):
        return n_chunks - 1 - n if rev else n
    tok = lambda w: pl.BlockSpec((CHUNK, HB * w), lambda g, n: (nn(n), g))
    per = lambda a, b: pl.BlockSpec((HB, None, a, b), lambda g, n: (g, nn(n), 0, 0))
    return tok, per


def _hcols(hh, w):
    return slice(hh * w, (hh + 1) * w)


def _decay(gc_col, gc_row):
    r, c = _iota2(CHUNK)
    d = jnp.exp(jnp.minimum(gc_col - gc_row, 0.0))
    return jnp.where(r >= c, d, 0.0), jnp.where(r > c, d, 0.0)


def _gdn_prep(q, k, v, beta_r, g_r, *, name):
    t = q.shape[0]
    n_chunks = t // CHUNK
    tok, per = _hn_specs(n_chunks)

    def body(q_ref, k_ref, v_ref, b_ref, g_ref, gc_ref, ti_ref, u_ref, w_ref, p_ref):
        r, c = _iota2(CHUNK)
        hs = range(HB)
        kk = [k_ref[:, _hcols(h, DK)] for h in hs]
        gc_col = [jnp.sum(jnp.where(c <= r, jnp.broadcast_to(g_ref[h], (CHUNK, CHUNK)), 0.0), axis=1, keepdims=True)
                  for h in hs]
        gc_row = [_col_to_row(gc_col[h]) for h in hs]
        beta_col = [_row_to_col(b_ref[h]) for h in hs]
        dec = [_decay(gc_col[h], gc_row[h]) for h in hs]
        kb = [kk[h] * beta_col[h] for h in hs]
        for h in hs:
            gc_ref[h] = gc_row[h]
            p_ref[h] = _dot_nt(q_ref[:, _hcols(h, DK)], kk[h]) * dec[h][0]
        pw = [_dot_nt(kb[h], kk[h]) * dec[h][1] for h in hs]
        y = [-pw[h] for h in hs]
        for _ in range(5):
            pw = [_dot(pw[h], pw[h]) for h in hs]
            yp = [_dot(y[h], pw[h]) for h in hs]
            y = [y[h] + pw[h] + yp[h] for h in hs]
        vb = [v_ref[:, _hcols(h, DV)] * beta_col[h] for h in hs]
        kbe = [kb[h] * jnp.exp(gc_col[h]) for h in hs]
        yv = [_dot(y[h], vb[h]) for h in hs]
        yk = [_dot(y[h], kbe[h]) for h in hs]
        for h in hs:
            ti_ref[h] = y[h]
            u_ref[:, _hcols(h, DV)] = vb[h] + yv[h]
            w_ref[:, _hcols(h, DK)] = kbe[h] + yk[h]

    return pl.pallas_call(
        body, name=name, grid=(HEADS // HB, n_chunks),
        in_specs=[tok(DK), tok(DK), tok(DV), per(1, CHUNK), per(1, CHUNK)],
        out_specs=[per(1, CHUNK), per(CHUNK, CHUNK), tok(DV), tok(DK), per(CHUNK, CHUNK)],
        out_shape=[jax.ShapeDtypeStruct((HEADS, n_chunks, 1, CHUNK), F32),
                   jax.ShapeDtypeStruct((HEADS, n_chunks, CHUNK, CHUNK), F32),
                   jax.ShapeDtypeStruct((t, HEADS * DV), F32), jax.ShapeDtypeStruct((t, HEADS * DK), F32),
                   jax.ShapeDtypeStruct((HEADS, n_chunks, CHUNK, CHUNK), F32)],
        compiler_params=_params(("parallel", "parallel")),
    )(q, k, v, beta_r, g_r)


def _gdn_scan(q, k, u, w, p, gc_r, *, name):
    t = q.shape[0]
    n_chunks = t // CHUNK
    tok, per = _hn_specs(n_chunks)

    def body(q_ref, k_ref, u_ref, w_ref, p_ref, gc_ref, o_ref, s_ref, st):
        @pl.when(pl.program_id(1) == 0)
        def _():
            st[...] = jnp.zeros_like(st)

        hs = range(HB)
        s = [st[h] for h in hs]
        gc_row = [gc_ref[h] for h in hs]
        gc_col = [_row_to_col(gc_row[h]) for h in hs]
        glast = [gc_row[h][:, CHUNK - 1:CHUNK] for h in hs]
        for h in hs:
            s_ref[h] = s[h]
        ws = [_dot(w_ref[:, _hcols(h, DK)], s[h]) for h in hs]
        qs = [_dot(q_ref[:, _hcols(h, DK)] * jnp.exp(gc_col[h]), s[h]) for h in hs]
        vn = [u_ref[:, _hcols(h, DV)] - ws[h] for h in hs]
        pv = [_dot(p_ref[h], vn[h]) for h in hs]
        kv = [_dot_tn(k_ref[:, _hcols(h, DK)] * jnp.exp(glast[h] - gc_col[h]), vn[h]) for h in hs]
        for h in hs:
            o_ref[:, _hcols(h, DV)] = qs[h] + pv[h]
            st[h] = s[h] * jnp.exp(glast[h]) + kv[h]

    return pl.pallas_call(
        body, name=name, grid=(HEADS // HB, n_chunks),
        in_specs=[tok(DK), tok(DK), tok(DV), tok(DK), per(CHUNK, CHUNK), per(1, CHUNK)],
        out_specs=[tok(DV), per(DK, DV)],
        out_shape=[jax.ShapeDtypeStruct((t, HEADS * DV), F32),
                   jax.ShapeDtypeStruct((HEADS, n_chunks, DK, DV), F32)],
        scratch_shapes=[pltpu.VMEM((HB, DK, DV), F32)],
        compiler_params=_params(("arbitrary", "arbitrary")),
    )(q, k, u, w, p, gc_r)


def _gdn_bwd(q, k, v, beta_r, gc_r, ti, u, w, s_all, do, *, name):
    t = q.shape[0]
    n_chunks = t // CHUNK
    tok, per = _hn_specs(n_chunks, rev=True)

    def body(q_ref, k_ref, v_ref, b_ref, gc_ref, ti_ref, u_ref, w_ref, s_ref, do_ref,
             dq_ref, dk_ref, dv_ref, db_ref, dg_ref, dst):
        @pl.when(pl.program_id(1) == 0)
        def _():
            dst[...] = jnp.zeros_like(dst)

        r, c = _iota2(CHUNK)
        rows = lax.broadcasted_iota(jnp.int32, (CHUNK, 1), 0)
        rsum = lambda a: jnp.sum(a, axis=1, keepdims=True)
        hs = range(HB)
        qq = [q_ref[:, _hcols(h, DK)] for h in hs]
        kk = [k_ref[:, _hcols(h, DK)] for h in hs]
        ww = [w_ref[:, _hcols(h, DK)] for h in hs]
        uu = [u_ref[:, _hcols(h, DV)] for h in hs]
        d_o = [do_ref[:, _hcols(h, DV)] for h in hs]
        s = [s_ref[h] for h in hs]
        d_s = [dst[h] for h in hs]
        gc_row = [gc_ref[h] for h in hs]
        gc_col = [_row_to_col(gc_row[h]) for h in hs]
        beta_col = [_row_to_col(b_ref[h]) for h in hs]
        eg = [jnp.exp(gc_col[h]) for h in hs]
        glast = [gc_row[h][:, CHUNK - 1:CHUNK] for h in hs]
        eglast = [jnp.exp(glast[h]) for h in hs]
        e2 = [jnp.exp(glast[h] - gc_col[h]) for h in hs]
        dec = [_decay(gc_col[h], gc_row[h]) for h in hs]
        kb = [kk[h] * beta_col[h] for h in hs]
        ke = [kk[h] * e2[h] for h in hs]
        qe = [qq[h] * eg[h] for h in hs]
        kkt = [_dot_nt(kb[h], kk[h]) for h in hs]
        qk = [_dot_nt(qq[h], kk[h]) for h in hs]
        ws = [_dot(ww[h], s[h]) for h in hs]
        a = [kkt[h] * dec[h][1] for h in hs]
        pp = [qk[h] * dec[h][0] for h in hs]
        vn = [uu[h] - ws[h] for h in hs]
        t1 = [_dot_tn(pp[h], d_o[h]) for h in hs]
        t2 = [_dot(ke[h], d_s[h]) for h in hs]
        dqe = [_dot_nt(d_o[h], s[h]) for h in hs]
        dke = [_dot_nt(vn[h], d_s[h]) for h in hs]
        dqk = [_dot_nt(d_o[h], vn[h]) * dec[h][0] for h in hs]
        dvn = [t1[h] + t2[h] for h in hs]
        t3 = [_dot_tn(qe[h], d_o[h]) for h in hs]
        t4 = [_dot_tn(ww[h], dvn[h]) for h in hs]
        dw = [-_dot_nt(dvn[h], s[h]) for h in hs]
        de2 = [rsum(dke[h] * ke[h]) for h in hs]
        dglast = [jnp.sum(de2[h], axis=0, keepdims=True)
                  + eglast[h] * jnp.sum(rsum(d_s[h] * s[h]), axis=0, keepdims=True) for h in hs]
        for h in hs:
            dst[h] = d_s[h] * eglast[h] + t3[h] - t4[h]
        yy = [ti_ref[h] for h in hs]
        t5 = [_dot_tn(yy[h], dvn[h]) for h in hs]
        t6 = [_dot_tn(yy[h], dw[h]) for h in hs]
        dvb = [dvn[h] + t5[h] for h in hs]
        dkbe = [dw[h] + t6[h] for h in hs]
        t7 = [_dot_nt(dvb[h], uu[h]) for h in hs]
        t8 = [_dot_nt(dkbe[h], ww[h]) for h in hs]
        d_a = [jnp.where(r > c, -(t7[h] + t8[h]), 0.0) for h in hs]
        dm = [d_a[h] * dec[h][1] for h in hs]
        t9 = [_dot(dm[h], kk[h]) for h in hs]
        t10 = [_dot(dqk[h], kk[h]) for h in hs]
        t11 = [_dot_tn(dqk[h], qq[h]) for h in hs]
        t12 = [_dot_tn(dm[h], kb[h]) for h in hs]
        for h in hs:
            dkb = t9[h] + dkbe[h] * eg[h]
            e_mat = d_a[h] * a[h] + dqk[h] * qk[h]
            dgc = (rsum(dqe[h] * qe[h]) - de2[h] + rsum(dkbe[h] * kb[h] * eg[h]) + rsum(e_mat)
                   - _row_to_col(jnp.sum(e_mat, axis=0, keepdims=True)))
            dgc = dgc + jnp.where(rows == CHUNK - 1, dglast[h], 0.0)
            dq_ref[:, _hcols(h, DK)] = dqe[h] * eg[h] + t10[h]
            dk_ref[:, _hcols(h, DK)] = t11[h] + dke[h] * e2[h] + t12[h] + dkb * beta_col[h]
            dv_ref[:, _hcols(h, DV)] = dvb[h] * beta_col[h]
            dbeta = rsum(dkb * kk[h]) + rsum(dvb[h] * v_ref[:, _hcols(h, DV)])
            db_ref[h] = _col_to_row(dbeta)
            dg_ref[h] = jnp.sum(jnp.where(r >= c, jnp.broadcast_to(dgc, (CHUNK, CHUNK)), 0.0), axis=0, keepdims=True)

    return pl.pallas_call(
        body, name=name, grid=(HEADS // HB, n_chunks),
        in_specs=[tok(DK), tok(DK), tok(DV), per(1, CHUNK), per(1, CHUNK), per(CHUNK, CHUNK), tok(DV), tok(DK),
                  per(DK, DV), tok(DV)],
        out_specs=[tok(DK), tok(DK), tok(DV), per(1, CHUNK), per(1, CHUNK)],
        out_shape=[jax.ShapeDtypeStruct((t, HEADS * DK), F32), jax.ShapeDtypeStruct((t, HEADS * DK), F32),
                   jax.ShapeDtypeStruct((t, HEADS * DV), F32),
                   jax.ShapeDtypeStruct((HEADS, n_chunks, 1, CHUNK), F32),
                   jax.ShapeDtypeStruct((HEADS, n_chunks, 1, CHUNK), F32)],
        scratch_shapes=[pltpu.VMEM((HB, DK, DV), F32)],
        compiler_params=_params(("arbitrary", "arbitrary")),
    )(q, k, v, beta_r, gc_r, ti, u, w, s_all, do)


def _rot(x, cos2, sin2):
    return x * cos2 + pltpu.roll(x, DK // 2, 1) * sin2


def _unrot(d, cos2, sin2):
    return d * cos2 + pltpu.roll(d * sin2, DK // 2, 1)


def _ret_consts(lg):
    r, c = _iota2(CHUNK)
    dm = jnp.where(r >= c, jnp.exp((r - c).astype(F32) * lg), 0.0)
    pos = lax.broadcasted_iota(jnp.int32, (CHUNK, 1), 0).astype(F32)
    return dm, jnp.exp((pos + 1.0) * lg), jnp.exp((CHUNK - 1.0 - pos) * lg), jnp.exp(CHUNK * lg)


def _ret_fwd(proj, cos2, sin2, lg_tab, col_q, col_k, col_v, *, name):
    t = proj.shape[0]
    n_chunks = t // CHUNK
    tok, per = _hn_specs(n_chunks)
    bq, bk, bv = col_q // (HB * DK), col_k // (HB * DK), col_v // (HB * DV)

    def body(q_ref, k_ref, v_ref, cos_ref, sin_ref, lg_ref, o_ref, s_ref, st):
        @pl.when(pl.program_id(1) == 0)
        def _():
            st[...] = jnp.zeros_like(st)

        cos2v, sin2v = cos_ref[...], sin_ref[...]
        hs = range(HB)
        s = [st[h] for h in hs]
        for h in hs:
            s_ref[h] = s[h]
        cst = [_ret_consts(lg_ref[h][:, 0:1]) for h in hs]
        qq = [_rot(q_ref[:, _hcols(h, DK)], cos2v, sin2v) for h in hs]
        kk = [_rot(k_ref[:, _hcols(h, DK)], cos2v, sin2v) * (DK ** -0.5) for h in hs]
        vv = [v_ref[:, _hcols(h, DV)] for h in hs]
        sc = [_dot_nt(qq[h], kk[h]) * cst[h][0] for h in hs]
        qs = [_dot(qq[h], s[h]) for h in hs]
        kv = [_dot_tn(kk[h] * cst[h][2], vv[h]) for h in hs]
        sv = [_dot(sc[h], vv[h]) for h in hs]
        for h in hs:
            o_ref[:, _hcols(h, DV)] = sv[h] + qs[h] * cst[h][1]
            st[h] = s[h] * cst[h][3] + kv[h]

    return pl.pallas_call(
        body, name=name, grid=(HEADS // HB, n_chunks),
        in_specs=[pl.BlockSpec((CHUNK, HB * DK), lambda g, n: (n, bq + g)),
                  pl.BlockSpec((CHUNK, HB * DK), lambda g, n: (n, bk + g)),
                  pl.BlockSpec((CHUNK, HB * DV), lambda g, n: (n, bv + g)),
                  pl.BlockSpec((CHUNK, DK), lambda g, n: (n, 0)), pl.BlockSpec((CHUNK, DK), lambda g, n: (n, 0)),
                  pl.BlockSpec((HB, 1, LANES), lambda g, n: (g, 0, 0))],
        out_specs=[tok(DV), per(DK, DV)],
        out_shape=[jax.ShapeDtypeStruct((t, HEADS * DV), F32),
                   jax.ShapeDtypeStruct((HEADS, n_chunks, DK, DV), F32)],
        scratch_shapes=[pltpu.VMEM((HB, DK, DV), F32)],
        compiler_params=_params(("arbitrary", "arbitrary")),
    )(proj, proj, proj, cos2, sin2, lg_tab)


def _ret_bwd(proj, cos2, sin2, lg_tab, s_all, do, col_q, col_k, col_v, *, name):
    t = proj.shape[0]
    n_chunks = t // CHUNK
    tok, per = _hn_specs(n_chunks, rev=True)
    bq, bk, bv = col_q // (HB * DK), col_k // (HB * DK), col_v // (HB * DV)
    rv = lambda n: n_chunks - 1 - n

    def body(q_ref, k_ref, v_ref, cos_ref, sin_ref, lg_ref, s_ref, do_ref, dq_ref, dk_ref, dv_ref, dst):
        @pl.when(pl.program_id(1) == 0)
        def _():
            dst[...] = jnp.zeros_like(dst)

        cos2v, sin2v = cos_ref[...], sin_ref[...]
        hs = range(HB)
        s = [s_ref[h] for h in hs]
        d_s = [dst[h] for h in hs]
        d_o = [do_ref[:, _hcols(h, DV)] for h in hs]
        cst = [_ret_consts(lg_ref[h][:, 0:1]) for h in hs]
        qq = [_rot(q_ref[:, _hcols(h, DK)], cos2v, sin2v) for h in hs]
        kk = [_rot(k_ref[:, _hcols(h, DK)], cos2v, sin2v) * (DK ** -0.5) for h in hs]
        vv = [v_ref[:, _hcols(h, DV)] for h in hs]
        dxo = [d_o[h] * cst[h][1] for h in hs]
        sc = [_dot_nt(qq[h], kk[h]) * cst[h][0] for h in hs]
        dsc = [_dot_nt(d_o[h], vv[h]) * cst[h][0] for h in hs]
        t1 = [_dot(kk[h] * cst[h][2], d_s[h]) for h in hs]
        t2 = [_dot_nt(dxo[h], s[h]) for h in hs]
        t3 = [_dot_nt(vv[h], d_s[h]) for h in hs]
        t4 = [_dot_tn(qq[h], dxo[h]) for h in hs]
        t5 = [_dot_tn(sc[h], d_o[h]) for h in hs]
        t6 = [_dot(dsc[h], kk[h]) for h in hs]
        t7 = [_dot_tn(dsc[h], qq[h]) for h in hs]
        for h in hs:
            dst[h] = d_s[h] * cst[h][3] + t4[h]
            dv_ref[:, _hcols(h, DV)] = (t5[h] + t1[h]).astype(BF16)
            dq_ref[:, _hcols(h, DK)] = _unrot(t6[h] + t2[h], cos2v, sin2v).astype(BF16)
            dk_ref[:, _hcols(h, DK)] = (_unrot(t7[h] + t3[h] * cst[h][2], cos2v, sin2v) * (DK ** -0.5)).astype(BF16)

    return pl.pallas_call(
        body, name=name, grid=(HEADS // HB, n_chunks),
        in_specs=[pl.BlockSpec((CHUNK, HB * DK), lambda g, n: (rv(n), bq + g)),
                  pl.BlockSpec((CHUNK, HB * DK), lambda g, n: (rv(n), bk + g)),
                  pl.BlockSpec((CHUNK, HB * DV), lambda g, n: (rv(n), bv + g)),
                  pl.BlockSpec((CHUNK, DK), lambda g, n: (rv(n), 0)),
                  pl.BlockSpec((CHUNK, DK), lambda g, n: (rv(n), 0)),
                  pl.BlockSpec((HB, 1, LANES), lambda g, n: (g, 0, 0)),
                  per(DK, DV), tok(DV)],
        out_specs=[tok(DK), tok(DK), tok(DV)],
        out_shape=[jax.ShapeDtypeStruct((t, HEADS * DK), BF16), jax.ShapeDtypeStruct((t, HEADS * DK), BF16),
                   jax.ShapeDtypeStruct((t, HEADS * DV), BF16)],
        scratch_shapes=[pltpu.VMEM((HB, DK, DV), F32)],
        compiler_params=_params(("arbitrary", "arbitrary")),
    )(proj, proj, proj, cos2, sin2, lg_tab, s_all, do)


def _merge_parts(oa, ob, z, rg, ga, gb):
    ra = lax.rsqrt(jnp.mean(oa * oa, axis=-1, keepdims=True) + EPS)
    xa = oa * ra
    mu = jnp.mean(ob, axis=-1, keepdims=True)
    cen = ob - mu
    rb = lax.rsqrt(jnp.mean(cen * cen, axis=-1, keepdims=True) + EPS)
    xb = cen * rb
    sz, sr = _sigmoid(z), _sigmoid(rg)
    return ra, xa, rb, xb, sz, sr, _sigmoid(ga), _sigmoid(gb)


def _merge_specs(tt, d, cz, crg, cga, cgb):
    blk = pl.BlockSpec((tt, d), lambda i: (i, 0))
    pcol = lambda c: pl.BlockSpec((tt, d), lambda i: (i, c // d))
    row = pl.BlockSpec((1, d), lambda i: (0, 0))
    return blk, [blk, blk, pcol(cz), pcol(crg), pcol(cga), pcol(cgb), row, row], row


def _merge_fwd(oa, ob, proj, wa, wb, cz, crg, cga, cgb, *, name):
    t, d = oa.shape
    tt = _pick(t, (256,))
    blk, in_specs, _ = _merge_specs(tt, d, cz, crg, cga, cgb)

    def body(oa_ref, ob_ref, z_ref, rg_ref, ga_ref, gb_ref, wa_ref, wb_ref, o_ref):
        for h in range(HEADS):
            cols = slice(h * DV, (h + 1) * DV)
            z, rg = z_ref[:, cols], rg_ref[:, cols]
            _, xa, _, xb, sz, sr, sga, sgb = _merge_parts(
                oa_ref[:, cols], ob_ref[:, cols], z, rg, ga_ref[:, cols], gb_ref[:, cols])
            o_a = xa * wa_ref[:, cols] * (z * sz)
            o_b = xb * wb_ref[:, cols] * (rg * sr)
            o_ref[:, cols] = (sga * o_a + sgb * o_b).astype(BF16)

    return pl.pallas_call(
        body, name=name, grid=(t // tt,), in_specs=in_specs, out_specs=blk,
        out_shape=jax.ShapeDtypeStruct((t, d), BF16), compiler_params=_params(("parallel",)),
    )(oa, ob, proj, proj, proj, proj, wa, wb)


def _merge_bwd(dmix, oa, ob, proj, wa, wb, cz, crg, cga, cgb, *, name):
    t, d = oa.shape
    tt = _pick(t, (256,))
    blk, in_specs, row = _merge_specs(tt, d, cz, crg, cga, cgb)

    def body(dm_ref, oa_ref, ob_ref, z_ref, rg_ref, ga_ref, gb_ref, wa_ref, wb_ref,
             doa_ref, dob_ref, dz_ref, drg_ref, dga_ref, dgb_ref, dwa_ref, dwb_ref):
        @pl.when(pl.program_id(0) == 0)
        def _():
            dwa_ref[...] = jnp.zeros_like(dwa_ref)
            dwb_ref[...] = jnp.zeros_like(dwb_ref)

        for h in range(HEADS):
            cols = slice(h * DV, (h + 1) * DV)
            z, rg, wa_h, wb_h, dmx = z_ref[:, cols], rg_ref[:, cols], wa_ref[:, cols], wb_ref[:, cols], dm_ref[:, cols]
            ra, xa, rb, xb, sz, sr, sga, sgb = _merge_parts(
                oa_ref[:, cols], ob_ref[:, cols], z, rg, ga_ref[:, cols], gb_ref[:, cols])
            na, nb = xa * wa_h, xb * wb_h
            sil_z, sil_r = z * sz, rg * sr
            o_a, o_b = na * sil_z, nb * sil_r
            dga_ref[:, cols] = (dmx * o_a * sga * (1.0 - sga)).astype(BF16)
            dgb_ref[:, cols] = (dmx * o_b * sgb * (1.0 - sgb)).astype(BF16)
            d_oa, d_ob = dmx * sga, dmx * sgb
            dz_ref[:, cols] = (d_oa * na * sz * (1.0 + z * (1.0 - sz))).astype(BF16)
            drg_ref[:, cols] = (d_ob * nb * sr * (1.0 + rg * (1.0 - sr))).astype(BF16)
            dna, dnb = d_oa * sil_z, d_ob * sil_r
            dwa_ref[:, cols] += jnp.sum(dna * xa, axis=0, keepdims=True)
            dwb_ref[:, cols] += jnp.sum(dnb * xb, axis=0, keepdims=True)
            gwa, gwb = dna * wa_h, dnb * wb_h
            doa_ref[:, cols] = ra * (gwa - xa * jnp.mean(gwa * xa, axis=-1, keepdims=True))
            dob_ref[:, cols] = rb * (gwb - jnp.mean(gwb, axis=-1, keepdims=True)
                                     - xb * jnp.mean(gwb * xb, axis=-1, keepdims=True))

    return pl.pallas_call(
        body, name=name, grid=(t // tt,), in_specs=[blk] + in_specs,
        out_specs=[blk, blk, blk, blk, blk, blk, row, row],
        out_shape=[jax.ShapeDtypeStruct((t, d), F32)] * 2 + [jax.ShapeDtypeStruct((t, d), BF16)] * 4
        + [jax.ShapeDtypeStruct((1, d), F32)] * 2,
        compiler_params=_params(("arbitrary",)),
    )(dmix, oa, ob, proj, proj, proj, proj, wa, wb)


def _row_block(rows, cols, itemsize=4, target=1 << 20):
    for rb in (512, 256, 128, 64, 32, 16, 8):
        if rows % rb == 0 and rb * cols * itemsize <= target:
            return rb
    return rows


def _adamw(w, g, m, v, *, name):
    rows, cols = w.shape
    rb = _row_block(rows, cols)

    def body(w_ref, g_ref, m_ref, v_ref, d_ref, nm_ref, nv_ref):
        gg = g_ref[...]
        mm = ADAM_B1 * m_ref[...] + (1.0 - ADAM_B1) * gg
        vv = ADAM_B2 * v_ref[...] + (1.0 - ADAM_B2) * (gg * gg)
        m_hat = mm / (1.0 - ADAM_B1 ** ADAM_STEP)
        v_hat = vv / (1.0 - ADAM_B2 ** ADAM_STEP)
        d_ref[...] = -ADAM_LR * (m_hat / (jnp.sqrt(v_hat) + ADAM_EPS) + ADAM_WD * w_ref[...])
        nm_ref[...] = mm
        nv_ref[...] = vv

    blk = pl.BlockSpec((rb, cols), lambda i: (i, 0))
    return pl.pallas_call(
        body, name=name, grid=(rows // rb,), in_specs=[blk] * 4, out_specs=[blk] * 3,
        out_shape=[jax.ShapeDtypeStruct((rows, cols), F32)] * 3, compiler_params=_params(("parallel",)),
    )(w, g, m, v)


def _pair_add(g, rsib, c_idx, *, name):
    _, _, hr, cols = g.shape
    rb = _row_block(hr, cols, 2)

    def body(c_ref, g_ref, r_ref, o_ref):
        o_ref[...] = (g_ref[...].astype(F32) + r_ref[...].astype(F32)).astype(BF16)

    return pl.pallas_call(
        body, name=name,
        grid_spec=pltpu.PrefetchScalarGridSpec(
            num_scalar_prefetch=1, grid=(N_CHIPS, hr // rb),
            in_specs=[pl.BlockSpec((None, None, rb, cols), lambda j, i, c: (j, c[0], i, 0)),
                      pl.BlockSpec((None, rb, cols), lambda j, i, c: (j, i, 0))],
            out_specs=pl.BlockSpec((None, rb, cols), lambda j, i, c: (j, i, 0))),
        out_shape=jax.ShapeDtypeStruct((N_CHIPS, hr, cols), BF16),
        compiler_params=_params(("parallel", "parallel")),
    )(c_idx, g, rsib)


def _chip_sum(qb, *, name):
    _, hr, cols = qb.shape
    rb = _row_block(hr, cols, 2, 1 << 19)

    def body(q_ref, o_ref):
        acc = q_ref[0].astype(F32)
        for i in range(1, N_CHIPS):
            acc = acc + q_ref[i].astype(F32)
        o_ref[...] = acc

    return pl.pallas_call(
        body, name=name, grid=(hr // rb,),
        in_specs=[pl.BlockSpec((N_CHIPS, rb, cols), lambda i: (0, i, 0))],
        out_specs=pl.BlockSpec((rb, cols), lambda i: (i, 0)),
        out_shape=jax.ShapeDtypeStruct((hr, cols), F32), compiler_params=_params(("parallel",)),
    )(qb)


def _place():
    x, y, c = lax.axis_index("x"), lax.axis_index("y"), lax.axis_index("c")
    return x, y, c, [(1 - x, y), (x, 1 - y), (1 - x, 1 - y)]


ANY = pl.BlockSpec(memory_space=pl.ANY)


def _gather_big(shards, *, name):
    nw = len(shards)

    def body(*refs):
        ins, outs = refs[:nw], refs[nw:2 * nw]
        ssem, rsem = refs[2 * nw:]
        x, y, c, chips = _place()
        s_me = 2 * x + y

        def rows(w, half):
            hr = shards[w].shape[0] // 2
            return pl.ds(pl.multiple_of(half * hr, 16), hr)

        def rcopy(w, k, src, slot, half, to):
            return pltpu.make_async_remote_copy(
                src_ref=src, dst_ref=outs[w].at[slot, rows(w, half)],
                send_sem=ssem.at[w * 6 + k], recv_sem=rsem.at[w * 6 + k], device_id=to, device_id_type=MESH)

        sent = []
        for w in range(nw):
            for j, chip in enumerate(chips):
                cp = rcopy(w, j, ins[w].at[rows(w, c)], s_me, c, (chip[0], chip[1], c))
                cp.start()
                sent.append(cp)
        for w in range(nw):
            for j, chip in enumerate(chips):
                s_j = 2 * chip[0] + chip[1]
                rcopy(w, j, ins[w].at[rows(w, c)], s_j, c, (x, y, c)).wait_recv()
                cp = rcopy(w, 3 + j, outs[w].at[s_j, rows(w, c)], s_j, c, (x, y, 1 - c))
                cp.start()
                sent.append(cp)
        for w in range(nw):
            for j, chip in enumerate(chips):
                s_j = 2 * chip[0] + chip[1]
                rcopy(w, 3 + j, ins[w].at[rows(w, c)], s_j, 1 - c, (x, y, c)).wait_recv()
        for cp in sent:
            cp.wait_send()

    return pl.pallas_call(
        body, name=name, in_specs=[ANY] * nw, out_specs=[ANY] * nw,
        out_shape=[jax.ShapeDtypeStruct((N_CHIPS,) + s.shape, s.dtype) for s in shards],
        scratch_shapes=[pltpu.SemaphoreType.DMA((nw * 6,)), pltpu.SemaphoreType.DMA((nw * 6,))],
        compiler_params=pltpu.CompilerParams(has_side_effects=True),
    )(*shards)


def _pair_exchange(gs, *, name):
    nw = len(gs)

    def body(*refs):
        ins, outs = refs[:nw], refs[nw:2 * nw]
        ssem, rsem = refs[2 * nw:]
        x, y, c, _ = _place()
        cps = []
        for w in range(nw):
            cp = pltpu.make_async_remote_copy(
                src_ref=ins[w].at[:, 1 - c], dst_ref=outs[w], send_sem=ssem.at[w], recv_sem=rsem.at[w],
                device_id=(x, y, 1 - c), device_id_type=MESH)
            cp.start()
            cps.append(cp)
        for cp in cps:
            cp.wait()

    return pl.pallas_call(
        body, name=name, in_specs=[ANY] * nw, out_specs=[ANY] * nw,
        out_shape=[jax.ShapeDtypeStruct((g.shape[0],) + g.shape[2:], g.dtype) for g in gs],
        scratch_shapes=[pltpu.SemaphoreType.DMA((nw,)), pltpu.SemaphoreType.DMA((nw,))],
        compiler_params=pltpu.CompilerParams(has_side_effects=True),
    )(*gs)


def _chip_scatter(ps, *, name):
    nw = len(ps)

    def body(*refs):
        ins, outs = refs[:nw], refs[nw:2 * nw]
        ssem, rsem = refs[2 * nw:]
        x, y, c, chips = _place()
        s_me = 2 * x + y
        cps = []
        for w in range(nw):
            for j, chip in enumerate(chips):
                s_j = 2 * chip[0] + chip[1]
                cp = pltpu.make_async_remote_copy(
                    src_ref=ins[w].at[s_j], dst_ref=outs[w].at[s_me],
                    send_sem=ssem.at[w * 3 + j], recv_sem=rsem.at[w * 3 + j],
                    device_id=(chip[0], chip[1], c), device_id_type=MESH)
                cp.start()
                cps.append(cp)
        for cp in cps:
            cp.wait()

    return pl.pallas_call(
        body, name=name, in_specs=[ANY] * nw, out_specs=[ANY] * nw,
        out_shape=[jax.ShapeDtypeStruct(p.shape, p.dtype) for p in ps],
        scratch_shapes=[pltpu.SemaphoreType.DMA((nw * 3,)), pltpu.SemaphoreType.DMA((nw * 3,))],
        compiler_params=pltpu.CompilerParams(has_side_effects=True),
    )(*ps)


def _pair_share(hs, *, name):
    nw = len(hs)

    def body(*refs):
        ins, outs = refs[:nw], refs[nw:2 * nw]
        ssem, rsem = refs[2 * nw:]
        x, y, c, _ = _place()
        cps = []
        for w in range(nw):
            cp = pltpu.make_async_remote_copy(
                src_ref=ins[w], dst_ref=outs[w], send_sem=ssem.at[w], recv_sem=rsem.at[w],
                device_id=(x, y, 1 - c), device_id_type=MESH)
            cp.start()
            cps.append(cp)
        for cp in cps:
            cp.wait()

    return pl.pallas_call(
        body, name=name, in_specs=[ANY] * nw, out_specs=[ANY] * nw,
        out_shape=[jax.ShapeDtypeStruct(h.shape, h.dtype) for h in hs],
        scratch_shapes=[pltpu.SemaphoreType.DMA((nw,)), pltpu.SemaphoreType.DMA((nw,))],
        compiler_params=pltpu.CompilerParams(has_side_effects=True),
    )(*hs)


def _gather_small(a, *, name):
    def body(a_ref, o_ref, ssem, rsem):
        x, y, c, chips = _place()
        s_me = 2 * x + y
        o_ref[s_me] = a_ref[...]
        cps = []
        for j, chip in enumerate(chips):
            cp = pltpu.make_async_remote_copy(
                src_ref=a_ref, dst_ref=o_ref.at[s_me], send_sem=ssem.at[j], recv_sem=rsem.at[j],
                device_id=(chip[0], chip[1], c), device_id_type=MESH)
            cp.start()
            cps.append(cp)
        for cp in cps:
            cp.wait()

    vm = pl.BlockSpec(memory_space=pltpu.VMEM)
    return pl.pallas_call(
        body, name=name, in_specs=[vm], out_specs=vm,
        out_shape=jax.ShapeDtypeStruct((N_CHIPS,) + a.shape, a.dtype),
        scratch_shapes=[pltpu.SemaphoreType.DMA((3,)), pltpu.SemaphoreType.DMA((3,))],
    )(a)


def _allreduce_small(p, *, name):
    def body(p_ref, o_ref, buf, ssem, rsem):
        x, y, c, _ = _place()
        me = 4 * x + 2 * y + c
        buf[me] = p_ref[...]
        cps = []
        for k in range(1, N_DEV):
            fx, fy, fc = (k >> 2) & 1, (k >> 1) & 1, k & 1
            peer = (x + fx - 2 * x * fx, y + fy - 2 * y * fy, c + fc - 2 * c * fc)
            cp = pltpu.make_async_remote_copy(
                src_ref=buf.at[me], dst_ref=buf.at[me], send_sem=ssem.at[k - 1], recv_sem=rsem.at[k - 1],
                device_id=peer, device_id_type=MESH)
            cp.start()
            cps.append(cp)
        for cp in cps:
            cp.wait()
        acc = buf[0]
        for i in range(1, N_DEV):
            acc = acc + buf[i]
        o_ref[...] = acc

    vm = pl.BlockSpec(memory_space=pltpu.VMEM)
    return pl.pallas_call(
        body, name=name, in_specs=[vm], out_specs=vm,
        out_shape=jax.ShapeDtypeStruct(p.shape, p.dtype),
        scratch_shapes=[pltpu.VMEM((N_DEV,) + p.shape, p.dtype), pltpu.SemaphoreType.DMA((N_DEV - 1,)),
                        pltpu.SemaphoreType.DMA((N_DEV - 1,))],
    )(p)


def _rows_to_tokens(r):
    h, n = r.shape[0], r.shape[1]
    return r.reshape(h, n * CHUNK).T


def _tokens_to_rows(a):
    t, h = a.shape
    return a.T.reshape(h, t // CHUNK, 1, CHUNK)


def kernel(x, norm1_w, w_in, conv_w, a_log, dt_bias, gdn_norm_w, ret_norm_w, w_out, norm2_w, w_gate, w_up, w_down, norm_f_w, loss_target, m_norm1_w, m_w_in, m_conv_w, m_a_log, m_dt_bias, m_gdn_norm_w, m_ret_norm_w, m_w_out, m_norm2_w, m_w_gate, m_w_up, m_w_down, m_norm_f_w, v_norm1_w, v_w_in, v_conv_w, v_a_log, v_dt_bias, v_gdn_norm_w, v_ret_norm_w, v_w_out, v_norm2_w, v_w_gate, v_w_up, v_w_down, v_norm_f_w):
    t, d = x.shape[1], x.shape[2]
    f = w_gate.shape[2] * N_CHIPS
    nqk, nv = HEADS * DK, HEADS * DV
    ncs = w_in.shape[2]
    c_idx = lax.axis_index("c")
    s_idx = 2 * lax.axis_index("x") + lax.axis_index("y")
    xs = x[0]
    tgt = loss_target[0]

    o_az = 2 * nqk + nv
    o_small = o_az + nv
    n_small = 2 * HEADS
    o_rq = o_small
    o_rk, o_rv = o_rq + nqk, o_rq + 2 * nqk
    o_rg = o_rv + nv
    o_ga = o_rg + nv
    o_gb = o_ga + d
    n_main = o_gb + d

    own = [w_in[0].astype(BF16), w_out[0].astype(BF16), w_gate[0].astype(BF16), w_up[0].astype(BF16),
           w_down[0].astype(BF16)]
    put_own = lambda full, mine: lax.dynamic_update_slice(full, mine[None], (s_idx, 0, 0))
    wg_in, wg_out, wg_gate, wg_up, wg_down = [
        put_own(g, o) for g, o in zip(_gather_big(own, name="gather_weights"), own)]
    conv_full = _gather_small(conv_w[0], name="gather_conv_w")
    conv_full = jnp.concatenate([conv_full[i] for i in range(N_CHIPS)], axis=1)
    cuts = []
    lo_small = hi_small = None
    for s in range(N_CHIPS):
        g0, g1 = s * ncs, (s + 1) * ncs
        if g1 <= o_small or g0 >= o_small + n_small:
            cuts.append((s, 0, ncs))
        else:
            lo_small, hi_small = o_small - g0, o_small + n_small - g0
            small_shard = s
            cuts += [(s, 0, lo_small), (s, hi_small, ncs)]
    w_main = jnp.concatenate([wg_in[s][:, a:b] for s, a, b in cuts], axis=1)
    w_small = jnp.pad(wg_in[small_shard][:, lo_small:hi_small], ((0, 0), (0, LANES - n_small)))
    w_o = wg_out.reshape(d, d)
    w_g = jnp.concatenate([wg_gate[i] for i in range(N_CHIPS)], axis=1)
    w_u = jnp.concatenate([wg_up[i] for i in range(N_CHIPS)], axis=1)
    w_d = wg_down.reshape(f, d)

    pad16 = lambda a: jnp.pad(a, ((0, 0), (HEADS, LANES - 2 * HEADS)))
    alog_row, dtb_row = pad16(a_log), pad16(dt_bias)
    wa_row = jnp.tile(gdn_norm_w, (1, HEADS))
    inv = ROPE_BASE ** (-jnp.arange(0, DK, 2, dtype=F32) / DK)
    ang = jnp.arange(t, dtype=F32)[:, None] * inv[None, :]
    cos2 = jnp.concatenate([jnp.cos(ang), jnp.cos(ang)], axis=1)
    sin2 = jnp.concatenate([-jnp.sin(ang), jnp.sin(ang)], axis=1)
    lg = jnp.log1p(-jnp.exp2(-5.0 - jnp.arange(HEADS, dtype=F32)))
    lg_tab = jnp.broadcast_to(lg[:, None, None], (HEADS, 1, LANES))

    fq = f // N_CHIPS
    u1 = _rms_fwd(xs, norm1_w, name="rms1_fwd")
    proj = _mm(u1, w_main, tm=512, tn=2048, tk=d, name="mm_proj")
    projs = _mm(u1, w_small, tm=1024, tn=LANES, tk=d, name="mm_proj_small")
    q_a, k_a, v_a = _conv_fwd(proj, conv_full, name="conv_fwd")
    bg = _bg_fwd(projs, alog_row, dtb_row, name="bg_fwd")
    beta_r = _tokens_to_rows(bg[:, :HEADS])
    g_r = _tokens_to_rows(bg[:, HEADS:2 * HEADS])
    gc_r, tinv, u_a, w_a, p_a = _gdn_prep(q_a, k_a, v_a, beta_r, g_r, name="gdn_prep")
    o_a, s_a = _gdn_scan(q_a, k_a, u_a, w_a, p_a, gc_r, name="gdn_scan")
    o_b, s_b = _ret_fwd(proj, cos2, sin2, lg_tab, o_rq, o_rk, o_rv, name="ret_fwd")
    mixed = _merge_fwd(o_a, o_b, proj, wa_row, ret_norm_w, o_az, o_rg, o_ga, o_gb, name="merge_fwd")
    h1 = _mm(mixed, w_o, res=xs, tm=512, tn=d, tk=d, name="mm_out")
    hn = _rms_fwd(h1, norm2_w, name="rms2_fwd")
    gt, up, act = _ffn_in(hn, w_g, w_u, tm=256, tn=fq, name="ffn_in")
    h2 = _mm(act, w_d, res=h1, tm=512, tn=d, tk=fq, name="mm_down")
    loss_row, dh2, dh2b, d_nf = _loss_head(h2, tgt, norm_f_w.reshape(1, d), name="loss_head")

    g_down = _mm(act, dh2b, ta=True, out_dtype=BF16, tm=fq, tn=d, tk=1024, name="mm_dw_down")
    dgt, dup = _ffn_back(dh2b, w_d, gt, up, tm=512, tn=fq, name="ffn_back")
    dhn = _mm(dgt, w_g, tb=True, tm=512, tn=d, tk=fq, name="mm_dhn_gate")
    dhn = _mm(dup, w_u, tb=True, res=dhn, tm=512, tn=d, tk=fq, name="mm_dhn_up")
    g_gate = _mm(hn, dgt, ta=True, out_dtype=BF16, tm=1024, tn=fq, tk=1024, name="mm_dw_gate")
    g_up = _mm(hn, dup, ta=True, out_dtype=BF16, tm=1024, tn=fq, tk=1024, name="mm_dw_up")
    dh1, dh1b, d_n2 = _rms_bwd(dhn, h1, norm2_w, dh2, name="rms2_bwd")
    dmix = _mm(dh1b, w_o, tb=True, tm=512, tn=d, tk=d, name="mm_dmix")
    g_out = _mm(mixed, dh1b, ta=True, out_dtype=BF16, tm=1024, tn=d, tk=1024, name="mm_dw_out")
    do_a, do_b, d_az, d_rg, d_ga, d_gb, d_wa, d_wb = _merge_bwd(
        dmix, o_a, o_b, proj, wa_row, ret_norm_w, o_az, o_rg, o_ga, o_gb, name="merge_bwd")
    d_rq, d_rk, d_rv = _ret_bwd(proj, cos2, sin2, lg_tab, s_b, do_b, o_rq, o_rk, o_rv, name="ret_bwd")
    dq_a, dk_a, dv_a, dbeta_r, dg_r = _gdn_bwd(q_a, k_a, v_a, beta_r, gc_r, tinv, u_a, w_a, s_a, do_a, name="gdn_bwd")
    dc = _conv_bwd_pre(proj, conv_full, dq_a, dk_a, dv_a, name="conv_bwd_pre")
    d_aqkv, d_cw = _conv_bwd(proj, dc, conv_full, name="conv_bwd")
    dbg = jnp.pad(jnp.concatenate([_rows_to_tokens(dbeta_r), _rows_to_tokens(dg_r)], axis=1),
                  ((0, 0), (0, LANES - 2 * HEADS)))
    dprojs, d_alog, d_dtb = _bg_bwd(projs, dbg, alog_row, dtb_row, name="bg_bwd")
    dproj = jnp.concatenate([d_aqkv, d_az, d_rq, d_rk, d_rv, d_rg, d_ga, d_gb], axis=1)
    du = _mm(dproj, w_main, tb=True, tm=1024, tn=d, tk=1024, name="mm_du")
    du = _mm(dprojs, w_small, tb=True, res=du, tm=1024, tn=d, tk=LANES, name="mm_du_small")
    g_main = _mm(u1, dproj, ta=True, out_dtype=BF16, tm=1024, tn=2048, tk=1024, name="mm_dw_in")
    g_small = _mm(u1, dprojs, ta=True, out_dtype=BF16, tm=1024, tn=LANES, tk=1024, name="mm_dw_in_small")
    dx, _, d_n1 = _rms_bwd(du, xs, norm1_w, dh1, name="rms1_bwd")

    pieces, off = [], 0
    for s in range(N_CHIPS):
        if s == small_shard:
            a = g_main[:, off:off + lo_small]
            b = g_main[:, off + lo_small:off + lo_small + ncs - hi_small]
            pieces.append(jnp.concatenate([a, g_small[:, :n_small], b], axis=1))
            off += ncs - n_small
        else:
            pieces.append(g_main[:, off:off + ncs])
            off += ncs
    col_split = lambda g, wd: jnp.stack([g[:, i * wd:(i + 1) * wd] for i in range(N_CHIPS)])
    halves = lambda g: g.reshape(N_CHIPS, 2, g.shape[1] // 2, g.shape[2])
    gs = [halves(jnp.stack(pieces)), halves(g_out.reshape(N_CHIPS, d // N_CHIPS, d)),
          halves(col_split(g_gate, f // N_CHIPS)), halves(col_split(g_up, f // N_CHIPS)),
          halves(g_down.reshape(N_CHIPS, f // N_CHIPS, d))]
    names = ["w_in", "w_out", "w_gate", "w_up", "w_down"]
    rsib = _pair_exchange(gs, name="grad_pair_exchange")
    c_arr = jnp.reshape(c_idx, (1,)).astype(jnp.int32)
    ps = [_pair_add(g, r, c_arr, name=f"grad_pair_add_{nm}") for g, r, nm in zip(gs, rsib, names)]
    qs = _chip_scatter(ps, name="grad_chip_scatter")
    qs = [lax.dynamic_update_slice(q, lax.dynamic_slice(p, (s_idx, 0, 0), (1,) + p.shape[1:]), (s_idx, 0, 0))
          for q, p in zip(qs, ps)]
    hs = [_chip_sum(q, name=f"grad_chip_sum_{nm}") for q, nm in zip(qs, names)]
    theirs = _pair_share(hs, name="grad_pair_share")
    big_w = [w_in[0], w_out[0], w_gate[0], w_up[0], w_down[0]]
    big_m = [m_w_in[0], m_w_out[0], m_w_gate[0], m_w_up[0], m_w_down[0]]
    big_v = [v_w_in[0], v_w_out[0], v_w_gate[0], v_w_up[0], v_w_down[0]]
    big = {}
    for nm, mine, other, w_, m_, v_ in zip(names, hs, theirs, big_w, big_m, big_v):
        g = jnp.where(c_idx == 0, jnp.concatenate([mine, other]), jnp.concatenate([other, mine]))
        dl, nm_, nv_ = _adamw(w_, g, m_, v_, name=f"adamw_{nm}")
        big[nm] = tuple(a[None] for a in (g, dl, nm_, nv_))

    d_wa_h = jnp.sum(d_wa.reshape(HEADS, DV), axis=0, keepdims=True)
    small = [d_n1, d_alog[:, HEADS:2 * HEADS], d_dtb[:, HEADS:2 * HEADS], d_wa_h, d_wb, d_n2, d_nf,
             d_cw[:CONV_W].reshape(1, -1)]
    sizes = [a.shape[1] for a in small]
    packed = jnp.concatenate(small, axis=1)
    n_pack = packed.shape[1]
    n_rows = -(-n_pack // LANES)
    n_rows = -(-n_rows // 8) * 8
    packed = jnp.pad(packed, ((0, 0), (0, n_rows * LANES - n_pack))).reshape(n_rows, LANES)
    red = _allreduce_small(packed, name="allreduce_small").reshape(1, -1)
    offs = np.cumsum([0] + sizes)
    g_n1, g_alog, g_dtb, g_wa, g_wb, g_n2, g_nf, g_cw = [red[:, offs[i]:offs[i + 1]] for i in range(len(sizes))]
    ncw = conv_w.shape[2]
    g_cw = lax.dynamic_slice(g_cw.reshape(CONV_W, -1), (0, s_idx * ncw), (CONV_W, ncw))

    def small_update(w_, g, m_, v_, nm):
        shape = w_.shape
        pad = (-w_.size) % LANES
        to2 = lambda a: jnp.pad(a.reshape(1, -1), ((0, 0), (0, pad)))
        outs = _adamw(to2(w_), to2(g), to2(m_), to2(v_), name=f"adamw_{nm}")
        return (g.reshape(shape),) + tuple(a[:, :w_.size].reshape(shape) for a in outs)

    res = {
        "norm1_w": small_update(norm1_w, g_n1, m_norm1_w, v_norm1_w, "norm1_w"),
        "w_in": big["w_in"],
        "conv_w": small_update(conv_w, g_cw, m_conv_w, v_conv_w, "conv_w"),
        "a_log": small_update(a_log, g_alog, m_a_log, v_a_log, "a_log"),
        "dt_bias": small_update(dt_bias, g_dtb, m_dt_bias, v_dt_bias, "dt_bias"),
        "gdn_norm_w": small_update(gdn_norm_w, g_wa, m_gdn_norm_w, v_gdn_norm_w, "gdn_norm_w"),
        "ret_norm_w": small_update(ret_norm_w, g_wb, m_ret_norm_w, v_ret_norm_w, "ret_norm_w"),
        "w_out": big["w_out"],
        "norm2_w": small_update(norm2_w, g_n2, m_norm2_w, v_norm2_w, "norm2_w"),
        "w_gate": big["w_gate"],
        "w_up": big["w_up"],
        "w_down": big["w_down"],
        "norm_f_w": small_update(norm_f_w, g_nf, m_norm_f_w, v_norm_f_w, "norm_f_w"),
    }
    order = ["norm1_w", "w_in", "conv_w", "a_log", "dt_bias", "gdn_norm_w", "ret_norm_w", "w_out", "norm2_w",
             "w_gate", "w_up", "w_down", "norm_f_w"]
    loss = lax.psum(loss_row[0, 0], ("x", "y", "c"))
    return (loss, dx[None], *[res[n][0] for n in order], *[res[n][1] for n in order],
            *[res[n][2] for n in order], *[res[n][3] for n in order])
```

```python
import functools

import jax
import jax.numpy as jnp
import numpy as np
from jax import lax
from jax.experimental import pallas as pl
from jax.experimental.pallas import tpu as pltpu

F32 = jnp.float32
BF16 = jnp.bfloat16
MESH = pl.DeviceIdType.MESH

HEADS = 8
DK = 128
DV = 256
CHUNK = 64
CONV_W = 4
EPS = 1e-6
ROPE_BASE = 10000.0
ADAM_LR, ADAM_B1, ADAM_B2, ADAM_EPS, ADAM_WD, ADAM_STEP = 0.001, 0.9, 0.999, 1e-08, 0.01, 10
N_CHIPS = 4
N_DEV = 8
LANES = 128
HALO = 8
VMEM_LIMIT = 56 * 1024 * 1024
HB = 8


def _pick(n, cands):
    for c in cands:
        if n % c == 0:
            return c
    raise ValueError(f"no tile for {n} in {cands}")


def _params(sem=None):
    return pltpu.CompilerParams(dimension_semantics=sem, vmem_limit_bytes=VMEM_LIMIT)


def _dot(a, b):
    return jnp.dot(a.astype(BF16), b.astype(BF16), preferred_element_type=F32)


def _dot_nt(a, b):
    return lax.dot_general(a.astype(BF16), b.astype(BF16), (((1,), (1,)), ((), ())), preferred_element_type=F32)


def _dot_tn(a, b):
    return lax.dot_general(a.astype(BF16), b.astype(BF16), (((0,), (0,)), ((), ())), preferred_element_type=F32)


def _sigmoid(x):
    return 1.0 / (1.0 + jnp.exp(-x))


def _iota2(n):
    return lax.broadcasted_iota(jnp.int32, (n, n), 0), lax.broadcasted_iota(jnp.int32, (n, n), 1)


def _row_to_col(row):
    n = row.shape[1]
    r, c = _iota2(n)
    return jnp.sum(jnp.where(r == c, jnp.broadcast_to(row, (n, n)), 0.0), axis=1, keepdims=True)


def _col_to_row(col):
    n = col.shape[0]
    r, c = _iota2(n)
    return jnp.sum(jnp.where(r == c, jnp.broadcast_to(col, (n, n)), 0.0), axis=0, keepdims=True)


class _Side:
    def __init__(self, ins, out_shapes, n_sem, phases):
        self.ins, self.out_shapes, self.n_sem, self.phases = list(ins), list(out_shapes), n_sem, phases

    def sems(self):
        return [pltpu.SemaphoreType.DMA((self.n_sem,)), pltpu.SemaphoreType.DMA((self.n_sem,))]


def _mm(a, b, *, name, tm, tn, tk, ta=False, tb=False, out_dtype=F32, res=None, side=None):
    m, k = (a.shape[1], a.shape[0]) if ta else a.shape
    n = b.shape[0] if tb else b.shape[1]
    tm, tn, tk = min(tm, m), min(tn, n), min(tk, k)
    assert m % tm == 0 and n % tn == 0 and k % tk == 0, (name, m, n, k)
    nk = k // tk
    nj, ni = n // tn, m // tm
    dn = (((0 if ta else 1,), (1 if tb else 0,)), ((), ()))
    n_in = 2 + (res is not None)
    n_side_in = len(side.ins) if side else 0
    n_side_out = len(side.out_shapes) if side else 0

    def body(*refs):
        a_ref, b_ref = refs[0], refs[1]
        r_ref = refs[2] if res is not None else None
        o_ref = refs[n_in + n_side_in]
        if side:
            s_in = refs[n_in:n_in + n_side_in]
            s_out = refs[n_in + n_side_in + 1:n_in + n_side_in + 1 + n_side_out]
            ssem, rsem = refs[-2], refs[-1]
            j_, i_, k_ = pl.program_id(0), pl.program_id(1), pl.program_id(2)
            when = [(j_ == 0) & (i_ == 0) & (k_ == 0)]
            if len(side.phases) == 3:
                when.append((j_ == nj // 2) & (i_ == 0) & (k_ == 0))
            when.append((j_ == nj - 1) & (i_ == ni - 1) & (k_ == nk - 1))
            assert len(when) == len(side.phases) and (len(when) == 2 or nj >= 2)

        def run_phase(p):
            @pl.when(when[p])
            def _():
                side.phases[p](s_in, s_out, ssem, rsem)

        if side:
            for p in range(len(side.phases) - 1):
                run_phase(p)

        def finish(r):
            if res is not None:
                r = r + r_ref[...]
            o_ref[...] = r.astype(out_dtype)

        part = lax.dot_general(a_ref[...], b_ref[...], dn, preferred_element_type=F32)
        if nk == 1:
            finish(part)
        else:
            acc = refs[n_in + n_side_in + 1 + n_side_out]
            kk = pl.program_id(2)

            @pl.when(kk == 0)
            def _():
                acc[...] = part

            @pl.when((kk > 0) & (kk < nk - 1))
            def _():
                acc[...] += part

            @pl.when(kk == nk - 1)
            def _():
                finish(acc[...] + part)

        if side:
            run_phase(len(side.phases) - 1)

    a_spec = pl.BlockSpec((tk, tm), lambda j, i, kk: (kk, i)) if ta else pl.BlockSpec((tm, tk), lambda j, i, kk: (i, kk))
    b_spec = pl.BlockSpec((tn, tk), lambda j, i, kk: (j, kk)) if tb else pl.BlockSpec((tk, tn), lambda j, i, kk: (kk, j))
    o_spec = pl.BlockSpec((tm, tn), lambda j, i, kk: (i, j))
    in_specs, args = [a_spec, b_spec], [a, b]
    if res is not None:
        in_specs.append(o_spec)
        args.append(res)
    out_specs, out_shape = o_spec, jax.ShapeDtypeStruct((m, n), out_dtype)
    scratch = [pltpu.VMEM((tm, tn), F32)] if nk > 1 else []
    sem = ("parallel", "parallel", "arbitrary")
    if side:
        hbm = pl.BlockSpec(memory_space=pl.ANY)
        in_specs += [hbm] * n_side_in
        args += side.ins
        out_specs, out_shape = [o_spec] + [hbm] * n_side_out, [out_shape] + side.out_shapes
        scratch += side.sems()
        sem = ("arbitrary",) * 3
    return pl.pallas_call(
        body, name=name, grid=(nj, ni, nk), in_specs=in_specs, out_specs=out_specs, out_shape=out_shape,
        scratch_shapes=scratch, compiler_params=_params(sem),
    )(*args)


def _rms_fwd(x, w, *, name):
    t, d = x.shape
    tt = _pick(t, (512, 256))

    def body(x_ref, w_ref, o_ref):
        xv = x_ref[...]
        r = lax.rsqrt(jnp.mean(xv * xv, axis=-1, keepdims=True) + EPS)
        o_ref[...] = (xv * r * w_ref[...]).astype(BF16)

    return pl.pallas_call(
        body, name=name, grid=(t // tt,),
        in_specs=[pl.BlockSpec((tt, d), lambda i: (i, 0)), pl.BlockSpec((1, d), lambda i: (0, 0))],
        out_specs=pl.BlockSpec((tt, d), lambda i: (i, 0)),
        out_shape=jax.ShapeDtypeStruct((t, d), BF16),
        compiler_params=_params(("parallel",)),
    )(x, w)


def _rms_bwd(dn, x, w, dres, *, name):
    t, d = x.shape
    tt = _pick(t, (256,))

    def body(dn_ref, x_ref, w_ref, dres_ref, dx_ref, dxb_ref, dw_ref):
        xv, g = x_ref[...], dn_ref[...]
        r = lax.rsqrt(jnp.mean(xv * xv, axis=-1, keepdims=True) + EPS)
        xh = xv * r
        gw = g * w_ref[...]
        dx = dres_ref[...] + r * (gw - xh * jnp.mean(gw * xh, axis=-1, keepdims=True))
        dx_ref[...] = dx
        dxb_ref[...] = dx.astype(BF16)

        @pl.when(pl.program_id(0) == 0)
        def _():
            dw_ref[...] = jnp.zeros_like(dw_ref)

        dw_ref[...] += jnp.sum(g * xh, axis=0, keepdims=True)

    blk = pl.BlockSpec((tt, d), lambda i: (i, 0))
    row = pl.BlockSpec((1, d), lambda i: (0, 0))
    return pl.pallas_call(
        body, name=name, grid=(t // tt,),
        in_specs=[blk, blk, row, blk], out_specs=[blk, blk, row],
        out_shape=[jax.ShapeDtypeStruct((t, d), F32), jax.ShapeDtypeStruct((t, d), BF16),
                   jax.ShapeDtypeStruct((1, d), F32)],
        compiler_params=_params(("arbitrary",)),
    )(dn, x, w, dres)


def _loss_head(h2, tgt, wf, *, name):
    t, d = h2.shape
    tt = _pick(t, (256,))

    def body(x_ref, t_ref, w_ref, loss_ref, dx_ref, dxb_ref, dw_ref):
        xv = x_ref[...]
        r = lax.rsqrt(jnp.mean(xv * xv, axis=-1, keepdims=True) + EPS)
        xh = xv * r
        err = xh * w_ref[...] - t_ref[...]
        lpart = 0.5 * jnp.sum(jnp.mean(err * err, axis=-1, keepdims=True), axis=0, keepdims=True)
        dy = err * (1.0 / d)
        gw = dy * w_ref[...]
        dx = r * (gw - xh * jnp.mean(gw * xh, axis=-1, keepdims=True))
        dx_ref[...] = dx
        dxb_ref[...] = dx.astype(BF16)

        @pl.when(pl.program_id(0) == 0)
        def _():
            dw_ref[...] = jnp.zeros_like(dw_ref)
            loss_ref[...] = jnp.zeros_like(loss_ref)

        dw_ref[...] += jnp.sum(dy * xh, axis=0, keepdims=True)
        loss_ref[...] += jnp.broadcast_to(lpart, loss_ref.shape)

    blk = pl.BlockSpec((tt, d), lambda i: (i, 0))
    row = pl.BlockSpec((1, d), lambda i: (0, 0))
    lrow = pl.BlockSpec((1, LANES), lambda i: (0, 0))
    return pl.pallas_call(
        body, name=name, grid=(t // tt,),
        in_specs=[blk, blk, row], out_specs=[lrow, blk, blk, row],
        out_shape=[jax.ShapeDtypeStruct((1, LANES), F32), jax.ShapeDtypeStruct((t, d), F32),
                   jax.ShapeDtypeStruct((t, d), BF16), jax.ShapeDtypeStruct((1, d), F32)],
        compiler_params=_params(("arbitrary",)),
    )(h2, tgt, wf)


def _ffn_in(hn, w_g, w_u, *, name, tm, tn):
    t, d = hn.shape
    f = w_g.shape[1]
    tm = min(tm, t)

    def body(a_ref, g_ref, u_ref, gt_ref, up_ref, act_ref):
        a = a_ref[...]
        g = jnp.dot(a, g_ref[...], preferred_element_type=F32)
        u = jnp.dot(a, u_ref[...], preferred_element_type=F32)
        gt_ref[...] = g.astype(BF16)
        up_ref[...] = u.astype(BF16)
        act_ref[...] = (g * _sigmoid(g) * u).astype(BF16)

    wblk = pl.BlockSpec((d, tn), lambda j, i: (0, j))
    oblk = pl.BlockSpec((tm, tn), lambda j, i: (i, j))
    return pl.pallas_call(
        body, name=name, grid=(f // tn, t // tm),
        in_specs=[pl.BlockSpec((tm, d), lambda j, i: (i, 0)), wblk, wblk], out_specs=[oblk] * 3,
        out_shape=[jax.ShapeDtypeStruct((t, f), BF16)] * 3,
        compiler_params=_params(("parallel", "parallel")),
    )(hn, w_g, w_u)


def _ffn_back(dh2b, w_d, gt, up, *, name, tm, tn):
    t, d = dh2b.shape
    f = w_d.shape[0]
    tm = min(tm, t)

    def body(a_ref, w_ref, g_ref, u_ref, dg_ref, du_ref):
        da = lax.dot_general(a_ref[...], w_ref[...], (((1,), (1,)), ((), ())), preferred_element_type=F32)
        g = g_ref[...].astype(F32)
        sg = _sigmoid(g)
        dg_ref[...] = (da * u_ref[...].astype(F32) * sg * (1.0 + g * (1.0 - sg))).astype(BF16)
        du_ref[...] = (da * g * sg).astype(BF16)

    oblk = pl.BlockSpec((tm, tn), lambda j, i: (i, j))
    return pl.pallas_call(
        body, name=name, grid=(f // tn, t // tm),
        in_specs=[pl.BlockSpec((tm, d), lambda j, i: (i, 0)), pl.BlockSpec((tn, d), lambda j, i: (j, 0)), oblk, oblk],
        out_specs=[oblk] * 2,
        out_shape=[jax.ShapeDtypeStruct((t, f), BF16)] * 2,
        compiler_params=_params(("parallel", "parallel")),
    )(dh2b, w_d, gt, up)


def _conv_cols(xs, cw_ref, cb, tt):
    cols = slice(cb * LANES, (cb + 1) * LANES)
    base = HALO - (CONV_W - 1)
    acc = xs[base:base + tt, cols] * cw_ref[0:1, cols]
    for j in range(1, CONV_W):
        acc = acc + xs[base + j:base + j + tt, cols] * cw_ref[j:j + 1, cols]
    return acc


def _fill_halo(xs, x_ref, xp_ref, tt):
    first = pl.program_id(0) == 0
    xs[0:HALO, :] = jnp.where(first, 0.0, xp_ref[...])
    xs[HALO:HALO + tt, :] = x_ref[...]


def _conv_specs(tt, ch):
    cur = pl.BlockSpec((tt, ch), lambda i: (i, 0))
    prev = pl.BlockSpec((HALO, ch), lambda i: (jnp.maximum(i * (tt // HALO) - 1, 0), 0))
    return cur, prev


def _conv_fwd(proj, conv_w, *, name):
    t = proj.shape[0]
    ch = conv_w.shape[1]
    nqk = HEADS * DK
    tt = _pick(t, (256,))
    cur, prev = _conv_specs(tt, ch)

    def body(x_ref, xp_ref, cw_ref, q_ref, k_ref, v_ref, xs):
        _fill_halo(xs, x_ref, xp_ref, tt)
        for cb in range(ch // LANES):
            c = _conv_cols(xs, cw_ref, cb, tt)
            s = c * _sigmoid(c)
            if cb < 2 * HEADS:
                s = s * lax.rsqrt(jnp.sum(s * s, axis=-1, keepdims=True) + EPS)
                if cb < HEADS:
                    q_ref[:, cb * LANES:(cb + 1) * LANES] = s * (DK ** -0.5)
                else:
                    k_ref[:, (cb - HEADS) * LANES:(cb - HEADS + 1) * LANES] = s
            else:
                v_ref[:, (cb - 2 * HEADS) * LANES:(cb - 2 * HEADS + 1) * LANES] = s

    return pl.pallas_call(
        body, name=name, grid=(t // tt,),
        in_specs=[cur, prev, pl.BlockSpec((CONV_W, ch), lambda i: (0, 0))],
        out_specs=[pl.BlockSpec((tt, nqk), lambda i: (i, 0)), pl.BlockSpec((tt, nqk), lambda i: (i, 0)),
                   pl.BlockSpec((tt, ch - 2 * nqk), lambda i: (i, 0))],
        out_shape=[jax.ShapeDtypeStruct((t, nqk), F32), jax.ShapeDtypeStruct((t, nqk), F32),
                   jax.ShapeDtypeStruct((t, ch - 2 * nqk), F32)],
        scratch_shapes=[pltpu.VMEM((HALO + tt, ch), F32)],
        compiler_params=_params(("arbitrary",)),
    )(proj, proj, conv_w)


def _conv_bwd_pre(proj, conv_w, dq, dk, dv, *, name):
    t = proj.shape[0]
    ch = conv_w.shape[1]
    nqk = HEADS * DK
    tt = _pick(t, (256,))
    cur, prev = _conv_specs(tt, ch)

    def body(x_ref, xp_ref, cw_ref, dq_ref, dk_ref, dv_ref, dc_ref, xs):
        _fill_halo(xs, x_ref, xp_ref, tt)
        for cb in range(ch // LANES):
            c = _conv_cols(xs, cw_ref, cb, tt)
            sg = _sigmoid(c)
            s = c * sg
            if cb < 2 * HEADS:
                if cb < HEADS:
                    d, scale = dq_ref[:, cb * LANES:(cb + 1) * LANES], DK ** -0.5
                else:
                    d, scale = dk_ref[:, (cb - HEADS) * LANES:(cb - HEADS + 1) * LANES], 1.0
                rinv = lax.rsqrt(jnp.sum(s * s, axis=-1, keepdims=True) + EPS)
                ds = scale * rinv * (d - s * (rinv * rinv) * jnp.sum(d * s, axis=-1, keepdims=True))
            else:
                ds = dv_ref[:, (cb - 2 * HEADS) * LANES:(cb - 2 * HEADS + 1) * LANES]
            dc_ref[:, cb * LANES:(cb + 1) * LANES] = ds * sg * (1.0 + c * (1.0 - sg))

    return pl.pallas_call(
        body, name=name, grid=(t // tt,),
        in_specs=[cur, prev, pl.BlockSpec((CONV_W, ch), lambda i: (0, 0)),
                  pl.BlockSpec((tt, nqk), lambda i: (i, 0)), pl.BlockSpec((tt, nqk), lambda i: (i, 0)),
                  pl.BlockSpec((tt, ch - 2 * nqk), lambda i: (i, 0))],
        out_specs=pl.BlockSpec((tt, ch), lambda i: (i, 0)),
        out_shape=jax.ShapeDtypeStruct((t, ch), F32),
        scratch_shapes=[pltpu.VMEM((HALO + tt, ch), F32)],
        compiler_params=_params(("arbitrary",)),
    )(proj, proj, conv_w, dq, dk, dv)


def _conv_bwd(proj, dc, conv_w, *, name):
    t = proj.shape[0]
    ch = conv_w.shape[1]
    tt = _pick(t, (256,))
    nt = t // tt
    cur, prev = _conv_specs(tt, ch)
    nxt = pl.BlockSpec((HALO, ch), lambda i: (jnp.minimum((i + 1) * (tt // HALO), t // HALO - 1), 0))

    def body(x_ref, xp_ref, d_ref, dn_ref, cw_ref, dx_ref, dw_ref, xs, ds):
        _fill_halo(xs, x_ref, xp_ref, tt)
        last = pl.program_id(0) == nt - 1
        ds[0:tt, :] = d_ref[...]
        ds[tt:tt + HALO, :] = jnp.where(last, 0.0, dn_ref[...])

        @pl.when(pl.program_id(0) == 0)
        def _():
            dw_ref[...] = jnp.zeros_like(dw_ref)

        base = HALO - (CONV_W - 1)
        for cb in range(ch // LANES):
            cols = slice(cb * LANES, (cb + 1) * LANES)
            d0 = ds[0:tt, cols]
            acc = None
            for j in range(CONV_W):
                sh = CONV_W - 1 - j
                term = ds[sh:sh + tt, cols] * cw_ref[j:j + 1, cols]
                acc = term if acc is None else acc + term
                dw_ref[j:j + 1, cols] += jnp.sum(d0 * xs[base + j:base + j + tt, cols], axis=0, keepdims=True)
            dx_ref[:, cols] = acc.astype(BF16)

    return pl.pallas_call(
        body, name=name, grid=(nt,),
        in_specs=[cur, prev, cur, nxt, pl.BlockSpec((CONV_W, ch), lambda i: (0, 0))],
        out_specs=[pl.BlockSpec((tt, ch), lambda i: (i, 0)), pl.BlockSpec((HALO, ch), lambda i: (0, 0))],
        out_shape=[jax.ShapeDtypeStruct((t, ch), BF16), jax.ShapeDtypeStruct((HALO, ch), F32)],
        scratch_shapes=[pltpu.VMEM((HALO + tt, ch), F32), pltpu.VMEM((tt + HALO, ch), F32)],
        compiler_params=_params(("arbitrary",)),
    )(proj, proj, dc, dc, conv_w)


def _bg_fwd(projs, alog_row, dtb_row, *, name):
    t = projs.shape[0]
    tt = _pick(t, (512, 256))

    def body(p_ref, al_ref, db_ref, o_ref):
        p = p_ref[...]
        lane = lax.broadcasted_iota(jnp.int32, p.shape, 1)
        z = p + db_ref[...]
        sp = jnp.maximum(z, 0.0) + jnp.log1p(jnp.exp(-jnp.abs(z)))
        g = -jnp.exp(al_ref[...]) * sp
        o_ref[...] = jnp.where(lane < HEADS, _sigmoid(p), jnp.where(lane < 2 * HEADS, g, 0.0))

    blk = pl.BlockSpec((tt, LANES), lambda i: (i, 0))
    row = pl.BlockSpec((1, LANES), lambda i: (0, 0))
    return pl.pallas_call(
        body, name=name, grid=(t // tt,), in_specs=[blk, row, row], out_specs=blk,
        out_shape=jax.ShapeDtypeStruct((t, LANES), F32), compiler_params=_params(("parallel",)),
    )(projs, alog_row, dtb_row)


def _bg_bwd(projs, dbg, alog_row, dtb_row, *, name):
    t = projs.shape[0]
    tt = _pick(t, (512, 256))

    def body(p_ref, d_ref, al_ref, db_ref, o_ref, dal_ref, ddb_ref):
        p, d = p_ref[...], d_ref[...]
        lane = lax.broadcasted_iota(jnp.int32, p.shape, 1)
        isg = (lane >= HEADS) & (lane < 2 * HEADS)
        be = _sigmoid(p)
        z = p + db_ref[...]
        sp = jnp.maximum(z, 0.0) + jnp.log1p(jnp.exp(-jnp.abs(z)))
        ea = jnp.exp(al_ref[...])
        d_aa = jnp.where(isg, d * (-ea) * _sigmoid(z), 0.0)
        o_ref[...] = jnp.where(lane < HEADS, d * be * (1.0 - be), d_aa).astype(BF16)

        @pl.when(pl.program_id(0) == 0)
        def _():
            dal_ref[...] = jnp.zeros_like(dal_ref)
            ddb_ref[...] = jnp.zeros_like(ddb_ref)

        dal_ref[...] += jnp.sum(jnp.where(isg, d * (-ea) * sp, 0.0), axis=0, keepdims=True)
        ddb_ref[...] += jnp.sum(d_aa, axis=0, keepdims=True)

    blk = pl.BlockSpec((tt, LANES), lambda i: (i, 0))
    row = pl.BlockSpec((1, LANES), lambda i: (0, 0))
    return pl.pallas_call(
        body, name=name, grid=(t // tt,), in_specs=[blk, blk, row, row], out_specs=[blk, row, row],
        out_shape=[jax.ShapeDtypeStruct((t, LANES), BF16), jax.ShapeDtypeStruct((1, LANES), F32),
                   jax.ShapeDtypeStruct((1, LANES), F32)],
        compiler_params=_params(("arbitrary",)),
    )(projs, dbg, alog_row, dtb_row)


def _hn_specs(n_chunks, rev=False):
    def nn(n):
        return n_chunks - 1 - n if rev else n
    tok = lambda w: pl.BlockSpec((CHUNK, HB * w), lambda g, n: (nn(n), g))
    per = lambda a, b: pl.BlockSpec((HB, None, a, b), lambda g, n: (g, nn(n), 0, 0))
    return tok, per


def _hcols(hh, w):
    return slice(hh * w, (hh + 1) * w)


def _decay(gc_col, gc_row):
    r, c = _iota2(CHUNK)
    d = jnp.exp(jnp.minimum(gc_col - gc_row, 0.0))
    return jnp.where(r >= c, d, 0.0), jnp.where(r > c, d, 0.0)


def _gdn_prep(q, k, v, beta_r, g_r, *, name):
    t = q.shape[0]
    n_chunks = t // CHUNK
    tok, per = _hn_specs(n_chunks)

    def body(q_ref, k_ref, v_ref, b_ref, g_ref, gc_ref, ti_ref, u_ref, w_ref, p_ref):
        r, c = _iota2(CHUNK)
        hs = range(HB)
        kk = [k_ref[:, _hcols(h, DK)] for h in hs]
        gc_col = [jnp.sum(jnp.where(c <= r, jnp.broadcast_to(g_ref[h], (CHUNK, CHUNK)), 0.0), axis=1, keepdims=True)
                  for h in hs]
        gc_row = [_col_to_row(gc_col[h]) for h in hs]
        beta_col = [_row_to_col(b_ref[h]) for h in hs]
        dec = [_decay(gc_col[h], gc_row[h]) for h in hs]
        kb = [kk[h] * beta_col[h] for h in hs]
        for h in hs:
            gc_ref[h] = gc_row[h]
            p_ref[h] = _dot_nt(q_ref[:, _hcols(h, DK)], kk[h]) * dec[h][0]
        pw = [_dot_nt(kb[h], kk[h]) * dec[h][1] for h in hs]
        y = [-pw[h] for h in hs]
        for _ in range(5):
            pw = [_dot(pw[h], pw[h]) for h in hs]
            yp = [_dot(y[h], pw[h]) for h in hs]
            y = [y[h] + pw[h] + yp[h] for h in hs]
        vb = [v_ref[:, _hcols(h, DV)] * beta_col[h] for h in hs]
        kbe = [kb[h] * jnp.exp(gc_col[h]) for h in hs]
        yv = [_dot(y[h], vb[h]) for h in hs]
        yk = [_dot(y[h], kbe[h]) for h in hs]
        for h in hs:
            ti_ref[h] = y[h]
            u_ref[:, _hcols(h, DV)] = vb[h] + yv[h]
            w_ref[:, _hcols(h, DK)] = kbe[h] + yk[h]

    return pl.pallas_call(
        body, name=name, grid=(HEADS // HB, n_chunks),
        in_specs=[tok(DK), tok(DK), tok(DV), per(1, CHUNK), per(1, CHUNK)],
        out_specs=[per(1, CHUNK), per(CHUNK, CHUNK), tok(DV), tok(DK), per(CHUNK, CHUNK)],
        out_shape=[jax.ShapeDtypeStruct((HEADS, n_chunks, 1, CHUNK), F32),
                   jax.ShapeDtypeStruct((HEADS, n_chunks, CHUNK, CHUNK), F32),
                   jax.ShapeDtypeStruct((t, HEADS * DV), F32), jax.ShapeDtypeStruct((t, HEADS * DK), F32),
                   jax.ShapeDtypeStruct((HEADS, n_chunks, CHUNK, CHUNK), F32)],
        compiler_params=_params(("parallel", "parallel")),
    )(q, k, v, beta_r, g_r)


def _gdn_scan(q, k, u, w, p, gc_r, *, name):
    t = q.shape[0]
    n_chunks = t // CHUNK
    tok, per = _hn_specs(n_chunks)

    def body(q_ref, k_ref, u_ref, w_ref, p_ref, gc_ref, o_ref, s_ref, st):
        @pl.when(pl.program_id(1) == 0)
        def _():
            st[...] = jnp.zeros_like(st)

        hs = range(HB)
        s = [st[h] for h in hs]
        gc_row = [gc_ref[h] for h in hs]
        gc_col = [_row_to_col(gc_row[h]) for h in hs]
        glast = [gc_row[h][:, CHUNK - 1:CHUNK] for h in hs]
        for h in hs:
            s_ref[h] = s[h]
        ws = [_dot(w_ref[:, _hcols(h, DK)], s[h]) for h in hs]
        qs = [_dot(q_ref[:, _hcols(h, DK)] * jnp.exp(gc_col[h]), s[h]) for h in hs]
        vn = [u_ref[:, _hcols(h, DV)] - ws[h] for h in hs]
        pv = [_dot(p_ref[h], vn[h]) for h in hs]
        kv = [_dot_tn(k_ref[:, _hcols(h, DK)] * jnp.exp(glast[h] - gc_col[h]), vn[h]) for h in hs]
        for h in hs:
            o_ref[:, _hcols(h, DV)] = qs[h] + pv[h]
            st[h] = s[h] * jnp.exp(glast[h]) + kv[h]

    return pl.pallas_call(
        body, name=name, grid=(HEADS // HB, n_chunks),
        in_specs=[tok(DK), tok(DK), tok(DV), tok(DK), per(CHUNK, CHUNK), per(1, CHUNK)],
        out_specs=[tok(DV), per(DK, DV)],
        out_shape=[jax.ShapeDtypeStruct((t, HEADS * DV), F32),
                   jax.ShapeDtypeStruct((HEADS, n_chunks, DK, DV), F32)],
        scratch_shapes=[pltpu.VMEM((HB, DK, DV), F32)],
        compiler_params=_params(("arbitrary", "arbitrary")),
    )(q, k, u, w, p, gc_r)


def _gdn_bwd(q, k, v, beta_r, gc_r, ti, u, w, s_all, do, *, name):
    t = q.shape[0]
    n_chunks = t // CHUNK
    tok, per = _hn_specs(n_chunks, rev=True)

    def body(q_ref, k_ref, v_ref, b_ref, gc_ref, ti_ref, u_ref, w_ref, s_ref, do_ref,
             dq_ref, dk_ref, dv_ref, db_ref, dg_ref, dst):
        @pl.when(pl.program_id(1) == 0)
        def _():
            dst[...] = jnp.zeros_like(dst)

        r, c = _iota2(CHUNK)
        rows = lax.broadcasted_iota(jnp.int32, (CHUNK, 1), 0)
        rsum = lambda a: jnp.sum(a, axis=1, keepdims=True)
        hs = range(HB)
        qq = [q_ref[:, _hcols(h, DK)] for h in hs]
        kk = [k_ref[:, _hcols(h, DK)] for h in hs]
        ww = [w_ref[:, _hcols(h, DK)] for h in hs]
        uu = [u_ref[:, _hcols(h, DV)] for h in hs]
        d_o = [do_ref[:, _hcols(h, DV)] for h in hs]
        s = [s_ref[h] for h in hs]
        d_s = [dst[h] for h in hs]
        gc_row = [gc_ref[h] for h in hs]
        gc_col = [_row_to_col(gc_row[h]) for h in hs]
        beta_col = [_row_to_col(b_ref[h]) for h in hs]
        eg = [jnp.exp(gc_col[h]) for h in hs]
        glast = [gc_row[h][:, CHUNK - 1:CHUNK] for h in hs]
        eglast = [jnp.exp(glast[h]) for h in hs]
        e2 = [jnp.exp(glast[h] - gc_col[h]) for h in hs]
        dec = [_decay(gc_col[h], gc_row[h]) for h in hs]
        kb = [kk[h] * beta_col[h] for h in hs]
        ke = [kk[h] * e2[h] for h in hs]
        qe = [qq[h] * eg[h] for h in hs]
        kkt = [_dot_nt(kb[h], kk[h]) for h in hs]
        qk = [_dot_nt(qq[h], kk[h]) for h in hs]
        ws = [_dot(ww[h], s[h]) for h in hs]
        a = [kkt[h] * dec[h][1] for h in hs]
        pp = [qk[h] * dec[h][0] for h in hs]
        vn = [uu[h] - ws[h] for h in hs]
        t1 = [_dot_tn(pp[h], d_o[h]) for h in hs]
        t2 = [_dot(ke[h], d_s[h]) for h in hs]
        dqe = [_dot_nt(d_o[h], s[h]) for h in hs]
        dke = [_dot_nt(vn[h], d_s[h]) for h in hs]
        dqk = [_dot_nt(d_o[h], vn[h]) * dec[h][0] for h in hs]
        dvn = [t1[h] + t2[h] for h in hs]
        t3 = [_dot_tn(qe[h], d_o[h]) for h in hs]
        t4 = [_dot_tn(ww[h], dvn[h]) for h in hs]
        dw = [-_dot_nt(dvn[h], s[h]) for h in hs]
        de2 = [rsum(dke[h] * ke[h]) for h in hs]
        dglast = [jnp.sum(de2[h], axis=0, keepdims=True)
                  + eglast[h] * jnp.sum(rsum(d_s[h] * s[h]), axis=0, keepdims=True) for h in hs]
        for h in hs:
            dst[h] = d_s[h] * eglast[h] + t3[h] - t4[h]
        yy = [ti_ref[h] for h in hs]
        t5 = [_dot_tn(yy[h], dvn[h]) for h in hs]
        t6 = [_dot_tn(yy[h], dw[h]) for h in hs]
        dvb = [dvn[h] + t5[h] for h in hs]
        dkbe = [dw[h] + t6[h] for h in hs]
        t7 = [_dot_nt(dvb[h], uu[h]) for h in hs]
        t8 = [_dot_nt(dkbe[h], ww[h]) for h in hs]
        d_a = [jnp.where(r > c, -(t7[h] + t8[h]), 0.0) for h in hs]
        dm = [d_a[h] * dec[h][1] for h in hs]
        t9 = [_dot(dm[h], kk[h]) for h in hs]
        t10 = [_dot(dqk[h], kk[h]) for h in hs]
        t11 = [_dot_tn(dqk[h], qq[h]) for h in hs]
        t12 = [_dot_tn(dm[h], kb[h]) for h in hs]
        for h in hs:
            dkb = t9[h] + dkbe[h] * eg[h]
            e_mat = d_a[h] * a[h] + dqk[h] * qk[h]
            dgc = (rsum(dqe[h] * qe[h]) - de2[h] + rsum(dkbe[h] * kb[h] * eg[h]) + rsum(e_mat)
                   - _row_to_col(jnp.sum(e_mat, axis=0, keepdims=True)))
            dgc = dgc + jnp.where(rows == CHUNK - 1, dglast[h], 0.0)
            dq_ref[:, _hcols(h, DK)] = dqe[h] * eg[h] + t10[h]
            dk_ref[:, _hcols(h, DK)] = t11[h] + dke[h] * e2[h] + t12[h] + dkb * beta_col[h]
            dv_ref[:, _hcols(h, DV)] = dvb[h] * beta_col[h]
            dbeta = rsum(dkb * kk[h]) + rsum(dvb[h] * v_ref[:, _hcols(h, DV)])
            db_ref[h] = _col_to_row(dbeta)
            dg_ref[h] = jnp.sum(jnp.where(r >= c, jnp.broadcast_to(dgc, (CHUNK, CHUNK)), 0.0), axis=0, keepdims=True)

    return pl.pallas_call(
        body, name=name, grid=(HEADS // HB, n_chunks),
        in_specs=[tok(DK), tok(DK), tok(DV), per(1, CHUNK), per(1, CHUNK), per(CHUNK, CHUNK), tok(DV), tok(DK),
                  per(DK, DV), tok(DV)],
        out_specs=[tok(DK), tok(DK), tok(DV), per(1, CHUNK), per(1, CHUNK)],
        out_shape=[jax.ShapeDtypeStruct((t, HEADS * DK), F32), jax.ShapeDtypeStruct((t, HEADS * DK), F32),
                   jax.ShapeDtypeStruct((t, HEADS * DV), F32),
                   jax.ShapeDtypeStruct((HEADS, n_chunks, 1, CHUNK), F32),
                   jax.ShapeDtypeStruct((HEADS, n_chunks, 1, CHUNK), F32)],
        scratch_shapes=[pltpu.VMEM((HB, DK, DV), F32)],
        compiler_params=_params(("arbitrary", "arbitrary")),
    )(q, k, v, beta_r, gc_r, ti, u, w, s_all, do)


def _rot(x, cos2, sin2):
    return x * cos2 + pltpu.roll(x, DK // 2, 1) * sin2


def _unrot(d, cos2, sin2):
    return d * cos2 + pltpu.roll(d * sin2, DK // 2, 1)


def _ret_consts(lg):
    r, c = _iota2(CHUNK)
    dm = jnp.where(r >= c, jnp.exp((r - c).astype(F32) * lg), 0.0)
    pos = lax.broadcasted_iota(jnp.int32, (CHUNK, 1), 0).astype(F32)
    return dm, jnp.exp((pos + 1.0) * lg), jnp.exp((CHUNK - 1.0 - pos) * lg), jnp.exp(CHUNK * lg)


def _ret_fwd(proj, cos2, sin2, lg_tab, col_q, col_k, col_v, *, name):
    t = proj.shape[0]
    n_chunks = t // CHUNK
    tok, per = _hn_specs(n_chunks)
    bq, bk, bv = col_q // (HB * DK), col_k // (HB * DK), col_v // (HB * DV)

    def body(q_ref, k_ref, v_ref, cos_ref, sin_ref, lg_ref, o_ref, s_ref, st):
        @pl.when(pl.program_id(1) == 0)
        def _():
            st[...] = jnp.zeros_like(st)

        cos2v, sin2v = cos_ref[...], sin_ref[...]
        hs = range(HB)
        s = [st[h] for h in hs]
        for h in hs:
            s_ref[h] = s[h]
        cst = [_ret_consts(lg_ref[h][:, 0:1]) for h in hs]
        qq = [_rot(q_ref[:, _hcols(h, DK)], cos2v, sin2v) for h in hs]
        kk = [_rot(k_ref[:, _hcols(h, DK)], cos2v, sin2v) * (DK ** -0.5) for h in hs]
        vv = [v_ref[:, _hcols(h, DV)] for h in hs]
        sc = [_dot_nt(qq[h], kk[h]) * cst[h][0] for h in hs]
        qs = [_dot(qq[h], s[h]) for h in hs]
        kv = [_dot_tn(kk[h] * cst[h][2], vv[h]) for h in hs]
        sv = [_dot(sc[h], vv[h]) for h in hs]
        for h in hs:
            o_ref[:, _hcols(h, DV)] = sv[h] + qs[h] * cst[h][1]
            st[h] = s[h] * cst[h][3] + kv[h]

    return pl.pallas_call(
        body, name=name, grid=(HEADS // HB, n_chunks),
        in_specs=[pl.BlockSpec((CHUNK, HB * DK), lambda g, n: (n, bq + g)),
                  pl.BlockSpec((CHUNK, HB * DK), lambda g, n: (n, bk + g)),
                  pl.BlockSpec((CHUNK, HB * DV), lambda g, n: (n, bv + g)),
                  pl.BlockSpec((CHUNK, DK), lambda g, n: (n, 0)), pl.BlockSpec((CHUNK, DK), lambda g, n: (n, 0)),
                  pl.BlockSpec((HB, 1, LANES), lambda g, n: (g, 0, 0))],
        out_specs=[tok(DV), per(DK, DV)],
        out_shape=[jax.ShapeDtypeStruct((t, HEADS * DV), F32),
                   jax.ShapeDtypeStruct((HEADS, n_chunks, DK, DV), F32)],
        scratch_shapes=[pltpu.VMEM((HB, DK, DV), F32)],
        compiler_params=_params(("arbitrary", "arbitrary")),
    )(proj, proj, proj, cos2, sin2, lg_tab)


def _ret_bwd(proj, cos2, sin2, lg_tab, s_all, do, col_q, col_k, col_v, *, name):
    t = proj.shape[0]
    n_chunks = t // CHUNK
    tok, per = _hn_specs(n_chunks, rev=True)
    bq, bk, bv = col_q // (HB * DK), col_k // (HB * DK), col_v // (HB * DV)
    rv = lambda n: n_chunks - 1 - n

    def body(q_ref, k_ref, v_ref, cos_ref, sin_ref, lg_ref, s_ref, do_ref, dq_ref, dk_ref, dv_ref, dst):
        @pl.when(pl.program_id(1) == 0)
        def _():
            dst[...] = jnp.zeros_like(dst)

        cos2v, sin2v = cos_ref[...], sin_ref[...]
        hs = range(HB)
        s = [s_ref[h] for h in hs]
        d_s = [dst[h] for h in hs]
        d_o = [do_ref[:, _hcols(h, DV)] for h in hs]
        cst = [_ret_consts(lg_ref[h][:, 0:1]) for h in hs]
        qq = [_rot(q_ref[:, _hcols(h, DK)], cos2v, sin2v) for h in hs]
        kk = [_rot(k_ref[:, _hcols(h, DK)], cos2v, sin2v) * (DK ** -0.5) for h in hs]
        vv = [v_ref[:, _hcols(h, DV)] for h in hs]
        dxo = [d_o[h] * cst[h][1] for h in hs]
        sc = [_dot_nt(qq[h], kk[h]) * cst[h][0] for h in hs]
        dsc = [_dot_nt(d_o[h], vv[h]) * cst[h][0] for h in hs]
        t1 = [_dot(kk[h] * cst[h][2], d_s[h]) for h in hs]
        t2 = [_dot_nt(dxo[h], s[h]) for h in hs]
        t3 = [_dot_nt(vv[h], d_s[h]) for h in hs]
        t4 = [_dot_tn(qq[h], dxo[h]) for h in hs]
        t5 = [_dot_tn(sc[h], d_o[h]) for h in hs]
        t6 = [_dot(dsc[h], kk[h]) for h in hs]
        t7 = [_dot_tn(dsc[h], qq[h]) for h in hs]
        for h in hs:
            dst[h] = d_s[h] * cst[h][3] + t4[h]
            dv_ref[:, _hcols(h, DV)] = (t5[h] + t1[h]).astype(BF16)
            dq_ref[:, _hcols(h, DK)] = _unrot(t6[h] + t2[h], cos2v, sin2v).astype(BF16)
            dk_ref[:, _hcols(h, DK)] = (_unrot(t7[h] + t3[h] * cst[h][2], cos2v, sin2v) * (DK ** -0.5)).astype(BF16)

    return pl.pallas_call(
        body, name=name, grid=(HEADS // HB, n_chunks),
        in_specs=[pl.BlockSpec((CHUNK, HB * DK), lambda g, n: (rv(n), bq + g)),
                  pl.BlockSpec((CHUNK, HB * DK), lambda g, n: (rv(n), bk + g)),
                  pl.BlockSpec((CHUNK, HB * DV), lambda g, n: (rv(n), bv + g)),
                  pl.BlockSpec((CHUNK, DK), lambda g, n: (rv(n), 0)),
                  pl.BlockSpec((CHUNK, DK), lambda g, n: (rv(n), 0)),
                  pl.BlockSpec((HB, 1, LANES), lambda g, n: (g, 0, 0)),
                  per(DK, DV), tok(DV)],
        out_specs=[tok(DK), tok(DK), tok(DV)],
        out_shape=[jax.ShapeDtypeStruct((t, HEADS * DK), BF16), jax.ShapeDtypeStruct((t, HEADS * DK), BF16),
                   jax.ShapeDtypeStruct((t, HEADS * DV), BF16)],
        scratch_shapes=[pltpu.VMEM((HB, DK, DV), F32)],
        compiler_params=_params(("arbitrary", "arbitrary")),
    )(proj, proj, proj, cos2, sin2, lg_tab, s_all, do)


def _merge_parts(oa, ob, z, rg, ga, gb):
    ra = lax.rsqrt(jnp.mean(oa * oa, axis=-1, keepdims=True) + EPS)
    xa = oa * ra
    mu = jnp.mean(ob, axis=-1, keepdims=True)
    cen = ob - mu
    rb = lax.rsqrt(jnp.mean(cen * cen, axis=-1, keepdims=True) + EPS)
    xb = cen * rb
    sz, sr = _sigmoid(z), _sigmoid(rg)
    return ra, xa, rb, xb, sz, sr, _sigmoid(ga), _sigmoid(gb)


def _merge_specs(tt, d, cz, crg, cga, cgb):
    blk = pl.BlockSpec((tt, d), lambda i: (i, 0))
    pcol = lambda c: pl.BlockSpec((tt, d), lambda i: (i, c // d))
    row = pl.BlockSpec((1, d), lambda i: (0, 0))
    return blk, [blk, blk, pcol(cz), pcol(crg), pcol(cga), pcol(cgb), row, row], row


def _merge_fwd(oa, ob, proj, wa, wb, cz, crg, cga, cgb, *, name):
    t, d = oa.shape
    tt = _pick(t, (256,))
    blk, in_specs, _ = _merge_specs(tt, d, cz, crg, cga, cgb)

    def body(oa_ref, ob_ref, z_ref, rg_ref, ga_ref, gb_ref, wa_ref, wb_ref, o_ref):
        for h in range(HEADS):
            cols = slice(h * DV, (h + 1) * DV)
            z, rg = z_ref[:, cols], rg_ref[:, cols]
            _, xa, _, xb, sz, sr, sga, sgb = _merge_parts(
                oa_ref[:, cols], ob_ref[:, cols], z, rg, ga_ref[:, cols], gb_ref[:, cols])
            o_a = xa * wa_ref[:, cols] * (z * sz)
            o_b = xb * wb_ref[:, cols] * (rg * sr)
            o_ref[:, cols] = (sga * o_a + sgb * o_b).astype(BF16)

    return pl.pallas_call(
        body, name=name, grid=(t // tt,), in_specs=in_specs, out_specs=blk,
        out_shape=jax.ShapeDtypeStruct((t, d), BF16), compiler_params=_params(("parallel",)),
    )(oa, ob, proj, proj, proj, proj, wa, wb)


def _merge_bwd(dmix, oa, ob, proj, wa, wb, cz, crg, cga, cgb, *, name):
    t, d = oa.shape
    tt = _pick(t, (256,))
    blk, in_specs, row = _merge_specs(tt, d, cz, crg, cga, cgb)

    def body(dm_ref, oa_ref, ob_ref, z_ref, rg_ref, ga_ref, gb_ref, wa_ref, wb_ref,
             doa_ref, dob_ref, dz_ref, drg_ref, dga_ref, dgb_ref, dwa_ref, dwb_ref):
        @pl.when(pl.program_id(0) == 0)
        def _():
            dwa_ref[...] = jnp.zeros_like(dwa_ref)
            dwb_ref[...] = jnp.zeros_like(dwb_ref)

        for h in range(HEADS):
            cols = slice(h * DV, (h + 1) * DV)
            z, rg, wa_h, wb_h, dmx = z_ref[:, cols], rg_ref[:, cols], wa_ref[:, cols], wb_ref[:, cols], dm_ref[:, cols]
            ra, xa, rb, xb, sz, sr, sga, sgb = _merge_parts(
                oa_ref[:, cols], ob_ref[:, cols], z, rg, ga_ref[:, cols], gb_ref[:, cols])
            na, nb = xa * wa_h, xb * wb_h
            sil_z, sil_r = z * sz, rg * sr
            o_a, o_b = na * sil_z, nb * sil_r
            dga_ref[:, cols] = (dmx * o_a * sga * (1.0 - sga)).astype(BF16)
            dgb_ref[:, cols] = (dmx * o_b * sgb * (1.0 - sgb)).astype(BF16)
            d_oa, d_ob = dmx * sga, dmx * sgb
            dz_ref[:, cols] = (d_oa * na * sz * (1.0 + z * (1.0 - sz))).astype(BF16)
            drg_ref[:, cols] = (d_ob * nb * sr * (1.0 + rg * (1.0 - sr))).astype(BF16)
            dna, dnb = d_oa * sil_z, d_ob * sil_r
            dwa_ref[:, cols] += jnp.sum(dna * xa, axis=0, keepdims=True)
            dwb_ref[:, cols] += jnp.sum(dnb * xb, axis=0, keepdims=True)
            gwa, gwb = dna * wa_h, dnb * wb_h
            doa_ref[:, cols] = ra * (gwa - xa * jnp.mean(gwa * xa, axis=-1, keepdims=True))
            dob_ref[:, cols] = rb * (gwb - jnp.mean(gwb, axis=-1, keepdims=True)
                                     - xb * jnp.mean(gwb * xb, axis=-1, keepdims=True))

    return pl.pallas_call(
        body, name=name, grid=(t // tt,), in_specs=[blk] + in_specs,
        out_specs=[blk, blk, blk, blk, blk, blk, row, row],
        out_shape=[jax.ShapeDtypeStruct((t, d), F32)] * 2 + [jax.ShapeDtypeStruct((t, d), BF16)] * 4
        + [jax.ShapeDtypeStruct((1, d), F32)] * 2,
        compiler_params=_params(("arbitrary",)),
    )(dmix, oa, ob, proj, proj, proj, proj, wa, wb)


def _row_block(rows, cols, itemsize=4, target=1 << 20):
    for rb in (512, 256, 128, 64, 32, 16, 8):
        if rows % rb == 0 and rb * cols * itemsize <= target:
            return rb
    return rows


def _adamw(w, g, m, v, *, name):
    rows, cols = w.shape
    rb = _row_block(rows, cols)

    def body(w_ref, g_ref, m_ref, v_ref, d_ref, nm_ref, nv_ref):
        gg = g_ref[...]
        mm = ADAM_B1 * m_ref[...] + (1.0 - ADAM_B1) * gg
        vv = ADAM_B2 * v_ref[...] + (1.0 - ADAM_B2) * (gg * gg)
        m_hat = mm / (1.0 - ADAM_B1 ** ADAM_STEP)
        v_hat = vv / (1.0 - ADAM_B2 ** ADAM_STEP)
        d_ref[...] = -ADAM_LR * (m_hat / (jnp.sqrt(v_hat) + ADAM_EPS) + ADAM_WD * w_ref[...])
        nm_ref[...] = mm
        nv_ref[...] = vv

    blk = pl.BlockSpec((rb, cols), lambda i: (i, 0))
    return pl.pallas_call(
        body, name=name, grid=(rows // rb,), in_specs=[blk] * 4, out_specs=[blk] * 3,
        out_shape=[jax.ShapeDtypeStruct((rows, cols), F32)] * 3, compiler_params=_params(("parallel",)),
    )(w, g, m, v)


def _adamw_halves(w, mine, other, c_idx, m, v, *, name):
    rows, cols = w.shape
    hr = rows // 2
    rb = _row_block(hr, cols)
    nb = hr // rb

    def body(c_ref, w_ref, a_ref, b_ref, m_ref, v_ref, g_ref, d_ref, nm_ref, nv_ref):
        gg = jnp.where(pl.program_id(0) // nb == c_ref[0], a_ref[...], b_ref[...])
        mm = ADAM_B1 * m_ref[...] + (1.0 - ADAM_B1) * gg
        vv = ADAM_B2 * v_ref[...] + (1.0 - ADAM_B2) * (gg * gg)
        m_hat = mm / (1.0 - ADAM_B1 ** ADAM_STEP)
        v_hat = vv / (1.0 - ADAM_B2 ** ADAM_STEP)
        g_ref[...] = gg
        d_ref[...] = -ADAM_LR * (m_hat / (jnp.sqrt(v_hat) + ADAM_EPS) + ADAM_WD * w_ref[...])
        nm_ref[...] = mm
        nv_ref[...] = vv

    blk = pl.BlockSpec((rb, cols), lambda i, c: (i, 0))
    half = pl.BlockSpec((rb, cols), lambda i, c: (i % nb, 0))
    return pl.pallas_call(
        body, name=name,
        grid_spec=pltpu.PrefetchScalarGridSpec(
            num_scalar_prefetch=1, grid=(rows // rb,), in_specs=[blk, half, half, blk, blk], out_specs=[blk] * 4),
        out_shape=[jax.ShapeDtypeStruct((rows, cols), F32)] * 4, compiler_params=_params(("parallel",)),
    )(c_idx, w, mine, other, m, v)


def _pair_add(g, rsib, c_idx, *, name):
    _, _, hr, cols = g.shape
    rb = _row_block(hr, cols, 2)

    def body(c_ref, g_ref, r_ref, o_ref):
        o_ref[...] = (g_ref[...].astype(F32) + r_ref[...].astype(F32)).astype(BF16)

    return pl.pallas_call(
        body, name=name,
        grid_spec=pltpu.PrefetchScalarGridSpec(
            num_scalar_prefetch=1, grid=(N_CHIPS, hr // rb),
            in_specs=[pl.BlockSpec((None, None, rb, cols), lambda j, i, c: (j, c[0], i, 0)),
                      pl.BlockSpec((None, rb, cols), lambda j, i, c: (j, i, 0))],
            out_specs=pl.BlockSpec((None, rb, cols), lambda j, i, c: (j, i, 0))),
        out_shape=jax.ShapeDtypeStruct((N_CHIPS, hr, cols), BF16),
        compiler_params=_params(("parallel", "parallel")),
    )(c_idx, g, rsib)


def _chip_sum(qb, *, name):
    _, hr, cols = qb.shape
    rb = _row_block(hr, cols, 2, 1 << 19)

    def body(q_ref, o_ref):
        acc = q_ref[0].astype(F32)
        for i in range(1, N_CHIPS):
            acc = acc + q_ref[i].astype(F32)
        o_ref[...] = acc

    return pl.pallas_call(
        body, name=name, grid=(hr // rb,),
        in_specs=[pl.BlockSpec((N_CHIPS, rb, cols), lambda i: (0, i, 0))],
        out_specs=pl.BlockSpec((rb, cols), lambda i: (i, 0)),
        out_shape=jax.ShapeDtypeStruct((hr, cols), F32), compiler_params=_params(("parallel",)),
    )(qb)


def _place():
    x, y, c = lax.axis_index("x"), lax.axis_index("y"), lax.axis_index("c")
    return x, y, c, [(1 - x, y), (x, 1 - y), (1 - x, 1 - y)]


ANY = pl.BlockSpec(memory_space=pl.ANY)


def _gather_side(shards):
    nw = len(shards)

    def plan(ins, outs, ssem, rsem):
        x, y, c, chips = _place()
        s_me = 2 * x + y

        def rows(w, half):
            hr = shards[w].shape[0] // 2
            return pl.ds(pl.multiple_of(half * hr, 16), hr)

        def rcopy(w, k, src, slot, half, to):
            return pltpu.make_async_remote_copy(
                src_ref=src, dst_ref=outs[w].at[slot, rows(w, half)],
                send_sem=ssem.at[w * 6 + k], recv_sem=rsem.at[w * 6 + k], device_id=to, device_id_type=MESH)

        out = []
        for w in range(nw):
            mine = ins[w].at[rows(w, c)]
            for j, chip in enumerate(chips):
                s_j = 2 * chip[0] + chip[1]
                out.append((rcopy(w, j, mine, s_me, c, (chip[0], chip[1], c)),
                            rcopy(w, j, mine, s_j, c, (x, y, c)),
                            rcopy(w, 3 + j, outs[w].at[s_j, rows(w, c)], s_j, c, (x, y, 1 - c)),
                            rcopy(w, 3 + j, mine, s_j, 1 - c, (x, y, c))))
        return out

    def start(*refs):
        for send, _, _, _ in plan(*refs):
            send.start()

    def middle(*refs):
        for _, landed, forward, _ in plan(*refs):
            landed.wait_recv()
            forward.start()

    def finish(*refs):
        for send, _, forward, from_sibling in plan(*refs):
            from_sibling.wait_recv()
            send.wait_send()
            forward.wait_send()

    return _Side(shards, [jax.ShapeDtypeStruct((N_CHIPS,) + s.shape, s.dtype) for s in shards], nw * 6,
                 [start, middle, finish])


def _run_side(side, *, name):
    n_in, n_out = len(side.ins), len(side.out_shapes)

    def body(*refs):
        ins, outs = refs[:n_in], refs[n_in:n_in + n_out]
        for phase in side.phases:
            phase(ins, outs, refs[-2], refs[-1])

    return pl.pallas_call(
        body, name=name, in_specs=[ANY] * n_in, out_specs=[ANY] * n_out, out_shape=side.out_shapes,
        scratch_shapes=side.sems(), compiler_params=pltpu.CompilerParams(has_side_effects=True),
    )(*side.ins)


def _pair_exchange(gs, *, name):
    nw = len(gs)

    def body(*refs):
        ins, outs = refs[:nw], refs[nw:2 * nw]
        ssem, rsem = refs[2 * nw:]
        x, y, c, _ = _place()
        cps = []
        for w in range(nw):
            cp = pltpu.make_async_remote_copy(
                src_ref=ins[w].at[:, 1 - c], dst_ref=outs[w], send_sem=ssem.at[w], recv_sem=rsem.at[w],
                device_id=(x, y, 1 - c), device_id_type=MESH)
            cp.start()
            cps.append(cp)
        for cp in cps:
            cp.wait()

    return pl.pallas_call(
        body, name=name, in_specs=[ANY] * nw, out_specs=[ANY] * nw,
        out_shape=[jax.ShapeDtypeStruct((g.shape[0],) + g.shape[2:], g.dtype) for g in gs],
        scratch_shapes=[pltpu.SemaphoreType.DMA((nw,)), pltpu.SemaphoreType.DMA((nw,))],
        compiler_params=pltpu.CompilerParams(has_side_effects=True),
    )(*gs)


def _scatter_side(ps):
    nw = len(ps)

    def copies(ins, outs, ssem, rsem):
        x, y, c, chips = _place()
        s_me = 2 * x + y
        return [pltpu.make_async_remote_copy(
            src_ref=ins[w].at[2 * chip[0] + chip[1]], dst_ref=outs[w].at[s_me],
            send_sem=ssem.at[w * 3 + j], recv_sem=rsem.at[w * 3 + j],
            device_id=(chip[0], chip[1], c), device_id_type=MESH)
            for w in range(nw) for j, chip in enumerate(chips)]

    def start(*refs):
        for cp in copies(*refs):
            cp.start()

    def finish(*refs):
        for cp in copies(*refs):
            cp.wait()

    return _Side(ps, [jax.ShapeDtypeStruct(p.shape, p.dtype) for p in ps], nw * 3, [start, finish])


def _pair_share(hs, *, name):
    nw = len(hs)

    def body(*refs):
        ins, outs = refs[:nw], refs[nw:2 * nw]
        ssem, rsem = refs[2 * nw:]
        x, y, c, _ = _place()
        cps = []
        for w in range(nw):
            cp = pltpu.make_async_remote_copy(
                src_ref=ins[w], dst_ref=outs[w], send_sem=ssem.at[w], recv_sem=rsem.at[w],
                device_id=(x, y, 1 - c), device_id_type=MESH)
            cp.start()
            cps.append(cp)
        for cp in cps:
            cp.wait()

    return pl.pallas_call(
        body, name=name, in_specs=[ANY] * nw, out_specs=[ANY] * nw,
        out_shape=[jax.ShapeDtypeStruct(h.shape, h.dtype) for h in hs],
        scratch_shapes=[pltpu.SemaphoreType.DMA((nw,)), pltpu.SemaphoreType.DMA((nw,))],
        compiler_params=pltpu.CompilerParams(has_side_effects=True),
    )(*hs)


def _gather_small(a, *, name):
    def body(a_ref, o_ref, ssem, rsem):
        x, y, c, chips = _place()
        s_me = 2 * x + y
        o_ref[s_me] = a_ref[...]
        cps = []
        for j, chip in enumerate(chips):
            cp = pltpu.make_async_remote_copy(
                src_ref=a_ref, dst_ref=o_ref.at[s_me], send_sem=ssem.at[j], recv_sem=rsem.at[j],
                device_id=(chip[0], chip[1], c), device_id_type=MESH)
            cp.start()
            cps.append(cp)
        for cp in cps:
            cp.wait()

    vm = pl.BlockSpec(memory_space=pltpu.VMEM)
    return pl.pallas_call(
        body, name=name, in_specs=[vm], out_specs=vm,
        out_shape=jax.ShapeDtypeStruct((N_CHIPS,) + a.shape, a.dtype),
        scratch_shapes=[pltpu.SemaphoreType.DMA((3,)), pltpu.SemaphoreType.DMA((3,))],
    )(a)


def _allreduce_small(p, *, name):
    def body(p_ref, o_ref, buf, ssem, rsem):
        x, y, c, _ = _place()
        me = 4 * x + 2 * y + c
        buf[me] = p_ref[...]
        cps = []
        for k in range(1, N_DEV):
            fx, fy, fc = (k >> 2) & 1, (k >> 1) & 1, k & 1
            peer = (x + fx - 2 * x * fx, y + fy - 2 * y * fy, c + fc - 2 * c * fc)
            cp = pltpu.make_async_remote_copy(
                src_ref=buf.at[me], dst_ref=buf.at[me], send_sem=ssem.at[k - 1], recv_sem=rsem.at[k - 1],
                device_id=peer, device_id_type=MESH)
            cp.start()
            cps.append(cp)
        for cp in cps:
            cp.wait()
        acc = buf[0]
        for i in range(1, N_DEV):
            acc = acc + buf[i]
        o_ref[...] = acc

    vm = pl.BlockSpec(memory_space=pltpu.VMEM)
    return pl.pallas_call(
        body, name=name, in_specs=[vm], out_specs=vm,
        out_shape=jax.ShapeDtypeStruct(p.shape, p.dtype),
        scratch_shapes=[pltpu.VMEM((N_DEV,) + p.shape, p.dtype), pltpu.SemaphoreType.DMA((N_DEV - 1,)),
                        pltpu.SemaphoreType.DMA((N_DEV - 1,))],
    )(p)


def _rows_to_tokens(r):
    h, n = r.shape[0], r.shape[1]
    return r.reshape(h, n * CHUNK).T


def _tokens_to_rows(a):
    t, h = a.shape
    return a.T.reshape(h, t // CHUNK, 1, CHUNK)


def kernel(x, norm1_w, w_in, conv_w, a_log, dt_bias, gdn_norm_w, ret_norm_w, w_out, norm2_w, w_gate, w_up, w_down, norm_f_w, loss_target, m_norm1_w, m_w_in, m_conv_w, m_a_log, m_dt_bias, m_gdn_norm_w, m_ret_norm_w, m_w_out, m_norm2_w, m_w_gate, m_w_up, m_w_down, m_norm_f_w, v_norm1_w, v_w_in, v_conv_w, v_a_log, v_dt_bias, v_gdn_norm_w, v_ret_norm_w, v_w_out, v_norm2_w, v_w_gate, v_w_up, v_w_down, v_norm_f_w):
    t, d = x.shape[1], x.shape[2]
    f = w_gate.shape[2] * N_CHIPS
    nqk, nv = HEADS * DK, HEADS * DV
    ncs = w_in.shape[2]
    c_idx = lax.axis_index("c")
    s_idx = 2 * lax.axis_index("x") + lax.axis_index("y")
    xs = x[0]
    tgt = loss_target[0]

    o_az = 2 * nqk + nv
    o_small = o_az + nv
    n_small = 2 * HEADS
    o_rq = o_small
    o_rk, o_rv = o_rq + nqk, o_rq + 2 * nqk
    o_rg = o_rv + nv
    o_ga = o_rg + nv
    o_gb = o_ga + d
    n_main = o_gb + d

    own = [w_in[0].astype(BF16), w_out[0].astype(BF16), w_gate[0].astype(BF16), w_up[0].astype(BF16),
           w_down[0].astype(BF16)]
    put_own = lambda full, mine: lax.dynamic_update_slice(full, mine[None], (s_idx, 0, 0))
    wg_in = put_own(_run_side(_gather_side(own[:1]), name="gather_w_in")[0], own[0])
    conv_full = _gather_small(conv_w[0], name="gather_conv_w")
    conv_full = jnp.concatenate([conv_full[i] for i in range(N_CHIPS)], axis=1)
    cuts = []
    lo_small = hi_small = None
    for s in range(N_CHIPS):
        g0, g1 = s * ncs, (s + 1) * ncs
        if g1 <= o_small or g0 >= o_small + n_small:
            cuts.append((s, 0, ncs))
        else:
            lo_small, hi_small = o_small - g0, o_small + n_small - g0
            small_shard = s
            cuts += [(s, 0, lo_small), (s, hi_small, ncs)]
    w_main = jnp.concatenate([wg_in[s][:, a:b] for s, a, b in cuts], axis=1)
    w_small = jnp.pad(wg_in[small_shard][:, lo_small:hi_small], ((0, 0), (0, LANES - n_small)))

    pad16 = lambda a: jnp.pad(a, ((0, 0), (HEADS, LANES - 2 * HEADS)))
    alog_row, dtb_row = pad16(a_log), pad16(dt_bias)
    wa_row = jnp.tile(gdn_norm_w, (1, HEADS))
    inv = ROPE_BASE ** (-jnp.arange(0, DK, 2, dtype=F32) / DK)
    ang = jnp.arange(t, dtype=F32)[:, None] * inv[None, :]
    cos2 = jnp.concatenate([jnp.cos(ang), jnp.cos(ang)], axis=1)
    sin2 = jnp.concatenate([-jnp.sin(ang), jnp.sin(ang)], axis=1)
    lg = jnp.log1p(-jnp.exp2(-5.0 - jnp.arange(HEADS, dtype=F32)))
    lg_tab = jnp.broadcast_to(lg[:, None, None], (HEADS, 1, LANES))

    fq = f // N_CHIPS
    u1 = _rms_fwd(xs, norm1_w, name="rms1_fwd")
    proj, *rest = _mm(u1, w_main, tm=512, tn=2048, tk=d, side=_gather_side(own[1:]), name="mm_proj")
    wg_out, wg_gate, wg_up, wg_down = [put_own(g, o) for g, o in zip(rest, own[1:])]
    w_o = wg_out.reshape(d, d)
    w_g = jnp.concatenate([wg_gate[i] for i in range(N_CHIPS)], axis=1)
    w_u = jnp.concatenate([wg_up[i] for i in range(N_CHIPS)], axis=1)
    w_d = wg_down.reshape(f, d)
    projs = _mm(u1, w_small, tm=1024, tn=LANES, tk=d, name="mm_proj_small")
    q_a, k_a, v_a = _conv_fwd(proj, conv_full, name="conv_fwd")
    bg = _bg_fwd(projs, alog_row, dtb_row, name="bg_fwd")
    beta_r = _tokens_to_rows(bg[:, :HEADS])
    g_r = _tokens_to_rows(bg[:, HEADS:2 * HEADS])
    gc_r, tinv, u_a, w_a, p_a = _gdn_prep(q_a, k_a, v_a, beta_r, g_r, name="gdn_prep")
    o_a, s_a = _gdn_scan(q_a, k_a, u_a, w_a, p_a, gc_r, name="gdn_scan")
    o_b, s_b = _ret_fwd(proj, cos2, sin2, lg_tab, o_rq, o_rk, o_rv, name="ret_fwd")
    mixed = _merge_fwd(o_a, o_b, proj, wa_row, ret_norm_w, o_az, o_rg, o_ga, o_gb, name="merge_fwd")
    h1 = _mm(mixed, w_o, res=xs, tm=512, tn=d, tk=d, name="mm_out")
    hn = _rms_fwd(h1, norm2_w, name="rms2_fwd")
    gt, up, act = _ffn_in(hn, w_g, w_u, tm=256, tn=fq, name="ffn_in")
    h2 = _mm(act, w_d, res=h1, tm=512, tn=d, tk=fq, name="mm_down")
    loss_row, dh2, dh2b, d_nf = _loss_head(h2, tgt, norm_f_w.reshape(1, d), name="loss_head")

    g_down = _mm(act, dh2b, ta=True, out_dtype=BF16, tm=fq, tn=d, tk=1024, name="mm_dw_down")
    dgt, dup = _ffn_back(dh2b, w_d, gt, up, tm=512, tn=fq, name="ffn_back")
    dhn = _mm(dgt, w_g, tb=True, tm=512, tn=d, tk=fq, name="mm_dhn_gate")
    dhn = _mm(dup, w_u, tb=True, res=dhn, tm=512, tn=d, tk=fq, name="mm_dhn_up")
    g_gate = _mm(hn, dgt, ta=True, out_dtype=BF16, tm=1024, tn=fq, tk=1024, name="mm_dw_gate")
    g_up = _mm(hn, dup, ta=True, out_dtype=BF16, tm=1024, tn=fq, tk=1024, name="mm_dw_up")
    dh1, dh1b, d_n2 = _rms_bwd(dhn, h1, norm2_w, dh2, name="rms2_bwd")
    dmix = _mm(dh1b, w_o, tb=True, tm=512, tn=d, tk=d, name="mm_dmix")
    g_out = _mm(mixed, dh1b, ta=True, out_dtype=BF16, tm=1024, tn=d, tk=1024, name="mm_dw_out")
    col_split = lambda g, wd: jnp.stack([g[:, i * wd:(i + 1) * wd] for i in range(N_CHIPS)])
    halves = lambda g: g.reshape(N_CHIPS, 2, g.shape[1] // 2, g.shape[2])
    c_arr = jnp.reshape(c_idx, (1,)).astype(jnp.int32)
    gs_ffn = [halves(g_out.reshape(N_CHIPS, d // N_CHIPS, d)), halves(col_split(g_gate, fq)),
              halves(col_split(g_up, fq)), halves(g_down.reshape(N_CHIPS, fq, d))]
    rsib = _pair_exchange(gs_ffn, name="grad_pair_exchange_ffn")
    ps_ffn = [_pair_add(g, r, c_arr, name=f"grad_pair_add_{nm}")
              for g, r, nm in zip(gs_ffn, rsib, ["w_out", "w_gate", "w_up", "w_down"])]
    do_a, do_b, d_az, d_rg, d_ga, d_gb, d_wa, d_wb = _merge_bwd(
        dmix, o_a, o_b, proj, wa_row, ret_norm_w, o_az, o_rg, o_ga, o_gb, name="merge_bwd")
    d_rq, d_rk, d_rv = _ret_bwd(proj, cos2, sin2, lg_tab, s_b, do_b, o_rq, o_rk, o_rv, name="ret_bwd")
    dq_a, dk_a, dv_a, dbeta_r, dg_r = _gdn_bwd(q_a, k_a, v_a, beta_r, gc_r, tinv, u_a, w_a, s_a, do_a, name="gdn_bwd")
    dc = _conv_bwd_pre(proj, conv_full, dq_a, dk_a, dv_a, name="conv_bwd_pre")
    d_aqkv, d_cw = _conv_bwd(proj, dc, conv_full, name="conv_bwd")
    dbg = jnp.pad(jnp.concatenate([_rows_to_tokens(dbeta_r), _rows_to_tokens(dg_r)], axis=1),
                  ((0, 0), (0, LANES - 2 * HEADS)))
    dprojs, d_alog, d_dtb = _bg_bwd(projs, dbg, alog_row, dtb_row, name="bg_bwd")
    dproj = jnp.concatenate([d_aqkv, d_az, d_rq, d_rk, d_rv, d_rg, d_ga, d_gb], axis=1)
    g_main, *qs_ffn = _mm(u1, dproj, ta=True, out_dtype=BF16, tm=1024, tn=2048, tk=1024,
                          side=_scatter_side(ps_ffn), name="mm_dw_in")
    g_small = _mm(u1, dprojs, ta=True, out_dtype=BF16, tm=1024, tn=LANES, tk=1024, name="mm_dw_in_small")
    pieces, off = [], 0
    for s in range(N_CHIPS):
        if s == small_shard:
            a = g_main[:, off:off + lo_small]
            b = g_main[:, off + lo_small:off + lo_small + ncs - hi_small]
            pieces.append(jnp.concatenate([a, g_small[:, :n_small], b], axis=1))
            off += ncs - n_small
        else:
            pieces.append(g_main[:, off:off + ncs])
            off += ncs
    gs_in = [halves(jnp.stack(pieces))]
    rsib = _pair_exchange(gs_in, name="grad_pair_exchange_in")
    ps_in = [_pair_add(gs_in[0], rsib[0], c_arr, name="grad_pair_add_w_in")]
    du, *qs_in = _mm(dproj, w_main, tb=True, tm=1024, tn=d, tk=1024, side=_scatter_side(ps_in), name="mm_du")
    du = _mm(dprojs, w_small, tb=True, res=du, tm=1024, tn=d, tk=LANES, name="mm_du_small")
    dx, _, d_n1 = _rms_bwd(du, xs, norm1_w, dh1, name="rms1_bwd")

    names = ["w_in", "w_out", "w_gate", "w_up", "w_down"]
    qs = [lax.dynamic_update_slice(q, lax.dynamic_slice(p, (s_idx, 0, 0), (1,) + p.shape[1:]), (s_idx, 0, 0))
          for q, p in zip(qs_in + qs_ffn, ps_in + ps_ffn)]
    hs = [_chip_sum(q, name=f"grad_chip_sum_{nm}") for q, nm in zip(qs, names)]
    theirs = _pair_share(hs, name="grad_pair_share")
    big_w = [w_in[0], w_out[0], w_gate[0], w_up[0], w_down[0]]
    big_m = [m_w_in[0], m_w_out[0], m_w_gate[0], m_w_up[0], m_w_down[0]]
    big_v = [v_w_in[0], v_w_out[0], v_w_gate[0], v_w_up[0], v_w_down[0]]
    big = {}
    for nm, mine, other, w_, m_, v_ in zip(names, hs, theirs, big_w, big_m, big_v):
        big[nm] = tuple(a[None] for a in _adamw_halves(w_, mine, other, c_arr, m_, v_, name=f"adamw_{nm}"))

    d_wa_h = jnp.sum(d_wa.reshape(HEADS, DV), axis=0, keepdims=True)
    small = [d_n1, d_alog[:, HEADS:2 * HEADS], d_dtb[:, HEADS:2 * HEADS], d_wa_h, d_wb, d_n2, d_nf,
             d_cw[:CONV_W].reshape(1, -1)]
    sizes = [a.shape[1] for a in small]
    packed = jnp.concatenate(small, axis=1)
    n_pack = packed.shape[1]
    n_rows = -(-n_pack // LANES)
    n_rows = -(-n_rows // 8) * 8
    packed = jnp.pad(packed, ((0, 0), (0, n_rows * LANES - n_pack))).reshape(n_rows, LANES)
    red = _allreduce_small(packed, name="allreduce_small").reshape(1, -1)
    offs = np.cumsum([0] + sizes)
    g_n1, g_alog, g_dtb, g_wa, g_wb, g_n2, g_nf, g_cw = [red[:, offs[i]:offs[i + 1]] for i in range(len(sizes))]
    ncw = conv_w.shape[2]
    g_cw = lax.dynamic_slice(g_cw.reshape(CONV_W, -1), (0, s_idx * ncw), (CONV_W, ncw))

    def small_update(w_, g, m_, v_, nm):
        shape = w_.shape
        pad = (-w_.size) % LANES
        to2 = lambda a: jnp.pad(a.reshape(1, -1), ((0, 0), (0, pad)))
        outs = _adamw(to2(w_), to2(g), to2(m_), to2(v_), name=f"adamw_{nm}")
        return (g.reshape(shape),) + tuple(a[:, :w_.size].reshape(shape) for a in outs)

    res = {
        "norm1_w": small_update(norm1_w, g_n1, m_norm1_w, v_norm1_w, "norm1_w"),
        "w_in": big["w_in"],
        "conv_w": small_update(conv_w, g_cw, m_conv_w, v_conv_w, "conv_w"),
        "a_log": small_update(a_log, g_alog, m_a_log, v_a_log, "a_log"),
        "dt_bias": small_update(dt_bias, g_dtb, m_dt_bias, v_dt_bias, "dt_bias"),
        "gdn_norm_w": small_update(gdn_norm_w, g_wa, m_gdn_norm_w, v_gdn_norm_w, "gdn_norm_w"),
        "ret_norm_w": small_update(ret_norm_w, g_wb, m_ret_norm_w, v_ret_norm_w, "ret_norm_w"),
        "w_out": big["w_out"],
        "norm2_w": small_update(norm2_w, g_n2, m_norm2_w, v_norm2_w, "norm2_w"),
        "w_gate": big["w_gate"],
        "w_up": big["w_up"],
        "w_down": big["w_down"],
        "norm_f_w": small_update(norm_f_w, g_nf, m_norm_f_w, v_norm_f_w, "norm_f_w"),
    }
    order = ["norm1_w", "w_in", "conv_w", "a_log", "dt_bias", "gdn_norm_w", "ret_norm_w", "w_out", "norm2_w",
             "w_gate", "w_up", "w_down", "norm_f_w"]
    loss = lax.psum(loss_row[0, 0], ("x", "y", "c"))
    return (loss, dx[None], *[res[n][0] for n in order], *[res[n][1] for n in order],
            *[res[n][2] for n in order], *[res[n][3] for n in order])
```

```python
import functools

import jax
import jax.numpy as jnp
import numpy as np
from jax import lax
from jax.experimental import pallas as pl
from jax.experimental.pallas import tpu as pltpu

F32 = jnp.float32
BF16 = jnp.bfloat16
MESH = pl.DeviceIdType.MESH

HEADS = 8
DK = 128
DV = 256
CHUNK = 64
CONV_W = 4
EPS = 1e-6
ROPE_BASE = 10000.0
ADAM_LR, ADAM_B1, ADAM_B2, ADAM_EPS, ADAM_WD, ADAM_STEP = 0.001, 0.9, 0.999, 1e-08, 0.01, 10
N_CHIPS = 4
N_DEV = 8
LANES = 128
HALO = 8
VMEM_LIMIT = 56 * 1024 * 1024
HB = 8


def _pick(n, cands):
    for c in cands:
        if n % c == 0:
            return c
    raise ValueError(f"no tile for {n} in {cands}")


def _params(sem=None):
    return pltpu.CompilerParams(dimension_semantics=sem, vmem_limit_bytes=VMEM_LIMIT)


def _dot(a, b):
    return jnp.dot(a.astype(BF16), b.astype(BF16), preferred_element_type=F32)


def _dot_nt(a, b):
    return lax.dot_general(a.astype(BF16), b.astype(BF16), (((1,), (1,)), ((), ())), preferred_element_type=F32)


def _dot_tn(a, b):
    return lax.dot_general(a.astype(BF16), b.astype(BF16), (((0,), (0,)), ((), ())), preferred_element_type=F32)


def _sigmoid(x):
    return 1.0 / (1.0 + jnp.exp(-x))


def _iota2(n):
    return lax.broadcasted_iota(jnp.int32, (n, n), 0), lax.broadcasted_iota(jnp.int32, (n, n), 1)


def _row_to_col(row):
    n = row.shape[1]
    r, c = _iota2(n)
    return jnp.sum(jnp.where(r == c, jnp.broadcast_to(row, (n, n)), 0.0), axis=1, keepdims=True)


def _col_to_row(col):
    n = col.shape[0]
    r, c = _iota2(n)
    return jnp.sum(jnp.where(r == c, jnp.broadcast_to(col, (n, n)), 0.0), axis=0, keepdims=True)


class _Side:
    def __init__(self, ins, out_shapes, n_sem, phases):
        self.ins, self.out_shapes, self.n_sem, self.phases = list(ins), list(out_shapes), n_sem, phases

    def sems(self):
        return [pltpu.SemaphoreType.DMA((self.n_sem,)), pltpu.SemaphoreType.DMA((self.n_sem,))]


def _mm(a, b, *, name, tm, tn, tk, ta=False, tb=False, out_dtype=F32, res=None, side=None, pair=None):
    m, k = (a.shape[1], a.shape[0]) if ta else a.shape
    n = b.shape[0] if tb else b.shape[1]
    tm, tn, tk = min(tm, m), min(tn, n), min(tk, k)
    assert m % tm == 0 and n % tn == 0 and k % tk == 0, (name, m, n, k)
    nk = k // tk
    nj, ni = n // tn, m // tm
    dn = (((0 if ta else 1,), (1 if tb else 0,)), ((), ()))
    n_ab = 4 if pair else 2
    n_in = n_ab + (res is not None)
    n_side_in = len(side.ins) if side else 0
    n_side_out = len(side.out_shapes) if side else 0

    def body(*refs):
        a_ref, b_ref = refs[0], refs[1]
        r_ref = refs[n_ab] if res is not None else None
        o_ref = refs[n_in + n_side_in]
        if side:
            s_in = refs[n_in:n_in + n_side_in]
            s_out = refs[n_in + n_side_in + 1:n_in + n_side_in + 1 + n_side_out]
            ssem, rsem = refs[-2], refs[-1]
            j_, i_, k_ = pl.program_id(0), pl.program_id(1), pl.program_id(2)
            when = [(j_ == 0) & (i_ == 0) & (k_ == 0)]
            if len(side.phases) == 3:
                when.append((j_ == (3 * nj) // 4) & (i_ == 0) & (k_ == 0))
            when.append((j_ == nj - 1) & (i_ == ni - 1) & (k_ == nk - 1))
            assert len(when) == len(side.phases) and (len(when) == 2 or nj >= 2)

        def run_phase(p):
            @pl.when(when[p])
            def _():
                side.phases[p](s_in, s_out, ssem, rsem)

        if side:
            for p in range(len(side.phases) - 1):
                run_phase(p)

        def finish(r):
            if res is not None:
                r = r + r_ref[...]
            o_ref[...] = r.astype(out_dtype)

        part = lax.dot_general(a_ref[...], b_ref[...], dn, preferred_element_type=F32)
        if pair:
            part = part + lax.dot_general(refs[2][...], refs[3][...], dn, preferred_element_type=F32)
        if nk == 1:
            finish(part)
        else:
            acc = refs[n_in + n_side_in + 1 + n_side_out]
            kk = pl.program_id(2)

            @pl.when(kk == 0)
            def _():
                acc[...] = part

            @pl.when((kk > 0) & (kk < nk - 1))
            def _():
                acc[...] += part

            @pl.when(kk == nk - 1)
            def _():
                finish(acc[...] + part)

        if side:
            run_phase(len(side.phases) - 1)

    a_spec = pl.BlockSpec((tk, tm), lambda j, i, kk: (kk, i)) if ta else pl.BlockSpec((tm, tk), lambda j, i, kk: (i, kk))
    b_spec = pl.BlockSpec((tn, tk), lambda j, i, kk: (j, kk)) if tb else pl.BlockSpec((tk, tn), lambda j, i, kk: (kk, j))
    o_spec = pl.BlockSpec((tm, tn), lambda j, i, kk: (i, j))
    in_specs, args = [a_spec, b_spec], [a, b]
    if pair:
        assert pair[0].shape == a.shape and pair[1].shape == b.shape
        in_specs += [a_spec, b_spec]
        args += list(pair)
    if res is not None:
        in_specs.append(o_spec)
        args.append(res)
    out_specs, out_shape = o_spec, jax.ShapeDtypeStruct((m, n), out_dtype)
    scratch = [pltpu.VMEM((tm, tn), F32)] if nk > 1 else []
    sem = ("parallel", "parallel", "arbitrary")
    if side:
        hbm = pl.BlockSpec(memory_space=pl.ANY)
        in_specs += [hbm] * n_side_in
        args += side.ins
        out_specs, out_shape = [o_spec] + [hbm] * n_side_out, [out_shape] + side.out_shapes
        scratch += side.sems()
        sem = ("arbitrary",) * 3
    return pl.pallas_call(
        body, name=name, grid=(nj, ni, nk), in_specs=in_specs, out_specs=out_specs, out_shape=out_shape,
        scratch_shapes=scratch, compiler_params=_params(sem),
    )(*args)


def _rms_fwd(x, w, *, name, w_narrow=None):
    t, d = x.shape
    tt = _pick(t, (512, 256))

    def body(*refs):
        x_ref, w_ref, o_ref = refs[0], refs[1], refs[-2 if w_narrow is not None else -1]
        xv = x_ref[...]
        r = lax.rsqrt(jnp.mean(xv * xv, axis=-1, keepdims=True) + EPS)
        u = (xv * r * w_ref[...]).astype(BF16)
        o_ref[...] = u
        if w_narrow is not None:
            refs[-1][...] = jnp.dot(u, refs[2][...], preferred_element_type=F32)

    blk = pl.BlockSpec((tt, d), lambda i: (i, 0))
    in_specs, args = [blk, pl.BlockSpec((1, d), lambda i: (0, 0))], [x, w]
    out_specs, out_shape = [blk], [jax.ShapeDtypeStruct((t, d), BF16)]
    if w_narrow is not None:
        in_specs.append(pl.BlockSpec(w_narrow.shape, lambda i: (0, 0)))
        args.append(w_narrow)
        out_specs.append(pl.BlockSpec((tt, LANES), lambda i: (i, 0)))
        out_shape.append(jax.ShapeDtypeStruct((t, LANES), F32))
    out = pl.pallas_call(
        body, name=name, grid=(t // tt,), in_specs=in_specs, out_specs=out_specs, out_shape=out_shape,
        compiler_params=_params(("parallel",)),
    )(*args)
    return out if w_narrow is not None else out[0]


def _rms_bwd(dn, x, w, dres, *, name, narrow=None):
    t, d = x.shape
    tt = _pick(t, (256,))

    def body(*refs):
        dn_ref, x_ref, w_ref, dres_ref = refs[:4]
        dx_ref, dxb_ref, dw_ref = refs[-3:]
        xv, g = x_ref[...], dn_ref[...]
        if narrow is not None:
            g = g + lax.dot_general(refs[4][...], refs[5][...], (((1,), (1,)), ((), ())), preferred_element_type=F32)
        r = lax.rsqrt(jnp.mean(xv * xv, axis=-1, keepdims=True) + EPS)
        xh = xv * r
        gw = g * w_ref[...]
        dx = dres_ref[...] + r * (gw - xh * jnp.mean(gw * xh, axis=-1, keepdims=True))
        dx_ref[...] = dx
        dxb_ref[...] = dx.astype(BF16)

        @pl.when(pl.program_id(0) == 0)
        def _():
            dw_ref[...] = jnp.zeros_like(dw_ref)

        dw_ref[...] += jnp.sum(g * xh, axis=0, keepdims=True)

    blk = pl.BlockSpec((tt, d), lambda i: (i, 0))
    row = pl.BlockSpec((1, d), lambda i: (0, 0))
    in_specs, args = [blk, blk, row, blk], [dn, x, w, dres]
    if narrow is not None:
        in_specs += [pl.BlockSpec((tt, LANES), lambda i: (i, 0)), pl.BlockSpec(narrow[1].shape, lambda i: (0, 0))]
        args += list(narrow)
    return pl.pallas_call(
        body, name=name, grid=(t // tt,), in_specs=in_specs, out_specs=[blk, blk, row],
        out_shape=[jax.ShapeDtypeStruct((t, d), F32), jax.ShapeDtypeStruct((t, d), BF16),
                   jax.ShapeDtypeStruct((1, d), F32)],
        compiler_params=_params(("arbitrary",)),
    )(*args)


def _loss_head(h2, tgt, wf, *, name):
    t, d = h2.shape
    tt = _pick(t, (256,))

    def body(x_ref, t_ref, w_ref, loss_ref, dx_ref, dxb_ref, dw_ref):
        xv = x_ref[...]
        r = lax.rsqrt(jnp.mean(xv * xv, axis=-1, keepdims=True) + EPS)
        xh = xv * r
        err = xh * w_ref[...] - t_ref[...]
        lpart = 0.5 * jnp.sum(jnp.mean(err * err, axis=-1, keepdims=True), axis=0, keepdims=True)
        dy = err * (1.0 / d)
        gw = dy * w_ref[...]
        dx = r * (gw - xh * jnp.mean(gw * xh, axis=-1, keepdims=True))
        dx_ref[...] = dx
        dxb_ref[...] = dx.astype(BF16)

        @pl.when(pl.program_id(0) == 0)
        def _():
            dw_ref[...] = jnp.zeros_like(dw_ref)
            loss_ref[...] = jnp.zeros_like(loss_ref)

        dw_ref[...] += jnp.sum(dy * xh, axis=0, keepdims=True)
        loss_ref[...] += jnp.broadcast_to(lpart, loss_ref.shape)

    blk = pl.BlockSpec((tt, d), lambda i: (i, 0))
    row = pl.BlockSpec((1, d), lambda i: (0, 0))
    lrow = pl.BlockSpec((1, LANES), lambda i: (0, 0))
    return pl.pallas_call(
        body, name=name, grid=(t // tt,),
        in_specs=[blk, blk, row], out_specs=[lrow, blk, blk, row],
        out_shape=[jax.ShapeDtypeStruct((1, LANES), F32), jax.ShapeDtypeStruct((t, d), F32),
                   jax.ShapeDtypeStruct((t, d), BF16), jax.ShapeDtypeStruct((1, d), F32)],
        compiler_params=_params(("arbitrary",)),
    )(h2, tgt, wf)


def _ffn_in(hn, w_g, w_u, *, name, tm, tn):
    t, d = hn.shape
    f = w_g.shape[1]
    tm = min(tm, t)

    def body(a_ref, g_ref, u_ref, gt_ref, up_ref, act_ref):
        a = a_ref[...]
        g = jnp.dot(a, g_ref[...], preferred_element_type=F32)
        u = jnp.dot(a, u_ref[...], preferred_element_type=F32)
        gt_ref[...] = g.astype(BF16)
        up_ref[...] = u.astype(BF16)
        act_ref[...] = (g * _sigmoid(g) * u).astype(BF16)

    wblk = pl.BlockSpec((d, tn), lambda j, i: (0, j))
    oblk = pl.BlockSpec((tm, tn), lambda j, i: (i, j))
    return pl.pallas_call(
        body, name=name, grid=(f // tn, t // tm),
        in_specs=[pl.BlockSpec((tm, d), lambda j, i: (i, 0)), wblk, wblk], out_specs=[oblk] * 3,
        out_shape=[jax.ShapeDtypeStruct((t, f), BF16)] * 3,
        compiler_params=_params(("parallel", "parallel")),
    )(hn, w_g, w_u)


def _ffn_back(dh2b, w_d, gt, up, *, name, tm, tn):
    t, d = dh2b.shape
    f = w_d.shape[0]
    tm = min(tm, t)

    def body(a_ref, w_ref, g_ref, u_ref, dg_ref, du_ref):
        da = lax.dot_general(a_ref[...], w_ref[...], (((1,), (1,)), ((), ())), preferred_element_type=F32)
        g = g_ref[...].astype(F32)
        sg = _sigmoid(g)
        dg_ref[...] = (da * u_ref[...].astype(F32) * sg * (1.0 + g * (1.0 - sg))).astype(BF16)
        du_ref[...] = (da * g * sg).astype(BF16)

    oblk = pl.BlockSpec((tm, tn), lambda j, i: (i, j))
    return pl.pallas_call(
        body, name=name, grid=(f // tn, t // tm),
        in_specs=[pl.BlockSpec((tm, d), lambda j, i: (i, 0)), pl.BlockSpec((tn, d), lambda j, i: (j, 0)), oblk, oblk],
        out_specs=[oblk] * 2,
        out_shape=[jax.ShapeDtypeStruct((t, f), BF16)] * 2,
        compiler_params=_params(("parallel", "parallel")),
    )(dh2b, w_d, gt, up)


def _conv_cols(xs, cw_ref, cb, tt):
    cols = slice(cb * LANES, (cb + 1) * LANES)
    base = HALO - (CONV_W - 1)
    acc = xs[base:base + tt, cols] * cw_ref[0:1, cols]
    for j in range(1, CONV_W):
        acc = acc + xs[base + j:base + j + tt, cols] * cw_ref[j:j + 1, cols]
    return acc


def _fill_halo(xs, x_ref, xp_ref, tt):
    first = pl.program_id(0) == 0
    xs[0:HALO, :] = jnp.where(first, 0.0, xp_ref[...])
    xs[HALO:HALO + tt, :] = x_ref[...]


def _conv_specs(tt, ch):
    cur = pl.BlockSpec((tt, ch), lambda i: (i, 0))
    prev = pl.BlockSpec((HALO, ch), lambda i: (jnp.maximum(i * (tt // HALO) - 1, 0), 0))
    return cur, prev


def _conv_fwd(proj, conv_w, *, name):
    t = proj.shape[0]
    ch = conv_w.shape[1]
    nqk = HEADS * DK
    tt = _pick(t, (256,))
    cur, prev = _conv_specs(tt, ch)

    def body(x_ref, xp_ref, cw_ref, q_ref, k_ref, v_ref, xs):
        _fill_halo(xs, x_ref, xp_ref, tt)
        for cb in range(ch // LANES):
            c = _conv_cols(xs, cw_ref, cb, tt)
            s = c * _sigmoid(c)
            if cb < 2 * HEADS:
                s = s * lax.rsqrt(jnp.sum(s * s, axis=-1, keepdims=True) + EPS)
                if cb < HEADS:
                    q_ref[:, cb * LANES:(cb + 1) * LANES] = s * (DK ** -0.5)
                else:
                    k_ref[:, (cb - HEADS) * LANES:(cb - HEADS + 1) * LANES] = s
            else:
                v_ref[:, (cb - 2 * HEADS) * LANES:(cb - 2 * HEADS + 1) * LANES] = s

    return pl.pallas_call(
        body, name=name, grid=(t // tt,),
        in_specs=[cur, prev, pl.BlockSpec((CONV_W, ch), lambda i: (0, 0))],
        out_specs=[pl.BlockSpec((tt, nqk), lambda i: (i, 0)), pl.BlockSpec((tt, nqk), lambda i: (i, 0)),
                   pl.BlockSpec((tt, ch - 2 * nqk), lambda i: (i, 0))],
        out_shape=[jax.ShapeDtypeStruct((t, nqk), F32), jax.ShapeDtypeStruct((t, nqk), F32),
                   jax.ShapeDtypeStruct((t, ch - 2 * nqk), F32)],
        scratch_shapes=[pltpu.VMEM((HALO + tt, ch), F32)],
        compiler_params=_params(("arbitrary",)),
    )(proj, proj, conv_w)


def _conv_bwd_pre(proj, conv_w, dq, dk, dv, *, name):
    t = proj.shape[0]
    ch = conv_w.shape[1]
    nqk = HEADS * DK
    tt = _pick(t, (256,))
    cur, prev = _conv_specs(tt, ch)

    def body(x_ref, xp_ref, cw_ref, dq_ref, dk_ref, dv_ref, dc_ref, xs):
        _fill_halo(xs, x_ref, xp_ref, tt)
        for cb in range(ch // LANES):
            c = _conv_cols(xs, cw_ref, cb, tt)
            sg = _sigmoid(c)
            s = c * sg
            if cb < 2 * HEADS:
                if cb < HEADS:
                    d, scale = dq_ref[:, cb * LANES:(cb + 1) * LANES], DK ** -0.5
                else:
                    d, scale = dk_ref[:, (cb - HEADS) * LANES:(cb - HEADS + 1) * LANES], 1.0
                rinv = lax.rsqrt(jnp.sum(s * s, axis=-1, keepdims=True) + EPS)
                ds = scale * rinv * (d - s * (rinv * rinv) * jnp.sum(d * s, axis=-1, keepdims=True))
            else:
                ds = dv_ref[:, (cb - 2 * HEADS) * LANES:(cb - 2 * HEADS + 1) * LANES]
            dc_ref[:, cb * LANES:(cb + 1) * LANES] = ds * sg * (1.0 + c * (1.0 - sg))

    return pl.pallas_call(
        body, name=name, grid=(t // tt,),
        in_specs=[cur, prev, pl.BlockSpec((CONV_W, ch), lambda i: (0, 0)),
                  pl.BlockSpec((tt, nqk), lambda i: (i, 0)), pl.BlockSpec((tt, nqk), lambda i: (i, 0)),
                  pl.BlockSpec((tt, ch - 2 * nqk), lambda i: (i, 0))],
        out_specs=pl.BlockSpec((tt, ch), lambda i: (i, 0)),
        out_shape=jax.ShapeDtypeStruct((t, ch), F32),
        scratch_shapes=[pltpu.VMEM((HALO + tt, ch), F32)],
        compiler_params=_params(("arbitrary",)),
    )(proj, proj, conv_w, dq, dk, dv)


def _conv_bwd(proj, dc, conv_w, dproj, *, name):
    t = proj.shape[0]
    ch = conv_w.shape[1]
    tt = _pick(t, (256,))
    nt = t // tt
    cur = pl.BlockSpec((tt, ch), lambda i: (i, 0))
    nxt = pl.BlockSpec((HALO, ch), lambda i: (jnp.minimum((i + 1) * (tt // HALO), t // HALO - 1), 0))

    def body(x_ref, d_ref, dn_ref, cw_ref, alias_ref, dx_ref, dw_ref, ds):
        last = pl.program_id(0) == nt - 1
        ds[0:tt, :] = d_ref[...]
        ds[tt:tt + HALO, :] = jnp.where(last, 0.0, dn_ref[...])

        @pl.when(pl.program_id(0) == 0)
        def _():
            dw_ref[...] = jnp.zeros_like(dw_ref)

        for cb in range(ch // LANES):
            cols = slice(cb * LANES, (cb + 1) * LANES)
            x0 = x_ref[:, cols]
            acc = None
            for j in range(CONV_W):
                sh = CONV_W - 1 - j
                dj = ds[sh:sh + tt, cols]
                term = dj * cw_ref[j:j + 1, cols]
                acc = term if acc is None else acc + term
                dw_ref[j:j + 1, cols] += jnp.sum(dj * x0, axis=0, keepdims=True)
            dx_ref[:, cols] = acc.astype(BF16)

    return pl.pallas_call(
        body, name=name, grid=(nt,),
        in_specs=[cur, cur, nxt, pl.BlockSpec((CONV_W, ch), lambda i: (0, 0)), pl.BlockSpec(memory_space=pl.ANY)],
        out_specs=[pl.BlockSpec((tt, ch), lambda i: (i, 0)), pl.BlockSpec((HALO, ch), lambda i: (0, 0))],
        out_shape=[jax.ShapeDtypeStruct(dproj.shape, BF16), jax.ShapeDtypeStruct((HALO, ch), F32)],
        scratch_shapes=[pltpu.VMEM((tt + HALO, ch), F32)],
        input_output_aliases={4: 0},
        compiler_params=_params(("arbitrary",)),
    )(proj, dc, dc, conv_w, dproj)


def _bg_fwd(projs, alog_row, dtb_row, *, name):
    t = projs.shape[0]
    tt = _pick(t, (512, 256))

    def body(p_ref, al_ref, db_ref, o_ref):
        p = p_ref[...]
        lane = lax.broadcasted_iota(jnp.int32, p.shape, 1)
        z = p + db_ref[...]
        sp = jnp.maximum(z, 0.0) + jnp.log1p(jnp.exp(-jnp.abs(z)))
        g = -jnp.exp(al_ref[...]) * sp
        o_ref[...] = jnp.where(lane < HEADS, _sigmoid(p), jnp.where(lane < 2 * HEADS, g, 0.0))

    blk = pl.BlockSpec((tt, LANES), lambda i: (i, 0))
    row = pl.BlockSpec((1, LANES), lambda i: (0, 0))
    return pl.pallas_call(
        body, name=name, grid=(t // tt,), in_specs=[blk, row, row], out_specs=blk,
        out_shape=jax.ShapeDtypeStruct((t, LANES), F32), compiler_params=_params(("parallel",)),
    )(projs, alog_row, dtb_row)


def _bg_bwd(projs, dbg, alog_row, dtb_row, *, name):
    t = projs.shape[0]
    tt = _pick(t, (512, 256))

    def body(p_ref, d_ref, al_ref, db_ref, o_ref, dal_ref, ddb_ref):
        p, d = p_ref[...], d_ref[...]
        lane = lax.broadcasted_iota(jnp.int32, p.shape, 1)
        isg = (lane >= HEADS) & (lane < 2 * HEADS)
        be = _sigmoid(p)
        z = p + db_ref[...]
        sp = jnp.maximum(z, 0.0) + jnp.log1p(jnp.exp(-jnp.abs(z)))
        ea = jnp.exp(al_ref[...])
        d_aa = jnp.where(isg, d * (-ea) * _sigmoid(z), 0.0)
        o_ref[...] = jnp.where(lane < HEADS, d * be * (1.0 - be), d_aa).astype(BF16)

        @pl.when(pl.program_id(0) == 0)
        def _():
            dal_ref[...] = jnp.zeros_like(dal_ref)
            ddb_ref[...] = jnp.zeros_like(ddb_ref)

        dal_ref[...] += jnp.sum(jnp.where(isg, d * (-ea) * sp, 0.0), axis=0, keepdims=True)
        ddb_ref[...] += jnp.sum(d_aa, axis=0, keepdims=True)

    blk = pl.BlockSpec((tt, LANES), lambda i: (i, 0))
    row = pl.BlockSpec((1, LANES), lambda i: (0, 0))
    return pl.pallas_call(
        body, name=name, grid=(t // tt,), in_specs=[blk, blk, row, row], out_specs=[blk, row, row],
        out_shape=[jax.ShapeDtypeStruct((t, LANES), BF16), jax.ShapeDtypeStruct((1, LANES), F32),
                   jax.ShapeDtypeStruct((1, LANES), F32)],
        compiler_params=_params(("arbitrary",)),
    )(projs, dbg, alog_row, dtb_row)


def _hn_specs(n_chunks, rev=False):
    def nn(n):
        return n_chunks - 1 - n if rev else n
    tok = lambda w: pl.BlockSpec((CHUNK, HB * w), lambda g, n: (nn(n), g))
    per = lambda a, b: pl.BlockSpec((HB, None, a, b), lambda g, n: (g, nn(n), 0, 0))
    return tok, per


def _hcols(hh, w):
    return slice(hh * w, (hh + 1) * w)


def _decay(gc_col, gc_row):
    r, c = _iota2(CHUNK)
    d = jnp.exp(jnp.minimum(gc_col - gc_row, 0.0))
    return jnp.where(r >= c, d, 0.0), jnp.where(r > c, d, 0.0)


def _gdn_prep(q, k, v, beta_r, g_r, *, name):
    t = q.shape[0]
    n_chunks = t // CHUNK
    tok, per = _hn_specs(n_chunks)

    def body(q_ref, k_ref, v_ref, b_ref, g_ref, gc_ref, ti_ref, u_ref, w_ref, p_ref):
        r, c = _iota2(CHUNK)
        hs = range(HB)
        kk = [k_ref[:, _hcols(h, DK)] for h in hs]
        gc_col = [jnp.sum(jnp.where(c <= r, jnp.broadcast_to(g_ref[h], (CHUNK, CHUNK)), 0.0), axis=1, keepdims=True)
                  for h in hs]
        gc_row = [_col_to_row(gc_col[h]) for h in hs]
        beta_col = [_row_to_col(b_ref[h]) for h in hs]
        dec = [_decay(gc_col[h], gc_row[h]) for h in hs]
        kb = [kk[h] * beta_col[h] for h in hs]
        for h in hs:
            gc_ref[h] = gc_row[h]
            p_ref[h] = _dot_nt(q_ref[:, _hcols(h, DK)], kk[h]) * dec[h][0]
        pw = [_dot_nt(kb[h], kk[h]) * dec[h][1] for h in hs]
        y = [-pw[h] for h in hs]
        for _ in range(5):
            pw = [_dot(pw[h], pw[h]) for h in hs]
            yp = [_dot(y[h], pw[h]) for h in hs]
            y = [y[h] + pw[h] + yp[h] for h in hs]
        vb = [v_ref[:, _hcols(h, DV)] * beta_col[h] for h in hs]
        kbe = [kb[h] * jnp.exp(gc_col[h]) for h in hs]
        yv = [_dot(y[h], vb[h]) for h in hs]
        yk = [_dot(y[h], kbe[h]) for h in hs]
        for h in hs:
            ti_ref[h] = y[h]
            u_ref[:, _hcols(h, DV)] = vb[h] + yv[h]
            w_ref[:, _hcols(h, DK)] = kbe[h] + yk[h]

    return pl.pallas_call(
        body, name=name, grid=(HEADS // HB, n_chunks),
        in_specs=[tok(DK), tok(DK), tok(DV), per(1, CHUNK), per(1, CHUNK)],
        out_specs=[per(1, CHUNK), per(CHUNK, CHUNK), tok(DV), tok(DK), per(CHUNK, CHUNK)],
        out_shape=[jax.ShapeDtypeStruct((HEADS, n_chunks, 1, CHUNK), F32),
                   jax.ShapeDtypeStruct((HEADS, n_chunks, CHUNK, CHUNK), F32),
                   jax.ShapeDtypeStruct((t, HEADS * DV), F32), jax.ShapeDtypeStruct((t, HEADS * DK), F32),
                   jax.ShapeDtypeStruct((HEADS, n_chunks, CHUNK, CHUNK), F32)],
        compiler_params=_params(("parallel", "parallel")),
    )(q, k, v, beta_r, g_r)


def _gdn_scan(q, k, u, w, p, gc_r, *, name):
    t = q.shape[0]
    n_chunks = t // CHUNK
    tok, per = _hn_specs(n_chunks)

    def body(q_ref, k_ref, u_ref, w_ref, p_ref, gc_ref, o_ref, s_ref, st):
        @pl.when(pl.program_id(1) == 0)
        def _():
            st[...] = jnp.zeros_like(st)

        hs = range(HB)
        s = [st[h] for h in hs]
        gc_row = [gc_ref[h] for h in hs]
        gc_col = [_row_to_col(gc_row[h]) for h in hs]
        glast = [gc_row[h][:, CHUNK - 1:CHUNK] for h in hs]
        for h in hs:
            s_ref[h] = s[h]
        ws = [_dot(w_ref[:, _hcols(h, DK)], s[h]) for h in hs]
        qs = [_dot(q_ref[:, _hcols(h, DK)] * jnp.exp(gc_col[h]), s[h]) for h in hs]
        vn = [u_ref[:, _hcols(h, DV)] - ws[h] for h in hs]
        pv = [_dot(p_ref[h], vn[h]) for h in hs]
        kv = [_dot_tn(k_ref[:, _hcols(h, DK)] * jnp.exp(glast[h] - gc_col[h]), vn[h]) for h in hs]
        for h in hs:
            o_ref[:, _hcols(h, DV)] = qs[h] + pv[h]
            st[h] = s[h] * jnp.exp(glast[h]) + kv[h]

    return pl.pallas_call(
        body, name=name, grid=(HEADS // HB, n_chunks),
        in_specs=[tok(DK), tok(DK), tok(DV), tok(DK), per(CHUNK, CHUNK), per(1, CHUNK)],
        out_specs=[tok(DV), per(DK, DV)],
        out_shape=[jax.ShapeDtypeStruct((t, HEADS * DV), F32),
                   jax.ShapeDtypeStruct((HEADS, n_chunks, DK, DV), F32)],
        scratch_shapes=[pltpu.VMEM((HB, DK, DV), F32)],
        compiler_params=_params(("arbitrary", "arbitrary")),
    )(q, k, u, w, p, gc_r)


def _gdn_bwd(q, k, v, beta_r, gc_r, ti, u, w, s_all, do, *, name):
    t = q.shape[0]
    n_chunks = t // CHUNK
    tok, per = _hn_specs(n_chunks, rev=True)

    def body(q_ref, k_ref, v_ref, b_ref, gc_ref, ti_ref, u_ref, w_ref, s_ref, do_ref,
             dq_ref, dk_ref, dv_ref, db_ref, dg_ref, dst):
        @pl.when(pl.program_id(1) == 0)
        def _():
            dst[...] = jnp.zeros_like(dst)

        r, c = _iota2(CHUNK)
        rows = lax.broadcasted_iota(jnp.int32, (CHUNK, 1), 0)
        rsum = lambda a: jnp.sum(a, axis=1, keepdims=True)
        hs = range(HB)
        qq = [q_ref[:, _hcols(h, DK)] for h in hs]
        kk = [k_ref[:, _hcols(h, DK)] for h in hs]
        ww = [w_ref[:, _hcols(h, DK)] for h in hs]
        uu = [u_ref[:, _hcols(h, DV)] for h in hs]
        d_o = [do_ref[:, _hcols(h, DV)] for h in hs]
        s = [s_ref[h] for h in hs]
        d_s = [dst[h] for h in hs]
        gc_row = [gc_ref[h] for h in hs]
        gc_col = [_row_to_col(gc_row[h]) for h in hs]
        beta_col = [_row_to_col(b_ref[h]) for h in hs]
        eg = [jnp.exp(gc_col[h]) for h in hs]
        glast = [gc_row[h][:, CHUNK - 1:CHUNK] for h in hs]
        eglast = [jnp.exp(glast[h]) for h in hs]
        e2 = [jnp.exp(glast[h] - gc_col[h]) for h in hs]
        dec = [_decay(gc_col[h], gc_row[h]) for h in hs]
        kb = [kk[h] * beta_col[h] for h in hs]
        ke = [kk[h] * e2[h] for h in hs]
        qe = [qq[h] * eg[h] for h in hs]
        kkt = [_dot_nt(kb[h], kk[h]) for h in hs]
        qk = [_dot_nt(qq[h], kk[h]) for h in hs]
        ws = [_dot(ww[h], s[h]) for h in hs]
        a = [kkt[h] * dec[h][1] for h in hs]
        pp = [qk[h] * dec[h][0] for h in hs]
        vn = [uu[h] - ws[h] for h in hs]
        t1 = [_dot_tn(pp[h], d_o[h]) for h in hs]
        t2 = [_dot(ke[h], d_s[h]) for h in hs]
        dqe = [_dot_nt(d_o[h], s[h]) for h in hs]
        dke = [_dot_nt(vn[h], d_s[h]) for h in hs]
        dqk = [_dot_nt(d_o[h], vn[h]) * dec[h][0] for h in hs]
        dvn = [t1[h] + t2[h] for h in hs]
        t3 = [_dot_tn(qe[h], d_o[h]) for h in hs]
        t4 = [_dot_tn(ww[h], dvn[h]) for h in hs]
        dw = [-_dot_nt(dvn[h], s[h]) for h in hs]
        de2 = [rsum(dke[h] * ke[h]) for h in hs]
        dglast = [jnp.sum(de2[h], axis=0, keepdims=True)
                  + eglast[h] * jnp.sum(rsum(d_s[h] * s[h]), axis=0, keepdims=True) for h in hs]
        for h in hs:
            dst[h] = d_s[h] * eglast[h] + t3[h] - t4[h]
        yy = [ti_ref[h] for h in hs]
        t5 = [_dot_tn(yy[h], dvn[h]) for h in hs]
        t6 = [_dot_tn(yy[h], dw[h]) for h in hs]
        dvb = [dvn[h] + t5[h] for h in hs]
        dkbe = [dw[h] + t6[h] for h in hs]
        t7 = [_dot_nt(dvb[h], uu[h]) for h in hs]
        t8 = [_dot_nt(dkbe[h], ww[h]) for h in hs]
        d_a = [jnp.where(r > c, -(t7[h] + t8[h]), 0.0) for h in hs]
        dm = [d_a[h] * dec[h][1] for h in hs]
        t9 = [_dot(dm[h], kk[h]) for h in hs]
        t10 = [_dot(dqk[h], kk[h]) for h in hs]
        t11 = [_dot_tn(dqk[h], qq[h]) for h in hs]
        t12 = [_dot_tn(dm[h], kb[h]) for h in hs]
        for h in hs:
            dkb = t9[h] + dkbe[h] * eg[h]
            e_mat = d_a[h] * a[h] + dqk[h] * qk[h]
            dgc = (rsum(dqe[h] * qe[h]) - de2[h] + rsum(dkbe[h] * kb[h] * eg[h]) + rsum(e_mat)
                   - _row_to_col(jnp.sum(e_mat, axis=0, keepdims=True)))
            dgc = dgc + jnp.where(rows == CHUNK - 1, dglast[h], 0.0)
            dq_ref[:, _hcols(h, DK)] = dqe[h] * eg[h] + t10[h]
            dk_ref[:, _hcols(h, DK)] = t11[h] + dke[h] * e2[h] + t12[h] + dkb * beta_col[h]
            dv_ref[:, _hcols(h, DV)] = dvb[h] * beta_col[h]
            dbeta = rsum(dkb * kk[h]) + rsum(dvb[h] * v_ref[:, _hcols(h, DV)])
            db_ref[h] = _col_to_row(dbeta)
            dg_ref[h] = jnp.sum(jnp.where(r >= c, jnp.broadcast_to(dgc, (CHUNK, CHUNK)), 0.0), axis=0, keepdims=True)

    return pl.pallas_call(
        body, name=name, grid=(HEADS // HB, n_chunks),
        in_specs=[tok(DK), tok(DK), tok(DV), per(1, CHUNK), per(1, CHUNK), per(CHUNK, CHUNK), tok(DV), tok(DK),
                  per(DK, DV), tok(DV)],
        out_specs=[tok(DK), tok(DK), tok(DV), per(1, CHUNK), per(1, CHUNK)],
        out_shape=[jax.ShapeDtypeStruct((t, HEADS * DK), F32), jax.ShapeDtypeStruct((t, HEADS * DK), F32),
                   jax.ShapeDtypeStruct((t, HEADS * DV), F32),
                   jax.ShapeDtypeStruct((HEADS, n_chunks, 1, CHUNK), F32),
                   jax.ShapeDtypeStruct((HEADS, n_chunks, 1, CHUNK), F32)],
        scratch_shapes=[pltpu.VMEM((HB, DK, DV), F32)],
        compiler_params=_params(("arbitrary", "arbitrary")),
    )(q, k, v, beta_r, gc_r, ti, u, w, s_all, do)


def _rot(x, cos2, sin2):
    return x * cos2 + pltpu.roll(x, DK // 2, 1) * sin2


def _unrot(d, cos2, sin2):
    return d * cos2 + pltpu.roll(d * sin2, DK // 2, 1)


def _ret_consts(lg):
    r, c = _iota2(CHUNK)
    dm = jnp.where(r >= c, jnp.exp((r - c).astype(F32) * lg), 0.0)
    pos = lax.broadcasted_iota(jnp.int32, (CHUNK, 1), 0).astype(F32)
    return dm, jnp.exp((pos + 1.0) * lg), jnp.exp((CHUNK - 1.0 - pos) * lg), jnp.exp(CHUNK * lg)


def _ret_fwd(proj, cos2, sin2, lg_tab, col_q, col_k, col_v, *, name):
    t = proj.shape[0]
    n_chunks = t // CHUNK
    tok, per = _hn_specs(n_chunks)
    bq, bk, bv = col_q // (HB * DK), col_k // (HB * DK), col_v // (HB * DV)

    def body(q_ref, k_ref, v_ref, cos_ref, sin_ref, lg_ref, o_ref, s_ref, st):
        @pl.when(pl.program_id(1) == 0)
        def _():
            st[...] = jnp.zeros_like(st)

        cos2v, sin2v = cos_ref[...], sin_ref[...]
        hs = range(HB)
        s = [st[h] for h in hs]
        for h in hs:
            s_ref[h] = s[h]
        cst = [_ret_consts(lg_ref[h][:, 0:1]) for h in hs]
        qq = [_rot(q_ref[:, _hcols(h, DK)], cos2v, sin2v) for h in hs]
        kk = [_rot(k_ref[:, _hcols(h, DK)], cos2v, sin2v) * (DK ** -0.5) for h in hs]
        vv = [v_ref[:, _hcols(h, DV)] for h in hs]
        sc = [_dot_nt(qq[h], kk[h]) * cst[h][0] for h in hs]
        qs = [_dot(qq[h], s[h]) for h in hs]
        kv = [_dot_tn(kk[h] * cst[h][2], vv[h]) for h in hs]
        sv = [_dot(sc[h], vv[h]) for h in hs]
        for h in hs:
            o_ref[:, _hcols(h, DV)] = sv[h] + qs[h] * cst[h][1]
            st[h] = s[h] * cst[h][3] + kv[h]

    return pl.pallas_call(
        body, name=name, grid=(HEADS // HB, n_chunks),
        in_specs=[pl.BlockSpec((CHUNK, HB * DK), lambda g, n: (n, bq + g)),
                  pl.BlockSpec((CHUNK, HB * DK), lambda g, n: (n, bk + g)),
                  pl.BlockSpec((CHUNK, HB * DV), lambda g, n: (n, bv + g)),
                  pl.BlockSpec((CHUNK, DK), lambda g, n: (n, 0)), pl.BlockSpec((CHUNK, DK), lambda g, n: (n, 0)),
                  pl.BlockSpec((HB, 1, LANES), lambda g, n: (g, 0, 0))],
        out_specs=[tok(DV), per(DK, DV)],
        out_shape=[jax.ShapeDtypeStruct((t, HEADS * DV), F32),
                   jax.ShapeDtypeStruct((HEADS, n_chunks, DK, DV), F32)],
        scratch_shapes=[pltpu.VMEM((HB, DK, DV), F32)],
        compiler_params=_params(("arbitrary", "arbitrary")),
    )(proj, proj, proj, cos2, sin2, lg_tab)


def _ret_bwd(proj, cos2, sin2, lg_tab, s_all, do, dproj, col_q, col_k, col_v, *, name):
    t = proj.shape[0]
    n_chunks = t // CHUNK
    tok, per = _hn_specs(n_chunks, rev=True)
    bq, bk, bv = col_q // (HB * DK), col_k // (HB * DK), col_v // (HB * DV)
    rv = lambda n: n_chunks - 1 - n
    nqk, wid = HEADS * DK, HEADS * (2 * DK + DV)
    assert HB == HEADS and col_k == col_q + nqk and col_v == col_k + nqk and col_q % wid == 0

    def body(q_ref, k_ref, v_ref, cos_ref, sin_ref, lg_ref, s_ref, do_ref, alias_ref, d_ref, dst):
        @pl.when(pl.program_id(1) == 0)
        def _():
            dst[...] = jnp.zeros_like(dst)

        cos2v, sin2v = cos_ref[...], sin_ref[...]
        hs = range(HB)
        s = [s_ref[h] for h in hs]
        d_s = [dst[h] for h in hs]
        d_o = [do_ref[:, _hcols(h, DV)] for h in hs]
        cst = [_ret_consts(lg_ref[h][:, 0:1]) for h in hs]
        qq = [_rot(q_ref[:, _hcols(h, DK)], cos2v, sin2v) for h in hs]
        kk = [_rot(k_ref[:, _hcols(h, DK)], cos2v, sin2v) * (DK ** -0.5) for h in hs]
        vv = [v_ref[:, _hcols(h, DV)] for h in hs]
        dxo = [d_o[h] * cst[h][1] for h in hs]
        sc = [_dot_nt(qq[h], kk[h]) * cst[h][0] for h in hs]
        dsc = [_dot_nt(d_o[h], vv[h]) * cst[h][0] for h in hs]
        t1 = [_dot(kk[h] * cst[h][2], d_s[h]) for h in hs]
        t2 = [_dot_nt(dxo[h], s[h]) for h in hs]
        t3 = [_dot_nt(vv[h], d_s[h]) for h in hs]
        t4 = [_dot_tn(qq[h], dxo[h]) for h in hs]
        t5 = [_dot_tn(sc[h], d_o[h]) for h in hs]
        t6 = [_dot(dsc[h], kk[h]) for h in hs]
        t7 = [_dot_tn(dsc[h], qq[h]) for h in hs]
        for h in hs:
            dst[h] = d_s[h] * cst[h][3] + t4[h]
            d_ref[:, 2 * nqk + h * DV:2 * nqk + (h + 1) * DV] = (t5[h] + t1[h]).astype(BF16)
            d_ref[:, _hcols(h, DK)] = _unrot(t6[h] + t2[h], cos2v, sin2v).astype(BF16)
            d_ref[:, nqk + h * DK:nqk + (h + 1) * DK] = (
                _unrot(t7[h] + t3[h] * cst[h][2], cos2v, sin2v) * (DK ** -0.5)).astype(BF16)

    return pl.pallas_call(
        body, name=name, grid=(HEADS // HB, n_chunks),
        in_specs=[pl.BlockSpec((CHUNK, HB * DK), lambda g, n: (rv(n), bq + g)),
                  pl.BlockSpec((CHUNK, HB * DK), lambda g, n: (rv(n), bk + g)),
                  pl.BlockSpec((CHUNK, HB * DV), lambda g, n: (rv(n), bv + g)),
                  pl.BlockSpec((CHUNK, DK), lambda g, n: (rv(n), 0)),
                  pl.BlockSpec((CHUNK, DK), lambda g, n: (rv(n), 0)),
                  pl.BlockSpec((HB, 1, LANES), lambda g, n: (g, 0, 0)),
                  per(DK, DV), tok(DV), pl.BlockSpec(memory_space=pl.ANY)],
        out_specs=pl.BlockSpec((CHUNK, wid), lambda g, n: (rv(n), col_q // wid)),
        out_shape=jax.ShapeDtypeStruct(dproj.shape, BF16),
        scratch_shapes=[pltpu.VMEM((HB, DK, DV), F32)],
        input_output_aliases={8: 0},
        compiler_params=_params(("arbitrary", "arbitrary")),
    )(proj, proj, proj, cos2, sin2, lg_tab, s_all, do, dproj)


def _merge_parts(oa, ob, z, rg, ga, gb):
    ra = lax.rsqrt(jnp.mean(oa * oa, axis=-1, keepdims=True) + EPS)
    xa = oa * ra
    mu = jnp.mean(ob, axis=-1, keepdims=True)
    cen = ob - mu
    rb = lax.rsqrt(jnp.mean(cen * cen, axis=-1, keepdims=True) + EPS)
    xb = cen * rb
    sz, sr = _sigmoid(z), _sigmoid(rg)
    return ra, xa, rb, xb, sz, sr, _sigmoid(ga), _sigmoid(gb)


def _merge_specs(tt, d, cz, crg, cga, cgb):
    blk = pl.BlockSpec((tt, d), lambda i: (i, 0))
    pcol = lambda c: pl.BlockSpec((tt, d), lambda i: (i, c // d))
    row = pl.BlockSpec((1, d), lambda i: (0, 0))
    return blk, [blk, blk, pcol(cz), pcol(crg), pcol(cga), pcol(cgb), row, row], row


def _merge_fwd(oa, ob, proj, wa, wb, cz, crg, cga, cgb, *, name):
    t, d = oa.shape
    tt = _pick(t, (256,))
    blk, in_specs, _ = _merge_specs(tt, d, cz, crg, cga, cgb)

    def body(oa_ref, ob_ref, z_ref, rg_ref, ga_ref, gb_ref, wa_ref, wb_ref, o_ref):
        for h in range(HEADS):
            cols = slice(h * DV, (h + 1) * DV)
            z, rg = z_ref[:, cols], rg_ref[:, cols]
            _, xa, _, xb, sz, sr, sga, sgb = _merge_parts(
                oa_ref[:, cols], ob_ref[:, cols], z, rg, ga_ref[:, cols], gb_ref[:, cols])
            o_a = xa * wa_ref[:, cols] * (z * sz)
            o_b = xb * wb_ref[:, cols] * (rg * sr)
            o_ref[:, cols] = (sga * o_a + sgb * o_b).astype(BF16)

    return pl.pallas_call(
        body, name=name, grid=(t // tt,), in_specs=in_specs, out_specs=blk,
        out_shape=jax.ShapeDtypeStruct((t, d), BF16), compiler_params=_params(("parallel",)),
    )(oa, ob, proj, proj, proj, proj, wa, wb)


def _merge_bwd(dmix, oa, ob, proj, wa, wb, cz, crg, cga, cgb, *, name):
    t, d = oa.shape
    tt = _pick(t, (256,))
    blk, in_specs, row = _merge_specs(tt, d, cz, crg, cga, cgb)
    assert (crg, cga, cgb) == (cz + d, cz + 2 * d, cz + 3 * d) and cz % (4 * d) == 0

    def body(dm_ref, oa_ref, ob_ref, z_ref, rg_ref, ga_ref, gb_ref, wa_ref, wb_ref,
             doa_ref, dob_ref, dp_ref, dwa_ref, dwb_ref):
        dz_ref, drg_ref, dga_ref, dgb_ref = [dp_ref.at[:, i * d:(i + 1) * d] for i in range(4)]

        @pl.when(pl.program_id(0) == 0)
        def _():
            dwa_ref[...] = jnp.zeros_like(dwa_ref)
            dwb_ref[...] = jnp.zeros_like(dwb_ref)

        for h in range(HEADS):
            cols = slice(h * DV, (h + 1) * DV)
            z, rg, wa_h, wb_h, dmx = z_ref[:, cols], rg_ref[:, cols], wa_ref[:, cols], wb_ref[:, cols], dm_ref[:, cols]
            ra, xa, rb, xb, sz, sr, sga, sgb = _merge_parts(
                oa_ref[:, cols], ob_ref[:, cols], z, rg, ga_ref[:, cols], gb_ref[:, cols])
            na, nb = xa * wa_h, xb * wb_h
            sil_z, sil_r = z * sz, rg * sr
            o_a, o_b = na * sil_z, nb * sil_r
            dga_ref[:, cols] = (dmx * o_a * sga * (1.0 - sga)).astype(BF16)
            dgb_ref[:, cols] = (dmx * o_b * sgb * (1.0 - sgb)).astype(BF16)
            d_oa, d_ob = dmx * sga, dmx * sgb
            dz_ref[:, cols] = (d_oa * na * sz * (1.0 + z * (1.0 - sz))).astype(BF16)
            drg_ref[:, cols] = (d_ob * nb * sr * (1.0 + rg * (1.0 - sr))).astype(BF16)
            dna, dnb = d_oa * sil_z, d_ob * sil_r
            dwa_ref[:, cols] += jnp.sum(dna * xa, axis=0, keepdims=True)
            dwb_ref[:, cols] += jnp.sum(dnb * xb, axis=0, keepdims=True)
            gwa, gwb = dna * wa_h, dnb * wb_h
            doa_ref[:, cols] = ra * (gwa - xa * jnp.mean(gwa * xa, axis=-1, keepdims=True))
            dob_ref[:, cols] = rb * (gwb - jnp.mean(gwb, axis=-1, keepdims=True)
                                     - xb * jnp.mean(gwb * xb, axis=-1, keepdims=True))

    return pl.pallas_call(
        body, name=name, grid=(t // tt,), in_specs=[blk] + in_specs,
        out_specs=[blk, blk, pl.BlockSpec((tt, 4 * d), lambda i: (i, cz // (4 * d))), row, row],
        out_shape=[jax.ShapeDtypeStruct((t, d), F32)] * 2 + [jax.ShapeDtypeStruct(proj.shape, BF16)]
        + [jax.ShapeDtypeStruct((1, d), F32)] * 2,
        compiler_params=_params(("arbitrary",)),
    )(dmix, oa, ob, proj, proj, proj, proj, wa, wb)


def _row_block(rows, cols, itemsize=4, target=1 << 20):
    for rb in (512, 256, 128, 64, 32, 16, 8):
        if rows % rb == 0 and rb * cols * itemsize <= target:
            return rb
    return rows


def _adamw(w, g, m, v, *, name):
    rows, cols = w.shape
    rb = _row_block(rows, cols)

    def body(w_ref, g_ref, m_ref, v_ref, d_ref, nm_ref, nv_ref):
        gg = g_ref[...]
        mm = ADAM_B1 * m_ref[...] + (1.0 - ADAM_B1) * gg
        vv = ADAM_B2 * v_ref[...] + (1.0 - ADAM_B2) * (gg * gg)
        m_hat = mm / (1.0 - ADAM_B1 ** ADAM_STEP)
        v_hat = vv / (1.0 - ADAM_B2 ** ADAM_STEP)
        d_ref[...] = -ADAM_LR * (m_hat / (jnp.sqrt(v_hat) + ADAM_EPS) + ADAM_WD * w_ref[...])
        nm_ref[...] = mm
        nv_ref[...] = vv

    blk = pl.BlockSpec((rb, cols), lambda i: (i, 0))
    return pl.pallas_call(
        body, name=name, grid=(rows // rb,), in_specs=[blk] * 4, out_specs=[blk] * 3,
        out_shape=[jax.ShapeDtypeStruct((rows, cols), F32)] * 3, compiler_params=_params(("parallel",)),
    )(w, g, m, v)


def _adamw_halves(w, mine, other, c_idx, m, v, *, name):
    rows, cols = w.shape
    hr = rows // 2
    rb = _row_block(hr, cols)
    nb = hr // rb

    def body(c_ref, w_ref, a_ref, b_ref, m_ref, v_ref, g_ref, d_ref, nm_ref, nv_ref):
        gg = jnp.where(pl.program_id(0) // nb == c_ref[0], a_ref[...], b_ref[...])
        mm = ADAM_B1 * m_ref[...] + (1.0 - ADAM_B1) * gg
        vv = ADAM_B2 * v_ref[...] + (1.0 - ADAM_B2) * (gg * gg)
        m_hat = mm / (1.0 - ADAM_B1 ** ADAM_STEP)
        v_hat = vv / (1.0 - ADAM_B2 ** ADAM_STEP)
        g_ref[...] = gg
        d_ref[...] = -ADAM_LR * (m_hat / (jnp.sqrt(v_hat) + ADAM_EPS) + ADAM_WD * w_ref[...])
        nm_ref[...] = mm
        nv_ref[...] = vv

    blk = pl.BlockSpec((rb, cols), lambda i, c: (i, 0))
    half = pl.BlockSpec((rb, cols), lambda i, c: (i % nb, 0))
    return pl.pallas_call(
        body, name=name,
        grid_spec=pltpu.PrefetchScalarGridSpec(
            num_scalar_prefetch=1, grid=(rows // rb,), in_specs=[blk, half, half, blk, blk], out_specs=[blk] * 4),
        out_shape=[jax.ShapeDtypeStruct((rows, cols), F32)] * 4, compiler_params=_params(("parallel",)),
    )(c_idx, w, mine, other, m, v)


def _pair_add(g, rsib, c_idx, *, name):
    _, _, hr, cols = g.shape
    rb = _row_block(hr, cols, 2)

    def body(c_ref, g_ref, r_ref, o_ref):
        o_ref[...] = (g_ref[...].astype(F32) + r_ref[...].astype(F32)).astype(BF16)

    return pl.pallas_call(
        body, name=name,
        grid_spec=pltpu.PrefetchScalarGridSpec(
            num_scalar_prefetch=1, grid=(N_CHIPS, hr // rb),
            in_specs=[pl.BlockSpec((None, None, rb, cols), lambda j, i, c: (j, c[0], i, 0)),
                      pl.BlockSpec((None, rb, cols), lambda j, i, c: (j, i, 0))],
            out_specs=pl.BlockSpec((None, rb, cols), lambda j, i, c: (j, i, 0))),
        out_shape=jax.ShapeDtypeStruct((N_CHIPS, hr, cols), BF16),
        compiler_params=_params(("parallel", "parallel")),
    )(c_idx, g, rsib)


def _chip_sum(qb, *, name):
    _, hr, cols = qb.shape
    rb = _row_block(hr, cols, 2, 1 << 19)

    def body(q_ref, o_ref):
        acc = q_ref[0].astype(F32)
        for i in range(1, N_CHIPS):
            acc = acc + q_ref[i].astype(F32)
        o_ref[...] = acc

    return pl.pallas_call(
        body, name=name, grid=(hr // rb,),
        in_specs=[pl.BlockSpec((N_CHIPS, rb, cols), lambda i: (0, i, 0))],
        out_specs=pl.BlockSpec((rb, cols), lambda i: (i, 0)),
        out_shape=jax.ShapeDtypeStruct((hr, cols), F32), compiler_params=_params(("parallel",)),
    )(qb)


def _place():
    x, y, c = lax.axis_index("x"), lax.axis_index("y"), lax.axis_index("c")
    return x, y, c, [(1 - x, y), (x, 1 - y), (1 - x, 1 - y)]


ANY = pl.BlockSpec(memory_space=pl.ANY)


def _gather_side(shards):
    nw = len(shards)

    def plan(ins, outs, ssem, rsem):
        x, y, c, chips = _place()
        s_me = 2 * x + y

        def rows(w, half):
            hr = shards[w].shape[0] // 2
            return pl.ds(pl.multiple_of(half * hr, 16), hr)

        def rcopy(w, k, src, slot, half, to):
            return pltpu.make_async_remote_copy(
                src_ref=src, dst_ref=outs[w].at[slot, rows(w, half)],
                send_sem=ssem.at[w * 6 + k], recv_sem=rsem.at[w * 6 + k], device_id=to, device_id_type=MESH)

        out = []
        for w in range(nw):
            mine = ins[w].at[rows(w, c)]
            for j, chip in enumerate(chips):
                s_j = 2 * chip[0] + chip[1]
                out.append((functools.partial(rcopy, w, j, mine, s_me, c, (chip[0], chip[1], c)),
                            functools.partial(rcopy, w, j, mine, s_j, c, (x, y, c)),
                            functools.partial(rcopy, w, 3 + j, outs[w].at[s_j, rows(w, c)], s_j, c, (x, y, 1 - c)),
                            functools.partial(rcopy, w, 3 + j, mine, s_j, 1 - c, (x, y, c))))
        return out

    def start(*refs):
        for send, _, _, _ in plan(*refs):
            send().start()

    def middle(*refs):
        for _, landed, forward, _ in plan(*refs):
            landed().wait_recv()
            forward().start()

    def finish(*refs):
        for send, _, forward, from_sibling in plan(*refs):
            from_sibling().wait_recv()
            send().wait_send()
            forward().wait_send()

    return _Side(shards, [jax.ShapeDtypeStruct((N_CHIPS,) + s.shape, s.dtype) for s in shards], nw * 6,
                 [start, middle, finish])


def _run_side(side, *, name):
    n_in, n_out = len(side.ins), len(side.out_shapes)

    def body(*refs):
        ins, outs = refs[:n_in], refs[n_in:n_in + n_out]
        for phase in side.phases:
            phase(ins, outs, refs[-2], refs[-1])

    return pl.pallas_call(
        body, name=name, in_specs=[ANY] * n_in, out_specs=[ANY] * n_out, out_shape=side.out_shapes,
        scratch_shapes=side.sems(), compiler_params=pltpu.CompilerParams(has_side_effects=True),
    )(*side.ins)


def _pair_exchange(gs, *, name):
    nw = len(gs)

    def body(*refs):
        ins, outs = refs[:nw], refs[nw:2 * nw]
        ssem, rsem = refs[2 * nw:]
        x, y, c, _ = _place()
        cps = []
        for w in range(nw):
            cp = pltpu.make_async_remote_copy(
                src_ref=ins[w].at[:, 1 - c], dst_ref=outs[w], send_sem=ssem.at[w], recv_sem=rsem.at[w],
                device_id=(x, y, 1 - c), device_id_type=MESH)
            cp.start()
            cps.append(cp)
        for cp in cps:
            cp.wait()

    return pl.pallas_call(
        body, name=name, in_specs=[ANY] * nw, out_specs=[ANY] * nw,
        out_shape=[jax.ShapeDtypeStruct((g.shape[0],) + g.shape[2:], g.dtype) for g in gs],
        scratch_shapes=[pltpu.SemaphoreType.DMA((nw,)), pltpu.SemaphoreType.DMA((nw,))],
        compiler_params=pltpu.CompilerParams(has_side_effects=True),
    )(*gs)


def _scatter_side(ps):
    nw = len(ps)

    def copies(ins, outs, ssem, rsem):
        x, y, c, chips = _place()
        s_me = 2 * x + y
        return [pltpu.make_async_remote_copy(
            src_ref=ins[w].at[2 * chip[0] + chip[1]], dst_ref=outs[w].at[s_me],
            send_sem=ssem.at[w * 3 + j], recv_sem=rsem.at[w * 3 + j],
            device_id=(chip[0], chip[1], c), device_id_type=MESH)
            for w in range(nw) for j, chip in enumerate(chips)]

    def start(*refs):
        for cp in copies(*refs):
            cp.start()

    def finish(*refs):
        for cp in copies(*refs):
            cp.wait()

    return _Side(ps, [jax.ShapeDtypeStruct(p.shape, p.dtype) for p in ps], nw * 3, [start, finish])


def _pair_share(hs, *, name):
    nw = len(hs)

    def body(*refs):
        ins, outs = refs[:nw], refs[nw:2 * nw]
        ssem, rsem = refs[2 * nw:]
        x, y, c, _ = _place()
        cps = []
        for w in range(nw):
            cp = pltpu.make_async_remote_copy(
                src_ref=ins[w], dst_ref=outs[w], send_sem=ssem.at[w], recv_sem=rsem.at[w],
                device_id=(x, y, 1 - c), device_id_type=MESH)
            cp.start()
            cps.append(cp)
        for cp in cps:
            cp.wait()

    return pl.pallas_call(
        body, name=name, in_specs=[ANY] * nw, out_specs=[ANY] * nw,
        out_shape=[jax.ShapeDtypeStruct(h.shape, h.dtype) for h in hs],
        scratch_shapes=[pltpu.SemaphoreType.DMA((nw,)), pltpu.SemaphoreType.DMA((nw,))],
        compiler_params=pltpu.CompilerParams(has_side_effects=True),
    )(*hs)


def _gather_small(a, *, name):
    def body(a_ref, o_ref, ssem, rsem):
        x, y, c, chips = _place()
        s_me = 2 * x + y
        o_ref[s_me] = a_ref[...]
        cps = []
        for j, chip in enumerate(chips):
            cp = pltpu.make_async_remote_copy(
                src_ref=a_ref, dst_ref=o_ref.at[s_me], send_sem=ssem.at[j], recv_sem=rsem.at[j],
                device_id=(chip[0], chip[1], c), device_id_type=MESH)
            cp.start()
            cps.append(cp)
        for cp in cps:
            cp.wait()

    vm = pl.BlockSpec(memory_space=pltpu.VMEM)
    return pl.pallas_call(
        body, name=name, in_specs=[vm], out_specs=vm,
        out_shape=jax.ShapeDtypeStruct((N_CHIPS,) + a.shape, a.dtype),
        scratch_shapes=[pltpu.SemaphoreType.DMA((3,)), pltpu.SemaphoreType.DMA((3,))],
    )(a)


def _allreduce_small(p, *, name):
    def body(p_ref, o_ref, buf, ssem, rsem):
        x, y, c, _ = _place()
        me = 4 * x + 2 * y + c
        buf[me] = p_ref[...]
        cps = []
        for k in range(1, N_DEV):
            fx, fy, fc = (k >> 2) & 1, (k >> 1) & 1, k & 1
            peer = (x + fx - 2 * x * fx, y + fy - 2 * y * fy, c + fc - 2 * c * fc)
            cp = pltpu.make_async_remote_copy(
                src_ref=buf.at[me], dst_ref=buf.at[me], send_sem=ssem.at[k - 1], recv_sem=rsem.at[k - 1],
                device_id=peer, device_id_type=MESH)
            cp.start()
            cps.append(cp)
        for cp in cps:
            cp.wait()
        acc = buf[0]
        for i in range(1, N_DEV):
            acc = acc + buf[i]
        o_ref[...] = acc

    vm = pl.BlockSpec(memory_space=pltpu.VMEM)
    return pl.pallas_call(
        body, name=name, in_specs=[vm], out_specs=vm,
        out_shape=jax.ShapeDtypeStruct(p.shape, p.dtype),
        scratch_shapes=[pltpu.VMEM((N_DEV,) + p.shape, p.dtype), pltpu.SemaphoreType.DMA((N_DEV - 1,)),
                        pltpu.SemaphoreType.DMA((N_DEV - 1,))],
    )(p)


def _rows_to_tokens(r):
    h, n = r.shape[0], r.shape[1]
    return r.reshape(h, n * CHUNK).T


def _tokens_to_rows(a):
    t, h = a.shape
    return a.T.reshape(h, t // CHUNK, 1, CHUNK)


def kernel(x, norm1_w, w_in, conv_w, a_log, dt_bias, gdn_norm_w, ret_norm_w, w_out, norm2_w, w_gate, w_up, w_down, norm_f_w, loss_target, m_norm1_w, m_w_in, m_conv_w, m_a_log, m_dt_bias, m_gdn_norm_w, m_ret_norm_w, m_w_out, m_norm2_w, m_w_gate, m_w_up, m_w_down, m_norm_f_w, v_norm1_w, v_w_in, v_conv_w, v_a_log, v_dt_bias, v_gdn_norm_w, v_ret_norm_w, v_w_out, v_norm2_w, v_w_gate, v_w_up, v_w_down, v_norm_f_w):
    t, d = x.shape[1], x.shape[2]
    f = w_gate.shape[2] * N_CHIPS
    nqk, nv = HEADS * DK, HEADS * DV
    ncs = w_in.shape[2]
    c_idx = lax.axis_index("c")
    s_idx = 2 * lax.axis_index("x") + lax.axis_index("y")
    xs = x[0]
    tgt = loss_target[0]

    n_small = 2 * HEADS
    widths = [2 * nqk + nv, nv, n_small, nqk, nqk, nv, nv, d, d]
    g_off = np.concatenate([[0], np.cumsum(widths)])
    order = [0, 3, 4, 5, 1, 6, 7, 8]
    m_off = np.concatenate([[0], np.cumsum([widths[i] for i in order])])
    o_rq, o_rk, o_rv, o_az, o_rg, o_ga, o_gb = [int(m_off[i]) for i in range(1, 8)]
    segs = [(int(g_off[i]), int(g_off[i + 1]), int(m_off[order.index(i)]) if i != 2 else None) for i in range(9)]

    def shard_pieces(s):
        out = []
        for a, b, dst in segs:
            lo, hi = max(a, s * ncs), min(b, (s + 1) * ncs)
            if lo < hi:
                out.append((lo - s * ncs, hi - s * ncs, None if dst is None else dst + lo - a))
        return out

    own = [w_in[0].astype(BF16), w_out[0].astype(BF16), w_gate[0].astype(BF16), w_up[0].astype(BF16),
           w_down[0].astype(BF16)]
    put_own = lambda full, mine: lax.dynamic_update_slice(full, mine[None], (s_idx, 0, 0))
    wg_in = put_own(_run_side(_gather_side(own[:1]), name="gather_w_in")[0], own[0])
    conv_full = _gather_small(conv_w[0], name="gather_conv_w")
    conv_full = jnp.concatenate([conv_full[i] for i in range(N_CHIPS)], axis=1)
    cuts = [(dst, s, a, b) for s in range(N_CHIPS) for a, b, dst in shard_pieces(s)]
    w_main = jnp.concatenate([wg_in[s][:, a:b] for dst, s, a, b in sorted(c for c in cuts if c[0] is not None)],
                             axis=1)
    w_small = jnp.concatenate([wg_in[s][:, a:b] for dst, s, a, b in cuts if dst is None], axis=1)
    w_small = jnp.pad(w_small, ((0, 0), (0, LANES - n_small)))

    pad16 = lambda a: jnp.pad(a, ((0, 0), (HEADS, LANES - 2 * HEADS)))
    alog_row, dtb_row = pad16(a_log), pad16(dt_bias)
    wa_row = jnp.tile(gdn_norm_w, (1, HEADS))
    inv = ROPE_BASE ** (-jnp.arange(0, DK, 2, dtype=F32) / DK)
    ang = jnp.arange(t, dtype=F32)[:, None] * inv[None, :]
    cos2 = jnp.concatenate([jnp.cos(ang), jnp.cos(ang)], axis=1)
    sin2 = jnp.concatenate([-jnp.sin(ang), jnp.sin(ang)], axis=1)
    lg = jnp.log1p(-jnp.exp2(-5.0 - jnp.arange(HEADS, dtype=F32)))
    lg_tab = jnp.broadcast_to(lg[:, None, None], (HEADS, 1, LANES))

    fq = f // N_CHIPS
    u1, projs = _rms_fwd(xs, norm1_w, w_narrow=w_small, name="rms1_fwd")
    proj, *rest = _mm(u1, w_main, tm=512, tn=2048, tk=d, side=_gather_side(own[1:]), name="mm_proj")
    wg_out, wg_gate, wg_up, wg_down = [put_own(g, o) for g, o in zip(rest, own[1:])]
    w_o = wg_out.reshape(d, d)
    w_g = jnp.concatenate([wg_gate[i] for i in range(N_CHIPS)], axis=1)
    w_u = jnp.concatenate([wg_up[i] for i in range(N_CHIPS)], axis=1)
    w_d = wg_down.reshape(f, d)
    q_a, k_a, v_a = _conv_fwd(proj, conv_full, name="conv_fwd")
    bg = _bg_fwd(projs, alog_row, dtb_row, name="bg_fwd")
    beta_r = _tokens_to_rows(bg[:, :HEADS])
    g_r = _tokens_to_rows(bg[:, HEADS:2 * HEADS])
    gc_r, tinv, u_a, w_a, p_a = _gdn_prep(q_a, k_a, v_a, beta_r, g_r, name="gdn_prep")
    o_a, s_a = _gdn_scan(q_a, k_a, u_a, w_a, p_a, gc_r, name="gdn_scan")
    o_b, s_b = _ret_fwd(proj, cos2, sin2, lg_tab, o_rq, o_rk, o_rv, name="ret_fwd")
    mixed = _merge_fwd(o_a, o_b, proj, wa_row, ret_norm_w, o_az, o_rg, o_ga, o_gb, name="merge_fwd")
    h1 = _mm(mixed, w_o, res=xs, tm=512, tn=d, tk=d, name="mm_out")
    hn = _rms_fwd(h1, norm2_w, name="rms2_fwd")
    gt, up, act = _ffn_in(hn, w_g, w_u, tm=256, tn=fq, name="ffn_in")
    h2 = _mm(act, w_d, res=h1, tm=1024, tn=d, tk=512, name="mm_down")
    loss_row, dh2, dh2b, d_nf = _loss_head(h2, tgt, norm_f_w.reshape(1, d), name="loss_head")

    g_down = _mm(act, dh2b, ta=True, out_dtype=BF16, tm=fq, tn=d, tk=1024, name="mm_dw_down")
    dgt, dup = _ffn_back(dh2b, w_d, gt, up, tm=512, tn=fq, name="ffn_back")
    dhn = _mm(dgt, w_g, tb=True, pair=(dup, w_u), tm=1024, tn=d, tk=512, name="mm_dhn")
    g_gate = _mm(hn, dgt, ta=True, out_dtype=BF16, tm=1024, tn=fq, tk=1024, name="mm_dw_gate")
    g_up = _mm(hn, dup, ta=True, out_dtype=BF16, tm=1024, tn=fq, tk=1024, name="mm_dw_up")
    dh1, dh1b, d_n2 = _rms_bwd(dhn, h1, norm2_w, dh2, name="rms2_bwd")
    dmix = _mm(dh1b, w_o, tb=True, tm=512, tn=d, tk=d, name="mm_dmix")
    g_out = _mm(mixed, dh1b, ta=True, out_dtype=BF16, tm=1024, tn=d, tk=1024, name="mm_dw_out")
    col_split = lambda g, wd: jnp.stack([g[:, i * wd:(i + 1) * wd] for i in range(N_CHIPS)])
    halves = lambda g: g.reshape(N_CHIPS, 2, g.shape[1] // 2, g.shape[2])
    c_arr = jnp.reshape(c_idx, (1,)).astype(jnp.int32)
    gs_ffn = [halves(g_out.reshape(N_CHIPS, d // N_CHIPS, d)), halves(col_split(g_gate, fq)),
              halves(col_split(g_up, fq)), halves(g_down.reshape(N_CHIPS, fq, d))]
    rsib = _pair_exchange(gs_ffn, name="grad_pair_exchange_ffn")
    ps_ffn = [_pair_add(g, r, c_arr, name=f"grad_pair_add_{nm}")
              for g, r, nm in zip(gs_ffn, rsib, ["w_out", "w_gate", "w_up", "w_down"])]
    do_a, do_b, dproj, d_wa, d_wb = _merge_bwd(
        dmix, o_a, o_b, proj, wa_row, ret_norm_w, o_az, o_rg, o_ga, o_gb, name="merge_bwd")
    dproj = _ret_bwd(proj, cos2, sin2, lg_tab, s_b, do_b, dproj, o_rq, o_rk, o_rv, name="ret_bwd")
    dq_a, dk_a, dv_a, dbeta_r, dg_r = _gdn_bwd(q_a, k_a, v_a, beta_r, gc_r, tinv, u_a, w_a, s_a, do_a, name="gdn_bwd")
    dc = _conv_bwd_pre(proj, conv_full, dq_a, dk_a, dv_a, name="conv_bwd_pre")
    dproj, d_cw = _conv_bwd(proj, dc, conv_full, dproj, name="conv_bwd")
    dbg = jnp.pad(jnp.concatenate([_rows_to_tokens(dbeta_r), _rows_to_tokens(dg_r)], axis=1),
                  ((0, 0), (0, LANES - 2 * HEADS)))
    dprojs, d_alog, d_dtb = _bg_bwd(projs, dbg, alog_row, dtb_row, name="bg_bwd")
    g_main, *qs_ffn = _mm(u1, dproj, ta=True, out_dtype=BF16, tm=1024, tn=2048, tk=1024,
                          side=_scatter_side(ps_ffn), name="mm_dw_in")
    g_small = _mm(u1, dprojs, ta=True, out_dtype=BF16, tm=1024, tn=LANES, tk=1024, name="mm_dw_in_small")
    pieces = []
    for s in range(N_CHIPS):
        seen, parts = 0, []
        for a, b, dst in shard_pieces(s):
            parts.append(g_small[:, seen:seen + b - a] if dst is None else g_main[:, dst:dst + b - a])
            seen += (b - a) if dst is None else 0
        pieces.append(jnp.concatenate(parts, axis=1))
    gs_in = [halves(jnp.stack(pieces))]
    rsib = _pair_exchange(gs_in, name="grad_pair_exchange_in")
    ps_in = [_pair_add(gs_in[0], rsib[0], c_arr, name="grad_pair_add_w_in")]
    du, *qs_in = _mm(dproj, w_main, tb=True, tm=1024, tn=d, tk=1024, side=_scatter_side(ps_in), name="mm_du")
    dx, _, d_n1 = _rms_bwd(du, xs, norm1_w, dh1, narrow=(dprojs, w_small), name="rms1_bwd")

    names = ["w_in", "w_out", "w_gate", "w_up", "w_down"]
    qs = [lax.dynamic_update_slice(q, lax.dynamic_slice(p, (s_idx, 0, 0), (1,) + p.shape[1:]), (s_idx, 0, 0))
          for q, p in zip(qs_in + qs_ffn, ps_in + ps_ffn)]
    hs = [_chip_sum(q, name=f"grad_chip_sum_{nm}") for q, nm in zip(qs, names)]
    theirs = _pair_share(hs, name="grad_pair_share")
    big_w = [w_in[0], w_out[0], w_gate[0], w_up[0], w_down[0]]
    big_m = [m_w_in[0], m_w_out[0], m_w_gate[0], m_w_up[0], m_w_down[0]]
    big_v = [v_w_in[0], v_w_out[0], v_w_gate[0], v_w_up[0], v_w_down[0]]
    big = {}
    for nm, mine, other, w_, m_, v_ in zip(names, hs, theirs, big_w, big_m, big_v):
        big[nm] = tuple(a[None] for a in _adamw_halves(w_, mine, other, c_arr, m_, v_, name=f"adamw_{nm}"))

    d_wa_h = jnp.sum(d_wa.reshape(HEADS, DV), axis=0, keepdims=True)
    small = [d_n1, d_alog[:, HEADS:2 * HEADS], d_dtb[:, HEADS:2 * HEADS], d_wa_h, d_wb, d_n2, d_nf,
             d_cw[:CONV_W].reshape(1, -1)]
    sizes = [a.shape[1] for a in small]
    packed = jnp.concatenate(small, axis=1)
    n_pack = packed.shape[1]
    n_rows = -(-n_pack // LANES)
    n_rows = -(-n_rows // 8) * 8
    packed = jnp.pad(packed, ((0, 0), (0, n_rows * LANES - n_pack))).reshape(n_rows, LANES)
    red = _allreduce_small(packed, name="allreduce_small").reshape(1, -1)
    offs = np.cumsum([0] + sizes)
    g_n1, g_alog, g_dtb, g_wa, g_wb, g_n2, g_nf, g_cw = [red[:, offs[i]:offs[i + 1]] for i in range(len(sizes))]
    ncw = conv_w.shape[2]
    g_cw = lax.dynamic_slice(g_cw.reshape(CONV_W, -1), (0, s_idx * ncw), (CONV_W, ncw))

    def small_update(w_, g, m_, v_, nm):
        shape = w_.shape
        pad = (-w_.size) % LANES
        to2 = lambda a: jnp.pad(a.reshape(1, -1), ((0, 0), (0, pad)))
        outs = _adamw(to2(w_), to2(g), to2(m_), to2(v_), name=f"adamw_{nm}")
        return (g.reshape(shape),) + tuple(a[:, :w_.size].reshape(shape) for a in outs)

    res = {
        "norm1_w": small_update(norm1_w, g_n1, m_norm1_w, v_norm1_w, "norm1_w"),
        "w_in": big["w_in"],
        "conv_w": small_update(conv_w, g_cw, m_conv_w, v_conv_w, "conv_w"),
        "a_log": small_update(a_log, g_alog, m_a_log, v_a_log, "a_log"),
        "dt_bias": small_update(dt_bias, g_dtb, m_dt_bias, v_dt_bias, "dt_bias"),
        "gdn_norm_w": small_update(gdn_norm_w, g_wa, m_gdn_norm_w, v_gdn_norm_w, "gdn_norm_w"),
        "ret_norm_w": small_update(ret_norm_w, g_wb, m_ret_norm_w, v_ret_norm_w, "ret_norm_w"),
        "w_out": big["w_out"],
        "norm2_w": small_update(norm2_w, g_n2, m_norm2_w, v_norm2_w, "norm2_w"),
        "w_gate": big["w_gate"],
        "w_up": big["w_up"],
        "w_down": big["w_down"],
        "norm_f_w": small_update(norm_f_w, g_nf, m_norm_f_w, v_norm_f_w, "norm_f_w"),
    }
    order = ["norm1_w", "w_in", "conv_w", "a_log", "dt_bias", "gdn_norm_w", "ret_norm_w", "w_out", "norm2_w",
             "w_gate", "w_up", "w_down", "norm_f_w"]
    loss = lax.psum(loss_row[0, 0], ("x", "y", "c"))
    return (loss, dx[None], *[res[n][0] for n in order], *[res[n][1] for n in order],
            *[res[n][2] for n in order], *[res[n][3] for n in order])
```

```python
import functools

import jax
import jax.numpy as jnp
import numpy as np
from jax import lax
from jax.experimental import pallas as pl
from jax.experimental.pallas import tpu as pltpu

F32 = jnp.float32
BF16 = jnp.bfloat16
MESH = pl.DeviceIdType.MESH

HEADS = 8
DK = 128
DV = 256
CHUNK = 64
CONV_W = 4
EPS = 1e-6
ROPE_BASE = 10000.0
ADAM_LR, ADAM_B1, ADAM_B2, ADAM_EPS, ADAM_WD, ADAM_STEP = 0.001, 0.9, 0.999, 1e-08, 0.01, 10
N_CHIPS = 4
N_DEV = 8
LANES = 128
HALO = 8
VMEM_LIMIT = 56 * 1024 * 1024
HB = 8


def _pick(n, cands):
    for c in cands:
        if n % c == 0:
            return c
    raise ValueError(f"no tile for {n} in {cands}")


def _params(sem=None):
    return pltpu.CompilerParams(dimension_semantics=sem, vmem_limit_bytes=VMEM_LIMIT)


def _dot(a, b):
    return jnp.dot(a.astype(BF16), b.astype(BF16), preferred_element_type=F32)


def _dot_nt(a, b):
    return lax.dot_general(a.astype(BF16), b.astype(BF16), (((1,), (1,)), ((), ())), preferred_element_type=F32)


def _dot_tn(a, b):
    return lax.dot_general(a.astype(BF16), b.astype(BF16), (((0,), (0,)), ((), ())), preferred_element_type=F32)


def _sigmoid(x):
    return 1.0 / (1.0 + jnp.exp(-x))


def _iota2(n):
    return lax.broadcasted_iota(jnp.int32, (n, n), 0), lax.broadcasted_iota(jnp.int32, (n, n), 1)


def _row_to_col(row):
    n = row.shape[1]
    r, c = _iota2(n)
    return jnp.sum(jnp.where(r == c, jnp.broadcast_to(row, (n, n)), 0.0), axis=1, keepdims=True)


def _col_to_row(col):
    n = col.shape[0]
    r, c = _iota2(n)
    return jnp.sum(jnp.where(r == c, jnp.broadcast_to(col, (n, n)), 0.0), axis=0, keepdims=True)


class _Side:
    def __init__(self, ins, out_shapes, n_sem, phases):
        self.ins, self.out_shapes, self.n_sem, self.phases = list(ins), list(out_shapes), n_sem, phases

    def sems(self):
        return [pltpu.SemaphoreType.DMA((self.n_sem,)), pltpu.SemaphoreType.DMA((self.n_sem,))]


def _mm(a, b, *, name, tm, tn, tk, ta=False, tb=False, out_dtype=F32, res=None, side=None, pair=None):
    m, k = (a.shape[1], a.shape[0]) if ta else a.shape
    n = b.shape[0] if tb else b.shape[1]
    tm, tn, tk = min(tm, m), min(tn, n), min(tk, k)
    assert m % tm == 0 and n % tn == 0 and k % tk == 0, (name, m, n, k)
    nk = k // tk
    nj, ni = n // tn, m // tm
    dn = (((0 if ta else 1,), (1 if tb else 0,)), ((), ()))
    n_ab = 4 if pair else 2
    n_in = n_ab + (res is not None)
    n_side_in = len(side.ins) if side else 0
    n_side_out = len(side.out_shapes) if side else 0

    def body(*refs):
        a_ref, b_ref = refs[0], refs[1]
        r_ref = refs[n_ab] if res is not None else None
        o_ref = refs[n_in + n_side_in]
        if side:
            s_in = refs[n_in:n_in + n_side_in]
            s_out = refs[n_in + n_side_in + 1:n_in + n_side_in + 1 + n_side_out]
            ssem, rsem = refs[-2], refs[-1]
            j_, i_, k_ = pl.program_id(0), pl.program_id(1), pl.program_id(2)
            when = [(j_ == 0) & (i_ == 0) & (k_ == 0)]
            if len(side.phases) == 3:
                when.append((j_ == (3 * nj) // 4) & (i_ == 0) & (k_ == 0))
            when.append((j_ == nj - 1) & (i_ == ni - 1) & (k_ == nk - 1))
            assert len(when) == len(side.phases) and (len(when) == 2 or nj >= 2)

        def run_phase(p):
            @pl.when(when[p])
            def _():
                side.phases[p](s_in, s_out, ssem, rsem)

        if side:
            for p in range(len(side.phases) - 1):
                run_phase(p)

        def finish(r):
            if res is not None:
                r = r + r_ref[...]
            o_ref[...] = r.astype(out_dtype)

        part = lax.dot_general(a_ref[...], b_ref[...], dn, preferred_element_type=F32)
        if pair:
            part = part + lax.dot_general(refs[2][...], refs[3][...], dn, preferred_element_type=F32)
        if nk == 1:
            finish(part)
        else:
            acc = refs[n_in + n_side_in + 1 + n_side_out]
            kk = pl.program_id(2)

            @pl.when(kk == 0)
            def _():
                acc[...] = part

            @pl.when((kk > 0) & (kk < nk - 1))
            def _():
                acc[...] += part

            @pl.when(kk == nk - 1)
            def _():
                finish(acc[...] + part)

        if side:
            run_phase(len(side.phases) - 1)

    a_spec = pl.BlockSpec((tk, tm), lambda j, i, kk: (kk, i)) if ta else pl.BlockSpec((tm, tk), lambda j, i, kk: (i, kk))
    b_spec = pl.BlockSpec((tn, tk), lambda j, i, kk: (j, kk)) if tb else pl.BlockSpec((tk, tn), lambda j, i, kk: (kk, j))
    o_spec = pl.BlockSpec((tm, tn), lambda j, i, kk: (i, j))
    in_specs, args = [a_spec, b_spec], [a, b]
    if pair:
        assert pair[0].shape == a.shape and pair[1].shape == b.shape
        in_specs += [a_spec, b_spec]
        args += list(pair)
    if res is not None:
        in_specs.append(o_spec)
        args.append(res)
    out_specs, out_shape = o_spec, jax.ShapeDtypeStruct((m, n), out_dtype)
    scratch = [pltpu.VMEM((tm, tn), F32)] if nk > 1 else []
    sem = ("parallel", "parallel", "arbitrary")
    if side:
        hbm = pl.BlockSpec(memory_space=pl.ANY)
        in_specs += [hbm] * n_side_in
        args += side.ins
        out_specs, out_shape = [o_spec] + [hbm] * n_side_out, [out_shape] + side.out_shapes
        scratch += side.sems()
        sem = ("arbitrary",) * 3
    return pl.pallas_call(
        body, name=name, grid=(nj, ni, nk), in_specs=in_specs, out_specs=out_specs, out_shape=out_shape,
        scratch_shapes=scratch, compiler_params=_params(sem),
    )(*args)


def _rms_fwd(x, w, *, name, w_narrow=None):
    t, d = x.shape
    tt = _pick(t, (512, 256))

    def body(*refs):
        x_ref, w_ref, o_ref = refs[0], refs[1], refs[-2 if w_narrow is not None else -1]
        xv = x_ref[...]
        r = lax.rsqrt(jnp.mean(xv * xv, axis=-1, keepdims=True) + EPS)
        u = (xv * r * w_ref[...]).astype(BF16)
        o_ref[...] = u
        if w_narrow is not None:
            refs[-1][...] = jnp.dot(u, refs[2][...], preferred_element_type=F32)

    blk = pl.BlockSpec((tt, d), lambda i: (i, 0))
    in_specs, args = [blk, pl.BlockSpec((1, d), lambda i: (0, 0))], [x, w]
    out_specs, out_shape = [blk], [jax.ShapeDtypeStruct((t, d), BF16)]
    if w_narrow is not None:
        in_specs.append(pl.BlockSpec(w_narrow.shape, lambda i: (0, 0)))
        args.append(w_narrow)
        out_specs.append(pl.BlockSpec((tt, LANES), lambda i: (i, 0)))
        out_shape.append(jax.ShapeDtypeStruct((t, LANES), F32))
    out = pl.pallas_call(
        body, name=name, grid=(t // tt,), in_specs=in_specs, out_specs=out_specs, out_shape=out_shape,
        compiler_params=_params(("parallel",)),
    )(*args)
    return out if w_narrow is not None else out[0]


def _rms_bwd(dn, x, w, dres, *, name, narrow=None):
    t, d = x.shape
    tt = _pick(t, (256,))

    def body(*refs):
        dn_ref, x_ref, w_ref, dres_ref = refs[:4]
        dx_ref, dxb_ref, dw_ref = refs[-3:]
        xv, g = x_ref[...], dn_ref[...].astype(F32)
        if narrow is not None:
            g = g + lax.dot_general(refs[4][...], refs[5][...], (((1,), (1,)), ((), ())), preferred_element_type=F32)
        r = lax.rsqrt(jnp.mean(xv * xv, axis=-1, keepdims=True) + EPS)
        xh = xv * r
        gw = g * w_ref[...]
        dx = dres_ref[...] + r * (gw - xh * jnp.mean(gw * xh, axis=-1, keepdims=True))
        dx_ref[...] = dx
        dxb_ref[...] = dx.astype(BF16)

        @pl.when(pl.program_id(0) == 0)
        def _():
            dw_ref[...] = jnp.zeros_like(dw_ref)

        dw_ref[...] += jnp.sum(g * xh, axis=0, keepdims=True)

    blk = pl.BlockSpec((tt, d), lambda i: (i, 0))
    row = pl.BlockSpec((1, d), lambda i: (0, 0))
    in_specs, args = [blk, blk, row, blk], [dn, x, w, dres]
    if narrow is not None:
        in_specs += [pl.BlockSpec((tt, LANES), lambda i: (i, 0)), pl.BlockSpec(narrow[1].shape, lambda i: (0, 0))]
        args += list(narrow)
    return pl.pallas_call(
        body, name=name, grid=(t // tt,), in_specs=in_specs, out_specs=[blk, blk, row],
        out_shape=[jax.ShapeDtypeStruct((t, d), F32), jax.ShapeDtypeStruct((t, d), BF16),
                   jax.ShapeDtypeStruct((1, d), F32)],
        compiler_params=_params(("arbitrary",)),
    )(*args)


def _loss_head(h2, tgt, wf, *, name):
    t, d = h2.shape
    tt = _pick(t, (256,))

    def body(x_ref, t_ref, w_ref, loss_ref, dx_ref, dxb_ref, dw_ref):
        xv = x_ref[...]
        r = lax.rsqrt(jnp.mean(xv * xv, axis=-1, keepdims=True) + EPS)
        xh = xv * r
        err = xh * w_ref[...] - t_ref[...]
        lpart = 0.5 * jnp.sum(jnp.mean(err * err, axis=-1, keepdims=True), axis=0, keepdims=True)
        dy = err * (1.0 / d)
        gw = dy * w_ref[...]
        dx = r * (gw - xh * jnp.mean(gw * xh, axis=-1, keepdims=True))
        dx_ref[...] = dx
        dxb_ref[...] = dx.astype(BF16)

        @pl.when(pl.program_id(0) == 0)
        def _():
            dw_ref[...] = jnp.zeros_like(dw_ref)
            loss_ref[...] = jnp.zeros_like(loss_ref)

        dw_ref[...] += jnp.sum(dy * xh, axis=0, keepdims=True)
        loss_ref[...] += jnp.broadcast_to(lpart, loss_ref.shape)

    blk = pl.BlockSpec((tt, d), lambda i: (i, 0))
    row = pl.BlockSpec((1, d), lambda i: (0, 0))
    lrow = pl.BlockSpec((1, LANES), lambda i: (0, 0))
    return pl.pallas_call(
        body, name=name, grid=(t // tt,),
        in_specs=[blk, blk, row], out_specs=[lrow, blk, blk, row],
        out_shape=[jax.ShapeDtypeStruct((1, LANES), F32), jax.ShapeDtypeStruct((t, d), F32),
                   jax.ShapeDtypeStruct((t, d), BF16), jax.ShapeDtypeStruct((1, d), F32)],
        compiler_params=_params(("arbitrary",)),
    )(h2, tgt, wf)


def _ffn_in(hn, w_g, w_u, *, name, tm, tn):
    t, d = hn.shape
    f = w_g.shape[1]
    tm = min(tm, t)

    def body(a_ref, g_ref, u_ref, gt_ref, up_ref, act_ref):
        a = a_ref[...]
        g = jnp.dot(a, g_ref[...], preferred_element_type=F32)
        u = jnp.dot(a, u_ref[...], preferred_element_type=F32)
        gt_ref[...] = g.astype(BF16)
        up_ref[...] = u.astype(BF16)
        act_ref[...] = (g * _sigmoid(g) * u).astype(BF16)

    wblk = pl.BlockSpec((d, tn), lambda j, i: (0, j))
    oblk = pl.BlockSpec((tm, tn), lambda j, i: (i, j))
    return pl.pallas_call(
        body, name=name, grid=(f // tn, t // tm),
        in_specs=[pl.BlockSpec((tm, d), lambda j, i: (i, 0)), wblk, wblk], out_specs=[oblk] * 3,
        out_shape=[jax.ShapeDtypeStruct((t, f), BF16)] * 3,
        compiler_params=_params(("parallel", "parallel")),
    )(hn, w_g, w_u)


def _ffn_back(dh2b, w_d, gt, up, *, name, tm, tn):
    t, d = dh2b.shape
    f = w_d.shape[0]
    tm = min(tm, t)

    def body(a_ref, w_ref, g_ref, u_ref, dg_ref, du_ref):
        da = lax.dot_general(a_ref[...], w_ref[...], (((1,), (1,)), ((), ())), preferred_element_type=F32)
        g = g_ref[...].astype(F32)
        sg = _sigmoid(g)
        dg_ref[...] = (da * u_ref[...].astype(F32) * sg * (1.0 + g * (1.0 - sg))).astype(BF16)
        du_ref[...] = (da * g * sg).astype(BF16)

    oblk = pl.BlockSpec((tm, tn), lambda j, i: (i, j))
    return pl.pallas_call(
        body, name=name, grid=(f // tn, t // tm),
        in_specs=[pl.BlockSpec((tm, d), lambda j, i: (i, 0)), pl.BlockSpec((tn, d), lambda j, i: (j, 0)), oblk, oblk],
        out_specs=[oblk] * 2,
        out_shape=[jax.ShapeDtypeStruct((t, f), BF16)] * 2,
        compiler_params=_params(("parallel", "parallel")),
    )(dh2b, w_d, gt, up)


def _conv_cols(xs, cw_ref, cb, tt):
    cols = slice(cb * LANES, (cb + 1) * LANES)
    base = HALO - (CONV_W - 1)
    acc = xs[base:base + tt, cols] * cw_ref[0:1, cols]
    for j in range(1, CONV_W):
        acc = acc + xs[base + j:base + j + tt, cols] * cw_ref[j:j + 1, cols]
    return acc


def _fill_halo(xs, x_ref, xp_ref, tt):
    first = pl.program_id(0) == 0
    xs[0:HALO, :] = jnp.where(first, 0.0, xp_ref[...])
    xs[HALO:HALO + tt, :] = x_ref[...]


def _conv_specs(tt, ch):
    cur = pl.BlockSpec((tt, ch), lambda i: (i, 0))
    prev = pl.BlockSpec((HALO, ch), lambda i: (jnp.maximum(i * (tt // HALO) - 1, 0), 0))
    return cur, prev


def _conv_fwd(proj, conv_w, *, name):
    t = proj.shape[0]
    ch = conv_w.shape[1]
    nqk = HEADS * DK
    tt = _pick(t, (256,))
    cur, prev = _conv_specs(tt, ch)

    def body(x_ref, xp_ref, cw_ref, q_ref, k_ref, v_ref, xs):
        _fill_halo(xs, x_ref, xp_ref, tt)
        for cb in range(ch // LANES):
            c = _conv_cols(xs, cw_ref, cb, tt)
            s = c * _sigmoid(c)
            if cb < 2 * HEADS:
                s = s * lax.rsqrt(jnp.sum(s * s, axis=-1, keepdims=True) + EPS)
                if cb < HEADS:
                    q_ref[:, cb * LANES:(cb + 1) * LANES] = s * (DK ** -0.5)
                else:
                    k_ref[:, (cb - HEADS) * LANES:(cb - HEADS + 1) * LANES] = s
            else:
                v_ref[:, (cb - 2 * HEADS) * LANES:(cb - 2 * HEADS + 1) * LANES] = s

    return pl.pallas_call(
        body, name=name, grid=(t // tt,),
        in_specs=[cur, prev, pl.BlockSpec((CONV_W, ch), lambda i: (0, 0))],
        out_specs=[pl.BlockSpec((tt, nqk), lambda i: (i, 0)), pl.BlockSpec((tt, nqk), lambda i: (i, 0)),
                   pl.BlockSpec((tt, ch - 2 * nqk), lambda i: (i, 0))],
        out_shape=[jax.ShapeDtypeStruct((t, nqk), F32), jax.ShapeDtypeStruct((t, nqk), F32),
                   jax.ShapeDtypeStruct((t, ch - 2 * nqk), F32)],
        scratch_shapes=[pltpu.VMEM((HALO + tt, ch), F32)],
        compiler_params=_params(("arbitrary",)),
    )(proj, proj, conv_w)


def _conv_bwd_pre(proj, conv_w, dq, dk, dv, *, name):
    t = proj.shape[0]
    ch = conv_w.shape[1]
    nqk = HEADS * DK
    tt = _pick(t, (256,))
    cur, prev = _conv_specs(tt, ch)

    def body(x_ref, xp_ref, cw_ref, dq_ref, dk_ref, dv_ref, dc_ref, xs):
        _fill_halo(xs, x_ref, xp_ref, tt)
        for cb in range(ch // LANES):
            c = _conv_cols(xs, cw_ref, cb, tt)
            sg = _sigmoid(c)
            s = c * sg
            if cb < 2 * HEADS:
                if cb < HEADS:
                    d, scale = dq_ref[:, cb * LANES:(cb + 1) * LANES], DK ** -0.5
                else:
                    d, scale = dk_ref[:, (cb - HEADS) * LANES:(cb - HEADS + 1) * LANES], 1.0
                rinv = lax.rsqrt(jnp.sum(s * s, axis=-1, keepdims=True) + EPS)
                ds = scale * rinv * (d - s * (rinv * rinv) * jnp.sum(d * s, axis=-1, keepdims=True))
            else:
                ds = dv_ref[:, (cb - 2 * HEADS) * LANES:(cb - 2 * HEADS + 1) * LANES]
            dc_ref[:, cb * LANES:(cb + 1) * LANES] = ds * sg * (1.0 + c * (1.0 - sg))

    return pl.pallas_call(
        body, name=name, grid=(t // tt,),
        in_specs=[cur, prev, pl.BlockSpec((CONV_W, ch), lambda i: (0, 0)),
                  pl.BlockSpec((tt, nqk), lambda i: (i, 0)), pl.BlockSpec((tt, nqk), lambda i: (i, 0)),
                  pl.BlockSpec((tt, ch - 2 * nqk), lambda i: (i, 0))],
        out_specs=pl.BlockSpec((tt, ch), lambda i: (i, 0)),
        out_shape=jax.ShapeDtypeStruct((t, ch), F32),
        scratch_shapes=[pltpu.VMEM((HALO + tt, ch), F32)],
        compiler_params=_params(("arbitrary",)),
    )(proj, proj, conv_w, dq, dk, dv)


def _conv_bwd(proj, dc, conv_w, dproj, *, name):
    t = proj.shape[0]
    ch = conv_w.shape[1]
    tt = _pick(t, (256,))
    nt = t // tt
    cur = pl.BlockSpec((tt, ch), lambda i: (i, 0))
    nxt = pl.BlockSpec((HALO, ch), lambda i: (jnp.minimum((i + 1) * (tt // HALO), t // HALO - 1), 0))

    def body(x_ref, d_ref, dn_ref, cw_ref, alias_ref, dx_ref, dw_ref, ds):
        last = pl.program_id(0) == nt - 1
        ds[0:tt, :] = d_ref[...]
        ds[tt:tt + HALO, :] = jnp.where(last, 0.0, dn_ref[...])

        @pl.when(pl.program_id(0) == 0)
        def _():
            dw_ref[...] = jnp.zeros_like(dw_ref)

        for cb in range(ch // LANES):
            cols = slice(cb * LANES, (cb + 1) * LANES)
            x0 = x_ref[:, cols]
            acc = None
            for j in range(CONV_W):
                sh = CONV_W - 1 - j
                dj = ds[sh:sh + tt, cols]
                term = dj * cw_ref[j:j + 1, cols]
                acc = term if acc is None else acc + term
                dw_ref[j:j + 1, cols] += jnp.sum(dj * x0, axis=0, keepdims=True)
            dx_ref[:, cols] = acc.astype(BF16)

    return pl.pallas_call(
        body, name=name, grid=(nt,),
        in_specs=[cur, cur, nxt, pl.BlockSpec((CONV_W, ch), lambda i: (0, 0)), pl.BlockSpec(memory_space=pl.ANY)],
        out_specs=[pl.BlockSpec((tt, ch), lambda i: (i, 0)), pl.BlockSpec((HALO, ch), lambda i: (0, 0))],
        out_shape=[jax.ShapeDtypeStruct(dproj.shape, BF16), jax.ShapeDtypeStruct((HALO, ch), F32)],
        scratch_shapes=[pltpu.VMEM((tt + HALO, ch), F32)],
        input_output_aliases={4: 0},
        compiler_params=_params(("arbitrary",)),
    )(proj, dc, dc, conv_w, dproj)


def _bg_fwd(projs, alog_row, dtb_row, *, name):
    t = projs.shape[0]
    tt = _pick(t, (512, 256))

    def body(p_ref, al_ref, db_ref, o_ref):
        p = p_ref[...]
        lane = lax.broadcasted_iota(jnp.int32, p.shape, 1)
        z = p + db_ref[...]
        sp = jnp.maximum(z, 0.0) + jnp.log1p(jnp.exp(-jnp.abs(z)))
        g = -jnp.exp(al_ref[...]) * sp
        o_ref[...] = jnp.where(lane < HEADS, _sigmoid(p), jnp.where(lane < 2 * HEADS, g, 0.0))

    blk = pl.BlockSpec((tt, LANES), lambda i: (i, 0))
    row = pl.BlockSpec((1, LANES), lambda i: (0, 0))
    return pl.pallas_call(
        body, name=name, grid=(t // tt,), in_specs=[blk, row, row], out_specs=blk,
        out_shape=jax.ShapeDtypeStruct((t, LANES), F32), compiler_params=_params(("parallel",)),
    )(projs, alog_row, dtb_row)


def _bg_bwd(projs, dbg, alog_row, dtb_row, *, name):
    t = projs.shape[0]
    tt = _pick(t, (512, 256))

    def body(p_ref, d_ref, al_ref, db_ref, o_ref, dal_ref, ddb_ref):
        p, d = p_ref[...], d_ref[...]
        lane = lax.broadcasted_iota(jnp.int32, p.shape, 1)
        isg = (lane >= HEADS) & (lane < 2 * HEADS)
        be = _sigmoid(p)
        z = p + db_ref[...]
        sp = jnp.maximum(z, 0.0) + jnp.log1p(jnp.exp(-jnp.abs(z)))
        ea = jnp.exp(al_ref[...])
        d_aa = jnp.where(isg, d * (-ea) * _sigmoid(z), 0.0)
        o_ref[...] = jnp.where(lane < HEADS, d * be * (1.0 - be), d_aa).astype(BF16)

        @pl.when(pl.program_id(0) == 0)
        def _():
            dal_ref[...] = jnp.zeros_like(dal_ref)
            ddb_ref[...] = jnp.zeros_like(ddb_ref)

        dal_ref[...] += jnp.sum(jnp.where(isg, d * (-ea) * sp, 0.0), axis=0, keepdims=True)
        ddb_ref[...] += jnp.sum(d_aa, axis=0, keepdims=True)

    blk = pl.BlockSpec((tt, LANES), lambda i: (i, 0))
    row = pl.BlockSpec((1, LANES), lambda i: (0, 0))
    return pl.pallas_call(
        body, name=name, grid=(t // tt,), in_specs=[blk, blk, row, row], out_specs=[blk, row, row],
        out_shape=[jax.ShapeDtypeStruct((t, LANES), BF16), jax.ShapeDtypeStruct((1, LANES), F32),
                   jax.ShapeDtypeStruct((1, LANES), F32)],
        compiler_params=_params(("arbitrary",)),
    )(projs, dbg, alog_row, dtb_row)


def _hn_specs(n_chunks, rev=False):
    def nn(n):
        return n_chunks - 1 - n if rev else n
    tok = lambda w: pl.BlockSpec((CHUNK, HB * w), lambda g, n: (nn(n), g))
    per = lambda a, b: pl.BlockSpec((HB, None, a, b), lambda g, n: (g, nn(n), 0, 0))
    return tok, per


def _hcols(hh, w):
    return slice(hh * w, (hh + 1) * w)


def _decay(gc_col, gc_row):
    r, c = _iota2(CHUNK)
    d = jnp.exp(jnp.minimum(gc_col - gc_row, 0.0))
    return jnp.where(r >= c, d, 0.0), jnp.where(r > c, d, 0.0)


def _gdn_prep(q, k, v, beta_r, g_r, *, name):
    t = q.shape[0]
    n_chunks = t // CHUNK
    tok, per = _hn_specs(n_chunks)

    def body(q_ref, k_ref, v_ref, b_ref, g_ref, gc_ref, ti_ref, u_ref, w_ref, p_ref):
        r, c = _iota2(CHUNK)
        hs = range(HB)
        kk = [k_ref[:, _hcols(h, DK)] for h in hs]
        gc_col = [jnp.sum(jnp.where(c <= r, jnp.broadcast_to(g_ref[h], (CHUNK, CHUNK)), 0.0), axis=1, keepdims=True)
                  for h in hs]
        gc_row = [_col_to_row(gc_col[h]) for h in hs]
        beta_col = [_row_to_col(b_ref[h]) for h in hs]
        dec = [_decay(gc_col[h], gc_row[h]) for h in hs]
        kb = [kk[h] * beta_col[h] for h in hs]
        for h in hs:
            gc_ref[h] = gc_row[h]
            p_ref[h] = _dot_nt(q_ref[:, _hcols(h, DK)], kk[h]) * dec[h][0]
        pw = [_dot_nt(kb[h], kk[h]) * dec[h][1] for h in hs]
        y = [-pw[h] for h in hs]
        for _ in range(5):
            pw = [_dot(pw[h], pw[h]) for h in hs]
            yp = [_dot(y[h], pw[h]) for h in hs]
            y = [y[h] + pw[h] + yp[h] for h in hs]
        vb = [v_ref[:, _hcols(h, DV)] * beta_col[h] for h in hs]
        kbe = [kb[h] * jnp.exp(gc_col[h]) for h in hs]
        yv = [_dot(y[h], vb[h]) for h in hs]
        yk = [_dot(y[h], kbe[h]) for h in hs]
        for h in hs:
            ti_ref[h] = y[h]
            u_ref[:, _hcols(h, DV)] = vb[h] + yv[h]
            w_ref[:, _hcols(h, DK)] = kbe[h] + yk[h]

    return pl.pallas_call(
        body, name=name, grid=(HEADS // HB, n_chunks),
        in_specs=[tok(DK), tok(DK), tok(DV), per(1, CHUNK), per(1, CHUNK)],
        out_specs=[per(1, CHUNK), per(CHUNK, CHUNK), tok(DV), tok(DK), per(CHUNK, CHUNK)],
        out_shape=[jax.ShapeDtypeStruct((HEADS, n_chunks, 1, CHUNK), F32),
                   jax.ShapeDtypeStruct((HEADS, n_chunks, CHUNK, CHUNK), F32),
                   jax.ShapeDtypeStruct((t, HEADS * DV), F32), jax.ShapeDtypeStruct((t, HEADS * DK), F32),
                   jax.ShapeDtypeStruct((HEADS, n_chunks, CHUNK, CHUNK), F32)],
        compiler_params=_params(("parallel", "parallel")),
    )(q, k, v, beta_r, g_r)


def _gdn_scan(q, k, u, w, p, gc_r, *, name):
    t = q.shape[0]
    n_chunks = t // CHUNK
    tok, per = _hn_specs(n_chunks)

    def body(q_ref, k_ref, u_ref, w_ref, p_ref, gc_ref, o_ref, s_ref, st):
        @pl.when(pl.program_id(1) == 0)
        def _():
            st[...] = jnp.zeros_like(st)

        hs = range(HB)
        s = [st[h] for h in hs]
        gc_row = [gc_ref[h] for h in hs]
        gc_col = [_row_to_col(gc_row[h]) for h in hs]
        glast = [gc_row[h][:, CHUNK - 1:CHUNK] for h in hs]
        for h in hs:
            s_ref[h] = s[h]
        ws = [_dot(w_ref[:, _hcols(h, DK)], s[h]) for h in hs]
        qs = [_dot(q_ref[:, _hcols(h, DK)] * jnp.exp(gc_col[h]), s[h]) for h in hs]
        vn = [u_ref[:, _hcols(h, DV)] - ws[h] for h in hs]
        pv = [_dot(p_ref[h], vn[h]) for h in hs]
        kv = [_dot_tn(k_ref[:, _hcols(h, DK)] * jnp.exp(glast[h] - gc_col[h]), vn[h]) for h in hs]
        for h in hs:
            o_ref[:, _hcols(h, DV)] = qs[h] + pv[h]
            st[h] = s[h] * jnp.exp(glast[h]) + kv[h]

    return pl.pallas_call(
        body, name=name, grid=(HEADS // HB, n_chunks),
        in_specs=[tok(DK), tok(DK), tok(DV), tok(DK), per(CHUNK, CHUNK), per(1, CHUNK)],
        out_specs=[tok(DV), per(DK, DV)],
        out_shape=[jax.ShapeDtypeStruct((t, HEADS * DV), F32),
                   jax.ShapeDtypeStruct((HEADS, n_chunks, DK, DV), F32)],
        scratch_shapes=[pltpu.VMEM((HB, DK, DV), F32)],
        compiler_params=_params(("arbitrary", "arbitrary")),
    )(q, k, u, w, p, gc_r)


def _gdn_bwd(q, k, v, beta_r, gc_r, ti, u, w, s_all, do, *, name):
    t = q.shape[0]
    n_chunks = t // CHUNK
    tok, per = _hn_specs(n_chunks, rev=True)

    def body(q_ref, k_ref, v_ref, b_ref, gc_ref, ti_ref, u_ref, w_ref, s_ref, do_ref,
             dq_ref, dk_ref, dv_ref, db_ref, dg_ref, dst):
        @pl.when(pl.program_id(1) == 0)
        def _():
            dst[...] = jnp.zeros_like(dst)

        r, c = _iota2(CHUNK)
        rows = lax.broadcasted_iota(jnp.int32, (CHUNK, 1), 0)
        rsum = lambda a: jnp.sum(a, axis=1, keepdims=True)
        hs = range(HB)
        qq = [q_ref[:, _hcols(h, DK)] for h in hs]
        kk = [k_ref[:, _hcols(h, DK)] for h in hs]
        ww = [w_ref[:, _hcols(h, DK)] for h in hs]
        uu = [u_ref[:, _hcols(h, DV)] for h in hs]
        d_o = [do_ref[:, _hcols(h, DV)] for h in hs]
        s = [s_ref[h] for h in hs]
        d_s = [dst[h] for h in hs]
        gc_row = [gc_ref[h] for h in hs]
        gc_col = [_row_to_col(gc_row[h]) for h in hs]
        beta_col = [_row_to_col(b_ref[h]) for h in hs]
        eg = [jnp.exp(gc_col[h]) for h in hs]
        glast = [gc_row[h][:, CHUNK - 1:CHUNK] for h in hs]
        eglast = [jnp.exp(glast[h]) for h in hs]
        e2 = [jnp.exp(glast[h] - gc_col[h]) for h in hs]
        dec = [_decay(gc_col[h], gc_row[h]) for h in hs]
        kb = [kk[h] * beta_col[h] for h in hs]
        ke = [kk[h] * e2[h] for h in hs]
        qe = [qq[h] * eg[h] for h in hs]
        kkt = [_dot_nt(kb[h], kk[h]) for h in hs]
        qk = [_dot_nt(qq[h], kk[h]) for h in hs]
        ws = [_dot(ww[h], s[h]) for h in hs]
        a = [kkt[h] * dec[h][1] for h in hs]
        pp = [qk[h] * dec[h][0] for h in hs]
        vn = [uu[h] - ws[h] for h in hs]
        t1 = [_dot_tn(pp[h], d_o[h]) for h in hs]
        t2 = [_dot(ke[h], d_s[h]) for h in hs]
        dqe = [_dot_nt(d_o[h], s[h]) for h in hs]
        dke = [_dot_nt(vn[h], d_s[h]) for h in hs]
        dqk = [_dot_nt(d_o[h], vn[h]) * dec[h][0] for h in hs]
        dvn = [t1[h] + t2[h] for h in hs]
        t3 = [_dot_tn(qe[h], d_o[h]) for h in hs]
        t4 = [_dot_tn(ww[h], dvn[h]) for h in hs]
        dw = [-_dot_nt(dvn[h], s[h]) for h in hs]
        de2 = [rsum(dke[h] * ke[h]) for h in hs]
        dglast = [jnp.sum(de2[h], axis=0, keepdims=True)
                  + eglast[h] * jnp.sum(rsum(d_s[h] * s[h]), axis=0, keepdims=True) for h in hs]
        for h in hs:
            dst[h] = d_s[h] * eglast[h] + t3[h] - t4[h]
        yy = [ti_ref[h] for h in hs]
        t5 = [_dot_tn(yy[h], dvn[h]) for h in hs]
        t6 = [_dot_tn(yy[h], dw[h]) for h in hs]
        dvb = [dvn[h] + t5[h] for h in hs]
        dkbe = [dw[h] + t6[h] for h in hs]
        t7 = [_dot_nt(dvb[h], uu[h]) for h in hs]
        t8 = [_dot_nt(dkbe[h], ww[h]) for h in hs]
        d_a = [jnp.where(r > c, -(t7[h] + t8[h]), 0.0) for h in hs]
        dm = [d_a[h] * dec[h][1] for h in hs]
        t9 = [_dot(dm[h], kk[h]) for h in hs]
        t10 = [_dot(dqk[h], kk[h]) for h in hs]
        t11 = [_dot_tn(dqk[h], qq[h]) for h in hs]
        t12 = [_dot_tn(dm[h], kb[h]) for h in hs]
        for h in hs:
            dkb = t9[h] + dkbe[h] * eg[h]
            e_mat = d_a[h] * a[h] + dqk[h] * qk[h]
            dgc = (rsum(dqe[h] * qe[h]) - de2[h] + rsum(dkbe[h] * kb[h] * eg[h]) + rsum(e_mat)
                   - _row_to_col(jnp.sum(e_mat, axis=0, keepdims=True)))
            dgc = dgc + jnp.where(rows == CHUNK - 1, dglast[h], 0.0)
            dq_ref[:, _hcols(h, DK)] = dqe[h] * eg[h] + t10[h]
            dk_ref[:, _hcols(h, DK)] = t11[h] + dke[h] * e2[h] + t12[h] + dkb * beta_col[h]
            dv_ref[:, _hcols(h, DV)] = dvb[h] * beta_col[h]
            dbeta = rsum(dkb * kk[h]) + rsum(dvb[h] * v_ref[:, _hcols(h, DV)])
            db_ref[h] = _col_to_row(dbeta)
            dg_ref[h] = jnp.sum(jnp.where(r >= c, jnp.broadcast_to(dgc, (CHUNK, CHUNK)), 0.0), axis=0, keepdims=True)

    return pl.pallas_call(
        body, name=name, grid=(HEADS // HB, n_chunks),
        in_specs=[tok(DK), tok(DK), tok(DV), per(1, CHUNK), per(1, CHUNK), per(CHUNK, CHUNK), tok(DV), tok(DK),
                  per(DK, DV), tok(DV)],
        out_specs=[tok(DK), tok(DK), tok(DV), per(1, CHUNK), per(1, CHUNK)],
        out_shape=[jax.ShapeDtypeStruct((t, HEADS * DK), F32), jax.ShapeDtypeStruct((t, HEADS * DK), F32),
                   jax.ShapeDtypeStruct((t, HEADS * DV), F32),
                   jax.ShapeDtypeStruct((HEADS, n_chunks, 1, CHUNK), F32),
                   jax.ShapeDtypeStruct((HEADS, n_chunks, 1, CHUNK), F32)],
        scratch_shapes=[pltpu.VMEM((HB, DK, DV), F32)],
        compiler_params=_params(("arbitrary", "arbitrary")),
    )(q, k, v, beta_r, gc_r, ti, u, w, s_all, do)


def _rot(x, cos2, sin2):
    return x * cos2 + pltpu.roll(x, DK // 2, 1) * sin2


def _unrot(d, cos2, sin2):
    return d * cos2 + pltpu.roll(d * sin2, DK // 2, 1)


def _ret_consts(lg):
    r, c = _iota2(CHUNK)
    dm = jnp.where(r >= c, jnp.exp((r - c).astype(F32) * lg), 0.0)
    pos = lax.broadcasted_iota(jnp.int32, (CHUNK, 1), 0).astype(F32)
    return dm, jnp.exp((pos + 1.0) * lg), jnp.exp((CHUNK - 1.0 - pos) * lg), jnp.exp(CHUNK * lg)


def _ret_fwd(proj, cos2, sin2, lg_tab, col_q, col_k, col_v, *, name):
    t = proj.shape[0]
    n_chunks = t // CHUNK
    tok, per = _hn_specs(n_chunks)
    bq, bk, bv = col_q // (HB * DK), col_k // (HB * DK), col_v // (HB * DV)

    def body(q_ref, k_ref, v_ref, cos_ref, sin_ref, lg_ref, o_ref, s_ref, st):
        @pl.when(pl.program_id(1) == 0)
        def _():
            st[...] = jnp.zeros_like(st)

        cos2v, sin2v = cos_ref[...], sin_ref[...]
        hs = range(HB)
        s = [st[h] for h in hs]
        for h in hs:
            s_ref[h] = s[h]
        cst = [_ret_consts(lg_ref[h][:, 0:1]) for h in hs]
        qq = [_rot(q_ref[:, _hcols(h, DK)], cos2v, sin2v) for h in hs]
        kk = [_rot(k_ref[:, _hcols(h, DK)], cos2v, sin2v) * (DK ** -0.5) for h in hs]
        vv = [v_ref[:, _hcols(h, DV)] for h in hs]
        sc = [_dot_nt(qq[h], kk[h]) * cst[h][0] for h in hs]
        qs = [_dot(qq[h], s[h]) for h in hs]
        kv = [_dot_tn(kk[h] * cst[h][2], vv[h]) for h in hs]
        sv = [_dot(sc[h], vv[h]) for h in hs]
        for h in hs:
            o_ref[:, _hcols(h, DV)] = sv[h] + qs[h] * cst[h][1]
            st[h] = s[h] * cst[h][3] + kv[h]

    return pl.pallas_call(
        body, name=name, grid=(HEADS // HB, n_chunks),
        in_specs=[pl.BlockSpec((CHUNK, HB * DK), lambda g, n: (n, bq + g)),
                  pl.BlockSpec((CHUNK, HB * DK), lambda g, n: (n, bk + g)),
                  pl.BlockSpec((CHUNK, HB * DV), lambda g, n: (n, bv + g)),
                  pl.BlockSpec((CHUNK, DK), lambda g, n: (n, 0)), pl.BlockSpec((CHUNK, DK), lambda g, n: (n, 0)),
                  pl.BlockSpec((HB, 1, LANES), lambda g, n: (g, 0, 0))],
        out_specs=[tok(DV), per(DK, DV)],
        out_shape=[jax.ShapeDtypeStruct((t, HEADS * DV), F32),
                   jax.ShapeDtypeStruct((HEADS, n_chunks, DK, DV), F32)],
        scratch_shapes=[pltpu.VMEM((HB, DK, DV), F32)],
        compiler_params=_params(("arbitrary", "arbitrary")),
    )(proj, proj, proj, cos2, sin2, lg_tab)


def _ret_bwd(proj, cos2, sin2, lg_tab, s_all, do, dproj, col_q, col_k, col_v, *, name):
    t = proj.shape[0]
    n_chunks = t // CHUNK
    tok, per = _hn_specs(n_chunks, rev=True)
    bq, bk, bv = col_q // (HB * DK), col_k // (HB * DK), col_v // (HB * DV)
    rv = lambda n: n_chunks - 1 - n
    nqk, wid = HEADS * DK, HEADS * (2 * DK + DV)
    assert HB == HEADS and col_k == col_q + nqk and col_v == col_k + nqk and col_q % wid == 0

    def body(q_ref, k_ref, v_ref, cos_ref, sin_ref, lg_ref, s_ref, do_ref, alias_ref, d_ref, dst):
        @pl.when(pl.program_id(1) == 0)
        def _():
            dst[...] = jnp.zeros_like(dst)

        cos2v, sin2v = cos_ref[...], sin_ref[...]
        hs = range(HB)
        s = [s_ref[h] for h in hs]
        d_s = [dst[h] for h in hs]
        d_o = [do_ref[:, _hcols(h, DV)] for h in hs]
        cst = [_ret_consts(lg_ref[h][:, 0:1]) for h in hs]
        qq = [_rot(q_ref[:, _hcols(h, DK)], cos2v, sin2v) for h in hs]
        kk = [_rot(k_ref[:, _hcols(h, DK)], cos2v, sin2v) * (DK ** -0.5) for h in hs]
        vv = [v_ref[:, _hcols(h, DV)] for h in hs]
        dxo = [d_o[h] * cst[h][1] for h in hs]
        sc = [_dot_nt(qq[h], kk[h]) * cst[h][0] for h in hs]
        dsc = [_dot_nt(d_o[h], vv[h]) * cst[h][0] for h in hs]
        t1 = [_dot(kk[h] * cst[h][2], d_s[h]) for h in hs]
        t2 = [_dot_nt(dxo[h], s[h]) for h in hs]
        t3 = [_dot_nt(vv[h], d_s[h]) for h in hs]
        t4 = [_dot_tn(qq[h], dxo[h]) for h in hs]
        t5 = [_dot_tn(sc[h], d_o[h]) for h in hs]
        t6 = [_dot(dsc[h], kk[h]) for h in hs]
        t7 = [_dot_tn(dsc[h], qq[h]) for h in hs]
        for h in hs:
            dst[h] = d_s[h] * cst[h][3] + t4[h]
            d_ref[:, 2 * nqk + h * DV:2 * nqk + (h + 1) * DV] = (t5[h] + t1[h]).astype(BF16)
            d_ref[:, _hcols(h, DK)] = _unrot(t6[h] + t2[h], cos2v, sin2v).astype(BF16)
            d_ref[:, nqk + h * DK:nqk + (h + 1) * DK] = (
                _unrot(t7[h] + t3[h] * cst[h][2], cos2v, sin2v) * (DK ** -0.5)).astype(BF16)

    return pl.pallas_call(
        body, name=name, grid=(HEADS // HB, n_chunks),
        in_specs=[pl.BlockSpec((CHUNK, HB * DK), lambda g, n: (rv(n), bq + g)),
                  pl.BlockSpec((CHUNK, HB * DK), lambda g, n: (rv(n), bk + g)),
                  pl.BlockSpec((CHUNK, HB * DV), lambda g, n: (rv(n), bv + g)),
                  pl.BlockSpec((CHUNK, DK), lambda g, n: (rv(n), 0)),
                  pl.BlockSpec((CHUNK, DK), lambda g, n: (rv(n), 0)),
                  pl.BlockSpec((HB, 1, LANES), lambda g, n: (g, 0, 0)),
                  per(DK, DV), tok(DV), pl.BlockSpec(memory_space=pl.ANY)],
        out_specs=pl.BlockSpec((CHUNK, wid), lambda g, n: (rv(n), col_q // wid)),
        out_shape=jax.ShapeDtypeStruct(dproj.shape, BF16),
        scratch_shapes=[pltpu.VMEM((HB, DK, DV), F32)],
        input_output_aliases={8: 0},
        compiler_params=_params(("arbitrary", "arbitrary")),
    )(proj, proj, proj, cos2, sin2, lg_tab, s_all, do, dproj)


def _merge_parts(oa, ob, z, rg, ga, gb):
    ra = lax.rsqrt(jnp.mean(oa * oa, axis=-1, keepdims=True) + EPS)
    xa = oa * ra
    mu = jnp.mean(ob, axis=-1, keepdims=True)
    cen = ob - mu
    rb = lax.rsqrt(jnp.mean(cen * cen, axis=-1, keepdims=True) + EPS)
    xb = cen * rb
    sz, sr = _sigmoid(z), _sigmoid(rg)
    return ra, xa, rb, xb, sz, sr, _sigmoid(ga), _sigmoid(gb)


def _merge_specs(tt, d, cz, crg, cga, cgb):
    blk = pl.BlockSpec((tt, d), lambda i: (i, 0))
    pcol = lambda c: pl.BlockSpec((tt, d), lambda i: (i, c // d))
    row = pl.BlockSpec((1, d), lambda i: (0, 0))
    return blk, [blk, blk, pcol(cz), pcol(crg), pcol(cga), pcol(cgb), row, row], row


def _merge_fwd(oa, ob, proj, wa, wb, cz, crg, cga, cgb, *, name):
    t, d = oa.shape
    tt = _pick(t, (256,))
    blk, in_specs, _ = _merge_specs(tt, d, cz, crg, cga, cgb)

    def body(oa_ref, ob_ref, z_ref, rg_ref, ga_ref, gb_ref, wa_ref, wb_ref, o_ref):
        for h in range(HEADS):
            cols = slice(h * DV, (h + 1) * DV)
            z, rg = z_ref[:, cols], rg_ref[:, cols]
            _, xa, _, xb, sz, sr, sga, sgb = _merge_parts(
                oa_ref[:, cols], ob_ref[:, cols], z, rg, ga_ref[:, cols], gb_ref[:, cols])
            o_a = xa * wa_ref[:, cols] * (z * sz)
            o_b = xb * wb_ref[:, cols] * (rg * sr)
            o_ref[:, cols] = (sga * o_a + sgb * o_b).astype(BF16)

    return pl.pallas_call(
        body, name=name, grid=(t // tt,), in_specs=in_specs, out_specs=blk,
        out_shape=jax.ShapeDtypeStruct((t, d), BF16), compiler_params=_params(("parallel",)),
    )(oa, ob, proj, proj, proj, proj, wa, wb)


def _merge_bwd(dmix, oa, ob, proj, wa, wb, cz, crg, cga, cgb, *, name):
    t, d = oa.shape
    tt = _pick(t, (256,))
    blk, in_specs, row = _merge_specs(tt, d, cz, crg, cga, cgb)
    assert (crg, cga, cgb) == (cz + d, cz + 2 * d, cz + 3 * d) and cz % (4 * d) == 0

    def body(dm_ref, oa_ref, ob_ref, z_ref, rg_ref, ga_ref, gb_ref, wa_ref, wb_ref,
             doa_ref, dob_ref, dp_ref, dwa_ref, dwb_ref):
        dz_ref, drg_ref, dga_ref, dgb_ref = [dp_ref.at[:, i * d:(i + 1) * d] for i in range(4)]

        @pl.when(pl.program_id(0) == 0)
        def _():
            dwa_ref[...] = jnp.zeros_like(dwa_ref)
            dwb_ref[...] = jnp.zeros_like(dwb_ref)

        for h in range(HEADS):
            cols = slice(h * DV, (h + 1) * DV)
            z, rg, wa_h, wb_h, dmx = z_ref[:, cols], rg_ref[:, cols], wa_ref[:, cols], wb_ref[:, cols], dm_ref[:, cols]
            ra, xa, rb, xb, sz, sr, sga, sgb = _merge_parts(
                oa_ref[:, cols], ob_ref[:, cols], z, rg, ga_ref[:, cols], gb_ref[:, cols])
            na, nb = xa * wa_h, xb * wb_h
            sil_z, sil_r = z * sz, rg * sr
            o_a, o_b = na * sil_z, nb * sil_r
            dga_ref[:, cols] = (dmx * o_a * sga * (1.0 - sga)).astype(BF16)
            dgb_ref[:, cols] = (dmx * o_b * sgb * (1.0 - sgb)).astype(BF16)
            d_oa, d_ob = dmx * sga, dmx * sgb
            dz_ref[:, cols] = (d_oa * na * sz * (1.0 + z * (1.0 - sz))).astype(BF16)
            drg_ref[:, cols] = (d_ob * nb * sr * (1.0 + rg * (1.0 - sr))).astype(BF16)
            dna, dnb = d_oa * sil_z, d_ob * sil_r
            dwa_ref[:, cols] += jnp.sum(dna * xa, axis=0, keepdims=True)
            dwb_ref[:, cols] += jnp.sum(dnb * xb, axis=0, keepdims=True)
            gwa, gwb = dna * wa_h, dnb * wb_h
            doa_ref[:, cols] = ra * (gwa - xa * jnp.mean(gwa * xa, axis=-1, keepdims=True))
            dob_ref[:, cols] = rb * (gwb - jnp.mean(gwb, axis=-1, keepdims=True)
                                     - xb * jnp.mean(gwb * xb, axis=-1, keepdims=True))

    return pl.pallas_call(
        body, name=name, grid=(t // tt,), in_specs=[blk] + in_specs,
        out_specs=[blk, blk, pl.BlockSpec((tt, 4 * d), lambda i: (i, cz // (4 * d))), row, row],
        out_shape=[jax.ShapeDtypeStruct((t, d), F32)] * 2 + [jax.ShapeDtypeStruct(proj.shape, BF16)]
        + [jax.ShapeDtypeStruct((1, d), F32)] * 2,
        compiler_params=_params(("arbitrary",)),
    )(dmix, oa, ob, proj, proj, proj, proj, wa, wb)


def _row_block(rows, cols, itemsize=4, target=1 << 20):
    for rb in (512, 256, 128, 64, 32, 16, 8):
        if rows % rb == 0 and rb * cols * itemsize <= target:
            return rb
    return rows


def _adamw(w, g, m, v, *, name):
    rows, cols = w.shape
    rb = _row_block(rows, cols)

    def body(w_ref, g_ref, m_ref, v_ref, d_ref, nm_ref, nv_ref):
        gg = g_ref[...]
        mm = ADAM_B1 * m_ref[...] + (1.0 - ADAM_B1) * gg
        vv = ADAM_B2 * v_ref[...] + (1.0 - ADAM_B2) * (gg * gg)
        m_hat = mm / (1.0 - ADAM_B1 ** ADAM_STEP)
        v_hat = vv / (1.0 - ADAM_B2 ** ADAM_STEP)
        d_ref[...] = -ADAM_LR * (m_hat / (jnp.sqrt(v_hat) + ADAM_EPS) + ADAM_WD * w_ref[...])
        nm_ref[...] = mm
        nv_ref[...] = vv

    blk = pl.BlockSpec((rb, cols), lambda i: (i, 0))
    return pl.pallas_call(
        body, name=name, grid=(rows // rb,), in_specs=[blk] * 4, out_specs=[blk] * 3,
        out_shape=[jax.ShapeDtypeStruct((rows, cols), F32)] * 3, compiler_params=_params(("parallel",)),
    )(w, g, m, v)


def _adamw_halves(w, mine, other, c_idx, m, v, *, name):
    rows, cols = w.shape
    hr = rows // 2
    rb = _row_block(hr, cols)
    nb = hr // rb

    def body(c_ref, w_ref, a_ref, b_ref, m_ref, v_ref, g_ref, d_ref, nm_ref, nv_ref):
        gg = jnp.where(pl.program_id(0) // nb == c_ref[0], a_ref[...], b_ref[...])
        mm = ADAM_B1 * m_ref[...] + (1.0 - ADAM_B1) * gg
        vv = ADAM_B2 * v_ref[...] + (1.0 - ADAM_B2) * (gg * gg)
        m_hat = mm / (1.0 - ADAM_B1 ** ADAM_STEP)
        v_hat = vv / (1.0 - ADAM_B2 ** ADAM_STEP)
        g_ref[...] = gg
        d_ref[...] = -ADAM_LR * (m_hat / (jnp.sqrt(v_hat) + ADAM_EPS) + ADAM_WD * w_ref[...])
        nm_ref[...] = mm
        nv_ref[...] = vv

    blk = pl.BlockSpec((rb, cols), lambda i, c: (i, 0))
    half = pl.BlockSpec((rb, cols), lambda i, c: (i % nb, 0))
    return pl.pallas_call(
        body, name=name,
        grid_spec=pltpu.PrefetchScalarGridSpec(
            num_scalar_prefetch=1, grid=(rows // rb,), in_specs=[blk, half, half, blk, blk], out_specs=[blk] * 4),
        out_shape=[jax.ShapeDtypeStruct((rows, cols), F32)] * 4, compiler_params=_params(("parallel",)),
    )(c_idx, w, mine, other, m, v)


def _pair_add(g, rsib, c_idx, *, name):
    _, _, hr, cols = g.shape
    rb = _row_block(hr, cols, 2)

    def body(c_ref, g_ref, r_ref, o_ref):
        o_ref[...] = (g_ref[...].astype(F32) + r_ref[...].astype(F32)).astype(BF16)

    return pl.pallas_call(
        body, name=name,
        grid_spec=pltpu.PrefetchScalarGridSpec(
            num_scalar_prefetch=1, grid=(N_CHIPS, hr // rb),
            in_specs=[pl.BlockSpec((None, None, rb, cols), lambda j, i, c: (j, c[0], i, 0)),
                      pl.BlockSpec((None, rb, cols), lambda j, i, c: (j, i, 0))],
            out_specs=pl.BlockSpec((None, rb, cols), lambda j, i, c: (j, i, 0))),
        out_shape=jax.ShapeDtypeStruct((N_CHIPS, hr, cols), BF16),
        compiler_params=_params(("parallel", "parallel")),
    )(c_idx, g, rsib)


def _chip_sum(qb, *, name):
    _, hr, cols = qb.shape
    rb = _row_block(hr, cols, 2, 1 << 19)

    def body(q_ref, o_ref):
        acc = q_ref[0].astype(F32)
        for i in range(1, N_CHIPS):
            acc = acc + q_ref[i].astype(F32)
        o_ref[...] = acc

    return pl.pallas_call(
        body, name=name, grid=(hr // rb,),
        in_specs=[pl.BlockSpec((N_CHIPS, rb, cols), lambda i: (0, i, 0))],
        out_specs=pl.BlockSpec((rb, cols), lambda i: (i, 0)),
        out_shape=jax.ShapeDtypeStruct((hr, cols), F32), compiler_params=_params(("parallel",)),
    )(qb)


def _place():
    x, y, c = lax.axis_index("x"), lax.axis_index("y"), lax.axis_index("c")
    return x, y, c, [(1 - x, y), (x, 1 - y), (1 - x, 1 - y)]


ANY = pl.BlockSpec(memory_space=pl.ANY)


def _gather_side(shards):
    nw = len(shards)

    def plan(ins, outs, ssem, rsem):
        x, y, c, chips = _place()
        s_me = 2 * x + y

        def rows(w, half):
            hr = shards[w].shape[0] // 2
            return pl.ds(pl.multiple_of(half * hr, 16), hr)

        def rcopy(w, k, src, slot, half, to):
            return pltpu.make_async_remote_copy(
                src_ref=src, dst_ref=outs[w].at[slot, rows(w, half)],
                send_sem=ssem.at[w * 6 + k], recv_sem=rsem.at[w * 6 + k], device_id=to, device_id_type=MESH)

        out = []
        for w in range(nw):
            mine = ins[w].at[rows(w, c)]
            for j, chip in enumerate(chips):
                s_j = 2 * chip[0] + chip[1]
                out.append((functools.partial(rcopy, w, j, mine, s_me, c, (chip[0], chip[1], c)),
                            functools.partial(rcopy, w, j, mine, s_j, c, (x, y, c)),
                            functools.partial(rcopy, w, 3 + j, outs[w].at[s_j, rows(w, c)], s_j, c, (x, y, 1 - c)),
                            functools.partial(rcopy, w, 3 + j, mine, s_j, 1 - c, (x, y, c))))
        return out

    def start(*refs):
        for send, _, _, _ in plan(*refs):
            send().start()

    def middle(*refs):
        for _, landed, forward, _ in plan(*refs):
            landed().wait_recv()
            forward().start()

    def finish(*refs):
        for send, _, forward, from_sibling in plan(*refs):
            from_sibling().wait_recv()
            send().wait_send()
            forward().wait_send()

    return _Side(shards, [jax.ShapeDtypeStruct((N_CHIPS,) + s.shape, s.dtype) for s in shards], nw * 6,
                 [start, middle, finish])


def _run_side(side, *, name):
    n_in, n_out = len(side.ins), len(side.out_shapes)

    def body(*refs):
        ins, outs = refs[:n_in], refs[n_in:n_in + n_out]
        for phase in side.phases:
            phase(ins, outs, refs[-2], refs[-1])

    return pl.pallas_call(
        body, name=name, in_specs=[ANY] * n_in, out_specs=[ANY] * n_out, out_shape=side.out_shapes,
        scratch_shapes=side.sems(), compiler_params=pltpu.CompilerParams(has_side_effects=True),
    )(*side.ins)


def _pair_exchange(gs, *, name):
    nw = len(gs)

    def body(*refs):
        ins, outs = refs[:nw], refs[nw:2 * nw]
        ssem, rsem = refs[2 * nw:]
        x, y, c, _ = _place()
        cps = []
        for w in range(nw):
            cp = pltpu.make_async_remote_copy(
                src_ref=ins[w].at[:, 1 - c], dst_ref=outs[w], send_sem=ssem.at[w], recv_sem=rsem.at[w],
                device_id=(x, y, 1 - c), device_id_type=MESH)
            cp.start()
            cps.append(cp)
        for cp in cps:
            cp.wait()

    return pl.pallas_call(
        body, name=name, in_specs=[ANY] * nw, out_specs=[ANY] * nw,
        out_shape=[jax.ShapeDtypeStruct((g.shape[0],) + g.shape[2:], g.dtype) for g in gs],
        scratch_shapes=[pltpu.SemaphoreType.DMA((nw,)), pltpu.SemaphoreType.DMA((nw,))],
        compiler_params=pltpu.CompilerParams(has_side_effects=True),
    )(*gs)


def _scatter_side(ps):
    nw = len(ps)

    def copies(ins, outs, ssem, rsem):
        x, y, c, chips = _place()
        s_me = 2 * x + y
        return [pltpu.make_async_remote_copy(
            src_ref=ins[w].at[2 * chip[0] + chip[1]], dst_ref=outs[w].at[s_me],
            send_sem=ssem.at[w * 3 + j], recv_sem=rsem.at[w * 3 + j],
            device_id=(chip[0], chip[1], c), device_id_type=MESH)
            for w in range(nw) for j, chip in enumerate(chips)]

    def start(*refs):
        for cp in copies(*refs):
            cp.start()

    def finish(*refs):
        for cp in copies(*refs):
            cp.wait()

    return _Side(ps, [jax.ShapeDtypeStruct(p.shape, p.dtype) for p in ps], nw * 3, [start, finish])


def _pair_share(hs, *, name):
    nw = len(hs)

    def body(*refs):
        ins, outs = refs[:nw], refs[nw:2 * nw]
        ssem, rsem = refs[2 * nw:]
        x, y, c, _ = _place()
        cps = []
        for w in range(nw):
            cp = pltpu.make_async_remote_copy(
                src_ref=ins[w], dst_ref=outs[w], send_sem=ssem.at[w], recv_sem=rsem.at[w],
                device_id=(x, y, 1 - c), device_id_type=MESH)
            cp.start()
            cps.append(cp)
        for cp in cps:
            cp.wait()

    return pl.pallas_call(
        body, name=name, in_specs=[ANY] * nw, out_specs=[ANY] * nw,
        out_shape=[jax.ShapeDtypeStruct(h.shape, h.dtype) for h in hs],
        scratch_shapes=[pltpu.SemaphoreType.DMA((nw,)), pltpu.SemaphoreType.DMA((nw,))],
        compiler_params=pltpu.CompilerParams(has_side_effects=True),
    )(*hs)


def _gather_small(a, *, name):
    def body(a_ref, o_ref, ssem, rsem):
        x, y, c, chips = _place()
        s_me = 2 * x + y
        o_ref[s_me] = a_ref[...]
        cps = []
        for j, chip in enumerate(chips):
            cp = pltpu.make_async_remote_copy(
                src_ref=a_ref, dst_ref=o_ref.at[s_me], send_sem=ssem.at[j], recv_sem=rsem.at[j],
                device_id=(chip[0], chip[1], c), device_id_type=MESH)
            cp.start()
            cps.append(cp)
        for cp in cps:
            cp.wait()

    vm = pl.BlockSpec(memory_space=pltpu.VMEM)
    return pl.pallas_call(
        body, name=name, in_specs=[vm], out_specs=vm,
        out_shape=jax.ShapeDtypeStruct((N_CHIPS,) + a.shape, a.dtype),
        scratch_shapes=[pltpu.SemaphoreType.DMA((3,)), pltpu.SemaphoreType.DMA((3,))],
    )(a)


def _allreduce_small(p, *, name):
    def body(p_ref, o_ref, buf, ssem, rsem):
        x, y, c, _ = _place()
        me = 4 * x + 2 * y + c
        buf[me] = p_ref[...]
        cps = []
        for k in range(1, N_DEV):
            fx, fy, fc = (k >> 2) & 1, (k >> 1) & 1, k & 1
            peer = (x + fx - 2 * x * fx, y + fy - 2 * y * fy, c + fc - 2 * c * fc)
            cp = pltpu.make_async_remote_copy(
                src_ref=buf.at[me], dst_ref=buf.at[me], send_sem=ssem.at[k - 1], recv_sem=rsem.at[k - 1],
                device_id=peer, device_id_type=MESH)
            cp.start()
            cps.append(cp)
        for cp in cps:
            cp.wait()
        acc = buf[0]
        for i in range(1, N_DEV):
            acc = acc + buf[i]
        o_ref[...] = acc

    vm = pl.BlockSpec(memory_space=pltpu.VMEM)
    return pl.pallas_call(
        body, name=name, in_specs=[vm], out_specs=vm,
        out_shape=jax.ShapeDtypeStruct(p.shape, p.dtype),
        scratch_shapes=[pltpu.VMEM((N_DEV,) + p.shape, p.dtype), pltpu.SemaphoreType.DMA((N_DEV - 1,)),
                        pltpu.SemaphoreType.DMA((N_DEV - 1,))],
    )(p)


def _rows_to_tokens(r):
    h, n = r.shape[0], r.shape[1]
    return r.reshape(h, n * CHUNK).T


def _tokens_to_rows(a):
    t, h = a.shape
    return a.T.reshape(h, t // CHUNK, 1, CHUNK)


def kernel(x, norm1_w, w_in, conv_w, a_log, dt_bias, gdn_norm_w, ret_norm_w, w_out, norm2_w, w_gate, w_up, w_down, norm_f_w, loss_target, m_norm1_w, m_w_in, m_conv_w, m_a_log, m_dt_bias, m_gdn_norm_w, m_ret_norm_w, m_w_out, m_norm2_w, m_w_gate, m_w_up, m_w_down, m_norm_f_w, v_norm1_w, v_w_in, v_conv_w, v_a_log, v_dt_bias, v_gdn_norm_w, v_ret_norm_w, v_w_out, v_norm2_w, v_w_gate, v_w_up, v_w_down, v_norm_f_w):
    t, d = x.shape[1], x.shape[2]
    f = w_gate.shape[2] * N_CHIPS
    nqk, nv = HEADS * DK, HEADS * DV
    ncs = w_in.shape[2]
    c_idx = lax.axis_index("c")
    s_idx = 2 * lax.axis_index("x") + lax.axis_index("y")
    xs = x[0]
    tgt = loss_target[0]

    n_small = 2 * HEADS
    widths = [2 * nqk + nv, nv, n_small, nqk, nqk, nv, nv, d, d]
    g_off = np.concatenate([[0], np.cumsum(widths)])
    order = [0, 3, 4, 5, 1, 6, 7, 8]
    m_off = np.concatenate([[0], np.cumsum([widths[i] for i in order])])
    o_rq, o_rk, o_rv, o_az, o_rg, o_ga, o_gb = [int(m_off[i]) for i in range(1, 8)]
    segs = [(int(g_off[i]), int(g_off[i + 1]), int(m_off[order.index(i)]) if i != 2 else None) for i in range(9)]

    def shard_pieces(s):
        out = []
        for a, b, dst in segs:
            lo, hi = max(a, s * ncs), min(b, (s + 1) * ncs)
            if lo < hi:
                out.append((lo - s * ncs, hi - s * ncs, None if dst is None else dst + lo - a))
        return out

    own = [w_in[0].astype(BF16), w_out[0].astype(BF16), w_gate[0].astype(BF16), w_up[0].astype(BF16),
           w_down[0].astype(BF16)]
    put_own = lambda full, mine: lax.dynamic_update_slice(full, mine[None], (s_idx, 0, 0))
    wg_in = put_own(_run_side(_gather_side(own[:1]), name="gather_w_in")[0], own[0])
    conv_full = _gather_small(conv_w[0], name="gather_conv_w")
    conv_full = jnp.concatenate([conv_full[i] for i in range(N_CHIPS)], axis=1)
    cuts = [(dst, s, a, b) for s in range(N_CHIPS) for a, b, dst in shard_pieces(s)]
    w_main = jnp.concatenate([wg_in[s][:, a:b] for dst, s, a, b in sorted(c for c in cuts if c[0] is not None)],
                             axis=1)
    w_small = jnp.concatenate([wg_in[s][:, a:b] for dst, s, a, b in cuts if dst is None], axis=1)
    w_small = jnp.pad(w_small, ((0, 0), (0, LANES - n_small)))

    pad16 = lambda a: jnp.pad(a, ((0, 0), (HEADS, LANES - 2 * HEADS)))
    alog_row, dtb_row = pad16(a_log), pad16(dt_bias)
    wa_row = jnp.tile(gdn_norm_w, (1, HEADS))
    inv = ROPE_BASE ** (-jnp.arange(0, DK, 2, dtype=F32) / DK)
    ang = jnp.arange(t, dtype=F32)[:, None] * inv[None, :]
    cos2 = jnp.concatenate([jnp.cos(ang), jnp.cos(ang)], axis=1)
    sin2 = jnp.concatenate([-jnp.sin(ang), jnp.sin(ang)], axis=1)
    lg = jnp.log1p(-jnp.exp2(-5.0 - jnp.arange(HEADS, dtype=F32)))
    lg_tab = jnp.broadcast_to(lg[:, None, None], (HEADS, 1, LANES))

    fq = f // N_CHIPS
    u1, projs = _rms_fwd(xs, norm1_w, w_narrow=w_small, name="rms1_fwd")
    proj, *rest = _mm(u1, w_main, tm=512, tn=2048, tk=d, side=_gather_side(own[1:]), name="mm_proj")
    wg_out, wg_gate, wg_up, wg_down = [put_own(g, o) for g, o in zip(rest, own[1:])]
    w_o = wg_out.reshape(d, d)
    w_g = jnp.concatenate([wg_gate[i] for i in range(N_CHIPS)], axis=1)
    w_u = jnp.concatenate([wg_up[i] for i in range(N_CHIPS)], axis=1)
    w_d = wg_down.reshape(f, d)
    q_a, k_a, v_a = _conv_fwd(proj, conv_full, name="conv_fwd")
    bg = _bg_fwd(projs, alog_row, dtb_row, name="bg_fwd")
    beta_r = _tokens_to_rows(bg[:, :HEADS])
    g_r = _tokens_to_rows(bg[:, HEADS:2 * HEADS])
    gc_r, tinv, u_a, w_a, p_a = _gdn_prep(q_a, k_a, v_a, beta_r, g_r, name="gdn_prep")
    o_a, s_a = _gdn_scan(q_a, k_a, u_a, w_a, p_a, gc_r, name="gdn_scan")
    o_b, s_b = _ret_fwd(proj, cos2, sin2, lg_tab, o_rq, o_rk, o_rv, name="ret_fwd")
    mixed = _merge_fwd(o_a, o_b, proj, wa_row, ret_norm_w, o_az, o_rg, o_ga, o_gb, name="merge_fwd")
    h1 = _mm(mixed, w_o, res=xs, tm=512, tn=d, tk=d, name="mm_out")
    hn = _rms_fwd(h1, norm2_w, name="rms2_fwd")
    gt, up, act = _ffn_in(hn, w_g, w_u, tm=256, tn=fq, name="ffn_in")
    h2 = _mm(act, w_d, res=h1, tm=512, tn=d, tk=fq, name="mm_down")
    loss_row, dh2, dh2b, d_nf = _loss_head(h2, tgt, norm_f_w.reshape(1, d), name="loss_head")

    g_down = _mm(act, dh2b, ta=True, out_dtype=BF16, tm=fq, tn=d, tk=1024, name="mm_dw_down")
    dgt, dup = _ffn_back(dh2b, w_d, gt, up, tm=512, tn=fq, name="ffn_back")
    dhn = _mm(dgt, w_g, tb=True, pair=(dup, w_u), tm=512, tn=d, tk=fq, name="mm_dhn")
    g_gate = _mm(hn, dgt, ta=True, out_dtype=BF16, tm=1024, tn=fq, tk=2048, name="mm_dw_gate")
    g_up = _mm(hn, dup, ta=True, out_dtype=BF16, tm=1024, tn=fq, tk=2048, name="mm_dw_up")
    dh1, dh1b, d_n2 = _rms_bwd(dhn, h1, norm2_w, dh2, name="rms2_bwd")
    dmix = _mm(dh1b, w_o, tb=True, tm=512, tn=d, tk=d, name="mm_dmix")
    g_out = _mm(mixed, dh1b, ta=True, out_dtype=BF16, tm=1024, tn=d, tk=1024, name="mm_dw_out")
    col_split = lambda g, wd: jnp.stack([g[:, i * wd:(i + 1) * wd] for i in range(N_CHIPS)])
    halves = lambda g: g.reshape(N_CHIPS, 2, g.shape[1] // 2, g.shape[2])
    c_arr = jnp.reshape(c_idx, (1,)).astype(jnp.int32)
    gs_ffn = [halves(g_out.reshape(N_CHIPS, d // N_CHIPS, d)), halves(col_split(g_gate, fq)),
              halves(col_split(g_up, fq)), halves(g_down.reshape(N_CHIPS, fq, d))]
    rsib = _pair_exchange(gs_ffn, name="grad_pair_exchange_ffn")
    ps_ffn = [_pair_add(g, r, c_arr, name=f"grad_pair_add_{nm}")
              for g, r, nm in zip(gs_ffn, rsib, ["w_out", "w_gate", "w_up", "w_down"])]
    do_a, do_b, dproj, d_wa, d_wb = _merge_bwd(
        dmix, o_a, o_b, proj, wa_row, ret_norm_w, o_az, o_rg, o_ga, o_gb, name="merge_bwd")
    dproj = _ret_bwd(proj, cos2, sin2, lg_tab, s_b, do_b, dproj, o_rq, o_rk, o_rv, name="ret_bwd")
    dq_a, dk_a, dv_a, dbeta_r, dg_r = _gdn_bwd(q_a, k_a, v_a, beta_r, gc_r, tinv, u_a, w_a, s_a, do_a, name="gdn_bwd")
    dc = _conv_bwd_pre(proj, conv_full, dq_a, dk_a, dv_a, name="conv_bwd_pre")
    dproj, d_cw = _conv_bwd(proj, dc, conv_full, dproj, name="conv_bwd")
    dbg = jnp.pad(jnp.concatenate([_rows_to_tokens(dbeta_r), _rows_to_tokens(dg_r)], axis=1),
                  ((0, 0), (0, LANES - 2 * HEADS)))
    dprojs, d_alog, d_dtb = _bg_bwd(projs, dbg, alog_row, dtb_row, name="bg_bwd")
    g_main, *qs_ffn = _mm(u1, dproj, ta=True, out_dtype=BF16, tm=1024, tn=2048, tk=2048,
                          side=_scatter_side(ps_ffn), name="mm_dw_in")
    g_small = _mm(u1, dprojs, ta=True, out_dtype=BF16, tm=1024, tn=LANES, tk=1024, name="mm_dw_in_small")
    pieces = []
    for s in range(N_CHIPS):
        seen, parts = 0, []
        for a, b, dst in shard_pieces(s):
            parts.append(g_small[:, seen:seen + b - a] if dst is None else g_main[:, dst:dst + b - a])
            seen += (b - a) if dst is None else 0
        pieces.append(jnp.concatenate(parts, axis=1))
    gs_in = [halves(jnp.stack(pieces))]
    rsib = _pair_exchange(gs_in, name="grad_pair_exchange_in")
    ps_in = [_pair_add(gs_in[0], rsib[0], c_arr, name="grad_pair_add_w_in")]
    du, *qs_in = _mm(dproj, w_main, tb=True, out_dtype=BF16, tm=1024, tn=d, tk=2048,
                     side=_scatter_side(ps_in), name="mm_du")
    dx, _, d_n1 = _rms_bwd(du, xs, norm1_w, dh1, narrow=(dprojs, w_small), name="rms1_bwd")

    names = ["w_in", "w_out", "w_gate", "w_up", "w_down"]
    qs = [lax.dynamic_update_slice(q, lax.dynamic_slice(p, (s_idx, 0, 0), (1,) + p.shape[1:]), (s_idx, 0, 0))
          for q, p in zip(qs_in + qs_ffn, ps_in + ps_ffn)]
    hs = [_chip_sum(q, name=f"grad_chip_sum_{nm}") for q, nm in zip(qs, names)]
    theirs = _pair_share(hs, name="grad_pair_share")
    big_w = [w_in[0], w_out[0], w_gate[0], w_up[0], w_down[0]]
    big_m = [m_w_in[0], m_w_out[0], m_w_gate[0], m_w_up[0], m_w_down[0]]
    big_v = [v_w_in[0], v_w_out[0], v_w_gate[0], v_w_up[0], v_w_down[0]]
    big = {}
    for nm, mine, other, w_, m_, v_ in zip(names, hs, theirs, big_w, big_m, big_v):
        big[nm] = tuple(a[None] for a in _adamw_halves(w_, mine, other, c_arr, m_, v_, name=f"adamw_{nm}"))

    d_wa_h = jnp.sum(d_wa.reshape(HEADS, DV), axis=0, keepdims=True)
    small = [d_n1, d_alog[:, HEADS:2 * HEADS], d_dtb[:, HEADS:2 * HEADS], d_wa_h, d_wb, d_n2, d_nf,
             d_cw[:CONV_W].reshape(1, -1)]
    sizes = [a.shape[1] for a in small]
    packed = jnp.concatenate(small, axis=1)
    n_pack = packed.shape[1]
    n_rows = -(-n_pack // LANES)
    n_rows = -(-n_rows // 8) * 8
    packed = jnp.pad(packed, ((0, 0), (0, n_rows * LANES - n_pack))).reshape(n_rows, LANES)
    red = _allreduce_small(packed, name="allreduce_small").reshape(1, -1)
    offs = np.cumsum([0] + sizes)
    g_n1, g_alog, g_dtb, g_wa, g_wb, g_n2, g_nf, g_cw = [red[:, offs[i]:offs[i + 1]] for i in range(len(sizes))]
    ncw = conv_w.shape[2]
    g_cw = lax.dynamic_slice(g_cw.reshape(CONV_W, -1), (0, s_idx * ncw), (CONV_W, ncw))

    def small_update(w_, g, m_, v_, nm):
        shape = w_.shape
        pad = (-w_.size) % LANES
        to2 = lambda a: jnp.pad(a.reshape(1, -1), ((0, 0), (0, pad)))
        outs = _adamw(to2(w_), to2(g), to2(m_), to2(v_), name=f"adamw_{nm}")
        return (g.reshape(shape),) + tuple(a[:, :w_.size].reshape(shape) for a in outs)

    res = {
        "norm1_w": small_update(norm1_w, g_n1, m_norm1_w, v_norm1_w, "norm1_w"),
        "w_in": big["w_in"],
        "conv_w": small_update(conv_w, g_cw, m_conv_w, v_conv_w, "conv_w"),
        "a_log": small_update(a_log, g_alog, m_a_log, v_a_log, "a_log"),
        "dt_bias": small_update(dt_bias, g_dtb, m_dt_bias, v_dt_bias, "dt_bias"),
        "gdn_norm_w": small_update(gdn_norm_w, g_wa, m_gdn_norm_w, v_gdn_norm_w, "gdn_norm_w"),
        "ret_norm_w": small_update(ret_norm_w, g_wb, m_ret_norm_w, v_ret_norm_w, "ret_norm_w"),
        "w_out": big["w_out"],
        "norm2_w": small_update(norm2_w, g_n2, m_norm2_w, v_norm2_w, "norm2_w"),
        "w_gate": big["w_gate"],
        "w_up": big["w_up"],
        "w_down": big["w_down"],
        "norm_f_w": small_update(norm_f_w, g_nf, m_norm_f_w, v_norm_f_w, "norm_f_w"),
    }
    order = ["norm1_w", "w_in", "conv_w", "a_log", "dt_bias", "gdn_norm_w", "ret_norm_w", "w_out", "norm2_w",
             "w_gate", "w_up", "w_down", "norm_f_w"]
    loss = lax.psum(loss_row[0, 0], ("x", "y", "c"))
    return (loss, dx[None], *[res[n][0] for n in order], *[res[n][1] for n in order],
            *[res[n][2] for n in order], *[res[n][3] for n in order])
```

```python
import functools

import jax
import jax.numpy as jnp
import numpy as np
from jax import lax
from jax.experimental import pallas as pl
from jax.experimental.pallas import tpu as pltpu

F32 = jnp.float32
BF16 = jnp.bfloat16
MESH = pl.DeviceIdType.MESH

HEADS = 8
DK = 128
DV = 256
CHUNK = 64
CONV_W = 4
EPS = 1e-6
ROPE_BASE = 10000.0
ADAM_LR, ADAM_B1, ADAM_B2, ADAM_EPS, ADAM_WD, ADAM_STEP = 0.001, 0.9, 0.999, 1e-08, 0.01, 10
N_CHIPS = 4
N_DEV = 8
LANES = 128
HALO = 8
VMEM_LIMIT = 56 * 1024 * 1024
HB = 8


def _pick(n, cands):
    for c in cands:
        if n % c == 0:
            return c
    raise ValueError(f"no tile for {n} in {cands}")


def _params(sem=None):
    return pltpu.CompilerParams(dimension_semantics=sem, vmem_limit_bytes=VMEM_LIMIT)


def _dot(a, b):
    return jnp.dot(a.astype(BF16), b.astype(BF16), preferred_element_type=F32)


def _dot_nt(a, b):
    return lax.dot_general(a.astype(BF16), b.astype(BF16), (((1,), (1,)), ((), ())), preferred_element_type=F32)


def _dot_tn(a, b):
    return lax.dot_general(a.astype(BF16), b.astype(BF16), (((0,), (0,)), ((), ())), preferred_element_type=F32)


def _sigmoid(x):
    return 1.0 / (1.0 + jnp.exp(-x))


def _iota2(n):
    return lax.broadcasted_iota(jnp.int32, (n, n), 0), lax.broadcasted_iota(jnp.int32, (n, n), 1)


def _row_to_col(row):
    n = row.shape[1]
    r, c = _iota2(n)
    return jnp.sum(jnp.where(r == c, jnp.broadcast_to(row, (n, n)), 0.0), axis=1, keepdims=True)


def _col_to_row(col):
    n = col.shape[0]
    r, c = _iota2(n)
    return jnp.sum(jnp.where(r == c, jnp.broadcast_to(col, (n, n)), 0.0), axis=0, keepdims=True)


class _Side:
    def __init__(self, ins, out_shapes, n_sem, phases):
        self.ins, self.out_shapes, self.n_sem, self.phases = list(ins), list(out_shapes), n_sem, phases

    def sems(self):
        return [pltpu.SemaphoreType.DMA((self.n_sem,)), pltpu.SemaphoreType.DMA((self.n_sem,))]


def _mm(a, b, *, name, tm, tn, tk, ta=False, tb=False, out_dtype=F32, res=None, side=None, pair=None,
        out_stacked=False):
    m, k = (a.shape[1], a.shape[0]) if ta else a.shape
    b_slots = b.ndim == 3
    if b_slots:
        assert tb and tk == b.shape[2]
        n = b.shape[1]
    else:
        n = b.shape[0] if tb else b.shape[1]
    tm, tn, tk = min(tm, m), min(tn, n), min(tk, k)
    assert m % tm == 0 and n % tn == 0 and k % tk == 0, (name, m, n, k)
    nk = k // tk
    nj, ni = n // tn, m // tm
    dn = (((0 if ta else 1,), (1 if tb else 0,)), ((), ()))
    n_ab = 4 if pair else 2
    n_in = n_ab + (res is not None)
    n_side_in = len(side.ins) if side else 0
    n_side_out = len(side.out_shapes) if side else 0

    def body(*refs):
        a_ref, b_ref = refs[0], refs[1]
        r_ref = refs[n_ab] if res is not None else None
        o_ref = refs[n_in + n_side_in]
        if side:
            s_in = refs[n_in:n_in + n_side_in]
            s_out = refs[n_in + n_side_in + 1:n_in + n_side_in + 1 + n_side_out]
            ssem, rsem = refs[-2], refs[-1]
            j_, i_, k_ = pl.program_id(0), pl.program_id(1), pl.program_id(2)
            n_mid = len(side.phases) - 2
            assert n_mid == 0 or nj >= 2 * n_mid
            when = [(j_ == 0) & (i_ == 0) & (k_ == 0)]
            when += [(j_ == nj // 2 + (p * (nj // 2)) // n_mid) & (i_ == 0) & (k_ == 0) for p in range(n_mid)]
            when.append((j_ == nj - 1) & (i_ == ni - 1) & (k_ == nk - 1))

        def run_phase(p):
            @pl.when(when[p])
            def _():
                side.phases[p](s_in, s_out, ssem, rsem)

        if side:
            for p in range(len(side.phases) - 1):
                run_phase(p)

        def finish(r):
            if res is not None:
                r = r + r_ref[...]
            o_ref[...] = r.astype(out_dtype)

        part = lax.dot_general(a_ref[...], b_ref[...], dn, preferred_element_type=F32)
        if pair:
            part = part + lax.dot_general(refs[2][...], refs[3][...], dn, preferred_element_type=F32)
        if nk == 1:
            finish(part)
        else:
            acc = refs[n_in + n_side_in + 1 + n_side_out]
            kk = pl.program_id(2)

            @pl.when(kk == 0)
            def _():
                acc[...] = part

            @pl.when((kk > 0) & (kk < nk - 1))
            def _():
                acc[...] += part

            @pl.when(kk == nk - 1)
            def _():
                finish(acc[...] + part)

        if side:
            run_phase(len(side.phases) - 1)

    a_spec = pl.BlockSpec((tk, tm), lambda j, i, kk: (kk, i)) if ta else pl.BlockSpec((tm, tk), lambda j, i, kk: (i, kk))
    b_spec = pl.BlockSpec((tn, tk), lambda j, i, kk: (j, kk)) if tb else pl.BlockSpec((tk, tn), lambda j, i, kk: (kk, j))
    if b_slots:
        b_spec = pl.BlockSpec((None, tn, tk), lambda j, i, kk: (kk, j, 0))
    o_spec = pl.BlockSpec((tm, tn), lambda j, i, kk: (i, j))
    o_shape = jax.ShapeDtypeStruct((m, n), out_dtype)
    if out_stacked:
        assert res is None
        o_spec = pl.BlockSpec((None, tm, tn), lambda j, i, kk: (j, i, 0))
        o_shape = jax.ShapeDtypeStruct((nj, m, tn), out_dtype)
    in_specs, args = [a_spec, b_spec], [a, b]
    if pair:
        assert pair[0].shape == a.shape and pair[1].shape == b.shape
        in_specs += [a_spec, b_spec]
        args += list(pair)
    if res is not None:
        in_specs.append(o_spec)
        args.append(res)
    out_specs, out_shape = o_spec, o_shape
    scratch = [pltpu.VMEM((tm, tn), F32)] if nk > 1 else []
    sem = ("parallel", "parallel", "arbitrary")
    if side:
        hbm = pl.BlockSpec(memory_space=pl.ANY)
        in_specs += [hbm] * n_side_in
        args += side.ins
        out_specs, out_shape = [o_spec] + [hbm] * n_side_out, [out_shape] + side.out_shapes
        scratch += side.sems()
        sem = ("arbitrary",) * 3
    return pl.pallas_call(
        body, name=name, grid=(nj, ni, nk), in_specs=in_specs, out_specs=out_specs, out_shape=out_shape,
        scratch_shapes=scratch, compiler_params=_params(sem),
    )(*args)


def _rms_fwd(x, w, *, name, w_narrow=None):
    t, d = x.shape
    tt = _pick(t, (512, 256))

    def body(*refs):
        x_ref, w_ref, o_ref = refs[0], refs[1], refs[-2 if w_narrow is not None else -1]
        xv = x_ref[...]
        r = lax.rsqrt(jnp.mean(xv * xv, axis=-1, keepdims=True) + EPS)
        u = (xv * r * w_ref[...]).astype(BF16)
        o_ref[...] = u
        if w_narrow is not None:
            refs[-1][...] = jnp.dot(u, refs[2][...], preferred_element_type=F32)

    blk = pl.BlockSpec((tt, d), lambda i: (i, 0))
    in_specs, args = [blk, pl.BlockSpec((1, d), lambda i: (0, 0))], [x, w]
    out_specs, out_shape = [blk], [jax.ShapeDtypeStruct((t, d), BF16)]
    if w_narrow is not None:
        in_specs.append(pl.BlockSpec(w_narrow.shape, lambda i: (0, 0)))
        args.append(w_narrow)
        out_specs.append(pl.BlockSpec((tt, LANES), lambda i: (i, 0)))
        out_shape.append(jax.ShapeDtypeStruct((t, LANES), F32))
    out = pl.pallas_call(
        body, name=name, grid=(t // tt,), in_specs=in_specs, out_specs=out_specs, out_shape=out_shape,
        compiler_params=_params(("parallel",)),
    )(*args)
    return out if w_narrow is not None else out[0]


def _rms_bwd(dn, x, w, dres, *, name, narrow=None):
    t, d = x.shape
    tt = _pick(t, (256,))

    def body(*refs):
        dn_ref, x_ref, w_ref, dres_ref = refs[:4]
        dx_ref, dxb_ref, dw_ref = refs[-3:]
        xv, g = x_ref[...], dn_ref[...].astype(F32)
        if narrow is not None:
            g = g + lax.dot_general(refs[4][...], refs[5][...], (((1,), (1,)), ((), ())), preferred_element_type=F32)
        r = lax.rsqrt(jnp.mean(xv * xv, axis=-1, keepdims=True) + EPS)
        xh = xv * r
        gw = g * w_ref[...]
        dx = dres_ref[...] + r * (gw - xh * jnp.mean(gw * xh, axis=-1, keepdims=True))
        dx_ref[...] = dx
        dxb_ref[...] = dx.astype(BF16)

        @pl.when(pl.program_id(0) == 0)
        def _():
            dw_ref[...] = jnp.zeros_like(dw_ref)

        dw_ref[...] += jnp.sum(g * xh, axis=0, keepdims=True)

    blk = pl.BlockSpec((tt, d), lambda i: (i, 0))
    row = pl.BlockSpec((1, d), lambda i: (0, 0))
    in_specs, args = [blk, blk, row, blk], [dn, x, w, dres]
    if narrow is not None:
        in_specs += [pl.BlockSpec((tt, LANES), lambda i: (i, 0)), pl.BlockSpec(narrow[1].shape, lambda i: (0, 0))]
        args += list(narrow)
    return pl.pallas_call(
        body, name=name, grid=(t // tt,), in_specs=in_specs, out_specs=[blk, blk, row],
        out_shape=[jax.ShapeDtypeStruct((t, d), F32), jax.ShapeDtypeStruct((t, d), BF16),
                   jax.ShapeDtypeStruct((1, d), F32)],
        compiler_params=_params(("arbitrary",)),
    )(*args)


def _loss_head(h2, tgt, wf, *, name):
    t, d = h2.shape
    tt = _pick(t, (256,))

    def body(x_ref, t_ref, w_ref, loss_ref, dx_ref, dxb_ref, dw_ref):
        xv = x_ref[...]
        r = lax.rsqrt(jnp.mean(xv * xv, axis=-1, keepdims=True) + EPS)
        xh = xv * r
        err = xh * w_ref[...] - t_ref[...]
        lpart = 0.5 * jnp.sum(jnp.mean(err * err, axis=-1, keepdims=True), axis=0, keepdims=True)
        dy = err * (1.0 / d)
        gw = dy * w_ref[...]
        dx = r * (gw - xh * jnp.mean(gw * xh, axis=-1, keepdims=True))
        dx_ref[...] = dx
        dxb_ref[...] = dx.astype(BF16)

        @pl.when(pl.program_id(0) == 0)
        def _():
            dw_ref[...] = jnp.zeros_like(dw_ref)
            loss_ref[...] = jnp.zeros_like(loss_ref)

        dw_ref[...] += jnp.sum(dy * xh, axis=0, keepdims=True)
        loss_ref[...] += jnp.broadcast_to(lpart, loss_ref.shape)

    blk = pl.BlockSpec((tt, d), lambda i: (i, 0))
    row = pl.BlockSpec((1, d), lambda i: (0, 0))
    lrow = pl.BlockSpec((1, LANES), lambda i: (0, 0))
    return pl.pallas_call(
        body, name=name, grid=(t // tt,),
        in_specs=[blk, blk, row], out_specs=[lrow, blk, blk, row],
        out_shape=[jax.ShapeDtypeStruct((1, LANES), F32), jax.ShapeDtypeStruct((t, d), F32),
                   jax.ShapeDtypeStruct((t, d), BF16), jax.ShapeDtypeStruct((1, d), F32)],
        compiler_params=_params(("arbitrary",)),
    )(h2, tgt, wf)


def _ffn_in(hn, w_g, w_u, *, name, tm):
    t, d = hn.shape
    tn = w_g.shape[2]
    f = w_g.shape[0] * tn
    tm = min(tm, t)

    def body(a_ref, g_ref, u_ref, gt_ref, up_ref, act_ref):
        a = a_ref[...]
        g = jnp.dot(a, g_ref[...], preferred_element_type=F32)
        u = jnp.dot(a, u_ref[...], preferred_element_type=F32)
        gt_ref[...] = g.astype(BF16)
        up_ref[...] = u.astype(BF16)
        act_ref[...] = (g * _sigmoid(g) * u).astype(BF16)

    wblk = pl.BlockSpec((None, d, tn), lambda j, i: (j, 0, 0))
    oblk = pl.BlockSpec((tm, tn), lambda j, i: (i, j))
    return pl.pallas_call(
        body, name=name, grid=(f // tn, t // tm),
        in_specs=[pl.BlockSpec((tm, d), lambda j, i: (i, 0)), wblk, wblk], out_specs=[oblk] * 3,
        out_shape=[jax.ShapeDtypeStruct((t, f), BF16)] * 3,
        compiler_params=_params(("parallel", "parallel")),
    )(hn, w_g, w_u)


def _ffn_back(dh2b, w_d, gt, up, *, name, tm, tn):
    t, d = dh2b.shape
    f = w_d.shape[0]
    tm = min(tm, t)

    def body(a_ref, w_ref, g_ref, u_ref, dg_ref, du_ref):
        da = lax.dot_general(a_ref[...], w_ref[...], (((1,), (1,)), ((), ())), preferred_element_type=F32)
        g = g_ref[...].astype(F32)
        sg = _sigmoid(g)
        dg_ref[...] = (da * u_ref[...].astype(F32) * sg * (1.0 + g * (1.0 - sg))).astype(BF16)
        du_ref[...] = (da * g * sg).astype(BF16)

    oblk = pl.BlockSpec((tm, tn), lambda j, i: (i, j))
    return pl.pallas_call(
        body, name=name, grid=(f // tn, t // tm),
        in_specs=[pl.BlockSpec((tm, d), lambda j, i: (i, 0)), pl.BlockSpec((tn, d), lambda j, i: (j, 0)), oblk, oblk],
        out_specs=[oblk] * 2,
        out_shape=[jax.ShapeDtypeStruct((t, f), BF16)] * 2,
        compiler_params=_params(("parallel", "parallel")),
    )(dh2b, w_d, gt, up)


def _conv_cols(xs, cw_ref, cb, tt):
    cols = slice(cb * LANES, (cb + 1) * LANES)
    base = HALO - (CONV_W - 1)
    acc = xs[base:base + tt, cols] * cw_ref[0:1, cols]
    for j in range(1, CONV_W):
        acc = acc + xs[base + j:base + j + tt, cols] * cw_ref[j:j + 1, cols]
    return acc


def _fill_halo(xs, x_ref, xp_ref, tt):
    first = pl.program_id(0) == 0
    xs[0:HALO, :] = jnp.where(first, 0.0, xp_ref[...])
    xs[HALO:HALO + tt, :] = x_ref[...]


def _conv_specs(tt, ch):
    cur = pl.BlockSpec((tt, ch), lambda i: (i, 0))
    prev = pl.BlockSpec((HALO, ch), lambda i: (jnp.maximum(i * (tt // HALO) - 1, 0), 0))
    return cur, prev


def _conv_fwd(proj, conv_w, *, name):
    t = proj.shape[0]
    ch = conv_w.shape[1]
    nqk = HEADS * DK
    tt = _pick(t, (256,))
    cur, prev = _conv_specs(tt, ch)

    def body(x_ref, xp_ref, cw_ref, q_ref, k_ref, v_ref, xs):
        _fill_halo(xs, x_ref, xp_ref, tt)
        for cb in range(ch // LANES):
            c = _conv_cols(xs, cw_ref, cb, tt)
            s = c * _sigmoid(c)
            if cb < 2 * HEADS:
                s = s * lax.rsqrt(jnp.sum(s * s, axis=-1, keepdims=True) + EPS)
                if cb < HEADS:
                    q_ref[:, cb * LANES:(cb + 1) * LANES] = s * (DK ** -0.5)
                else:
                    k_ref[:, (cb - HEADS) * LANES:(cb - HEADS + 1) * LANES] = s
            else:
                v_ref[:, (cb - 2 * HEADS) * LANES:(cb - 2 * HEADS + 1) * LANES] = s

    return pl.pallas_call(
        body, name=name, grid=(t // tt,),
        in_specs=[cur, prev, pl.BlockSpec((CONV_W, ch), lambda i: (0, 0))],
        out_specs=[pl.BlockSpec((tt, nqk), lambda i: (i, 0)), pl.BlockSpec((tt, nqk), lambda i: (i, 0)),
                   pl.BlockSpec((tt, ch - 2 * nqk), lambda i: (i, 0))],
        out_shape=[jax.ShapeDtypeStruct((t, nqk), F32), jax.ShapeDtypeStruct((t, nqk), F32),
                   jax.ShapeDtypeStruct((t, ch - 2 * nqk), F32)],
        scratch_shapes=[pltpu.VMEM((HALO + tt, ch), F32)],
        compiler_params=_params(("arbitrary",)),
    )(proj, proj, conv_w)


def _conv_bwd_pre(proj, conv_w, dq, dk, dv, *, name):
    t = proj.shape[0]
    ch = conv_w.shape[1]
    nqk = HEADS * DK
    tt = _pick(t, (256,))
    cur, prev = _conv_specs(tt, ch)

    def body(x_ref, xp_ref, cw_ref, dq_ref, dk_ref, dv_ref, dc_ref, xs):
        _fill_halo(xs, x_ref, xp_ref, tt)
        for cb in range(ch // LANES):
            c = _conv_cols(xs, cw_ref, cb, tt)
            sg = _sigmoid(c)
            s = c * sg
            if cb < 2 * HEADS:
                if cb < HEADS:
                    d, scale = dq_ref[:, cb * LANES:(cb + 1) * LANES], DK ** -0.5
                else:
                    d, scale = dk_ref[:, (cb - HEADS) * LANES:(cb - HEADS + 1) * LANES], 1.0
                rinv = lax.rsqrt(jnp.sum(s * s, axis=-1, keepdims=True) + EPS)
                ds = scale * rinv * (d - s * (rinv * rinv) * jnp.sum(d * s, axis=-1, keepdims=True))
            else:
                ds = dv_ref[:, (cb - 2 * HEADS) * LANES:(cb - 2 * HEADS + 1) * LANES]
            dc_ref[:, cb * LANES:(cb + 1) * LANES] = ds * sg * (1.0 + c * (1.0 - sg))

    return pl.pallas_call(
        body, name=name, grid=(t // tt,),
        in_specs=[cur, prev, pl.BlockSpec((CONV_W, ch), lambda i: (0, 0)),
                  pl.BlockSpec((tt, nqk), lambda i: (i, 0)), pl.BlockSpec((tt, nqk), lambda i: (i, 0)),
                  pl.BlockSpec((tt, ch - 2 * nqk), lambda i: (i, 0))],
        out_specs=pl.BlockSpec((tt, ch), lambda i: (i, 0)),
        out_shape=jax.ShapeDtypeStruct((t, ch), F32),
        scratch_shapes=[pltpu.VMEM((HALO + tt, ch), F32)],
        compiler_params=_params(("arbitrary",)),
    )(proj, proj, conv_w, dq, dk, dv)


def _conv_bwd(proj, dc, conv_w, dproj, *, name):
    t = proj.shape[0]
    ch = conv_w.shape[1]
    tt = _pick(t, (256,))
    nt = t // tt
    cur = pl.BlockSpec((tt, ch), lambda i: (i, 0))
    nxt = pl.BlockSpec((HALO, ch), lambda i: (jnp.minimum((i + 1) * (tt // HALO), t // HALO - 1), 0))

    def body(x_ref, d_ref, dn_ref, cw_ref, alias_ref, dx_ref, dw_ref, ds):
        last = pl.program_id(0) == nt - 1
        ds[0:tt, :] = d_ref[...]
        ds[tt:tt + HALO, :] = jnp.where(last, 0.0, dn_ref[...])

        @pl.when(pl.program_id(0) == 0)
        def _():
            dw_ref[...] = jnp.zeros_like(dw_ref)

        for cb in range(ch // LANES):
            cols = slice(cb * LANES, (cb + 1) * LANES)
            x0 = x_ref[:, cols]
            acc = None
            for j in range(CONV_W):
                sh = CONV_W - 1 - j
                dj = ds[sh:sh + tt, cols]
                term = dj * cw_ref[j:j + 1, cols]
                acc = term if acc is None else acc + term
                dw_ref[j:j + 1, cols] += jnp.sum(dj * x0, axis=0, keepdims=True)
            dx_ref[:, cols] = acc.astype(BF16)

    return pl.pallas_call(
        body, name=name, grid=(nt,),
        in_specs=[cur, cur, nxt, pl.BlockSpec((CONV_W, ch), lambda i: (0, 0)), pl.BlockSpec(memory_space=pl.ANY)],
        out_specs=[pl.BlockSpec((tt, ch), lambda i: (i, 0)), pl.BlockSpec((HALO, ch), lambda i: (0, 0))],
        out_shape=[jax.ShapeDtypeStruct(dproj.shape, BF16), jax.ShapeDtypeStruct((HALO, ch), F32)],
        scratch_shapes=[pltpu.VMEM((tt + HALO, ch), F32)],
        input_output_aliases={4: 0},
        compiler_params=_params(("arbitrary",)),
    )(proj, dc, dc, conv_w, dproj)


def _bg_fwd(projs, alog_row, dtb_row, *, name):
    t = projs.shape[0]
    tt = _pick(t, (512, 256))

    def body(p_ref, al_ref, db_ref, o_ref):
        p = p_ref[...]
        lane = lax.broadcasted_iota(jnp.int32, p.shape, 1)
        z = p + db_ref[...]
        sp = jnp.maximum(z, 0.0) + jnp.log1p(jnp.exp(-jnp.abs(z)))
        g = -jnp.exp(al_ref[...]) * sp
        o_ref[...] = jnp.where(lane < HEADS, _sigmoid(p), jnp.where(lane < 2 * HEADS, g, 0.0))

    blk = pl.BlockSpec((tt, LANES), lambda i: (i, 0))
    row = pl.BlockSpec((1, LANES), lambda i: (0, 0))
    return pl.pallas_call(
        body, name=name, grid=(t // tt,), in_specs=[blk, row, row], out_specs=blk,
        out_shape=jax.ShapeDtypeStruct((t, LANES), F32), compiler_params=_params(("parallel",)),
    )(projs, alog_row, dtb_row)


def _bg_bwd(projs, dbg, alog_row, dtb_row, *, name):
    t = projs.shape[0]
    tt = _pick(t, (512, 256))

    def body(p_ref, d_ref, al_ref, db_ref, o_ref, dal_ref, ddb_ref):
        p, d = p_ref[...], d_ref[...]
        lane = lax.broadcasted_iota(jnp.int32, p.shape, 1)
        isg = (lane >= HEADS) & (lane < 2 * HEADS)
        be = _sigmoid(p)
        z = p + db_ref[...]
        sp = jnp.maximum(z, 0.0) + jnp.log1p(jnp.exp(-jnp.abs(z)))
        ea = jnp.exp(al_ref[...])
        d_aa = jnp.where(isg, d * (-ea) * _sigmoid(z), 0.0)
        o_ref[...] = jnp.where(lane < HEADS, d * be * (1.0 - be), d_aa).astype(BF16)

        @pl.when(pl.program_id(0) == 0)
        def _():
            dal_ref[...] = jnp.zeros_like(dal_ref)
            ddb_ref[...] = jnp.zeros_like(ddb_ref)

        dal_ref[...] += jnp.sum(jnp.where(isg, d * (-ea) * sp, 0.0), axis=0, keepdims=True)
        ddb_ref[...] += jnp.sum(d_aa, axis=0, keepdims=True)

    blk = pl.BlockSpec((tt, LANES), lambda i: (i, 0))
    row = pl.BlockSpec((1, LANES), lambda i: (0, 0))
    return pl.pallas_call(
        body, name=name, grid=(t // tt,), in_specs=[blk, blk, row, row], out_specs=[blk, row, row],
        out_shape=[jax.ShapeDtypeStruct((t, LANES), BF16), jax.ShapeDtypeStruct((1, LANES), F32),
                   jax.ShapeDtypeStruct((1, LANES), F32)],
        compiler_params=_params(("arbitrary",)),
    )(projs, dbg, alog_row, dtb_row)


def _hn_specs(n_chunks, rev=False):
    def nn(n):
        return n_chunks - 1 - n if rev else n
    tok = lambda w: pl.BlockSpec((CHUNK, HB * w), lambda g, n: (nn(n), g))
    per = lambda a, b: pl.BlockSpec((HB, None, a, b), lambda g, n: (g, nn(n), 0, 0))
    return tok, per


def _hcols(hh, w):
    return slice(hh * w, (hh + 1) * w)


def _decay(gc_col, gc_row):
    r, c = _iota2(CHUNK)
    d = jnp.exp(jnp.minimum(gc_col - gc_row, 0.0))
    return jnp.where(r >= c, d, 0.0), jnp.where(r > c, d, 0.0)


def _gdn_prep(q, k, v, beta_r, g_r, *, name):
    t = q.shape[0]
    n_chunks = t // CHUNK
    tok, per = _hn_specs(n_chunks)

    def body(q_ref, k_ref, v_ref, b_ref, g_ref, gc_ref, ti_ref, u_ref, w_ref, p_ref):
        r, c = _iota2(CHUNK)
        hs = range(HB)
        kk = [k_ref[:, _hcols(h, DK)] for h in hs]
        gc_col = [jnp.sum(jnp.where(c <= r, jnp.broadcast_to(g_ref[h], (CHUNK, CHUNK)), 0.0), axis=1, keepdims=True)
                  for h in hs]
        gc_row = [_col_to_row(gc_col[h]) for h in hs]
        beta_col = [_row_to_col(b_ref[h]) for h in hs]
        dec = [_decay(gc_col[h], gc_row[h]) for h in hs]
        kb = [kk[h] * beta_col[h] for h in hs]
        for h in hs:
            gc_ref[h] = gc_row[h]
            p_ref[h] = _dot_nt(q_ref[:, _hcols(h, DK)], kk[h]) * dec[h][0]
        pw = [_dot_nt(kb[h], kk[h]) * dec[h][1] for h in hs]
        y = [-pw[h] for h in hs]
        for _ in range(5):
            pw = [_dot(pw[h], pw[h]) for h in hs]
            yp = [_dot(y[h], pw[h]) for h in hs]
            y = [y[h] + pw[h] + yp[h] for h in hs]
        vb = [v_ref[:, _hcols(h, DV)] * beta_col[h] for h in hs]
        kbe = [kb[h] * jnp.exp(gc_col[h]) for h in hs]
        yv = [_dot(y[h], vb[h]) for h in hs]
        yk = [_dot(y[h], kbe[h]) for h in hs]
        for h in hs:
            ti_ref[h] = y[h]
            u_ref[:, _hcols(h, DV)] = vb[h] + yv[h]
            w_ref[:, _hcols(h, DK)] = kbe[h] + yk[h]

    return pl.pallas_call(
        body, name=name, grid=(HEADS // HB, n_chunks),
        in_specs=[tok(DK), tok(DK), tok(DV), per(1, CHUNK), per(1, CHUNK)],
        out_specs=[per(1, CHUNK), per(CHUNK, CHUNK), tok(DV), tok(DK), per(CHUNK, CHUNK)],
        out_shape=[jax.ShapeDtypeStruct((HEADS, n_chunks, 1, CHUNK), F32),
                   jax.ShapeDtypeStruct((HEADS, n_chunks, CHUNK, CHUNK), F32),
                   jax.ShapeDtypeStruct((t, HEADS * DV), F32), jax.ShapeDtypeStruct((t, HEADS * DK), F32),
                   jax.ShapeDtypeStruct((HEADS, n_chunks, CHUNK, CHUNK), F32)],
        compiler_params=_params(("parallel", "parallel")),
    )(q, k, v, beta_r, g_r)


def _gdn_scan(q, k, u, w, p, gc_r, *, name):
    t = q.shape[0]
    n_chunks = t // CHUNK
    tok, per = _hn_specs(n_chunks)

    def body(q_ref, k_ref, u_ref, w_ref, p_ref, gc_ref, o_ref, s_ref, st):
        @pl.when(pl.program_id(1) == 0)
        def _():
            st[...] = jnp.zeros_like(st)

        hs = range(HB)
        s = [st[h] for h in hs]
        gc_row = [gc_ref[h] for h in hs]
        gc_col = [_row_to_col(gc_row[h]) for h in hs]
        glast = [gc_row[h][:, CHUNK - 1:CHUNK] for h in hs]
        for h in hs:
            s_ref[h] = s[h]
        ws = [_dot(w_ref[:, _hcols(h, DK)], s[h]) for h in hs]
        qs = [_dot(q_ref[:, _hcols(h, DK)] * jnp.exp(gc_col[h]), s[h]) for h in hs]
        vn = [u_ref[:, _hcols(h, DV)] - ws[h] for h in hs]
        pv = [_dot(p_ref[h], vn[h]) for h in hs]
        kv = [_dot_tn(k_ref[:, _hcols(h, DK)] * jnp.exp(glast[h] - gc_col[h]), vn[h]) for h in hs]
        for h in hs:
            o_ref[:, _hcols(h, DV)] = qs[h] + pv[h]
            st[h] = s[h] * jnp.exp(glast[h]) + kv[h]

    return pl.pallas_call(
        body, name=name, grid=(HEADS // HB, n_chunks),
        in_specs=[tok(DK), tok(DK), tok(DV), tok(DK), per(CHUNK, CHUNK), per(1, CHUNK)],
        out_specs=[tok(DV), per(DK, DV)],
        out_shape=[jax.ShapeDtypeStruct((t, HEADS * DV), F32),
                   jax.ShapeDtypeStruct((HEADS, n_chunks, DK, DV), F32)],
        scratch_shapes=[pltpu.VMEM((HB, DK, DV), F32)],
        compiler_params=_params(("arbitrary", "arbitrary")),
    )(q, k, u, w, p, gc_r)


def _gdn_bwd(q, k, v, beta_r, gc_r, ti, u, w, s_all, do, *, name):
    t = q.shape[0]
    n_chunks = t // CHUNK
    tok, per = _hn_specs(n_chunks, rev=True)

    def body(q_ref, k_ref, v_ref, b_ref, gc_ref, ti_ref, u_ref, w_ref, s_ref, do_ref,
             dq_ref, dk_ref, dv_ref, db_ref, dg_ref, dst):
        @pl.when(pl.program_id(1) == 0)
        def _():
            dst[...] = jnp.zeros_like(dst)

        r, c = _iota2(CHUNK)
        rows = lax.broadcasted_iota(jnp.int32, (CHUNK, 1), 0)
        rsum = lambda a: jnp.sum(a, axis=1, keepdims=True)
        hs = range(HB)
        qq = [q_ref[:, _hcols(h, DK)] for h in hs]
        kk = [k_ref[:, _hcols(h, DK)] for h in hs]
        ww = [w_ref[:, _hcols(h, DK)] for h in hs]
        uu = [u_ref[:, _hcols(h, DV)] for h in hs]
        d_o = [do_ref[:, _hcols(h, DV)] for h in hs]
        s = [s_ref[h] for h in hs]
        d_s = [dst[h] for h in hs]
        gc_row = [gc_ref[h] for h in hs]
        gc_col = [_row_to_col(gc_row[h]) for h in hs]
        beta_col = [_row_to_col(b_ref[h]) for h in hs]
        eg = [jnp.exp(gc_col[h]) for h in hs]
        glast = [gc_row[h][:, CHUNK - 1:CHUNK] for h in hs]
        eglast = [jnp.exp(glast[h]) for h in hs]
        e2 = [jnp.exp(glast[h] - gc_col[h]) for h in hs]
        dec = [_decay(gc_col[h], gc_row[h]) for h in hs]
        kb = [kk[h] * beta_col[h] for h in hs]
        ke = [kk[h] * e2[h] for h in hs]
        qe = [qq[h] * eg[h] for h in hs]
        kkt = [_dot_nt(kb[h], kk[h]) for h in hs]
        qk = [_dot_nt(qq[h], kk[h]) for h in hs]
        ws = [_dot(ww[h], s[h]) for h in hs]
        a = [kkt[h] * dec[h][1] for h in hs]
        pp = [qk[h] * dec[h][0] for h in hs]
        vn = [uu[h] - ws[h] for h in hs]
        t1 = [_dot_tn(pp[h], d_o[h]) for h in hs]
        t2 = [_dot(ke[h], d_s[h]) for h in hs]
        dqe = [_dot_nt(d_o[h], s[h]) for h in hs]
        dke = [_dot_nt(vn[h], d_s[h]) for h in hs]
        dqk = [_dot_nt(d_o[h], vn[h]) * dec[h][0] for h in hs]
        dvn = [t1[h] + t2[h] for h in hs]
        t3 = [_dot_tn(qe[h], d_o[h]) for h in hs]
        t4 = [_dot_tn(ww[h], dvn[h]) for h in hs]
        dw = [-_dot_nt(dvn[h], s[h]) for h in hs]
        de2 = [rsum(dke[h] * ke[h]) for h in hs]
        dglast = [jnp.sum(de2[h], axis=0, keepdims=True)
                  + eglast[h] * jnp.sum(rsum(d_s[h] * s[h]), axis=0, keepdims=True) for h in hs]
        for h in hs:
            dst[h] = d_s[h] * eglast[h] + t3[h] - t4[h]
        yy = [ti_ref[h] for h in hs]
        t5 = [_dot_tn(yy[h], dvn[h]) for h in hs]
        t6 = [_dot_tn(yy[h], dw[h]) for h in hs]
        dvb = [dvn[h] + t5[h] for h in hs]
        dkbe = [dw[h] + t6[h] for h in hs]
        t7 = [_dot_nt(dvb[h], uu[h]) for h in hs]
        t8 = [_dot_nt(dkbe[h], ww[h]) for h in hs]
        d_a = [jnp.where(r > c, -(t7[h] + t8[h]), 0.0) for h in hs]
        dm = [d_a[h] * dec[h][1] for h in hs]
        t9 = [_dot(dm[h], kk[h]) for h in hs]
        t10 = [_dot(dqk[h], kk[h]) for h in hs]
        t11 = [_dot_tn(dqk[h], qq[h]) for h in hs]
        t12 = [_dot_tn(dm[h], kb[h]) for h in hs]
        for h in hs:
            dkb = t9[h] + dkbe[h] * eg[h]
            e_mat = d_a[h] * a[h] + dqk[h] * qk[h]
            dgc = (rsum(dqe[h] * qe[h]) - de2[h] + rsum(dkbe[h] * kb[h] * eg[h]) + rsum(e_mat)
                   - _row_to_col(jnp.sum(e_mat, axis=0, keepdims=True)))
            dgc = dgc + jnp.where(rows == CHUNK - 1, dglast[h], 0.0)
            dq_ref[:, _hcols(h, DK)] = dqe[h] * eg[h] + t10[h]
            dk_ref[:, _hcols(h, DK)] = t11[h] + dke[h] * e2[h] + t12[h] + dkb * beta_col[h]
            dv_ref[:, _hcols(h, DV)] = dvb[h] * beta_col[h]
            dbeta = rsum(dkb * kk[h]) + rsum(dvb[h] * v_ref[:, _hcols(h, DV)])
            db_ref[h] = _col_to_row(dbeta)
            dg_ref[h] = jnp.sum(jnp.where(r >= c, jnp.broadcast_to(dgc, (CHUNK, CHUNK)), 0.0), axis=0, keepdims=True)

    return pl.pallas_call(
        body, name=name, grid=(HEADS // HB, n_chunks),
        in_specs=[tok(DK), tok(DK), tok(DV), per(1, CHUNK), per(1, CHUNK), per(CHUNK, CHUNK), tok(DV), tok(DK),
                  per(DK, DV), tok(DV)],
        out_specs=[tok(DK), tok(DK), tok(DV), per(1, CHUNK), per(1, CHUNK)],
        out_shape=[jax.ShapeDtypeStruct((t, HEADS * DK), F32), jax.ShapeDtypeStruct((t, HEADS * DK), F32),
                   jax.ShapeDtypeStruct((t, HEADS * DV), F32),
                   jax.ShapeDtypeStruct((HEADS, n_chunks, 1, CHUNK), F32),
                   jax.ShapeDtypeStruct((HEADS, n_chunks, 1, CHUNK), F32)],
        scratch_shapes=[pltpu.VMEM((HB, DK, DV), F32)],
        compiler_params=_params(("arbitrary", "arbitrary")),
    )(q, k, v, beta_r, gc_r, ti, u, w, s_all, do)


def _rot(x, cos2, sin2):
    return x * cos2 + pltpu.roll(x, DK // 2, 1) * sin2


def _unrot(d, cos2, sin2):
    return d * cos2 + pltpu.roll(d * sin2, DK // 2, 1)


def _ret_consts(lg):
    r, c = _iota2(CHUNK)
    dm = jnp.where(r >= c, jnp.exp((r - c).astype(F32) * lg), 0.0)
    pos = lax.broadcasted_iota(jnp.int32, (CHUNK, 1), 0).astype(F32)
    return dm, jnp.exp((pos + 1.0) * lg), jnp.exp((CHUNK - 1.0 - pos) * lg), jnp.exp(CHUNK * lg)


def _ret_fwd(proj, cos2, sin2, lg_tab, col_q, col_k, col_v, *, name):
    t = proj.shape[0]
    n_chunks = t // CHUNK
    tok, per = _hn_specs(n_chunks)
    bq, bk, bv = col_q // (HB * DK), col_k // (HB * DK), col_v // (HB * DV)

    def body(q_ref, k_ref, v_ref, cos_ref, sin_ref, lg_ref, o_ref, s_ref, st):
        @pl.when(pl.program_id(1) == 0)
        def _():
            st[...] = jnp.zeros_like(st)

        cos2v, sin2v = cos_ref[...], sin_ref[...]
        hs = range(HB)
        s = [st[h] for h in hs]
        for h in hs:
            s_ref[h] = s[h]
        cst = [_ret_consts(lg_ref[h][:, 0:1]) for h in hs]
        qq = [_rot(q_ref[:, _hcols(h, DK)], cos2v, sin2v) for h in hs]
        kk = [_rot(k_ref[:, _hcols(h, DK)], cos2v, sin2v) * (DK ** -0.5) for h in hs]
        vv = [v_ref[:, _hcols(h, DV)] for h in hs]
        sc = [_dot_nt(qq[h], kk[h]) * cst[h][0] for h in hs]
        qs = [_dot(qq[h], s[h]) for h in hs]
        kv = [_dot_tn(kk[h] * cst[h][2], vv[h]) for h in hs]
        sv = [_dot(sc[h], vv[h]) for h in hs]
        for h in hs:
            o_ref[:, _hcols(h, DV)] = sv[h] + qs[h] * cst[h][1]
            st[h] = s[h] * cst[h][3] + kv[h]

    return pl.pallas_call(
        body, name=name, grid=(HEADS // HB, n_chunks),
        in_specs=[pl.BlockSpec((CHUNK, HB * DK), lambda g, n: (n, bq + g)),
                  pl.BlockSpec((CHUNK, HB * DK), lambda g, n: (n, bk + g)),
                  pl.BlockSpec((CHUNK, HB * DV), lambda g, n: (n, bv + g)),
                  pl.BlockSpec((CHUNK, DK), lambda g, n: (n, 0)), pl.BlockSpec((CHUNK, DK), lambda g, n: (n, 0)),
                  pl.BlockSpec((HB, 1, LANES), lambda g, n: (g, 0, 0))],
        out_specs=[tok(DV), per(DK, DV)],
        out_shape=[jax.ShapeDtypeStruct((t, HEADS * DV), F32),
                   jax.ShapeDtypeStruct((HEADS, n_chunks, DK, DV), F32)],
        scratch_shapes=[pltpu.VMEM((HB, DK, DV), F32)],
        compiler_params=_params(("arbitrary", "arbitrary")),
    )(proj, proj, proj, cos2, sin2, lg_tab)


def _ret_bwd(proj, cos2, sin2, lg_tab, s_all, do, dproj, col_q, col_k, col_v, *, name):
    t = proj.shape[0]
    n_chunks = t // CHUNK
    tok, per = _hn_specs(n_chunks, rev=True)
    bq, bk, bv = col_q // (HB * DK), col_k // (HB * DK), col_v // (HB * DV)
    rv = lambda n: n_chunks - 1 - n
    nqk, wid = HEADS * DK, HEADS * (2 * DK + DV)
    assert HB == HEADS and col_k == col_q + nqk and col_v == col_k + nqk and col_q % wid == 0

    def body(q_ref, k_ref, v_ref, cos_ref, sin_ref, lg_ref, s_ref, do_ref, alias_ref, d_ref, dst):
        @pl.when(pl.program_id(1) == 0)
        def _():
            dst[...] = jnp.zeros_like(dst)

        cos2v, sin2v = cos_ref[...], sin_ref[...]
        hs = range(HB)
        s = [s_ref[h] for h in hs]
        d_s = [dst[h] for h in hs]
        d_o = [do_ref[:, _hcols(h, DV)] for h in hs]
        cst = [_ret_consts(lg_ref[h][:, 0:1]) for h in hs]
        qq = [_rot(q_ref[:, _hcols(h, DK)], cos2v, sin2v) for h in hs]
        kk = [_rot(k_ref[:, _hcols(h, DK)], cos2v, sin2v) * (DK ** -0.5) for h in hs]
        vv = [v_ref[:, _hcols(h, DV)] for h in hs]
        dxo = [d_o[h] * cst[h][1] for h in hs]
        sc = [_dot_nt(qq[h], kk[h]) * cst[h][0] for h in hs]
        dsc = [_dot_nt(d_o[h], vv[h]) * cst[h][0] for h in hs]
        t1 = [_dot(kk[h] * cst[h][2], d_s[h]) for h in hs]
        t2 = [_dot_nt(dxo[h], s[h]) for h in hs]
        t3 = [_dot_nt(vv[h], d_s[h]) for h in hs]
        t4 = [_dot_tn(qq[h], dxo[h]) for h in hs]
        t5 = [_dot_tn(sc[h], d_o[h]) for h in hs]
        t6 = [_dot(dsc[h], kk[h]) for h in hs]
        t7 = [_dot_tn(dsc[h], qq[h]) for h in hs]
        for h in hs:
            dst[h] = d_s[h] * cst[h][3] + t4[h]
            d_ref[:, 2 * nqk + h * DV:2 * nqk + (h + 1) * DV] = (t5[h] + t1[h]).astype(BF16)
            d_ref[:, _hcols(h, DK)] = _unrot(t6[h] + t2[h], cos2v, sin2v).astype(BF16)
            d_ref[:, nqk + h * DK:nqk + (h + 1) * DK] = (
                _unrot(t7[h] + t3[h] * cst[h][2], cos2v, sin2v) * (DK ** -0.5)).astype(BF16)

    return pl.pallas_call(
        body, name=name, grid=(HEADS // HB, n_chunks),
        in_specs=[pl.BlockSpec((CHUNK, HB * DK), lambda g, n: (rv(n), bq + g)),
                  pl.BlockSpec((CHUNK, HB * DK), lambda g, n: (rv(n), bk + g)),
                  pl.BlockSpec((CHUNK, HB * DV), lambda g, n: (rv(n), bv + g)),
                  pl.BlockSpec((CHUNK, DK), lambda g, n: (rv(n), 0)),
                  pl.BlockSpec((CHUNK, DK), lambda g, n: (rv(n), 0)),
                  pl.BlockSpec((HB, 1, LANES), lambda g, n: (g, 0, 0)),
                  per(DK, DV), tok(DV), pl.BlockSpec(memory_space=pl.ANY)],
        out_specs=pl.BlockSpec((CHUNK, wid), lambda g, n: (rv(n), col_q // wid)),
        out_shape=jax.ShapeDtypeStruct(dproj.shape, BF16),
        scratch_shapes=[pltpu.VMEM((HB, DK, DV), F32)],
        input_output_aliases={8: 0},
        compiler_params=_params(("arbitrary", "arbitrary")),
    )(proj, proj, proj, cos2, sin2, lg_tab, s_all, do, dproj)


def _merge_parts(oa, ob, z, rg, ga, gb):
    ra = lax.rsqrt(jnp.mean(oa * oa, axis=-1, keepdims=True) + EPS)
    xa = oa * ra
    mu = jnp.mean(ob, axis=-1, keepdims=True)
    cen = ob - mu
    rb = lax.rsqrt(jnp.mean(cen * cen, axis=-1, keepdims=True) + EPS)
    xb = cen * rb
    sz, sr = _sigmoid(z), _sigmoid(rg)
    return ra, xa, rb, xb, sz, sr, _sigmoid(ga), _sigmoid(gb)


def _merge_specs(tt, d, cz, crg, cga, cgb):
    blk = pl.BlockSpec((tt, d), lambda i: (i, 0))
    pcol = lambda c: pl.BlockSpec((tt, d), lambda i: (i, c // d))
    row = pl.BlockSpec((1, d), lambda i: (0, 0))
    return blk, [blk, blk, pcol(cz), pcol(crg), pcol(cga), pcol(cgb), row, row], row


def _merge_fwd(oa, ob, proj, wa, wb, cz, crg, cga, cgb, *, name):
    t, d = oa.shape
    tt = _pick(t, (256,))
    blk, in_specs, _ = _merge_specs(tt, d, cz, crg, cga, cgb)

    def body(oa_ref, ob_ref, z_ref, rg_ref, ga_ref, gb_ref, wa_ref, wb_ref, o_ref):
        for h in range(HEADS):
            cols = slice(h * DV, (h + 1) * DV)
            z, rg = z_ref[:, cols], rg_ref[:, cols]
            _, xa, _, xb, sz, sr, sga, sgb = _merge_parts(
                oa_ref[:, cols], ob_ref[:, cols], z, rg, ga_ref[:, cols], gb_ref[:, cols])
            o_a = xa * wa_ref[:, cols] * (z * sz)
            o_b = xb * wb_ref[:, cols] * (rg * sr)
            o_ref[:, cols] = (sga * o_a + sgb * o_b).astype(BF16)

    return pl.pallas_call(
        body, name=name, grid=(t // tt,), in_specs=in_specs, out_specs=blk,
        out_shape=jax.ShapeDtypeStruct((t, d), BF16), compiler_params=_params(("parallel",)),
    )(oa, ob, proj, proj, proj, proj, wa, wb)


def _merge_bwd(dmix, oa, ob, proj, wa, wb, cz, crg, cga, cgb, *, name):
    t, d = oa.shape
    tt = _pick(t, (256,))
    blk, in_specs, row = _merge_specs(tt, d, cz, crg, cga, cgb)
    assert (crg, cga, cgb) == (cz + d, cz + 2 * d, cz + 3 * d) and cz % (4 * d) == 0

    def body(dm_ref, oa_ref, ob_ref, z_ref, rg_ref, ga_ref, gb_ref, wa_ref, wb_ref,
             doa_ref, dob_ref, dp_ref, dwa_ref, dwb_ref):
        dz_ref, drg_ref, dga_ref, dgb_ref = [dp_ref.at[:, i * d:(i + 1) * d] for i in range(4)]

        @pl.when(pl.program_id(0) == 0)
        def _():
            dwa_ref[...] = jnp.zeros_like(dwa_ref)
            dwb_ref[...] = jnp.zeros_like(dwb_ref)

        for h in range(HEADS):
            cols = slice(h * DV, (h + 1) * DV)
            z, rg, wa_h, wb_h, dmx = z_ref[:, cols], rg_ref[:, cols], wa_ref[:, cols], wb_ref[:, cols], dm_ref[:, cols]
            ra, xa, rb, xb, sz, sr, sga, sgb = _merge_parts(
                oa_ref[:, cols], ob_ref[:, cols], z, rg, ga_ref[:, cols], gb_ref[:, cols])
            na, nb = xa * wa_h, xb * wb_h
            sil_z, sil_r = z * sz, rg * sr
            o_a, o_b = na * sil_z, nb * sil_r
            dga_ref[:, cols] = (dmx * o_a * sga * (1.0 - sga)).astype(BF16)
            dgb_ref[:, cols] = (dmx * o_b * sgb * (1.0 - sgb)).astype(BF16)
            d_oa, d_ob = dmx * sga, dmx * sgb
            dz_ref[:, cols] = (d_oa * na * sz * (1.0 + z * (1.0 - sz))).astype(BF16)
            drg_ref[:, cols] = (d_ob * nb * sr * (1.0 + rg * (1.0 - sr))).astype(BF16)
            dna, dnb = d_oa * sil_z, d_ob * sil_r
            dwa_ref[:, cols] += jnp.sum(dna * xa, axis=0, keepdims=True)
            dwb_ref[:, cols] += jnp.sum(dnb * xb, axis=0, keepdims=True)
            gwa, gwb = dna * wa_h, dnb * wb_h
            doa_ref[:, cols] = ra * (gwa - xa * jnp.mean(gwa * xa, axis=-1, keepdims=True))
            dob_ref[:, cols] = rb * (gwb - jnp.mean(gwb, axis=-1, keepdims=True)
                                     - xb * jnp.mean(gwb * xb, axis=-1, keepdims=True))

    return pl.pallas_call(
        body, name=name, grid=(t // tt,), in_specs=[blk] + in_specs,
        out_specs=[blk, blk, pl.BlockSpec((tt, 4 * d), lambda i: (i, cz // (4 * d))), row, row],
        out_shape=[jax.ShapeDtypeStruct((t, d), F32)] * 2 + [jax.ShapeDtypeStruct(proj.shape, BF16)]
        + [jax.ShapeDtypeStruct((1, d), F32)] * 2,
        compiler_params=_params(("arbitrary",)),
    )(dmix, oa, ob, proj, proj, proj, proj, wa, wb)


def _row_block(rows, cols, itemsize=4, target=1 << 20):
    for rb in (512, 256, 128, 64, 32, 16, 8):
        if rows % rb == 0 and rb * cols * itemsize <= target:
            return rb
    return rows


def _adamw(w, g, m, v, *, name):
    rows, cols = w.shape
    rb = _row_block(rows, cols)

    def body(w_ref, g_ref, m_ref, v_ref, d_ref, nm_ref, nv_ref):
        gg = g_ref[...]
        mm = ADAM_B1 * m_ref[...] + (1.0 - ADAM_B1) * gg
        vv = ADAM_B2 * v_ref[...] + (1.0 - ADAM_B2) * (gg * gg)
        m_hat = mm / (1.0 - ADAM_B1 ** ADAM_STEP)
        v_hat = vv / (1.0 - ADAM_B2 ** ADAM_STEP)
        d_ref[...] = -ADAM_LR * (m_hat / (jnp.sqrt(v_hat) + ADAM_EPS) + ADAM_WD * w_ref[...])
        nm_ref[...] = mm
        nv_ref[...] = vv

    blk = pl.BlockSpec((rb, cols), lambda i: (i, 0))
    return pl.pallas_call(
        body, name=name, grid=(rows // rb,), in_specs=[blk] * 4, out_specs=[blk] * 3,
        out_shape=[jax.ShapeDtypeStruct((rows, cols), F32)] * 3, compiler_params=_params(("parallel",)),
    )(w, g, m, v)


def _adamw_halves(w, mine, other, c_idx, m, v, *, name):
    rows, cols = w.shape
    hr = rows // 2
    rb = _row_block(hr, cols)
    nb = hr // rb

    def body(c_ref, w_ref, a_ref, b_ref, m_ref, v_ref, g_ref, d_ref, nm_ref, nv_ref):
        gg = jnp.where(pl.program_id(0) // nb == c_ref[0], a_ref[...], b_ref[...])
        mm = ADAM_B1 * m_ref[...] + (1.0 - ADAM_B1) * gg
        vv = ADAM_B2 * v_ref[...] + (1.0 - ADAM_B2) * (gg * gg)
        m_hat = mm / (1.0 - ADAM_B1 ** ADAM_STEP)
        v_hat = vv / (1.0 - ADAM_B2 ** ADAM_STEP)
        g_ref[...] = gg
        d_ref[...] = -ADAM_LR * (m_hat / (jnp.sqrt(v_hat) + ADAM_EPS) + ADAM_WD * w_ref[...])
        nm_ref[...] = mm
        nv_ref[...] = vv

    blk = pl.BlockSpec((rb, cols), lambda i, c: (i, 0))
    half = pl.BlockSpec((rb, cols), lambda i, c: (i % nb, 0))
    return pl.pallas_call(
        body, name=name,
        grid_spec=pltpu.PrefetchScalarGridSpec(
            num_scalar_prefetch=1, grid=(rows // rb,), in_specs=[blk, half, half, blk, blk], out_specs=[blk] * 4),
        out_shape=[jax.ShapeDtypeStruct((rows, cols), F32)] * 4, compiler_params=_params(("parallel",)),
    )(c_idx, w, mine, other, m, v)


def _pair_add(g, rsib, c_idx, *, name):
    _, _, hr, cols = g.shape
    rb = _row_block(hr, cols, 2)

    def body(c_ref, g_ref, r_ref, o_ref):
        o_ref[...] = (g_ref[...].astype(F32) + r_ref[...].astype(F32)).astype(BF16)

    return pl.pallas_call(
        body, name=name,
        grid_spec=pltpu.PrefetchScalarGridSpec(
            num_scalar_prefetch=1, grid=(N_CHIPS, hr // rb),
            in_specs=[pl.BlockSpec((None, None, rb, cols), lambda j, i, c: (j, c[0], i, 0)),
                      pl.BlockSpec((None, rb, cols), lambda j, i, c: (j, i, 0))],
            out_specs=pl.BlockSpec((None, rb, cols), lambda j, i, c: (j, i, 0))),
        out_shape=jax.ShapeDtypeStruct((N_CHIPS, hr, cols), BF16),
        compiler_params=_params(("parallel", "parallel")),
    )(c_idx, g, rsib)


def _chip_sum(qb, *, name):
    _, hr, cols = qb.shape
    rb = _row_block(hr, cols, 2, 1 << 19)

    def body(q_ref, o_ref):
        acc = q_ref[0].astype(F32)
        for i in range(1, N_CHIPS):
            acc = acc + q_ref[i].astype(F32)
        o_ref[...] = acc

    return pl.pallas_call(
        body, name=name, grid=(hr // rb,),
        in_specs=[pl.BlockSpec((N_CHIPS, rb, cols), lambda i: (0, i, 0))],
        out_specs=pl.BlockSpec((rb, cols), lambda i: (i, 0)),
        out_shape=jax.ShapeDtypeStruct((hr, cols), F32), compiler_params=_params(("parallel",)),
    )(qb)


def _place():
    x, y, c = lax.axis_index("x"), lax.axis_index("y"), lax.axis_index("c")
    return x, y, c, [(1 - x, y), (x, 1 - y), (1 - x, 1 - y)]


ANY = pl.BlockSpec(memory_space=pl.ANY)


def _gather_side(shards):
    nw, n_k = len(shards), 8

    def plan(ins, outs, ssem, rsem):
        x, y, c, _ = _place()
        nbr_x, nbr_y, sib = (1 - x, y, c), (x, 1 - y, c), (x, y, 1 - c)
        s_me, s_x, s_y, s_d = 2 * x + y, 2 * (1 - x) + y, 2 * x + 1 - y, 2 * (1 - x) + 1 - y

        def rows(w, half, quarter=None):
            hr = shards[w].shape[0] // 2
            if quarter is None:
                return pl.ds(pl.multiple_of(half * hr, 16), hr)
            return pl.ds(pl.multiple_of(half * hr + quarter * (hr // 2), 16), hr // 2)

        def rcopy(w, k, src, slot, rws, to):
            return pltpu.make_async_remote_copy(
                src_ref=src, dst_ref=outs[w].at[slot, rws], send_sem=ssem.at[w * n_k + k],
                recv_sem=rsem.at[w * n_k + k], device_id=to, device_id_type=MESH)

        out = []
        for w in range(nw):
            mine = ins[w].at[rows(w, c)]
            q = [rows(w, c, 0), rows(w, c, 1)]
            p = functools.partial
            out.append([
                (p(rcopy, w, 0, mine, s_me, rows(w, c), nbr_x), p(rcopy, w, 0, mine, s_x, rows(w, c), nbr_x)),
                (p(rcopy, w, 1, mine, s_me, rows(w, c), nbr_y), p(rcopy, w, 1, mine, s_y, rows(w, c), nbr_y)),
                (p(rcopy, w, 2, outs[w].at[s_x, q[0]], s_x, q[0], nbr_y),
                 p(rcopy, w, 2, outs[w].at[s_x, q[0]], s_d, q[0], nbr_y)),
                (p(rcopy, w, 3, outs[w].at[s_y, q[1]], s_y, q[1], nbr_x),
                 p(rcopy, w, 3, outs[w].at[s_y, q[1]], s_d, q[1], nbr_x)),
                (p(rcopy, w, 4, outs[w].at[s_x, rows(w, c)], s_x, rows(w, c), sib),
                 p(rcopy, w, 4, mine, s_x, rows(w, 1 - c), sib)),
                (p(rcopy, w, 5, outs[w].at[s_y, rows(w, c)], s_y, rows(w, c), sib),
                 p(rcopy, w, 5, mine, s_y, rows(w, 1 - c), sib)),
                (p(rcopy, w, 6, outs[w].at[s_d, q[0]], s_d, q[0], sib),
                 p(rcopy, w, 6, outs[w].at[s_d, q[0]], s_d, rows(w, 1 - c, 0), sib)),
                (p(rcopy, w, 7, outs[w].at[s_d, q[1]], s_d, q[1], sib),
                 p(rcopy, w, 7, outs[w].at[s_d, q[1]], s_d, rows(w, 1 - c, 1), sib)),
            ])
        return out

    def send_own(*refs):
        for cps in plan(*refs):
            cps[0][0]().start()
            cps[1][0]().start()

    def relay(*refs):
        for cps in plan(*refs):
            cps[0][1]().wait_recv()
            cps[2][0]().start()
            cps[4][0]().start()
            cps[1][1]().wait_recv()
            cps[3][0]().start()
            cps[5][0]().start()

    def pass_diagonal(*refs):
        for cps in plan(*refs):
            cps[2][1]().wait_recv()
            cps[6][0]().start()
            cps[3][1]().wait_recv()
            cps[7][0]().start()

    def finish(*refs):
        for cps in plan(*refs):
            for k in (4, 5, 6, 7):
                cps[k][1]().wait_recv()
            for k in range(n_k):
                cps[k][0]().wait_send()

    return _Side(shards, [jax.ShapeDtypeStruct((N_CHIPS,) + s.shape, s.dtype) for s in shards], nw * n_k,
                 [send_own, relay, pass_diagonal, finish])


def _run_side(side, *, name):
    n_in, n_out = len(side.ins), len(side.out_shapes)

    def body(*refs):
        ins, outs = refs[:n_in], refs[n_in:n_in + n_out]
        for phase in side.phases:
            phase(ins, outs, refs[-2], refs[-1])

    return pl.pallas_call(
        body, name=name, in_specs=[ANY] * n_in, out_specs=[ANY] * n_out, out_shape=side.out_shapes,
        scratch_shapes=side.sems(), compiler_params=pltpu.CompilerParams(has_side_effects=True),
    )(*side.ins)


def _pair_exchange(gs, *, name):
    nw = len(gs)

    def body(*refs):
        ins, outs = refs[:nw], refs[nw:2 * nw]
        ssem, rsem = refs[2 * nw:]
        x, y, c, _ = _place()
        cps = []
        for w in range(nw):
            cp = pltpu.make_async_remote_copy(
                src_ref=ins[w].at[:, 1 - c], dst_ref=outs[w], send_sem=ssem.at[w], recv_sem=rsem.at[w],
                device_id=(x, y, 1 - c), device_id_type=MESH)
            cp.start()
            cps.append(cp)
        for cp in cps:
            cp.wait()

    return pl.pallas_call(
        body, name=name, in_specs=[ANY] * nw, out_specs=[ANY] * nw,
        out_shape=[jax.ShapeDtypeStruct((g.shape[0],) + g.shape[2:], g.dtype) for g in gs],
        scratch_shapes=[pltpu.SemaphoreType.DMA((nw,)), pltpu.SemaphoreType.DMA((nw,))],
        compiler_params=pltpu.CompilerParams(has_side_effects=True),
    )(*gs)


def _scatter_side(ps):
    nw = len(ps)

    def copies(ins, outs, ssem, rsem):
        x, y, c, chips = _place()
        s_me = 2 * x + y
        return [pltpu.make_async_remote_copy(
            src_ref=ins[w].at[2 * chip[0] + chip[1]], dst_ref=outs[w].at[s_me],
            send_sem=ssem.at[w * 3 + j], recv_sem=rsem.at[w * 3 + j],
            device_id=(chip[0], chip[1], c), device_id_type=MESH)
            for w in range(nw) for j, chip in enumerate(chips)]

    def start(*refs):
        for cp in copies(*refs):
            cp.start()

    def finish(*refs):
        for cp in copies(*refs):
            cp.wait()

    return _Side(ps, [jax.ShapeDtypeStruct(p.shape, p.dtype) for p in ps], nw * 3, [start, finish])


def _pair_share(hs, *, name):
    nw = len(hs)

    def body(*refs):
        ins, outs = refs[:nw], refs[nw:2 * nw]
        ssem, rsem = refs[2 * nw:]
        x, y, c, _ = _place()
        cps = []
        for w in range(nw):
            cp = pltpu.make_async_remote_copy(
                src_ref=ins[w], dst_ref=outs[w], send_sem=ssem.at[w], recv_sem=rsem.at[w],
                device_id=(x, y, 1 - c), device_id_type=MESH)
            cp.start()
            cps.append(cp)
        for cp in cps:
            cp.wait()

    return pl.pallas_call(
        body, name=name, in_specs=[ANY] * nw, out_specs=[ANY] * nw,
        out_shape=[jax.ShapeDtypeStruct(h.shape, h.dtype) for h in hs],
        scratch_shapes=[pltpu.SemaphoreType.DMA((nw,)), pltpu.SemaphoreType.DMA((nw,))],
        compiler_params=pltpu.CompilerParams(has_side_effects=True),
    )(*hs)


def _gather_small(a, *, name):
    def body(a_ref, o_ref, ssem, rsem):
        x, y, c, chips = _place()
        s_me = 2 * x + y
        o_ref[s_me] = a_ref[...]
        cps = []
        for j, chip in enumerate(chips):
            cp = pltpu.make_async_remote_copy(
                src_ref=a_ref, dst_ref=o_ref.at[s_me], send_sem=ssem.at[j], recv_sem=rsem.at[j],
                device_id=(chip[0], chip[1], c), device_id_type=MESH)
            cp.start()
            cps.append(cp)
        for cp in cps:
            cp.wait()

    vm = pl.BlockSpec(memory_space=pltpu.VMEM)
    return pl.pallas_call(
        body, name=name, in_specs=[vm], out_specs=vm,
        out_shape=jax.ShapeDtypeStruct((N_CHIPS,) + a.shape, a.dtype),
        scratch_shapes=[pltpu.SemaphoreType.DMA((3,)), pltpu.SemaphoreType.DMA((3,))],
    )(a)


def _allreduce_small(p, *, name):
    def body(p_ref, o_ref, buf, ssem, rsem):
        x, y, c, _ = _place()
        me = 4 * x + 2 * y + c
        buf[me] = p_ref[...]
        cps = []
        for k in range(1, N_DEV):
            fx, fy, fc = (k >> 2) & 1, (k >> 1) & 1, k & 1
            peer = (x + fx - 2 * x * fx, y + fy - 2 * y * fy, c + fc - 2 * c * fc)
            cp = pltpu.make_async_remote_copy(
                src_ref=buf.at[me], dst_ref=buf.at[me], send_sem=ssem.at[k - 1], recv_sem=rsem.at[k - 1],
                device_id=peer, device_id_type=MESH)
            cp.start()
            cps.append(cp)
        for cp in cps:
            cp.wait()
        acc = buf[0]
        for i in range(1, N_DEV):
            acc = acc + buf[i]
        o_ref[...] = acc

    vm = pl.BlockSpec(memory_space=pltpu.VMEM)
    return pl.pallas_call(
        body, name=name, in_specs=[vm], out_specs=vm,
        out_shape=jax.ShapeDtypeStruct(p.shape, p.dtype),
        scratch_shapes=[pltpu.VMEM((N_DEV,) + p.shape, p.dtype), pltpu.SemaphoreType.DMA((N_DEV - 1,)),
                        pltpu.SemaphoreType.DMA((N_DEV - 1,))],
    )(p)


def _rows_to_tokens(r):
    h, n = r.shape[0], r.shape[1]
    return r.reshape(h, n * CHUNK).T


def _tokens_to_rows(a):
    t, h = a.shape
    return a.T.reshape(h, t // CHUNK, 1, CHUNK)


def kernel(x, norm1_w, w_in, conv_w, a_log, dt_bias, gdn_norm_w, ret_norm_w, w_out, norm2_w, w_gate, w_up, w_down, norm_f_w, loss_target, m_norm1_w, m_w_in, m_conv_w, m_a_log, m_dt_bias, m_gdn_norm_w, m_ret_norm_w, m_w_out, m_norm2_w, m_w_gate, m_w_up, m_w_down, m_norm_f_w, v_norm1_w, v_w_in, v_conv_w, v_a_log, v_dt_bias, v_gdn_norm_w, v_ret_norm_w, v_w_out, v_norm2_w, v_w_gate, v_w_up, v_w_down, v_norm_f_w):
    t, d = x.shape[1], x.shape[2]
    f = w_gate.shape[2] * N_CHIPS
    nqk, nv = HEADS * DK, HEADS * DV
    ncs = w_in.shape[2]
    c_idx = lax.axis_index("c")
    s_idx = 2 * lax.axis_index("x") + lax.axis_index("y")
    xs = x[0]
    tgt = loss_target[0]

    n_small = 2 * HEADS
    widths = [2 * nqk + nv, nv, n_small, nqk, nqk, nv, nv, d, d]
    g_off = np.concatenate([[0], np.cumsum(widths)])
    order = [0, 3, 4, 5, 1, 6, 7, 8]
    m_off = np.concatenate([[0], np.cumsum([widths[i] for i in order])])
    o_rq, o_rk, o_rv, o_az, o_rg, o_ga, o_gb = [int(m_off[i]) for i in range(1, 8)]
    segs = [(int(g_off[i]), int(g_off[i + 1]), int(m_off[order.index(i)]) if i != 2 else None) for i in range(9)]

    def shard_pieces(s):
        out = []
        for a, b, dst in segs:
            lo, hi = max(a, s * ncs), min(b, (s + 1) * ncs)
            if lo < hi:
                out.append((lo - s * ncs, hi - s * ncs, None if dst is None else dst + lo - a))
        return out

    own = [w_in[0].astype(BF16), w_out[0].astype(BF16), w_gate[0].astype(BF16), w_up[0].astype(BF16),
           w_down[0].astype(BF16)]
    put_own = lambda full, mine: lax.dynamic_update_slice(full, mine[None], (s_idx, 0, 0))
    wg_in = put_own(_run_side(_gather_side(own[:1]), name="gather_w_in")[0], own[0])
    conv_full = _gather_small(conv_w[0], name="gather_conv_w")
    conv_full = jnp.concatenate([conv_full[i] for i in range(N_CHIPS)], axis=1)
    cuts = [(dst, s, a, b) for s in range(N_CHIPS) for a, b, dst in shard_pieces(s)]
    w_main = jnp.concatenate([wg_in[s][:, a:b] for dst, s, a, b in sorted(c for c in cuts if c[0] is not None)],
                             axis=1)
    w_small = jnp.concatenate([wg_in[s][:, a:b] for dst, s, a, b in cuts if dst is None], axis=1)
    w_small = jnp.pad(w_small, ((0, 0), (0, LANES - n_small)))

    pad16 = lambda a: jnp.pad(a, ((0, 0), (HEADS, LANES - 2 * HEADS)))
    alog_row, dtb_row = pad16(a_log), pad16(dt_bias)
    wa_row = jnp.tile(gdn_norm_w, (1, HEADS))
    inv = ROPE_BASE ** (-jnp.arange(0, DK, 2, dtype=F32) / DK)
    ang = jnp.arange(t, dtype=F32)[:, None] * inv[None, :]
    cos2 = jnp.concatenate([jnp.cos(ang), jnp.cos(ang)], axis=1)
    sin2 = jnp.concatenate([-jnp.sin(ang), jnp.sin(ang)], axis=1)
    lg = jnp.log1p(-jnp.exp2(-5.0 - jnp.arange(HEADS, dtype=F32)))
    lg_tab = jnp.broadcast_to(lg[:, None, None], (HEADS, 1, LANES))

    fq = f // N_CHIPS
    u1, projs = _rms_fwd(xs, norm1_w, w_narrow=w_small, name="rms1_fwd")
    proj, *rest = _mm(u1, w_main, tm=512, tn=2048, tk=d, side=_gather_side(own[1:]), name="mm_proj")
    wg_out, wg_gate, wg_up, wg_down = [put_own(g, o) for g, o in zip(rest, own[1:])]
    w_o = wg_out.reshape(d, d)
    w_g, w_u = wg_gate, wg_up
    w_d = wg_down.reshape(f, d)
    q_a, k_a, v_a = _conv_fwd(proj, conv_full, name="conv_fwd")
    bg = _bg_fwd(projs, alog_row, dtb_row, name="bg_fwd")
    beta_r = _tokens_to_rows(bg[:, :HEADS])
    g_r = _tokens_to_rows(bg[:, HEADS:2 * HEADS])
    gc_r, tinv, u_a, w_a, p_a = _gdn_prep(q_a, k_a, v_a, beta_r, g_r, name="gdn_prep")
    o_a, s_a = _gdn_scan(q_a, k_a, u_a, w_a, p_a, gc_r, name="gdn_scan")
    o_b, s_b = _ret_fwd(proj, cos2, sin2, lg_tab, o_rq, o_rk, o_rv, name="ret_fwd")
    mixed = _merge_fwd(o_a, o_b, proj, wa_row, ret_norm_w, o_az, o_rg, o_ga, o_gb, name="merge_fwd")
    h1 = _mm(mixed, w_o, res=xs, tm=512, tn=d, tk=d, name="mm_out")
    hn = _rms_fwd(h1, norm2_w, name="rms2_fwd")
    gt, up, act = _ffn_in(hn, w_g, w_u, tm=256, name="ffn_in")
    h2 = _mm(act, w_d, res=h1, tm=512, tn=d, tk=fq, name="mm_down")
    loss_row, dh2, dh2b, d_nf = _loss_head(h2, tgt, norm_f_w.reshape(1, d), name="loss_head")

    g_down = _mm(act, dh2b, ta=True, out_dtype=BF16, tm=fq, tn=d, tk=1024, name="mm_dw_down")
    dgt, dup = _ffn_back(dh2b, w_d, gt, up, tm=512, tn=fq, name="ffn_back")
    dhn = _mm(dgt, w_g, tb=True, pair=(dup, w_u), tm=512, tn=d, tk=fq, name="mm_dhn")
    g_gate = _mm(hn, dgt, ta=True, out_dtype=BF16, out_stacked=True, tm=1024, tn=fq, tk=2048, name="mm_dw_gate")
    g_up = _mm(hn, dup, ta=True, out_dtype=BF16, out_stacked=True, tm=1024, tn=fq, tk=2048, name="mm_dw_up")
    dh1, dh1b, d_n2 = _rms_bwd(dhn, h1, norm2_w, dh2, name="rms2_bwd")
    dmix = _mm(dh1b, w_o, tb=True, tm=512, tn=d, tk=d, name="mm_dmix")
    g_out = _mm(mixed, dh1b, ta=True, out_dtype=BF16, tm=1024, tn=d, tk=1024, name="mm_dw_out")
    halves = lambda g: g.reshape(N_CHIPS, 2, g.shape[1] // 2, g.shape[2])
    c_arr = jnp.reshape(c_idx, (1,)).astype(jnp.int32)
    gs_ffn = [halves(g_out.reshape(N_CHIPS, d // N_CHIPS, d)), halves(g_gate), halves(g_up),
              halves(g_down.reshape(N_CHIPS, fq, d))]
    rsib = _pair_exchange(gs_ffn, name="grad_pair_exchange_ffn")
    ps_ffn = [_pair_add(g, r, c_arr, name=f"grad_pair_add_{nm}")
              for g, r, nm in zip(gs_ffn, rsib, ["w_out", "w_gate", "w_up", "w_down"])]
    do_a, do_b, dproj, d_wa, d_wb = _merge_bwd(
        dmix, o_a, o_b, proj, wa_row, ret_norm_w, o_az, o_rg, o_ga, o_gb, name="merge_bwd")
    dproj = _ret_bwd(proj, cos2, sin2, lg_tab, s_b, do_b, dproj, o_rq, o_rk, o_rv, name="ret_bwd")
    dq_a, dk_a, dv_a, dbeta_r, dg_r = _gdn_bwd(q_a, k_a, v_a, beta_r, gc_r, tinv, u_a, w_a, s_a, do_a, name="gdn_bwd")
    dc = _conv_bwd_pre(proj, conv_full, dq_a, dk_a, dv_a, name="conv_bwd_pre")
    dproj, d_cw = _conv_bwd(proj, dc, conv_full, dproj, name="conv_bwd")
    dbg = jnp.pad(jnp.concatenate([_rows_to_tokens(dbeta_r), _rows_to_tokens(dg_r)], axis=1),
                  ((0, 0), (0, LANES - 2 * HEADS)))
    dprojs, d_alog, d_dtb = _bg_bwd(projs, dbg, alog_row, dtb_row, name="bg_bwd")
    g_main, *qs_ffn = _mm(u1, dproj, ta=True, out_dtype=BF16, tm=1024, tn=2048, tk=2048,
                          side=_scatter_side(ps_ffn), name="mm_dw_in")
    g_small = _mm(u1, dprojs, ta=True, out_dtype=BF16, tm=1024, tn=LANES, tk=1024, name="mm_dw_in_small")
    pieces = []
    for s in range(N_CHIPS):
        seen, parts = 0, []
        for a, b, dst in shard_pieces(s):
            parts.append(g_small[:, seen:seen + b - a] if dst is None else g_main[:, dst:dst + b - a])
            seen += (b - a) if dst is None else 0
        pieces.append(jnp.concatenate(parts, axis=1))
    gs_in = [halves(jnp.stack(pieces))]
    rsib = _pair_exchange(gs_in, name="grad_pair_exchange_in")
    ps_in = [_pair_add(gs_in[0], rsib[0], c_arr, name="grad_pair_add_w_in")]
    du, *qs_in = _mm(dproj, w_main, tb=True, out_dtype=BF16, tm=1024, tn=d, tk=2048,
                     side=_scatter_side(ps_in), name="mm_du")
    dx, _, d_n1 = _rms_bwd(du, xs, norm1_w, dh1, narrow=(dprojs, w_small), name="rms1_bwd")

    names = ["w_in", "w_out", "w_gate", "w_up", "w_down"]
    qs = [lax.dynamic_update_slice(q, lax.dynamic_slice(p, (s_idx, 0, 0), (1,) + p.shape[1:]), (s_idx, 0, 0))
          for q, p in zip(qs_in + qs_ffn, ps_in + ps_ffn)]
    hs = [_chip_sum(q, name=f"grad_chip_sum_{nm}") for q, nm in zip(qs, names)]
    theirs = _pair_share(hs, name="grad_pair_share")
    big_w = [w_in[0], w_out[0], w_gate[0], w_up[0], w_down[0]]
    big_m = [m_w_in[0], m_w_out[0], m_w_gate[0], m_w_up[0], m_w_down[0]]
    big_v = [v_w_in[0], v_w_out[0], v_w_gate[0], v_w_up[0], v_w_down[0]]
    big = {}
    for nm, mine, other, w_, m_, v_ in zip(names, hs, theirs, big_w, big_m, big_v):
        big[nm] = tuple(a[None] for a in _adamw_halves(w_, mine, other, c_arr, m_, v_, name=f"adamw_{nm}"))

    d_wa_h = jnp.sum(d_wa.reshape(HEADS, DV), axis=0, keepdims=True)
    small = [d_n1, d_alog[:, HEADS:2 * HEADS], d_dtb[:, HEADS:2 * HEADS], d_wa_h, d_wb, d_n2, d_nf,
             d_cw[:CONV_W].reshape(1, -1)]
    sizes = [a.shape[1] for a in small]
    packed = jnp.concatenate(small, axis=1)
    n_pack = packed.shape[1]
    n_rows = -(-n_pack // LANES)
    n_rows = -(-n_rows // 8) * 8
    packed = jnp.pad(packed, ((0, 0), (0, n_rows * LANES - n_pack))).reshape(n_rows, LANES)
    red = _allreduce_small(packed, name="allreduce_small").reshape(1, -1)
    offs = np.cumsum([0] + sizes)
    g_n1, g_alog, g_dtb, g_wa, g_wb, g_n2, g_nf, g_cw = [red[:, offs[i]:offs[i + 1]] for i in range(len(sizes))]
    ncw = conv_w.shape[2]
    g_cw = lax.dynamic_slice(g_cw.reshape(CONV_W, -1), (0, s_idx * ncw), (CONV_W, ncw))

    def small_update(w_, g, m_, v_, nm):
        shape = w_.shape
        pad = (-w_.size) % LANES
        to2 = lambda a: jnp.pad(a.reshape(1, -1), ((0, 0), (0, pad)))
        outs = _adamw(to2(w_), to2(g), to2(m_), to2(v_), name=f"adamw_{nm}")
        return (g.reshape(shape),) + tuple(a[:, :w_.size].reshape(shape) for a in outs)

    res = {
        "norm1_w": small_update(norm1_w, g_n1, m_norm1_w, v_norm1_w, "norm1_w"),
        "w_in": big["w_in"],
        "conv_w": small_update(conv_w, g_cw, m_conv_w, v_conv_w, "conv_w"),
        "a_log": small_update(a_log, g_alog, m_a_log, v_a_log, "a_log"),
        "dt_bias": small_update(dt_bias, g_dtb, m_dt_bias, v_dt_bias, "dt_bias"),
        "gdn_norm_w": small_update(gdn_norm_w, g_wa, m_gdn_norm_w, v_gdn_norm_w, "gdn_norm_w"),
        "ret_norm_w": small_update(ret_norm_w, g_wb, m_ret_norm_w, v_ret_norm_w, "ret_norm_w"),
        "w_out": big["w_out"],
        "norm2_w": small_update(norm2_w, g_n2, m_norm2_w, v_norm2_w, "norm2_w"),
        "w_gate": big["w_gate"],
        "w_up": big["w_up"],
        "w_down": big["w_down"],
        "norm_f_w": small_update(norm_f_w, g_nf, m_norm_f_w, v_norm_f_w, "norm_f_w"),
    }
    order = ["norm1_w", "w_in", "conv_w", "a_log", "dt_bias", "gdn_norm_w", "ret_norm_w", "w_out", "norm2_w",
             "w_gate", "w_up", "w_down", "norm_f_w"]
    loss = lax.psum(loss_row[0, 0], ("x", "y", "c"))
    return (loss, dx[None], *[res[n][0] for n in order], *[res[n][1] for n in order],
            *[res[n][2] for n in order], *[res[n][3] for n in order])
```

```python
import functools

import jax
import jax.numpy as jnp
import numpy as np
from jax import lax
from jax.experimental import pallas as pl
from jax.experimental.pallas import tpu as pltpu

F32 = jnp.float32
BF16 = jnp.bfloat16
MESH = pl.DeviceIdType.MESH

HEADS = 8
DK = 128
DV = 256
CHUNK = 64
CONV_W = 4
EPS = 1e-6
ROPE_BASE = 10000.0
ADAM_LR, ADAM_B1, ADAM_B2, ADAM_EPS, ADAM_WD, ADAM_STEP = 0.001, 0.9, 0.999, 1e-08, 0.01, 10
N_CHIPS = 4
N_DEV = 8
LANES = 128
HALO = 8
VMEM_LIMIT = 56 * 1024 * 1024
HB = 8


def _pick(n, cands):
    for c in cands:
        if n % c == 0:
            return c
    raise ValueError(f"no tile for {n} in {cands}")


def _params(sem=None):
    return pltpu.CompilerParams(dimension_semantics=sem, vmem_limit_bytes=VMEM_LIMIT)


def _dot(a, b):
    return jnp.dot(a.astype(BF16), b.astype(BF16), preferred_element_type=F32)


def _dot_nt(a, b):
    return lax.dot_general(a.astype(BF16), b.astype(BF16), (((1,), (1,)), ((), ())), preferred_element_type=F32)


def _dot_tn(a, b):
    return lax.dot_general(a.astype(BF16), b.astype(BF16), (((0,), (0,)), ((), ())), preferred_element_type=F32)


def _sigmoid(x):
    return 1.0 / (1.0 + jnp.exp(-x))


def _iota2(n):
    return lax.broadcasted_iota(jnp.int32, (n, n), 0), lax.broadcasted_iota(jnp.int32, (n, n), 1)


def _row_to_col(row):
    n = row.shape[1]
    r, c = _iota2(n)
    return jnp.sum(jnp.where(r == c, jnp.broadcast_to(row, (n, n)), 0.0), axis=1, keepdims=True)


def _col_to_row(col):
    n = col.shape[0]
    r, c = _iota2(n)
    return jnp.sum(jnp.where(r == c, jnp.broadcast_to(col, (n, n)), 0.0), axis=0, keepdims=True)


class _Side:
    def __init__(self, ins, out_shapes, n_sem, phases):
        self.ins, self.out_shapes, self.n_sem, self.phases = list(ins), list(out_shapes), n_sem, phases

    def sems(self):
        return [pltpu.SemaphoreType.DMA((self.n_sem,)), pltpu.SemaphoreType.DMA((self.n_sem,))]


def _mm(a, b, *, name, tm, tn, tk, ta=False, tb=False, out_dtype=F32, res=None, side=None, pair=None,
        out_stacked=False):
    m, k = (a.shape[1], a.shape[0]) if ta else a.shape
    b_slots = b.ndim == 3
    if b_slots:
        assert tb and tk == b.shape[2]
        n = b.shape[1]
    else:
        n = b.shape[0] if tb else b.shape[1]
    tm, tn, tk = min(tm, m), min(tn, n), min(tk, k)
    assert m % tm == 0 and n % tn == 0 and k % tk == 0, (name, m, n, k)
    nk = k // tk
    nj, ni = n // tn, m // tm
    dn = (((0 if ta else 1,), (1 if tb else 0,)), ((), ()))
    n_ab = 4 if pair else 2
    n_in = n_ab + (res is not None)
    n_side_in = len(side.ins) if side else 0
    n_side_out = len(side.out_shapes) if side else 0

    def body(*refs):
        a_ref, b_ref = refs[0], refs[1]
        r_ref = refs[n_ab] if res is not None else None
        o_ref = refs[n_in + n_side_in]
        if side:
            s_in = refs[n_in:n_in + n_side_in]
            s_out = refs[n_in + n_side_in + 1:n_in + n_side_in + 1 + n_side_out]
            ssem, rsem = refs[-2], refs[-1]
            j_, i_, k_ = pl.program_id(0), pl.program_id(1), pl.program_id(2)
            n_mid = len(side.phases) - 2
            assert n_mid == 0 or nj >= 2 * n_mid
            when = [(j_ == 0) & (i_ == 0) & (k_ == 0)]
            when += [(j_ == nj // 2 + (p * (nj // 2)) // n_mid) & (i_ == 0) & (k_ == 0) for p in range(n_mid)]
            when.append((j_ == nj - 1) & (i_ == ni - 1) & (k_ == nk - 1))

        def run_phase(p):
            @pl.when(when[p])
            def _():
                side.phases[p](s_in, s_out, ssem, rsem)

        if side:
            for p in range(len(side.phases) - 1):
                run_phase(p)

        def finish(r):
            if res is not None:
                r = r + r_ref[...]
            o_ref[...] = r.astype(out_dtype)

        part = lax.dot_general(a_ref[...], b_ref[...], dn, preferred_element_type=F32)
        if pair:
            part = part + lax.dot_general(refs[2][...], refs[3][...], dn, preferred_element_type=F32)
        if nk == 1:
            finish(part)
        else:
            acc = refs[n_in + n_side_in + 1 + n_side_out]
            kk = pl.program_id(2)

            @pl.when(kk == 0)
            def _():
                acc[...] = part

            @pl.when((kk > 0) & (kk < nk - 1))
            def _():
                acc[...] += part

            @pl.when(kk == nk - 1)
            def _():
                finish(acc[...] + part)

        if side:
            run_phase(len(side.phases) - 1)

    a_spec = pl.BlockSpec((tk, tm), lambda j, i, kk: (kk, i)) if ta else pl.BlockSpec((tm, tk), lambda j, i, kk: (i, kk))
    b_spec = pl.BlockSpec((tn, tk), lambda j, i, kk: (j, kk)) if tb else pl.BlockSpec((tk, tn), lambda j, i, kk: (kk, j))
    if b_slots:
        b_spec = pl.BlockSpec((None, tn, tk), lambda j, i, kk: (kk, j, 0))
    o_spec = pl.BlockSpec((tm, tn), lambda j, i, kk: (i, j))
    o_shape = jax.ShapeDtypeStruct((m, n), out_dtype)
    if out_stacked:
        assert res is None
        o_spec = pl.BlockSpec((None, tm, tn), lambda j, i, kk: (j, i, 0))
        o_shape = jax.ShapeDtypeStruct((nj, m, tn), out_dtype)
    in_specs, args = [a_spec, b_spec], [a, b]
    if pair:
        assert pair[0].shape == a.shape and pair[1].shape == b.shape
        in_specs += [a_spec, b_spec]
        args += list(pair)
    if res is not None:
        in_specs.append(o_spec)
        args.append(res)
    out_specs, out_shape = o_spec, o_shape
    scratch = [pltpu.VMEM((tm, tn), F32)] if nk > 1 else []
    sem = ("parallel", "parallel", "arbitrary")
    if side:
        hbm = pl.BlockSpec(memory_space=pl.ANY)
        in_specs += [hbm] * n_side_in
        args += side.ins
        out_specs, out_shape = [o_spec] + [hbm] * n_side_out, [out_shape] + side.out_shapes
        scratch += side.sems()
        sem = ("arbitrary",) * 3
    return pl.pallas_call(
        body, name=name, grid=(nj, ni, nk), in_specs=in_specs, out_specs=out_specs, out_shape=out_shape,
        scratch_shapes=scratch, compiler_params=_params(sem),
    )(*args)


def _rms_fwd(x, w, *, name, w_narrow=None):
    t, d = x.shape
    tt = _pick(t, (512, 256))

    def body(*refs):
        x_ref, w_ref, o_ref = refs[0], refs[1], refs[-2 if w_narrow is not None else -1]
        xv = x_ref[...]
        r = lax.rsqrt(jnp.mean(xv * xv, axis=-1, keepdims=True) + EPS)
        u = (xv * r * w_ref[...]).astype(BF16)
        o_ref[...] = u
        if w_narrow is not None:
            refs[-1][...] = jnp.dot(u, refs[2][...], preferred_element_type=F32)

    blk = pl.BlockSpec((tt, d), lambda i: (i, 0))
    in_specs, args = [blk, pl.BlockSpec((1, d), lambda i: (0, 0))], [x, w]
    out_specs, out_shape = [blk], [jax.ShapeDtypeStruct((t, d), BF16)]
    if w_narrow is not None:
        in_specs.append(pl.BlockSpec(w_narrow.shape, lambda i: (0, 0)))
        args.append(w_narrow)
        out_specs.append(pl.BlockSpec((tt, LANES), lambda i: (i, 0)))
        out_shape.append(jax.ShapeDtypeStruct((t, LANES), F32))
    out = pl.pallas_call(
        body, name=name, grid=(t // tt,), in_specs=in_specs, out_specs=out_specs, out_shape=out_shape,
        compiler_params=_params(("parallel",)),
    )(*args)
    return out if w_narrow is not None else out[0]


def _rms_bwd(dn, x, w, dres, *, name, narrow=None):
    t, d = x.shape
    tt = _pick(t, (256,))

    def body(*refs):
        dn_ref, x_ref, w_ref, dres_ref = refs[:4]
        dx_ref, dxb_ref, dw_ref = refs[-3:]
        xv, g = x_ref[...], dn_ref[...].astype(F32)
        if narrow is not None:
            g = g + lax.dot_general(refs[4][...], refs[5][...], (((1,), (1,)), ((), ())), preferred_element_type=F32)
        r = lax.rsqrt(jnp.mean(xv * xv, axis=-1, keepdims=True) + EPS)
        xh = xv * r
        gw = g * w_ref[...]
        dx = dres_ref[...] + r * (gw - xh * jnp.mean(gw * xh, axis=-1, keepdims=True))
        dx_ref[...] = dx
        dxb_ref[...] = dx.astype(BF16)

        @pl.when(pl.program_id(0) == 0)
        def _():
            dw_ref[...] = jnp.zeros_like(dw_ref)

        dw_ref[...] += jnp.sum(g * xh, axis=0, keepdims=True)

    blk = pl.BlockSpec((tt, d), lambda i: (i, 0))
    row = pl.BlockSpec((1, d), lambda i: (0, 0))
    in_specs, args = [blk, blk, row, blk], [dn, x, w, dres]
    if narrow is not None:
        in_specs += [pl.BlockSpec((tt, LANES), lambda i: (i, 0)), pl.BlockSpec(narrow[1].shape, lambda i: (0, 0))]
        args += list(narrow)
    return pl.pallas_call(
        body, name=name, grid=(t // tt,), in_specs=in_specs, out_specs=[blk, blk, row],
        out_shape=[jax.ShapeDtypeStruct((t, d), F32), jax.ShapeDtypeStruct((t, d), BF16),
                   jax.ShapeDtypeStruct((1, d), F32)],
        compiler_params=_params(("arbitrary",)),
    )(*args)


def _loss_head(h2, tgt, wf, *, name):
    t, d = h2.shape
    tt = _pick(t, (256,))

    def body(x_ref, t_ref, w_ref, loss_ref, dx_ref, dxb_ref, dw_ref):
        xv = x_ref[...]
        r = lax.rsqrt(jnp.mean(xv * xv, axis=-1, keepdims=True) + EPS)
        xh = xv * r
        err = xh * w_ref[...] - t_ref[...]
        lpart = 0.5 * jnp.sum(jnp.mean(err * err, axis=-1, keepdims=True), axis=0, keepdims=True)
        dy = err * (1.0 / d)
        gw = dy * w_ref[...]
        dx = r * (gw - xh * jnp.mean(gw * xh, axis=-1, keepdims=True))
        dx_ref[...] = dx
        dxb_ref[...] = dx.astype(BF16)

        @pl.when(pl.program_id(0) == 0)
        def _():
            dw_ref[...] = jnp.zeros_like(dw_ref)
            loss_ref[...] = jnp.zeros_like(loss_ref)

        dw_ref[...] += jnp.sum(dy * xh, axis=0, keepdims=True)
        loss_ref[...] += jnp.broadcast_to(lpart, loss_ref.shape)

    blk = pl.BlockSpec((tt, d), lambda i: (i, 0))
    row = pl.BlockSpec((1, d), lambda i: (0, 0))
    lrow = pl.BlockSpec((1, LANES), lambda i: (0, 0))
    return pl.pallas_call(
        body, name=name, grid=(t // tt,),
        in_specs=[blk, blk, row], out_specs=[lrow, blk, blk, row],
        out_shape=[jax.ShapeDtypeStruct((1, LANES), F32), jax.ShapeDtypeStruct((t, d), F32),
                   jax.ShapeDtypeStruct((t, d), BF16), jax.ShapeDtypeStruct((1, d), F32)],
        compiler_params=_params(("arbitrary",)),
    )(h2, tgt, wf)


def _ffn_in(hn, w_g, w_u, *, name, tm):
    t, d = hn.shape
    tn = w_g.shape[2]
    f = w_g.shape[0] * tn
    tm = min(tm, t)

    def body(a_ref, g_ref, u_ref, gt_ref, up_ref, act_ref):
        a = a_ref[...]
        g = jnp.dot(a, g_ref[...], preferred_element_type=F32)
        u = jnp.dot(a, u_ref[...], preferred_element_type=F32)
        gt_ref[...] = g.astype(BF16)
        up_ref[...] = u.astype(BF16)
        act_ref[...] = (g * _sigmoid(g) * u).astype(BF16)

    wblk = pl.BlockSpec((None, d, tn), lambda j, i: (j, 0, 0))
    oblk = pl.BlockSpec((tm, tn), lambda j, i: (i, j))
    return pl.pallas_call(
        body, name=name, grid=(f // tn, t // tm),
        in_specs=[pl.BlockSpec((tm, d), lambda j, i: (i, 0)), wblk, wblk], out_specs=[oblk] * 3,
        out_shape=[jax.ShapeDtypeStruct((t, f), BF16)] * 3,
        compiler_params=_params(("parallel", "parallel")),
    )(hn, w_g, w_u)


def _ffn_back(dh2b, w_d, gt, up, *, name, tm, tn):
    t, d = dh2b.shape
    f = w_d.shape[0]
    tm = min(tm, t)

    def body(a_ref, w_ref, g_ref, u_ref, dg_ref, du_ref):
        da = lax.dot_general(a_ref[...], w_ref[...], (((1,), (1,)), ((), ())), preferred_element_type=F32)
        g = g_ref[...].astype(F32)
        sg = _sigmoid(g)
        dg_ref[...] = (da * u_ref[...].astype(F32) * sg * (1.0 + g * (1.0 - sg))).astype(BF16)
        du_ref[...] = (da * g * sg).astype(BF16)

    oblk = pl.BlockSpec((tm, tn), lambda j, i: (i, j))
    return pl.pallas_call(
        body, name=name, grid=(f // tn, t // tm),
        in_specs=[pl.BlockSpec((tm, d), lambda j, i: (i, 0)), pl.BlockSpec((tn, d), lambda j, i: (j, 0)), oblk, oblk],
        out_specs=[oblk] * 2,
        out_shape=[jax.ShapeDtypeStruct((t, f), BF16)] * 2,
        compiler_params=_params(("parallel", "parallel")),
    )(dh2b, w_d, gt, up)


def _conv_cols(xs, cw_ref, cb, tt):
    cols = slice(cb * LANES, (cb + 1) * LANES)
    base = HALO - (CONV_W - 1)
    acc = xs[base:base + tt, cols] * cw_ref[0:1, cols]
    for j in range(1, CONV_W):
        acc = acc + xs[base + j:base + j + tt, cols] * cw_ref[j:j + 1, cols]
    return acc


def _fill_halo(xs, x_ref, xp_ref, tt):
    first = pl.program_id(0) == 0
    xs[0:HALO, :] = jnp.where(first, 0.0, xp_ref[...])
    xs[HALO:HALO + tt, :] = x_ref[...]


def _conv_specs(tt, ch):
    cur = pl.BlockSpec((tt, ch), lambda i: (i, 0))
    prev = pl.BlockSpec((HALO, ch), lambda i: (jnp.maximum(i * (tt // HALO) - 1, 0), 0))
    return cur, prev


def _conv_fwd(proj, conv_w, *, name):
    t = proj.shape[0]
    ch = conv_w.shape[1]
    nqk = HEADS * DK
    tt = _pick(t, (256,))
    cur, prev = _conv_specs(tt, ch)

    def body(x_ref, xp_ref, cw_ref, q_ref, k_ref, v_ref, xs):
        _fill_halo(xs, x_ref, xp_ref, tt)
        for cb in range(ch // LANES):
            c = _conv_cols(xs, cw_ref, cb, tt)
            s = c * _sigmoid(c)
            if cb < 2 * HEADS:
                s = s * lax.rsqrt(jnp.sum(s * s, axis=-1, keepdims=True) + EPS)
                if cb < HEADS:
                    q_ref[:, cb * LANES:(cb + 1) * LANES] = s * (DK ** -0.5)
                else:
                    k_ref[:, (cb - HEADS) * LANES:(cb - HEADS + 1) * LANES] = s
            else:
                v_ref[:, (cb - 2 * HEADS) * LANES:(cb - 2 * HEADS + 1) * LANES] = s

    return pl.pallas_call(
        body, name=name, grid=(t // tt,),
        in_specs=[cur, prev, pl.BlockSpec((CONV_W, ch), lambda i: (0, 0))],
        out_specs=[pl.BlockSpec((tt, nqk), lambda i: (i, 0)), pl.BlockSpec((tt, nqk), lambda i: (i, 0)),
                   pl.BlockSpec((tt, ch - 2 * nqk), lambda i: (i, 0))],
        out_shape=[jax.ShapeDtypeStruct((t, nqk), F32), jax.ShapeDtypeStruct((t, nqk), F32),
                   jax.ShapeDtypeStruct((t, ch - 2 * nqk), F32)],
        scratch_shapes=[pltpu.VMEM((HALO + tt, ch), F32)],
        compiler_params=_params(("arbitrary",)),
    )(proj, proj, conv_w)


def _conv_bwd_pre(proj, conv_w, dq, dk, dv, *, name):
    t = proj.shape[0]
    ch = conv_w.shape[1]
    nqk = HEADS * DK
    tt = _pick(t, (256,))
    cur, prev = _conv_specs(tt, ch)

    head = 2 * HALO

    def body(x_ref, xp_ref, cw_ref, dq_ref, dk_ref, dv_ref, dc_ref, xs):
        first = pl.program_id(0) == 0
        xs[0:HALO, :] = jnp.where(first, 0.0, xp_ref[...])
        xs[HALO:HALO + head, :] = x_ref[0:head, :]
        xb = x_ref[...].astype(BF16)
        r, cc = _iota2(tt)
        moved = [None] + [jnp.dot((cc == r - sh).astype(BF16), xb, preferred_element_type=F32)
                          for sh in range(1, CONV_W)]
        for cb in range(ch // LANES):
            cols = slice(cb * LANES, (cb + 1) * LANES)
            c_head = c_main = None
            for j in range(CONV_W):
                sh = CONV_W - 1 - j
                w_j = cw_ref[j:j + 1, cols]
                x_m = x_ref[head:tt, cols] if sh == 0 else moved[sh][head:tt, cols]
                x_h = xs[HALO - sh:HALO - sh + head, cols]
                c_main = x_m * w_j if c_main is None else c_main + x_m * w_j
                c_head = x_h * w_j if c_head is None else c_head + x_h * w_j
            c = jnp.concatenate([c_head, c_main], axis=0)
            sg = _sigmoid(c)
            s = c * sg
            if cb < 2 * HEADS:
                if cb < HEADS:
                    d, scale = dq_ref[:, cb * LANES:(cb + 1) * LANES], DK ** -0.5
                else:
                    d, scale = dk_ref[:, (cb - HEADS) * LANES:(cb - HEADS + 1) * LANES], 1.0
                rinv = lax.rsqrt(jnp.sum(s * s, axis=-1, keepdims=True) + EPS)
                ds = scale * rinv * (d - s * (rinv * rinv) * jnp.sum(d * s, axis=-1, keepdims=True))
            else:
                ds = dv_ref[:, (cb - 2 * HEADS) * LANES:(cb - 2 * HEADS + 1) * LANES]
            dc_ref[:, cb * LANES:(cb + 1) * LANES] = ds * sg * (1.0 + c * (1.0 - sg))

    return pl.pallas_call(
        body, name=name, grid=(t // tt,),
        in_specs=[cur, prev, pl.BlockSpec((CONV_W, ch), lambda i: (0, 0)),
                  pl.BlockSpec((tt, nqk), lambda i: (i, 0)), pl.BlockSpec((tt, nqk), lambda i: (i, 0)),
                  pl.BlockSpec((tt, ch - 2 * nqk), lambda i: (i, 0))],
        out_specs=pl.BlockSpec((tt, ch), lambda i: (i, 0)),
        out_shape=jax.ShapeDtypeStruct((t, ch), F32),
        scratch_shapes=[pltpu.VMEM((HALO + head, ch), F32)],
        compiler_params=_params(("arbitrary",)),
    )(proj, proj, conv_w, dq, dk, dv)


def _conv_bwd(proj, dc, conv_w, dproj, *, name):
    t = proj.shape[0]
    ch = conv_w.shape[1]
    tt = _pick(t, (256,))
    nt = t // tt
    cur = pl.BlockSpec((tt, ch), lambda i: (i, 0))
    nxt = pl.BlockSpec((HALO, ch), lambda i: (jnp.minimum((i + 1) * (tt // HALO), t // HALO - 1), 0))

    tail = 2 * HALO
    main = tt - tail

    def body(x_ref, d_ref, dn_ref, cw_ref, alias_ref, dx_ref, dw_ref, ds):
        last = pl.program_id(0) == nt - 1
        ds[0:tail, :] = d_ref[main:tt, :]
        ds[tail:tail + HALO, :] = jnp.where(last, 0.0, dn_ref[...])

        @pl.when(pl.program_id(0) == 0)
        def _():
            dw_ref[...] = jnp.zeros_like(dw_ref)

        dcb = d_ref[...].astype(BF16)
        r, c = _iota2(tt)
        moved = [None] + [jnp.dot((c == r + sh).astype(BF16), dcb, preferred_element_type=F32)
                          for sh in range(1, CONV_W)]
        for cb in range(ch // LANES):
            cols = slice(cb * LANES, (cb + 1) * LANES)
            x_m, x_t = x_ref[0:main, cols], x_ref[main:tt, cols]
            acc_m = acc_t = None
            for j in range(CONV_W):
                sh = CONV_W - 1 - j
                w_j = cw_ref[j:j + 1, cols]
                dj_m = d_ref[0:main, cols] if sh == 0 else moved[sh][0:main, cols]
                dj_t = ds[sh:sh + tail, cols]
                acc_m = dj_m * w_j if acc_m is None else acc_m + dj_m * w_j
                acc_t = dj_t * w_j if acc_t is None else acc_t + dj_t * w_j
                dw_ref[j:j + 1, cols] += (jnp.sum(dj_m * x_m, axis=0, keepdims=True)
                                          + jnp.sum(dj_t * x_t, axis=0, keepdims=True))
            dx_ref[0:main, cols] = acc_m.astype(BF16)
            dx_ref[main:tt, cols] = acc_t.astype(BF16)

    return pl.pallas_call(
        body, name=name, grid=(nt,),
        in_specs=[cur, cur, nxt, pl.BlockSpec((CONV_W, ch), lambda i: (0, 0)), pl.BlockSpec(memory_space=pl.ANY)],
        out_specs=[pl.BlockSpec((tt, ch), lambda i: (i, 0)), pl.BlockSpec((HALO, ch), lambda i: (0, 0))],
        out_shape=[jax.ShapeDtypeStruct(dproj.shape, BF16), jax.ShapeDtypeStruct((HALO, ch), F32)],
        scratch_shapes=[pltpu.VMEM((tail + HALO, ch), F32)],
        input_output_aliases={4: 0},
        compiler_params=_params(("arbitrary",)),
    )(proj, dc, dc, conv_w, dproj)


def _bg_fwd(projs, alog_row, dtb_row, *, name):
    t = projs.shape[0]
    tt = _pick(t, (512, 256))

    def body(p_ref, al_ref, db_ref, o_ref):
        p = p_ref[...]
        lane = lax.broadcasted_iota(jnp.int32, p.shape, 1)
        z = p + db_ref[...]
        sp = jnp.maximum(z, 0.0) + jnp.log1p(jnp.exp(-jnp.abs(z)))
        g = -jnp.exp(al_ref[...]) * sp
        o_ref[...] = jnp.where(lane < HEADS, _sigmoid(p), jnp.where(lane < 2 * HEADS, g, 0.0))

    blk = pl.BlockSpec((tt, LANES), lambda i: (i, 0))
    row = pl.BlockSpec((1, LANES), lambda i: (0, 0))
    return pl.pallas_call(
        body, name=name, grid=(t // tt,), in_specs=[blk, row, row], out_specs=blk,
        out_shape=jax.ShapeDtypeStruct((t, LANES), F32), compiler_params=_params(("parallel",)),
    )(projs, alog_row, dtb_row)


def _bg_bwd(projs, dbg, alog_row, dtb_row, *, name):
    t = projs.shape[0]
    tt = _pick(t, (512, 256))

    def body(p_ref, d_ref, al_ref, db_ref, o_ref, dal_ref, ddb_ref):
        p, d = p_ref[...], d_ref[...]
        lane = lax.broadcasted_iota(jnp.int32, p.shape, 1)
        isg = (lane >= HEADS) & (lane < 2 * HEADS)
        be = _sigmoid(p)
        z = p + db_ref[...]
        sp = jnp.maximum(z, 0.0) + jnp.log1p(jnp.exp(-jnp.abs(z)))
        ea = jnp.exp(al_ref[...])
        d_aa = jnp.where(isg, d * (-ea) * _sigmoid(z), 0.0)
        o_ref[...] = jnp.where(lane < HEADS, d * be * (1.0 - be), d_aa).astype(BF16)

        @pl.when(pl.program_id(0) == 0)
        def _():
            dal_ref[...] = jnp.zeros_like(dal_ref)
            ddb_ref[...] = jnp.zeros_like(ddb_ref)

        dal_ref[...] += jnp.sum(jnp.where(isg, d * (-ea) * sp, 0.0), axis=0, keepdims=True)
        ddb_ref[...] += jnp.sum(d_aa, axis=0, keepdims=True)

    blk = pl.BlockSpec((tt, LANES), lambda i: (i, 0))
    row = pl.BlockSpec((1, LANES), lambda i: (0, 0))
    return pl.pallas_call(
        body, name=name, grid=(t // tt,), in_specs=[blk, blk, row, row], out_specs=[blk, row, row],
        out_shape=[jax.ShapeDtypeStruct((t, LANES), BF16), jax.ShapeDtypeStruct((1, LANES), F32),
                   jax.ShapeDtypeStruct((1, LANES), F32)],
        compiler_params=_params(("arbitrary",)),
    )(projs, dbg, alog_row, dtb_row)


def _hn_specs(n_chunks, rev=False):
    def nn(n):
        return n_chunks - 1 - n if rev else n
    tok = lambda w: pl.BlockSpec((CHUNK, HB * w), lambda g, n: (nn(n), g))
    per = lambda a, b: pl.BlockSpec((HB, None, a, b), lambda g, n: (g, nn(n), 0, 0))
    return tok, per


def _hcols(hh, w):
    return slice(hh * w, (hh + 1) * w)


def _decay(gc_col, gc_row):
    r, c = _iota2(CHUNK)
    d = jnp.exp(jnp.minimum(gc_col - gc_row, 0.0))
    return jnp.where(r >= c, d, 0.0), jnp.where(r > c, d, 0.0)


def _gdn_prep(q, k, v, beta_r, g_r, *, name):
    t = q.shape[0]
    n_chunks = t // CHUNK
    tok, per = _hn_specs(n_chunks)

    def body(q_ref, k_ref, v_ref, b_ref, g_ref, gc_ref, ti_ref, u_ref, w_ref, p_ref):
        r, c = _iota2(CHUNK)
        hs = range(HB)
        kk = [k_ref[:, _hcols(h, DK)] for h in hs]
        gc_col = [jnp.sum(jnp.where(c <= r, jnp.broadcast_to(g_ref[h], (CHUNK, CHUNK)), 0.0), axis=1, keepdims=True)
                  for h in hs]
        gc_row = [_col_to_row(gc_col[h]) for h in hs]
        beta_col = [_row_to_col(b_ref[h]) for h in hs]
        dec = [_decay(gc_col[h], gc_row[h]) for h in hs]
        kb = [kk[h] * beta_col[h] for h in hs]
        for h in hs:
            gc_ref[h] = gc_row[h]
            p_ref[h] = _dot_nt(q_ref[:, _hcols(h, DK)], kk[h]) * dec[h][0]
        pw = [_dot_nt(kb[h], kk[h]) * dec[h][1] for h in hs]
        y = [-pw[h] for h in hs]
        for _ in range(5):
            pw = [_dot(pw[h], pw[h]) for h in hs]
            yp = [_dot(y[h], pw[h]) for h in hs]
            y = [y[h] + pw[h] + yp[h] for h in hs]
        vb = [v_ref[:, _hcols(h, DV)] * beta_col[h] for h in hs]
        kbe = [kb[h] * jnp.exp(gc_col[h]) for h in hs]
        yv = [_dot(y[h], vb[h]) for h in hs]
        yk = [_dot(y[h], kbe[h]) for h in hs]
        for h in hs:
            ti_ref[h] = y[h]
            u_ref[:, _hcols(h, DV)] = vb[h] + yv[h]
            w_ref[:, _hcols(h, DK)] = kbe[h] + yk[h]

    return pl.pallas_call(
        body, name=name, grid=(HEADS // HB, n_chunks),
        in_specs=[tok(DK), tok(DK), tok(DV), per(1, CHUNK), per(1, CHUNK)],
        out_specs=[per(1, CHUNK), per(CHUNK, CHUNK), tok(DV), tok(DK), per(CHUNK, CHUNK)],
        out_shape=[jax.ShapeDtypeStruct((HEADS, n_chunks, 1, CHUNK), F32),
                   jax.ShapeDtypeStruct((HEADS, n_chunks, CHUNK, CHUNK), F32),
                   jax.ShapeDtypeStruct((t, HEADS * DV), F32), jax.ShapeDtypeStruct((t, HEADS * DK), F32),
                   jax.ShapeDtypeStruct((HEADS, n_chunks, CHUNK, CHUNK), F32)],
        compiler_params=_params(("parallel", "parallel")),
    )(q, k, v, beta_r, g_r)


def _gdn_scan(q, k, u, w, p, gc_r, *, name):
    t = q.shape[0]
    n_chunks = t // CHUNK
    tok, per = _hn_specs(n_chunks)

    def body(q_ref, k_ref, u_ref, w_ref, p_ref, gc_ref, o_ref, s_ref, st):
        @pl.when(pl.program_id(1) == 0)
        def _():
            st[...] = jnp.zeros_like(st)

        hs = range(HB)
        s = [st[h] for h in hs]
        gc_row = [gc_ref[h] for h in hs]
        gc_col = [_row_to_col(gc_row[h]) for h in hs]
        glast = [gc_row[h][:, CHUNK - 1:CHUNK] for h in hs]
        for h in hs:
            s_ref[h] = s[h]
        ws = [_dot(w_ref[:, _hcols(h, DK)], s[h]) for h in hs]
        qs = [_dot(q_ref[:, _hcols(h, DK)] * jnp.exp(gc_col[h]), s[h]) for h in hs]
        vn = [u_ref[:, _hcols(h, DV)] - ws[h] for h in hs]
        pv = [_dot(p_ref[h], vn[h]) for h in hs]
        kv = [_dot_tn(k_ref[:, _hcols(h, DK)] * jnp.exp(glast[h] - gc_col[h]), vn[h]) for h in hs]
        for h in hs:
            o_ref[:, _hcols(h, DV)] = qs[h] + pv[h]
            st[h] = s[h] * jnp.exp(glast[h]) + kv[h]

    return pl.pallas_call(
        body, name=name, grid=(HEADS // HB, n_chunks),
        in_specs=[tok(DK), tok(DK), tok(DV), tok(DK), per(CHUNK, CHUNK), per(1, CHUNK)],
        out_specs=[tok(DV), per(DK, DV)],
        out_shape=[jax.ShapeDtypeStruct((t, HEADS * DV), F32),
                   jax.ShapeDtypeStruct((HEADS, n_chunks, DK, DV), F32)],
        scratch_shapes=[pltpu.VMEM((HB, DK, DV), F32)],
        compiler_params=_params(("arbitrary", "arbitrary")),
    )(q, k, u, w, p, gc_r)


def _gdn_bwd(q, k, v, beta_r, gc_r, ti, u, w, s_all, do, *, name):
    t = q.shape[0]
    n_chunks = t // CHUNK
    tok, per = _hn_specs(n_chunks, rev=True)

    def body(q_ref, k_ref, v_ref, b_ref, gc_ref, ti_ref, u_ref, w_ref, s_ref, do_ref,
             dq_ref, dk_ref, dv_ref, db_ref, dg_ref, dst):
        @pl.when(pl.program_id(1) == 0)
        def _():
            dst[...] = jnp.zeros_like(dst)

        r, c = _iota2(CHUNK)
        rows = lax.broadcasted_iota(jnp.int32, (CHUNK, 1), 0)
        rsum = lambda a: jnp.sum(a, axis=1, keepdims=True)
        hs = range(HB)
        qq = [q_ref[:, _hcols(h, DK)] for h in hs]
        kk = [k_ref[:, _hcols(h, DK)] for h in hs]
        ww = [w_ref[:, _hcols(h, DK)] for h in hs]
        uu = [u_ref[:, _hcols(h, DV)] for h in hs]
        d_o = [do_ref[:, _hcols(h, DV)] for h in hs]
        s = [s_ref[h] for h in hs]
        d_s = [dst[h] for h in hs]
        gc_row = [gc_ref[h] for h in hs]
        gc_col = [_row_to_col(gc_row[h]) for h in hs]
        beta_col = [_row_to_col(b_ref[h]) for h in hs]
        eg = [jnp.exp(gc_col[h]) for h in hs]
        glast = [gc_row[h][:, CHUNK - 1:CHUNK] for h in hs]
        eglast = [jnp.exp(glast[h]) for h in hs]
        e2 = [jnp.exp(glast[h] - gc_col[h]) for h in hs]
        dec = [_decay(gc_col[h], gc_row[h]) for h in hs]
        kb = [kk[h] * beta_col[h] for h in hs]
        ke = [kk[h] * e2[h] for h in hs]
        qe = [qq[h] * eg[h] for h in hs]
        kkt = [_dot_nt(kb[h], kk[h]) for h in hs]
        qk = [_dot_nt(qq[h], kk[h]) for h in hs]
        ws = [_dot(ww[h], s[h]) for h in hs]
        a = [kkt[h] * dec[h][1] for h in hs]
        pp = [qk[h] * dec[h][0] for h in hs]
        vn = [uu[h] - ws[h] for h in hs]
        t1 = [_dot_tn(pp[h], d_o[h]) for h in hs]
        t2 = [_dot(ke[h], d_s[h]) for h in hs]
        dqe = [_dot_nt(d_o[h], s[h]) for h in hs]
        dke = [_dot_nt(vn[h], d_s[h]) for h in hs]
        dqk = [_dot_nt(d_o[h], vn[h]) * dec[h][0] for h in hs]
        dvn = [t1[h] + t2[h] for h in hs]
        t3 = [_dot_tn(qe[h], d_o[h]) for h in hs]
        t4 = [_dot_tn(ww[h], dvn[h]) for h in hs]
        dw = [-_dot_nt(dvn[h], s[h]) for h in hs]
        de2 = [rsum(dke[h] * ke[h]) for h in hs]
        dglast = [jnp.sum(de2[h], axis=0, keepdims=True)
                  + eglast[h] * jnp.sum(rsum(d_s[h] * s[h]), axis=0, keepdims=True) for h in hs]
        for h in hs:
            dst[h] = d_s[h] * eglast[h] + t3[h] - t4[h]
        yy = [ti_ref[h] for h in hs]
        t5 = [_dot_tn(yy[h], dvn[h]) for h in hs]
        t6 = [_dot_tn(yy[h], dw[h]) for h in hs]
        dvb = [dvn[h] + t5[h] for h in hs]
        dkbe = [dw[h] + t6[h] for h in hs]
        t7 = [_dot_nt(dvb[h], uu[h]) for h in hs]
        t8 = [_dot_nt(dkbe[h], ww[h]) for h in hs]
        d_a = [jnp.where(r > c, -(t7[h] + t8[h]), 0.0) for h in hs]
        dm = [d_a[h] * dec[h][1] for h in hs]
        t9 = [_dot(dm[h], kk[h]) for h in hs]
        t10 = [_dot(dqk[h], kk[h]) for h in hs]
        t11 = [_dot_tn(dqk[h], qq[h]) for h in hs]
        t12 = [_dot_tn(dm[h], kb[h]) for h in hs]
        for h in hs:
            dkb = t9[h] + dkbe[h] * eg[h]
            e_mat = d_a[h] * a[h] + dqk[h] * qk[h]
            dgc = (rsum(dqe[h] * qe[h]) - de2[h] + rsum(dkbe[h] * kb[h] * eg[h]) + rsum(e_mat)
                   - _row_to_col(jnp.sum(e_mat, axis=0, keepdims=True)))
            dgc = dgc + jnp.where(rows == CHUNK - 1, dglast[h], 0.0)
            dq_ref[:, _hcols(h, DK)] = dqe[h] * eg[h] + t10[h]
            dk_ref[:, _hcols(h, DK)] = t11[h] + dke[h] * e2[h] + t12[h] + dkb * beta_col[h]
            dv_ref[:, _hcols(h, DV)] = dvb[h] * beta_col[h]
            dbeta = rsum(dkb * kk[h]) + rsum(dvb[h] * v_ref[:, _hcols(h, DV)])
            db_ref[h] = _col_to_row(dbeta)
            dg_ref[h] = jnp.sum(jnp.where(r >= c, jnp.broadcast_to(dgc, (CHUNK, CHUNK)), 0.0), axis=0, keepdims=True)

    return pl.pallas_call(
        body, name=name, grid=(HEADS // HB, n_chunks),
        in_specs=[tok(DK), tok(DK), tok(DV), per(1, CHUNK), per(1, CHUNK), per(CHUNK, CHUNK), tok(DV), tok(DK),
                  per(DK, DV), tok(DV)],
        out_specs=[tok(DK), tok(DK), tok(DV), per(1, CHUNK), per(1, CHUNK)],
        out_shape=[jax.ShapeDtypeStruct((t, HEADS * DK), F32), jax.ShapeDtypeStruct((t, HEADS * DK), F32),
                   jax.ShapeDtypeStruct((t, HEADS * DV), F32),
                   jax.ShapeDtypeStruct((HEADS, n_chunks, 1, CHUNK), F32),
                   jax.ShapeDtypeStruct((HEADS, n_chunks, 1, CHUNK), F32)],
        scratch_shapes=[pltpu.VMEM((HB, DK, DV), F32)],
        compiler_params=_params(("arbitrary", "arbitrary")),
    )(q, k, v, beta_r, gc_r, ti, u, w, s_all, do)


def _rot(x, cos2, sin2):
    return x * cos2 + pltpu.roll(x, DK // 2, 1) * sin2


def _unrot(d, cos2, sin2):
    return d * cos2 + pltpu.roll(d * sin2, DK // 2, 1)


def _ret_consts(lg):
    r, c = _iota2(CHUNK)
    dm = jnp.where(r >= c, jnp.exp((r - c).astype(F32) * lg), 0.0)
    pos = lax.broadcasted_iota(jnp.int32, (CHUNK, 1), 0).astype(F32)
    return dm, jnp.exp((pos + 1.0) * lg), jnp.exp((CHUNK - 1.0 - pos) * lg), jnp.exp(CHUNK * lg)


def _ret_fwd(proj, cos2, sin2, lg_tab, col_q, col_k, col_v, *, name):
    t = proj.shape[0]
    n_chunks = t // CHUNK
    tok, per = _hn_specs(n_chunks)
    bq, bk, bv = col_q // (HB * DK), col_k // (HB * DK), col_v // (HB * DV)

    def body(q_ref, k_ref, v_ref, cos_ref, sin_ref, lg_ref, o_ref, s_ref, st):
        @pl.when(pl.program_id(1) == 0)
        def _():
            st[...] = jnp.zeros_like(st)

        cos2v, sin2v = cos_ref[...], sin_ref[...]
        hs = range(HB)
        s = [st[h] for h in hs]
        for h in hs:
            s_ref[h] = s[h]
        cst = [_ret_consts(lg_ref[h][:, 0:1]) for h in hs]
        qq = [_rot(q_ref[:, _hcols(h, DK)], cos2v, sin2v) for h in hs]
        kk = [_rot(k_ref[:, _hcols(h, DK)], cos2v, sin2v) * (DK ** -0.5) for h in hs]
        vv = [v_ref[:, _hcols(h, DV)] for h in hs]
        sc = [_dot_nt(qq[h], kk[h]) * cst[h][0] for h in hs]
        qs = [_dot(qq[h], s[h]) for h in hs]
        kv = [_dot_tn(kk[h] * cst[h][2], vv[h]) for h in hs]
        sv = [_dot(sc[h], vv[h]) for h in hs]
        for h in hs:
            o_ref[:, _hcols(h, DV)] = sv[h] + qs[h] * cst[h][1]
            st[h] = s[h] * cst[h][3] + kv[h]

    return pl.pallas_call(
        body, name=name, grid=(HEADS // HB, n_chunks),
        in_specs=[pl.BlockSpec((CHUNK, HB * DK), lambda g, n: (n, bq + g)),
                  pl.BlockSpec((CHUNK, HB * DK), lambda g, n: (n, bk + g)),
                  pl.BlockSpec((CHUNK, HB * DV), lambda g, n: (n, bv + g)),
                  pl.BlockSpec((CHUNK, DK), lambda g, n: (n, 0)), pl.BlockSpec((CHUNK, DK), lambda g, n: (n, 0)),
                  pl.BlockSpec((HB, 1, LANES), lambda g, n: (g, 0, 0))],
        out_specs=[tok(DV), per(DK, DV)],
        out_shape=[jax.ShapeDtypeStruct((t, HEADS * DV), F32),
                   jax.ShapeDtypeStruct((HEADS, n_chunks, DK, DV), F32)],
        scratch_shapes=[pltpu.VMEM((HB, DK, DV), F32)],
        compiler_params=_params(("arbitrary", "arbitrary")),
    )(proj, proj, proj, cos2, sin2, lg_tab)


def _ret_bwd(proj, cos2, sin2, lg_tab, s_all, do, dproj, col_q, col_k, col_v, *, name):
    t = proj.shape[0]
    n_chunks = t // CHUNK
    tok, per = _hn_specs(n_chunks, rev=True)
    bq, bk, bv = col_q // (HB * DK), col_k // (HB * DK), col_v // (HB * DV)
    rv = lambda n: n_chunks - 1 - n
    nqk, wid = HEADS * DK, HEADS * (2 * DK + DV)
    assert HB == HEADS and col_k == col_q + nqk and col_v == col_k + nqk and col_q % wid == 0

    def body(q_ref, k_ref, v_ref, cos_ref, sin_ref, lg_ref, s_ref, do_ref, alias_ref, d_ref, dst):
        @pl.when(pl.program_id(1) == 0)
        def _():
            dst[...] = jnp.zeros_like(dst)

        cos2v, sin2v = cos_ref[...], sin_ref[...]
        hs = range(HB)
        s = [s_ref[h] for h in hs]
        d_s = [dst[h] for h in hs]
        d_o = [do_ref[:, _hcols(h, DV)] for h in hs]
        cst = [_ret_consts(lg_ref[h][:, 0:1]) for h in hs]
        qq = [_rot(q_ref[:, _hcols(h, DK)], cos2v, sin2v) for h in hs]
        kk = [_rot(k_ref[:, _hcols(h, DK)], cos2v, sin2v) * (DK ** -0.5) for h in hs]
        vv = [v_ref[:, _hcols(h, DV)] for h in hs]
        dxo = [d_o[h] * cst[h][1] for h in hs]
        sc = [_dot_nt(qq[h], kk[h]) * cst[h][0] for h in hs]
        dsc = [_dot_nt(d_o[h], vv[h]) * cst[h][0] for h in hs]
        t1 = [_dot(kk[h] * cst[h][2], d_s[h]) for h in hs]
        t2 = [_dot_nt(dxo[h], s[h]) for h in hs]
        t3 = [_dot_nt(vv[h], d_s[h]) for h in hs]
        t4 = [_dot_tn(qq[h], dxo[h]) for h in hs]
        t5 = [_dot_tn(sc[h], d_o[h]) for h in hs]
        t6 = [_dot(dsc[h], kk[h]) for h in hs]
        t7 = [_dot_tn(dsc[h], qq[h]) for h in hs]
        for h in hs:
            dst[h] = d_s[h] * cst[h][3] + t4[h]
            d_ref[:, 2 * nqk + h * DV:2 * nqk + (h + 1) * DV] = (t5[h] + t1[h]).astype(BF16)
            d_ref[:, _hcols(h, DK)] = _unrot(t6[h] + t2[h], cos2v, sin2v).astype(BF16)
            d_ref[:, nqk + h * DK:nqk + (h + 1) * DK] = (
                _unrot(t7[h] + t3[h] * cst[h][2], cos2v, sin2v) * (DK ** -0.5)).astype(BF16)

    return pl.pallas_call(
        body, name=name, grid=(HEADS // HB, n_chunks),
        in_specs=[pl.BlockSpec((CHUNK, HB * DK), lambda g, n: (rv(n), bq + g)),
                  pl.BlockSpec((CHUNK, HB * DK), lambda g, n: (rv(n), bk + g)),
                  pl.BlockSpec((CHUNK, HB * DV), lambda g, n: (rv(n), bv + g)),
                  pl.BlockSpec((CHUNK, DK), lambda g, n: (rv(n), 0)),
                  pl.BlockSpec((CHUNK, DK), lambda g, n: (rv(n), 0)),
                  pl.BlockSpec((HB, 1, LANES), lambda g, n: (g, 0, 0)),
                  per(DK, DV), tok(DV), pl.BlockSpec(memory_space=pl.ANY)],
        out_specs=pl.BlockSpec((CHUNK, wid), lambda g, n: (rv(n), col_q // wid)),
        out_shape=jax.ShapeDtypeStruct(dproj.shape, BF16),
        scratch_shapes=[pltpu.VMEM((HB, DK, DV), F32)],
        input_output_aliases={8: 0},
        compiler_params=_params(("arbitrary", "arbitrary")),
    )(proj, proj, proj, cos2, sin2, lg_tab, s_all, do, dproj)


def _merge_parts(oa, ob, z, rg, ga, gb):
    ra = lax.rsqrt(jnp.mean(oa * oa, axis=-1, keepdims=True) + EPS)
    xa = oa * ra
    mu = jnp.mean(ob, axis=-1, keepdims=True)
    cen = ob - mu
    rb = lax.rsqrt(jnp.mean(cen * cen, axis=-1, keepdims=True) + EPS)
    xb = cen * rb
    sz, sr = _sigmoid(z), _sigmoid(rg)
    return ra, xa, rb, xb, sz, sr, _sigmoid(ga), _sigmoid(gb)


def _merge_specs(tt, d, cz, crg, cga, cgb):
    blk = pl.BlockSpec((tt, d), lambda i: (i, 0))
    pcol = lambda c: pl.BlockSpec((tt, d), lambda i: (i, c // d))
    row = pl.BlockSpec((1, d), lambda i: (0, 0))
    return blk, [blk, blk, pcol(cz), pcol(crg), pcol(cga), pcol(cgb), row, row], row


def _merge_fwd(oa, ob, proj, wa, wb, cz, crg, cga, cgb, *, name):
    t, d = oa.shape
    tt = _pick(t, (256,))
    blk, in_specs, _ = _merge_specs(tt, d, cz, crg, cga, cgb)

    def body(oa_ref, ob_ref, z_ref, rg_ref, ga_ref, gb_ref, wa_ref, wb_ref, o_ref):
        for h in range(HEADS):
            cols = slice(h * DV, (h + 1) * DV)
            z, rg = z_ref[:, cols], rg_ref[:, cols]
            _, xa, _, xb, sz, sr, sga, sgb = _merge_parts(
                oa_ref[:, cols], ob_ref[:, cols], z, rg, ga_ref[:, cols], gb_ref[:, cols])
            o_a = xa * wa_ref[:, cols] * (z * sz)
            o_b = xb * wb_ref[:, cols] * (rg * sr)
            o_ref[:, cols] = (sga * o_a + sgb * o_b).astype(BF16)

    return pl.pallas_call(
        body, name=name, grid=(t // tt,), in_specs=in_specs, out_specs=blk,
        out_shape=jax.ShapeDtypeStruct((t, d), BF16), compiler_params=_params(("parallel",)),
    )(oa, ob, proj, proj, proj, proj, wa, wb)


def _merge_bwd(dmix, oa, ob, proj, wa, wb, cz, crg, cga, cgb, *, name):
    t, d = oa.shape
    tt = _pick(t, (256,))
    blk, in_specs, row = _merge_specs(tt, d, cz, crg, cga, cgb)
    assert (crg, cga, cgb) == (cz + d, cz + 2 * d, cz + 3 * d) and cz % (4 * d) == 0

    def body(dm_ref, oa_ref, ob_ref, z_ref, rg_ref, ga_ref, gb_ref, wa_ref, wb_ref,
             doa_ref, dob_ref, dp_ref, dwa_ref, dwb_ref):
        dz_ref, drg_ref, dga_ref, dgb_ref = [dp_ref.at[:, i * d:(i + 1) * d] for i in range(4)]

        @pl.when(pl.program_id(0) == 0)
        def _():
            dwa_ref[...] = jnp.zeros_like(dwa_ref)
            dwb_ref[...] = jnp.zeros_like(dwb_ref)

        for h in range(HEADS):
            cols = slice(h * DV, (h + 1) * DV)
            z, rg, wa_h, wb_h, dmx = z_ref[:, cols], rg_ref[:, cols], wa_ref[:, cols], wb_ref[:, cols], dm_ref[:, cols]
            ra, xa, rb, xb, sz, sr, sga, sgb = _merge_parts(
                oa_ref[:, cols], ob_ref[:, cols], z, rg, ga_ref[:, cols], gb_ref[:, cols])
            na, nb = xa * wa_h, xb * wb_h
            sil_z, sil_r = z * sz, rg * sr
            o_a, o_b = na * sil_z, nb * sil_r
            dga_ref[:, cols] = (dmx * o_a * sga * (1.0 - sga)).astype(BF16)
            dgb_ref[:, cols] = (dmx * o_b * sgb * (1.0 - sgb)).astype(BF16)
            d_oa, d_ob = dmx * sga, dmx * sgb
            dz_ref[:, cols] = (d_oa * na * sz * (1.0 + z * (1.0 - sz))).astype(BF16)
            drg_ref[:, cols] = (d_ob * nb * sr * (1.0 + rg * (1.0 - sr))).astype(BF16)
            dna, dnb = d_oa * sil_z, d_ob * sil_r
            dwa_ref[:, cols] += jnp.sum(dna * xa, axis=0, keepdims=True)
            dwb_ref[:, cols] += jnp.sum(dnb * xb, axis=0, keepdims=True)
            gwa, gwb = dna * wa_h, dnb * wb_h
            doa_ref[:, cols] = ra * (gwa - xa * jnp.mean(gwa * xa, axis=-1, keepdims=True))
            dob_ref[:, cols] = rb * (gwb - jnp.mean(gwb, axis=-1, keepdims=True)
                                     - xb * jnp.mean(gwb * xb, axis=-1, keepdims=True))

    return pl.pallas_call(
        body, name=name, grid=(t // tt,), in_specs=[blk] + in_specs,
        out_specs=[blk, blk, pl.BlockSpec((tt, 4 * d), lambda i: (i, cz // (4 * d))), row, row],
        out_shape=[jax.ShapeDtypeStruct((t, d), F32)] * 2 + [jax.ShapeDtypeStruct(proj.shape, BF16)]
        + [jax.ShapeDtypeStruct((1, d), F32)] * 2,
        compiler_params=_params(("arbitrary",)),
    )(dmix, oa, ob, proj, proj, proj, proj, wa, wb)


def _row_block(rows, cols, itemsize=4, target=1 << 20):
    for rb in (512, 256, 128, 64, 32, 16, 8):
        if rows % rb == 0 and rb * cols * itemsize <= target:
            return rb
    return rows


def _adamw(w, g, m, v, *, name):
    rows, cols = w.shape
    rb = _row_block(rows, cols)

    def body(w_ref, g_ref, m_ref, v_ref, d_ref, nm_ref, nv_ref):
        gg = g_ref[...]
        mm = ADAM_B1 * m_ref[...] + (1.0 - ADAM_B1) * gg
        vv = ADAM_B2 * v_ref[...] + (1.0 - ADAM_B2) * (gg * gg)
        m_hat = mm / (1.0 - ADAM_B1 ** ADAM_STEP)
        v_hat = vv / (1.0 - ADAM_B2 ** ADAM_STEP)
        d_ref[...] = -ADAM_LR * (m_hat / (jnp.sqrt(v_hat) + ADAM_EPS) + ADAM_WD * w_ref[...])
        nm_ref[...] = mm
        nv_ref[...] = vv

    blk = pl.BlockSpec((rb, cols), lambda i: (i, 0))
    return pl.pallas_call(
        body, name=name, grid=(rows // rb,), in_specs=[blk] * 4, out_specs=[blk] * 3,
        out_shape=[jax.ShapeDtypeStruct((rows, cols), F32)] * 3, compiler_params=_params(("parallel",)),
    )(w, g, m, v)


def _adamw_halves(w, mine, other, c_idx, m, v, *, name):
    rows, cols = w.shape
    hr = rows // 2
    rb = _row_block(hr, cols)
    nb = hr // rb

    def body(c_ref, w_ref, a_ref, b_ref, m_ref, v_ref, g_ref, d_ref, nm_ref, nv_ref):
        gg = jnp.where(pl.program_id(0) // nb == c_ref[0], a_ref[...], b_ref[...])
        mm = ADAM_B1 * m_ref[...] + (1.0 - ADAM_B1) * gg
        vv = ADAM_B2 * v_ref[...] + (1.0 - ADAM_B2) * (gg * gg)
        m_hat = mm / (1.0 - ADAM_B1 ** ADAM_STEP)
        v_hat = vv / (1.0 - ADAM_B2 ** ADAM_STEP)
        g_ref[...] = gg
        d_ref[...] = -ADAM_LR * (m_hat / (jnp.sqrt(v_hat) + ADAM_EPS) + ADAM_WD * w_ref[...])
        nm_ref[...] = mm
        nv_ref[...] = vv

    blk = pl.BlockSpec((rb, cols), lambda i, c: (i, 0))
    half = pl.BlockSpec((rb, cols), lambda i, c: (i % nb, 0))
    return pl.pallas_call(
        body, name=name,
        grid_spec=pltpu.PrefetchScalarGridSpec(
            num_scalar_prefetch=1, grid=(rows // rb,), in_specs=[blk, half, half, blk, blk], out_specs=[blk] * 4),
        out_shape=[jax.ShapeDtypeStruct((rows, cols), F32)] * 4, compiler_params=_params(("parallel",)),
    )(c_idx, w, mine, other, m, v)


def _pair_add(g, rsib, c_idx, *, name):
    _, _, hr, cols = g.shape
    rb = _row_block(hr, cols, 2)

    def body(c_ref, g_ref, r_ref, o_ref):
        o_ref[...] = (g_ref[...].astype(F32) + r_ref[...].astype(F32)).astype(BF16)

    return pl.pallas_call(
        body, name=name,
        grid_spec=pltpu.PrefetchScalarGridSpec(
            num_scalar_prefetch=1, grid=(N_CHIPS, hr // rb),
            in_specs=[pl.BlockSpec((None, None, rb, cols), lambda j, i, c: (j, c[0], i, 0)),
                      pl.BlockSpec((None, rb, cols), lambda j, i, c: (j, i, 0))],
            out_specs=pl.BlockSpec((None, rb, cols), lambda j, i, c: (j, i, 0))),
        out_shape=jax.ShapeDtypeStruct((N_CHIPS, hr, cols), BF16),
        compiler_params=_params(("parallel", "parallel")),
    )(c_idx, g, rsib)


def _chip_sum(qb, *, name):
    _, hr, cols = qb.shape
    rb = _row_block(hr, cols, 2, 1 << 19)

    def body(q_ref, o_ref):
        acc = q_ref[0].astype(F32)
        for i in range(1, N_CHIPS):
            acc = acc + q_ref[i].astype(F32)
        o_ref[...] = acc

    return pl.pallas_call(
        body, name=name, grid=(hr // rb,),
        in_specs=[pl.BlockSpec((N_CHIPS, rb, cols), lambda i: (0, i, 0))],
        out_specs=pl.BlockSpec((rb, cols), lambda i: (i, 0)),
        out_shape=jax.ShapeDtypeStruct((hr, cols), F32), compiler_params=_params(("parallel",)),
    )(qb)


def _place():
    x, y, c = lax.axis_index("x"), lax.axis_index("y"), lax.axis_index("c")
    return x, y, c, [(1 - x, y), (x, 1 - y), (1 - x, 1 - y)]


ANY = pl.BlockSpec(memory_space=pl.ANY)


def _gather_side(shards):
    nw, n_k = len(shards), 8

    def plan(ins, outs, ssem, rsem):
        x, y, c, _ = _place()
        nbr_x, nbr_y, sib = (1 - x, y, c), (x, 1 - y, c), (x, y, 1 - c)
        s_me, s_x, s_y, s_d = 2 * x + y, 2 * (1 - x) + y, 2 * x + 1 - y, 2 * (1 - x) + 1 - y

        def rows(w, half, quarter=None):
            hr = shards[w].shape[0] // 2
            if quarter is None:
                return pl.ds(pl.multiple_of(half * hr, 16), hr)
            return pl.ds(pl.multiple_of(half * hr + quarter * (hr // 2), 16), hr // 2)

        def rcopy(w, k, src, slot, rws, to):
            return pltpu.make_async_remote_copy(
                src_ref=src, dst_ref=outs[w].at[slot, rws], send_sem=ssem.at[w * n_k + k],
                recv_sem=rsem.at[w * n_k + k], device_id=to, device_id_type=MESH)

        out = []
        for w in range(nw):
            mine = ins[w].at[rows(w, c)]
            q = [rows(w, c, 0), rows(w, c, 1)]
            p = functools.partial
            out.append([
                (p(rcopy, w, 0, mine, s_me, rows(w, c), nbr_x), p(rcopy, w, 0, mine, s_x, rows(w, c), nbr_x)),
                (p(rcopy, w, 1, mine, s_me, rows(w, c), nbr_y), p(rcopy, w, 1, mine, s_y, rows(w, c), nbr_y)),
                (p(rcopy, w, 2, outs[w].at[s_x, q[0]], s_x, q[0], nbr_y),
                 p(rcopy, w, 2, outs[w].at[s_x, q[0]], s_d, q[0], nbr_y)),
                (p(rcopy, w, 3, outs[w].at[s_y, q[1]], s_y, q[1], nbr_x),
                 p(rcopy, w, 3, outs[w].at[s_y, q[1]], s_d, q[1], nbr_x)),
                (p(rcopy, w, 4, outs[w].at[s_x, rows(w, c)], s_x, rows(w, c), sib),
                 p(rcopy, w, 4, mine, s_x, rows(w, 1 - c), sib)),
                (p(rcopy, w, 5, outs[w].at[s_y, rows(w, c)], s_y, rows(w, c), sib),
                 p(rcopy, w, 5, mine, s_y, rows(w, 1 - c), sib)),
                (p(rcopy, w, 6, outs[w].at[s_d, q[0]], s_d, q[0], sib),
                 p(rcopy, w, 6, outs[w].at[s_d, q[0]], s_d, rows(w, 1 - c, 0), sib)),
                (p(rcopy, w, 7, outs[w].at[s_d, q[1]], s_d, q[1], sib),
                 p(rcopy, w, 7, outs[w].at[s_d, q[1]], s_d, rows(w, 1 - c, 1), sib)),
            ])
        return out

    def send_own(*refs):
        for cps in plan(*refs):
            cps[0][0]().start()
            cps[1][0]().start()

    def relay(*refs):
        for cps in plan(*refs):
            cps[0][1]().wait_recv()
            cps[2][0]().start()
            cps[4][0]().start()
            cps[1][1]().wait_recv()
            cps[3][0]().start()
            cps[5][0]().start()

    def pass_diagonal(*refs):
        for cps in plan(*refs):
            cps[2][1]().wait_recv()
            cps[6][0]().start()
            cps[3][1]().wait_recv()
            cps[7][0]().start()

    def finish(*refs):
        for cps in plan(*refs):
            for k in (4, 5, 6, 7):
                cps[k][1]().wait_recv()
            for k in range(n_k):
                cps[k][0]().wait_send()

    return _Side(shards, [jax.ShapeDtypeStruct((N_CHIPS,) + s.shape, s.dtype) for s in shards], nw * n_k,
                 [send_own, relay, pass_diagonal, finish])


def _run_side(side, *, name):
    n_in, n_out = len(side.ins), len(side.out_shapes)

    def body(*refs):
        ins, outs = refs[:n_in], refs[n_in:n_in + n_out]
        for phase in side.phases:
            phase(ins, outs, refs[-2], refs[-1])

    return pl.pallas_call(
        body, name=name, in_specs=[ANY] * n_in, out_specs=[ANY] * n_out, out_shape=side.out_shapes,
        scratch_shapes=side.sems(), compiler_params=pltpu.CompilerParams(has_side_effects=True),
    )(*side.ins)


def _pair_exchange(gs, *, name):
    nw = len(gs)

    def body(*refs):
        ins, outs = refs[:nw], refs[nw:2 * nw]
        ssem, rsem = refs[2 * nw:]
        x, y, c, _ = _place()
        cps = []
        for w in range(nw):
            cp = pltpu.make_async_remote_copy(
                src_ref=ins[w].at[:, 1 - c], dst_ref=outs[w], send_sem=ssem.at[w], recv_sem=rsem.at[w],
                device_id=(x, y, 1 - c), device_id_type=MESH)
            cp.start()
            cps.append(cp)
        for cp in cps:
            cp.wait()

    return pl.pallas_call(
        body, name=name, in_specs=[ANY] * nw, out_specs=[ANY] * nw,
        out_shape=[jax.ShapeDtypeStruct((g.shape[0],) + g.shape[2:], g.dtype) for g in gs],
        scratch_shapes=[pltpu.SemaphoreType.DMA((nw,)), pltpu.SemaphoreType.DMA((nw,))],
        compiler_params=pltpu.CompilerParams(has_side_effects=True),
    )(*gs)


def _scatter_side(ps):
    nw = len(ps)

    def copies(ins, outs, ssem, rsem):
        x, y, c, chips = _place()
        s_me = 2 * x + y
        return [pltpu.make_async_remote_copy(
            src_ref=ins[w].at[2 * chip[0] + chip[1]], dst_ref=outs[w].at[s_me],
            send_sem=ssem.at[w * 3 + j], recv_sem=rsem.at[w * 3 + j],
            device_id=(chip[0], chip[1], c), device_id_type=MESH)
            for w in range(nw) for j, chip in enumerate(chips)]

    def start(*refs):
        for cp in copies(*refs):
            cp.start()

    def finish(*refs):
        for cp in copies(*refs):
            cp.wait()

    return _Side(ps, [jax.ShapeDtypeStruct(p.shape, p.dtype) for p in ps], nw * 3, [start, finish])


def _pair_share(hs, *, name):
    nw = len(hs)

    def body(*refs):
        ins, outs = refs[:nw], refs[nw:2 * nw]
        ssem, rsem = refs[2 * nw:]
        x, y, c, _ = _place()
        cps = []
        for w in range(nw):
            cp = pltpu.make_async_remote_copy(
                src_ref=ins[w], dst_ref=outs[w], send_sem=ssem.at[w], recv_sem=rsem.at[w],
                device_id=(x, y, 1 - c), device_id_type=MESH)
            cp.start()
            cps.append(cp)
        for cp in cps:
            cp.wait()

    return pl.pallas_call(
        body, name=name, in_specs=[ANY] * nw, out_specs=[ANY] * nw,
        out_shape=[jax.ShapeDtypeStruct(h.shape, h.dtype) for h in hs],
        scratch_shapes=[pltpu.SemaphoreType.DMA((nw,)), pltpu.SemaphoreType.DMA((nw,))],
        compiler_params=pltpu.CompilerParams(has_side_effects=True),
    )(*hs)


def _gather_small(a, *, name):
    def body(a_ref, o_ref, ssem, rsem):
        x, y, c, chips = _place()
        s_me = 2 * x + y
        o_ref[s_me] = a_ref[...]
        cps = []
        for j, chip in enumerate(chips):
            cp = pltpu.make_async_remote_copy(
                src_ref=a_ref, dst_ref=o_ref.at[s_me], send_sem=ssem.at[j], recv_sem=rsem.at[j],
                device_id=(chip[0], chip[1], c), device_id_type=MESH)
            cp.start()
            cps.append(cp)
        for cp in cps:
            cp.wait()

    vm = pl.BlockSpec(memory_space=pltpu.VMEM)
    return pl.pallas_call(
        body, name=name, in_specs=[vm], out_specs=vm,
        out_shape=jax.ShapeDtypeStruct((N_CHIPS,) + a.shape, a.dtype),
        scratch_shapes=[pltpu.SemaphoreType.DMA((3,)), pltpu.SemaphoreType.DMA((3,))],
    )(a)


def _allreduce_small(p, *, name):
    def body(p_ref, o_ref, buf, ssem, rsem):
        x, y, c, _ = _place()
        me = 4 * x + 2 * y + c
        buf[me] = p_ref[...]
        cps = []
        for k in range(1, N_DEV):
            fx, fy, fc = (k >> 2) & 1, (k >> 1) & 1, k & 1
            peer = (x + fx - 2 * x * fx, y + fy - 2 * y * fy, c + fc - 2 * c * fc)
            cp = pltpu.make_async_remote_copy(
                src_ref=buf.at[me], dst_ref=buf.at[me], send_sem=ssem.at[k - 1], recv_sem=rsem.at[k - 1],
                device_id=peer, device_id_type=MESH)
            cp.start()
            cps.append(cp)
        for cp in cps:
            cp.wait()
        acc = buf[0]
        for i in range(1, N_DEV):
            acc = acc + buf[i]
        o_ref[...] = acc

    vm = pl.BlockSpec(memory_space=pltpu.VMEM)
    return pl.pallas_call(
        body, name=name, in_specs=[vm], out_specs=vm,
        out_shape=jax.ShapeDtypeStruct(p.shape, p.dtype),
        scratch_shapes=[pltpu.VMEM((N_DEV,) + p.shape, p.dtype), pltpu.SemaphoreType.DMA((N_DEV - 1,)),
                        pltpu.SemaphoreType.DMA((N_DEV - 1,))],
    )(p)


def _rows_to_tokens(r):
    h, n = r.shape[0], r.shape[1]
    return r.reshape(h, n * CHUNK).T


def _tokens_to_rows(a):
    t, h = a.shape
    return a.T.reshape(h, t // CHUNK, 1, CHUNK)


def kernel(x, norm1_w, w_in, conv_w, a_log, dt_bias, gdn_norm_w, ret_norm_w, w_out, norm2_w, w_gate, w_up, w_down, norm_f_w, loss_target, m_norm1_w, m_w_in, m_conv_w, m_a_log, m_dt_bias, m_gdn_norm_w, m_ret_norm_w, m_w_out, m_norm2_w, m_w_gate, m_w_up, m_w_down, m_norm_f_w, v_norm1_w, v_w_in, v_conv_w, v_a_log, v_dt_bias, v_gdn_norm_w, v_ret_norm_w, v_w_out, v_norm2_w, v_w_gate, v_w_up, v_w_down, v_norm_f_w):
    t, d = x.shape[1], x.shape[2]
    f = w_gate.shape[2] * N_CHIPS
    nqk, nv = HEADS * DK, HEADS * DV
    ncs = w_in.shape[2]
    c_idx = lax.axis_index("c")
    s_idx = 2 * lax.axis_index("x") + lax.axis_index("y")
    xs = x[0]
    tgt = loss_target[0]

    n_small = 2 * HEADS
    widths = [2 * nqk + nv, nv, n_small, nqk, nqk, nv, nv, d, d]
    g_off = np.concatenate([[0], np.cumsum(widths)])
    order = [0, 3, 4, 5, 1, 6, 7, 8]
    m_off = np.concatenate([[0], np.cumsum([widths[i] for i in order])])
    o_rq, o_rk, o_rv, o_az, o_rg, o_ga, o_gb = [int(m_off[i]) for i in range(1, 8)]
    segs = [(int(g_off[i]), int(g_off[i + 1]), int(m_off[order.index(i)]) if i != 2 else None) for i in range(9)]

    def shard_pieces(s):
        out = []
        for a, b, dst in segs:
            lo, hi = max(a, s * ncs), min(b, (s + 1) * ncs)
            if lo < hi:
                out.append((lo - s * ncs, hi - s * ncs, None if dst is None else dst + lo - a))
        return out

    own = [w_in[0].astype(BF16), w_out[0].astype(BF16), w_gate[0].astype(BF16), w_up[0].astype(BF16),
           w_down[0].astype(BF16)]
    put_own = lambda full, mine: lax.dynamic_update_slice(full, mine[None], (s_idx, 0, 0))
    wg_in = put_own(_run_side(_gather_side(own[:1]), name="gather_w_in")[0], own[0])
    conv_full = _gather_small(conv_w[0], name="gather_conv_w")
    conv_full = jnp.concatenate([conv_full[i] for i in range(N_CHIPS)], axis=1)
    cuts = [(dst, s, a, b) for s in range(N_CHIPS) for a, b, dst in shard_pieces(s)]
    w_main = jnp.concatenate([wg_in[s][:, a:b] for dst, s, a, b in sorted(c for c in cuts if c[0] is not None)],
                             axis=1)
    w_small = jnp.concatenate([wg_in[s][:, a:b] for dst, s, a, b in cuts if dst is None], axis=1)
    w_small = jnp.pad(w_small, ((0, 0), (0, LANES - n_small)))

    pad16 = lambda a: jnp.pad(a, ((0, 0), (HEADS, LANES - 2 * HEADS)))
    alog_row, dtb_row = pad16(a_log), pad16(dt_bias)
    wa_row = jnp.tile(gdn_norm_w, (1, HEADS))
    inv = ROPE_BASE ** (-jnp.arange(0, DK, 2, dtype=F32) / DK)
    ang = jnp.arange(t, dtype=F32)[:, None] * inv[None, :]
    cos2 = jnp.concatenate([jnp.cos(ang), jnp.cos(ang)], axis=1)
    sin2 = jnp.concatenate([-jnp.sin(ang), jnp.sin(ang)], axis=1)
    lg = jnp.log1p(-jnp.exp2(-5.0 - jnp.arange(HEADS, dtype=F32)))
    lg_tab = jnp.broadcast_to(lg[:, None, None], (HEADS, 1, LANES))

    fq = f // N_CHIPS
    u1, projs = _rms_fwd(xs, norm1_w, w_narrow=w_small, name="rms1_fwd")
    proj, *rest = _mm(u1, w_main, tm=512, tn=2048, tk=d, side=_gather_side(own[1:]), name="mm_proj")
    wg_out, wg_gate, wg_up, wg_down = [put_own(g, o) for g, o in zip(rest, own[1:])]
    w_o = wg_out.reshape(d, d)
    w_g, w_u = wg_gate, wg_up
    w_d = wg_down.reshape(f, d)
    q_a, k_a, v_a = _conv_fwd(proj, conv_full, name="conv_fwd")
    bg = _bg_fwd(projs, alog_row, dtb_row, name="bg_fwd")
    beta_r = _tokens_to_rows(bg[:, :HEADS])
    g_r = _tokens_to_rows(bg[:, HEADS:2 * HEADS])
    gc_r, tinv, u_a, w_a, p_a = _gdn_prep(q_a, k_a, v_a, beta_r, g_r, name="gdn_prep")
    o_a, s_a = _gdn_scan(q_a, k_a, u_a, w_a, p_a, gc_r, name="gdn_scan")
    o_b, s_b = _ret_fwd(proj, cos2, sin2, lg_tab, o_rq, o_rk, o_rv, name="ret_fwd")
    mixed = _merge_fwd(o_a, o_b, proj, wa_row, ret_norm_w, o_az, o_rg, o_ga, o_gb, name="merge_fwd")
    h1 = _mm(mixed, w_o, res=xs, tm=512, tn=d, tk=d, name="mm_out")
    hn = _rms_fwd(h1, norm2_w, name="rms2_fwd")
    gt, up, act = _ffn_in(hn, w_g, w_u, tm=256, name="ffn_in")
    h2 = _mm(act, w_d, res=h1, tm=512, tn=1024, tk=f, name="mm_down")
    loss_row, dh2, dh2b, d_nf = _loss_head(h2, tgt, norm_f_w.reshape(1, d), name="loss_head")

    g_down = _mm(act, dh2b, ta=True, out_dtype=BF16, tm=fq, tn=d, tk=1024, name="mm_dw_down")
    dgt, dup = _ffn_back(dh2b, w_d, gt, up, tm=256, tn=fq, name="ffn_back")
    dhn = _mm(dgt, w_g, tb=True, pair=(dup, w_u), tm=512, tn=d, tk=fq, name="mm_dhn")
    g_gate = _mm(hn, dgt, ta=True, out_dtype=BF16, out_stacked=True, tm=1024, tn=fq, tk=2048, name="mm_dw_gate")
    g_up = _mm(hn, dup, ta=True, out_dtype=BF16, out_stacked=True, tm=1024, tn=fq, tk=2048, name="mm_dw_up")
    dh1, dh1b, d_n2 = _rms_bwd(dhn, h1, norm2_w, dh2, name="rms2_bwd")
    dmix = _mm(dh1b, w_o, tb=True, tm=512, tn=d, tk=d, name="mm_dmix")
    g_out = _mm(mixed, dh1b, ta=True, out_dtype=BF16, tm=1024, tn=d, tk=1024, name="mm_dw_out")
    halves = lambda g: g.reshape(N_CHIPS, 2, g.shape[1] // 2, g.shape[2])
    c_arr = jnp.reshape(c_idx, (1,)).astype(jnp.int32)
    gs_ffn = [halves(g_out.reshape(N_CHIPS, d // N_CHIPS, d)), halves(g_gate), halves(g_up),
              halves(g_down.reshape(N_CHIPS, fq, d))]
    rsib = _pair_exchange(gs_ffn, name="grad_pair_exchange_ffn")
    ps_ffn = [_pair_add(g, r, c_arr, name=f"grad_pair_add_{nm}")
              for g, r, nm in zip(gs_ffn, rsib, ["w_out", "w_gate", "w_up", "w_down"])]
    do_a, do_b, dproj, d_wa, d_wb = _merge_bwd(
        dmix, o_a, o_b, proj, wa_row, ret_norm_w, o_az, o_rg, o_ga, o_gb, name="merge_bwd")
    dproj = _ret_bwd(proj, cos2, sin2, lg_tab, s_b, do_b, dproj, o_rq, o_rk, o_rv, name="ret_bwd")
    dq_a, dk_a, dv_a, dbeta_r, dg_r = _gdn_bwd(q_a, k_a, v_a, beta_r, gc_r, tinv, u_a, w_a, s_a, do_a, name="gdn_bwd")
    dc = _conv_bwd_pre(proj, conv_full, dq_a, dk_a, dv_a, name="conv_bwd_pre")
    dproj, d_cw = _conv_bwd(proj, dc, conv_full, dproj, name="conv_bwd")
    dbg = jnp.pad(jnp.concatenate([_rows_to_tokens(dbeta_r), _rows_to_tokens(dg_r)], axis=1),
                  ((0, 0), (0, LANES - 2 * HEADS)))
    dprojs, d_alog, d_dtb = _bg_bwd(projs, dbg, alog_row, dtb_row, name="bg_bwd")
    g_main, *qs_ffn = _mm(u1, dproj, ta=True, out_dtype=BF16, tm=1024, tn=2048, tk=2048,
                          side=_scatter_side(ps_ffn), name="mm_dw_in")
    g_small = _mm(u1, dprojs, ta=True, out_dtype=BF16, tm=1024, tn=LANES, tk=1024, name="mm_dw_in_small")
    pieces = []
    for s in range(N_CHIPS):
        seen, parts = 0, []
        for a, b, dst in shard_pieces(s):
            parts.append(g_small[:, seen:seen + b - a] if dst is None else g_main[:, dst:dst + b - a])
            seen += (b - a) if dst is None else 0
        pieces.append(jnp.concatenate(parts, axis=1))
    gs_in = [halves(jnp.stack(pieces))]
    rsib = _pair_exchange(gs_in, name="grad_pair_exchange_in")
    ps_in = [_pair_add(gs_in[0], rsib[0], c_arr, name="grad_pair_add_w_in")]
    du, *qs_in = _mm(dproj, w_main, tb=True, out_dtype=BF16, tm=1024, tn=d, tk=2048,
                     side=_scatter_side(ps_in), name="mm_du")
    dx, _, d_n1 = _rms_bwd(du, xs, norm1_w, dh1, narrow=(dprojs, w_small), name="rms1_bwd")

    names = ["w_in", "w_out", "w_gate", "w_up", "w_down"]
    qs = [lax.dynamic_update_slice(q, lax.dynamic_slice(p, (s_idx, 0, 0), (1,) + p.shape[1:]), (s_idx, 0, 0))
          for q, p in zip(qs_in + qs_ffn, ps_in + ps_ffn)]
    hs = [_chip_sum(q, name=f"grad_chip_sum_{nm}") for q, nm in zip(qs, names)]
    theirs = _pair_share(hs, name="grad_pair_share")
    big_w = [w_in[0], w_out[0], w_gate[0], w_up[0], w_down[0]]
    big_m = [m_w_in[0], m_w_out[0], m_w_gate[0], m_w_up[0], m_w_down[0]]
    big_v = [v_w_in[0], v_w_out[0], v_w_gate[0], v_w_up[0], v_w_down[0]]
    big = {}
    for nm, mine, other, w_, m_, v_ in zip(names, hs, theirs, big_w, big_m, big_v):
        big[nm] = tuple(a[None] for a in _adamw_halves(w_, mine, other, c_arr, m_, v_, name=f"adamw_{nm}"))

    d_wa_h = jnp.sum(d_wa.reshape(HEADS, DV), axis=0, keepdims=True)
    small = [d_n1, d_alog[:, HEADS:2 * HEADS], d_dtb[:, HEADS:2 * HEADS], d_wa_h, d_wb, d_n2, d_nf,
             d_cw[:CONV_W].reshape(1, -1)]
    sizes = [a.shape[1] for a in small]
    packed = jnp.concatenate(small, axis=1)
    n_pack = packed.shape[1]
    n_rows = -(-n_pack // LANES)
    n_rows = -(-n_rows // 8) * 8
    packed = jnp.pad(packed, ((0, 0), (0, n_rows * LANES - n_pack))).reshape(n_rows, LANES)
    red = _allreduce_small(packed, name="allreduce_small").reshape(1, -1)
    offs = np.cumsum([0] + sizes)
    g_n1, g_alog, g_dtb, g_wa, g_wb, g_n2, g_nf, g_cw = [red[:, offs[i]:offs[i + 1]] for i in range(len(sizes))]
    ncw = conv_w.shape[2]
    g_cw = lax.dynamic_slice(g_cw.reshape(CONV_W, -1), (0, s_idx * ncw), (CONV_W, ncw))

    def small_update(w_, g, m_, v_, nm):
        shape = w_.shape
        pad = (-w_.size) % LANES
        to2 = lambda a: jnp.pad(a.reshape(1, -1), ((0, 0), (0, pad)))
        outs = _adamw(to2(w_), to2(g), to2(m_), to2(v_), name=f"adamw_{nm}")
        return (g.reshape(shape),) + tuple(a[:, :w_.size].reshape(shape) for a in outs)

    res = {
        "norm1_w": small_update(norm1_w, g_n1, m_norm1_w, v_norm1_w, "norm1_w"),
        "w_in": big["w_in"],
        "conv_w": small_update(conv_w, g_cw, m_conv_w, v_conv_w, "conv_w"),
        "a_log": small_update(a_log, g_alog, m_a_log, v_a_log, "a_log"),
        "dt_bias": small_update(dt_bias, g_dtb, m_dt_bias, v_dt_bias, "dt_bias"),
        "gdn_norm_w": small_update(gdn_norm_w, g_wa, m_gdn_norm_w, v_gdn_norm_w, "gdn_norm_w"),
        "ret_norm_w": small_update(ret_norm_w, g_wb, m_ret_norm_w, v_ret_norm_w, "ret_norm_w"),
        "w_out": big["w_out"],
        "norm2_w": small_update(norm2_w, g_n2, m_norm2_w, v_norm2_w, "norm2_w"),
        "w_gate": big["w_gate"],
        "w_up": big["w_up"],
        "w_down": big["w_down"],
        "norm_f_w": small_update(norm_f_w, g_nf, m_norm_f_w, v_norm_f_w, "norm_f_w"),
    }
    order = ["norm1_w", "w_in", "conv_w", "a_log", "dt_bias", "gdn_norm_w", "ret_norm_w", "w_out", "norm2_w",
             "w_gate", "w_up", "w_down", "norm_f_w"]
    loss = lax.psum(loss_row[0, 0], ("x", "y", "c"))
    return (loss, dx[None], *[res[n][0] for n in order], *[res[n][1] for n in order],
            *[res[n][2] for n in order], *[res[n][3] for n in order])
```

```python
import functools

import jax
import jax.numpy as jnp
import numpy as np
from jax import lax
from jax.experimental import pallas as pl
from jax.experimental.pallas import tpu as pltpu

F32 = jnp.float32
BF16 = jnp.bfloat16
MESH = pl.DeviceIdType.MESH

HEADS = 8
DK = 128
DV = 256
CHUNK = 64
CONV_W = 4
EPS = 1e-6
ROPE_BASE = 10000.0
ADAM_LR, ADAM_B1, ADAM_B2, ADAM_EPS, ADAM_WD, ADAM_STEP = 0.001, 0.9, 0.999, 1e-08, 0.01, 10
N_CHIPS = 4
N_DEV = 8
LANES = 128
HALO = 8
VMEM_LIMIT = 56 * 1024 * 1024
HB = 8


def _pick(n, cands):
    for c in cands:
        if n % c == 0:
            return c
    raise ValueError(f"no tile for {n} in {cands}")


def _params(sem=None):
    return pltpu.CompilerParams(dimension_semantics=sem, vmem_limit_bytes=VMEM_LIMIT)


def _dot(a, b):
    return jnp.dot(a.astype(BF16), b.astype(BF16), preferred_element_type=F32)


def _dot_nt(a, b):
    return lax.dot_general(a.astype(BF16), b.astype(BF16), (((1,), (1,)), ((), ())), preferred_element_type=F32)


def _dot_tn(a, b):
    return lax.dot_general(a.astype(BF16), b.astype(BF16), (((0,), (0,)), ((), ())), preferred_element_type=F32)


def _sigmoid(x):
    return 1.0 / (1.0 + jnp.exp(-x))


def _iota2(n):
    return lax.broadcasted_iota(jnp.int32, (n, n), 0), lax.broadcasted_iota(jnp.int32, (n, n), 1)


def _row_to_col(row):
    n = row.shape[1]
    r, c = _iota2(n)
    return jnp.sum(jnp.where(r == c, jnp.broadcast_to(row, (n, n)), 0.0), axis=1, keepdims=True)


def _col_to_row(col):
    n = col.shape[0]
    r, c = _iota2(n)
    return jnp.sum(jnp.where(r == c, jnp.broadcast_to(col, (n, n)), 0.0), axis=0, keepdims=True)


class _Side:
    def __init__(self, ins, out_shapes, n_sem, phases):
        self.ins, self.out_shapes, self.n_sem, self.phases = list(ins), list(out_shapes), n_sem, phases

    def sems(self):
        return [pltpu.SemaphoreType.DMA((self.n_sem,)), pltpu.SemaphoreType.DMA((self.n_sem,))]


def _mm(a, b, *, name, tm, tn, tk, ta=False, tb=False, out_dtype=F32, res=None, side=None, pair=None,
        out_stacked=False):
    m, k = (a.shape[1], a.shape[0]) if ta else a.shape
    b_slots = b.ndim == 3
    if b_slots:
        assert tb and tk == b.shape[2]
        n = b.shape[1]
    else:
        n = b.shape[0] if tb else b.shape[1]
    tm, tn, tk = min(tm, m), min(tn, n), min(tk, k)
    assert m % tm == 0 and n % tn == 0 and k % tk == 0, (name, m, n, k)
    nk = k // tk
    nj, ni = n // tn, m // tm
    dn = (((0 if ta else 1,), (1 if tb else 0,)), ((), ()))
    n_ab = 4 if pair else 2
    n_in = n_ab + (res is not None)
    n_side_in = len(side.ins) if side else 0
    n_side_out = len(side.out_shapes) if side else 0

    def body(*refs):
        a_ref, b_ref = refs[0], refs[1]
        r_ref = refs[n_ab] if res is not None else None
        o_ref = refs[n_in + n_side_in]
        if side:
            s_in = refs[n_in:n_in + n_side_in]
            s_out = refs[n_in + n_side_in + 1:n_in + n_side_in + 1 + n_side_out]
            ssem, rsem = refs[-2], refs[-1]
            j_, i_, k_ = pl.program_id(0), pl.program_id(1), pl.program_id(2)
            n_mid = len(side.phases) - 2
            assert n_mid == 0 or nj >= 2 * n_mid
            when = [(j_ == 0) & (i_ == 0) & (k_ == 0)]
            when += [(j_ == nj // 2 + (p * (nj // 2)) // n_mid) & (i_ == 0) & (k_ == 0) for p in range(n_mid)]
            when.append((j_ == nj - 1) & (i_ == ni - 1) & (k_ == nk - 1))

        def run_phase(p):
            @pl.when(when[p])
            def _():
                side.phases[p](s_in, s_out, ssem, rsem)

        if side:
            for p in range(len(side.phases) - 1):
                run_phase(p)

        def finish(r):
            if res is not None:
                r = r + r_ref[...]
            o_ref[...] = r.astype(out_dtype)

        part = lax.dot_general(a_ref[...], b_ref[...], dn, preferred_element_type=F32)
        if pair:
            part = part + lax.dot_general(refs[2][...], refs[3][...], dn, preferred_element_type=F32)
        if nk == 1:
            finish(part)
        else:
            acc = refs[n_in + n_side_in + 1 + n_side_out]
            kk = pl.program_id(2)

            @pl.when(kk == 0)
            def _():
                acc[...] = part

            @pl.when((kk > 0) & (kk < nk - 1))
            def _():
                acc[...] += part

            @pl.when(kk == nk - 1)
            def _():
                finish(acc[...] + part)

        if side:
            run_phase(len(side.phases) - 1)

    a_spec = pl.BlockSpec((tk, tm), lambda j, i, kk: (kk, i)) if ta else pl.BlockSpec((tm, tk), lambda j, i, kk: (i, kk))
    b_spec = pl.BlockSpec((tn, tk), lambda j, i, kk: (j, kk)) if tb else pl.BlockSpec((tk, tn), lambda j, i, kk: (kk, j))
    if b_slots:
        b_spec = pl.BlockSpec((None, tn, tk), lambda j, i, kk: (kk, j, 0))
    o_spec = pl.BlockSpec((tm, tn), lambda j, i, kk: (i, j))
    o_shape = jax.ShapeDtypeStruct((m, n), out_dtype)
    if out_stacked:
        assert res is None
        o_spec = pl.BlockSpec((None, tm, tn), lambda j, i, kk: (j, i, 0))
        o_shape = jax.ShapeDtypeStruct((nj, m, tn), out_dtype)
    in_specs, args = [a_spec, b_spec], [a, b]
    if pair:
        assert pair[0].shape == a.shape and pair[1].shape == b.shape
        in_specs += [a_spec, b_spec]
        args += list(pair)
    if res is not None:
        in_specs.append(o_spec)
        args.append(res)
    out_specs, out_shape = o_spec, o_shape
    scratch = [pltpu.VMEM((tm, tn), F32)] if nk > 1 else []
    sem = ("parallel", "parallel", "arbitrary")
    if side:
        hbm = pl.BlockSpec(memory_space=pl.ANY)
        in_specs += [hbm] * n_side_in
        args += side.ins
        out_specs, out_shape = [o_spec] + [hbm] * n_side_out, [out_shape] + side.out_shapes
        scratch += side.sems()
        sem = ("arbitrary",) * 3
    return pl.pallas_call(
        body, name=name, grid=(nj, ni, nk), in_specs=in_specs, out_specs=out_specs, out_shape=out_shape,
        scratch_shapes=scratch, compiler_params=_params(sem),
    )(*args)


def _rms_fwd(x, w, *, name, w_narrow=None):
    t, d = x.shape
    tt = _pick(t, (512, 256))

    def body(*refs):
        x_ref, w_ref, o_ref = refs[0], refs[1], refs[-2 if w_narrow is not None else -1]
        xv = x_ref[...]
        r = lax.rsqrt(jnp.mean(xv * xv, axis=-1, keepdims=True) + EPS)
        u = (xv * r * w_ref[...]).astype(BF16)
        o_ref[...] = u
        if w_narrow is not None:
            refs[-1][...] = jnp.dot(u, refs[2][...], preferred_element_type=F32)

    blk = pl.BlockSpec((tt, d), lambda i: (i, 0))
    in_specs, args = [blk, pl.BlockSpec((1, d), lambda i: (0, 0))], [x, w]
    out_specs, out_shape = [blk], [jax.ShapeDtypeStruct((t, d), BF16)]
    if w_narrow is not None:
        in_specs.append(pl.BlockSpec(w_narrow.shape, lambda i: (0, 0)))
        args.append(w_narrow)
        out_specs.append(pl.BlockSpec((tt, LANES), lambda i: (i, 0)))
        out_shape.append(jax.ShapeDtypeStruct((t, LANES), F32))
    out = pl.pallas_call(
        body, name=name, grid=(t // tt,), in_specs=in_specs, out_specs=out_specs, out_shape=out_shape,
        compiler_params=_params(("parallel",)),
    )(*args)
    return out if w_narrow is not None else out[0]


def _rms_bwd(dn, x, w, dres, *, name, narrow=None):
    t, d = x.shape
    tt = _pick(t, (256,))

    def body(*refs):
        dn_ref, x_ref, w_ref, dres_ref = refs[:4]
        dx_ref, dxb_ref, dw_ref = refs[-3:]
        xv, g = x_ref[...], dn_ref[...].astype(F32)
        if narrow is not None:
            g = g + lax.dot_general(refs[4][...], refs[5][...], (((1,), (1,)), ((), ())), preferred_element_type=F32)
        r = lax.rsqrt(jnp.mean(xv * xv, axis=-1, keepdims=True) + EPS)
        xh = xv * r
        gw = g * w_ref[...]
        dx = dres_ref[...] + r * (gw - xh * jnp.mean(gw * xh, axis=-1, keepdims=True))
        dx_ref[...] = dx
        dxb_ref[...] = dx.astype(BF16)

        @pl.when(pl.program_id(0) == 0)
        def _():
            dw_ref[...] = jnp.zeros_like(dw_ref)

        dw_ref[...] += jnp.sum(g * xh, axis=0, keepdims=True)

    blk = pl.BlockSpec((tt, d), lambda i: (i, 0))
    row = pl.BlockSpec((1, d), lambda i: (0, 0))
    in_specs, args = [blk, blk, row, blk], [dn, x, w, dres]
    if narrow is not None:
        in_specs += [pl.BlockSpec((tt, LANES), lambda i: (i, 0)), pl.BlockSpec(narrow[1].shape, lambda i: (0, 0))]
        args += list(narrow)
    return pl.pallas_call(
        body, name=name, grid=(t // tt,), in_specs=in_specs, out_specs=[blk, blk, row],
        out_shape=[jax.ShapeDtypeStruct((t, d), F32), jax.ShapeDtypeStruct((t, d), BF16),
                   jax.ShapeDtypeStruct((1, d), F32)],
        compiler_params=_params(("arbitrary",)),
    )(*args)


def _loss_head(h2, tgt, wf, *, name):
    t, d = h2.shape
    tt = _pick(t, (256,))

    def body(x_ref, t_ref, w_ref, loss_ref, dx_ref, dxb_ref, dw_ref):
        xv = x_ref[...]
        r = lax.rsqrt(jnp.mean(xv * xv, axis=-1, keepdims=True) + EPS)
        xh = xv * r
        err = xh * w_ref[...] - t_ref[...]
        lpart = 0.5 * jnp.sum(jnp.mean(err * err, axis=-1, keepdims=True), axis=0, keepdims=True)
        dy = err * (1.0 / d)
        gw = dy * w_ref[...]
        dx = r * (gw - xh * jnp.mean(gw * xh, axis=-1, keepdims=True))
        dx_ref[...] = dx
        dxb_ref[...] = dx.astype(BF16)

        @pl.when(pl.program_id(0) == 0)
        def _():
            dw_ref[...] = jnp.zeros_like(dw_ref)
            loss_ref[...] = jnp.zeros_like(loss_ref)

        dw_ref[...] += jnp.sum(dy * xh, axis=0, keepdims=True)
        loss_ref[...] += jnp.broadcast_to(lpart, loss_ref.shape)

    blk = pl.BlockSpec((tt, d), lambda i: (i, 0))
    row = pl.BlockSpec((1, d), lambda i: (0, 0))
    lrow = pl.BlockSpec((1, LANES), lambda i: (0, 0))
    return pl.pallas_call(
        body, name=name, grid=(t // tt,),
        in_specs=[blk, blk, row], out_specs=[lrow, blk, blk, row],
        out_shape=[jax.ShapeDtypeStruct((1, LANES), F32), jax.ShapeDtypeStruct((t, d), F32),
                   jax.ShapeDtypeStruct((t, d), BF16), jax.ShapeDtypeStruct((1, d), F32)],
        compiler_params=_params(("arbitrary",)),
    )(h2, tgt, wf)


def _ffn_in(hn, w_g, w_u, *, name, tm):
    t, d = hn.shape
    tn = w_g.shape[2]
    f = w_g.shape[0] * tn
    tm = min(tm, t)

    def body(a_ref, g_ref, u_ref, gt_ref, up_ref, act_ref):
        a = a_ref[...]
        g = jnp.dot(a, g_ref[...], preferred_element_type=F32)
        u = jnp.dot(a, u_ref[...], preferred_element_type=F32)
        gt_ref[...] = g.astype(BF16)
        up_ref[...] = u.astype(BF16)
        act_ref[...] = (g * _sigmoid(g) * u).astype(BF16)

    wblk = pl.BlockSpec((None, d, tn), lambda j, i: (j, 0, 0))
    oblk = pl.BlockSpec((tm, tn), lambda j, i: (i, j))
    return pl.pallas_call(
        body, name=name, grid=(f // tn, t // tm),
        in_specs=[pl.BlockSpec((tm, d), lambda j, i: (i, 0)), wblk, wblk], out_specs=[oblk] * 3,
        out_shape=[jax.ShapeDtypeStruct((t, f), BF16)] * 3,
        compiler_params=_params(("parallel", "parallel")),
    )(hn, w_g, w_u)


def _ffn_back(dh2b, w_d, gt, up, *, name, tm, tn):
    t, d = dh2b.shape
    f = w_d.shape[0]
    tm = min(tm, t)

    def body(a_ref, w_ref, g_ref, u_ref, dg_ref, du_ref):
        da = lax.dot_general(a_ref[...], w_ref[...], (((1,), (1,)), ((), ())), preferred_element_type=F32)
        g = g_ref[...].astype(F32)
        sg = _sigmoid(g)
        dg_ref[...] = (da * u_ref[...].astype(F32) * sg * (1.0 + g * (1.0 - sg))).astype(BF16)
        du_ref[...] = (da * g * sg).astype(BF16)

    oblk = pl.BlockSpec((tm, tn), lambda j, i: (i, j))
    return pl.pallas_call(
        body, name=name, grid=(f // tn, t // tm),
        in_specs=[pl.BlockSpec((tm, d), lambda j, i: (i, 0)), pl.BlockSpec((tn, d), lambda j, i: (j, 0)), oblk, oblk],
        out_specs=[oblk] * 2,
        out_shape=[jax.ShapeDtypeStruct((t, f), BF16)] * 2,
        compiler_params=_params(("parallel", "parallel")),
    )(dh2b, w_d, gt, up)


def _conv_cols(xs, cw_ref, cb, tt):
    cols = slice(cb * LANES, (cb + 1) * LANES)
    base = HALO - (CONV_W - 1)
    acc = xs[base:base + tt, cols] * cw_ref[0:1, cols]
    for j in range(1, CONV_W):
        acc = acc + xs[base + j:base + j + tt, cols] * cw_ref[j:j + 1, cols]
    return acc


def _fill_halo(xs, x_ref, xp_ref, tt):
    first = pl.program_id(0) == 0
    xs[0:HALO, :] = jnp.where(first, 0.0, xp_ref[...])
    xs[HALO:HALO + tt, :] = x_ref[...]


def _conv_specs(tt, ch):
    cur = pl.BlockSpec((tt, ch), lambda i: (i, 0))
    prev = pl.BlockSpec((HALO, ch), lambda i: (jnp.maximum(i * (tt // HALO) - 1, 0), 0))
    return cur, prev


def _conv_fwd(proj, conv_w, *, name):
    t = proj.shape[0]
    ch = conv_w.shape[1]
    nqk = HEADS * DK
    tt = _pick(t, (256,))
    cur, prev = _conv_specs(tt, ch)

    def body(x_ref, xp_ref, cw_ref, q_ref, k_ref, v_ref, xs):
        _fill_halo(xs, x_ref, xp_ref, tt)
        for cb in range(ch // LANES):
            c = _conv_cols(xs, cw_ref, cb, tt)
            s = c * _sigmoid(c)
            if cb < 2 * HEADS:
                s = s * lax.rsqrt(jnp.sum(s * s, axis=-1, keepdims=True) + EPS)
                if cb < HEADS:
                    q_ref[:, cb * LANES:(cb + 1) * LANES] = s * (DK ** -0.5)
                else:
                    k_ref[:, (cb - HEADS) * LANES:(cb - HEADS + 1) * LANES] = s
            else:
                v_ref[:, (cb - 2 * HEADS) * LANES:(cb - 2 * HEADS + 1) * LANES] = s

    return pl.pallas_call(
        body, name=name, grid=(t // tt,),
        in_specs=[cur, prev, pl.BlockSpec((CONV_W, ch), lambda i: (0, 0))],
        out_specs=[pl.BlockSpec((tt, nqk), lambda i: (i, 0)), pl.BlockSpec((tt, nqk), lambda i: (i, 0)),
                   pl.BlockSpec((tt, ch - 2 * nqk), lambda i: (i, 0))],
        out_shape=[jax.ShapeDtypeStruct((t, nqk), F32), jax.ShapeDtypeStruct((t, nqk), F32),
                   jax.ShapeDtypeStruct((t, ch - 2 * nqk), F32)],
        scratch_shapes=[pltpu.VMEM((HALO + tt, ch), F32)],
        compiler_params=_params(("arbitrary",)),
    )(proj, proj, conv_w)


def _conv_bwd_pre(proj, conv_w, dq, dk, dv, *, name):
    t = proj.shape[0]
    ch = conv_w.shape[1]
    nqk = HEADS * DK
    tt = _pick(t, (256,))
    cur, prev = _conv_specs(tt, ch)

    head = 2 * HALO

    def body(x_ref, xp_ref, cw_ref, dq_ref, dk_ref, dv_ref, dc_ref, xs):
        first = pl.program_id(0) == 0
        xs[0:HALO, :] = jnp.where(first, 0.0, xp_ref[...])
        xs[HALO:HALO + head, :] = x_ref[0:head, :]
        xb = x_ref[...].astype(BF16)
        r, cc = _iota2(tt)
        moved = [None] + [jnp.dot((cc == r - sh).astype(BF16), xb, preferred_element_type=F32)
                          for sh in range(1, CONV_W)]
        for cb in range(ch // LANES):
            cols = slice(cb * LANES, (cb + 1) * LANES)
            c_head = c_main = None
            for j in range(CONV_W):
                sh = CONV_W - 1 - j
                w_j = cw_ref[j:j + 1, cols]
                x_m = x_ref[head:tt, cols] if sh == 0 else moved[sh][head:tt, cols]
                x_h = xs[HALO - sh:HALO - sh + head, cols]
                c_main = x_m * w_j if c_main is None else c_main + x_m * w_j
                c_head = x_h * w_j if c_head is None else c_head + x_h * w_j
            c = jnp.concatenate([c_head, c_main], axis=0)
            sg = _sigmoid(c)
            s = c * sg
            if cb < 2 * HEADS:
                if cb < HEADS:
                    d, scale = dq_ref[:, cb * LANES:(cb + 1) * LANES], DK ** -0.5
                else:
                    d, scale = dk_ref[:, (cb - HEADS) * LANES:(cb - HEADS + 1) * LANES], 1.0
                rinv = lax.rsqrt(jnp.sum(s * s, axis=-1, keepdims=True) + EPS)
                ds = scale * rinv * (d - s * (rinv * rinv) * jnp.sum(d * s, axis=-1, keepdims=True))
            else:
                ds = dv_ref[:, (cb - 2 * HEADS) * LANES:(cb - 2 * HEADS + 1) * LANES]
            dc_ref[:, cb * LANES:(cb + 1) * LANES] = ds * sg * (1.0 + c * (1.0 - sg))

    return pl.pallas_call(
        body, name=name, grid=(t // tt,),
        in_specs=[cur, prev, pl.BlockSpec((CONV_W, ch), lambda i: (0, 0)),
                  pl.BlockSpec((tt, nqk), lambda i: (i, 0)), pl.BlockSpec((tt, nqk), lambda i: (i, 0)),
                  pl.BlockSpec((tt, ch - 2 * nqk), lambda i: (i, 0))],
        out_specs=pl.BlockSpec((tt, ch), lambda i: (i, 0)),
        out_shape=jax.ShapeDtypeStruct((t, ch), F32),
        scratch_shapes=[pltpu.VMEM((HALO + head, ch), F32)],
        compiler_params=_params(("arbitrary",)),
    )(proj, proj, conv_w, dq, dk, dv)


def _conv_bwd(proj, dc, conv_w, dproj, *, name):
    t = proj.shape[0]
    ch = conv_w.shape[1]
    tt = _pick(t, (256,))
    nt = t // tt
    cur = pl.BlockSpec((tt, ch), lambda i: (i, 0))
    nxt = pl.BlockSpec((HALO, ch), lambda i: (jnp.minimum((i + 1) * (tt // HALO), t // HALO - 1), 0))

    tail = 2 * HALO
    main = tt - tail

    def body(x_ref, d_ref, dn_ref, cw_ref, alias_ref, dx_ref, dw_ref, ds):
        last = pl.program_id(0) == nt - 1
        ds[0:tail, :] = d_ref[main:tt, :]
        ds[tail:tail + HALO, :] = jnp.where(last, 0.0, dn_ref[...])

        @pl.when(pl.program_id(0) == 0)
        def _():
            dw_ref[...] = jnp.zeros_like(dw_ref)

        dcb = d_ref[...].astype(BF16)
        r, c = _iota2(tt)
        moved = [None] + [jnp.dot((c == r + sh).astype(BF16), dcb, preferred_element_type=F32)
                          for sh in range(1, CONV_W)]
        for cb in range(ch // LANES):
            cols = slice(cb * LANES, (cb + 1) * LANES)
            x_m, x_t = x_ref[0:main, cols], x_ref[main:tt, cols]
            acc_m = acc_t = None
            for j in range(CONV_W):
                sh = CONV_W - 1 - j
                w_j = cw_ref[j:j + 1, cols]
                dj_m = d_ref[0:main, cols] if sh == 0 else moved[sh][0:main, cols]
                dj_t = ds[sh:sh + tail, cols]
                acc_m = dj_m * w_j if acc_m is None else acc_m + dj_m * w_j
                acc_t = dj_t * w_j if acc_t is None else acc_t + dj_t * w_j
                dw_ref[j:j + 1, cols] += (jnp.sum(dj_m * x_m, axis=0, keepdims=True)
                                          + jnp.sum(dj_t * x_t, axis=0, keepdims=True))
            dx_ref[0:main, cols] = acc_m.astype(BF16)
            dx_ref[main:tt, cols] = acc_t.astype(BF16)

    return pl.pallas_call(
        body, name=name, grid=(nt,),
        in_specs=[cur, cur, nxt, pl.BlockSpec((CONV_W, ch), lambda i: (0, 0)), pl.BlockSpec(memory_space=pl.ANY)],
        out_specs=[pl.BlockSpec((tt, ch), lambda i: (i, 0)), pl.BlockSpec((HALO, ch), lambda i: (0, 0))],
        out_shape=[jax.ShapeDtypeStruct(dproj.shape, BF16), jax.ShapeDtypeStruct((HALO, ch), F32)],
        scratch_shapes=[pltpu.VMEM((tail + HALO, ch), F32)],
        input_output_aliases={4: 0},
        compiler_params=_params(("arbitrary",)),
    )(proj, dc, dc, conv_w, dproj)


def _bg_fwd(projs, alog_row, dtb_row, *, name):
    t = projs.shape[0]
    tt = _pick(t, (512, 256))

    def body(p_ref, al_ref, db_ref, o_ref):
        p = p_ref[...]
        lane = lax.broadcasted_iota(jnp.int32, p.shape, 1)
        z = p + db_ref[...]
        sp = jnp.maximum(z, 0.0) + jnp.log1p(jnp.exp(-jnp.abs(z)))
        g = -jnp.exp(al_ref[...]) * sp
        o_ref[...] = jnp.where(lane < HEADS, _sigmoid(p), jnp.where(lane < 2 * HEADS, g, 0.0))

    blk = pl.BlockSpec((tt, LANES), lambda i: (i, 0))
    row = pl.BlockSpec((1, LANES), lambda i: (0, 0))
    return pl.pallas_call(
        body, name=name, grid=(t // tt,), in_specs=[blk, row, row], out_specs=blk,
        out_shape=jax.ShapeDtypeStruct((t, LANES), F32), compiler_params=_params(("parallel",)),
    )(projs, alog_row, dtb_row)


def _bg_bwd(projs, dbg, alog_row, dtb_row, *, name):
    t = projs.shape[0]
    tt = _pick(t, (512, 256))

    def body(p_ref, d_ref, al_ref, db_ref, o_ref, dal_ref, ddb_ref):
        p, d = p_ref[...], d_ref[...]
        lane = lax.broadcasted_iota(jnp.int32, p.shape, 1)
        isg = (lane >= HEADS) & (lane < 2 * HEADS)
        be = _sigmoid(p)
        z = p + db_ref[...]
        sp = jnp.maximum(z, 0.0) + jnp.log1p(jnp.exp(-jnp.abs(z)))
        ea = jnp.exp(al_ref[...])
        d_aa = jnp.where(isg, d * (-ea) * _sigmoid(z), 0.0)
        o_ref[...] = jnp.where(lane < HEADS, d * be * (1.0 - be), d_aa).astype(BF16)

        @pl.when(pl.program_id(0) == 0)
        def _():
            dal_ref[...] = jnp.zeros_like(dal_ref)
            ddb_ref[...] = jnp.zeros_like(ddb_ref)

        dal_ref[...] += jnp.sum(jnp.where(isg, d * (-ea) * sp, 0.0), axis=0, keepdims=True)
        ddb_ref[...] += jnp.sum(d_aa, axis=0, keepdims=True)

    blk = pl.BlockSpec((tt, LANES), lambda i: (i, 0))
    row = pl.BlockSpec((1, LANES), lambda i: (0, 0))
    return pl.pallas_call(
        body, name=name, grid=(t // tt,), in_specs=[blk, blk, row, row], out_specs=[blk, row, row],
        out_shape=[jax.ShapeDtypeStruct((t, LANES), BF16), jax.ShapeDtypeStruct((1, LANES), F32),
                   jax.ShapeDtypeStruct((1, LANES), F32)],
        compiler_params=_params(("arbitrary",)),
    )(projs, dbg, alog_row, dtb_row)


def _hn_specs(n_chunks, rev=False):
    def nn(n):
        return n_chunks - 1 - n if rev else n
    tok = lambda w: pl.BlockSpec((CHUNK, HB * w), lambda g, n: (nn(n), g))
    per = lambda a, b: pl.BlockSpec((HB, None, a, b), lambda g, n: (g, nn(n), 0, 0))
    return tok, per


def _hcols(hh, w):
    return slice(hh * w, (hh + 1) * w)


def _decay(gc_col, gc_row):
    r, c = _iota2(CHUNK)
    d = jnp.exp(jnp.minimum(gc_col - gc_row, 0.0))
    return jnp.where(r >= c, d, 0.0), jnp.where(r > c, d, 0.0)


def _gdn_prep(q, k, v, beta_r, g_r, *, name):
    t = q.shape[0]
    n_chunks = t // CHUNK
    tok, per = _hn_specs(n_chunks)

    def body(q_ref, k_ref, v_ref, b_ref, g_ref, gc_ref, ti_ref, u_ref, w_ref, p_ref):
        r, c = _iota2(CHUNK)
        hs = range(HB)
        kk = [k_ref[:, _hcols(h, DK)] for h in hs]
        gc_col = [jnp.sum(jnp.where(c <= r, jnp.broadcast_to(g_ref[h], (CHUNK, CHUNK)), 0.0), axis=1, keepdims=True)
                  for h in hs]
        gc_row = [_col_to_row(gc_col[h]) for h in hs]
        beta_col = [_row_to_col(b_ref[h]) for h in hs]
        dec = [_decay(gc_col[h], gc_row[h]) for h in hs]
        kb = [kk[h] * beta_col[h] for h in hs]
        for h in hs:
            gc_ref[h] = gc_row[h]
            p_ref[h] = (_dot_nt(q_ref[:, _hcols(h, DK)], kk[h]) * dec[h][0]).astype(BF16)
        pw = [_dot_nt(kb[h], kk[h]) * dec[h][1] for h in hs]
        y = [-pw[h] for h in hs]
        for _ in range(5):
            pw = [_dot(pw[h], pw[h]) for h in hs]
            yp = [_dot(y[h], pw[h]) for h in hs]
            y = [y[h] + pw[h] + yp[h] for h in hs]
        vb = [v_ref[:, _hcols(h, DV)] * beta_col[h] for h in hs]
        kbe = [kb[h] * jnp.exp(gc_col[h]) for h in hs]
        yv = [_dot(y[h], vb[h]) for h in hs]
        yk = [_dot(y[h], kbe[h]) for h in hs]
        for h in hs:
            ti_ref[h] = y[h].astype(BF16)
            u_ref[:, _hcols(h, DV)] = vb[h] + yv[h]
            w_ref[:, _hcols(h, DK)] = (kbe[h] + yk[h]).astype(BF16)

    return pl.pallas_call(
        body, name=name, grid=(HEADS // HB, n_chunks),
        in_specs=[tok(DK), tok(DK), tok(DV), per(1, CHUNK), per(1, CHUNK)],
        out_specs=[per(1, CHUNK), per(CHUNK, CHUNK), tok(DV), tok(DK), per(CHUNK, CHUNK)],
        out_shape=[jax.ShapeDtypeStruct((HEADS, n_chunks, 1, CHUNK), F32),
                   jax.ShapeDtypeStruct((HEADS, n_chunks, CHUNK, CHUNK), BF16),
                   jax.ShapeDtypeStruct((t, HEADS * DV), F32), jax.ShapeDtypeStruct((t, HEADS * DK), BF16),
                   jax.ShapeDtypeStruct((HEADS, n_chunks, CHUNK, CHUNK), BF16)],
        compiler_params=_params(("parallel", "parallel")),
    )(q, k, v, beta_r, g_r)


def _gdn_scan(q, k, u, w, p, gc_r, *, name):
    t = q.shape[0]
    n_chunks = t // CHUNK
    tok, per = _hn_specs(n_chunks)

    def body(q_ref, k_ref, u_ref, w_ref, p_ref, gc_ref, o_ref, s_ref, st):
        @pl.when(pl.program_id(1) == 0)
        def _():
            st[...] = jnp.zeros_like(st)

        hs = range(HB)
        s = [st[h] for h in hs]
        gc_row = [gc_ref[h] for h in hs]
        gc_col = [_row_to_col(gc_row[h]) for h in hs]
        glast = [gc_row[h][:, CHUNK - 1:CHUNK] for h in hs]
        for h in hs:
            s_ref[h] = s[h].astype(BF16)
        ws = [_dot(w_ref[:, _hcols(h, DK)], s[h]) for h in hs]
        qs = [_dot(q_ref[:, _hcols(h, DK)] * jnp.exp(gc_col[h]), s[h]) for h in hs]
        vn = [u_ref[:, _hcols(h, DV)] - ws[h] for h in hs]
        pv = [_dot(p_ref[h], vn[h]) for h in hs]
        kv = [_dot_tn(k_ref[:, _hcols(h, DK)] * jnp.exp(glast[h] - gc_col[h]), vn[h]) for h in hs]
        for h in hs:
            o_ref[:, _hcols(h, DV)] = qs[h] + pv[h]
            st[h] = s[h] * jnp.exp(glast[h]) + kv[h]

    return pl.pallas_call(
        body, name=name, grid=(HEADS // HB, n_chunks),
        in_specs=[tok(DK), tok(DK), tok(DV), tok(DK), per(CHUNK, CHUNK), per(1, CHUNK)],
        out_specs=[tok(DV), per(DK, DV)],
        out_shape=[jax.ShapeDtypeStruct((t, HEADS * DV), F32),
                   jax.ShapeDtypeStruct((HEADS, n_chunks, DK, DV), BF16)],
        scratch_shapes=[pltpu.VMEM((HB, DK, DV), F32)],
        compiler_params=_params(("arbitrary", "arbitrary")),
    )(q, k, u, w, p, gc_r)


def _gdn_bwd(q, k, v, beta_r, gc_r, ti, u, w, s_all, do, *, name):
    t = q.shape[0]
    n_chunks = t // CHUNK
    tok, per = _hn_specs(n_chunks, rev=True)

    def body(q_ref, k_ref, v_ref, b_ref, gc_ref, ti_ref, u_ref, w_ref, s_ref, do_ref,
             dq_ref, dk_ref, dv_ref, db_ref, dg_ref, dst):
        @pl.when(pl.program_id(1) == 0)
        def _():
            dst[...] = jnp.zeros_like(dst)

        r, c = _iota2(CHUNK)
        rows = lax.broadcasted_iota(jnp.int32, (CHUNK, 1), 0)
        rsum = lambda a: jnp.sum(a, axis=1, keepdims=True)
        hs = range(HB)
        qq = [q_ref[:, _hcols(h, DK)] for h in hs]
        kk = [k_ref[:, _hcols(h, DK)] for h in hs]
        ww = [w_ref[:, _hcols(h, DK)] for h in hs]
        uu = [u_ref[:, _hcols(h, DV)] for h in hs]
        d_o = [do_ref[:, _hcols(h, DV)] for h in hs]
        s = [s_ref[h].astype(F32) for h in hs]
        d_s = [dst[h] for h in hs]
        gc_row = [gc_ref[h] for h in hs]
        gc_col = [_row_to_col(gc_row[h]) for h in hs]
        beta_col = [_row_to_col(b_ref[h]) for h in hs]
        eg = [jnp.exp(gc_col[h]) for h in hs]
        glast = [gc_row[h][:, CHUNK - 1:CHUNK] for h in hs]
        eglast = [jnp.exp(glast[h]) for h in hs]
        e2 = [jnp.exp(glast[h] - gc_col[h]) for h in hs]
        dec = [_decay(gc_col[h], gc_row[h]) for h in hs]
        kb = [kk[h] * beta_col[h] for h in hs]
        ke = [kk[h] * e2[h] for h in hs]
        qe = [qq[h] * eg[h] for h in hs]
        kkt = [_dot_nt(kb[h], kk[h]) for h in hs]
        qk = [_dot_nt(qq[h], kk[h]) for h in hs]
        ws = [_dot(ww[h], s[h]) for h in hs]
        a = [kkt[h] * dec[h][1] for h in hs]
        pp = [qk[h] * dec[h][0] for h in hs]
        vn = [uu[h] - ws[h] for h in hs]
        t1 = [_dot_tn(pp[h], d_o[h]) for h in hs]
        t2 = [_dot(ke[h], d_s[h]) for h in hs]
        dqe = [_dot_nt(d_o[h], s[h]) for h in hs]
        dke = [_dot_nt(vn[h], d_s[h]) for h in hs]
        dqk = [_dot_nt(d_o[h], vn[h]) * dec[h][0] for h in hs]
        dvn = [t1[h] + t2[h] for h in hs]
        t3 = [_dot_tn(qe[h], d_o[h]) for h in hs]
        t4 = [_dot_tn(ww[h], dvn[h]) for h in hs]
        dw = [-_dot_nt(dvn[h], s[h]) for h in hs]
        de2 = [rsum(dke[h] * ke[h]) for h in hs]
        dglast = [jnp.sum(de2[h], axis=0, keepdims=True)
                  + eglast[h] * jnp.sum(rsum(d_s[h] * s[h]), axis=0, keepdims=True) for h in hs]
        for h in hs:
            dst[h] = d_s[h] * eglast[h] + t3[h] - t4[h]
        yy = [ti_ref[h] for h in hs]
        t5 = [_dot_tn(yy[h], dvn[h]) for h in hs]
        t6 = [_dot_tn(yy[h], dw[h]) for h in hs]
        dvb = [dvn[h] + t5[h] for h in hs]
        dkbe = [dw[h] + t6[h] for h in hs]
        t7 = [_dot_nt(dvb[h], uu[h]) for h in hs]
        t8 = [_dot_nt(dkbe[h], ww[h]) for h in hs]
        d_a = [jnp.where(r > c, -(t7[h] + t8[h]), 0.0) for h in hs]
        dm = [d_a[h] * dec[h][1] for h in hs]
        t9 = [_dot(dm[h], kk[h]) for h in hs]
        t10 = [_dot(dqk[h], kk[h]) for h in hs]
        t11 = [_dot_tn(dqk[h], qq[h]) for h in hs]
        t12 = [_dot_tn(dm[h], kb[h]) for h in hs]
        for h in hs:
            dkb = t9[h] + dkbe[h] * eg[h]
            e_mat = d_a[h] * a[h] + dqk[h] * qk[h]
            dgc = (rsum(dqe[h] * qe[h]) - de2[h] + rsum(dkbe[h] * kb[h] * eg[h]) + rsum(e_mat)
                   - _row_to_col(jnp.sum(e_mat, axis=0, keepdims=True)))
            dgc = dgc + jnp.where(rows == CHUNK - 1, dglast[h], 0.0)
            dq_ref[:, _hcols(h, DK)] = dqe[h] * eg[h] + t10[h]
            dk_ref[:, _hcols(h, DK)] = t11[h] + dke[h] * e2[h] + t12[h] + dkb * beta_col[h]
            dv_ref[:, _hcols(h, DV)] = dvb[h] * beta_col[h]
            dbeta = rsum(dkb * kk[h]) + rsum(dvb[h] * v_ref[:, _hcols(h, DV)])
            db_ref[h] = _col_to_row(dbeta)
            dg_ref[h] = jnp.sum(jnp.where(r >= c, jnp.broadcast_to(dgc, (CHUNK, CHUNK)), 0.0), axis=0, keepdims=True)

    return pl.pallas_call(
        body, name=name, grid=(HEADS // HB, n_chunks),
        in_specs=[tok(DK), tok(DK), tok(DV), per(1, CHUNK), per(1, CHUNK), per(CHUNK, CHUNK), tok(DV), tok(DK),
                  per(DK, DV), tok(DV)],
        out_specs=[tok(DK), tok(DK), tok(DV), per(1, CHUNK), per(1, CHUNK)],
        out_shape=[jax.ShapeDtypeStruct((t, HEADS * DK), F32), jax.ShapeDtypeStruct((t, HEADS * DK), F32),
                   jax.ShapeDtypeStruct((t, HEADS * DV), F32),
                   jax.ShapeDtypeStruct((HEADS, n_chunks, 1, CHUNK), F32),
                   jax.ShapeDtypeStruct((HEADS, n_chunks, 1, CHUNK), F32)],
        scratch_shapes=[pltpu.VMEM((HB, DK, DV), F32)],
        compiler_params=_params(("arbitrary", "arbitrary")),
    )(q, k, v, beta_r, gc_r, ti, u, w, s_all, do)


def _rot(x, cos2, sin2):
    return x * cos2 + pltpu.roll(x, DK // 2, 1) * sin2


def _unrot(d, cos2, sin2):
    return d * cos2 + pltpu.roll(d * sin2, DK // 2, 1)


def _ret_consts(lg):
    r, c = _iota2(CHUNK)
    dm = jnp.where(r >= c, jnp.exp((r - c).astype(F32) * lg), 0.0)
    pos = lax.broadcasted_iota(jnp.int32, (CHUNK, 1), 0).astype(F32)
    return dm, jnp.exp((pos + 1.0) * lg), jnp.exp((CHUNK - 1.0 - pos) * lg), jnp.exp(CHUNK * lg)


def _ret_fwd(proj, cos2, sin2, lg_tab, col_q, col_k, col_v, *, name):
    t = proj.shape[0]
    n_chunks = t // CHUNK
    tok, per = _hn_specs(n_chunks)
    bq, bk, bv = col_q // (HB * DK), col_k // (HB * DK), col_v // (HB * DV)

    def body(q_ref, k_ref, v_ref, cos_ref, sin_ref, lg_ref, o_ref, s_ref, st):
        @pl.when(pl.program_id(1) == 0)
        def _():
            st[...] = jnp.zeros_like(st)

        cos2v, sin2v = cos_ref[...], sin_ref[...]
        hs = range(HB)
        s = [st[h] for h in hs]
        for h in hs:
            s_ref[h] = s[h].astype(BF16)
        cst = [_ret_consts(lg_ref[h][:, 0:1]) for h in hs]
        qq = [_rot(q_ref[:, _hcols(h, DK)], cos2v, sin2v) for h in hs]
        kk = [_rot(k_ref[:, _hcols(h, DK)], cos2v, sin2v) * (DK ** -0.5) for h in hs]
        vv = [v_ref[:, _hcols(h, DV)] for h in hs]
        sc = [_dot_nt(qq[h], kk[h]) * cst[h][0] for h in hs]
        qs = [_dot(qq[h], s[h]) for h in hs]
        kv = [_dot_tn(kk[h] * cst[h][2], vv[h]) for h in hs]
        sv = [_dot(sc[h], vv[h]) for h in hs]
        for h in hs:
            o_ref[:, _hcols(h, DV)] = sv[h] + qs[h] * cst[h][1]
            st[h] = s[h] * cst[h][3] + kv[h]

    return pl.pallas_call(
        body, name=name, grid=(HEADS // HB, n_chunks),
        in_specs=[pl.BlockSpec((CHUNK, HB * DK), lambda g, n: (n, bq + g)),
                  pl.BlockSpec((CHUNK, HB * DK), lambda g, n: (n, bk + g)),
                  pl.BlockSpec((CHUNK, HB * DV), lambda g, n: (n, bv + g)),
                  pl.BlockSpec((CHUNK, DK), lambda g, n: (n, 0)), pl.BlockSpec((CHUNK, DK), lambda g, n: (n, 0)),
                  pl.BlockSpec((HB, 1, LANES), lambda g, n: (g, 0, 0))],
        out_specs=[tok(DV), per(DK, DV)],
        out_shape=[jax.ShapeDtypeStruct((t, HEADS * DV), F32),
                   jax.ShapeDtypeStruct((HEADS, n_chunks, DK, DV), BF16)],
        scratch_shapes=[pltpu.VMEM((HB, DK, DV), F32)],
        compiler_params=_params(("arbitrary", "arbitrary")),
    )(proj, proj, proj, cos2, sin2, lg_tab)


def _ret_bwd(proj, cos2, sin2, lg_tab, s_all, do, dproj, col_q, col_k, col_v, *, name):
    t = proj.shape[0]
    n_chunks = t // CHUNK
    tok, per = _hn_specs(n_chunks, rev=True)
    bq, bk, bv = col_q // (HB * DK), col_k // (HB * DK), col_v // (HB * DV)
    rv = lambda n: n_chunks - 1 - n
    nqk, wid = HEADS * DK, HEADS * (2 * DK + DV)
    assert HB == HEADS and col_k == col_q + nqk and col_v == col_k + nqk and col_q % wid == 0

    def body(q_ref, k_ref, v_ref, cos_ref, sin_ref, lg_ref, s_ref, do_ref, alias_ref, d_ref, dst):
        @pl.when(pl.program_id(1) == 0)
        def _():
            dst[...] = jnp.zeros_like(dst)

        cos2v, sin2v = cos_ref[...], sin_ref[...]
        hs = range(HB)
        s = [s_ref[h].astype(F32) for h in hs]
        d_s = [dst[h] for h in hs]
        d_o = [do_ref[:, _hcols(h, DV)] for h in hs]
        cst = [_ret_consts(lg_ref[h][:, 0:1]) for h in hs]
        qq = [_rot(q_ref[:, _hcols(h, DK)], cos2v, sin2v) for h in hs]
        kk = [_rot(k_ref[:, _hcols(h, DK)], cos2v, sin2v) * (DK ** -0.5) for h in hs]
        vv = [v_ref[:, _hcols(h, DV)] for h in hs]
        dxo = [d_o[h] * cst[h][1] for h in hs]
        sc = [_dot_nt(qq[h], kk[h]) * cst[h][0] for h in hs]
        dsc = [_dot_nt(d_o[h], vv[h]) * cst[h][0] for h in hs]
        t1 = [_dot(kk[h] * cst[h][2], d_s[h]) for h in hs]
        t2 = [_dot_nt(dxo[h], s[h]) for h in hs]
        t3 = [_dot_nt(vv[h], d_s[h]) for h in hs]
        t4 = [_dot_tn(qq[h], dxo[h]) for h in hs]
        t5 = [_dot_tn(sc[h], d_o[h]) for h in hs]
        t6 = [_dot(dsc[h], kk[h]) for h in hs]
        t7 = [_dot_tn(dsc[h], qq[h]) for h in hs]
        for h in hs:
            dst[h] = d_s[h] * cst[h][3] + t4[h]
            d_ref[:, 2 * nqk + h * DV:2 * nqk + (h + 1) * DV] = (t5[h] + t1[h]).astype(BF16)
            d_ref[:, _hcols(h, DK)] = _unrot(t6[h] + t2[h], cos2v, sin2v).astype(BF16)
            d_ref[:, nqk + h * DK:nqk + (h + 1) * DK] = (
                _unrot(t7[h] + t3[h] * cst[h][2], cos2v, sin2v) * (DK ** -0.5)).astype(BF16)

    return pl.pallas_call(
        body, name=name, grid=(HEADS // HB, n_chunks),
        in_specs=[pl.BlockSpec((CHUNK, HB * DK), lambda g, n: (rv(n), bq + g)),
                  pl.BlockSpec((CHUNK, HB * DK), lambda g, n: (rv(n), bk + g)),
                  pl.BlockSpec((CHUNK, HB * DV), lambda g, n: (rv(n), bv + g)),
                  pl.BlockSpec((CHUNK, DK), lambda g, n: (rv(n), 0)),
                  pl.BlockSpec((CHUNK, DK), lambda g, n: (rv(n), 0)),
                  pl.BlockSpec((HB, 1, LANES), lambda g, n: (g, 0, 0)),
                  per(DK, DV), tok(DV), pl.BlockSpec(memory_space=pl.ANY)],
        out_specs=pl.BlockSpec((CHUNK, wid), lambda g, n: (rv(n), col_q // wid)),
        out_shape=jax.ShapeDtypeStruct(dproj.shape, BF16),
        scratch_shapes=[pltpu.VMEM((HB, DK, DV), F32)],
        input_output_aliases={8: 0},
        compiler_params=_params(("arbitrary", "arbitrary")),
    )(proj, proj, proj, cos2, sin2, lg_tab, s_all, do, dproj)


def _merge_parts(oa, ob, z, rg, ga, gb):
    ra = lax.rsqrt(jnp.mean(oa * oa, axis=-1, keepdims=True) + EPS)
    xa = oa * ra
    mu = jnp.mean(ob, axis=-1, keepdims=True)
    cen = ob - mu
    rb = lax.rsqrt(jnp.mean(cen * cen, axis=-1, keepdims=True) + EPS)
    xb = cen * rb
    sz, sr = _sigmoid(z), _sigmoid(rg)
    return ra, xa, rb, xb, sz, sr, _sigmoid(ga), _sigmoid(gb)


def _merge_specs(tt, d, cz, crg, cga, cgb):
    blk = pl.BlockSpec((tt, d), lambda i: (i, 0))
    pcol = lambda c: pl.BlockSpec((tt, d), lambda i: (i, c // d))
    row = pl.BlockSpec((1, d), lambda i: (0, 0))
    return blk, [blk, blk, pcol(cz), pcol(crg), pcol(cga), pcol(cgb), row, row], row


def _merge_fwd(oa, ob, proj, wa, wb, cz, crg, cga, cgb, *, name):
    t, d = oa.shape
    tt = _pick(t, (256,))
    blk, in_specs, _ = _merge_specs(tt, d, cz, crg, cga, cgb)

    def body(oa_ref, ob_ref, z_ref, rg_ref, ga_ref, gb_ref, wa_ref, wb_ref, o_ref):
        for h in range(HEADS):
            cols = slice(h * DV, (h + 1) * DV)
            z, rg = z_ref[:, cols], rg_ref[:, cols]
            _, xa, _, xb, sz, sr, sga, sgb = _merge_parts(
                oa_ref[:, cols], ob_ref[:, cols], z, rg, ga_ref[:, cols], gb_ref[:, cols])
            o_a = xa * wa_ref[:, cols] * (z * sz)
            o_b = xb * wb_ref[:, cols] * (rg * sr)
            o_ref[:, cols] = (sga * o_a + sgb * o_b).astype(BF16)

    return pl.pallas_call(
        body, name=name, grid=(t // tt,), in_specs=in_specs, out_specs=blk,
        out_shape=jax.ShapeDtypeStruct((t, d), BF16), compiler_params=_params(("parallel",)),
    )(oa, ob, proj, proj, proj, proj, wa, wb)


def _merge_bwd(dmix, oa, ob, proj, wa, wb, cz, crg, cga, cgb, *, name, side):
    t, d = oa.shape
    tt = _pick(t, (256,))
    nt = t // tt
    blk, in_specs, row = _merge_specs(tt, d, cz, crg, cga, cgb)
    assert (crg, cga, cgb) == (cz + d, cz + 2 * d, cz + 3 * d) and cz % (4 * d) == 0
    n_si, n_so = len(side.ins), len(side.out_shapes)
    assert len(side.phases) == 2

    def body(*refs):
        dm_ref, oa_ref, ob_ref, z_ref, rg_ref, ga_ref, gb_ref, wa_ref, wb_ref = refs[:9]
        s_in = refs[9:9 + n_si]
        doa_ref, dob_ref, dp_ref, dwa_ref, dwb_ref = refs[9 + n_si:14 + n_si]
        s_out, ssem, rsem = refs[14 + n_si:14 + n_si + n_so], refs[-2], refs[-1]
        dz_ref, drg_ref, dga_ref, dgb_ref = [dp_ref.at[:, i * d:(i + 1) * d] for i in range(4)]

        @pl.when(pl.program_id(0) == 0)
        def _():
            side.phases[0](s_in, s_out, ssem, rsem)
            dwa_ref[...] = jnp.zeros_like(dwa_ref)
            dwb_ref[...] = jnp.zeros_like(dwb_ref)

        for h in range(HEADS):
            cols = slice(h * DV, (h + 1) * DV)
            z, rg, wa_h, wb_h, dmx = z_ref[:, cols], rg_ref[:, cols], wa_ref[:, cols], wb_ref[:, cols], dm_ref[:, cols]
            ra, xa, rb, xb, sz, sr, sga, sgb = _merge_parts(
                oa_ref[:, cols], ob_ref[:, cols], z, rg, ga_ref[:, cols], gb_ref[:, cols])
            na, nb = xa * wa_h, xb * wb_h
            sil_z, sil_r = z * sz, rg * sr
            o_a, o_b = na * sil_z, nb * sil_r
            dga_ref[:, cols] = (dmx * o_a * sga * (1.0 - sga)).astype(BF16)
            dgb_ref[:, cols] = (dmx * o_b * sgb * (1.0 - sgb)).astype(BF16)
            d_oa, d_ob = dmx * sga, dmx * sgb
            dz_ref[:, cols] = (d_oa * na * sz * (1.0 + z * (1.0 - sz))).astype(BF16)
            drg_ref[:, cols] = (d_ob * nb * sr * (1.0 + rg * (1.0 - sr))).astype(BF16)
            dna, dnb = d_oa * sil_z, d_ob * sil_r
            dwa_ref[:, cols] += jnp.sum(dna * xa, axis=0, keepdims=True)
            dwb_ref[:, cols] += jnp.sum(dnb * xb, axis=0, keepdims=True)
            gwa, gwb = dna * wa_h, dnb * wb_h
            doa_ref[:, cols] = ra * (gwa - xa * jnp.mean(gwa * xa, axis=-1, keepdims=True))
            dob_ref[:, cols] = rb * (gwb - jnp.mean(gwb, axis=-1, keepdims=True)
                                     - xb * jnp.mean(gwb * xb, axis=-1, keepdims=True))

        @pl.when(pl.program_id(0) == nt - 1)
        def _():
            side.phases[1](s_in, s_out, ssem, rsem)

    hbm = pl.BlockSpec(memory_space=pl.ANY)
    return pl.pallas_call(
        body, name=name, grid=(nt,), in_specs=[blk] + in_specs + [hbm] * n_si,
        out_specs=[blk, blk, pl.BlockSpec((tt, 4 * d), lambda i: (i, cz // (4 * d))), row, row] + [hbm] * n_so,
        out_shape=[jax.ShapeDtypeStruct((t, d), F32)] * 2 + [jax.ShapeDtypeStruct(proj.shape, BF16)]
        + [jax.ShapeDtypeStruct((1, d), F32)] * 2 + side.out_shapes,
        scratch_shapes=side.sems(),
        compiler_params=_params(("arbitrary",)),
    )(dmix, oa, ob, proj, proj, proj, proj, wa, wb, *side.ins)


def _row_block(rows, cols, itemsize=4, target=1 << 20):
    for rb in (512, 256, 128, 64, 32, 16, 8):
        if rows % rb == 0 and rb * cols * itemsize <= target:
            return rb
    return rows


def _adamw(w, g, m, v, *, name):
    rows, cols = w.shape
    rb = _row_block(rows, cols)

    def body(w_ref, g_ref, m_ref, v_ref, d_ref, nm_ref, nv_ref):
        gg = g_ref[...]
        mm = ADAM_B1 * m_ref[...] + (1.0 - ADAM_B1) * gg
        vv = ADAM_B2 * v_ref[...] + (1.0 - ADAM_B2) * (gg * gg)
        m_hat = mm / (1.0 - ADAM_B1 ** ADAM_STEP)
        v_hat = vv / (1.0 - ADAM_B2 ** ADAM_STEP)
        d_ref[...] = -ADAM_LR * (m_hat / (jnp.sqrt(v_hat) + ADAM_EPS) + ADAM_WD * w_ref[...])
        nm_ref[...] = mm
        nv_ref[...] = vv

    blk = pl.BlockSpec((rb, cols), lambda i: (i, 0))
    return pl.pallas_call(
        body, name=name, grid=(rows // rb,), in_specs=[blk] * 4, out_specs=[blk] * 3,
        out_shape=[jax.ShapeDtypeStruct((rows, cols), F32)] * 3, compiler_params=_params(("parallel",)),
    )(w, g, m, v)


def _adamw_halves(w, mine, other, c_idx, m, v, *, name):
    rows, cols = w.shape
    hr = rows // 2
    rb = _row_block(hr, cols)
    nb = hr // rb

    def body(c_ref, w_ref, a_ref, b_ref, m_ref, v_ref, g_ref, d_ref, nm_ref, nv_ref):
        gg = jnp.where(pl.program_id(0) // nb == c_ref[0], a_ref[...], b_ref[...])
        mm = ADAM_B1 * m_ref[...] + (1.0 - ADAM_B1) * gg
        vv = ADAM_B2 * v_ref[...] + (1.0 - ADAM_B2) * (gg * gg)
        m_hat = mm / (1.0 - ADAM_B1 ** ADAM_STEP)
        v_hat = vv / (1.0 - ADAM_B2 ** ADAM_STEP)
        g_ref[...] = gg
        d_ref[...] = -ADAM_LR * (m_hat / (jnp.sqrt(v_hat) + ADAM_EPS) + ADAM_WD * w_ref[...])
        nm_ref[...] = mm
        nv_ref[...] = vv

    blk = pl.BlockSpec((rb, cols), lambda i, c: (i, 0))
    half = pl.BlockSpec((rb, cols), lambda i, c: (i % nb, 0))
    return pl.pallas_call(
        body, name=name,
        grid_spec=pltpu.PrefetchScalarGridSpec(
            num_scalar_prefetch=1, grid=(rows // rb,), in_specs=[blk, half, half, blk, blk], out_specs=[blk] * 4),
        out_shape=[jax.ShapeDtypeStruct((rows, cols), F32)] * 4, compiler_params=_params(("parallel",)),
    )(c_idx, w, mine, other, m, v)


def _pair_add(g, rsib, c_idx, *, name):
    _, _, hr, cols = g.shape
    rb = _row_block(hr, cols, 2)

    def body(c_ref, g_ref, r_ref, o_ref):
        o_ref[...] = (g_ref[...].astype(F32) + r_ref[...].astype(F32)).astype(BF16)

    return pl.pallas_call(
        body, name=name,
        grid_spec=pltpu.PrefetchScalarGridSpec(
            num_scalar_prefetch=1, grid=(N_CHIPS, hr // rb),
            in_specs=[pl.BlockSpec((None, None, rb, cols), lambda j, i, c: (j, c[0], i, 0)),
                      pl.BlockSpec((None, rb, cols), lambda j, i, c: (j, i, 0))],
            out_specs=pl.BlockSpec((None, rb, cols), lambda j, i, c: (j, i, 0))),
        out_shape=jax.ShapeDtypeStruct((N_CHIPS, hr, cols), BF16),
        compiler_params=_params(("parallel", "parallel")),
    )(c_idx, g, rsib)


def _chip_sum(qb, *, name):
    _, hr, cols = qb.shape
    rb = _row_block(hr, cols, 2, 1 << 19)

    def body(q_ref, o_ref):
        acc = q_ref[0].astype(F32)
        for i in range(1, N_CHIPS):
            acc = acc + q_ref[i].astype(F32)
        o_ref[...] = acc

    return pl.pallas_call(
        body, name=name, grid=(hr // rb,),
        in_specs=[pl.BlockSpec((N_CHIPS, rb, cols), lambda i: (0, i, 0))],
        out_specs=pl.BlockSpec((rb, cols), lambda i: (i, 0)),
        out_shape=jax.ShapeDtypeStruct((hr, cols), F32), compiler_params=_params(("parallel",)),
    )(qb)


def _place():
    x, y, c = lax.axis_index("x"), lax.axis_index("y"), lax.axis_index("c")
    return x, y, c, [(1 - x, y), (x, 1 - y), (1 - x, 1 - y)]


ANY = pl.BlockSpec(memory_space=pl.ANY)


def _gather_side(shards):
    nw, n_k = len(shards), 8

    def plan(ins, outs, ssem, rsem):
        x, y, c, _ = _place()
        nbr_x, nbr_y, sib = (1 - x, y, c), (x, 1 - y, c), (x, y, 1 - c)
        s_me, s_x, s_y, s_d = 2 * x + y, 2 * (1 - x) + y, 2 * x + 1 - y, 2 * (1 - x) + 1 - y

        def rows(w, half, quarter=None):
            hr = shards[w].shape[0] // 2
            if quarter is None:
                return pl.ds(pl.multiple_of(half * hr, 16), hr)
            return pl.ds(pl.multiple_of(half * hr + quarter * (hr // 2), 16), hr // 2)

        def rcopy(w, k, src, slot, rws, to):
            return pltpu.make_async_remote_copy(
                src_ref=src, dst_ref=outs[w].at[slot, rws], send_sem=ssem.at[w * n_k + k],
                recv_sem=rsem.at[w * n_k + k], device_id=to, device_id_type=MESH)

        out = []
        for w in range(nw):
            mine = ins[w].at[rows(w, c)]
            q = [rows(w, c, 0), rows(w, c, 1)]
            p = functools.partial
            out.append([
                (p(rcopy, w, 0, mine, s_me, rows(w, c), nbr_x), p(rcopy, w, 0, mine, s_x, rows(w, c), nbr_x)),
                (p(rcopy, w, 1, mine, s_me, rows(w, c), nbr_y), p(rcopy, w, 1, mine, s_y, rows(w, c), nbr_y)),
                (p(rcopy, w, 2, outs[w].at[s_x, q[0]], s_x, q[0], nbr_y),
                 p(rcopy, w, 2, outs[w].at[s_x, q[0]], s_d, q[0], nbr_y)),
                (p(rcopy, w, 3, outs[w].at[s_y, q[1]], s_y, q[1], nbr_x),
                 p(rcopy, w, 3, outs[w].at[s_y, q[1]], s_d, q[1], nbr_x)),
                (p(rcopy, w, 4, outs[w].at[s_x, rows(w, c)], s_x, rows(w, c), sib),
                 p(rcopy, w, 4, mine, s_x, rows(w, 1 - c), sib)),
                (p(rcopy, w, 5, outs[w].at[s_y, rows(w, c)], s_y, rows(w, c), sib),
                 p(rcopy, w, 5, mine, s_y, rows(w, 1 - c), sib)),
                (p(rcopy, w, 6, outs[w].at[s_d, q[0]], s_d, q[0], sib),
                 p(rcopy, w, 6, outs[w].at[s_d, q[0]], s_d, rows(w, 1 - c, 0), sib)),
                (p(rcopy, w, 7, outs[w].at[s_d, q[1]], s_d, q[1], sib),
                 p(rcopy, w, 7, outs[w].at[s_d, q[1]], s_d, rows(w, 1 - c, 1), sib)),
            ])
        return out

    def send_own(*refs):
        for cps in plan(*refs):
            cps[0][0]().start()
            cps[1][0]().start()

    def relay(*refs):
        for cps in plan(*refs):
            cps[0][1]().wait_recv()
            cps[2][0]().start()
            cps[4][0]().start()
            cps[1][1]().wait_recv()
            cps[3][0]().start()
            cps[5][0]().start()

    def pass_diagonal(*refs):
        for cps in plan(*refs):
            cps[2][1]().wait_recv()
            cps[6][0]().start()
            cps[3][1]().wait_recv()
            cps[7][0]().start()

    def finish(*refs):
        for cps in plan(*refs):
            for k in (4, 5, 6, 7):
                cps[k][1]().wait_recv()
            for k in range(n_k):
                cps[k][0]().wait_send()

    return _Side(shards, [jax.ShapeDtypeStruct((N_CHIPS,) + s.shape, s.dtype) for s in shards], nw * n_k,
                 [send_own, relay, pass_diagonal, finish])


def _run_side(side, *, name):
    n_in, n_out = len(side.ins), len(side.out_shapes)

    def body(*refs):
        ins, outs = refs[:n_in], refs[n_in:n_in + n_out]
        for phase in side.phases:
            phase(ins, outs, refs[-2], refs[-1])

    return pl.pallas_call(
        body, name=name, in_specs=[ANY] * n_in, out_specs=[ANY] * n_out, out_shape=side.out_shapes,
        scratch_shapes=side.sems(), compiler_params=pltpu.CompilerParams(has_side_effects=True),
    )(*side.ins)


def _exchange_side(gs):
    nw = len(gs)

    def copies(ins, outs, ssem, rsem):
        x, y, c, _ = _place()
        return [pltpu.make_async_remote_copy(
            src_ref=ins[w].at[:, 1 - c], dst_ref=outs[w], send_sem=ssem.at[w], recv_sem=rsem.at[w],
            device_id=(x, y, 1 - c), device_id_type=MESH) for w in range(nw)]

    def start(*refs):
        for cp in copies(*refs):
            cp.start()

    def finish(*refs):
        for cp in copies(*refs):
            cp.wait()

    return _Side(gs, [jax.ShapeDtypeStruct((g.shape[0],) + g.shape[2:], g.dtype) for g in gs], nw, [start, finish])


def _scatter_side(ps):
    nw = len(ps)

    def copies(ins, outs, ssem, rsem):
        x, y, c, chips = _place()
        s_me = 2 * x + y
        return [pltpu.make_async_remote_copy(
            src_ref=ins[w].at[2 * chip[0] + chip[1]], dst_ref=outs[w].at[s_me],
            send_sem=ssem.at[w * 3 + j], recv_sem=rsem.at[w * 3 + j],
            device_id=(chip[0], chip[1], c), device_id_type=MESH)
            for w in range(nw) for j, chip in enumerate(chips)]

    def start(*refs):
        for cp in copies(*refs):
            cp.start()

    def finish(*refs):
        for cp in copies(*refs):
            cp.wait()

    return _Side(ps, [jax.ShapeDtypeStruct(p.shape, p.dtype) for p in ps], nw * 3, [start, finish])


def _pair_share(hs, *, name):
    nw = len(hs)

    def body(*refs):
        ins, outs = refs[:nw], refs[nw:2 * nw]
        ssem, rsem = refs[2 * nw:]
        x, y, c, _ = _place()
        cps = []
        for w in range(nw):
            cp = pltpu.make_async_remote_copy(
                src_ref=ins[w], dst_ref=outs[w], send_sem=ssem.at[w], recv_sem=rsem.at[w],
                device_id=(x, y, 1 - c), device_id_type=MESH)
            cp.start()
            cps.append(cp)
        for cp in cps:
            cp.wait()

    return pl.pallas_call(
        body, name=name, in_specs=[ANY] * nw, out_specs=[ANY] * nw,
        out_shape=[jax.ShapeDtypeStruct(h.shape, h.dtype) for h in hs],
        scratch_shapes=[pltpu.SemaphoreType.DMA((nw,)), pltpu.SemaphoreType.DMA((nw,))],
        compiler_params=pltpu.CompilerParams(has_side_effects=True),
    )(*hs)


def _gather_small(a, *, name):
    def body(a_ref, o_ref, ssem, rsem):
        x, y, c, chips = _place()
        s_me = 2 * x + y
        o_ref[s_me] = a_ref[...]
        cps = []
        for j, chip in enumerate(chips):
            cp = pltpu.make_async_remote_copy(
                src_ref=a_ref, dst_ref=o_ref.at[s_me], send_sem=ssem.at[j], recv_sem=rsem.at[j],
                device_id=(chip[0], chip[1], c), device_id_type=MESH)
            cp.start()
            cps.append(cp)
        for cp in cps:
            cp.wait()

    vm = pl.BlockSpec(memory_space=pltpu.VMEM)
    return pl.pallas_call(
        body, name=name, in_specs=[vm], out_specs=vm,
        out_shape=jax.ShapeDtypeStruct((N_CHIPS,) + a.shape, a.dtype),
        scratch_shapes=[pltpu.SemaphoreType.DMA((3,)), pltpu.SemaphoreType.DMA((3,))],
    )(a)


def _allreduce_small(p, *, name):
    def body(p_ref, o_ref, buf, ssem, rsem):
        x, y, c, _ = _place()
        me = 4 * x + 2 * y + c
        buf[me] = p_ref[...]
        cps = []
        for k in range(1, N_DEV):
            fx, fy, fc = (k >> 2) & 1, (k >> 1) & 1, k & 1
            peer = (x + fx - 2 * x * fx, y + fy - 2 * y * fy, c + fc - 2 * c * fc)
            cp = pltpu.make_async_remote_copy(
                src_ref=buf.at[me], dst_ref=buf.at[me], send_sem=ssem.at[k - 1], recv_sem=rsem.at[k - 1],
                device_id=peer, device_id_type=MESH)
            cp.start()
            cps.append(cp)
        for cp in cps:
            cp.wait()
        acc = buf[0]
        for i in range(1, N_DEV):
            acc = acc + buf[i]
        o_ref[...] = acc

    vm = pl.BlockSpec(memory_space=pltpu.VMEM)
    return pl.pallas_call(
        body, name=name, in_specs=[vm], out_specs=vm,
        out_shape=jax.ShapeDtypeStruct(p.shape, p.dtype),
        scratch_shapes=[pltpu.VMEM((N_DEV,) + p.shape, p.dtype), pltpu.SemaphoreType.DMA((N_DEV - 1,)),
                        pltpu.SemaphoreType.DMA((N_DEV - 1,))],
    )(p)


def _rows_to_tokens(r):
    h, n = r.shape[0], r.shape[1]
    return r.reshape(h, n * CHUNK).T


def _tokens_to_rows(a):
    t, h = a.shape
    return a.T.reshape(h, t // CHUNK, 1, CHUNK)


def kernel(x, norm1_w, w_in, conv_w, a_log, dt_bias, gdn_norm_w, ret_norm_w, w_out, norm2_w, w_gate, w_up, w_down, norm_f_w, loss_target, m_norm1_w, m_w_in, m_conv_w, m_a_log, m_dt_bias, m_gdn_norm_w, m_ret_norm_w, m_w_out, m_norm2_w, m_w_gate, m_w_up, m_w_down, m_norm_f_w, v_norm1_w, v_w_in, v_conv_w, v_a_log, v_dt_bias, v_gdn_norm_w, v_ret_norm_w, v_w_out, v_norm2_w, v_w_gate, v_w_up, v_w_down, v_norm_f_w):
    t, d = x.shape[1], x.shape[2]
    f = w_gate.shape[2] * N_CHIPS
    nqk, nv = HEADS * DK, HEADS * DV
    ncs = w_in.shape[2]
    c_idx = lax.axis_index("c")
    s_idx = 2 * lax.axis_index("x") + lax.axis_index("y")
    xs = x[0]
    tgt = loss_target[0]

    n_small = 2 * HEADS
    widths = [2 * nqk + nv, nv, n_small, nqk, nqk, nv, nv, d, d]
    g_off = np.concatenate([[0], np.cumsum(widths)])
    order = [0, 3, 4, 5, 1, 6, 7, 8]
    m_off = np.concatenate([[0], np.cumsum([widths[i] for i in order])])
    o_rq, o_rk, o_rv, o_az, o_rg, o_ga, o_gb = [int(m_off[i]) for i in range(1, 8)]
    segs = [(int(g_off[i]), int(g_off[i + 1]), int(m_off[order.index(i)]) if i != 2 else None) for i in range(9)]

    def shard_pieces(s):
        out = []
        for a, b, dst in segs:
            lo, hi = max(a, s * ncs), min(b, (s + 1) * ncs)
            if lo < hi:
                out.append((lo - s * ncs, hi - s * ncs, None if dst is None else dst + lo - a))
        return out

    own = [w_in[0].astype(BF16), w_out[0].astype(BF16), w_gate[0].astype(BF16), w_up[0].astype(BF16),
           w_down[0].astype(BF16)]
    put_own = lambda full, mine: lax.dynamic_update_slice(full, mine[None], (s_idx, 0, 0))
    wg_in = put_own(_run_side(_gather_side(own[:1]), name="gather_w_in")[0], own[0])
    conv_full = _gather_small(conv_w[0], name="gather_conv_w")
    conv_full = jnp.concatenate([conv_full[i] for i in range(N_CHIPS)], axis=1)
    cuts = [(dst, s, a, b) for s in range(N_CHIPS) for a, b, dst in shard_pieces(s)]
    w_main = jnp.concatenate([wg_in[s][:, a:b] for dst, s, a, b in sorted(c for c in cuts if c[0] is not None)],
                             axis=1)
    w_small = jnp.concatenate([wg_in[s][:, a:b] for dst, s, a, b in cuts if dst is None], axis=1)
    w_small = jnp.pad(w_small, ((0, 0), (0, LANES - n_small)))

    pad16 = lambda a: jnp.pad(a, ((0, 0), (HEADS, LANES - 2 * HEADS)))
    alog_row, dtb_row = pad16(a_log), pad16(dt_bias)
    wa_row = jnp.tile(gdn_norm_w, (1, HEADS))
    inv = ROPE_BASE ** (-jnp.arange(0, DK, 2, dtype=F32) / DK)
    ang = jnp.arange(t, dtype=F32)[:, None] * inv[None, :]
    cos2 = jnp.concatenate([jnp.cos(ang), jnp.cos(ang)], axis=1)
    sin2 = jnp.concatenate([-jnp.sin(ang), jnp.sin(ang)], axis=1)
    lg = jnp.log1p(-jnp.exp2(-5.0 - jnp.arange(HEADS, dtype=F32)))
    lg_tab = jnp.broadcast_to(lg[:, None, None], (HEADS, 1, LANES))

    fq = f // N_CHIPS
    u1, projs = _rms_fwd(xs, norm1_w, w_narrow=w_small, name="rms1_fwd")
    proj, *rest = _mm(u1, w_main, tm=512, tn=2048, tk=d, side=_gather_side(own[1:]), name="mm_proj")
    wg_out, wg_gate, wg_up, wg_down = [put_own(g, o) for g, o in zip(rest, own[1:])]
    w_o = wg_out.reshape(d, d)
    w_g, w_u = wg_gate, wg_up
    w_d = wg_down.reshape(f, d)
    q_a, k_a, v_a = _conv_fwd(proj, conv_full, name="conv_fwd")
    bg = _bg_fwd(projs, alog_row, dtb_row, name="bg_fwd")
    beta_r = _tokens_to_rows(bg[:, :HEADS])
    g_r = _tokens_to_rows(bg[:, HEADS:2 * HEADS])
    gc_r, tinv, u_a, w_a, p_a = _gdn_prep(q_a, k_a, v_a, beta_r, g_r, name="gdn_prep")
    o_a, s_a = _gdn_scan(q_a, k_a, u_a, w_a, p_a, gc_r, name="gdn_scan")
    o_b, s_b = _ret_fwd(proj, cos2, sin2, lg_tab, o_rq, o_rk, o_rv, name="ret_fwd")
    mixed = _merge_fwd(o_a, o_b, proj, wa_row, ret_norm_w, o_az, o_rg, o_ga, o_gb, name="merge_fwd")
    h1 = _mm(mixed, w_o, res=xs, tm=512, tn=d, tk=d, name="mm_out")
    hn = _rms_fwd(h1, norm2_w, name="rms2_fwd")
    gt, up, act = _ffn_in(hn, w_g, w_u, tm=256, name="ffn_in")
    h2 = _mm(act, w_d, res=h1, tm=512, tn=1024, tk=f, name="mm_down")
    loss_row, dh2, dh2b, d_nf = _loss_head(h2, tgt, norm_f_w.reshape(1, d), name="loss_head")

    g_down = _mm(act, dh2b, ta=True, out_dtype=BF16, tm=fq, tn=d, tk=1024, name="mm_dw_down")
    dgt, dup = _ffn_back(dh2b, w_d, gt, up, tm=512, tn=fq, name="ffn_back")
    dhn = _mm(dgt, w_g, tb=True, pair=(dup, w_u), tm=512, tn=d, tk=fq, name="mm_dhn")
    g_gate = _mm(hn, dgt, ta=True, out_dtype=BF16, out_stacked=True, tm=1024, tn=fq, tk=2048, name="mm_dw_gate")
    g_up = _mm(hn, dup, ta=True, out_dtype=BF16, out_stacked=True, tm=1024, tn=fq, tk=2048, name="mm_dw_up")
    dh1, dh1b, d_n2 = _rms_bwd(dhn, h1, norm2_w, dh2, name="rms2_bwd")
    dmix = _mm(dh1b, w_o, tb=True, tm=512, tn=d, tk=d, name="mm_dmix")
    g_out = _mm(mixed, dh1b, ta=True, out_dtype=BF16, tm=1024, tn=d, tk=1024, name="mm_dw_out")
    halves = lambda g: g.reshape(N_CHIPS, 2, g.shape[1] // 2, g.shape[2])
    c_arr = jnp.reshape(c_idx, (1,)).astype(jnp.int32)
    gs_ffn = [halves(g_out.reshape(N_CHIPS, d // N_CHIPS, d)), halves(g_gate), halves(g_up),
              halves(g_down.reshape(N_CHIPS, fq, d))]
    do_a, do_b, dproj, d_wa, d_wb, *rsib = _merge_bwd(
        dmix, o_a, o_b, proj, wa_row, ret_norm_w, o_az, o_rg, o_ga, o_gb, side=_exchange_side(gs_ffn),
        name="merge_bwd")
    ps_ffn = [_pair_add(g, r, c_arr, name=f"grad_pair_add_{nm}")
              for g, r, nm in zip(gs_ffn, rsib, ["w_out", "w_gate", "w_up", "w_down"])]
    dproj = _ret_bwd(proj, cos2, sin2, lg_tab, s_b, do_b, dproj, o_rq, o_rk, o_rv, name="ret_bwd")
    dq_a, dk_a, dv_a, dbeta_r, dg_r = _gdn_bwd(q_a, k_a, v_a, beta_r, gc_r, tinv, u_a, w_a, s_a, do_a, name="gdn_bwd")
    dc = _conv_bwd_pre(proj, conv_full, dq_a, dk_a, dv_a, name="conv_bwd_pre")
    dproj, d_cw = _conv_bwd(proj, dc, conv_full, dproj, name="conv_bwd")
    dbg = jnp.pad(jnp.concatenate([_rows_to_tokens(dbeta_r), _rows_to_tokens(dg_r)], axis=1),
                  ((0, 0), (0, LANES - 2 * HEADS)))
    dprojs, d_alog, d_dtb = _bg_bwd(projs, dbg, alog_row, dtb_row, name="bg_bwd")
    g_main, *qs_ffn = _mm(u1, dproj, ta=True, out_dtype=BF16, tm=1024, tn=2048, tk=2048,
                          side=_scatter_side(ps_ffn), name="mm_dw_in")
    g_small = _mm(u1, dprojs, ta=True, out_dtype=BF16, tm=1024, tn=LANES, tk=1024, name="mm_dw_in_small")
    pieces = []
    for s in range(N_CHIPS):
        seen, parts = 0, []
        for a, b, dst in shard_pieces(s):
            parts.append(g_small[:, seen:seen + b - a] if dst is None else g_main[:, dst:dst + b - a])
            seen += (b - a) if dst is None else 0
        pieces.append(jnp.concatenate(parts, axis=1))
    gs_in = [halves(jnp.stack(pieces))]
    rsib = _run_side(_exchange_side(gs_in), name="grad_pair_exchange_in")
    ps_in = [_pair_add(gs_in[0], rsib[0], c_arr, name="grad_pair_add_w_in")]
    du, *qs_in = _mm(dproj, w_main, tb=True, out_dtype=BF16, tm=1024, tn=d, tk=2048,
                     side=_scatter_side(ps_in), name="mm_du")
    dx, _, d_n1 = _rms_bwd(du, xs, norm1_w, dh1, narrow=(dprojs, w_small), name="rms1_bwd")

    names = ["w_in", "w_out", "w_gate", "w_up", "w_down"]
    qs = [lax.dynamic_update_slice(q, lax.dynamic_slice(p, (s_idx, 0, 0), (1,) + p.shape[1:]), (s_idx, 0, 0))
          for q, p in zip(qs_in + qs_ffn, ps_in + ps_ffn)]
    hs = [_chip_sum(q, name=f"grad_chip_sum_{nm}") for q, nm in zip(qs, names)]
    theirs = _pair_share(hs, name="grad_pair_share")
    big_w = [w_in[0], w_out[0], w_gate[0], w_up[0], w_down[0]]
    big_m = [m_w_in[0], m_w_out[0], m_w_gate[0], m_w_up[0], m_w_down[0]]
    big_v = [v_w_in[0], v_w_out[0], v_w_gate[0], v_w_up[0], v_w_down[0]]
    big = {}
    for nm, mine, other, w_, m_, v_ in zip(names, hs, theirs, big_w, big_m, big_v):
        big[nm] = tuple(a[None] for a in _adamw_halves(w_, mine, other, c_arr, m_, v_, name=f"adamw_{nm}"))

    d_wa_h = jnp.sum(d_wa.reshape(HEADS, DV), axis=0, keepdims=True)
    small = [d_n1, d_alog[:, HEADS:2 * HEADS], d_dtb[:, HEADS:2 * HEADS], d_wa_h, d_wb, d_n2, d_nf,
             d_cw[:CONV_W].reshape(1, -1)]
    sizes = [a.shape[1] for a in small]
    packed = jnp.concatenate(small, axis=1)
    n_pack = packed.shape[1]
    n_rows = -(-n_pack // LANES)
    n_rows = -(-n_rows // 8) * 8
    packed = jnp.pad(packed, ((0, 0), (0, n_rows * LANES - n_pack))).reshape(n_rows, LANES)
    red = _allreduce_small(packed, name="allreduce_small").reshape(1, -1)
    offs = np.cumsum([0] + sizes)
    g_n1, g_alog, g_dtb, g_wa, g_wb, g_n2, g_nf, g_cw = [red[:, offs[i]:offs[i + 1]] for i in range(len(sizes))]
    ncw = conv_w.shape[2]
    g_cw = lax.dynamic_slice(g_cw.reshape(CONV_W, -1), (0, s_idx * ncw), (CONV_W, ncw))

    def small_update(w_, g, m_, v_, nm):
        shape = w_.shape
        pad = (-w_.size) % LANES
        to2 = lambda a: jnp.pad(a.reshape(1, -1), ((0, 0), (0, pad)))
        outs = _adamw(to2(w_), to2(g), to2(m_), to2(v_), name=f"adamw_{nm}")
        return (g.reshape(shape),) + tuple(a[:, :w_.size].reshape(shape) for a in outs)

    res = {
        "norm1_w": small_update(norm1_w, g_n1, m_norm1_w, v_norm1_w, "norm1_w"),
        "w_in": big["w_in"],
        "conv_w": small_update(conv_w, g_cw, m_conv_w, v_conv_w, "conv_w"),
        "a_log": small_update(a_log, g_alog, m_a_log, v_a_log, "a_log"),
        "dt_bias": small_update(dt_bias, g_dtb, m_dt_bias, v_dt_bias, "dt_bias"),
        "gdn_norm_w": small_update(gdn_norm_w, g_wa, m_gdn_norm_w, v_gdn_norm_w, "gdn_norm_w"),
        "ret_norm_w": small_update(ret_norm_w, g_wb, m_ret_norm_w, v_ret_norm_w, "ret_norm_w"),
        "w_out": big["w_out"],
        "norm2_w": small_update(norm2_w, g_n2, m_norm2_w, v_norm2_w, "norm2_w"),
        "w_gate": big["w_gate"],
        "w_up": big["w_up"],
        "w_down": big["w_down"],
        "norm_f_w": small_update(norm_f_w, g_nf, m_norm_f_w, v_norm_f_w, "norm_f_w"),
    }
    order = ["norm1_w", "w_in", "conv_w", "a_log", "dt_bias", "gdn_norm_w", "ret_norm_w", "w_out", "norm2_w",
             "w_gate", "w_up", "w_down", "norm_f_w"]
    loss = lax.psum(loss_row[0, 0], ("x", "y", "c"))
    return (loss, dx[None], *[res[n][0] for n in order], *[res[n][1] for n in order],
            *[res[n][2] for n in order], *[res[n][3] for n in order])
```

```python
import functools

import jax
import jax.numpy as jnp
import numpy as np
from jax import lax
from jax.experimental import pallas as pl
from jax.experimental.pallas import tpu as pltpu

F32 = jnp.float32
BF16 = jnp.bfloat16
MESH = pl.DeviceIdType.MESH

HEADS = 8
DK = 128
DV = 256
CHUNK = 64
CONV_W = 4
EPS = 1e-6
ROPE_BASE = 10000.0
ADAM_LR, ADAM_B1, ADAM_B2, ADAM_EPS, ADAM_WD, ADAM_STEP = 0.001, 0.9, 0.999, 1e-08, 0.01, 10
N_CHIPS = 4
N_DEV = 8
LANES = 128
HALO = 8
VMEM_LIMIT = 56 * 1024 * 1024
HB = 8


def _pick(n, cands):
    for c in cands:
        if n % c == 0:
            return c
    raise ValueError(f"no tile for {n} in {cands}")


def _params(sem=None):
    return pltpu.CompilerParams(dimension_semantics=sem, vmem_limit_bytes=VMEM_LIMIT)


def _dot(a, b):
    return jnp.dot(a.astype(BF16), b.astype(BF16), preferred_element_type=F32)


def _dot_nt(a, b):
    return lax.dot_general(a.astype(BF16), b.astype(BF16), (((1,), (1,)), ((), ())), preferred_element_type=F32)


def _dot_tn(a, b):
    return lax.dot_general(a.astype(BF16), b.astype(BF16), (((0,), (0,)), ((), ())), preferred_element_type=F32)


def _sigmoid(x):
    return 1.0 / (1.0 + jnp.exp(-x))


def _iota2(n):
    return lax.broadcasted_iota(jnp.int32, (n, n), 0), lax.broadcasted_iota(jnp.int32, (n, n), 1)


def _row_to_col(row):
    n = row.shape[1]
    r, c = _iota2(n)
    return jnp.sum(jnp.where(r == c, jnp.broadcast_to(row, (n, n)), 0.0), axis=1, keepdims=True)


def _col_to_row(col):
    n = col.shape[0]
    r, c = _iota2(n)
    return jnp.sum(jnp.where(r == c, jnp.broadcast_to(col, (n, n)), 0.0), axis=0, keepdims=True)


class _Side:
    def __init__(self, ins, out_shapes, n_sem, phases):
        self.ins, self.out_shapes, self.n_sem, self.phases = list(ins), list(out_shapes), n_sem, phases

    def sems(self):
        return [pltpu.SemaphoreType.DMA((self.n_sem,)), pltpu.SemaphoreType.DMA((self.n_sem,))]


def _mm(a, b, *, name, tm, tn, tk, ta=False, tb=False, out_dtype=F32, res=None, side=None, pair=None,
        out_stacked=False):
    m, k = (a.shape[1], a.shape[0]) if ta else a.shape
    b_slots = b.ndim == 3
    if b_slots:
        assert tb and tk == b.shape[2]
        n = b.shape[1]
    else:
        n = b.shape[0] if tb else b.shape[1]
    tm, tn, tk = min(tm, m), min(tn, n), min(tk, k)
    assert m % tm == 0 and n % tn == 0 and k % tk == 0, (name, m, n, k)
    nk = k // tk
    nj, ni = n // tn, m // tm
    dn = (((0 if ta else 1,), (1 if tb else 0,)), ((), ()))
    n_ab = 4 if pair else 2
    n_in = n_ab + (res is not None)
    n_side_in = len(side.ins) if side else 0
    n_side_out = len(side.out_shapes) if side else 0

    def body(*refs):
        a_ref, b_ref = refs[0], refs[1]
        r_ref = refs[n_ab] if res is not None else None
        o_ref = refs[n_in + n_side_in]
        if side:
            s_in = refs[n_in:n_in + n_side_in]
            s_out = refs[n_in + n_side_in + 1:n_in + n_side_in + 1 + n_side_out]
            ssem, rsem = refs[-2], refs[-1]
            j_, i_, k_ = pl.program_id(0), pl.program_id(1), pl.program_id(2)
            n_mid = len(side.phases) - 2
            assert n_mid == 0 or nj >= 2 * n_mid
            when = [(j_ == 0) & (i_ == 0) & (k_ == 0)]
            when += [(j_ == nj // 2 + (p * (nj // 2)) // n_mid) & (i_ == 0) & (k_ == 0) for p in range(n_mid)]
            when.append((j_ == nj - 1) & (i_ == ni - 1) & (k_ == nk - 1))

        def run_phase(p):
            @pl.when(when[p])
            def _():
                side.phases[p](s_in, s_out, ssem, rsem)

        if side:
            for p in range(len(side.phases) - 1):
                run_phase(p)

        def finish(r):
            if res is not None:
                r = r + r_ref[...]
            o_ref[...] = r.astype(out_dtype)

        part = lax.dot_general(a_ref[...], b_ref[...], dn, preferred_element_type=F32)
        if pair:
            part = part + lax.dot_general(refs[2][...], refs[3][...], dn, preferred_element_type=F32)
        if nk == 1:
            finish(part)
        else:
            acc = refs[n_in + n_side_in + 1 + n_side_out]
            kk = pl.program_id(2)

            @pl.when(kk == 0)
            def _():
                acc[...] = part

            @pl.when((kk > 0) & (kk < nk - 1))
            def _():
                acc[...] += part

            @pl.when(kk == nk - 1)
            def _():
                finish(acc[...] + part)

        if side:
            run_phase(len(side.phases) - 1)

    a_spec = pl.BlockSpec((tk, tm), lambda j, i, kk: (kk, i)) if ta else pl.BlockSpec((tm, tk), lambda j, i, kk: (i, kk))
    b_spec = pl.BlockSpec((tn, tk), lambda j, i, kk: (j, kk)) if tb else pl.BlockSpec((tk, tn), lambda j, i, kk: (kk, j))
    if b_slots:
        b_spec = pl.BlockSpec((None, tn, tk), lambda j, i, kk: (kk, j, 0))
    o_spec = pl.BlockSpec((tm, tn), lambda j, i, kk: (i, j))
    o_shape = jax.ShapeDtypeStruct((m, n), out_dtype)
    if out_stacked:
        assert res is None
        o_spec = pl.BlockSpec((None, tm, tn), lambda j, i, kk: (j, i, 0))
        o_shape = jax.ShapeDtypeStruct((nj, m, tn), out_dtype)
    in_specs, args = [a_spec, b_spec], [a, b]
    if pair:
        assert pair[0].shape == a.shape and pair[1].shape == b.shape
        in_specs += [a_spec, b_spec]
        args += list(pair)
    if res is not None:
        in_specs.append(o_spec)
        args.append(res)
    out_specs, out_shape = o_spec, o_shape
    scratch = [pltpu.VMEM((tm, tn), F32)] if nk > 1 else []
    sem = ("parallel", "parallel", "arbitrary")
    if side:
        hbm = pl.BlockSpec(memory_space=pl.ANY)
        in_specs += [hbm] * n_side_in
        args += side.ins
        out_specs, out_shape = [o_spec] + [hbm] * n_side_out, [out_shape] + side.out_shapes
        scratch += side.sems()
        sem = ("arbitrary",) * 3
    return pl.pallas_call(
        body, name=name, grid=(nj, ni, nk), in_specs=in_specs, out_specs=out_specs, out_shape=out_shape,
        scratch_shapes=scratch, compiler_params=_params(sem),
    )(*args)


def _rms_fwd(x, w, *, name, w_narrow=None):
    t, d = x.shape
    tt = _pick(t, (512, 256))

    def body(*refs):
        x_ref, w_ref, o_ref = refs[0], refs[1], refs[-2 if w_narrow is not None else -1]
        xv = x_ref[...]
        r = lax.rsqrt(jnp.mean(xv * xv, axis=-1, keepdims=True) + EPS)
        u = (xv * r * w_ref[...]).astype(BF16)
        o_ref[...] = u
        if w_narrow is not None:
            refs[-1][...] = jnp.dot(u, refs[2][...], preferred_element_type=F32)

    blk = pl.BlockSpec((tt, d), lambda i: (i, 0))
    in_specs, args = [blk, pl.BlockSpec((1, d), lambda i: (0, 0))], [x, w]
    out_specs, out_shape = [blk], [jax.ShapeDtypeStruct((t, d), BF16)]
    if w_narrow is not None:
        in_specs.append(pl.BlockSpec(w_narrow.shape, lambda i: (0, 0)))
        args.append(w_narrow)
        out_specs.append(pl.BlockSpec((tt, LANES), lambda i: (i, 0)))
        out_shape.append(jax.ShapeDtypeStruct((t, LANES), F32))
    out = pl.pallas_call(
        body, name=name, grid=(t // tt,), in_specs=in_specs, out_specs=out_specs, out_shape=out_shape,
        compiler_params=_params(("parallel",)),
    )(*args)
    return out if w_narrow is not None else out[0]


def _rms_bwd(dn, x, w, dres, *, name, narrow=None):
    t, d = x.shape
    tt = _pick(t, (256,))

    def body(*refs):
        dn_ref, x_ref, w_ref, dres_ref = refs[:4]
        dx_ref, dxb_ref, dw_ref = refs[-3:]
        xv, g = x_ref[...], dn_ref[...].astype(F32)
        if narrow is not None:
            g = g + lax.dot_general(refs[4][...], refs[5][...], (((1,), (1,)), ((), ())), preferred_element_type=F32)
        r = lax.rsqrt(jnp.mean(xv * xv, axis=-1, keepdims=True) + EPS)
        xh = xv * r
        gw = g * w_ref[...]
        dx = dres_ref[...] + r * (gw - xh * jnp.mean(gw * xh, axis=-1, keepdims=True))
        dx_ref[...] = dx
        dxb_ref[...] = dx.astype(BF16)

        @pl.when(pl.program_id(0) == 0)
        def _():
            dw_ref[...] = jnp.zeros_like(dw_ref)

        dw_ref[...] += jnp.sum(g * xh, axis=0, keepdims=True)

    blk = pl.BlockSpec((tt, d), lambda i: (i, 0))
    row = pl.BlockSpec((1, d), lambda i: (0, 0))
    in_specs, args = [blk, blk, row, blk], [dn, x, w, dres]
    if narrow is not None:
        in_specs += [pl.BlockSpec((tt, LANES), lambda i: (i, 0)), pl.BlockSpec(narrow[1].shape, lambda i: (0, 0))]
        args += list(narrow)
    return pl.pallas_call(
        body, name=name, grid=(t // tt,), in_specs=in_specs, out_specs=[blk, blk, row],
        out_shape=[jax.ShapeDtypeStruct((t, d), F32), jax.ShapeDtypeStruct((t, d), BF16),
                   jax.ShapeDtypeStruct((1, d), F32)],
        compiler_params=_params(("arbitrary",)),
    )(*args)


def _loss_head(h2, tgt, wf, *, name):
    t, d = h2.shape
    tt = _pick(t, (256,))

    def body(x_ref, t_ref, w_ref, loss_ref, dx_ref, dxb_ref, dw_ref):
        xv = x_ref[...]
        r = lax.rsqrt(jnp.mean(xv * xv, axis=-1, keepdims=True) + EPS)
        xh = xv * r
        err = xh * w_ref[...] - t_ref[...]
        lpart = 0.5 * jnp.sum(jnp.mean(err * err, axis=-1, keepdims=True), axis=0, keepdims=True)
        dy = err * (1.0 / d)
        gw = dy * w_ref[...]
        dx = r * (gw - xh * jnp.mean(gw * xh, axis=-1, keepdims=True))
        dx_ref[...] = dx
        dxb_ref[...] = dx.astype(BF16)

        @pl.when(pl.program_id(0) == 0)
        def _():
            dw_ref[...] = jnp.zeros_like(dw_ref)
            loss_ref[...] = jnp.zeros_like(loss_ref)

        dw_ref[...] += jnp.sum(dy * xh, axis=0, keepdims=True)
        loss_ref[...] += jnp.broadcast_to(lpart, loss_ref.shape)

    blk = pl.BlockSpec((tt, d), lambda i: (i, 0))
    row = pl.BlockSpec((1, d), lambda i: (0, 0))
    lrow = pl.BlockSpec((1, LANES), lambda i: (0, 0))
    return pl.pallas_call(
        body, name=name, grid=(t // tt,),
        in_specs=[blk, blk, row], out_specs=[lrow, blk, blk, row],
        out_shape=[jax.ShapeDtypeStruct((1, LANES), F32), jax.ShapeDtypeStruct((t, d), F32),
                   jax.ShapeDtypeStruct((t, d), BF16), jax.ShapeDtypeStruct((1, d), F32)],
        compiler_params=_params(("arbitrary",)),
    )(h2, tgt, wf)


def _ffn_in(hn, w_g, w_u, *, name, tm):
    t, d = hn.shape
    tn = w_g.shape[2]
    f = w_g.shape[0] * tn
    tm = min(tm, t)

    def body(a_ref, g_ref, u_ref, gt_ref, up_ref, act_ref):
        a = a_ref[...]
        g = jnp.dot(a, g_ref[...], preferred_element_type=F32)
        u = jnp.dot(a, u_ref[...], preferred_element_type=F32)
        gt_ref[...] = g.astype(BF16)
        up_ref[...] = u.astype(BF16)
        act_ref[...] = (g * _sigmoid(g) * u).astype(BF16)

    wblk = pl.BlockSpec((None, d, tn), lambda j, i: (j, 0, 0))
    oblk = pl.BlockSpec((tm, tn), lambda j, i: (i, j))
    return pl.pallas_call(
        body, name=name, grid=(f // tn, t // tm),
        in_specs=[pl.BlockSpec((tm, d), lambda j, i: (i, 0)), wblk, wblk], out_specs=[oblk] * 3,
        out_shape=[jax.ShapeDtypeStruct((t, f), BF16)] * 3,
        compiler_params=_params(("parallel", "parallel")),
    )(hn, w_g, w_u)


def _ffn_back(dh2b, w_d, gt, up, *, name, tm, tn):
    t, d = dh2b.shape
    f = w_d.shape[0]
    tm = min(tm, t)

    n_sub = 2 if tm % 32 == 0 else 1

    def body(a_ref, w_ref, g_ref, u_ref, dg_ref, du_ref):
        w = w_ref[...]
        subs = [slice(i * (tm // n_sub), (i + 1) * (tm // n_sub)) for i in range(n_sub)]
        das = [lax.dot_general(a_ref[rs, :], w, (((1,), (1,)), ((), ())), preferred_element_type=F32) for rs in subs]
        for rs, da in zip(subs, das):
            g = g_ref[rs, :].astype(F32)
            sg = _sigmoid(g)
            dg_ref[rs, :] = (da * u_ref[rs, :].astype(F32) * sg * (1.0 + g * (1.0 - sg))).astype(BF16)
            du_ref[rs, :] = (da * g * sg).astype(BF16)

    oblk = pl.BlockSpec((tm, tn), lambda j, i: (i, j))
    return pl.pallas_call(
        body, name=name, grid=(f // tn, t // tm),
        in_specs=[pl.BlockSpec((tm, d), lambda j, i: (i, 0)), pl.BlockSpec((tn, d), lambda j, i: (j, 0)), oblk, oblk],
        out_specs=[oblk] * 2,
        out_shape=[jax.ShapeDtypeStruct((t, f), BF16)] * 2,
        compiler_params=_params(("parallel", "parallel")),
    )(dh2b, w_d, gt, up)


def _conv_cols(xs, cw_ref, cb, tt):
    cols = slice(cb * LANES, (cb + 1) * LANES)
    base = HALO - (CONV_W - 1)
    acc = xs[base:base + tt, cols] * cw_ref[0:1, cols]
    for j in range(1, CONV_W):
        acc = acc + xs[base + j:base + j + tt, cols] * cw_ref[j:j + 1, cols]
    return acc


def _fill_halo(xs, x_ref, xp_ref, tt):
    first = pl.program_id(0) == 0
    xs[0:HALO, :] = jnp.where(first, 0.0, xp_ref[...])
    xs[HALO:HALO + tt, :] = x_ref[...]


def _conv_specs(tt, ch):
    cur = pl.BlockSpec((tt, ch), lambda i: (i, 0))
    prev = pl.BlockSpec((HALO, ch), lambda i: (jnp.maximum(i * (tt // HALO) - 1, 0), 0))
    return cur, prev


def _conv_fwd(proj, conv_w, *, name):
    t = proj.shape[0]
    ch = conv_w.shape[1]
    nqk = HEADS * DK
    tt = _pick(t, (256,))
    cur, prev = _conv_specs(tt, ch)

    def body(x_ref, xp_ref, cw_ref, q_ref, k_ref, v_ref, xs):
        _fill_halo(xs, x_ref, xp_ref, tt)
        for cb in range(ch // LANES):
            c = _conv_cols(xs, cw_ref, cb, tt)
            s = c * _sigmoid(c)
            if cb < 2 * HEADS:
                s = s * lax.rsqrt(jnp.sum(s * s, axis=-1, keepdims=True) + EPS)
                if cb < HEADS:
                    q_ref[:, cb * LANES:(cb + 1) * LANES] = s * (DK ** -0.5)
                else:
                    k_ref[:, (cb - HEADS) * LANES:(cb - HEADS + 1) * LANES] = s
            else:
                v_ref[:, (cb - 2 * HEADS) * LANES:(cb - 2 * HEADS + 1) * LANES] = s

    return pl.pallas_call(
        body, name=name, grid=(t // tt,),
        in_specs=[cur, prev, pl.BlockSpec((CONV_W, ch), lambda i: (0, 0))],
        out_specs=[pl.BlockSpec((tt, nqk), lambda i: (i, 0)), pl.BlockSpec((tt, nqk), lambda i: (i, 0)),
                   pl.BlockSpec((tt, ch - 2 * nqk), lambda i: (i, 0))],
        out_shape=[jax.ShapeDtypeStruct((t, nqk), F32), jax.ShapeDtypeStruct((t, nqk), F32),
                   jax.ShapeDtypeStruct((t, ch - 2 * nqk), F32)],
        scratch_shapes=[pltpu.VMEM((HALO + tt, ch), F32)],
        compiler_params=_params(("arbitrary",)),
    )(proj, proj, conv_w)


def _conv_bwd_pre(proj, conv_w, dq, dk, dv, *, name):
    t = proj.shape[0]
    ch = conv_w.shape[1]
    nqk = HEADS * DK
    tt = _pick(t, (256,))
    cur, prev = _conv_specs(tt, ch)

    head = 2 * HALO

    def body(x_ref, xp_ref, cw_ref, dq_ref, dk_ref, dv_ref, dc_ref, xs):
        first = pl.program_id(0) == 0
        xs[0:HALO, :] = jnp.where(first, 0.0, xp_ref[...])
        xs[HALO:HALO + head, :] = x_ref[0:head, :]
        xb = x_ref[...].astype(BF16)
        r, cc = _iota2(tt)
        moved = [None] + [jnp.dot((cc == r - sh).astype(BF16), xb, preferred_element_type=F32)
                          for sh in range(1, CONV_W)]
        for cb in range(ch // LANES):
            cols = slice(cb * LANES, (cb + 1) * LANES)
            c_head = c_main = None
            for j in range(CONV_W):
                sh = CONV_W - 1 - j
                w_j = cw_ref[j:j + 1, cols]
                x_m = x_ref[head:tt, cols] if sh == 0 else moved[sh][head:tt, cols]
                x_h = xs[HALO - sh:HALO - sh + head, cols]
                c_main = x_m * w_j if c_main is None else c_main + x_m * w_j
                c_head = x_h * w_j if c_head is None else c_head + x_h * w_j
            c = jnp.concatenate([c_head, c_main], axis=0)
            sg = _sigmoid(c)
            s = c * sg
            if cb < 2 * HEADS:
                if cb < HEADS:
                    d, scale = dq_ref[:, cb * LANES:(cb + 1) * LANES], DK ** -0.5
                else:
                    d, scale = dk_ref[:, (cb - HEADS) * LANES:(cb - HEADS + 1) * LANES], 1.0
                rinv = lax.rsqrt(jnp.sum(s * s, axis=-1, keepdims=True) + EPS)
                ds = scale * rinv * (d - s * (rinv * rinv) * jnp.sum(d * s, axis=-1, keepdims=True))
            else:
                ds = dv_ref[:, (cb - 2 * HEADS) * LANES:(cb - 2 * HEADS + 1) * LANES]
            dc_ref[:, cb * LANES:(cb + 1) * LANES] = ds * sg * (1.0 + c * (1.0 - sg))

    return pl.pallas_call(
        body, name=name, grid=(t // tt,),
        in_specs=[cur, prev, pl.BlockSpec((CONV_W, ch), lambda i: (0, 0)),
                  pl.BlockSpec((tt, nqk), lambda i: (i, 0)), pl.BlockSpec((tt, nqk), lambda i: (i, 0)),
                  pl.BlockSpec((tt, ch - 2 * nqk), lambda i: (i, 0))],
        out_specs=pl.BlockSpec((tt, ch), lambda i: (i, 0)),
        out_shape=jax.ShapeDtypeStruct((t, ch), F32),
        scratch_shapes=[pltpu.VMEM((HALO + head, ch), F32)],
        compiler_params=_params(("arbitrary",)),
    )(proj, proj, conv_w, dq, dk, dv)


def _conv_bwd(proj, dc, conv_w, dproj, *, name):
    t = proj.shape[0]
    ch = conv_w.shape[1]
    tt = _pick(t, (256,))
    nt = t // tt
    cur = pl.BlockSpec((tt, ch), lambda i: (i, 0))
    nxt = pl.BlockSpec((HALO, ch), lambda i: (jnp.minimum((i + 1) * (tt // HALO), t // HALO - 1), 0))

    tail = 2 * HALO
    main = tt - tail

    def body(x_ref, d_ref, dn_ref, cw_ref, alias_ref, dx_ref, dw_ref, ds):
        last = pl.program_id(0) == nt - 1
        ds[0:tail, :] = d_ref[main:tt, :]
        ds[tail:tail + HALO, :] = jnp.where(last, 0.0, dn_ref[...])

        @pl.when(pl.program_id(0) == 0)
        def _():
            dw_ref[...] = jnp.zeros_like(dw_ref)

        dcb = d_ref[...].astype(BF16)
        r, c = _iota2(tt)
        moved = [None] + [jnp.dot((c == r + sh).astype(BF16), dcb, preferred_element_type=F32)
                          for sh in range(1, CONV_W)]
        for cb in range(ch // LANES):
            cols = slice(cb * LANES, (cb + 1) * LANES)
            x_m, x_t = x_ref[0:main, cols], x_ref[main:tt, cols]
            acc_m = acc_t = None
            for j in range(CONV_W):
                sh = CONV_W - 1 - j
                w_j = cw_ref[j:j + 1, cols]
                dj_m = d_ref[0:main, cols] if sh == 0 else moved[sh][0:main, cols]
                dj_t = ds[sh:sh + tail, cols]
                acc_m = dj_m * w_j if acc_m is None else acc_m + dj_m * w_j
                acc_t = dj_t * w_j if acc_t is None else acc_t + dj_t * w_j
                dw_ref[j:j + 1, cols] += (jnp.sum(dj_m * x_m, axis=0, keepdims=True)
                                          + jnp.sum(dj_t * x_t, axis=0, keepdims=True))
            dx_ref[0:main, cols] = acc_m.astype(BF16)
            dx_ref[main:tt, cols] = acc_t.astype(BF16)

    return pl.pallas_call(
        body, name=name, grid=(nt,),
        in_specs=[cur, cur, nxt, pl.BlockSpec((CONV_W, ch), lambda i: (0, 0)), pl.BlockSpec(memory_space=pl.ANY)],
        out_specs=[pl.BlockSpec((tt, ch), lambda i: (i, 0)), pl.BlockSpec((HALO, ch), lambda i: (0, 0))],
        out_shape=[jax.ShapeDtypeStruct(dproj.shape, BF16), jax.ShapeDtypeStruct((HALO, ch), F32)],
        scratch_shapes=[pltpu.VMEM((tail + HALO, ch), F32)],
        input_output_aliases={4: 0},
        compiler_params=_params(("arbitrary",)),
    )(proj, dc, dc, conv_w, dproj)


def _bg_fwd(projs, alog_row, dtb_row, *, name):
    t = projs.shape[0]
    tt = _pick(t, (512, 256))

    def body(p_ref, al_ref, db_ref, o_ref):
        p = p_ref[...]
        lane = lax.broadcasted_iota(jnp.int32, p.shape, 1)
        z = p + db_ref[...]
        sp = jnp.maximum(z, 0.0) + jnp.log1p(jnp.exp(-jnp.abs(z)))
        g = -jnp.exp(al_ref[...]) * sp
        o_ref[...] = jnp.where(lane < HEADS, _sigmoid(p), jnp.where(lane < 2 * HEADS, g, 0.0))

    blk = pl.BlockSpec((tt, LANES), lambda i: (i, 0))
    row = pl.BlockSpec((1, LANES), lambda i: (0, 0))
    return pl.pallas_call(
        body, name=name, grid=(t // tt,), in_specs=[blk, row, row], out_specs=blk,
        out_shape=jax.ShapeDtypeStruct((t, LANES), F32), compiler_params=_params(("parallel",)),
    )(projs, alog_row, dtb_row)


def _bg_bwd(projs, dbg, alog_row, dtb_row, *, name):
    t = projs.shape[0]
    tt = _pick(t, (512, 256))

    def body(p_ref, d_ref, al_ref, db_ref, o_ref, dal_ref, ddb_ref):
        p, d = p_ref[...], d_ref[...]
        lane = lax.broadcasted_iota(jnp.int32, p.shape, 1)
        isg = (lane >= HEADS) & (lane < 2 * HEADS)
        be = _sigmoid(p)
        z = p + db_ref[...]
        sp = jnp.maximum(z, 0.0) + jnp.log1p(jnp.exp(-jnp.abs(z)))
        ea = jnp.exp(al_ref[...])
        d_aa = jnp.where(isg, d * (-ea) * _sigmoid(z), 0.0)
        o_ref[...] = jnp.where(lane < HEADS, d * be * (1.0 - be), d_aa).astype(BF16)

        @pl.when(pl.program_id(0) == 0)
        def _():
            dal_ref[...] = jnp.zeros_like(dal_ref)
            ddb_ref[...] = jnp.zeros_like(ddb_ref)

        dal_ref[...] += jnp.sum(jnp.where(isg, d * (-ea) * sp, 0.0), axis=0, keepdims=True)
        ddb_ref[...] += jnp.sum(d_aa, axis=0, keepdims=True)

    blk = pl.BlockSpec((tt, LANES), lambda i: (i, 0))
    row = pl.BlockSpec((1, LANES), lambda i: (0, 0))
    return pl.pallas_call(
        body, name=name, grid=(t // tt,), in_specs=[blk, blk, row, row], out_specs=[blk, row, row],
        out_shape=[jax.ShapeDtypeStruct((t, LANES), BF16), jax.ShapeDtypeStruct((1, LANES), F32),
                   jax.ShapeDtypeStruct((1, LANES), F32)],
        compiler_params=_params(("arbitrary",)),
    )(projs, dbg, alog_row, dtb_row)


def _hn_specs(n_chunks, rev=False):
    def nn(n):
        return n_chunks - 1 - n if rev else n
    tok = lambda w: pl.BlockSpec((CHUNK, HB * w), lambda g, n: (nn(n), g))
    per = lambda a, b: pl.BlockSpec((HB, None, a, b), lambda g, n: (g, nn(n), 0, 0))
    return tok, per


def _hcols(hh, w):
    return slice(hh * w, (hh + 1) * w)


def _decay(gc_col, gc_row):
    r, c = _iota2(CHUNK)
    d = jnp.exp(jnp.minimum(gc_col - gc_row, 0.0))
    return jnp.where(r >= c, d, 0.0), jnp.where(r > c, d, 0.0)


def _gdn_prep(q, k, v, beta_r, g_r, *, name):
    t = q.shape[0]
    n_chunks = t // CHUNK
    cps = 2 if n_chunks % 2 == 0 else 1
    tok = lambda w: pl.BlockSpec((cps * CHUNK, HB * w), lambda g, n: (n, g))
    per = lambda a, b: pl.BlockSpec((HB, cps, a, b), lambda g, n: (g, n, 0, 0))

    def body(q_ref, k_ref, v_ref, b_ref, g_ref, gc_ref, ti_ref, u_ref, w_ref, p_ref):
        r, c = _iota2(CHUNK)
        its = [(h, cc) for cc in range(cps) for h in range(HB)]
        rows = lambda cc: slice(cc * CHUNK, (cc + 1) * CHUNK)
        n = range(len(its))
        kk = [k_ref[rows(cc), _hcols(h, DK)] for h, cc in its]
        gc_col = [jnp.sum(jnp.where(c <= r, jnp.broadcast_to(g_ref[h, cc], (CHUNK, CHUNK)), 0.0), axis=1,
                          keepdims=True) for h, cc in its]
        gc_row = [_col_to_row(gc_col[i]) for i in n]
        beta_col = [_row_to_col(b_ref[h, cc]) for h, cc in its]
        dec = [_decay(gc_col[i], gc_row[i]) for i in n]
        kb = [kk[i] * beta_col[i] for i in n]
        for i, (h, cc) in enumerate(its):
            gc_ref[h, cc] = gc_row[i]
            p_ref[h, cc] = (_dot_nt(q_ref[rows(cc), _hcols(h, DK)], kk[i]) * dec[i][0]).astype(BF16)
        pw = [_dot_nt(kb[i], kk[i]) * dec[i][1] for i in n]
        y = [-pw[i] for i in n]
        for _ in range(5):
            pw = [_dot(pw[i], pw[i]) for i in n]
            yp = [_dot(y[i], pw[i]) for i in n]
            y = [y[i] + pw[i] + yp[i] for i in n]
        vb = [v_ref[rows(cc), _hcols(h, DV)] * beta_col[i] for i, (h, cc) in enumerate(its)]
        kbe = [kb[i] * jnp.exp(gc_col[i]) for i in n]
        yv = [_dot(y[i], vb[i]) for i in n]
        yk = [_dot(y[i], kbe[i]) for i in n]
        for i, (h, cc) in enumerate(its):
            ti_ref[h, cc] = y[i].astype(BF16)
            u_ref[rows(cc), _hcols(h, DV)] = vb[i] + yv[i]
            w_ref[rows(cc), _hcols(h, DK)] = (kbe[i] + yk[i]).astype(BF16)

    return pl.pallas_call(
        body, name=name, grid=(HEADS // HB, n_chunks // cps),
        in_specs=[tok(DK), tok(DK), tok(DV), per(1, CHUNK), per(1, CHUNK)],
        out_specs=[per(1, CHUNK), per(CHUNK, CHUNK), tok(DV), tok(DK), per(CHUNK, CHUNK)],
        out_shape=[jax.ShapeDtypeStruct((HEADS, n_chunks, 1, CHUNK), F32),
                   jax.ShapeDtypeStruct((HEADS, n_chunks, CHUNK, CHUNK), BF16),
                   jax.ShapeDtypeStruct((t, HEADS * DV), F32), jax.ShapeDtypeStruct((t, HEADS * DK), BF16),
                   jax.ShapeDtypeStruct((HEADS, n_chunks, CHUNK, CHUNK), BF16)],
        compiler_params=_params(("parallel", "parallel")),
    )(q, k, v, beta_r, g_r)


def _gdn_scan(q, k, u, w, p, gc_r, *, name):
    t = q.shape[0]
    n_chunks = t // CHUNK
    tok, per = _hn_specs(n_chunks)

    def body(q_ref, k_ref, u_ref, w_ref, p_ref, gc_ref, o_ref, s_ref, st):
        @pl.when(pl.program_id(1) == 0)
        def _():
            st[...] = jnp.zeros_like(st)

        hs = range(HB)
        s = [st[h] for h in hs]
        gc_row = [gc_ref[h] for h in hs]
        gc_col = [_row_to_col(gc_row[h]) for h in hs]
        glast = [gc_row[h][:, CHUNK - 1:CHUNK] for h in hs]
        for h in hs:
            s_ref[h] = s[h].astype(BF16)
        ws = [_dot(w_ref[:, _hcols(h, DK)], s[h]) for h in hs]
        qs = [_dot(q_ref[:, _hcols(h, DK)] * jnp.exp(gc_col[h]), s[h]) for h in hs]
        vn = [u_ref[:, _hcols(h, DV)] - ws[h] for h in hs]
        pv = [_dot(p_ref[h], vn[h]) for h in hs]
        kv = [_dot_tn(k_ref[:, _hcols(h, DK)] * jnp.exp(glast[h] - gc_col[h]), vn[h]) for h in hs]
        for h in hs:
            o_ref[:, _hcols(h, DV)] = qs[h] + pv[h]
            st[h] = s[h] * jnp.exp(glast[h]) + kv[h]

    return pl.pallas_call(
        body, name=name, grid=(HEADS // HB, n_chunks),
        in_specs=[tok(DK), tok(DK), tok(DV), tok(DK), per(CHUNK, CHUNK), per(1, CHUNK)],
        out_specs=[tok(DV), per(DK, DV)],
        out_shape=[jax.ShapeDtypeStruct((t, HEADS * DV), F32),
                   jax.ShapeDtypeStruct((HEADS, n_chunks, DK, DV), BF16)],
        scratch_shapes=[pltpu.VMEM((HB, DK, DV), F32)],
        compiler_params=_params(("arbitrary", "arbitrary")),
    )(q, k, u, w, p, gc_r)


def _gdn_bwd(q, k, v, beta_r, gc_r, ti, u, w, s_all, do, *, name):
    t = q.shape[0]
    n_chunks = t // CHUNK
    tok, per = _hn_specs(n_chunks, rev=True)

    def body(q_ref, k_ref, v_ref, b_ref, gc_ref, ti_ref, u_ref, w_ref, s_ref, do_ref,
             dq_ref, dk_ref, dv_ref, db_ref, dg_ref, dst):
        @pl.when(pl.program_id(1) == 0)
        def _():
            dst[...] = jnp.zeros_like(dst)

        r, c = _iota2(CHUNK)
        rows = lax.broadcasted_iota(jnp.int32, (CHUNK, 1), 0)
        rsum = lambda a: jnp.sum(a, axis=1, keepdims=True)
        hs = range(HB)
        qq = [q_ref[:, _hcols(h, DK)] for h in hs]
        kk = [k_ref[:, _hcols(h, DK)] for h in hs]
        ww = [w_ref[:, _hcols(h, DK)] for h in hs]
        uu = [u_ref[:, _hcols(h, DV)] for h in hs]
        d_o = [do_ref[:, _hcols(h, DV)] for h in hs]
        s = [s_ref[h].astype(F32) for h in hs]
        d_s = [dst[h] for h in hs]
        gc_row = [gc_ref[h] for h in hs]
        gc_col = [_row_to_col(gc_row[h]) for h in hs]
        beta_col = [_row_to_col(b_ref[h]) for h in hs]
        eg = [jnp.exp(gc_col[h]) for h in hs]
        glast = [gc_row[h][:, CHUNK - 1:CHUNK] for h in hs]
        eglast = [jnp.exp(glast[h]) for h in hs]
        e2 = [jnp.exp(glast[h] - gc_col[h]) for h in hs]
        dec = [_decay(gc_col[h], gc_row[h]) for h in hs]
        kb = [kk[h] * beta_col[h] for h in hs]
        ke = [kk[h] * e2[h] for h in hs]
        qe = [qq[h] * eg[h] for h in hs]
        kkt = [_dot_nt(kb[h], kk[h]) for h in hs]
        qk = [_dot_nt(qq[h], kk[h]) for h in hs]
        ws = [_dot(ww[h], s[h]) for h in hs]
        a = [kkt[h] * dec[h][1] for h in hs]
        pp = [qk[h] * dec[h][0] for h in hs]
        vn = [uu[h] - ws[h] for h in hs]
        t1 = [_dot_tn(pp[h], d_o[h]) for h in hs]
        t2 = [_dot(ke[h], d_s[h]) for h in hs]
        dqe = [_dot_nt(d_o[h], s[h]) for h in hs]
        dke = [_dot_nt(vn[h], d_s[h]) for h in hs]
        dqk = [_dot_nt(d_o[h], vn[h]) * dec[h][0] for h in hs]
        dvn = [t1[h] + t2[h] for h in hs]
        t3 = [_dot_tn(qe[h], d_o[h]) for h in hs]
        t4 = [_dot_tn(ww[h], dvn[h]) for h in hs]
        dw = [-_dot_nt(dvn[h], s[h]) for h in hs]
        de2 = [rsum(dke[h] * ke[h]) for h in hs]
        dglast = [jnp.sum(de2[h], axis=0, keepdims=True)
                  + eglast[h] * jnp.sum(rsum(d_s[h] * s[h]), axis=0, keepdims=True) for h in hs]
        for h in hs:
            dst[h] = d_s[h] * eglast[h] + t3[h] - t4[h]
        yy = [ti_ref[h] for h in hs]
        t5 = [_dot_tn(yy[h], dvn[h]) for h in hs]
        t6 = [_dot_tn(yy[h], dw[h]) for h in hs]
        dvb = [dvn[h] + t5[h] for h in hs]
        dkbe = [dw[h] + t6[h] for h in hs]
        t7 = [_dot_nt(dvb[h], uu[h]) for h in hs]
        t8 = [_dot_nt(dkbe[h], ww[h]) for h in hs]
        d_a = [jnp.where(r > c, -(t7[h] + t8[h]), 0.0) for h in hs]
        dm = [d_a[h] * dec[h][1] for h in hs]
        t9 = [_dot(dm[h], kk[h]) for h in hs]
        t10 = [_dot(dqk[h], kk[h]) for h in hs]
        t11 = [_dot_tn(dqk[h], qq[h]) for h in hs]
        t12 = [_dot_tn(dm[h], kb[h]) for h in hs]
        for h in hs:
            dkb = t9[h] + dkbe[h] * eg[h]
            e_mat = d_a[h] * a[h] + dqk[h] * qk[h]
            dgc = (rsum(dqe[h] * qe[h]) - de2[h] + rsum(dkbe[h] * kb[h] * eg[h]) + rsum(e_mat)
                   - _row_to_col(jnp.sum(e_mat, axis=0, keepdims=True)))
            dgc = dgc + jnp.where(rows == CHUNK - 1, dglast[h], 0.0)
            dq_ref[:, _hcols(h, DK)] = dqe[h] * eg[h] + t10[h]
            dk_ref[:, _hcols(h, DK)] = t11[h] + dke[h] * e2[h] + t12[h] + dkb * beta_col[h]
            dv_ref[:, _hcols(h, DV)] = dvb[h] * beta_col[h]
            dbeta = rsum(dkb * kk[h]) + rsum(dvb[h] * v_ref[:, _hcols(h, DV)])
            db_ref[h] = _col_to_row(dbeta)
            dg_ref[h] = jnp.sum(jnp.where(r >= c, jnp.broadcast_to(dgc, (CHUNK, CHUNK)), 0.0), axis=0, keepdims=True)

    return pl.pallas_call(
        body, name=name, grid=(HEADS // HB, n_chunks),
        in_specs=[tok(DK), tok(DK), tok(DV), per(1, CHUNK), per(1, CHUNK), per(CHUNK, CHUNK), tok(DV), tok(DK),
                  per(DK, DV), tok(DV)],
        out_specs=[tok(DK), tok(DK), tok(DV), per(1, CHUNK), per(1, CHUNK)],
        out_shape=[jax.ShapeDtypeStruct((t, HEADS * DK), F32), jax.ShapeDtypeStruct((t, HEADS * DK), F32),
                   jax.ShapeDtypeStruct((t, HEADS * DV), F32),
                   jax.ShapeDtypeStruct((HEADS, n_chunks, 1, CHUNK), F32),
                   jax.ShapeDtypeStruct((HEADS, n_chunks, 1, CHUNK), F32)],
        scratch_shapes=[pltpu.VMEM((HB, DK, DV), F32)],
        compiler_params=_params(("arbitrary", "arbitrary")),
    )(q, k, v, beta_r, gc_r, ti, u, w, s_all, do)


def _rot(x, cos2, sin2):
    return x * cos2 + pltpu.roll(x, DK // 2, 1) * sin2


def _unrot(d, cos2, sin2):
    return d * cos2 + pltpu.roll(d * sin2, DK // 2, 1)


def _ret_consts(lg):
    r, c = _iota2(CHUNK)
    dm = jnp.where(r >= c, jnp.exp((r - c).astype(F32) * lg), 0.0)
    pos = lax.broadcasted_iota(jnp.int32, (CHUNK, 1), 0).astype(F32)
    return dm, jnp.exp((pos + 1.0) * lg), jnp.exp((CHUNK - 1.0 - pos) * lg), jnp.exp(CHUNK * lg)


def _ret_fwd(proj, cos2, sin2, lg_tab, col_q, col_k, col_v, *, name):
    t = proj.shape[0]
    n_chunks = t // CHUNK
    tok, per = _hn_specs(n_chunks)
    bq, bk, bv = col_q // (HB * DK), col_k // (HB * DK), col_v // (HB * DV)

    def body(q_ref, k_ref, v_ref, cos_ref, sin_ref, lg_ref, o_ref, s_ref, st):
        @pl.when(pl.program_id(1) == 0)
        def _():
            st[...] = jnp.zeros_like(st)

        cos2v, sin2v = cos_ref[...], sin_ref[...]
        hs = range(HB)
        s = [st[h] for h in hs]
        for h in hs:
            s_ref[h] = s[h].astype(BF16)
        cst = [_ret_consts(lg_ref[h][:, 0:1]) for h in hs]
        qq = [_rot(q_ref[:, _hcols(h, DK)], cos2v, sin2v) for h in hs]
        kk = [_rot(k_ref[:, _hcols(h, DK)], cos2v, sin2v) * (DK ** -0.5) for h in hs]
        vv = [v_ref[:, _hcols(h, DV)] for h in hs]
        sc = [_dot_nt(qq[h], kk[h]) * cst[h][0] for h in hs]
        qs = [_dot(qq[h], s[h]) for h in hs]
        kv = [_dot_tn(kk[h] * cst[h][2], vv[h]) for h in hs]
        sv = [_dot(sc[h], vv[h]) for h in hs]
        for h in hs:
            o_ref[:, _hcols(h, DV)] = sv[h] + qs[h] * cst[h][1]
            st[h] = s[h] * cst[h][3] + kv[h]

    return pl.pallas_call(
        body, name=name, grid=(HEADS // HB, n_chunks),
        in_specs=[pl.BlockSpec((CHUNK, HB * DK), lambda g, n: (n, bq + g)),
                  pl.BlockSpec((CHUNK, HB * DK), lambda g, n: (n, bk + g)),
                  pl.BlockSpec((CHUNK, HB * DV), lambda g, n: (n, bv + g)),
                  pl.BlockSpec((CHUNK, DK), lambda g, n: (n, 0)), pl.BlockSpec((CHUNK, DK), lambda g, n: (n, 0)),
                  pl.BlockSpec((HB, 1, LANES), lambda g, n: (g, 0, 0))],
        out_specs=[tok(DV), per(DK, DV)],
        out_shape=[jax.ShapeDtypeStruct((t, HEADS * DV), F32),
                   jax.ShapeDtypeStruct((HEADS, n_chunks, DK, DV), BF16)],
        scratch_shapes=[pltpu.VMEM((HB, DK, DV), F32)],
        compiler_params=_params(("arbitrary", "arbitrary")),
    )(proj, proj, proj, cos2, sin2, lg_tab)


def _ret_bwd(proj, cos2, sin2, lg_tab, s_all, do, dproj, col_q, col_k, col_v, *, name):
    t = proj.shape[0]
    n_chunks = t // CHUNK
    tok, per = _hn_specs(n_chunks, rev=True)
    bq, bk, bv = col_q // (HB * DK), col_k // (HB * DK), col_v // (HB * DV)
    rv = lambda n: n_chunks - 1 - n
    nqk, wid = HEADS * DK, HEADS * (2 * DK + DV)
    assert HB == HEADS and col_k == col_q + nqk and col_v == col_k + nqk and col_q % wid == 0

    def body(q_ref, k_ref, v_ref, cos_ref, sin_ref, lg_ref, s_ref, do_ref, alias_ref, d_ref, dst):
        @pl.when(pl.program_id(1) == 0)
        def _():
            dst[...] = jnp.zeros_like(dst)

        cos2v, sin2v = cos_ref[...], sin_ref[...]
        hs = range(HB)
        s = [s_ref[h].astype(F32) for h in hs]
        d_s = [dst[h] for h in hs]
        d_o = [do_ref[:, _hcols(h, DV)] for h in hs]
        cst = [_ret_consts(lg_ref[h][:, 0:1]) for h in hs]
        qq = [_rot(q_ref[:, _hcols(h, DK)], cos2v, sin2v) for h in hs]
        kk = [_rot(k_ref[:, _hcols(h, DK)], cos2v, sin2v) * (DK ** -0.5) for h in hs]
        vv = [v_ref[:, _hcols(h, DV)] for h in hs]
        dxo = [d_o[h] * cst[h][1] for h in hs]
        sc = [_dot_nt(qq[h], kk[h]) * cst[h][0] for h in hs]
        dsc = [_dot_nt(d_o[h], vv[h]) * cst[h][0] for h in hs]
        t1 = [_dot(kk[h] * cst[h][2], d_s[h]) for h in hs]
        t2 = [_dot_nt(dxo[h], s[h]) for h in hs]
        t3 = [_dot_nt(vv[h], d_s[h]) for h in hs]
        t4 = [_dot_tn(qq[h], dxo[h]) for h in hs]
        t5 = [_dot_tn(sc[h], d_o[h]) for h in hs]
        t6 = [_dot(dsc[h], kk[h]) for h in hs]
        t7 = [_dot_tn(dsc[h], qq[h]) for h in hs]
        for h in hs:
            dst[h] = d_s[h] * cst[h][3] + t4[h]
            d_ref[:, 2 * nqk + h * DV:2 * nqk + (h + 1) * DV] = (t5[h] + t1[h]).astype(BF16)
            d_ref[:, _hcols(h, DK)] = _unrot(t6[h] + t2[h], cos2v, sin2v).astype(BF16)
            d_ref[:, nqk + h * DK:nqk + (h + 1) * DK] = (
                _unrot(t7[h] + t3[h] * cst[h][2], cos2v, sin2v) * (DK ** -0.5)).astype(BF16)

    return pl.pallas_call(
        body, name=name, grid=(HEADS // HB, n_chunks),
        in_specs=[pl.BlockSpec((CHUNK, HB * DK), lambda g, n: (rv(n), bq + g)),
                  pl.BlockSpec((CHUNK, HB * DK), lambda g, n: (rv(n), bk + g)),
                  pl.BlockSpec((CHUNK, HB * DV), lambda g, n: (rv(n), bv + g)),
                  pl.BlockSpec((CHUNK, DK), lambda g, n: (rv(n), 0)),
                  pl.BlockSpec((CHUNK, DK), lambda g, n: (rv(n), 0)),
                  pl.BlockSpec((HB, 1, LANES), lambda g, n: (g, 0, 0)),
                  per(DK, DV), tok(DV), pl.BlockSpec(memory_space=pl.ANY)],
        out_specs=pl.BlockSpec((CHUNK, wid), lambda g, n: (rv(n), col_q // wid)),
        out_shape=jax.ShapeDtypeStruct(dproj.shape, BF16),
        scratch_shapes=[pltpu.VMEM((HB, DK, DV), F32)],
        input_output_aliases={8: 0},
        compiler_params=_params(("arbitrary", "arbitrary")),
    )(proj, proj, proj, cos2, sin2, lg_tab, s_all, do, dproj)


def _merge_parts(oa, ob, z, rg, ga, gb):
    ra = lax.rsqrt(jnp.mean(oa * oa, axis=-1, keepdims=True) + EPS)
    xa = oa * ra
    mu = jnp.mean(ob, axis=-1, keepdims=True)
    cen = ob - mu
    rb = lax.rsqrt(jnp.mean(cen * cen, axis=-1, keepdims=True) + EPS)
    xb = cen * rb
    sz, sr = _sigmoid(z), _sigmoid(rg)
    return ra, xa, rb, xb, sz, sr, _sigmoid(ga), _sigmoid(gb)


def _merge_specs(tt, d, cz, crg, cga, cgb):
    blk = pl.BlockSpec((tt, d), lambda i: (i, 0))
    pcol = lambda c: pl.BlockSpec((tt, d), lambda i: (i, c // d))
    row = pl.BlockSpec((1, d), lambda i: (0, 0))
    return blk, [blk, blk, pcol(cz), pcol(crg), pcol(cga), pcol(cgb), row, row], row


def _merge_fwd(oa, ob, proj, wa, wb, cz, crg, cga, cgb, *, name):
    t, d = oa.shape
    tt = _pick(t, (256,))
    blk, in_specs, _ = _merge_specs(tt, d, cz, crg, cga, cgb)

    def body(oa_ref, ob_ref, z_ref, rg_ref, ga_ref, gb_ref, wa_ref, wb_ref, o_ref):
        for h in range(HEADS):
            cols = slice(h * DV, (h + 1) * DV)
            z, rg = z_ref[:, cols], rg_ref[:, cols]
            _, xa, _, xb, sz, sr, sga, sgb = _merge_parts(
                oa_ref[:, cols], ob_ref[:, cols], z, rg, ga_ref[:, cols], gb_ref[:, cols])
            o_a = xa * wa_ref[:, cols] * (z * sz)
            o_b = xb * wb_ref[:, cols] * (rg * sr)
            o_ref[:, cols] = (sga * o_a + sgb * o_b).astype(BF16)

    return pl.pallas_call(
        body, name=name, grid=(t // tt,), in_specs=in_specs, out_specs=blk,
        out_shape=jax.ShapeDtypeStruct((t, d), BF16), compiler_params=_params(("parallel",)),
    )(oa, ob, proj, proj, proj, proj, wa, wb)


def _merge_bwd(dmix, oa, ob, proj, wa, wb, cz, crg, cga, cgb, *, name, side):
    t, d = oa.shape
    tt = _pick(t, (256,))
    nt = t // tt
    blk, in_specs, row = _merge_specs(tt, d, cz, crg, cga, cgb)
    assert (crg, cga, cgb) == (cz + d, cz + 2 * d, cz + 3 * d) and cz % (4 * d) == 0
    n_si, n_so = len(side.ins), len(side.out_shapes)
    assert len(side.phases) == 2

    def body(*refs):
        dm_ref, oa_ref, ob_ref, z_ref, rg_ref, ga_ref, gb_ref, wa_ref, wb_ref = refs[:9]
        s_in = refs[9:9 + n_si]
        doa_ref, dob_ref, dp_ref, dwa_ref, dwb_ref = refs[9 + n_si:14 + n_si]
        s_out, ssem, rsem = refs[14 + n_si:14 + n_si + n_so], refs[-2], refs[-1]
        dz_ref, drg_ref, dga_ref, dgb_ref = [dp_ref.at[:, i * d:(i + 1) * d] for i in range(4)]

        @pl.when(pl.program_id(0) == 0)
        def _():
            side.phases[0](s_in, s_out, ssem, rsem)
            dwa_ref[...] = jnp.zeros_like(dwa_ref)
            dwb_ref[...] = jnp.zeros_like(dwb_ref)

        for h in range(HEADS):
            cols = slice(h * DV, (h + 1) * DV)
            z, rg, wa_h, wb_h, dmx = z_ref[:, cols], rg_ref[:, cols], wa_ref[:, cols], wb_ref[:, cols], dm_ref[:, cols]
            ra, xa, rb, xb, sz, sr, sga, sgb = _merge_parts(
                oa_ref[:, cols], ob_ref[:, cols], z, rg, ga_ref[:, cols], gb_ref[:, cols])
            na, nb = xa * wa_h, xb * wb_h
            sil_z, sil_r = z * sz, rg * sr
            o_a, o_b = na * sil_z, nb * sil_r
            dga_ref[:, cols] = (dmx * o_a * sga * (1.0 - sga)).astype(BF16)
            dgb_ref[:, cols] = (dmx * o_b * sgb * (1.0 - sgb)).astype(BF16)
            d_oa, d_ob = dmx * sga, dmx * sgb
            dz_ref[:, cols] = (d_oa * na * sz * (1.0 + z * (1.0 - sz))).astype(BF16)
            drg_ref[:, cols] = (d_ob * nb * sr * (1.0 + rg * (1.0 - sr))).astype(BF16)
            dna, dnb = d_oa * sil_z, d_ob * sil_r
            dwa_ref[:, cols] += jnp.sum(dna * xa, axis=0, keepdims=True)
            dwb_ref[:, cols] += jnp.sum(dnb * xb, axis=0, keepdims=True)
            gwa, gwb = dna * wa_h, dnb * wb_h
            doa_ref[:, cols] = ra * (gwa - xa * jnp.mean(gwa * xa, axis=-1, keepdims=True))
            dob_ref[:, cols] = rb * (gwb - jnp.mean(gwb, axis=-1, keepdims=True)
                                     - xb * jnp.mean(gwb * xb, axis=-1, keepdims=True))

        @pl.when(pl.program_id(0) == nt - 1)
        def _():
            side.phases[1](s_in, s_out, ssem, rsem)

    hbm = pl.BlockSpec(memory_space=pl.ANY)
    return pl.pallas_call(
        body, name=name, grid=(nt,), in_specs=[blk] + in_specs + [hbm] * n_si,
        out_specs=[blk, blk, pl.BlockSpec((tt, 4 * d), lambda i: (i, cz // (4 * d))), row, row] + [hbm] * n_so,
        out_shape=[jax.ShapeDtypeStruct((t, d), F32)] * 2 + [jax.ShapeDtypeStruct(proj.shape, BF16)]
        + [jax.ShapeDtypeStruct((1, d), F32)] * 2 + side.out_shapes,
        scratch_shapes=side.sems(),
        compiler_params=_params(("arbitrary",)),
    )(dmix, oa, ob, proj, proj, proj, proj, wa, wb, *side.ins)


def _row_block(rows, cols, itemsize=4, target=1 << 20):
    for rb in (512, 256, 128, 64, 32, 16, 8):
        if rows % rb == 0 and rb * cols * itemsize <= target:
            return rb
    return rows


def _adamw(w, g, m, v, *, name):
    rows, cols = w.shape
    rb = _row_block(rows, cols)

    def body(w_ref, g_ref, m_ref, v_ref, d_ref, nm_ref, nv_ref):
        gg = g_ref[...]
        mm = ADAM_B1 * m_ref[...] + (1.0 - ADAM_B1) * gg
        vv = ADAM_B2 * v_ref[...] + (1.0 - ADAM_B2) * (gg * gg)
        m_hat = mm / (1.0 - ADAM_B1 ** ADAM_STEP)
        v_hat = vv / (1.0 - ADAM_B2 ** ADAM_STEP)
        d_ref[...] = -ADAM_LR * (m_hat / (jnp.sqrt(v_hat) + ADAM_EPS) + ADAM_WD * w_ref[...])
        nm_ref[...] = mm
        nv_ref[...] = vv

    blk = pl.BlockSpec((rb, cols), lambda i: (i, 0))
    return pl.pallas_call(
        body, name=name, grid=(rows // rb,), in_specs=[blk] * 4, out_specs=[blk] * 3,
        out_shape=[jax.ShapeDtypeStruct((rows, cols), F32)] * 3, compiler_params=_params(("parallel",)),
    )(w, g, m, v)


def _adamw_halves(w, mine, other, c_idx, m, v, *, name):
    rows, cols = w.shape
    hr = rows // 2
    rb = _row_block(hr, cols)
    nb = hr // rb

    def body(c_ref, w_ref, a_ref, b_ref, m_ref, v_ref, g_ref, d_ref, nm_ref, nv_ref):
        gg = jnp.where(pl.program_id(0) // nb == c_ref[0], a_ref[...], b_ref[...])
        mm = ADAM_B1 * m_ref[...] + (1.0 - ADAM_B1) * gg
        vv = ADAM_B2 * v_ref[...] + (1.0 - ADAM_B2) * (gg * gg)
        m_hat = mm / (1.0 - ADAM_B1 ** ADAM_STEP)
        v_hat = vv / (1.0 - ADAM_B2 ** ADAM_STEP)
        g_ref[...] = gg
        d_ref[...] = -ADAM_LR * (m_hat / (jnp.sqrt(v_hat) + ADAM_EPS) + ADAM_WD * w_ref[...])
        nm_ref[...] = mm
        nv_ref[...] = vv

    blk = pl.BlockSpec((rb, cols), lambda i, c: (i, 0))
    half = pl.BlockSpec((rb, cols), lambda i, c: (i % nb, 0))
    return pl.pallas_call(
        body, name=name,
        grid_spec=pltpu.PrefetchScalarGridSpec(
            num_scalar_prefetch=1, grid=(rows // rb,), in_specs=[blk, half, half, blk, blk], out_specs=[blk] * 4),
        out_shape=[jax.ShapeDtypeStruct((rows, cols), F32)] * 4, compiler_params=_params(("parallel",)),
    )(c_idx, w, mine, other, m, v)


def _pair_add(g, rsib, c_idx, *, name):
    _, _, hr, cols = g.shape
    rb = _row_block(hr, cols, 2)

    def body(c_ref, g_ref, r_ref, o_ref):
        o_ref[...] = (g_ref[...].astype(F32) + r_ref[...].astype(F32)).astype(BF16)

    return pl.pallas_call(
        body, name=name,
        grid_spec=pltpu.PrefetchScalarGridSpec(
            num_scalar_prefetch=1, grid=(N_CHIPS, hr // rb),
            in_specs=[pl.BlockSpec((None, None, rb, cols), lambda j, i, c: (j, c[0], i, 0)),
                      pl.BlockSpec((None, rb, cols), lambda j, i, c: (j, i, 0))],
            out_specs=pl.BlockSpec((None, rb, cols), lambda j, i, c: (j, i, 0))),
        out_shape=jax.ShapeDtypeStruct((N_CHIPS, hr, cols), BF16),
        compiler_params=_params(("parallel", "parallel")),
    )(c_idx, g, rsib)


def _chip_sum(qb, *, name):
    _, hr, cols = qb.shape
    rb = _row_block(hr, cols, 2, 1 << 19)

    def body(q_ref, o_ref):
        acc = q_ref[0].astype(F32)
        for i in range(1, N_CHIPS):
            acc = acc + q_ref[i].astype(F32)
        o_ref[...] = acc

    return pl.pallas_call(
        body, name=name, grid=(hr // rb,),
        in_specs=[pl.BlockSpec((N_CHIPS, rb, cols), lambda i: (0, i, 0))],
        out_specs=pl.BlockSpec((rb, cols), lambda i: (i, 0)),
        out_shape=jax.ShapeDtypeStruct((hr, cols), F32), compiler_params=_params(("parallel",)),
    )(qb)


def _place():
    x, y, c = lax.axis_index("x"), lax.axis_index("y"), lax.axis_index("c")
    return x, y, c, [(1 - x, y), (x, 1 - y), (1 - x, 1 - y)]


ANY = pl.BlockSpec(memory_space=pl.ANY)


def _gather_side(shards):
    nw, n_k = len(shards), 8

    def plan(ins, outs, ssem, rsem):
        x, y, c, _ = _place()
        nbr_x, nbr_y, sib = (1 - x, y, c), (x, 1 - y, c), (x, y, 1 - c)
        s_me, s_x, s_y, s_d = 2 * x + y, 2 * (1 - x) + y, 2 * x + 1 - y, 2 * (1 - x) + 1 - y

        def rows(w, half, quarter=None):
            hr = shards[w].shape[0] // 2
            if quarter is None:
                return pl.ds(pl.multiple_of(half * hr, 16), hr)
            return pl.ds(pl.multiple_of(half * hr + quarter * (hr // 2), 16), hr // 2)

        def rcopy(w, k, src, slot, rws, to):
            return pltpu.make_async_remote_copy(
                src_ref=src, dst_ref=outs[w].at[slot, rws], send_sem=ssem.at[w * n_k + k],
                recv_sem=rsem.at[w * n_k + k], device_id=to, device_id_type=MESH)

        out = []
        for w in range(nw):
            mine = ins[w].at[rows(w, c)]
            q = [rows(w, c, 0), rows(w, c, 1)]
            p = functools.partial
            out.append([
                (p(rcopy, w, 0, mine, s_me, rows(w, c), nbr_x), p(rcopy, w, 0, mine, s_x, rows(w, c), nbr_x)),
                (p(rcopy, w, 1, mine, s_me, rows(w, c), nbr_y), p(rcopy, w, 1, mine, s_y, rows(w, c), nbr_y)),
                (p(rcopy, w, 2, outs[w].at[s_x, q[0]], s_x, q[0], nbr_y),
                 p(rcopy, w, 2, outs[w].at[s_x, q[0]], s_d, q[0], nbr_y)),
                (p(rcopy, w, 3, outs[w].at[s_y, q[1]], s_y, q[1], nbr_x),
                 p(rcopy, w, 3, outs[w].at[s_y, q[1]], s_d, q[1], nbr_x)),
                (p(rcopy, w, 4, outs[w].at[s_x, rows(w, c)], s_x, rows(w, c), sib),
                 p(rcopy, w, 4, mine, s_x, rows(w, 1 - c), sib)),
                (p(rcopy, w, 5, outs[w].at[s_y, rows(w, c)], s_y, rows(w, c), sib),
                 p(rcopy, w, 5, mine, s_y, rows(w, 1 - c), sib)),
                (p(rcopy, w, 6, outs[w].at[s_d, q[0]], s_d, q[0], sib),
                 p(rcopy, w, 6, outs[w].at[s_d, q[0]], s_d, rows(w, 1 - c, 0), sib)),
                (p(rcopy, w, 7, outs[w].at[s_d, q[1]], s_d, q[1], sib),
                 p(rcopy, w, 7, outs[w].at[s_d, q[1]], s_d, rows(w, 1 - c, 1), sib)),
            ])
        return out

    def send_own(*refs):
        for cps in plan(*refs):
            cps[0][0]().start()
            cps[1][0]().start()

    def relay(*refs):
        for cps in plan(*refs):
            cps[0][1]().wait_recv()
            cps[2][0]().start()
            cps[4][0]().start()
            cps[1][1]().wait_recv()
            cps[3][0]().start()
            cps[5][0]().start()

    def pass_diagonal(*refs):
        for cps in plan(*refs):
            cps[2][1]().wait_recv()
            cps[6][0]().start()
            cps[3][1]().wait_recv()
            cps[7][0]().start()

    def finish(*refs):
        for cps in plan(*refs):
            for k in (4, 5, 6, 7):
                cps[k][1]().wait_recv()
            for k in range(n_k):
                cps[k][0]().wait_send()

    return _Side(shards, [jax.ShapeDtypeStruct((N_CHIPS,) + s.shape, s.dtype) for s in shards], nw * n_k,
                 [send_own, relay, pass_diagonal, finish])


def _run_side(side, *, name):
    n_in, n_out = len(side.ins), len(side.out_shapes)

    def body(*refs):
        ins, outs = refs[:n_in], refs[n_in:n_in + n_out]
        for phase in side.phases:
            phase(ins, outs, refs[-2], refs[-1])

    return pl.pallas_call(
        body, name=name, in_specs=[ANY] * n_in, out_specs=[ANY] * n_out, out_shape=side.out_shapes,
        scratch_shapes=side.sems(), compiler_params=pltpu.CompilerParams(has_side_effects=True),
    )(*side.ins)


def _exchange_side(gs):
    nw = len(gs)

    def copies(ins, outs, ssem, rsem):
        x, y, c, _ = _place()
        return [pltpu.make_async_remote_copy(
            src_ref=ins[w].at[:, 1 - c], dst_ref=outs[w], send_sem=ssem.at[w], recv_sem=rsem.at[w],
            device_id=(x, y, 1 - c), device_id_type=MESH) for w in range(nw)]

    def start(*refs):
        for cp in copies(*refs):
            cp.start()

    def finish(*refs):
        for cp in copies(*refs):
            cp.wait()

    return _Side(gs, [jax.ShapeDtypeStruct((g.shape[0],) + g.shape[2:], g.dtype) for g in gs], nw, [start, finish])


def _scatter_side(ps):
    nw = len(ps)

    def copies(ins, outs, ssem, rsem):
        x, y, c, chips = _place()
        s_me = 2 * x + y
        return [pltpu.make_async_remote_copy(
            src_ref=ins[w].at[2 * chip[0] + chip[1]], dst_ref=outs[w].at[s_me],
            send_sem=ssem.at[w * 3 + j], recv_sem=rsem.at[w * 3 + j],
            device_id=(chip[0], chip[1], c), device_id_type=MESH)
            for w in range(nw) for j, chip in enumerate(chips)]

    def start(*refs):
        for cp in copies(*refs):
            cp.start()

    def finish(*refs):
        for cp in copies(*refs):
            cp.wait()

    return _Side(ps, [jax.ShapeDtypeStruct(p.shape, p.dtype) for p in ps], nw * 3, [start, finish])


def _pair_share(hs, *, name):
    nw = len(hs)

    def body(*refs):
        ins, outs = refs[:nw], refs[nw:2 * nw]
        ssem, rsem = refs[2 * nw:]
        x, y, c, _ = _place()
        cps = []
        for w in range(nw):
            cp = pltpu.make_async_remote_copy(
                src_ref=ins[w], dst_ref=outs[w], send_sem=ssem.at[w], recv_sem=rsem.at[w],
                device_id=(x, y, 1 - c), device_id_type=MESH)
            cp.start()
            cps.append(cp)
        for cp in cps:
            cp.wait()

    return pl.pallas_call(
        body, name=name, in_specs=[ANY] * nw, out_specs=[ANY] * nw,
        out_shape=[jax.ShapeDtypeStruct(h.shape, h.dtype) for h in hs],
        scratch_shapes=[pltpu.SemaphoreType.DMA((nw,)), pltpu.SemaphoreType.DMA((nw,))],
        compiler_params=pltpu.CompilerParams(has_side_effects=True),
    )(*hs)


def _gather_small(a, *, name):
    def body(a_ref, o_ref, ssem, rsem):
        x, y, c, chips = _place()
        s_me = 2 * x + y
        o_ref[s_me] = a_ref[...]
        cps = []
        for j, chip in enumerate(chips):
            cp = pltpu.make_async_remote_copy(
                src_ref=a_ref, dst_ref=o_ref.at[s_me], send_sem=ssem.at[j], recv_sem=rsem.at[j],
                device_id=(chip[0], chip[1], c), device_id_type=MESH)
            cp.start()
            cps.append(cp)
        for cp in cps:
            cp.wait()

    vm = pl.BlockSpec(memory_space=pltpu.VMEM)
    return pl.pallas_call(
        body, name=name, in_specs=[vm], out_specs=vm,
        out_shape=jax.ShapeDtypeStruct((N_CHIPS,) + a.shape, a.dtype),
        scratch_shapes=[pltpu.SemaphoreType.DMA((3,)), pltpu.SemaphoreType.DMA((3,))],
    )(a)


def _allreduce_small(p, *, name):
    def body(p_ref, o_ref, buf, ssem, rsem):
        x, y, c, _ = _place()
        me = 4 * x + 2 * y + c
        buf[me] = p_ref[...]
        cps = []
        for k in range(1, N_DEV):
            fx, fy, fc = (k >> 2) & 1, (k >> 1) & 1, k & 1
            peer = (x + fx - 2 * x * fx, y + fy - 2 * y * fy, c + fc - 2 * c * fc)
            cp = pltpu.make_async_remote_copy(
                src_ref=buf.at[me], dst_ref=buf.at[me], send_sem=ssem.at[k - 1], recv_sem=rsem.at[k - 1],
                device_id=peer, device_id_type=MESH)
            cp.start()
            cps.append(cp)
        for cp in cps:
            cp.wait()
        acc = buf[0]
        for i in range(1, N_DEV):
            acc = acc + buf[i]
        o_ref[...] = acc

    vm = pl.BlockSpec(memory_space=pltpu.VMEM)
    return pl.pallas_call(
        body, name=name, in_specs=[vm], out_specs=vm,
        out_shape=jax.ShapeDtypeStruct(p.shape, p.dtype),
        scratch_shapes=[pltpu.VMEM((N_DEV,) + p.shape, p.dtype), pltpu.SemaphoreType.DMA((N_DEV - 1,)),
                        pltpu.SemaphoreType.DMA((N_DEV - 1,))],
    )(p)


def _rows_to_tokens(r):
    h, n = r.shape[0], r.shape[1]
    return r.reshape(h, n * CHUNK).T


def _tokens_to_rows(a):
    t, h = a.shape
    return a.T.reshape(h, t // CHUNK, 1, CHUNK)


def kernel(x, norm1_w, w_in, conv_w, a_log, dt_bias, gdn_norm_w, ret_norm_w, w_out, norm2_w, w_gate, w_up, w_down, norm_f_w, loss_target, m_norm1_w, m_w_in, m_conv_w, m_a_log, m_dt_bias, m_gdn_norm_w, m_ret_norm_w, m_w_out, m_norm2_w, m_w_gate, m_w_up, m_w_down, m_norm_f_w, v_norm1_w, v_w_in, v_conv_w, v_a_log, v_dt_bias, v_gdn_norm_w, v_ret_norm_w, v_w_out, v_norm2_w, v_w_gate, v_w_up, v_w_down, v_norm_f_w):
    t, d = x.shape[1], x.shape[2]
    f = w_gate.shape[2] * N_CHIPS
    nqk, nv = HEADS * DK, HEADS * DV
    ncs = w_in.shape[2]
    c_idx = lax.axis_index("c")
    s_idx = 2 * lax.axis_index("x") + lax.axis_index("y")
    xs = x[0]
    tgt = loss_target[0]

    n_small = 2 * HEADS
    widths = [2 * nqk + nv, nv, n_small, nqk, nqk, nv, nv, d, d]
    g_off = np.concatenate([[0], np.cumsum(widths)])
    order = [0, 3, 4, 5, 1, 6, 7, 8]
    m_off = np.concatenate([[0], np.cumsum([widths[i] for i in order])])
    o_rq, o_rk, o_rv, o_az, o_rg, o_ga, o_gb = [int(m_off[i]) for i in range(1, 8)]
    segs = [(int(g_off[i]), int(g_off[i + 1]), int(m_off[order.index(i)]) if i != 2 else None) for i in range(9)]

    def shard_pieces(s):
        out = []
        for a, b, dst in segs:
            lo, hi = max(a, s * ncs), min(b, (s + 1) * ncs)
            if lo < hi:
                out.append((lo - s * ncs, hi - s * ncs, None if dst is None else dst + lo - a))
        return out

    own = [w_in[0].astype(BF16), w_out[0].astype(BF16), w_gate[0].astype(BF16), w_up[0].astype(BF16),
           w_down[0].astype(BF16)]
    put_own = lambda full, mine: lax.dynamic_update_slice(full, mine[None], (s_idx, 0, 0))
    wg_in = put_own(_run_side(_gather_side(own[:1]), name="gather_w_in")[0], own[0])
    conv_full = _gather_small(conv_w[0], name="gather_conv_w")
    conv_full = jnp.concatenate([conv_full[i] for i in range(N_CHIPS)], axis=1)
    cuts = [(dst, s, a, b) for s in range(N_CHIPS) for a, b, dst in shard_pieces(s)]
    w_main = jnp.concatenate([wg_in[s][:, a:b] for dst, s, a, b in sorted(c for c in cuts if c[0] is not None)],
                             axis=1)
    w_small = jnp.concatenate([wg_in[s][:, a:b] for dst, s, a, b in cuts if dst is None], axis=1)
    w_small = jnp.pad(w_small, ((0, 0), (0, LANES - n_small)))

    pad16 = lambda a: jnp.pad(a, ((0, 0), (HEADS, LANES - 2 * HEADS)))
    alog_row, dtb_row = pad16(a_log), pad16(dt_bias)
    wa_row = jnp.tile(gdn_norm_w, (1, HEADS))
    inv = ROPE_BASE ** (-jnp.arange(0, DK, 2, dtype=F32) / DK)
    ang = jnp.arange(t, dtype=F32)[:, None] * inv[None, :]
    cos2 = jnp.concatenate([jnp.cos(ang), jnp.cos(ang)], axis=1)
    sin2 = jnp.concatenate([-jnp.sin(ang), jnp.sin(ang)], axis=1)
    lg = jnp.log1p(-jnp.exp2(-5.0 - jnp.arange(HEADS, dtype=F32)))
    lg_tab = jnp.broadcast_to(lg[:, None, None], (HEADS, 1, LANES))

    fq = f // N_CHIPS
    u1, projs = _rms_fwd(xs, norm1_w, w_narrow=w_small, name="rms1_fwd")
    proj, *rest = _mm(u1, w_main, tm=512, tn=2048, tk=d, side=_gather_side(own[1:]), name="mm_proj")
    wg_out, wg_gate, wg_up, wg_down = [put_own(g, o) for g, o in zip(rest, own[1:])]
    w_o = wg_out.reshape(d, d)
    w_g, w_u = wg_gate, wg_up
    w_d = wg_down.reshape(f, d)
    q_a, k_a, v_a = _conv_fwd(proj, conv_full, name="conv_fwd")
    bg = _bg_fwd(projs, alog_row, dtb_row, name="bg_fwd")
    beta_r = _tokens_to_rows(bg[:, :HEADS])
    g_r = _tokens_to_rows(bg[:, HEADS:2 * HEADS])
    gc_r, tinv, u_a, w_a, p_a = _gdn_prep(q_a, k_a, v_a, beta_r, g_r, name="gdn_prep")
    o_a, s_a = _gdn_scan(q_a, k_a, u_a, w_a, p_a, gc_r, name="gdn_scan")
    o_b, s_b = _ret_fwd(proj, cos2, sin2, lg_tab, o_rq, o_rk, o_rv, name="ret_fwd")
    mixed = _merge_fwd(o_a, o_b, proj, wa_row, ret_norm_w, o_az, o_rg, o_ga, o_gb, name="merge_fwd")
    h1 = _mm(mixed, w_o, res=xs, tm=512, tn=d, tk=d, name="mm_out")
    hn = _rms_fwd(h1, norm2_w, name="rms2_fwd")
    gt, up, act = _ffn_in(hn, w_g, w_u, tm=256, name="ffn_in")
    h2 = _mm(act, w_d, res=h1, tm=512, tn=1024, tk=f, name="mm_down")
    loss_row, dh2, dh2b, d_nf = _loss_head(h2, tgt, norm_f_w.reshape(1, d), name="loss_head")

    g_down = _mm(act, dh2b, ta=True, out_dtype=BF16, tm=fq, tn=d, tk=1024, name="mm_dw_down")
    dgt, dup = _ffn_back(dh2b, w_d, gt, up, tm=512, tn=fq, name="ffn_back")
    dhn = _mm(dgt, w_g, tb=True, pair=(dup, w_u), tm=512, tn=d, tk=fq, name="mm_dhn")
    g_gate = _mm(hn, dgt, ta=True, out_dtype=BF16, out_stacked=True, tm=1024, tn=fq, tk=2048, name="mm_dw_gate")
    g_up = _mm(hn, dup, ta=True, out_dtype=BF16, out_stacked=True, tm=1024, tn=fq, tk=2048, name="mm_dw_up")
    dh1, dh1b, d_n2 = _rms_bwd(dhn, h1, norm2_w, dh2, name="rms2_bwd")
    dmix = _mm(dh1b, w_o, tb=True, tm=512, tn=d, tk=d, name="mm_dmix")
    g_out = _mm(mixed, dh1b, ta=True, out_dtype=BF16, tm=1024, tn=d, tk=1024, name="mm_dw_out")
    halves = lambda g: g.reshape(N_CHIPS, 2, g.shape[1] // 2, g.shape[2])
    c_arr = jnp.reshape(c_idx, (1,)).astype(jnp.int32)
    gs_ffn = [halves(g_out.reshape(N_CHIPS, d // N_CHIPS, d)), halves(g_gate), halves(g_up),
              halves(g_down.reshape(N_CHIPS, fq, d))]
    do_a, do_b, dproj, d_wa, d_wb, *rsib = _merge_bwd(
        dmix, o_a, o_b, proj, wa_row, ret_norm_w, o_az, o_rg, o_ga, o_gb, side=_exchange_side(gs_ffn),
        name="merge_bwd")
    ps_ffn = [_pair_add(g, r, c_arr, name=f"grad_pair_add_{nm}")
              for g, r, nm in zip(gs_ffn, rsib, ["w_out", "w_gate", "w_up", "w_down"])]
    dproj = _ret_bwd(proj, cos2, sin2, lg_tab, s_b, do_b, dproj, o_rq, o_rk, o_rv, name="ret_bwd")
    dq_a, dk_a, dv_a, dbeta_r, dg_r = _gdn_bwd(q_a, k_a, v_a, beta_r, gc_r, tinv, u_a, w_a, s_a, do_a, name="gdn_bwd")
    dc = _conv_bwd_pre(proj, conv_full, dq_a, dk_a, dv_a, name="conv_bwd_pre")
    dproj, d_cw = _conv_bwd(proj, dc, conv_full, dproj, name="conv_bwd")
    dbg = jnp.pad(jnp.concatenate([_rows_to_tokens(dbeta_r), _rows_to_tokens(dg_r)], axis=1),
                  ((0, 0), (0, LANES - 2 * HEADS)))
    dprojs, d_alog, d_dtb = _bg_bwd(projs, dbg, alog_row, dtb_row, name="bg_bwd")
    g_main, *qs_ffn = _mm(u1, dproj, ta=True, out_dtype=BF16, tm=1024, tn=2048, tk=2048,
                          side=_scatter_side(ps_ffn), name="mm_dw_in")
    g_small = _mm(u1, dprojs, ta=True, out_dtype=BF16, tm=1024, tn=LANES, tk=1024, name="mm_dw_in_small")
    pieces = []
    for s in range(N_CHIPS):
        seen, parts = 0, []
        for a, b, dst in shard_pieces(s):
            parts.append(g_small[:, seen:seen + b - a] if dst is None else g_main[:, dst:dst + b - a])
            seen += (b - a) if dst is None else 0
        pieces.append(jnp.concatenate(parts, axis=1))
    gs_in = [halves(jnp.stack(pieces))]
    rsib = _run_side(_exchange_side(gs_in), name="grad_pair_exchange_in")
    ps_in = [_pair_add(gs_in[0], rsib[0], c_arr, name="grad_pair_add_w_in")]
    du, *qs_in = _mm(dproj, w_main, tb=True, out_dtype=BF16, tm=1024, tn=d, tk=2048,
                     side=_scatter_side(ps_in), name="mm_du")
    dx, _, d_n1 = _rms_bwd(du, xs, norm1_w, dh1, narrow=(dprojs, w_small), name="rms1_bwd")

    names = ["w_in", "w_out", "w_gate", "w_up", "w_down"]
    qs = [lax.dynamic_update_slice(q, lax.dynamic_slice(p, (s_idx, 0, 0), (1,) + p.shape[1:]), (s_idx, 0, 0))
          for q, p in zip(qs_in + qs_ffn, ps_in + ps_ffn)]
    hs = [_chip_sum(q, name=f"grad_chip_sum_{nm}") for q, nm in zip(qs, names)]
    theirs = _pair_share(hs, name="grad_pair_share")
    big_w = [w_in[0], w_out[0], w_gate[0], w_up[0], w_down[0]]
    big_m = [m_w_in[0], m_w_out[0], m_w_gate[0], m_w_up[0], m_w_down[0]]
    big_v = [v_w_in[0], v_w_out[0], v_w_gate[0], v_w_up[0], v_w_down[0]]
    big = {}
    for nm, mine, other, w_, m_, v_ in zip(names, hs, theirs, big_w, big_m, big_v):
        big[nm] = tuple(a[None] for a in _adamw_halves(w_, mine, other, c_arr, m_, v_, name=f"adamw_{nm}"))

    d_wa_h = jnp.sum(d_wa.reshape(HEADS, DV), axis=0, keepdims=True)
    small = [d_n1, d_alog[:, HEADS:2 * HEADS], d_dtb[:, HEADS:2 * HEADS], d_wa_h, d_wb, d_n2, d_nf,
             d_cw[:CONV_W].reshape(1, -1)]
    sizes = [a.shape[1] for a in small]
    packed = jnp.concatenate(small, axis=1)
    n_pack = packed.shape[1]
    n_rows = -(-n_pack // LANES)
    n_rows = -(-n_rows // 8) * 8
    packed = jnp.pad(packed, ((0, 0), (0, n_rows * LANES - n_pack))).reshape(n_rows, LANES)
    red = _allreduce_small(packed, name="allreduce_small").reshape(1, -1)
    offs = np.cumsum([0] + sizes)
    g_n1, g_alog, g_dtb, g_wa, g_wb, g_n2, g_nf, g_cw = [red[:, offs[i]:offs[i + 1]] for i in range(len(sizes))]
    ncw = conv_w.shape[2]
    g_cw = lax.dynamic_slice(g_cw.reshape(CONV_W, -1), (0, s_idx * ncw), (CONV_W, ncw))

    def small_update(w_, g, m_, v_, nm):
        shape = w_.shape
        pad = (-w_.size) % LANES
        to2 = lambda a: jnp.pad(a.reshape(1, -1), ((0, 0), (0, pad)))
        outs = _adamw(to2(w_), to2(g), to2(m_), to2(v_), name=f"adamw_{nm}")
        return (g.reshape(shape),) + tuple(a[:, :w_.size].reshape(shape) for a in outs)

    res = {
        "norm1_w": small_update(norm1_w, g_n1, m_norm1_w, v_norm1_w, "norm1_w"),
        "w_in": big["w_in"],
        "conv_w": small_update(conv_w, g_cw, m_conv_w, v_conv_w, "conv_w"),
        "a_log": small_update(a_log, g_alog, m_a_log, v_a_log, "a_log"),
        "dt_bias": small_update(dt_bias, g_dtb, m_dt_bias, v_dt_bias, "dt_bias"),
        "gdn_norm_w": small_update(gdn_norm_w, g_wa, m_gdn_norm_w, v_gdn_norm_w, "gdn_norm_w"),
        "ret_norm_w": small_update(ret_norm_w, g_wb, m_ret_norm_w, v_ret_norm_w, "ret_norm_w"),
        "w_out": big["w_out"],
        "norm2_w": small_update(norm2_w, g_n2, m_norm2_w, v_norm2_w, "norm2_w"),
        "w_gate": big["w_gate"],
        "w_up": big["w_up"],
        "w_down": big["w_down"],
        "norm_f_w": small_update(norm_f_w, g_nf, m_norm_f_w, v_norm_f_w, "norm_f_w"),
    }
    order = ["norm1_w", "w_in", "conv_w", "a_log", "dt_bias", "gdn_norm_w", "ret_norm_w", "w_out", "norm2_w",
             "w_gate", "w_up", "w_down", "norm_f_w"]
    loss = lax.psum(loss_row[0, 0], ("x", "y", "c"))
    return (loss, dx[None], *[res[n][0] for n in order], *[res[n][1] for n in order],
            *[res[n][2] for n in order], *[res[n][3] for n in order])
```

```python
import functools

import jax
import jax.numpy as jnp
import numpy as np
from jax import lax
from jax.experimental import pallas as pl
from jax.experimental.pallas import tpu as pltpu

F32 = jnp.float32
BF16 = jnp.bfloat16
MESH = pl.DeviceIdType.MESH

HEADS = 8
DK = 128
DV = 256
CHUNK = 64
CONV_W = 4
EPS = 1e-6
ROPE_BASE = 10000.0
ADAM_LR, ADAM_B1, ADAM_B2, ADAM_EPS, ADAM_WD, ADAM_STEP = 0.001, 0.9, 0.999, 1e-08, 0.01, 10
N_CHIPS = 4
N_DEV = 8
LANES = 128
HALO = 8
VMEM_LIMIT = 56 * 1024 * 1024
HB = 8


def _pick(n, cands):
    for c in cands:
        if n % c == 0:
            return c
    raise ValueError(f"no tile for {n} in {cands}")


def _params(sem=None):
    return pltpu.CompilerParams(dimension_semantics=sem, vmem_limit_bytes=VMEM_LIMIT)


def _dot(a, b):
    return jnp.dot(a.astype(BF16), b.astype(BF16), preferred_element_type=F32)


def _dot_nt(a, b):
    return lax.dot_general(a.astype(BF16), b.astype(BF16), (((1,), (1,)), ((), ())), preferred_element_type=F32)


def _dot_tn(a, b):
    return lax.dot_general(a.astype(BF16), b.astype(BF16), (((0,), (0,)), ((), ())), preferred_element_type=F32)


def _sigmoid(x):
    return 1.0 / (1.0 + jnp.exp(-x))


def _iota2(n):
    return lax.broadcasted_iota(jnp.int32, (n, n), 0), lax.broadcasted_iota(jnp.int32, (n, n), 1)


def _row_to_col(row):
    n = row.shape[1]
    r, c = _iota2(n)
    return jnp.sum(jnp.where(r == c, jnp.broadcast_to(row, (n, n)), 0.0), axis=1, keepdims=True)


def _col_to_row(col):
    n = col.shape[0]
    r, c = _iota2(n)
    return jnp.sum(jnp.where(r == c, jnp.broadcast_to(col, (n, n)), 0.0), axis=0, keepdims=True)


class _Side:
    def __init__(self, ins, out_shapes, n_sem, phases):
        self.ins, self.out_shapes, self.n_sem, self.phases = list(ins), list(out_shapes), n_sem, phases

    def sems(self):
        return [pltpu.SemaphoreType.DMA((self.n_sem,)), pltpu.SemaphoreType.DMA((self.n_sem,))]


def _mm(a, b, *, name, tm, tn, tk, ta=False, tb=False, out_dtype=F32, res=None, side=None, pair=None,
        out_stacked=False):
    m, k = (a.shape[1], a.shape[0]) if ta else a.shape
    b_slots = b.ndim == 3
    if b_slots:
        assert tb and tk == b.shape[2]
        n = b.shape[1]
    else:
        n = b.shape[0] if tb else b.shape[1]
    tm, tn, tk = min(tm, m), min(tn, n), min(tk, k)
    assert m % tm == 0 and n % tn == 0 and k % tk == 0, (name, m, n, k)
    nk = k // tk
    nj, ni = n // tn, m // tm
    dn = (((0 if ta else 1,), (1 if tb else 0,)), ((), ()))
    n_ab = 4 if pair else 2
    n_in = n_ab + (res is not None)
    n_side_in = len(side.ins) if side else 0
    n_side_out = len(side.out_shapes) if side else 0

    def body(*refs):
        a_ref, b_ref = refs[0], refs[1]
        r_ref = refs[n_ab] if res is not None else None
        o_ref = refs[n_in + n_side_in]
        if side:
            s_in = refs[n_in:n_in + n_side_in]
            s_out = refs[n_in + n_side_in + 1:n_in + n_side_in + 1 + n_side_out]
            ssem, rsem = refs[-2], refs[-1]
            j_, i_, k_ = pl.program_id(0), pl.program_id(1), pl.program_id(2)
            n_mid = len(side.phases) - 2
            assert n_mid == 0 or nj >= 2 * n_mid
            when = [(j_ == 0) & (i_ == 0) & (k_ == 0)]
            when += [(j_ == nj // 2 + (p * (nj // 2)) // n_mid) & (i_ == 0) & (k_ == 0) for p in range(n_mid)]
            when.append((j_ == nj - 1) & (i_ == ni - 1) & (k_ == nk - 1))

        def run_phase(p):
            @pl.when(when[p])
            def _():
                side.phases[p](s_in, s_out, ssem, rsem)

        if side:
            for p in range(len(side.phases) - 1):
                run_phase(p)

        def finish(r):
            if res is not None:
                r = r + r_ref[...]
            o_ref[...] = r.astype(out_dtype)

        part = lax.dot_general(a_ref[...], b_ref[...], dn, preferred_element_type=F32)
        if pair:
            part = part + lax.dot_general(refs[2][...], refs[3][...], dn, preferred_element_type=F32)
        if nk == 1:
            finish(part)
        else:
            acc = refs[n_in + n_side_in + 1 + n_side_out]
            kk = pl.program_id(2)

            @pl.when(kk == 0)
            def _():
                acc[...] = part

            @pl.when((kk > 0) & (kk < nk - 1))
            def _():
                acc[...] += part

            @pl.when(kk == nk - 1)
            def _():
                finish(acc[...] + part)

        if side:
            run_phase(len(side.phases) - 1)

    a_spec = pl.BlockSpec((tk, tm), lambda j, i, kk: (kk, i)) if ta else pl.BlockSpec((tm, tk), lambda j, i, kk: (i, kk))
    b_spec = pl.BlockSpec((tn, tk), lambda j, i, kk: (j, kk)) if tb else pl.BlockSpec((tk, tn), lambda j, i, kk: (kk, j))
    if b_slots:
        b_spec = pl.BlockSpec((None, tn, tk), lambda j, i, kk: (kk, j, 0))
    o_spec = pl.BlockSpec((tm, tn), lambda j, i, kk: (i, j))
    o_shape = jax.ShapeDtypeStruct((m, n), out_dtype)
    if out_stacked:
        assert res is None
        o_spec = pl.BlockSpec((None, tm, tn), lambda j, i, kk: (j, i, 0))
        o_shape = jax.ShapeDtypeStruct((nj, m, tn), out_dtype)
    in_specs, args = [a_spec, b_spec], [a, b]
    if pair:
        assert pair[0].shape == a.shape and pair[1].shape == b.shape
        in_specs += [a_spec, b_spec]
        args += list(pair)
    if res is not None:
        in_specs.append(o_spec)
        args.append(res)
    out_specs, out_shape = o_spec, o_shape
    scratch = [pltpu.VMEM((tm, tn), F32)] if nk > 1 else []
    sem = ("parallel", "parallel", "arbitrary")
    if side:
        hbm = pl.BlockSpec(memory_space=pl.ANY)
        in_specs += [hbm] * n_side_in
        args += side.ins
        out_specs, out_shape = [o_spec] + [hbm] * n_side_out, [out_shape] + side.out_shapes
        scratch += side.sems()
        sem = ("arbitrary",) * 3
    return pl.pallas_call(
        body, name=name, grid=(nj, ni, nk), in_specs=in_specs, out_specs=out_specs, out_shape=out_shape,
        scratch_shapes=scratch, compiler_params=_params(sem),
    )(*args)


def _rms_fwd(x, w, *, name, w_narrow=None):
    t, d = x.shape
    tt = _pick(t, (512, 256))

    def body(*refs):
        x_ref, w_ref, o_ref = refs[0], refs[1], refs[-2 if w_narrow is not None else -1]
        xv = x_ref[...]
        r = lax.rsqrt(jnp.mean(xv * xv, axis=-1, keepdims=True) + EPS)
        u = (xv * r * w_ref[...]).astype(BF16)
        o_ref[...] = u
        if w_narrow is not None:
            refs[-1][...] = jnp.dot(u, refs[2][...], preferred_element_type=F32)

    blk = pl.BlockSpec((tt, d), lambda i: (i, 0))
    in_specs, args = [blk, pl.BlockSpec((1, d), lambda i: (0, 0))], [x, w]
    out_specs, out_shape = [blk], [jax.ShapeDtypeStruct((t, d), BF16)]
    if w_narrow is not None:
        in_specs.append(pl.BlockSpec(w_narrow.shape, lambda i: (0, 0)))
        args.append(w_narrow)
        out_specs.append(pl.BlockSpec((tt, LANES), lambda i: (i, 0)))
        out_shape.append(jax.ShapeDtypeStruct((t, LANES), F32))
    out = pl.pallas_call(
        body, name=name, grid=(t // tt,), in_specs=in_specs, out_specs=out_specs, out_shape=out_shape,
        compiler_params=_params(("parallel",)),
    )(*args)
    return out if w_narrow is not None else out[0]


def _rms_bwd(dn, x, w, dres, *, name, narrow=None):
    t, d = x.shape
    tt = _pick(t, (256,))

    def body(*refs):
        dn_ref, x_ref, w_ref, dres_ref = refs[:4]
        dx_ref, dxb_ref, dw_ref = refs[-3:]
        xv, g = x_ref[...], dn_ref[...].astype(F32)
        if narrow is not None:
            g = g + lax.dot_general(refs[4][...], refs[5][...], (((1,), (1,)), ((), ())), preferred_element_type=F32)
        r = lax.rsqrt(jnp.mean(xv * xv, axis=-1, keepdims=True) + EPS)
        xh = xv * r
        gw = g * w_ref[...]
        dx = dres_ref[...] + r * (gw - xh * jnp.mean(gw * xh, axis=-1, keepdims=True))
        dx_ref[...] = dx
        dxb_ref[...] = dx.astype(BF16)

        @pl.when(pl.program_id(0) == 0)
        def _():
            dw_ref[...] = jnp.zeros_like(dw_ref)

        dw_ref[...] += jnp.sum(g * xh, axis=0, keepdims=True)

    blk = pl.BlockSpec((tt, d), lambda i: (i, 0))
    row = pl.BlockSpec((1, d), lambda i: (0, 0))
    in_specs, args = [blk, blk, row, blk], [dn, x, w, dres]
    if narrow is not None:
        in_specs += [pl.BlockSpec((tt, LANES), lambda i: (i, 0)), pl.BlockSpec(narrow[1].shape, lambda i: (0, 0))]
        args += list(narrow)
    return pl.pallas_call(
        body, name=name, grid=(t // tt,), in_specs=in_specs, out_specs=[blk, blk, row],
        out_shape=[jax.ShapeDtypeStruct((t, d), F32), jax.ShapeDtypeStruct((t, d), BF16),
                   jax.ShapeDtypeStruct((1, d), F32)],
        compiler_params=_params(("arbitrary",)),
    )(*args)


def _loss_head(h2, tgt, wf, *, name):
    t, d = h2.shape
    tt = _pick(t, (256,))

    def body(x_ref, t_ref, w_ref, loss_ref, dx_ref, dxb_ref, dw_ref):
        xv = x_ref[...]
        r = lax.rsqrt(jnp.mean(xv * xv, axis=-1, keepdims=True) + EPS)
        xh = xv * r
        err = xh * w_ref[...] - t_ref[...]
        lpart = 0.5 * jnp.sum(jnp.mean(err * err, axis=-1, keepdims=True), axis=0, keepdims=True)
        dy = err * (1.0 / d)
        gw = dy * w_ref[...]
        dx = r * (gw - xh * jnp.mean(gw * xh, axis=-1, keepdims=True))
        dx_ref[...] = dx
        dxb_ref[...] = dx.astype(BF16)

        @pl.when(pl.program_id(0) == 0)
        def _():
            dw_ref[...] = jnp.zeros_like(dw_ref)
            loss_ref[...] = jnp.zeros_like(loss_ref)

        dw_ref[...] += jnp.sum(dy * xh, axis=0, keepdims=True)
        loss_ref[...] += jnp.broadcast_to(lpart, loss_ref.shape)

    blk = pl.BlockSpec((tt, d), lambda i: (i, 0))
    row = pl.BlockSpec((1, d), lambda i: (0, 0))
    lrow = pl.BlockSpec((1, LANES), lambda i: (0, 0))
    return pl.pallas_call(
        body, name=name, grid=(t // tt,),
        in_specs=[blk, blk, row], out_specs=[lrow, blk, blk, row],
        out_shape=[jax.ShapeDtypeStruct((1, LANES), F32), jax.ShapeDtypeStruct((t, d), F32),
                   jax.ShapeDtypeStruct((t, d), BF16), jax.ShapeDtypeStruct((1, d), F32)],
        compiler_params=_params(("arbitrary",)),
    )(h2, tgt, wf)


def _ffn_in(hn, w_g, w_u, *, name, tm):
    t, d = hn.shape
    tn = w_g.shape[2]
    f = w_g.shape[0] * tn
    tm = min(tm, t)

    def body(a_ref, g_ref, u_ref, gt_ref, up_ref, act_ref):
        a = a_ref[...]
        g = jnp.dot(a, g_ref[...], preferred_element_type=F32)
        u = jnp.dot(a, u_ref[...], preferred_element_type=F32)
        gt_ref[...] = g.astype(BF16)
        up_ref[...] = u.astype(BF16)
        act_ref[...] = (g * _sigmoid(g) * u).astype(BF16)

    wblk = pl.BlockSpec((None, d, tn), lambda j, i: (j, 0, 0))
    oblk = pl.BlockSpec((tm, tn), lambda j, i: (i, j))
    return pl.pallas_call(
        body, name=name, grid=(f // tn, t // tm),
        in_specs=[pl.BlockSpec((tm, d), lambda j, i: (i, 0)), wblk, wblk], out_specs=[oblk] * 3,
        out_shape=[jax.ShapeDtypeStruct((t, f), BF16)] * 3,
        compiler_params=_params(("parallel", "parallel")),
    )(hn, w_g, w_u)


def _ffn_back(dh2b, w_d, gt, up, *, name, tm, tn):
    t, d = dh2b.shape
    f = w_d.shape[0]
    tm = min(tm, t)

    n_sub = 2 if tm % 32 == 0 else 1

    def body(a_ref, w_ref, g_ref, u_ref, dg_ref, du_ref):
        w = w_ref[...]
        subs = [slice(i * (tm // n_sub), (i + 1) * (tm // n_sub)) for i in range(n_sub)]
        das = [lax.dot_general(a_ref[rs, :], w, (((1,), (1,)), ((), ())), preferred_element_type=F32) for rs in subs]
        for rs, da in zip(subs, das):
            g = g_ref[rs, :].astype(F32)
            sg = _sigmoid(g)
            dg_ref[rs, :] = (da * u_ref[rs, :].astype(F32) * sg * (1.0 + g * (1.0 - sg))).astype(BF16)
            du_ref[rs, :] = (da * g * sg).astype(BF16)

    oblk = pl.BlockSpec((tm, tn), lambda j, i: (i, j))
    return pl.pallas_call(
        body, name=name, grid=(f // tn, t // tm),
        in_specs=[pl.BlockSpec((tm, d), lambda j, i: (i, 0)), pl.BlockSpec((tn, d), lambda j, i: (j, 0)), oblk, oblk],
        out_specs=[oblk] * 2,
        out_shape=[jax.ShapeDtypeStruct((t, f), BF16)] * 2,
        compiler_params=_params(("parallel", "parallel")),
    )(dh2b, w_d, gt, up)


def _conv_cols(xs, cw_ref, cb, tt):
    cols = slice(cb * LANES, (cb + 1) * LANES)
    base = HALO - (CONV_W - 1)
    acc = xs[base:base + tt, cols] * cw_ref[0:1, cols]
    for j in range(1, CONV_W):
        acc = acc + xs[base + j:base + j + tt, cols] * cw_ref[j:j + 1, cols]
    return acc


def _fill_halo(xs, x_ref, xp_ref, tt):
    first = pl.program_id(0) == 0
    xs[0:HALO, :] = jnp.where(first, 0.0, xp_ref[...])
    xs[HALO:HALO + tt, :] = x_ref[...]


def _conv_specs(tt, ch):
    cur = pl.BlockSpec((tt, ch), lambda i: (i, 0))
    prev = pl.BlockSpec((HALO, ch), lambda i: (jnp.maximum(i * (tt // HALO) - 1, 0), 0))
    return cur, prev


def _conv_fwd(proj, conv_w, *, name):
    t = proj.shape[0]
    ch = conv_w.shape[1]
    nqk = HEADS * DK
    tt = _pick(t, (256,))
    cur, prev = _conv_specs(tt, ch)

    def body(x_ref, xp_ref, cw_ref, q_ref, k_ref, v_ref, xs):
        _fill_halo(xs, x_ref, xp_ref, tt)
        for cb in range(ch // LANES):
            c = _conv_cols(xs, cw_ref, cb, tt)
            s = c * _sigmoid(c)
            if cb < 2 * HEADS:
                s = s * lax.rsqrt(jnp.sum(s * s, axis=-1, keepdims=True) + EPS)
                if cb < HEADS:
                    q_ref[:, cb * LANES:(cb + 1) * LANES] = s * (DK ** -0.5)
                else:
                    k_ref[:, (cb - HEADS) * LANES:(cb - HEADS + 1) * LANES] = s
            else:
                v_ref[:, (cb - 2 * HEADS) * LANES:(cb - 2 * HEADS + 1) * LANES] = s

    return pl.pallas_call(
        body, name=name, grid=(t // tt,),
        in_specs=[cur, prev, pl.BlockSpec((CONV_W, ch), lambda i: (0, 0))],
        out_specs=[pl.BlockSpec((tt, nqk), lambda i: (i, 0)), pl.BlockSpec((tt, nqk), lambda i: (i, 0)),
                   pl.BlockSpec((tt, ch - 2 * nqk), lambda i: (i, 0))],
        out_shape=[jax.ShapeDtypeStruct((t, nqk), F32), jax.ShapeDtypeStruct((t, nqk), F32),
                   jax.ShapeDtypeStruct((t, ch - 2 * nqk), F32)],
        scratch_shapes=[pltpu.VMEM((HALO + tt, ch), F32)],
        compiler_params=_params(("arbitrary",)),
    )(proj, proj, conv_w)


def _conv_bwd_pre(proj, conv_w, dq, dk, dv, *, name):
    t = proj.shape[0]
    ch = conv_w.shape[1]
    nqk = HEADS * DK
    tt = _pick(t, (256,))
    cur, prev = _conv_specs(tt, ch)

    head = 2 * HALO

    def body(x_ref, xp_ref, cw_ref, dq_ref, dk_ref, dv_ref, dc_ref, xs):
        first = pl.program_id(0) == 0
        xs[0:HALO, :] = jnp.where(first, 0.0, xp_ref[...])
        xs[HALO:HALO + head, :] = x_ref[0:head, :]
        xb = x_ref[...].astype(BF16)
        r, cc = _iota2(tt)
        moved = [None] + [jnp.dot((cc == r - sh).astype(BF16), xb, preferred_element_type=F32)
                          for sh in range(1, CONV_W)]
        for cb in range(ch // LANES):
            cols = slice(cb * LANES, (cb + 1) * LANES)
            c_head = c_main = None
            for j in range(CONV_W):
                sh = CONV_W - 1 - j
                w_j = cw_ref[j:j + 1, cols]
                x_m = x_ref[head:tt, cols] if sh == 0 else moved[sh][head:tt, cols]
                x_h = xs[HALO - sh:HALO - sh + head, cols]
                c_main = x_m * w_j if c_main is None else c_main + x_m * w_j
                c_head = x_h * w_j if c_head is None else c_head + x_h * w_j
            c = jnp.concatenate([c_head, c_main], axis=0)
            sg = _sigmoid(c)
            s = c * sg
            if cb < 2 * HEADS:
                if cb < HEADS:
                    d, scale = dq_ref[:, cb * LANES:(cb + 1) * LANES], DK ** -0.5
                else:
                    d, scale = dk_ref[:, (cb - HEADS) * LANES:(cb - HEADS + 1) * LANES], 1.0
                rinv = lax.rsqrt(jnp.sum(s * s, axis=-1, keepdims=True) + EPS)
                ds = scale * rinv * (d - s * (rinv * rinv) * jnp.sum(d * s, axis=-1, keepdims=True))
            else:
                ds = dv_ref[:, (cb - 2 * HEADS) * LANES:(cb - 2 * HEADS + 1) * LANES]
            dc_ref[:, cb * LANES:(cb + 1) * LANES] = ds * sg * (1.0 + c * (1.0 - sg))

    return pl.pallas_call(
        body, name=name, grid=(t // tt,),
        in_specs=[cur, prev, pl.BlockSpec((CONV_W, ch), lambda i: (0, 0)),
                  pl.BlockSpec((tt, nqk), lambda i: (i, 0)), pl.BlockSpec((tt, nqk), lambda i: (i, 0)),
                  pl.BlockSpec((tt, ch - 2 * nqk), lambda i: (i, 0))],
        out_specs=pl.BlockSpec((tt, ch), lambda i: (i, 0)),
        out_shape=jax.ShapeDtypeStruct((t, ch), F32),
        scratch_shapes=[pltpu.VMEM((HALO + head, ch), F32)],
        compiler_params=_params(("arbitrary",)),
    )(proj, proj, conv_w, dq, dk, dv)


def _conv_bwd(proj, dc, conv_w, dproj, *, name):
    t = proj.shape[0]
    ch = conv_w.shape[1]
    tt = _pick(t, (256,))
    nt = t // tt
    cur = pl.BlockSpec((tt, ch), lambda i: (i, 0))
    nxt = pl.BlockSpec((HALO, ch), lambda i: (jnp.minimum((i + 1) * (tt // HALO), t // HALO - 1), 0))

    tail = 2 * HALO
    main = tt - tail

    def body(x_ref, d_ref, dn_ref, cw_ref, alias_ref, dx_ref, dw_ref, ds):
        last = pl.program_id(0) == nt - 1
        ds[0:tail, :] = d_ref[main:tt, :]
        ds[tail:tail + HALO, :] = jnp.where(last, 0.0, dn_ref[...])

        @pl.when(pl.program_id(0) == 0)
        def _():
            dw_ref[...] = jnp.zeros_like(dw_ref)

        dcb = d_ref[...].astype(BF16)
        r, c = _iota2(tt)
        moved = [None] + [jnp.dot((c == r + sh).astype(BF16), dcb, preferred_element_type=F32)
                          for sh in range(1, CONV_W)]
        for cb in range(ch // LANES):
            cols = slice(cb * LANES, (cb + 1) * LANES)
            x_m, x_t = x_ref[0:main, cols], x_ref[main:tt, cols]
            acc_m = acc_t = None
            for j in range(CONV_W):
                sh = CONV_W - 1 - j
                w_j = cw_ref[j:j + 1, cols]
                dj_m = d_ref[0:main, cols] if sh == 0 else moved[sh][0:main, cols]
                dj_t = ds[sh:sh + tail, cols]
                acc_m = dj_m * w_j if acc_m is None else acc_m + dj_m * w_j
                acc_t = dj_t * w_j if acc_t is None else acc_t + dj_t * w_j
                dw_ref[j:j + 1, cols] += (jnp.sum(dj_m * x_m, axis=0, keepdims=True)
                                          + jnp.sum(dj_t * x_t, axis=0, keepdims=True))
            dx_ref[0:main, cols] = acc_m.astype(BF16)
            dx_ref[main:tt, cols] = acc_t.astype(BF16)

    return pl.pallas_call(
        body, name=name, grid=(nt,),
        in_specs=[cur, cur, nxt, pl.BlockSpec((CONV_W, ch), lambda i: (0, 0)), pl.BlockSpec(memory_space=pl.ANY)],
        out_specs=[pl.BlockSpec((tt, ch), lambda i: (i, 0)), pl.BlockSpec((HALO, ch), lambda i: (0, 0))],
        out_shape=[jax.ShapeDtypeStruct(dproj.shape, BF16), jax.ShapeDtypeStruct((HALO, ch), F32)],
        scratch_shapes=[pltpu.VMEM((tail + HALO, ch), F32)],
        input_output_aliases={4: 0},
        compiler_params=_params(("arbitrary",)),
    )(proj, dc, dc, conv_w, dproj)


def _bg_fwd(projs, alog_row, dtb_row, *, name):
    t = projs.shape[0]
    tt = _pick(t, (512, 256))

    def body(p_ref, al_ref, db_ref, o_ref):
        p = p_ref[...]
        lane = lax.broadcasted_iota(jnp.int32, p.shape, 1)
        z = p + db_ref[...]
        sp = jnp.maximum(z, 0.0) + jnp.log1p(jnp.exp(-jnp.abs(z)))
        g = -jnp.exp(al_ref[...]) * sp
        o_ref[...] = jnp.where(lane < HEADS, _sigmoid(p), jnp.where(lane < 2 * HEADS, g, 0.0))

    blk = pl.BlockSpec((tt, LANES), lambda i: (i, 0))
    row = pl.BlockSpec((1, LANES), lambda i: (0, 0))
    return pl.pallas_call(
        body, name=name, grid=(t // tt,), in_specs=[blk, row, row], out_specs=blk,
        out_shape=jax.ShapeDtypeStruct((t, LANES), F32), compiler_params=_params(("parallel",)),
    )(projs, alog_row, dtb_row)


def _bg_bwd(projs, dbg, alog_row, dtb_row, *, name):
    t = projs.shape[0]
    tt = _pick(t, (512, 256))

    def body(p_ref, d_ref, al_ref, db_ref, o_ref, dal_ref, ddb_ref):
        p, d = p_ref[...], d_ref[...]
        lane = lax.broadcasted_iota(jnp.int32, p.shape, 1)
        isg = (lane >= HEADS) & (lane < 2 * HEADS)
        be = _sigmoid(p)
        z = p + db_ref[...]
        sp = jnp.maximum(z, 0.0) + jnp.log1p(jnp.exp(-jnp.abs(z)))
        ea = jnp.exp(al_ref[...])
        d_aa = jnp.where(isg, d * (-ea) * _sigmoid(z), 0.0)
        o_ref[...] = jnp.where(lane < HEADS, d * be * (1.0 - be), d_aa).astype(BF16)

        @pl.when(pl.program_id(0) == 0)
        def _():
            dal_ref[...] = jnp.zeros_like(dal_ref)
            ddb_ref[...] = jnp.zeros_like(ddb_ref)

        dal_ref[...] += jnp.sum(jnp.where(isg, d * (-ea) * sp, 0.0), axis=0, keepdims=True)
        ddb_ref[...] += jnp.sum(d_aa, axis=0, keepdims=True)

    blk = pl.BlockSpec((tt, LANES), lambda i: (i, 0))
    row = pl.BlockSpec((1, LANES), lambda i: (0, 0))
    return pl.pallas_call(
        body, name=name, grid=(t // tt,), in_specs=[blk, blk, row, row], out_specs=[blk, row, row],
        out_shape=[jax.ShapeDtypeStruct((t, LANES), BF16), jax.ShapeDtypeStruct((1, LANES), F32),
                   jax.ShapeDtypeStruct((1, LANES), F32)],
        compiler_params=_params(("arbitrary",)),
    )(projs, dbg, alog_row, dtb_row)


def _hn_specs(n_chunks, rev=False):
    def nn(n):
        return n_chunks - 1 - n if rev else n
    tok = lambda w: pl.BlockSpec((CHUNK, HB * w), lambda g, n: (nn(n), g))
    per = lambda a, b: pl.BlockSpec((HB, None, a, b), lambda g, n: (g, nn(n), 0, 0))
    return tok, per


def _hcols(hh, w):
    return slice(hh * w, (hh + 1) * w)


def _decay(gc_col, gc_row):
    r, c = _iota2(CHUNK)
    d = jnp.exp(jnp.minimum(gc_col - gc_row, 0.0))
    return jnp.where(r >= c, d, 0.0), jnp.where(r > c, d, 0.0)


def _gdn_prep(q, k, v, beta_r, g_r, *, name):
    t = q.shape[0]
    n_chunks = t // CHUNK
    cps = 2 if n_chunks % 2 == 0 else 1
    tok = lambda w: pl.BlockSpec((cps * CHUNK, HB * w), lambda g, n: (n, g))
    per = lambda a, b: pl.BlockSpec((HB, cps, a, b), lambda g, n: (g, n, 0, 0))

    def body(q_ref, k_ref, v_ref, b_ref, g_ref, gc_ref, ti_ref, u_ref, w_ref, p_ref):
        r, c = _iota2(CHUNK)
        its = [(h, cc) for cc in range(cps) for h in range(HB)]
        rows = lambda cc: slice(cc * CHUNK, (cc + 1) * CHUNK)
        n = range(len(its))
        kk = [k_ref[rows(cc), _hcols(h, DK)] for h, cc in its]
        gc_col = [jnp.sum(jnp.where(c <= r, jnp.broadcast_to(g_ref[h, cc], (CHUNK, CHUNK)), 0.0), axis=1,
                          keepdims=True) for h, cc in its]
        gc_row = [_col_to_row(gc_col[i]) for i in n]
        beta_col = [_row_to_col(b_ref[h, cc]) for h, cc in its]
        dec = [_decay(gc_col[i], gc_row[i]) for i in n]
        kb = [kk[i] * beta_col[i] for i in n]
        for i, (h, cc) in enumerate(its):
            gc_ref[h, cc] = gc_row[i]
            p_ref[h, cc] = (_dot_nt(q_ref[rows(cc), _hcols(h, DK)], kk[i]) * dec[i][0]).astype(BF16)
        pw = [_dot_nt(kb[i], kk[i]) * dec[i][1] for i in n]
        y = [-pw[i] for i in n]
        for _ in range(5):
            pw = [_dot(pw[i], pw[i]) for i in n]
            yp = [_dot(y[i], pw[i]) for i in n]
            y = [y[i] + pw[i] + yp[i] for i in n]
        vb = [v_ref[rows(cc), _hcols(h, DV)] * beta_col[i] for i, (h, cc) in enumerate(its)]
        kbe = [kb[i] * jnp.exp(gc_col[i]) for i in n]
        yv = [_dot(y[i], vb[i]) for i in n]
        yk = [_dot(y[i], kbe[i]) for i in n]
        for i, (h, cc) in enumerate(its):
            ti_ref[h, cc] = y[i].astype(BF16)
            u_ref[rows(cc), _hcols(h, DV)] = vb[i] + yv[i]
            w_ref[rows(cc), _hcols(h, DK)] = (kbe[i] + yk[i]).astype(BF16)

    return pl.pallas_call(
        body, name=name, grid=(HEADS // HB, n_chunks // cps),
        in_specs=[tok(DK), tok(DK), tok(DV), per(1, CHUNK), per(1, CHUNK)],
        out_specs=[per(1, CHUNK), per(CHUNK, CHUNK), tok(DV), tok(DK), per(CHUNK, CHUNK)],
        out_shape=[jax.ShapeDtypeStruct((HEADS, n_chunks, 1, CHUNK), F32),
                   jax.ShapeDtypeStruct((HEADS, n_chunks, CHUNK, CHUNK), BF16),
                   jax.ShapeDtypeStruct((t, HEADS * DV), F32), jax.ShapeDtypeStruct((t, HEADS * DK), BF16),
                   jax.ShapeDtypeStruct((HEADS, n_chunks, CHUNK, CHUNK), BF16)],
        compiler_params=_params(("parallel", "parallel")),
    )(q, k, v, beta_r, g_r)


def _interleave(*step_gens):
    live = list(step_gens)
    while live:
        for g in list(live):
            try:
                next(g)
            except StopIteration:
                live.remove(g)


def _gdn_scan_steps(q_ref, k_ref, u_ref, w_ref, p_ref, gc_ref, o_ref, s_ref, st):
    hs = range(HB)
    s = [st[h] for h in hs]
    gc_row = [gc_ref[h] for h in hs]
    gc_col = [_row_to_col(gc_row[h]) for h in hs]
    glast = [gc_row[h][:, CHUNK - 1:CHUNK] for h in hs]
    for h in hs:
        s_ref[h] = s[h].astype(BF16)
    yield
    ws = [_dot(w_ref[:, _hcols(h, DK)], s[h]) for h in hs]
    yield
    qs = [_dot(q_ref[:, _hcols(h, DK)] * jnp.exp(gc_col[h]), s[h]) for h in hs]
    vn = [u_ref[:, _hcols(h, DV)] - ws[h] for h in hs]
    yield
    pv = [_dot(p_ref[h], vn[h]) for h in hs]
    yield
    kv = [_dot_tn(k_ref[:, _hcols(h, DK)] * jnp.exp(glast[h] - gc_col[h]), vn[h]) for h in hs]
    yield
    for h in hs:
        o_ref[:, _hcols(h, DV)] = qs[h] + pv[h]
        st[h] = s[h] * jnp.exp(glast[h]) + kv[h]


def _gdn_bwd_steps(q_ref, k_ref, v_ref, b_ref, gc_ref, ti_ref, u_ref, w_ref, s_ref, do_ref,
                   dq_ref, dk_ref, dv_ref, db_ref, dg_ref, dst):
    if True:
        r, c = _iota2(CHUNK)
        rows = lax.broadcasted_iota(jnp.int32, (CHUNK, 1), 0)
        rsum = lambda a: jnp.sum(a, axis=1, keepdims=True)
        hs = range(HB)
        qq = [q_ref[:, _hcols(h, DK)] for h in hs]
        kk = [k_ref[:, _hcols(h, DK)] for h in hs]
        ww = [w_ref[:, _hcols(h, DK)] for h in hs]
        uu = [u_ref[:, _hcols(h, DV)] for h in hs]
        d_o = [do_ref[:, _hcols(h, DV)] for h in hs]
        s = [s_ref[h].astype(F32) for h in hs]
        d_s = [dst[h] for h in hs]
        gc_row = [gc_ref[h] for h in hs]
        gc_col = [_row_to_col(gc_row[h]) for h in hs]
        beta_col = [_row_to_col(b_ref[h]) for h in hs]
        eg = [jnp.exp(gc_col[h]) for h in hs]
        glast = [gc_row[h][:, CHUNK - 1:CHUNK] for h in hs]
        eglast = [jnp.exp(glast[h]) for h in hs]
        e2 = [jnp.exp(glast[h] - gc_col[h]) for h in hs]
        dec = [_decay(gc_col[h], gc_row[h]) for h in hs]
        kb = [kk[h] * beta_col[h] for h in hs]
        ke = [kk[h] * e2[h] for h in hs]
        qe = [qq[h] * eg[h] for h in hs]
        yield
        kkt = [_dot_nt(kb[h], kk[h]) for h in hs]
        qk = [_dot_nt(qq[h], kk[h]) for h in hs]
        ws = [_dot(ww[h], s[h]) for h in hs]
        yield
        a = [kkt[h] * dec[h][1] for h in hs]
        pp = [qk[h] * dec[h][0] for h in hs]
        vn = [uu[h] - ws[h] for h in hs]
        t1 = [_dot_tn(pp[h], d_o[h]) for h in hs]
        t2 = [_dot(ke[h], d_s[h]) for h in hs]
        yield
        dqe = [_dot_nt(d_o[h], s[h]) for h in hs]
        dke = [_dot_nt(vn[h], d_s[h]) for h in hs]
        dqk = [_dot_nt(d_o[h], vn[h]) * dec[h][0] for h in hs]
        yield
        dvn = [t1[h] + t2[h] for h in hs]
        t3 = [_dot_tn(qe[h], d_o[h]) for h in hs]
        t4 = [_dot_tn(ww[h], dvn[h]) for h in hs]
        dw = [-_dot_nt(dvn[h], s[h]) for h in hs]
        yield
        de2 = [rsum(dke[h] * ke[h]) for h in hs]
        dglast = [jnp.sum(de2[h], axis=0, keepdims=True)
                  + eglast[h] * jnp.sum(rsum(d_s[h] * s[h]), axis=0, keepdims=True) for h in hs]
        for h in hs:
            dst[h] = d_s[h] * eglast[h] + t3[h] - t4[h]
        yy = [ti_ref[h] for h in hs]
        t5 = [_dot_tn(yy[h], dvn[h]) for h in hs]
        t6 = [_dot_tn(yy[h], dw[h]) for h in hs]
        yield
        dvb = [dvn[h] + t5[h] for h in hs]
        dkbe = [dw[h] + t6[h] for h in hs]
        t7 = [_dot_nt(dvb[h], uu[h]) for h in hs]
        t8 = [_dot_nt(dkbe[h], ww[h]) for h in hs]
        yield
        d_a = [jnp.where(r > c, -(t7[h] + t8[h]), 0.0) for h in hs]
        dm = [d_a[h] * dec[h][1] for h in hs]
        t9 = [_dot(dm[h], kk[h]) for h in hs]
        t10 = [_dot(dqk[h], kk[h]) for h in hs]
        yield
        t11 = [_dot_tn(dqk[h], qq[h]) for h in hs]
        t12 = [_dot_tn(dm[h], kb[h]) for h in hs]
        yield
        for h in hs:
            dkb = t9[h] + dkbe[h] * eg[h]
            e_mat = d_a[h] * a[h] + dqk[h] * qk[h]
            dgc = (rsum(dqe[h] * qe[h]) - de2[h] + rsum(dkbe[h] * kb[h] * eg[h]) + rsum(e_mat)
                   - _row_to_col(jnp.sum(e_mat, axis=0, keepdims=True)))
            dgc = dgc + jnp.where(rows == CHUNK - 1, dglast[h], 0.0)
            dq_ref[:, _hcols(h, DK)] = dqe[h] * eg[h] + t10[h]
            dk_ref[:, _hcols(h, DK)] = t11[h] + dke[h] * e2[h] + t12[h] + dkb * beta_col[h]
            dv_ref[:, _hcols(h, DV)] = dvb[h] * beta_col[h]
            dbeta = rsum(dkb * kk[h]) + rsum(dvb[h] * v_ref[:, _hcols(h, DV)])
            db_ref[h] = _col_to_row(dbeta)
            dg_ref[h] = jnp.sum(jnp.where(r >= c, jnp.broadcast_to(dgc, (CHUNK, CHUNK)), 0.0), axis=0, keepdims=True)


def _rot(x, cos2, sin2):
    return x * cos2 + pltpu.roll(x, DK // 2, 1) * sin2


def _unrot(d, cos2, sin2):
    return d * cos2 + pltpu.roll(d * sin2, DK // 2, 1)


def _ret_consts(lg):
    r, c = _iota2(CHUNK)
    dm = jnp.where(r >= c, jnp.exp((r - c).astype(F32) * lg), 0.0)
    pos = lax.broadcasted_iota(jnp.int32, (CHUNK, 1), 0).astype(F32)
    return dm, jnp.exp((pos + 1.0) * lg), jnp.exp((CHUNK - 1.0 - pos) * lg), jnp.exp(CHUNK * lg)


def _ret_fwd_steps(q_ref, k_ref, v_ref, cos_ref, sin_ref, lg_ref, o_ref, s_ref, st):
    cos2v, sin2v = cos_ref[...], sin_ref[...]
    hs = range(HB)
    s = [st[h] for h in hs]
    for h in hs:
        s_ref[h] = s[h].astype(BF16)
    cst = [_ret_consts(lg_ref[h][:, 0:1]) for h in hs]
    qq = [_rot(q_ref[:, _hcols(h, DK)], cos2v, sin2v) for h in hs]
    kk = [_rot(k_ref[:, _hcols(h, DK)], cos2v, sin2v) * (DK ** -0.5) for h in hs]
    vv = [v_ref[:, _hcols(h, DV)] for h in hs]
    yield
    sc = [_dot_nt(qq[h], kk[h]) * cst[h][0] for h in hs]
    yield
    qs = [_dot(qq[h], s[h]) for h in hs]
    yield
    kv = [_dot_tn(kk[h] * cst[h][2], vv[h]) for h in hs]
    yield
    sv = [_dot(sc[h], vv[h]) for h in hs]
    yield
    for h in hs:
        o_ref[:, _hcols(h, DV)] = sv[h] + qs[h] * cst[h][1]
        st[h] = s[h] * cst[h][3] + kv[h]


def _mix_fwd(q, k, u, w, p, gc_r, proj, cos2, sin2, lg_tab, col_q, col_k, col_v, *, name):
    t = q.shape[0]
    n_chunks = t // CHUNK
    tok, per = _hn_specs(n_chunks)
    bq, bk, bv = col_q // (HB * DK), col_k // (HB * DK), col_v // (HB * DV)

    def body(*refs):
        st_a, st_b = refs[-2:]

        @pl.when(pl.program_id(1) == 0)
        def _():
            st_a[...] = jnp.zeros_like(st_a)
            st_b[...] = jnp.zeros_like(st_b)

        _interleave(_gdn_scan_steps(*refs[0:6], refs[12], refs[13], st_a),
                    _ret_fwd_steps(*refs[6:12], refs[14], refs[15], st_b))

    state = jax.ShapeDtypeStruct((HEADS, n_chunks, DK, DV), BF16)
    return pl.pallas_call(
        body, name=name, grid=(HEADS // HB, n_chunks),
        in_specs=[tok(DK), tok(DK), tok(DV), tok(DK), per(CHUNK, CHUNK), per(1, CHUNK),
                  pl.BlockSpec((CHUNK, HB * DK), lambda g, n: (n, bq + g)),
                  pl.BlockSpec((CHUNK, HB * DK), lambda g, n: (n, bk + g)),
                  pl.BlockSpec((CHUNK, HB * DV), lambda g, n: (n, bv + g)),
                  pl.BlockSpec((CHUNK, DK), lambda g, n: (n, 0)), pl.BlockSpec((CHUNK, DK), lambda g, n: (n, 0)),
                  pl.BlockSpec((HB, 1, LANES), lambda g, n: (g, 0, 0))],
        out_specs=[tok(DV), per(DK, DV), tok(DV), per(DK, DV)],
        out_shape=[jax.ShapeDtypeStruct((t, HEADS * DV), F32), state, jax.ShapeDtypeStruct((t, HEADS * DV), F32), state],
        scratch_shapes=[pltpu.VMEM((HB, DK, DV), F32)] * 2,
        compiler_params=_params(("arbitrary", "arbitrary")),
    )(q, k, u, w, p, gc_r, proj, proj, proj, cos2, sin2, lg_tab)


def _ret_bwd_steps(q_ref, k_ref, v_ref, cos_ref, sin_ref, lg_ref, s_ref, do_ref, d_ref, dst):
    nqk = HEADS * DK
    cos2v, sin2v = cos_ref[...], sin_ref[...]
    hs = range(HB)
    s = [s_ref[h].astype(F32) for h in hs]
    d_s = [dst[h] for h in hs]
    d_o = [do_ref[:, _hcols(h, DV)] for h in hs]
    cst = [_ret_consts(lg_ref[h][:, 0:1]) for h in hs]
    qq = [_rot(q_ref[:, _hcols(h, DK)], cos2v, sin2v) for h in hs]
    kk = [_rot(k_ref[:, _hcols(h, DK)], cos2v, sin2v) * (DK ** -0.5) for h in hs]
    vv = [v_ref[:, _hcols(h, DV)] for h in hs]
    dxo = [d_o[h] * cst[h][1] for h in hs]
    yield
    sc = [_dot_nt(qq[h], kk[h]) * cst[h][0] for h in hs]
    dsc = [_dot_nt(d_o[h], vv[h]) * cst[h][0] for h in hs]
    yield
    t1 = [_dot(kk[h] * cst[h][2], d_s[h]) for h in hs]
    t2 = [_dot_nt(dxo[h], s[h]) for h in hs]
    yield
    t3 = [_dot_nt(vv[h], d_s[h]) for h in hs]
    t4 = [_dot_tn(qq[h], dxo[h]) for h in hs]
    yield
    t5 = [_dot_tn(sc[h], d_o[h]) for h in hs]
    t6 = [_dot(dsc[h], kk[h]) for h in hs]
    yield
    t7 = [_dot_tn(dsc[h], qq[h]) for h in hs]
    yield
    for h in hs:
        dst[h] = d_s[h] * cst[h][3] + t4[h]
        d_ref[:, 2 * nqk + h * DV:2 * nqk + (h + 1) * DV] = (t5[h] + t1[h]).astype(BF16)
        d_ref[:, _hcols(h, DK)] = _unrot(t6[h] + t2[h], cos2v, sin2v).astype(BF16)
        d_ref[:, nqk + h * DK:nqk + (h + 1) * DK] = (
            _unrot(t7[h] + t3[h] * cst[h][2], cos2v, sin2v) * (DK ** -0.5)).astype(BF16)


def _mix_bwd(q, k, v, beta_r, gc_r, ti, u, w, s_a, do_a, proj, cos2, sin2, lg_tab, s_b, do_b, dproj,
             col_q, col_k, col_v, *, name):
    t = q.shape[0]
    n_chunks = t // CHUNK
    tok, per = _hn_specs(n_chunks, rev=True)
    bq, bk, bv = col_q // (HB * DK), col_k // (HB * DK), col_v // (HB * DV)
    rv = lambda n: n_chunks - 1 - n
    nqk, wid = HEADS * DK, HEADS * (2 * DK + DV)
    assert HB == HEADS and col_k == col_q + nqk and col_v == col_k + nqk and col_q % wid == 0
    n_a, n_b = 10, 8

    def body(*refs):
        ins_a, ins_b = refs[:n_a], refs[n_a:n_a + n_b]
        outs = refs[n_a + n_b + 1:n_a + n_b + 7]
        dst_a, dst_b = refs[-2:]

        @pl.when(pl.program_id(1) == 0)
        def _():
            dst_a[...] = jnp.zeros_like(dst_a)
            dst_b[...] = jnp.zeros_like(dst_b)

        _interleave(_gdn_bwd_steps(*ins_a, *outs[:5], dst_a), _ret_bwd_steps(*ins_b, outs[5], dst_b))

    rows = jax.ShapeDtypeStruct((HEADS, n_chunks, 1, CHUNK), F32)
    return pl.pallas_call(
        body, name=name, grid=(HEADS // HB, n_chunks),
        in_specs=[tok(DK), tok(DK), tok(DV), per(1, CHUNK), per(1, CHUNK), per(CHUNK, CHUNK), tok(DV), tok(DK),
                  per(DK, DV), tok(DV),
                  pl.BlockSpec((CHUNK, HB * DK), lambda g, n: (rv(n), bq + g)),
                  pl.BlockSpec((CHUNK, HB * DK), lambda g, n: (rv(n), bk + g)),
                  pl.BlockSpec((CHUNK, HB * DV), lambda g, n: (rv(n), bv + g)),
                  pl.BlockSpec((CHUNK, DK), lambda g, n: (rv(n), 0)),
                  pl.BlockSpec((CHUNK, DK), lambda g, n: (rv(n), 0)),
                  pl.BlockSpec((HB, 1, LANES), lambda g, n: (g, 0, 0)),
                  per(DK, DV), tok(DV), pl.BlockSpec(memory_space=pl.ANY)],
        out_specs=[tok(DK), tok(DK), tok(DV), per(1, CHUNK), per(1, CHUNK),
                   pl.BlockSpec((CHUNK, wid), lambda g, n: (rv(n), col_q // wid))],
        out_shape=[jax.ShapeDtypeStruct((t, HEADS * DK), F32), jax.ShapeDtypeStruct((t, HEADS * DK), F32),
                   jax.ShapeDtypeStruct((t, HEADS * DV), F32), rows, rows, jax.ShapeDtypeStruct(dproj.shape, BF16)],
        scratch_shapes=[pltpu.VMEM((HB, DK, DV), F32)] * 2,
        input_output_aliases={n_a + n_b: 5},
        compiler_params=_params(("arbitrary", "arbitrary")),
    )(q, k, v, beta_r, gc_r, ti, u, w, s_a, do_a, proj, proj, proj, cos2, sin2, lg_tab, s_b, do_b, dproj)


def _merge_parts(oa, ob, z, rg, ga, gb):
    ra = lax.rsqrt(jnp.mean(oa * oa, axis=-1, keepdims=True) + EPS)
    xa = oa * ra
    mu = jnp.mean(ob, axis=-1, keepdims=True)
    cen = ob - mu
    rb = lax.rsqrt(jnp.mean(cen * cen, axis=-1, keepdims=True) + EPS)
    xb = cen * rb
    sz, sr = _sigmoid(z), _sigmoid(rg)
    return ra, xa, rb, xb, sz, sr, _sigmoid(ga), _sigmoid(gb)


def _merge_specs(tt, d, cz, crg, cga, cgb):
    blk = pl.BlockSpec((tt, d), lambda i: (i, 0))
    pcol = lambda c: pl.BlockSpec((tt, d), lambda i: (i, c // d))
    row = pl.BlockSpec((1, d), lambda i: (0, 0))
    return blk, [blk, blk, pcol(cz), pcol(crg), pcol(cga), pcol(cgb), row, row], row


def _merge_fwd(oa, ob, proj, wa, wb, cz, crg, cga, cgb, *, name):
    t, d = oa.shape
    tt = _pick(t, (256,))
    blk, in_specs, _ = _merge_specs(tt, d, cz, crg, cga, cgb)

    def body(oa_ref, ob_ref, z_ref, rg_ref, ga_ref, gb_ref, wa_ref, wb_ref, o_ref):
        for h in range(HEADS):
            cols = slice(h * DV, (h + 1) * DV)
            z, rg = z_ref[:, cols], rg_ref[:, cols]
            _, xa, _, xb, sz, sr, sga, sgb = _merge_parts(
                oa_ref[:, cols], ob_ref[:, cols], z, rg, ga_ref[:, cols], gb_ref[:, cols])
            o_a = xa * wa_ref[:, cols] * (z * sz)
            o_b = xb * wb_ref[:, cols] * (rg * sr)
            o_ref[:, cols] = (sga * o_a + sgb * o_b).astype(BF16)

    return pl.pallas_call(
        body, name=name, grid=(t // tt,), in_specs=in_specs, out_specs=blk,
        out_shape=jax.ShapeDtypeStruct((t, d), BF16), compiler_params=_params(("parallel",)),
    )(oa, ob, proj, proj, proj, proj, wa, wb)


def _merge_bwd(dmix, oa, ob, proj, wa, wb, cz, crg, cga, cgb, *, name, side):
    t, d = oa.shape
    tt = _pick(t, (256,))
    nt = t // tt
    blk, in_specs, row = _merge_specs(tt, d, cz, crg, cga, cgb)
    assert (crg, cga, cgb) == (cz + d, cz + 2 * d, cz + 3 * d) and cz % (4 * d) == 0
    n_si, n_so = len(side.ins), len(side.out_shapes)
    assert len(side.phases) == 2

    def body(*refs):
        dm_ref, oa_ref, ob_ref, z_ref, rg_ref, ga_ref, gb_ref, wa_ref, wb_ref = refs[:9]
        s_in = refs[9:9 + n_si]
        doa_ref, dob_ref, dp_ref, dwa_ref, dwb_ref = refs[9 + n_si:14 + n_si]
        s_out, ssem, rsem = refs[14 + n_si:14 + n_si + n_so], refs[-2], refs[-1]
        dz_ref, drg_ref, dga_ref, dgb_ref = [dp_ref.at[:, i * d:(i + 1) * d] for i in range(4)]

        @pl.when(pl.program_id(0) == 0)
        def _():
            side.phases[0](s_in, s_out, ssem, rsem)
            dwa_ref[...] = jnp.zeros_like(dwa_ref)
            dwb_ref[...] = jnp.zeros_like(dwb_ref)

        for h in range(HEADS):
            cols = slice(h * DV, (h + 1) * DV)
            z, rg, wa_h, wb_h, dmx = z_ref[:, cols], rg_ref[:, cols], wa_ref[:, cols], wb_ref[:, cols], dm_ref[:, cols]
            ra, xa, rb, xb, sz, sr, sga, sgb = _merge_parts(
                oa_ref[:, cols], ob_ref[:, cols], z, rg, ga_ref[:, cols], gb_ref[:, cols])
            na, nb = xa * wa_h, xb * wb_h
            sil_z, sil_r = z * sz, rg * sr
            o_a, o_b = na * sil_z, nb * sil_r
            dga_ref[:, cols] = (dmx * o_a * sga * (1.0 - sga)).astype(BF16)
            dgb_ref[:, cols] = (dmx * o_b * sgb * (1.0 - sgb)).astype(BF16)
            d_oa, d_ob = dmx * sga, dmx * sgb
            dz_ref[:, cols] = (d_oa * na * sz * (1.0 + z * (1.0 - sz))).astype(BF16)
            drg_ref[:, cols] = (d_ob * nb * sr * (1.0 + rg * (1.0 - sr))).astype(BF16)
            dna, dnb = d_oa * sil_z, d_ob * sil_r
            dwa_ref[:, cols] += jnp.sum(dna * xa, axis=0, keepdims=True)
            dwb_ref[:, cols] += jnp.sum(dnb * xb, axis=0, keepdims=True)
            gwa, gwb = dna * wa_h, dnb * wb_h
            doa_ref[:, cols] = ra * (gwa - xa * jnp.mean(gwa * xa, axis=-1, keepdims=True))
            dob_ref[:, cols] = rb * (gwb - jnp.mean(gwb, axis=-1, keepdims=True)
                                     - xb * jnp.mean(gwb * xb, axis=-1, keepdims=True))

        @pl.when(pl.program_id(0) == nt - 1)
        def _():
            side.phases[1](s_in, s_out, ssem, rsem)

    hbm = pl.BlockSpec(memory_space=pl.ANY)
    return pl.pallas_call(
        body, name=name, grid=(nt,), in_specs=[blk] + in_specs + [hbm] * n_si,
        out_specs=[blk, blk, pl.BlockSpec((tt, 4 * d), lambda i: (i, cz // (4 * d))), row, row] + [hbm] * n_so,
        out_shape=[jax.ShapeDtypeStruct((t, d), F32)] * 2 + [jax.ShapeDtypeStruct(proj.shape, BF16)]
        + [jax.ShapeDtypeStruct((1, d), F32)] * 2 + side.out_shapes,
        scratch_shapes=side.sems(),
        compiler_params=_params(("arbitrary",)),
    )(dmix, oa, ob, proj, proj, proj, proj, wa, wb, *side.ins)


def _row_block(rows, cols, itemsize=4, target=1 << 20):
    for rb in (512, 256, 128, 64, 32, 16, 8):
        if rows % rb == 0 and rb * cols * itemsize <= target:
            return rb
    return rows


def _adamw(w, g, m, v, *, name):
    rows, cols = w.shape
    rb = _row_block(rows, cols)

    def body(w_ref, g_ref, m_ref, v_ref, d_ref, nm_ref, nv_ref):
        gg = g_ref[...]
        mm = ADAM_B1 * m_ref[...] + (1.0 - ADAM_B1) * gg
        vv = ADAM_B2 * v_ref[...] + (1.0 - ADAM_B2) * (gg * gg)
        m_hat = mm / (1.0 - ADAM_B1 ** ADAM_STEP)
        v_hat = vv / (1.0 - ADAM_B2 ** ADAM_STEP)
        d_ref[...] = -ADAM_LR * (m_hat / (jnp.sqrt(v_hat) + ADAM_EPS) + ADAM_WD * w_ref[...])
        nm_ref[...] = mm
        nv_ref[...] = vv

    blk = pl.BlockSpec((rb, cols), lambda i: (i, 0))
    return pl.pallas_call(
        body, name=name, grid=(rows // rb,), in_specs=[blk] * 4, out_specs=[blk] * 3,
        out_shape=[jax.ShapeDtypeStruct((rows, cols), F32)] * 3, compiler_params=_params(("parallel",)),
    )(w, g, m, v)


def _adamw_halves(w, mine, other, c_idx, m, v, *, name):
    rows, cols = w.shape
    hr = rows // 2
    rb = _row_block(hr, cols)
    nb = hr // rb

    def body(c_ref, w_ref, a_ref, b_ref, m_ref, v_ref, g_ref, d_ref, nm_ref, nv_ref):
        gg = jnp.where(pl.program_id(0) // nb == c_ref[0], a_ref[...], b_ref[...])
        mm = ADAM_B1 * m_ref[...] + (1.0 - ADAM_B1) * gg
        vv = ADAM_B2 * v_ref[...] + (1.0 - ADAM_B2) * (gg * gg)
        m_hat = mm / (1.0 - ADAM_B1 ** ADAM_STEP)
        v_hat = vv / (1.0 - ADAM_B2 ** ADAM_STEP)
        g_ref[...] = gg
        d_ref[...] = -ADAM_LR * (m_hat / (jnp.sqrt(v_hat) + ADAM_EPS) + ADAM_WD * w_ref[...])
        nm_ref[...] = mm
        nv_ref[...] = vv

    blk = pl.BlockSpec((rb, cols), lambda i, c: (i, 0))
    half = pl.BlockSpec((rb, cols), lambda i, c: (i % nb, 0))
    return pl.pallas_call(
        body, name=name,
        grid_spec=pltpu.PrefetchScalarGridSpec(
            num_scalar_prefetch=1, grid=(rows // rb,), in_specs=[blk, half, half, blk, blk], out_specs=[blk] * 4),
        out_shape=[jax.ShapeDtypeStruct((rows, cols), F32)] * 4, compiler_params=_params(("parallel",)),
    )(c_idx, w, mine, other, m, v)


def _pair_add(g, rsib, c_idx, *, name):
    _, _, hr, cols = g.shape
    rb = _row_block(hr, cols, 2)

    def body(c_ref, g_ref, r_ref, o_ref):
        o_ref[...] = (g_ref[...].astype(F32) + r_ref[...].astype(F32)).astype(BF16)

    return pl.pallas_call(
        body, name=name,
        grid_spec=pltpu.PrefetchScalarGridSpec(
            num_scalar_prefetch=1, grid=(N_CHIPS, hr // rb),
            in_specs=[pl.BlockSpec((None, None, rb, cols), lambda j, i, c: (j, c[0], i, 0)),
                      pl.BlockSpec((None, rb, cols), lambda j, i, c: (j, i, 0))],
            out_specs=pl.BlockSpec((None, rb, cols), lambda j, i, c: (j, i, 0))),
        out_shape=jax.ShapeDtypeStruct((N_CHIPS, hr, cols), BF16),
        compiler_params=_params(("parallel", "parallel")),
    )(c_idx, g, rsib)


def _chip_sum(qb, *, name):
    _, hr, cols = qb.shape
    rb = _row_block(hr, cols, 2, 1 << 19)

    def body(q_ref, o_ref):
        acc = q_ref[0].astype(F32)
        for i in range(1, N_CHIPS):
            acc = acc + q_ref[i].astype(F32)
        o_ref[...] = acc

    return pl.pallas_call(
        body, name=name, grid=(hr // rb,),
        in_specs=[pl.BlockSpec((N_CHIPS, rb, cols), lambda i: (0, i, 0))],
        out_specs=pl.BlockSpec((rb, cols), lambda i: (i, 0)),
        out_shape=jax.ShapeDtypeStruct((hr, cols), F32), compiler_params=_params(("parallel",)),
    )(qb)


def _place():
    x, y, c = lax.axis_index("x"), lax.axis_index("y"), lax.axis_index("c")
    return x, y, c, [(1 - x, y), (x, 1 - y), (1 - x, 1 - y)]


ANY = pl.BlockSpec(memory_space=pl.ANY)


def _gather_side(shards):
    nw, n_k = len(shards), 8

    def plan(ins, outs, ssem, rsem):
        x, y, c, _ = _place()
        nbr_x, nbr_y, sib = (1 - x, y, c), (x, 1 - y, c), (x, y, 1 - c)
        s_me, s_x, s_y, s_d = 2 * x + y, 2 * (1 - x) + y, 2 * x + 1 - y, 2 * (1 - x) + 1 - y

        def rows(w, half, quarter=None):
            hr = shards[w].shape[0] // 2
            if quarter is None:
                return pl.ds(pl.multiple_of(half * hr, 16), hr)
            return pl.ds(pl.multiple_of(half * hr + quarter * (hr // 2), 16), hr // 2)

        def rcopy(w, k, src, slot, rws, to):
            return pltpu.make_async_remote_copy(
                src_ref=src, dst_ref=outs[w].at[slot, rws], send_sem=ssem.at[w * n_k + k],
                recv_sem=rsem.at[w * n_k + k], device_id=to, device_id_type=MESH)

        out = []
        for w in range(nw):
            mine = ins[w].at[rows(w, c)]
            q = [rows(w, c, 0), rows(w, c, 1)]
            p = functools.partial
            out.append([
                (p(rcopy, w, 0, mine, s_me, rows(w, c), nbr_x), p(rcopy, w, 0, mine, s_x, rows(w, c), nbr_x)),
                (p(rcopy, w, 1, mine, s_me, rows(w, c), nbr_y), p(rcopy, w, 1, mine, s_y, rows(w, c), nbr_y)),
                (p(rcopy, w, 2, outs[w].at[s_x, q[0]], s_x, q[0], nbr_y),
                 p(rcopy, w, 2, outs[w].at[s_x, q[0]], s_d, q[0], nbr_y)),
                (p(rcopy, w, 3, outs[w].at[s_y, q[1]], s_y, q[1], nbr_x),
                 p(rcopy, w, 3, outs[w].at[s_y, q[1]], s_d, q[1], nbr_x)),
                (p(rcopy, w, 4, outs[w].at[s_x, rows(w, c)], s_x, rows(w, c), sib),
                 p(rcopy, w, 4, mine, s_x, rows(w, 1 - c), sib)),
                (p(rcopy, w, 5, outs[w].at[s_y, rows(w, c)], s_y, rows(w, c), sib),
                 p(rcopy, w, 5, mine, s_y, rows(w, 1 - c), sib)),
                (p(rcopy, w, 6, outs[w].at[s_d, q[0]], s_d, q[0], sib),
                 p(rcopy, w, 6, outs[w].at[s_d, q[0]], s_d, rows(w, 1 - c, 0), sib)),
                (p(rcopy, w, 7, outs[w].at[s_d, q[1]], s_d, q[1], sib),
                 p(rcopy, w, 7, outs[w].at[s_d, q[1]], s_d, rows(w, 1 - c, 1), sib)),
            ])
        return out

    def send_own(*refs):
        for cps in plan(*refs):
            cps[0][0]().start()
            cps[1][0]().start()

    def relay(*refs):
        for cps in plan(*refs):
            cps[0][1]().wait_recv()
            cps[2][0]().start()
            cps[4][0]().start()
            cps[1][1]().wait_recv()
            cps[3][0]().start()
            cps[5][0]().start()

    def pass_diagonal(*refs):
        for cps in plan(*refs):
            cps[2][1]().wait_recv()
            cps[6][0]().start()
            cps[3][1]().wait_recv()
            cps[7][0]().start()

    def finish(*refs):
        for cps in plan(*refs):
            for k in (4, 5, 6, 7):
                cps[k][1]().wait_recv()
            for k in range(n_k):
                cps[k][0]().wait_send()

    return _Side(shards, [jax.ShapeDtypeStruct((N_CHIPS,) + s.shape, s.dtype) for s in shards], nw * n_k,
                 [send_own, relay, pass_diagonal, finish])


def _run_side(side, *, name):
    n_in, n_out = len(side.ins), len(side.out_shapes)

    def body(*refs):
        ins, outs = refs[:n_in], refs[n_in:n_in + n_out]
        for phase in side.phases:
            phase(ins, outs, refs[-2], refs[-1])

    return pl.pallas_call(
        body, name=name, in_specs=[ANY] * n_in, out_specs=[ANY] * n_out, out_shape=side.out_shapes,
        scratch_shapes=side.sems(), compiler_params=pltpu.CompilerParams(has_side_effects=True),
    )(*side.ins)


def _exchange_side(gs):
    nw = len(gs)

    def copies(ins, outs, ssem, rsem):
        x, y, c, _ = _place()
        return [pltpu.make_async_remote_copy(
            src_ref=ins[w].at[:, 1 - c], dst_ref=outs[w], send_sem=ssem.at[w], recv_sem=rsem.at[w],
            device_id=(x, y, 1 - c), device_id_type=MESH) for w in range(nw)]

    def start(*refs):
        for cp in copies(*refs):
            cp.start()

    def finish(*refs):
        for cp in copies(*refs):
            cp.wait()

    return _Side(gs, [jax.ShapeDtypeStruct((g.shape[0],) + g.shape[2:], g.dtype) for g in gs], nw, [start, finish])


def _scatter_side(ps):
    nw = len(ps)

    def copies(ins, outs, ssem, rsem):
        x, y, c, chips = _place()
        s_me = 2 * x + y
        return [pltpu.make_async_remote_copy(
            src_ref=ins[w].at[2 * chip[0] + chip[1]], dst_ref=outs[w].at[s_me],
            send_sem=ssem.at[w * 3 + j], recv_sem=rsem.at[w * 3 + j],
            device_id=(chip[0], chip[1], c), device_id_type=MESH)
            for w in range(nw) for j, chip in enumerate(chips)]

    def start(*refs):
        for cp in copies(*refs):
            cp.start()

    def finish(*refs):
        for cp in copies(*refs):
            cp.wait()

    return _Side(ps, [jax.ShapeDtypeStruct(p.shape, p.dtype) for p in ps], nw * 3, [start, finish])


def _pair_share(hs, *, name):
    nw = len(hs)

    def body(*refs):
        ins, outs = refs[:nw], refs[nw:2 * nw]
        ssem, rsem = refs[2 * nw:]
        x, y, c, _ = _place()
        cps = []
        for w in range(nw):
            cp = pltpu.make_async_remote_copy(
                src_ref=ins[w], dst_ref=outs[w], send_sem=ssem.at[w], recv_sem=rsem.at[w],
                device_id=(x, y, 1 - c), device_id_type=MESH)
            cp.start()
            cps.append(cp)
        for cp in cps:
            cp.wait()

    return pl.pallas_call(
        body, name=name, in_specs=[ANY] * nw, out_specs=[ANY] * nw,
        out_shape=[jax.ShapeDtypeStruct(h.shape, h.dtype) for h in hs],
        scratch_shapes=[pltpu.SemaphoreType.DMA((nw,)), pltpu.SemaphoreType.DMA((nw,))],
        compiler_params=pltpu.CompilerParams(has_side_effects=True),
    )(*hs)


def _gather_small(a, *, name):
    def body(a_ref, o_ref, ssem, rsem):
        x, y, c, chips = _place()
        s_me = 2 * x + y
        o_ref[s_me] = a_ref[...]
        cps = []
        for j, chip in enumerate(chips):
            cp = pltpu.make_async_remote_copy(
                src_ref=a_ref, dst_ref=o_ref.at[s_me], send_sem=ssem.at[j], recv_sem=rsem.at[j],
                device_id=(chip[0], chip[1], c), device_id_type=MESH)
            cp.start()
            cps.append(cp)
        for cp in cps:
            cp.wait()

    vm = pl.BlockSpec(memory_space=pltpu.VMEM)
    return pl.pallas_call(
        body, name=name, in_specs=[vm], out_specs=vm,
        out_shape=jax.ShapeDtypeStruct((N_CHIPS,) + a.shape, a.dtype),
        scratch_shapes=[pltpu.SemaphoreType.DMA((3,)), pltpu.SemaphoreType.DMA((3,))],
    )(a)


def _allreduce_small(p, *, name):
    def body(p_ref, o_ref, buf, ssem, rsem):
        x, y, c, _ = _place()
        me = 4 * x + 2 * y + c
        buf[me] = p_ref[...]
        cps = []
        for k in range(1, N_DEV):
            fx, fy, fc = (k >> 2) & 1, (k >> 1) & 1, k & 1
            peer = (x + fx - 2 * x * fx, y + fy - 2 * y * fy, c + fc - 2 * c * fc)
            cp = pltpu.make_async_remote_copy(
                src_ref=buf.at[me], dst_ref=buf.at[me], send_sem=ssem.at[k - 1], recv_sem=rsem.at[k - 1],
                device_id=peer, device_id_type=MESH)
            cp.start()
            cps.append(cp)
        for cp in cps:
            cp.wait()
        acc = buf[0]
        for i in range(1, N_DEV):
            acc = acc + buf[i]
        o_ref[...] = acc

    vm = pl.BlockSpec(memory_space=pltpu.VMEM)
    return pl.pallas_call(
        body, name=name, in_specs=[vm], out_specs=vm,
        out_shape=jax.ShapeDtypeStruct(p.shape, p.dtype),
        scratch_shapes=[pltpu.VMEM((N_DEV,) + p.shape, p.dtype), pltpu.SemaphoreType.DMA((N_DEV - 1,)),
                        pltpu.SemaphoreType.DMA((N_DEV - 1,))],
    )(p)


def _rows_to_tokens(r):
    h, n = r.shape[0], r.shape[1]
    return r.reshape(h, n * CHUNK).T


def _tokens_to_rows(a):
    t, h = a.shape
    return a.T.reshape(h, t // CHUNK, 1, CHUNK)


def kernel(x, norm1_w, w_in, conv_w, a_log, dt_bias, gdn_norm_w, ret_norm_w, w_out, norm2_w, w_gate, w_up, w_down, norm_f_w, loss_target, m_norm1_w, m_w_in, m_conv_w, m_a_log, m_dt_bias, m_gdn_norm_w, m_ret_norm_w, m_w_out, m_norm2_w, m_w_gate, m_w_up, m_w_down, m_norm_f_w, v_norm1_w, v_w_in, v_conv_w, v_a_log, v_dt_bias, v_gdn_norm_w, v_ret_norm_w, v_w_out, v_norm2_w, v_w_gate, v_w_up, v_w_down, v_norm_f_w):
    t, d = x.shape[1], x.shape[2]
    f = w_gate.shape[2] * N_CHIPS
    nqk, nv = HEADS * DK, HEADS * DV
    ncs = w_in.shape[2]
    c_idx = lax.axis_index("c")
    s_idx = 2 * lax.axis_index("x") + lax.axis_index("y")
    xs = x[0]
    tgt = loss_target[0]

    n_small = 2 * HEADS
    widths = [2 * nqk + nv, nv, n_small, nqk, nqk, nv, nv, d, d]
    g_off = np.concatenate([[0], np.cumsum(widths)])
    order = [0, 3, 4, 5, 1, 6, 7, 8]
    m_off = np.concatenate([[0], np.cumsum([widths[i] for i in order])])
    o_rq, o_rk, o_rv, o_az, o_rg, o_ga, o_gb = [int(m_off[i]) for i in range(1, 8)]
    segs = [(int(g_off[i]), int(g_off[i + 1]), int(m_off[order.index(i)]) if i != 2 else None) for i in range(9)]

    def shard_pieces(s):
        out = []
        for a, b, dst in segs:
            lo, hi = max(a, s * ncs), min(b, (s + 1) * ncs)
            if lo < hi:
                out.append((lo - s * ncs, hi - s * ncs, None if dst is None else dst + lo - a))
        return out

    own = [w_in[0].astype(BF16), w_out[0].astype(BF16), w_gate[0].astype(BF16), w_up[0].astype(BF16),
           w_down[0].astype(BF16)]
    put_own = lambda full, mine: lax.dynamic_update_slice(full, mine[None], (s_idx, 0, 0))
    wg_in = put_own(_run_side(_gather_side(own[:1]), name="gather_w_in")[0], own[0])
    conv_full = _gather_small(conv_w[0], name="gather_conv_w")
    conv_full = jnp.concatenate([conv_full[i] for i in range(N_CHIPS)], axis=1)
    cuts = [(dst, s, a, b) for s in range(N_CHIPS) for a, b, dst in shard_pieces(s)]
    w_main = jnp.concatenate([wg_in[s][:, a:b] for dst, s, a, b in sorted(c for c in cuts if c[0] is not None)],
                             axis=1)
    w_small = jnp.concatenate([wg_in[s][:, a:b] for dst, s, a, b in cuts if dst is None], axis=1)
    w_small = jnp.pad(w_small, ((0, 0), (0, LANES - n_small)))

    pad16 = lambda a: jnp.pad(a, ((0, 0), (HEADS, LANES - 2 * HEADS)))
    alog_row, dtb_row = pad16(a_log), pad16(dt_bias)
    wa_row = jnp.tile(gdn_norm_w, (1, HEADS))
    inv = ROPE_BASE ** (-jnp.arange(0, DK, 2, dtype=F32) / DK)
    ang = jnp.arange(t, dtype=F32)[:, None] * inv[None, :]
    cos2 = jnp.concatenate([jnp.cos(ang), jnp.cos(ang)], axis=1)
    sin2 = jnp.concatenate([-jnp.sin(ang), jnp.sin(ang)], axis=1)
    lg = jnp.log1p(-jnp.exp2(-5.0 - jnp.arange(HEADS, dtype=F32)))
    lg_tab = jnp.broadcast_to(lg[:, None, None], (HEADS, 1, LANES))

    fq = f // N_CHIPS
    u1, projs = _rms_fwd(xs, norm1_w, w_narrow=w_small, name="rms1_fwd")
    proj, *rest = _mm(u1, w_main, tm=512, tn=2048, tk=d, side=_gather_side(own[1:]), name="mm_proj")
    wg_out, wg_gate, wg_up, wg_down = [put_own(g, o) for g, o in zip(rest, own[1:])]
    w_o = wg_out.reshape(d, d)
    w_g, w_u = wg_gate, wg_up
    w_d = wg_down.reshape(f, d)
    q_a, k_a, v_a = _conv_fwd(proj, conv_full, name="conv_fwd")
    bg = _bg_fwd(projs, alog_row, dtb_row, name="bg_fwd")
    beta_r = _tokens_to_rows(bg[:, :HEADS])
    g_r = _tokens_to_rows(bg[:, HEADS:2 * HEADS])
    gc_r, tinv, u_a, w_a, p_a = _gdn_prep(q_a, k_a, v_a, beta_r, g_r, name="gdn_prep")
    o_a, s_a, o_b, s_b = _mix_fwd(q_a, k_a, u_a, w_a, p_a, gc_r, proj, cos2, sin2, lg_tab, o_rq, o_rk, o_rv,
                                  name="mix_fwd")
    mixed = _merge_fwd(o_a, o_b, proj, wa_row, ret_norm_w, o_az, o_rg, o_ga, o_gb, name="merge_fwd")
    h1 = _mm(mixed, w_o, res=xs, tm=512, tn=d, tk=d, name="mm_out")
    hn = _rms_fwd(h1, norm2_w, name="rms2_fwd")
    gt, up, act = _ffn_in(hn, w_g, w_u, tm=256, name="ffn_in")
    h2 = _mm(act, w_d, res=h1, tm=512, tn=1024, tk=f, name="mm_down")
    loss_row, dh2, dh2b, d_nf = _loss_head(h2, tgt, norm_f_w.reshape(1, d), name="loss_head")

    g_down = _mm(act, dh2b, ta=True, out_dtype=BF16, tm=fq, tn=d, tk=1024, name="mm_dw_down")
    dgt, dup = _ffn_back(dh2b, w_d, gt, up, tm=512, tn=fq, name="ffn_back")
    dhn = _mm(dgt, w_g, tb=True, pair=(dup, w_u), tm=512, tn=d, tk=fq, name="mm_dhn")
    g_gate = _mm(hn, dgt, ta=True, out_dtype=BF16, out_stacked=True, tm=1024, tn=fq, tk=2048, name="mm_dw_gate")
    g_up = _mm(hn, dup, ta=True, out_dtype=BF16, out_stacked=True, tm=1024, tn=fq, tk=2048, name="mm_dw_up")
    dh1, dh1b, d_n2 = _rms_bwd(dhn, h1, norm2_w, dh2, name="rms2_bwd")
    dmix = _mm(dh1b, w_o, tb=True, tm=512, tn=d, tk=d, name="mm_dmix")
    g_out = _mm(mixed, dh1b, ta=True, out_dtype=BF16, tm=1024, tn=d, tk=1024, name="mm_dw_out")
    halves = lambda g: g.reshape(N_CHIPS, 2, g.shape[1] // 2, g.shape[2])
    c_arr = jnp.reshape(c_idx, (1,)).astype(jnp.int32)
    gs_ffn = [halves(g_out.reshape(N_CHIPS, d // N_CHIPS, d)), halves(g_gate), halves(g_up),
              halves(g_down.reshape(N_CHIPS, fq, d))]
    do_a, do_b, dproj, d_wa, d_wb, *rsib = _merge_bwd(
        dmix, o_a, o_b, proj, wa_row, ret_norm_w, o_az, o_rg, o_ga, o_gb, side=_exchange_side(gs_ffn),
        name="merge_bwd")
    ps_ffn = [_pair_add(g, r, c_arr, name=f"grad_pair_add_{nm}")
              for g, r, nm in zip(gs_ffn, rsib, ["w_out", "w_gate", "w_up", "w_down"])]
    dq_a, dk_a, dv_a, dbeta_r, dg_r, dproj = _mix_bwd(
        q_a, k_a, v_a, beta_r, gc_r, tinv, u_a, w_a, s_a, do_a, proj, cos2, sin2, lg_tab, s_b, do_b, dproj,
        o_rq, o_rk, o_rv, name="mix_bwd")
    dc = _conv_bwd_pre(proj, conv_full, dq_a, dk_a, dv_a, name="conv_bwd_pre")
    dproj, d_cw = _conv_bwd(proj, dc, conv_full, dproj, name="conv_bwd")
    dbg = jnp.pad(jnp.concatenate([_rows_to_tokens(dbeta_r), _rows_to_tokens(dg_r)], axis=1),
                  ((0, 0), (0, LANES - 2 * HEADS)))
    dprojs, d_alog, d_dtb = _bg_bwd(projs, dbg, alog_row, dtb_row, name="bg_bwd")
    g_main, *qs_ffn = _mm(u1, dproj, ta=True, out_dtype=BF16, tm=1024, tn=2048, tk=2048,
                          side=_scatter_side(ps_ffn), name="mm_dw_in")
    g_small = _mm(u1, dprojs, ta=True, out_dtype=BF16, tm=1024, tn=LANES, tk=1024, name="mm_dw_in_small")
    pieces = []
    for s in range(N_CHIPS):
        seen, parts = 0, []
        for a, b, dst in shard_pieces(s):
            parts.append(g_small[:, seen:seen + b - a] if dst is None else g_main[:, dst:dst + b - a])
            seen += (b - a) if dst is None else 0
        pieces.append(jnp.concatenate(parts, axis=1))
    gs_in = [halves(jnp.stack(pieces))]
    rsib = _run_side(_exchange_side(gs_in), name="grad_pair_exchange_in")
    ps_in = [_pair_add(gs_in[0], rsib[0], c_arr, name="grad_pair_add_w_in")]
    du, *qs_in = _mm(dproj, w_main, tb=True, out_dtype=BF16, tm=1024, tn=d, tk=2048,
                     side=_scatter_side(ps_in), name="mm_du")
    dx, _, d_n1 = _rms_bwd(du, xs, norm1_w, dh1, narrow=(dprojs, w_small), name="rms1_bwd")

    names = ["w_in", "w_out", "w_gate", "w_up", "w_down"]
    qs = [lax.dynamic_update_slice(q, lax.dynamic_slice(p, (s_idx, 0, 0), (1,) + p.shape[1:]), (s_idx, 0, 0))
          for q, p in zip(qs_in + qs_ffn, ps_in + ps_ffn)]
    hs = [_chip_sum(q, name=f"grad_chip_sum_{nm}") for q, nm in zip(qs, names)]
    theirs = _pair_share(hs, name="grad_pair_share")
    big_w = [w_in[0], w_out[0], w_gate[0], w_up[0], w_down[0]]
    big_m = [m_w_in[0], m_w_out[0], m_w_gate[0], m_w_up[0], m_w_down[0]]
    big_v = [v_w_in[0], v_w_out[0], v_w_gate[0], v_w_up[0], v_w_down[0]]
    big = {}
    for nm, mine, other, w_, m_, v_ in zip(names, hs, theirs, big_w, big_m, big_v):
        big[nm] = tuple(a[None] for a in _adamw_halves(w_, mine, other, c_arr, m_, v_, name=f"adamw_{nm}"))

    d_wa_h = jnp.sum(d_wa.reshape(HEADS, DV), axis=0, keepdims=True)
    small = [d_n1, d_alog[:, HEADS:2 * HEADS], d_dtb[:, HEADS:2 * HEADS], d_wa_h, d_wb, d_n2, d_nf,
             d_cw[:CONV_W].reshape(1, -1)]
    sizes = [a.shape[1] for a in small]
    packed = jnp.concatenate(small, axis=1)
    n_pack = packed.shape[1]
    n_rows = -(-n_pack // LANES)
    n_rows = -(-n_rows // 8) * 8
    packed = jnp.pad(packed, ((0, 0), (0, n_rows * LANES - n_pack))).reshape(n_rows, LANES)
    red = _allreduce_small(packed, name="allreduce_small").reshape(1, -1)
    offs = np.cumsum([0] + sizes)
    g_n1, g_alog, g_dtb, g_wa, g_wb, g_n2, g_nf, g_cw = [red[:, offs[i]:offs[i + 1]] for i in range(len(sizes))]
    ncw = conv_w.shape[2]
    g_cw = lax.dynamic_slice(g_cw.reshape(CONV_W, -1), (0, s_idx * ncw), (CONV_W, ncw))

    def small_update(w_, g, m_, v_, nm):
        shape = w_.shape
        pad = (-w_.size) % LANES
        to2 = lambda a: jnp.pad(a.reshape(1, -1), ((0, 0), (0, pad)))
        outs = _adamw(to2(w_), to2(g), to2(m_), to2(v_), name=f"adamw_{nm}")
        return (g.reshape(shape),) + tuple(a[:, :w_.size].reshape(shape) for a in outs)

    res = {
        "norm1_w": small_update(norm1_w, g_n1, m_norm1_w, v_norm1_w, "norm1_w"),
        "w_in": big["w_in"],
        "conv_w": small_update(conv_w, g_cw, m_conv_w, v_conv_w, "conv_w"),
        "a_log": small_update(a_log, g_alog, m_a_log, v_a_log, "a_log"),
        "dt_bias": small_update(dt_bias, g_dtb, m_dt_bias, v_dt_bias, "dt_bias"),
        "gdn_norm_w": small_update(gdn_norm_w, g_wa, m_gdn_norm_w, v_gdn_norm_w, "gdn_norm_w"),
        "ret_norm_w": small_update(ret_norm_w, g_wb, m_ret_norm_w, v_ret_norm_w, "ret_norm_w"),
        "w_out": big["w_out"],
        "norm2_w": small_update(norm2_w, g_n2, m_norm2_w, v_norm2_w, "norm2_w"),
        "w_gate": big["w_gate"],
        "w_up": big["w_up"],
        "w_down": big["w_down"],
        "norm_f_w": small_update(norm_f_w, g_nf, m_norm_f_w, v_norm_f_w, "norm_f_w"),
    }
    order = ["norm1_w", "w_in", "conv_w", "a_log", "dt_bias", "gdn_norm_w", "ret_norm_w", "w_out", "norm2_w",
             "w_gate", "w_up", "w_down", "norm_f_w"]
    loss = lax.psum(loss_row[0, 0], ("x", "y", "c"))
    return (loss, dx[None], *[res[n][0] for n in order], *[res[n][1] for n in order],
            *[res[n][2] for n in order], *[res[n][3] for n in order])
```

```python
import functools

import jax
import jax.numpy as jnp
import numpy as np
from jax import lax
from jax.experimental import pallas as pl
from jax.experimental.pallas import tpu as pltpu

F32 = jnp.float32
BF16 = jnp.bfloat16
MESH = pl.DeviceIdType.MESH

HEADS = 8
DK = 128
DV = 256
CHUNK = 64
CONV_W = 4
EPS = 1e-6
ROPE_BASE = 10000.0
ADAM_LR, ADAM_B1, ADAM_B2, ADAM_EPS, ADAM_WD, ADAM_STEP = 0.001, 0.9, 0.999, 1e-08, 0.01, 10
N_CHIPS = 4
N_DEV = 8
LANES = 128
HALO = 8
VMEM_LIMIT = 56 * 1024 * 1024
HB = 8


def _pick(n, cands):
    for c in cands:
        if n % c == 0:
            return c
    raise ValueError(f"no tile for {n} in {cands}")


def _params(sem=None):
    return pltpu.CompilerParams(dimension_semantics=sem, vmem_limit_bytes=VMEM_LIMIT)


def _dot(a, b):
    return jnp.dot(a.astype(BF16), b.astype(BF16), preferred_element_type=F32)


def _dot_nt(a, b):
    return lax.dot_general(a.astype(BF16), b.astype(BF16), (((1,), (1,)), ((), ())), preferred_element_type=F32)


def _dot_tn(a, b):
    return lax.dot_general(a.astype(BF16), b.astype(BF16), (((0,), (0,)), ((), ())), preferred_element_type=F32)


def _sigmoid(x):
    return 1.0 / (1.0 + jnp.exp(-x))


def _iota2(n):
    return lax.broadcasted_iota(jnp.int32, (n, n), 0), lax.broadcasted_iota(jnp.int32, (n, n), 1)


def _row_to_col(row):
    n = row.shape[1]
    r, c = _iota2(n)
    return jnp.sum(jnp.where(r == c, jnp.broadcast_to(row, (n, n)), 0.0), axis=1, keepdims=True)


def _col_to_row(col):
    n = col.shape[0]
    r, c = _iota2(n)
    return jnp.sum(jnp.where(r == c, jnp.broadcast_to(col, (n, n)), 0.0), axis=0, keepdims=True)


class _Side:
    def __init__(self, ins, out_shapes, n_sem, phases):
        self.ins, self.out_shapes, self.n_sem, self.phases = list(ins), list(out_shapes), n_sem, phases

    def sems(self):
        return [pltpu.SemaphoreType.DMA((self.n_sem,)), pltpu.SemaphoreType.DMA((self.n_sem,))]


def _mm(a, b, *, name, tm, tn, tk, ta=False, tb=False, out_dtype=F32, res=None, side=None, pair=None,
        out_stacked=False):
    m, k = (a.shape[1], a.shape[0]) if ta else a.shape
    b_slots = b.ndim == 3
    if b_slots:
        assert tb and tk == b.shape[2]
        n = b.shape[1]
    else:
        n = b.shape[0] if tb else b.shape[1]
    tm, tn, tk = min(tm, m), min(tn, n), min(tk, k)
    assert m % tm == 0 and n % tn == 0 and k % tk == 0, (name, m, n, k)
    nk = k // tk
    nj, ni = n // tn, m // tm
    dn = (((0 if ta else 1,), (1 if tb else 0,)), ((), ()))
    n_ab = 4 if pair else 2
    n_in = n_ab + (res is not None)
    n_side_in = len(side.ins) if side else 0
    n_side_out = len(side.out_shapes) if side else 0

    def body(*refs):
        a_ref, b_ref = refs[0], refs[1]
        r_ref = refs[n_ab] if res is not None else None
        o_ref = refs[n_in + n_side_in]
        if side:
            s_in = refs[n_in:n_in + n_side_in]
            s_out = refs[n_in + n_side_in + 1:n_in + n_side_in + 1 + n_side_out]
            ssem, rsem = refs[-2], refs[-1]
            j_, i_, k_ = pl.program_id(0), pl.program_id(1), pl.program_id(2)
            n_mid = len(side.phases) - 2
            assert n_mid == 0 or nj >= 2 * n_mid
            when = [(j_ == 0) & (i_ == 0) & (k_ == 0)]
            when += [(j_ == nj // 2 + (p * (nj // 2)) // n_mid) & (i_ == 0) & (k_ == 0) for p in range(n_mid)]
            when.append((j_ == nj - 1) & (i_ == ni - 1) & (k_ == nk - 1))

        def run_phase(p):
            @pl.when(when[p])
            def _():
                side.phases[p](s_in, s_out, ssem, rsem)

        if side:
            for p in range(len(side.phases) - 1):
                run_phase(p)

        def finish(r):
            if res is not None:
                r = r + r_ref[...]
            o_ref[...] = r.astype(out_dtype)

        part = lax.dot_general(a_ref[...], b_ref[...], dn, preferred_element_type=F32)
        if pair:
            part = part + lax.dot_general(refs[2][...], refs[3][...], dn, preferred_element_type=F32)
        if nk == 1:
            finish(part)
        else:
            acc = refs[n_in + n_side_in + 1 + n_side_out]
            kk = pl.program_id(2)

            @pl.when(kk == 0)
            def _():
                acc[...] = part

            @pl.when((kk > 0) & (kk < nk - 1))
            def _():
                acc[...] += part

            @pl.when(kk == nk - 1)
            def _():
                finish(acc[...] + part)

        if side:
            run_phase(len(side.phases) - 1)

    a_spec = pl.BlockSpec((tk, tm), lambda j, i, kk: (kk, i)) if ta else pl.BlockSpec((tm, tk), lambda j, i, kk: (i, kk))
    b_spec = pl.BlockSpec((tn, tk), lambda j, i, kk: (j, kk)) if tb else pl.BlockSpec((tk, tn), lambda j, i, kk: (kk, j))
    if b_slots:
        b_spec = pl.BlockSpec((None, tn, tk), lambda j, i, kk: (kk, j, 0))
    o_spec = pl.BlockSpec((tm, tn), lambda j, i, kk: (i, j))
    o_shape = jax.ShapeDtypeStruct((m, n), out_dtype)
    if out_stacked:
        assert res is None
        o_spec = pl.BlockSpec((None, tm, tn), lambda j, i, kk: (j, i, 0))
        o_shape = jax.ShapeDtypeStruct((nj, m, tn), out_dtype)
    in_specs, args = [a_spec, b_spec], [a, b]
    if pair:
        assert pair[0].shape == a.shape and pair[1].shape == b.shape
        in_specs += [a_spec, b_spec]
        args += list(pair)
    if res is not None:
        in_specs.append(o_spec)
        args.append(res)
    out_specs, out_shape = o_spec, o_shape
    scratch = [pltpu.VMEM((tm, tn), F32)] if nk > 1 else []
    sem = ("parallel", "parallel", "arbitrary")
    if side:
        hbm = pl.BlockSpec(memory_space=pl.ANY)
        in_specs += [hbm] * n_side_in
        args += side.ins
        out_specs, out_shape = [o_spec] + [hbm] * n_side_out, [out_shape] + side.out_shapes
        scratch += side.sems()
        sem = ("arbitrary",) * 3
    return pl.pallas_call(
        body, name=name, grid=(nj, ni, nk), in_specs=in_specs, out_specs=out_specs, out_shape=out_shape,
        scratch_shapes=scratch, compiler_params=_params(sem),
    )(*args)


def _mm_res_norm(a, b, res, w, *, name, tm):
    m, k = a.shape
    n = b.shape[1]
    tm = min(tm, m)

    def body(a_ref, b_ref, r_ref, w_ref, h_ref, hn_ref):
        h = jnp.dot(a_ref[...], b_ref[...], preferred_element_type=F32) + r_ref[...]
        h_ref[...] = h
        r = lax.rsqrt(jnp.mean(h * h, axis=-1, keepdims=True) + EPS)
        hn_ref[...] = (h * r * w_ref[...]).astype(BF16)

    row = pl.BlockSpec((tm, n), lambda i: (i, 0))
    return pl.pallas_call(
        body, name=name, grid=(m // tm,),
        in_specs=[pl.BlockSpec((tm, k), lambda i: (i, 0)), pl.BlockSpec((k, n), lambda i: (0, 0)), row,
                  pl.BlockSpec((1, n), lambda i: (0, 0))],
        out_specs=[row, row],
        out_shape=[jax.ShapeDtypeStruct((m, n), F32), jax.ShapeDtypeStruct((m, n), BF16)],
        compiler_params=_params(("parallel",)),
    )(a, b, res, w)


def _rms_fwd(x, w, *, name, w_narrow=None):
    t, d = x.shape
    tt = _pick(t, (512, 256))

    def body(*refs):
        x_ref, w_ref, o_ref = refs[0], refs[1], refs[-2 if w_narrow is not None else -1]
        xv = x_ref[...]
        r = lax.rsqrt(jnp.mean(xv * xv, axis=-1, keepdims=True) + EPS)
        u = (xv * r * w_ref[...]).astype(BF16)
        o_ref[...] = u
        if w_narrow is not None:
            refs[-1][...] = jnp.dot(u, refs[2][...], preferred_element_type=F32)

    blk = pl.BlockSpec((tt, d), lambda i: (i, 0))
    in_specs, args = [blk, pl.BlockSpec((1, d), lambda i: (0, 0))], [x, w]
    out_specs, out_shape = [blk], [jax.ShapeDtypeStruct((t, d), BF16)]
    if w_narrow is not None:
        in_specs.append(pl.BlockSpec(w_narrow.shape, lambda i: (0, 0)))
        args.append(w_narrow)
        out_specs.append(pl.BlockSpec((tt, LANES), lambda i: (i, 0)))
        out_shape.append(jax.ShapeDtypeStruct((t, LANES), F32))
    out = pl.pallas_call(
        body, name=name, grid=(t // tt,), in_specs=in_specs, out_specs=out_specs, out_shape=out_shape,
        compiler_params=_params(("parallel",)),
    )(*args)
    return out if w_narrow is not None else out[0]


def _rms_bwd(dn, x, w, dres, *, name, narrow=None):
    t, d = x.shape
    tt = _pick(t, (256,))

    def body(*refs):
        dn_ref, x_ref, w_ref, dres_ref = refs[:4]
        dx_ref, dxb_ref, dw_ref = refs[-3:]
        xv, g = x_ref[...], dn_ref[...].astype(F32)
        if narrow is not None:
            g = g + lax.dot_general(refs[4][...], refs[5][...], (((1,), (1,)), ((), ())), preferred_element_type=F32)
        r = lax.rsqrt(jnp.mean(xv * xv, axis=-1, keepdims=True) + EPS)
        xh = xv * r
        gw = g * w_ref[...]
        dx = dres_ref[...] + r * (gw - xh * jnp.mean(gw * xh, axis=-1, keepdims=True))
        dx_ref[...] = dx
        dxb_ref[...] = dx.astype(BF16)

        @pl.when(pl.program_id(0) == 0)
        def _():
            dw_ref[...] = jnp.zeros_like(dw_ref)

        dw_ref[...] += jnp.sum(g * xh, axis=0, keepdims=True)

    blk = pl.BlockSpec((tt, d), lambda i: (i, 0))
    row = pl.BlockSpec((1, d), lambda i: (0, 0))
    in_specs, args = [blk, blk, row, blk], [dn, x, w, dres]
    if narrow is not None:
        in_specs += [pl.BlockSpec((tt, LANES), lambda i: (i, 0)), pl.BlockSpec(narrow[1].shape, lambda i: (0, 0))]
        args += list(narrow)
    return pl.pallas_call(
        body, name=name, grid=(t // tt,), in_specs=in_specs, out_specs=[blk, blk, row],
        out_shape=[jax.ShapeDtypeStruct((t, d), F32), jax.ShapeDtypeStruct((t, d), BF16),
                   jax.ShapeDtypeStruct((1, d), F32)],
        compiler_params=_params(("arbitrary",)),
    )(*args)


def _loss_head(h2, tgt, wf, *, name):
    t, d = h2.shape
    tt = _pick(t, (256,))

    def body(x_ref, t_ref, w_ref, loss_ref, dx_ref, dxb_ref, dw_ref):
        xv = x_ref[...]
        r = lax.rsqrt(jnp.mean(xv * xv, axis=-1, keepdims=True) + EPS)
        xh = xv * r
        err = xh * w_ref[...] - t_ref[...]
        lpart = 0.5 * jnp.sum(jnp.mean(err * err, axis=-1, keepdims=True), axis=0, keepdims=True)
        dy = err * (1.0 / d)
        gw = dy * w_ref[...]
        dx = r * (gw - xh * jnp.mean(gw * xh, axis=-1, keepdims=True))
        dx_ref[...] = dx
        dxb_ref[...] = dx.astype(BF16)

        @pl.when(pl.program_id(0) == 0)
        def _():
            dw_ref[...] = jnp.zeros_like(dw_ref)
            loss_ref[...] = jnp.zeros_like(loss_ref)

        dw_ref[...] += jnp.sum(dy * xh, axis=0, keepdims=True)
        loss_ref[...] += jnp.broadcast_to(lpart, loss_ref.shape)

    blk = pl.BlockSpec((tt, d), lambda i: (i, 0))
    row = pl.BlockSpec((1, d), lambda i: (0, 0))
    lrow = pl.BlockSpec((1, LANES), lambda i: (0, 0))
    return pl.pallas_call(
        body, name=name, grid=(t // tt,),
        in_specs=[blk, blk, row], out_specs=[lrow, blk, blk, row],
        out_shape=[jax.ShapeDtypeStruct((1, LANES), F32), jax.ShapeDtypeStruct((t, d), F32),
                   jax.ShapeDtypeStruct((t, d), BF16), jax.ShapeDtypeStruct((1, d), F32)],
        compiler_params=_params(("arbitrary",)),
    )(h2, tgt, wf)


def _ffn_in(hn, w_g, w_u, *, name, tm):
    t, d = hn.shape
    tn = w_g.shape[2]
    f = w_g.shape[0] * tn
    tm = min(tm, t)

    def body(a_ref, g_ref, u_ref, gt_ref, up_ref, act_ref):
        a = a_ref[...]
        g = jnp.dot(a, g_ref[...], preferred_element_type=F32)
        u = jnp.dot(a, u_ref[...], preferred_element_type=F32)
        gt_ref[...] = g.astype(BF16)
        up_ref[...] = u.astype(BF16)
        act_ref[...] = (g * _sigmoid(g) * u).astype(BF16)

    wblk = pl.BlockSpec((None, d, tn), lambda j, i: (j, 0, 0))
    oblk = pl.BlockSpec((tm, tn), lambda j, i: (i, j))
    return pl.pallas_call(
        body, name=name, grid=(f // tn, t // tm),
        in_specs=[pl.BlockSpec((tm, d), lambda j, i: (i, 0)), wblk, wblk], out_specs=[oblk] * 3,
        out_shape=[jax.ShapeDtypeStruct((t, f), BF16)] * 3,
        compiler_params=_params(("parallel", "parallel")),
    )(hn, w_g, w_u)


def _ffn_back(dh2b, w_d, gt, up, *, name, tm, tn):
    t, d = dh2b.shape
    f = w_d.shape[0]
    tm = min(tm, t)

    n_sub = 2 if tm % 32 == 0 else 1

    def body(a_ref, w_ref, g_ref, u_ref, dg_ref, du_ref):
        w = w_ref[...]
        subs = [slice(i * (tm // n_sub), (i + 1) * (tm // n_sub)) for i in range(n_sub)]
        das = [lax.dot_general(a_ref[rs, :], w, (((1,), (1,)), ((), ())), preferred_element_type=F32) for rs in subs]
        for rs, da in zip(subs, das):
            g = g_ref[rs, :].astype(F32)
            sg = _sigmoid(g)
            dg_ref[rs, :] = (da * u_ref[rs, :].astype(F32) * sg * (1.0 + g * (1.0 - sg))).astype(BF16)
            du_ref[rs, :] = (da * g * sg).astype(BF16)

    oblk = pl.BlockSpec((tm, tn), lambda j, i: (i, j))
    return pl.pallas_call(
        body, name=name, grid=(f // tn, t // tm),
        in_specs=[pl.BlockSpec((tm, d), lambda j, i: (i, 0)), pl.BlockSpec((tn, d), lambda j, i: (j, 0)), oblk, oblk],
        out_specs=[oblk] * 2,
        out_shape=[jax.ShapeDtypeStruct((t, f), BF16)] * 2,
        compiler_params=_params(("parallel", "parallel")),
    )(dh2b, w_d, gt, up)


CONV_HEAD = 2 * HALO


def _causal_conv(x_ref, xp_ref, cw_ref, xs, tt):
    first = pl.program_id(0) == 0
    xs[0:HALO, :] = jnp.where(first, 0.0, xp_ref[...])
    xs[HALO:HALO + CONV_HEAD, :] = x_ref[0:CONV_HEAD, :]
    xb = x_ref[...].astype(BF16)
    r, c = _iota2(tt)
    moved = [None] + [jnp.dot((c == r - sh).astype(BF16), xb, preferred_element_type=F32)
                      for sh in range(1, CONV_W)]

    def cols(cb):
        cs = slice(cb * LANES, (cb + 1) * LANES)
        c_head = c_main = None
        for j in range(CONV_W):
            sh = CONV_W - 1 - j
            w_j = cw_ref[j:j + 1, cs]
            x_m = x_ref[CONV_HEAD:tt, cs] if sh == 0 else moved[sh][CONV_HEAD:tt, cs]
            x_h = xs[HALO - sh:HALO - sh + CONV_HEAD, cs]
            c_main = x_m * w_j if c_main is None else c_main + x_m * w_j
            c_head = x_h * w_j if c_head is None else c_head + x_h * w_j
        return jnp.concatenate([c_head, c_main], axis=0)

    return cols


def _conv_specs(tt, ch):
    cur = pl.BlockSpec((tt, ch), lambda i: (i, 0))
    prev = pl.BlockSpec((HALO, ch), lambda i: (jnp.maximum(i * (tt // HALO) - 1, 0), 0))
    return cur, prev


def _conv_fwd(proj, conv_w, *, name):
    t = proj.shape[0]
    ch = conv_w.shape[1]
    nqk = HEADS * DK
    tt = _pick(t, (256,))
    cur, prev = _conv_specs(tt, ch)

    def body(x_ref, xp_ref, cw_ref, q_ref, k_ref, v_ref, xs):
        conv = _causal_conv(x_ref, xp_ref, cw_ref, xs, tt)
        for cb in range(ch // LANES):
            c = conv(cb)
            s = c * _sigmoid(c)
            if cb < 2 * HEADS:
                s = s * lax.rsqrt(jnp.sum(s * s, axis=-1, keepdims=True) + EPS)
                if cb < HEADS:
                    q_ref[:, cb * LANES:(cb + 1) * LANES] = s * (DK ** -0.5)
                else:
                    k_ref[:, (cb - HEADS) * LANES:(cb - HEADS + 1) * LANES] = s
            else:
                v_ref[:, (cb - 2 * HEADS) * LANES:(cb - 2 * HEADS + 1) * LANES] = s

    return pl.pallas_call(
        body, name=name, grid=(t // tt,),
        in_specs=[cur, prev, pl.BlockSpec((CONV_W, ch), lambda i: (0, 0))],
        out_specs=[pl.BlockSpec((tt, nqk), lambda i: (i, 0)), pl.BlockSpec((tt, nqk), lambda i: (i, 0)),
                   pl.BlockSpec((tt, ch - 2 * nqk), lambda i: (i, 0))],
        out_shape=[jax.ShapeDtypeStruct((t, nqk), F32), jax.ShapeDtypeStruct((t, nqk), F32),
                   jax.ShapeDtypeStruct((t, ch - 2 * nqk), F32)],
        scratch_shapes=[pltpu.VMEM((HALO + CONV_HEAD, ch), F32)],
        compiler_params=_params(("arbitrary",)),
    )(proj, proj, conv_w)


def _conv_bwd_pre(proj, conv_w, dq, dk, dv, *, name):
    t = proj.shape[0]
    ch = conv_w.shape[1]
    nqk = HEADS * DK
    tt = _pick(t, (256,))
    cur, prev = _conv_specs(tt, ch)

    def body(x_ref, xp_ref, cw_ref, dq_ref, dk_ref, dv_ref, dc_ref, xs):
        conv = _causal_conv(x_ref, xp_ref, cw_ref, xs, tt)
        for cb in range(ch // LANES):
            c = conv(cb)
            sg = _sigmoid(c)
            s = c * sg
            if cb < 2 * HEADS:
                if cb < HEADS:
                    d, scale = dq_ref[:, cb * LANES:(cb + 1) * LANES], DK ** -0.5
                else:
                    d, scale = dk_ref[:, (cb - HEADS) * LANES:(cb - HEADS + 1) * LANES], 1.0
                rinv = lax.rsqrt(jnp.sum(s * s, axis=-1, keepdims=True) + EPS)
                ds = scale * rinv * (d - s * (rinv * rinv) * jnp.sum(d * s, axis=-1, keepdims=True))
            else:
                ds = dv_ref[:, (cb - 2 * HEADS) * LANES:(cb - 2 * HEADS + 1) * LANES]
            dc_ref[:, cb * LANES:(cb + 1) * LANES] = ds * sg * (1.0 + c * (1.0 - sg))

    return pl.pallas_call(
        body, name=name, grid=(t // tt,),
        in_specs=[cur, prev, pl.BlockSpec((CONV_W, ch), lambda i: (0, 0)),
                  pl.BlockSpec((tt, nqk), lambda i: (i, 0)), pl.BlockSpec((tt, nqk), lambda i: (i, 0)),
                  pl.BlockSpec((tt, ch - 2 * nqk), lambda i: (i, 0))],
        out_specs=pl.BlockSpec((tt, ch), lambda i: (i, 0)),
        out_shape=jax.ShapeDtypeStruct((t, ch), F32),
        scratch_shapes=[pltpu.VMEM((HALO + CONV_HEAD, ch), F32)],
        compiler_params=_params(("arbitrary",)),
    )(proj, proj, conv_w, dq, dk, dv)


def _conv_bwd(proj, dc, conv_w, dproj, *, name):
    t = proj.shape[0]
    ch = conv_w.shape[1]
    tt = _pick(t, (256,))
    nt = t // tt
    cur = pl.BlockSpec((tt, ch), lambda i: (i, 0))
    nxt = pl.BlockSpec((HALO, ch), lambda i: (jnp.minimum((i + 1) * (tt // HALO), t // HALO - 1), 0))

    tail = 2 * HALO
    main = tt - tail

    def body(x_ref, d_ref, dn_ref, cw_ref, alias_ref, dx_ref, dw_ref, ds):
        last = pl.program_id(0) == nt - 1
        ds[0:tail, :] = d_ref[main:tt, :]
        ds[tail:tail + HALO, :] = jnp.where(last, 0.0, dn_ref[...])

        @pl.when(pl.program_id(0) == 0)
        def _():
            dw_ref[...] = jnp.zeros_like(dw_ref)

        dcb = d_ref[...].astype(BF16)
        r, c = _iota2(tt)
        moved = [None] + [jnp.dot((c == r + sh).astype(BF16), dcb, preferred_element_type=F32)
                          for sh in range(1, CONV_W)]
        for cb in range(ch // LANES):
            cols = slice(cb * LANES, (cb + 1) * LANES)
            x_m, x_t = x_ref[0:main, cols], x_ref[main:tt, cols]
            acc_m = acc_t = None
            for j in range(CONV_W):
                sh = CONV_W - 1 - j
                w_j = cw_ref[j:j + 1, cols]
                dj_m = d_ref[0:main, cols] if sh == 0 else moved[sh][0:main, cols]
                dj_t = ds[sh:sh + tail, cols]
                acc_m = dj_m * w_j if acc_m is None else acc_m + dj_m * w_j
                acc_t = dj_t * w_j if acc_t is None else acc_t + dj_t * w_j
                dw_ref[j:j + 1, cols] += (jnp.sum(dj_m * x_m, axis=0, keepdims=True)
                                          + jnp.sum(dj_t * x_t, axis=0, keepdims=True))
            dx_ref[0:main, cols] = acc_m.astype(BF16)
            dx_ref[main:tt, cols] = acc_t.astype(BF16)

    return pl.pallas_call(
        body, name=name, grid=(nt,),
        in_specs=[cur, cur, nxt, pl.BlockSpec((CONV_W, ch), lambda i: (0, 0)), pl.BlockSpec(memory_space=pl.ANY)],
        out_specs=[pl.BlockSpec((tt, ch), lambda i: (i, 0)), pl.BlockSpec((HALO, ch), lambda i: (0, 0))],
        out_shape=[jax.ShapeDtypeStruct(dproj.shape, BF16), jax.ShapeDtypeStruct((HALO, ch), F32)],
        scratch_shapes=[pltpu.VMEM((tail + HALO, ch), F32)],
        input_output_aliases={4: 0},
        compiler_params=_params(("arbitrary",)),
    )(proj, dc, dc, conv_w, dproj)


def _bg_fwd(projs, alog_row, dtb_row, *, name):
    t = projs.shape[0]
    tt = _pick(t, (512, 256))

    def body(p_ref, al_ref, db_ref, o_ref):
        p = p_ref[...]
        lane = lax.broadcasted_iota(jnp.int32, p.shape, 1)
        z = p + db_ref[...]
        sp = jnp.maximum(z, 0.0) + jnp.log1p(jnp.exp(-jnp.abs(z)))
        g = -jnp.exp(al_ref[...]) * sp
        o_ref[...] = jnp.where(lane < HEADS, _sigmoid(p), jnp.where(lane < 2 * HEADS, g, 0.0))

    blk = pl.BlockSpec((tt, LANES), lambda i: (i, 0))
    row = pl.BlockSpec((1, LANES), lambda i: (0, 0))
    return pl.pallas_call(
        body, name=name, grid=(t // tt,), in_specs=[blk, row, row], out_specs=blk,
        out_shape=jax.ShapeDtypeStruct((t, LANES), F32), compiler_params=_params(("parallel",)),
    )(projs, alog_row, dtb_row)


def _bg_bwd(projs, dbg, alog_row, dtb_row, *, name):
    t = projs.shape[0]
    tt = _pick(t, (512, 256))

    def body(p_ref, d_ref, al_ref, db_ref, o_ref, dal_ref, ddb_ref):
        p, d = p_ref[...], d_ref[...]
        lane = lax.broadcasted_iota(jnp.int32, p.shape, 1)
        isg = (lane >= HEADS) & (lane < 2 * HEADS)
        be = _sigmoid(p)
        z = p + db_ref[...]
        sp = jnp.maximum(z, 0.0) + jnp.log1p(jnp.exp(-jnp.abs(z)))
        ea = jnp.exp(al_ref[...])
        d_aa = jnp.where(isg, d * (-ea) * _sigmoid(z), 0.0)
        o_ref[...] = jnp.where(lane < HEADS, d * be * (1.0 - be), d_aa).astype(BF16)

        @pl.when(pl.program_id(0) == 0)
        def _():
            dal_ref[...] = jnp.zeros_like(dal_ref)
            ddb_ref[...] = jnp.zeros_like(ddb_ref)

        dal_ref[...] += jnp.sum(jnp.where(isg, d * (-ea) * sp, 0.0), axis=0, keepdims=True)
        ddb_ref[...] += jnp.sum(d_aa, axis=0, keepdims=True)

    blk = pl.BlockSpec((tt, LANES), lambda i: (i, 0))
    row = pl.BlockSpec((1, LANES), lambda i: (0, 0))
    return pl.pallas_call(
        body, name=name, grid=(t // tt,), in_specs=[blk, blk, row, row], out_specs=[blk, row, row],
        out_shape=[jax.ShapeDtypeStruct((t, LANES), BF16), jax.ShapeDtypeStruct((1, LANES), F32),
                   jax.ShapeDtypeStruct((1, LANES), F32)],
        compiler_params=_params(("arbitrary",)),
    )(projs, dbg, alog_row, dtb_row)


def _hn_specs(n_chunks, rev=False):
    def nn(n):
        return n_chunks - 1 - n if rev else n
    tok = lambda w: pl.BlockSpec((CHUNK, HB * w), lambda g, n: (nn(n), g))
    per = lambda a, b: pl.BlockSpec((HB, None, a, b), lambda g, n: (g, nn(n), 0, 0))
    return tok, per


def _hcols(hh, w):
    return slice(hh * w, (hh + 1) * w)


def _decay(gc_col, gc_row):
    r, c = _iota2(CHUNK)
    d = jnp.exp(jnp.minimum(gc_col - gc_row, 0.0))
    return jnp.where(r >= c, d, 0.0), jnp.where(r > c, d, 0.0)


def _gdn_prep(q, k, v, beta_r, g_r, *, name):
    t = q.shape[0]
    n_chunks = t // CHUNK
    cps = 2 if n_chunks % 2 == 0 else 1
    tok = lambda w: pl.BlockSpec((cps * CHUNK, HB * w), lambda g, n: (n, g))
    per = lambda a, b: pl.BlockSpec((HB, cps, a, b), lambda g, n: (g, n, 0, 0))

    def body(q_ref, k_ref, v_ref, b_ref, g_ref, gc_ref, ti_ref, u_ref, w_ref, p_ref):
        r, c = _iota2(CHUNK)
        its = [(h, cc) for cc in range(cps) for h in range(HB)]
        rows = lambda cc: slice(cc * CHUNK, (cc + 1) * CHUNK)
        n = range(len(its))
        kk = [k_ref[rows(cc), _hcols(h, DK)] for h, cc in its]
        gc_col = [jnp.sum(jnp.where(c <= r, jnp.broadcast_to(g_ref[h, cc], (CHUNK, CHUNK)), 0.0), axis=1,
                          keepdims=True) for h, cc in its]
        gc_row = [_col_to_row(gc_col[i]) for i in n]
        beta_col = [_row_to_col(b_ref[h, cc]) for h, cc in its]
        dec = [_decay(gc_col[i], gc_row[i]) for i in n]
        kb = [kk[i] * beta_col[i] for i in n]
        for i, (h, cc) in enumerate(its):
            gc_ref[h, cc] = gc_row[i]
            p_ref[h, cc] = (_dot_nt(q_ref[rows(cc), _hcols(h, DK)], kk[i]) * dec[i][0]).astype(BF16)
        pw = [_dot_nt(kb[i], kk[i]) * dec[i][1] for i in n]
        y = [-pw[i] for i in n]
        for _ in range(5):
            pw = [_dot(pw[i], pw[i]) for i in n]
            yp = [_dot(y[i], pw[i]) for i in n]
            y = [y[i] + pw[i] + yp[i] for i in n]
        vb = [v_ref[rows(cc), _hcols(h, DV)] * beta_col[i] for i, (h, cc) in enumerate(its)]
        kbe = [kb[i] * jnp.exp(gc_col[i]) for i in n]
        yv = [_dot(y[i], vb[i]) for i in n]
        yk = [_dot(y[i], kbe[i]) for i in n]
        for i, (h, cc) in enumerate(its):
            ti_ref[h, cc] = y[i].astype(BF16)
            u_ref[rows(cc), _hcols(h, DV)] = vb[i] + yv[i]
            w_ref[rows(cc), _hcols(h, DK)] = (kbe[i] + yk[i]).astype(BF16)

    return pl.pallas_call(
        body, name=name, grid=(HEADS // HB, n_chunks // cps),
        in_specs=[tok(DK), tok(DK), tok(DV), per(1, CHUNK), per(1, CHUNK)],
        out_specs=[per(1, CHUNK), per(CHUNK, CHUNK), tok(DV), tok(DK), per(CHUNK, CHUNK)],
        out_shape=[jax.ShapeDtypeStruct((HEADS, n_chunks, 1, CHUNK), F32),
                   jax.ShapeDtypeStruct((HEADS, n_chunks, CHUNK, CHUNK), BF16),
                   jax.ShapeDtypeStruct((t, HEADS * DV), F32), jax.ShapeDtypeStruct((t, HEADS * DK), BF16),
                   jax.ShapeDtypeStruct((HEADS, n_chunks, CHUNK, CHUNK), BF16)],
        compiler_params=_params(("parallel", "parallel")),
    )(q, k, v, beta_r, g_r)


def _interleave(*step_gens):
    live = list(step_gens)
    while live:
        for g in list(live):
            try:
                next(g)
            except StopIteration:
                live.remove(g)


def _gdn_scan_steps(q_ref, k_ref, u_ref, w_ref, p_ref, gc_ref, o_ref, s_ref, st):
    hs = range(HB)
    s = [st[h] for h in hs]
    gc_row = [gc_ref[h] for h in hs]
    gc_col = [_row_to_col(gc_row[h]) for h in hs]
    glast = [gc_row[h][:, CHUNK - 1:CHUNK] for h in hs]
    for h in hs:
        s_ref[h] = s[h].astype(BF16)
    yield
    ws = [_dot(w_ref[:, _hcols(h, DK)], s[h]) for h in hs]
    yield
    qs = [_dot(q_ref[:, _hcols(h, DK)] * jnp.exp(gc_col[h]), s[h]) for h in hs]
    vn = [u_ref[:, _hcols(h, DV)] - ws[h] for h in hs]
    yield
    pv = [_dot(p_ref[h], vn[h]) for h in hs]
    yield
    kv = [_dot_tn(k_ref[:, _hcols(h, DK)] * jnp.exp(glast[h] - gc_col[h]), vn[h]) for h in hs]
    yield
    for h in hs:
        o_ref[:, _hcols(h, DV)] = qs[h] + pv[h]
        st[h] = s[h] * jnp.exp(glast[h]) + kv[h]


def _gdn_bwd_steps(q_ref, k_ref, v_ref, b_ref, gc_ref, ti_ref, u_ref, w_ref, s_ref, do_ref,
                   dq_ref, dk_ref, dv_ref, db_ref, dg_ref, dst):
    r, c = _iota2(CHUNK)
    rows = lax.broadcasted_iota(jnp.int32, (CHUNK, 1), 0)
    rsum = lambda a: jnp.sum(a, axis=1, keepdims=True)
    hs = range(HB)
    qq = [q_ref[:, _hcols(h, DK)] for h in hs]
    kk = [k_ref[:, _hcols(h, DK)] for h in hs]
    ww = [w_ref[:, _hcols(h, DK)] for h in hs]
    uu = [u_ref[:, _hcols(h, DV)] for h in hs]
    d_o = [do_ref[:, _hcols(h, DV)] for h in hs]
    s = [s_ref[h].astype(F32) for h in hs]
    d_s = [dst[h] for h in hs]
    gc_row = [gc_ref[h] for h in hs]
    gc_col = [_row_to_col(gc_row[h]) for h in hs]
    beta_col = [_row_to_col(b_ref[h]) for h in hs]
    eg = [jnp.exp(gc_col[h]) for h in hs]
    glast = [gc_row[h][:, CHUNK - 1:CHUNK] for h in hs]
    eglast = [jnp.exp(glast[h]) for h in hs]
    e2 = [jnp.exp(glast[h] - gc_col[h]) for h in hs]
    dec = [_decay(gc_col[h], gc_row[h]) for h in hs]
    kb = [kk[h] * beta_col[h] for h in hs]
    ke = [kk[h] * e2[h] for h in hs]
    qe = [qq[h] * eg[h] for h in hs]
    yield
    kkt = [_dot_nt(kb[h], kk[h]) for h in hs]
    qk = [_dot_nt(qq[h], kk[h]) for h in hs]
    ws = [_dot(ww[h], s[h]) for h in hs]
    yield
    a = [kkt[h] * dec[h][1] for h in hs]
    pp = [qk[h] * dec[h][0] for h in hs]
    vn = [uu[h] - ws[h] for h in hs]
    t1 = [_dot_tn(pp[h], d_o[h]) for h in hs]
    t2 = [_dot(ke[h], d_s[h]) for h in hs]
    yield
    dqe = [_dot_nt(d_o[h], s[h]) for h in hs]
    dke = [_dot_nt(vn[h], d_s[h]) for h in hs]
    dqk = [_dot_nt(d_o[h], vn[h]) * dec[h][0] for h in hs]
    yield
    dvn = [t1[h] + t2[h] for h in hs]
    t3 = [_dot_tn(qe[h], d_o[h]) for h in hs]
    t4 = [_dot_tn(ww[h], dvn[h]) for h in hs]
    dw = [-_dot_nt(dvn[h], s[h]) for h in hs]
    yield
    de2 = [rsum(dke[h] * ke[h]) for h in hs]
    dglast = [jnp.sum(de2[h], axis=0, keepdims=True)
              + eglast[h] * jnp.sum(rsum(d_s[h] * s[h]), axis=0, keepdims=True) for h in hs]
    for h in hs:
        dst[h] = d_s[h] * eglast[h] + t3[h] - t4[h]
    yy = [ti_ref[h] for h in hs]
    t5 = [_dot_tn(yy[h], dvn[h]) for h in hs]
    t6 = [_dot_tn(yy[h], dw[h]) for h in hs]
    yield
    dvb = [dvn[h] + t5[h] for h in hs]
    dkbe = [dw[h] + t6[h] for h in hs]
    t7 = [_dot_nt(dvb[h], uu[h]) for h in hs]
    t8 = [_dot_nt(dkbe[h], ww[h]) for h in hs]
    yield
    d_a = [jnp.where(r > c, -(t7[h] + t8[h]), 0.0) for h in hs]
    dm = [d_a[h] * dec[h][1] for h in hs]
    t9 = [_dot(dm[h], kk[h]) for h in hs]
    t10 = [_dot(dqk[h], kk[h]) for h in hs]
    yield
    t11 = [_dot_tn(dqk[h], qq[h]) for h in hs]
    t12 = [_dot_tn(dm[h], kb[h]) for h in hs]
    yield
    for h in hs:
        dkb = t9[h] + dkbe[h] * eg[h]
        e_mat = d_a[h] * a[h] + dqk[h] * qk[h]
        dgc = (rsum(dqe[h] * qe[h]) - de2[h] + rsum(dkbe[h] * kb[h] * eg[h]) + rsum(e_mat)
               - _row_to_col(jnp.sum(e_mat, axis=0, keepdims=True)))
        dgc = dgc + jnp.where(rows == CHUNK - 1, dglast[h], 0.0)
        dq_ref[:, _hcols(h, DK)] = dqe[h] * eg[h] + t10[h]
        dk_ref[:, _hcols(h, DK)] = t11[h] + dke[h] * e2[h] + t12[h] + dkb * beta_col[h]
        dv_ref[:, _hcols(h, DV)] = dvb[h] * beta_col[h]
        dbeta = rsum(dkb * kk[h]) + rsum(dvb[h] * v_ref[:, _hcols(h, DV)])
        db_ref[h] = _col_to_row(dbeta)
        dg_ref[h] = jnp.sum(jnp.where(r >= c, jnp.broadcast_to(dgc, (CHUNK, CHUNK)), 0.0), axis=0, keepdims=True)


def _rot(x, cos2, sin2):
    return x * cos2 + pltpu.roll(x, DK // 2, 1) * sin2


def _unrot(d, cos2, sin2):
    return d * cos2 + pltpu.roll(d * sin2, DK // 2, 1)


def _ret_consts(lg):
    r, c = _iota2(CHUNK)
    dm = jnp.where(r >= c, jnp.exp((r - c).astype(F32) * lg), 0.0)
    pos = lax.broadcasted_iota(jnp.int32, (CHUNK, 1), 0).astype(F32)
    return dm, jnp.exp((pos + 1.0) * lg), jnp.exp((CHUNK - 1.0 - pos) * lg), jnp.exp(CHUNK * lg)


def _ret_fwd_steps(q_ref, k_ref, v_ref, cos_ref, sin_ref, lg_ref, o_ref, s_ref, st):
    cos2v, sin2v = cos_ref[...], sin_ref[...]
    hs = range(HB)
    s = [st[h] for h in hs]
    for h in hs:
        s_ref[h] = s[h].astype(BF16)
    cst = [_ret_consts(lg_ref[h][:, 0:1]) for h in hs]
    qq = [_rot(q_ref[:, _hcols(h, DK)], cos2v, sin2v) for h in hs]
    kk = [_rot(k_ref[:, _hcols(h, DK)], cos2v, sin2v) * (DK ** -0.5) for h in hs]
    vv = [v_ref[:, _hcols(h, DV)] for h in hs]
    yield
    sc = [_dot_nt(qq[h], kk[h]) * cst[h][0] for h in hs]
    yield
    qs = [_dot(qq[h], s[h]) for h in hs]
    yield
    kv = [_dot_tn(kk[h] * cst[h][2], vv[h]) for h in hs]
    yield
    sv = [_dot(sc[h], vv[h]) for h in hs]
    yield
    for h in hs:
        o_ref[:, _hcols(h, DV)] = sv[h] + qs[h] * cst[h][1]
        st[h] = s[h] * cst[h][3] + kv[h]


def _mix_fwd(q, k, u, w, p, gc_r, proj, cos2, sin2, lg_tab, col_q, col_k, col_v, *, name):
    t = q.shape[0]
    n_chunks = t // CHUNK
    tok, per = _hn_specs(n_chunks)
    bq, bk, bv = col_q // (HB * DK), col_k // (HB * DK), col_v // (HB * DV)

    def body(*refs):
        st_a, st_b = refs[-2:]

        @pl.when(pl.program_id(1) == 0)
        def _():
            st_a[...] = jnp.zeros_like(st_a)
            st_b[...] = jnp.zeros_like(st_b)

        _interleave(_gdn_scan_steps(*refs[0:6], refs[12], refs[13], st_a),
                    _ret_fwd_steps(*refs[6:12], refs[14], refs[15], st_b))

    state = jax.ShapeDtypeStruct((HEADS, n_chunks, DK, DV), BF16)
    return pl.pallas_call(
        body, name=name, grid=(HEADS // HB, n_chunks),
        in_specs=[tok(DK), tok(DK), tok(DV), tok(DK), per(CHUNK, CHUNK), per(1, CHUNK),
                  pl.BlockSpec((CHUNK, HB * DK), lambda g, n: (n, bq + g)),
                  pl.BlockSpec((CHUNK, HB * DK), lambda g, n: (n, bk + g)),
                  pl.BlockSpec((CHUNK, HB * DV), lambda g, n: (n, bv + g)),
                  pl.BlockSpec((CHUNK, DK), lambda g, n: (n, 0)), pl.BlockSpec((CHUNK, DK), lambda g, n: (n, 0)),
                  pl.BlockSpec((HB, 1, LANES), lambda g, n: (g, 0, 0))],
        out_specs=[tok(DV), per(DK, DV), tok(DV), per(DK, DV)],
        out_shape=[jax.ShapeDtypeStruct((t, HEADS * DV), F32), state, jax.ShapeDtypeStruct((t, HEADS * DV), F32), state],
        scratch_shapes=[pltpu.VMEM((HB, DK, DV), F32)] * 2,
        compiler_params=_params(("arbitrary", "arbitrary")),
    )(q, k, u, w, p, gc_r, proj, proj, proj, cos2, sin2, lg_tab)


def _ret_bwd_steps(q_ref, k_ref, v_ref, cos_ref, sin_ref, lg_ref, s_ref, do_ref, d_ref, dst):
    nqk = HEADS * DK
    cos2v, sin2v = cos_ref[...], sin_ref[...]
    hs = range(HB)
    s = [s_ref[h].astype(F32) for h in hs]
    d_s = [dst[h] for h in hs]
    d_o = [do_ref[:, _hcols(h, DV)] for h in hs]
    cst = [_ret_consts(lg_ref[h][:, 0:1]) for h in hs]
    qq = [_rot(q_ref[:, _hcols(h, DK)], cos2v, sin2v) for h in hs]
    kk = [_rot(k_ref[:, _hcols(h, DK)], cos2v, sin2v) * (DK ** -0.5) for h in hs]
    vv = [v_ref[:, _hcols(h, DV)] for h in hs]
    dxo = [d_o[h] * cst[h][1] for h in hs]
    yield
    sc = [_dot_nt(qq[h], kk[h]) * cst[h][0] for h in hs]
    dsc = [_dot_nt(d_o[h], vv[h]) * cst[h][0] for h in hs]
    yield
    t1 = [_dot(kk[h] * cst[h][2], d_s[h]) for h in hs]
    t2 = [_dot_nt(dxo[h], s[h]) for h in hs]
    yield
    t3 = [_dot_nt(vv[h], d_s[h]) for h in hs]
    t4 = [_dot_tn(qq[h], dxo[h]) for h in hs]
    yield
    t5 = [_dot_tn(sc[h], d_o[h]) for h in hs]
    t6 = [_dot(dsc[h], kk[h]) for h in hs]
    yield
    t7 = [_dot_tn(dsc[h], qq[h]) for h in hs]
    yield
    for h in hs:
        dst[h] = d_s[h] * cst[h][3] + t4[h]
        d_ref[:, 2 * nqk + h * DV:2 * nqk + (h + 1) * DV] = (t5[h] + t1[h]).astype(BF16)
        d_ref[:, _hcols(h, DK)] = _unrot(t6[h] + t2[h], cos2v, sin2v).astype(BF16)
        d_ref[:, nqk + h * DK:nqk + (h + 1) * DK] = (
            _unrot(t7[h] + t3[h] * cst[h][2], cos2v, sin2v) * (DK ** -0.5)).astype(BF16)


def _mix_bwd(q, k, v, beta_r, gc_r, ti, u, w, s_a, do_a, proj, cos2, sin2, lg_tab, s_b, do_b, dproj,
             col_q, col_k, col_v, *, name):
    t = q.shape[0]
    n_chunks = t // CHUNK
    tok, per = _hn_specs(n_chunks, rev=True)
    bq, bk, bv = col_q // (HB * DK), col_k // (HB * DK), col_v // (HB * DV)
    rv = lambda n: n_chunks - 1 - n
    nqk, wid = HEADS * DK, HEADS * (2 * DK + DV)
    assert HB == HEADS and col_k == col_q + nqk and col_v == col_k + nqk and col_q % wid == 0
    n_a, n_b = 10, 8

    def body(*refs):
        ins_a, ins_b = refs[:n_a], refs[n_a:n_a + n_b]
        outs = refs[n_a + n_b + 1:n_a + n_b + 7]
        dst_a, dst_b = refs[-2:]

        @pl.when(pl.program_id(1) == 0)
        def _():
            dst_a[...] = jnp.zeros_like(dst_a)
            dst_b[...] = jnp.zeros_like(dst_b)

        _interleave(_gdn_bwd_steps(*ins_a, *outs[:5], dst_a), _ret_bwd_steps(*ins_b, outs[5], dst_b))

    rows = jax.ShapeDtypeStruct((HEADS, n_chunks, 1, CHUNK), F32)
    return pl.pallas_call(
        body, name=name, grid=(HEADS // HB, n_chunks),
        in_specs=[tok(DK), tok(DK), tok(DV), per(1, CHUNK), per(1, CHUNK), per(CHUNK, CHUNK), tok(DV), tok(DK),
                  per(DK, DV), tok(DV),
                  pl.BlockSpec((CHUNK, HB * DK), lambda g, n: (rv(n), bq + g)),
                  pl.BlockSpec((CHUNK, HB * DK), lambda g, n: (rv(n), bk + g)),
                  pl.BlockSpec((CHUNK, HB * DV), lambda g, n: (rv(n), bv + g)),
                  pl.BlockSpec((CHUNK, DK), lambda g, n: (rv(n), 0)),
                  pl.BlockSpec((CHUNK, DK), lambda g, n: (rv(n), 0)),
                  pl.BlockSpec((HB, 1, LANES), lambda g, n: (g, 0, 0)),
                  per(DK, DV), tok(DV), pl.BlockSpec(memory_space=pl.ANY)],
        out_specs=[tok(DK), tok(DK), tok(DV), per(1, CHUNK), per(1, CHUNK),
                   pl.BlockSpec((CHUNK, wid), lambda g, n: (rv(n), col_q // wid))],
        out_shape=[jax.ShapeDtypeStruct((t, HEADS * DK), F32), jax.ShapeDtypeStruct((t, HEADS * DK), F32),
                   jax.ShapeDtypeStruct((t, HEADS * DV), F32), rows, rows, jax.ShapeDtypeStruct(dproj.shape, BF16)],
        scratch_shapes=[pltpu.VMEM((HB, DK, DV), F32)] * 2,
        input_output_aliases={n_a + n_b: 5},
        compiler_params=_params(("arbitrary", "arbitrary")),
    )(q, k, v, beta_r, gc_r, ti, u, w, s_a, do_a, proj, proj, proj, cos2, sin2, lg_tab, s_b, do_b, dproj)


def _merge_parts(oa, ob, z, rg, ga, gb):
    ra = lax.rsqrt(jnp.mean(oa * oa, axis=-1, keepdims=True) + EPS)
    xa = oa * ra
    mu = jnp.mean(ob, axis=-1, keepdims=True)
    cen = ob - mu
    rb = lax.rsqrt(jnp.mean(cen * cen, axis=-1, keepdims=True) + EPS)
    xb = cen * rb
    sz, sr = _sigmoid(z), _sigmoid(rg)
    return ra, xa, rb, xb, sz, sr, _sigmoid(ga), _sigmoid(gb)


def _merge_specs(tt, d, cz, crg, cga, cgb):
    blk = pl.BlockSpec((tt, d), lambda i: (i, 0))
    pcol = lambda c: pl.BlockSpec((tt, d), lambda i: (i, c // d))
    row = pl.BlockSpec((1, d), lambda i: (0, 0))
    return blk, [blk, blk, pcol(cz), pcol(crg), pcol(cga), pcol(cgb), row, row], row


def _merge_fwd(oa, ob, proj, wa, wb, cz, crg, cga, cgb, *, name):
    t, d = oa.shape
    tt = _pick(t, (256,))
    blk, in_specs, _ = _merge_specs(tt, d, cz, crg, cga, cgb)

    def body(oa_ref, ob_ref, z_ref, rg_ref, ga_ref, gb_ref, wa_ref, wb_ref, o_ref):
        for h in range(HEADS):
            cols = slice(h * DV, (h + 1) * DV)
            z, rg = z_ref[:, cols], rg_ref[:, cols]
            _, xa, _, xb, sz, sr, sga, sgb = _merge_parts(
                oa_ref[:, cols], ob_ref[:, cols], z, rg, ga_ref[:, cols], gb_ref[:, cols])
            o_a = xa * wa_ref[:, cols] * (z * sz)
            o_b = xb * wb_ref[:, cols] * (rg * sr)
            o_ref[:, cols] = (sga * o_a + sgb * o_b).astype(BF16)

    return pl.pallas_call(
        body, name=name, grid=(t // tt,), in_specs=in_specs, out_specs=blk,
        out_shape=jax.ShapeDtypeStruct((t, d), BF16), compiler_params=_params(("parallel",)),
    )(oa, ob, proj, proj, proj, proj, wa, wb)


def _merge_bwd(dmix, oa, ob, proj, wa, wb, cz, crg, cga, cgb, *, name, side):
    t, d = oa.shape
    tt = _pick(t, (256,))
    nt = t // tt
    blk, in_specs, row = _merge_specs(tt, d, cz, crg, cga, cgb)
    assert (crg, cga, cgb) == (cz + d, cz + 2 * d, cz + 3 * d) and cz % (4 * d) == 0
    n_si, n_so = len(side.ins), len(side.out_shapes)
    assert len(side.phases) == 2

    def body(*refs):
        dm_ref, oa_ref, ob_ref, z_ref, rg_ref, ga_ref, gb_ref, wa_ref, wb_ref = refs[:9]
        s_in = refs[9:9 + n_si]
        doa_ref, dob_ref, dp_ref, dwa_ref, dwb_ref = refs[9 + n_si:14 + n_si]
        s_out, ssem, rsem = refs[14 + n_si:14 + n_si + n_so], refs[-2], refs[-1]
        dz_ref, drg_ref, dga_ref, dgb_ref = [dp_ref.at[:, i * d:(i + 1) * d] for i in range(4)]

        @pl.when(pl.program_id(0) == 0)
        def _():
            side.phases[0](s_in, s_out, ssem, rsem)
            dwa_ref[...] = jnp.zeros_like(dwa_ref)
            dwb_ref[...] = jnp.zeros_like(dwb_ref)

        for h in range(HEADS):
            cols = slice(h * DV, (h + 1) * DV)
            z, rg, wa_h, wb_h, dmx = z_ref[:, cols], rg_ref[:, cols], wa_ref[:, cols], wb_ref[:, cols], dm_ref[:, cols]
            ra, xa, rb, xb, sz, sr, sga, sgb = _merge_parts(
                oa_ref[:, cols], ob_ref[:, cols], z, rg, ga_ref[:, cols], gb_ref[:, cols])
            na, nb = xa * wa_h, xb * wb_h
            sil_z, sil_r = z * sz, rg * sr
            o_a, o_b = na * sil_z, nb * sil_r
            dga_ref[:, cols] = (dmx * o_a * sga * (1.0 - sga)).astype(BF16)
            dgb_ref[:, cols] = (dmx * o_b * sgb * (1.0 - sgb)).astype(BF16)
            d_oa, d_ob = dmx * sga, dmx * sgb
            dz_ref[:, cols] = (d_oa * na * sz * (1.0 + z * (1.0 - sz))).astype(BF16)
            drg_ref[:, cols] = (d_ob * nb * sr * (1.0 + rg * (1.0 - sr))).astype(BF16)
            dna, dnb = d_oa * sil_z, d_ob * sil_r
            dwa_ref[:, cols] += jnp.sum(dna * xa, axis=0, keepdims=True)
            dwb_ref[:, cols] += jnp.sum(dnb * xb, axis=0, keepdims=True)
            gwa, gwb = dna * wa_h, dnb * wb_h
            doa_ref[:, cols] = ra * (gwa - xa * jnp.mean(gwa * xa, axis=-1, keepdims=True))
            dob_ref[:, cols] = rb * (gwb - jnp.mean(gwb, axis=-1, keepdims=True)
                                     - xb * jnp.mean(gwb * xb, axis=-1, keepdims=True))

        @pl.when(pl.program_id(0) == nt - 1)
        def _():
            side.phases[1](s_in, s_out, ssem, rsem)

    hbm = pl.BlockSpec(memory_space=pl.ANY)
    return pl.pallas_call(
        body, name=name, grid=(nt,), in_specs=[blk] + in_specs + [hbm] * n_si,
        out_specs=[blk, blk, pl.BlockSpec((tt, 4 * d), lambda i: (i, cz // (4 * d))), row, row] + [hbm] * n_so,
        out_shape=[jax.ShapeDtypeStruct((t, d), F32)] * 2 + [jax.ShapeDtypeStruct(proj.shape, BF16)]
        + [jax.ShapeDtypeStruct((1, d), F32)] * 2 + side.out_shapes,
        scratch_shapes=side.sems(),
        compiler_params=_params(("arbitrary",)),
    )(dmix, oa, ob, proj, proj, proj, proj, wa, wb, *side.ins)


def _row_block(rows, cols, itemsize=4, target=1 << 20):
    for rb in (512, 256, 128, 64, 32, 16, 8):
        if rows % rb == 0 and rb * cols * itemsize <= target:
            return rb
    return rows


def _adamw(w, g, m, v, *, name):
    rows, cols = w.shape
    rb = _row_block(rows, cols)

    def body(w_ref, g_ref, m_ref, v_ref, d_ref, nm_ref, nv_ref):
        gg = g_ref[...]
        mm = ADAM_B1 * m_ref[...] + (1.0 - ADAM_B1) * gg
        vv = ADAM_B2 * v_ref[...] + (1.0 - ADAM_B2) * (gg * gg)
        m_hat = mm / (1.0 - ADAM_B1 ** ADAM_STEP)
        v_hat = vv / (1.0 - ADAM_B2 ** ADAM_STEP)
        d_ref[...] = -ADAM_LR * (m_hat / (jnp.sqrt(v_hat) + ADAM_EPS) + ADAM_WD * w_ref[...])
        nm_ref[...] = mm
        nv_ref[...] = vv

    blk = pl.BlockSpec((rb, cols), lambda i: (i, 0))
    return pl.pallas_call(
        body, name=name, grid=(rows // rb,), in_specs=[blk] * 4, out_specs=[blk] * 3,
        out_shape=[jax.ShapeDtypeStruct((rows, cols), F32)] * 3, compiler_params=_params(("parallel",)),
    )(w, g, m, v)


def _adamw_halves(w, mine, other, c_idx, m, v, *, name):
    rows, cols = w.shape
    hr = rows // 2
    rb = _row_block(hr, cols)
    nb = hr // rb

    def body(c_ref, w_ref, a_ref, b_ref, m_ref, v_ref, g_ref, d_ref, nm_ref, nv_ref):
        gg = jnp.where(pl.program_id(0) // nb == c_ref[0], a_ref[...], b_ref[...])
        mm = ADAM_B1 * m_ref[...] + (1.0 - ADAM_B1) * gg
        vv = ADAM_B2 * v_ref[...] + (1.0 - ADAM_B2) * (gg * gg)
        m_hat = mm / (1.0 - ADAM_B1 ** ADAM_STEP)
        v_hat = vv / (1.0 - ADAM_B2 ** ADAM_STEP)
        g_ref[...] = gg
        d_ref[...] = -ADAM_LR * (m_hat / (jnp.sqrt(v_hat) + ADAM_EPS) + ADAM_WD * w_ref[...])
        nm_ref[...] = mm
        nv_ref[...] = vv

    blk = pl.BlockSpec((rb, cols), lambda i, c: (i, 0))
    half = pl.BlockSpec((rb, cols), lambda i, c: (i % nb, 0))
    return pl.pallas_call(
        body, name=name,
        grid_spec=pltpu.PrefetchScalarGridSpec(
            num_scalar_prefetch=1, grid=(rows // rb,), in_specs=[blk, half, half, blk, blk], out_specs=[blk] * 4),
        out_shape=[jax.ShapeDtypeStruct((rows, cols), F32)] * 4, compiler_params=_params(("parallel",)),
    )(c_idx, w, mine, other, m, v)


def _pair_add(g, rsib, c_idx, *, name):
    _, _, hr, cols = g.shape
    rb = _row_block(hr, cols, 2)

    def body(c_ref, g_ref, r_ref, o_ref):
        o_ref[...] = (g_ref[...].astype(F32) + r_ref[...].astype(F32)).astype(BF16)

    return pl.pallas_call(
        body, name=name,
        grid_spec=pltpu.PrefetchScalarGridSpec(
            num_scalar_prefetch=1, grid=(N_CHIPS, hr // rb),
            in_specs=[pl.BlockSpec((None, None, rb, cols), lambda j, i, c: (j, c[0], i, 0)),
                      pl.BlockSpec((None, rb, cols), lambda j, i, c: (j, i, 0))],
            out_specs=pl.BlockSpec((None, rb, cols), lambda j, i, c: (j, i, 0))),
        out_shape=jax.ShapeDtypeStruct((N_CHIPS, hr, cols), BF16),
        compiler_params=_params(("parallel", "parallel")),
    )(c_idx, g, rsib)


def _chip_sum(qb, *, name):
    _, hr, cols = qb.shape
    rb = _row_block(hr, cols, 2, 1 << 19)

    def body(q_ref, o_ref):
        acc = q_ref[0].astype(F32)
        for i in range(1, N_CHIPS):
            acc = acc + q_ref[i].astype(F32)
        o_ref[...] = acc

    return pl.pallas_call(
        body, name=name, grid=(hr // rb,),
        in_specs=[pl.BlockSpec((N_CHIPS, rb, cols), lambda i: (0, i, 0))],
        out_specs=pl.BlockSpec((rb, cols), lambda i: (i, 0)),
        out_shape=jax.ShapeDtypeStruct((hr, cols), F32), compiler_params=_params(("parallel",)),
    )(qb)


def _place():
    x, y, c = lax.axis_index("x"), lax.axis_index("y"), lax.axis_index("c")
    return x, y, c, [(1 - x, y), (x, 1 - y), (1 - x, 1 - y)]


ANY = pl.BlockSpec(memory_space=pl.ANY)


def _gather_side(shards):
    nw, n_k = len(shards), 8

    def plan(ins, outs, ssem, rsem):
        x, y, c, _ = _place()
        nbr_x, nbr_y, sib = (1 - x, y, c), (x, 1 - y, c), (x, y, 1 - c)
        s_me, s_x, s_y, s_d = 2 * x + y, 2 * (1 - x) + y, 2 * x + 1 - y, 2 * (1 - x) + 1 - y

        def rows(w, half, quarter=None):
            hr = shards[w].shape[0] // 2
            if quarter is None:
                return pl.ds(pl.multiple_of(half * hr, 16), hr)
            return pl.ds(pl.multiple_of(half * hr + quarter * (hr // 2), 16), hr // 2)

        def rcopy(w, k, src, slot, rws, to):
            return pltpu.make_async_remote_copy(
                src_ref=src, dst_ref=outs[w].at[slot, rws], send_sem=ssem.at[w * n_k + k],
                recv_sem=rsem.at[w * n_k + k], device_id=to, device_id_type=MESH)

        out = []
        for w in range(nw):
            mine = ins[w].at[rows(w, c)]
            q = [rows(w, c, 0), rows(w, c, 1)]
            p = functools.partial
            out.append([
                (p(rcopy, w, 0, mine, s_me, rows(w, c), nbr_x), p(rcopy, w, 0, mine, s_x, rows(w, c), nbr_x)),
                (p(rcopy, w, 1, mine, s_me, rows(w, c), nbr_y), p(rcopy, w, 1, mine, s_y, rows(w, c), nbr_y)),
                (p(rcopy, w, 2, outs[w].at[s_x, q[0]], s_x, q[0], nbr_y),
                 p(rcopy, w, 2, outs[w].at[s_x, q[0]], s_d, q[0], nbr_y)),
                (p(rcopy, w, 3, outs[w].at[s_y, q[1]], s_y, q[1], nbr_x),
                 p(rcopy, w, 3, outs[w].at[s_y, q[1]], s_d, q[1], nbr_x)),
                (p(rcopy, w, 4, outs[w].at[s_x, rows(w, c)], s_x, rows(w, c), sib),
                 p(rcopy, w, 4, mine, s_x, rows(w, 1 - c), sib)),
                (p(rcopy, w, 5, outs[w].at[s_y, rows(w, c)], s_y, rows(w, c), sib),
                 p(rcopy, w, 5, mine, s_y, rows(w, 1 - c), sib)),
                (p(rcopy, w, 6, outs[w].at[s_d, q[0]], s_d, q[0], sib),
                 p(rcopy, w, 6, outs[w].at[s_d, q[0]], s_d, rows(w, 1 - c, 0), sib)),
                (p(rcopy, w, 7, outs[w].at[s_d, q[1]], s_d, q[1], sib),
                 p(rcopy, w, 7, outs[w].at[s_d, q[1]], s_d, rows(w, 1 - c, 1), sib)),
            ])
        return out

    def send_own(*refs):
        for cps in plan(*refs):
            cps[0][0]().start()
            cps[1][0]().start()

    def relay(*refs):
        for cps in plan(*refs):
            cps[0][1]().wait_recv()
            cps[2][0]().start()
            cps[4][0]().start()
            cps[1][1]().wait_recv()
            cps[3][0]().start()
            cps[5][0]().start()

    def pass_diagonal(*refs):
        for cps in plan(*refs):
            cps[2][1]().wait_recv()
            cps[6][0]().start()
            cps[3][1]().wait_recv()
            cps[7][0]().start()

    def finish(*refs):
        for cps in plan(*refs):
            for k in (4, 5, 6, 7):
                cps[k][1]().wait_recv()
            for k in range(n_k):
                cps[k][0]().wait_send()

    return _Side(shards, [jax.ShapeDtypeStruct((N_CHIPS,) + s.shape, s.dtype) for s in shards], nw * n_k,
                 [send_own, relay, pass_diagonal, finish])


def _run_side(side, *, name):
    n_in, n_out = len(side.ins), len(side.out_shapes)

    def body(*refs):
        ins, outs = refs[:n_in], refs[n_in:n_in + n_out]
        for phase in side.phases:
            phase(ins, outs, refs[-2], refs[-1])

    return pl.pallas_call(
        body, name=name, in_specs=[ANY] * n_in, out_specs=[ANY] * n_out, out_shape=side.out_shapes,
        scratch_shapes=side.sems(), compiler_params=pltpu.CompilerParams(has_side_effects=True),
    )(*side.ins)


def _exchange_side(gs):
    nw = len(gs)

    def copies(ins, outs, ssem, rsem):
        x, y, c, _ = _place()
        return [pltpu.make_async_remote_copy(
            src_ref=ins[w].at[:, 1 - c], dst_ref=outs[w], send_sem=ssem.at[w], recv_sem=rsem.at[w],
            device_id=(x, y, 1 - c), device_id_type=MESH) for w in range(nw)]

    def start(*refs):
        for cp in copies(*refs):
            cp.start()

    def finish(*refs):
        for cp in copies(*refs):
            cp.wait()

    return _Side(gs, [jax.ShapeDtypeStruct((g.shape[0],) + g.shape[2:], g.dtype) for g in gs], nw, [start, finish])


def _scatter_side(ps):
    nw = len(ps)

    def copies(ins, outs, ssem, rsem):
        x, y, c, chips = _place()
        s_me = 2 * x + y
        return [pltpu.make_async_remote_copy(
            src_ref=ins[w].at[2 * chip[0] + chip[1]], dst_ref=outs[w].at[s_me],
            send_sem=ssem.at[w * 3 + j], recv_sem=rsem.at[w * 3 + j],
            device_id=(chip[0], chip[1], c), device_id_type=MESH)
            for w in range(nw) for j, chip in enumerate(chips)]

    def start(*refs):
        for cp in copies(*refs):
            cp.start()

    def finish(*refs):
        for cp in copies(*refs):
            cp.wait()

    return _Side(ps, [jax.ShapeDtypeStruct(p.shape, p.dtype) for p in ps], nw * 3, [start, finish])


def _pair_share(hs, *, name):
    nw = len(hs)

    def body(*refs):
        ins, outs = refs[:nw], refs[nw:2 * nw]
        ssem, rsem = refs[2 * nw:]
        x, y, c, _ = _place()
        cps = []
        for w in range(nw):
            cp = pltpu.make_async_remote_copy(
                src_ref=ins[w], dst_ref=outs[w], send_sem=ssem.at[w], recv_sem=rsem.at[w],
                device_id=(x, y, 1 - c), device_id_type=MESH)
            cp.start()
            cps.append(cp)
        for cp in cps:
            cp.wait()

    return pl.pallas_call(
        body, name=name, in_specs=[ANY] * nw, out_specs=[ANY] * nw,
        out_shape=[jax.ShapeDtypeStruct(h.shape, h.dtype) for h in hs],
        scratch_shapes=[pltpu.SemaphoreType.DMA((nw,)), pltpu.SemaphoreType.DMA((nw,))],
        compiler_params=pltpu.CompilerParams(has_side_effects=True),
    )(*hs)


def _gather_small(a, *, name):
    def body(a_ref, o_ref, ssem, rsem):
        x, y, c, chips = _place()
        s_me = 2 * x + y
        o_ref[s_me] = a_ref[...]
        cps = []
        for j, chip in enumerate(chips):
            cp = pltpu.make_async_remote_copy(
                src_ref=a_ref, dst_ref=o_ref.at[s_me], send_sem=ssem.at[j], recv_sem=rsem.at[j],
                device_id=(chip[0], chip[1], c), device_id_type=MESH)
            cp.start()
            cps.append(cp)
        for cp in cps:
            cp.wait()

    vm = pl.BlockSpec(memory_space=pltpu.VMEM)
    return pl.pallas_call(
        body, name=name, in_specs=[vm], out_specs=vm,
        out_shape=jax.ShapeDtypeStruct((N_CHIPS,) + a.shape, a.dtype),
        scratch_shapes=[pltpu.SemaphoreType.DMA((3,)), pltpu.SemaphoreType.DMA((3,))],
    )(a)


def _allreduce_small(p, *, name):
    def body(p_ref, o_ref, buf, ssem, rsem):
        x, y, c, _ = _place()
        me = 4 * x + 2 * y + c
        buf[me] = p_ref[...]
        cps = []
        for k in range(1, N_DEV):
            fx, fy, fc = (k >> 2) & 1, (k >> 1) & 1, k & 1
            peer = (x + fx - 2 * x * fx, y + fy - 2 * y * fy, c + fc - 2 * c * fc)
            cp = pltpu.make_async_remote_copy(
                src_ref=buf.at[me], dst_ref=buf.at[me], send_sem=ssem.at[k - 1], recv_sem=rsem.at[k - 1],
                device_id=peer, device_id_type=MESH)
            cp.start()
            cps.append(cp)
        for cp in cps:
            cp.wait()
        acc = buf[0]
        for i in range(1, N_DEV):
            acc = acc + buf[i]
        o_ref[...] = acc

    vm = pl.BlockSpec(memory_space=pltpu.VMEM)
    return pl.pallas_call(
        body, name=name, in_specs=[vm], out_specs=vm,
        out_shape=jax.ShapeDtypeStruct(p.shape, p.dtype),
        scratch_shapes=[pltpu.VMEM((N_DEV,) + p.shape, p.dtype), pltpu.SemaphoreType.DMA((N_DEV - 1,)),
                        pltpu.SemaphoreType.DMA((N_DEV - 1,))],
    )(p)


def _rows_to_tokens(r):
    h, n = r.shape[0], r.shape[1]
    return r.reshape(h, n * CHUNK).T


def _tokens_to_rows(a):
    t, h = a.shape
    return a.T.reshape(h, t // CHUNK, 1, CHUNK)


def kernel(x, norm1_w, w_in, conv_w, a_log, dt_bias, gdn_norm_w, ret_norm_w, w_out, norm2_w, w_gate, w_up, w_down, norm_f_w, loss_target, m_norm1_w, m_w_in, m_conv_w, m_a_log, m_dt_bias, m_gdn_norm_w, m_ret_norm_w, m_w_out, m_norm2_w, m_w_gate, m_w_up, m_w_down, m_norm_f_w, v_norm1_w, v_w_in, v_conv_w, v_a_log, v_dt_bias, v_gdn_norm_w, v_ret_norm_w, v_w_out, v_norm2_w, v_w_gate, v_w_up, v_w_down, v_norm_f_w):
    t, d = x.shape[1], x.shape[2]
    f = w_gate.shape[2] * N_CHIPS
    nqk, nv = HEADS * DK, HEADS * DV
    ncs = w_in.shape[2]
    c_idx = lax.axis_index("c")
    s_idx = 2 * lax.axis_index("x") + lax.axis_index("y")
    xs = x[0]
    tgt = loss_target[0]

    n_small = 2 * HEADS
    widths = [2 * nqk + nv, nv, n_small, nqk, nqk, nv, nv, d, d]
    g_off = np.concatenate([[0], np.cumsum(widths)])
    order = [0, 3, 4, 5, 1, 6, 7, 8]
    m_off = np.concatenate([[0], np.cumsum([widths[i] for i in order])])
    o_rq, o_rk, o_rv, o_az, o_rg, o_ga, o_gb = [int(m_off[i]) for i in range(1, 8)]
    segs = [(int(g_off[i]), int(g_off[i + 1]), int(m_off[order.index(i)]) if i != 2 else None) for i in range(9)]

    def shard_pieces(s):
        out = []
        for a, b, dst in segs:
            lo, hi = max(a, s * ncs), min(b, (s + 1) * ncs)
            if lo < hi:
                out.append((lo - s * ncs, hi - s * ncs, None if dst is None else dst + lo - a))
        return out

    own = [w_in[0].astype(BF16), w_out[0].astype(BF16), w_gate[0].astype(BF16), w_up[0].astype(BF16),
           w_down[0].astype(BF16)]
    put_own = lambda full, mine: lax.dynamic_update_slice(full, mine[None], (s_idx, 0, 0))
    wg_in = put_own(_run_side(_gather_side(own[:1]), name="gather_w_in")[0], own[0])
    conv_full = _gather_small(conv_w[0], name="gather_conv_w")
    conv_full = jnp.concatenate([conv_full[i] for i in range(N_CHIPS)], axis=1)
    cuts = [(dst, s, a, b) for s in range(N_CHIPS) for a, b, dst in shard_pieces(s)]
    w_main = jnp.concatenate([wg_in[s][:, a:b] for dst, s, a, b in sorted(c for c in cuts if c[0] is not None)],
                             axis=1)
    w_small = jnp.concatenate([wg_in[s][:, a:b] for dst, s, a, b in cuts if dst is None], axis=1)
    w_small = jnp.pad(w_small, ((0, 0), (0, LANES - n_small)))

    pad16 = lambda a: jnp.pad(a, ((0, 0), (HEADS, LANES - 2 * HEADS)))
    alog_row, dtb_row = pad16(a_log), pad16(dt_bias)
    wa_row = jnp.tile(gdn_norm_w, (1, HEADS))
    inv = ROPE_BASE ** (-jnp.arange(0, DK, 2, dtype=F32) / DK)
    ang = jnp.arange(t, dtype=F32)[:, None] * inv[None, :]
    cos2 = jnp.concatenate([jnp.cos(ang), jnp.cos(ang)], axis=1)
    sin2 = jnp.concatenate([-jnp.sin(ang), jnp.sin(ang)], axis=1)
    lg = jnp.log1p(-jnp.exp2(-5.0 - jnp.arange(HEADS, dtype=F32)))
    lg_tab = jnp.broadcast_to(lg[:, None, None], (HEADS, 1, LANES))

    fq = f // N_CHIPS
    u1, projs = _rms_fwd(xs, norm1_w, w_narrow=w_small, name="rms1_fwd")
    proj, *rest = _mm(u1, w_main, tm=512, tn=2048, tk=d, side=_gather_side(own[1:]), name="mm_proj")
    wg_out, wg_gate, wg_up, wg_down = [put_own(g, o) for g, o in zip(rest, own[1:])]
    w_o = wg_out.reshape(d, d)
    w_g, w_u = wg_gate, wg_up
    w_d = wg_down.reshape(f, d)
    q_a, k_a, v_a = _conv_fwd(proj, conv_full, name="conv_fwd")
    bg = _bg_fwd(projs, alog_row, dtb_row, name="bg_fwd")
    beta_r = _tokens_to_rows(bg[:, :HEADS])
    g_r = _tokens_to_rows(bg[:, HEADS:2 * HEADS])
    gc_r, tinv, u_a, w_a, p_a = _gdn_prep(q_a, k_a, v_a, beta_r, g_r, name="gdn_prep")
    o_a, s_a, o_b, s_b = _mix_fwd(q_a, k_a, u_a, w_a, p_a, gc_r, proj, cos2, sin2, lg_tab, o_rq, o_rk, o_rv,
                                  name="mix_fwd")
    mixed = _merge_fwd(o_a, o_b, proj, wa_row, ret_norm_w, o_az, o_rg, o_ga, o_gb, name="merge_fwd")
    h1, hn = _mm_res_norm(mixed, w_o, xs, norm2_w, tm=512, name="mm_out")
    gt, up, act = _ffn_in(hn, w_g, w_u, tm=512, name="ffn_in")
    h2 = _mm(act, w_d, res=h1, tm=512, tn=1024, tk=f, name="mm_down")
    loss_row, dh2, dh2b, d_nf = _loss_head(h2, tgt, norm_f_w.reshape(1, d), name="loss_head")

    g_down = _mm(act, dh2b, ta=True, out_dtype=BF16, tm=fq, tn=d, tk=1024, name="mm_dw_down")
    dgt, dup = _ffn_back(dh2b, w_d, gt, up, tm=512, tn=fq, name="ffn_back")
    dhn = _mm(dgt, w_g, tb=True, pair=(dup, w_u), tm=512, tn=d, tk=fq, name="mm_dhn")
    g_gate = _mm(hn, dgt, ta=True, out_dtype=BF16, out_stacked=True, tm=1024, tn=fq, tk=2048, name="mm_dw_gate")
    g_up = _mm(hn, dup, ta=True, out_dtype=BF16, out_stacked=True, tm=1024, tn=fq, tk=2048, name="mm_dw_up")
    dh1, dh1b, d_n2 = _rms_bwd(dhn, h1, norm2_w, dh2, name="rms2_bwd")
    dmix = _mm(dh1b, w_o, tb=True, tm=512, tn=d, tk=d, name="mm_dmix")
    g_out = _mm(mixed, dh1b, ta=True, out_dtype=BF16, tm=1024, tn=d, tk=1024, name="mm_dw_out")
    halves = lambda g: g.reshape(N_CHIPS, 2, g.shape[1] // 2, g.shape[2])
    c_arr = jnp.reshape(c_idx, (1,)).astype(jnp.int32)
    gs_ffn = [halves(g_out.reshape(N_CHIPS, d // N_CHIPS, d)), halves(g_gate), halves(g_up),
              halves(g_down.reshape(N_CHIPS, fq, d))]
    do_a, do_b, dproj, d_wa, d_wb, *rsib = _merge_bwd(
        dmix, o_a, o_b, proj, wa_row, ret_norm_w, o_az, o_rg, o_ga, o_gb, side=_exchange_side(gs_ffn),
        name="merge_bwd")
    ps_ffn = [_pair_add(g, r, c_arr, name=f"grad_pair_add_{nm}")
              for g, r, nm in zip(gs_ffn, rsib, ["w_out", "w_gate", "w_up", "w_down"])]
    dq_a, dk_a, dv_a, dbeta_r, dg_r, dproj = _mix_bwd(
        q_a, k_a, v_a, beta_r, gc_r, tinv, u_a, w_a, s_a, do_a, proj, cos2, sin2, lg_tab, s_b, do_b, dproj,
        o_rq, o_rk, o_rv, name="mix_bwd")
    dc = _conv_bwd_pre(proj, conv_full, dq_a, dk_a, dv_a, name="conv_bwd_pre")
    dproj, d_cw = _conv_bwd(proj, dc, conv_full, dproj, name="conv_bwd")
    dbg = jnp.pad(jnp.concatenate([_rows_to_tokens(dbeta_r), _rows_to_tokens(dg_r)], axis=1),
                  ((0, 0), (0, LANES - 2 * HEADS)))
    dprojs, d_alog, d_dtb = _bg_bwd(projs, dbg, alog_row, dtb_row, name="bg_bwd")
    g_main, *qs_ffn = _mm(u1, dproj, ta=True, out_dtype=BF16, tm=1024, tn=2048, tk=2048,
                          side=_scatter_side(ps_ffn), name="mm_dw_in")
    g_small = _mm(u1, dprojs, ta=True, out_dtype=BF16, tm=1024, tn=LANES, tk=1024, name="mm_dw_in_small")
    pieces = []
    for s in range(N_CHIPS):
        seen, parts = 0, []
        for a, b, dst in shard_pieces(s):
            parts.append(g_small[:, seen:seen + b - a] if dst is None else g_main[:, dst:dst + b - a])
            seen += (b - a) if dst is None else 0
        pieces.append(jnp.concatenate(parts, axis=1))
    gs_in = [halves(jnp.stack(pieces))]
    rsib = _run_side(_exchange_side(gs_in), name="grad_pair_exchange_in")
    ps_in = [_pair_add(gs_in[0], rsib[0], c_arr, name="grad_pair_add_w_in")]
    du, *qs_in = _mm(dproj, w_main, tb=True, out_dtype=BF16, tm=1024, tn=d, tk=2048,
                     side=_scatter_side(ps_in), name="mm_du")
    dx, _, d_n1 = _rms_bwd(du, xs, norm1_w, dh1, narrow=(dprojs, w_small), name="rms1_bwd")

    names = ["w_in", "w_out", "w_gate", "w_up", "w_down"]
    qs = [lax.dynamic_update_slice(q, lax.dynamic_slice(p, (s_idx, 0, 0), (1,) + p.shape[1:]), (s_idx, 0, 0))
          for q, p in zip(qs_in + qs_ffn, ps_in + ps_ffn)]
    hs = [_chip_sum(q, name=f"grad_chip_sum_{nm}") for q, nm in zip(qs, names)]
    theirs = _pair_share(hs, name="grad_pair_share")
    big_w = [w_in[0], w_out[0], w_gate[0], w_up[0], w_down[0]]
    big_m = [m_w_in[0], m_w_out[0], m_w_gate[0], m_w_up[0], m_w_down[0]]
    big_v = [v_w_in[0], v_w_out[0], v_w_gate[0], v_w_up[0], v_w_down[0]]
    big = {}
    for nm, mine, other, w_, m_, v_ in zip(names, hs, theirs, big_w, big_m, big_v):
        big[nm] = tuple(a[None] for a in _adamw_halves(w_, mine, other, c_arr, m_, v_, name=f"adamw_{nm}"))

    d_wa_h = jnp.sum(d_wa.reshape(HEADS, DV), axis=0, keepdims=True)
    small = [d_n1, d_alog[:, HEADS:2 * HEADS], d_dtb[:, HEADS:2 * HEADS], d_wa_h, d_wb, d_n2, d_nf,
             d_cw[:CONV_W].reshape(1, -1)]
    sizes = [a.shape[1] for a in small]
    packed = jnp.concatenate(small, axis=1)
    n_pack = packed.shape[1]
    n_rows = -(-n_pack // LANES)
    n_rows = -(-n_rows // 8) * 8
    packed = jnp.pad(packed, ((0, 0), (0, n_rows * LANES - n_pack))).reshape(n_rows, LANES)
    red = _allreduce_small(packed, name="allreduce_small").reshape(1, -1)
    offs = np.cumsum([0] + sizes)
    g_n1, g_alog, g_dtb, g_wa, g_wb, g_n2, g_nf, g_cw = [red[:, offs[i]:offs[i + 1]] for i in range(len(sizes))]
    ncw = conv_w.shape[2]
    g_cw = lax.dynamic_slice(g_cw.reshape(CONV_W, -1), (0, s_idx * ncw), (CONV_W, ncw))

    def small_update(w_, g, m_, v_, nm):
        shape = w_.shape
        pad = (-w_.size) % LANES
        to2 = lambda a: jnp.pad(a.reshape(1, -1), ((0, 0), (0, pad)))
        outs = _adamw(to2(w_), to2(g), to2(m_), to2(v_), name=f"adamw_{nm}")
        return (g.reshape(shape),) + tuple(a[:, :w_.size].reshape(shape) for a in outs)

    res = {
        "norm1_w": small_update(norm1_w, g_n1, m_norm1_w, v_norm1_w, "norm1_w"),
        "w_in": big["w_in"],
        "conv_w": small_update(conv_w, g_cw, m_conv_w, v_conv_w, "conv_w"),
        "a_log": small_update(a_log, g_alog, m_a_log, v_a_log, "a_log"),
        "dt_bias": small_update(dt_bias, g_dtb, m_dt_bias, v_dt_bias, "dt_bias"),
        "gdn_norm_w": small_update(gdn_norm_w, g_wa, m_gdn_norm_w, v_gdn_norm_w, "gdn_norm_w"),
        "ret_norm_w": small_update(ret_norm_w, g_wb, m_ret_norm_w, v_ret_norm_w, "ret_norm_w"),
        "w_out": big["w_out"],
        "norm2_w": small_update(norm2_w, g_n2, m_norm2_w, v_norm2_w, "norm2_w"),
        "w_gate": big["w_gate"],
        "w_up": big["w_up"],
        "w_down": big["w_down"],
        "norm_f_w": small_update(norm_f_w, g_nf, m_norm_f_w, v_norm_f_w, "norm_f_w"),
    }
    order = ["norm1_w", "w_in", "conv_w", "a_log", "dt_bias", "gdn_norm_w", "ret_norm_w", "w_out", "norm2_w",
             "w_gate", "w_up", "w_down", "norm_f_w"]
    loss = lax.psum(loss_row[0, 0], ("x", "y", "c"))
    return (loss, dx[None], *[res[n][0] for n in order], *[res[n][1] for n in order],
            *[res[n][2] for n in order], *[res[n][3] for n in order])
```

```python
import functools

import jax
import jax.numpy as jnp
import numpy as np
from jax import lax
from jax.experimental import pallas as pl
from jax.experimental.pallas import tpu as pltpu

F32 = jnp.float32
BF16 = jnp.bfloat16
MESH = pl.DeviceIdType.MESH

HEADS = 8
DK = 128
DV = 256
CHUNK = 64
CONV_W = 4
EPS = 1e-6
ROPE_BASE = 10000.0
ADAM_LR, ADAM_B1, ADAM_B2, ADAM_EPS, ADAM_WD, ADAM_STEP = 0.001, 0.9, 0.999, 1e-08, 0.01, 10
N_CHIPS = 4
N_DEV = 8
LANES = 128
HALO = 8
VMEM_LIMIT = 56 * 1024 * 1024
HB = 8


def _pick(n, cands):
    for c in cands:
        if n % c == 0:
            return c
    raise ValueError(f"no tile for {n} in {cands}")


def _params(sem=None):
    return pltpu.CompilerParams(dimension_semantics=sem, vmem_limit_bytes=VMEM_LIMIT)


def _dot(a, b):
    return jnp.dot(a.astype(BF16), b.astype(BF16), preferred_element_type=F32)


def _dot_nt(a, b):
    return lax.dot_general(a.astype(BF16), b.astype(BF16), (((1,), (1,)), ((), ())), preferred_element_type=F32)


def _dot_tn(a, b):
    return lax.dot_general(a.astype(BF16), b.astype(BF16), (((0,), (0,)), ((), ())), preferred_element_type=F32)


def _sigmoid(x):
    return 1.0 / (1.0 + jnp.exp(-x))


def _iota2(n):
    return lax.broadcasted_iota(jnp.int32, (n, n), 0), lax.broadcasted_iota(jnp.int32, (n, n), 1)


def _row_to_col(row):
    n = row.shape[1]
    r, c = _iota2(n)
    return jnp.sum(jnp.where(r == c, jnp.broadcast_to(row, (n, n)), 0.0), axis=1, keepdims=True)


def _col_to_row(col):
    n = col.shape[0]
    r, c = _iota2(n)
    return jnp.sum(jnp.where(r == c, jnp.broadcast_to(col, (n, n)), 0.0), axis=0, keepdims=True)


class _Side:
    def __init__(self, ins, out_shapes, n_sem, phases):
        self.ins, self.out_shapes, self.n_sem, self.phases = list(ins), list(out_shapes), n_sem, phases

    def sems(self):
        return [pltpu.SemaphoreType.DMA((self.n_sem,)), pltpu.SemaphoreType.DMA((self.n_sem,))]


def _mm(a, b, *, name, tm, tn, tk, ta=False, tb=False, out_dtype=F32, res=None, side=None, pair=None,
        out_stacked=False):
    m, k = (a.shape[1], a.shape[0]) if ta else a.shape
    b_slots = b.ndim == 3
    if b_slots:
        assert tb and tk == b.shape[2]
        n = b.shape[1]
    else:
        n = b.shape[0] if tb else b.shape[1]
    tm, tn, tk = min(tm, m), min(tn, n), min(tk, k)
    assert m % tm == 0 and n % tn == 0 and k % tk == 0, (name, m, n, k)
    nk = k // tk
    nj, ni = n // tn, m // tm
    dn = (((0 if ta else 1,), (1 if tb else 0,)), ((), ()))
    n_ab = 4 if pair else 2
    n_in = n_ab + (res is not None)
    n_side_in = len(side.ins) if side else 0
    n_side_out = len(side.out_shapes) if side else 0

    def body(*refs):
        a_ref, b_ref = refs[0], refs[1]
        r_ref = refs[n_ab] if res is not None else None
        o_ref = refs[n_in + n_side_in]
        if side:
            s_in = refs[n_in:n_in + n_side_in]
            s_out = refs[n_in + n_side_in + 1:n_in + n_side_in + 1 + n_side_out]
            ssem, rsem = refs[-2], refs[-1]
            j_, i_, k_ = pl.program_id(0), pl.program_id(1), pl.program_id(2)
            n_mid = len(side.phases) - 2
            assert n_mid == 0 or nj >= 2 * n_mid
            when = [(j_ == 0) & (i_ == 0) & (k_ == 0)]
            when += [(j_ == nj // 2 + (p * (nj // 2)) // n_mid) & (i_ == 0) & (k_ == 0) for p in range(n_mid)]
            when.append((j_ == nj - 1) & (i_ == ni - 1) & (k_ == nk - 1))

        def run_phase(p):
            @pl.when(when[p])
            def _():
                side.phases[p](s_in, s_out, ssem, rsem)

        if side:
            for p in range(len(side.phases) - 1):
                run_phase(p)

        def finish(r):
            if res is not None:
                r = r + r_ref[...]
            o_ref[...] = r.astype(out_dtype)

        part = lax.dot_general(a_ref[...], b_ref[...], dn, preferred_element_type=F32)
        if pair:
            part = part + lax.dot_general(refs[2][...], refs[3][...], dn, preferred_element_type=F32)
        if nk == 1:
            finish(part)
        else:
            acc = refs[n_in + n_side_in + 1 + n_side_out]
            kk = pl.program_id(2)

            @pl.when(kk == 0)
            def _():
                acc[...] = part

            @pl.when((kk > 0) & (kk < nk - 1))
            def _():
                acc[...] += part

            @pl.when(kk == nk - 1)
            def _():
                finish(acc[...] + part)

        if side:
            run_phase(len(side.phases) - 1)

    a_spec = pl.BlockSpec((tk, tm), lambda j, i, kk: (kk, i)) if ta else pl.BlockSpec((tm, tk), lambda j, i, kk: (i, kk))
    b_spec = pl.BlockSpec((tn, tk), lambda j, i, kk: (j, kk)) if tb else pl.BlockSpec((tk, tn), lambda j, i, kk: (kk, j))
    if b_slots:
        b_spec = pl.BlockSpec((None, tn, tk), lambda j, i, kk: (kk, j, 0))
    o_spec = pl.BlockSpec((tm, tn), lambda j, i, kk: (i, j))
    o_shape = jax.ShapeDtypeStruct((m, n), out_dtype)
    if out_stacked:
        assert res is None
        o_spec = pl.BlockSpec((None, tm, tn), lambda j, i, kk: (j, i, 0))
        o_shape = jax.ShapeDtypeStruct((nj, m, tn), out_dtype)
    in_specs, args = [a_spec, b_spec], [a, b]
    if pair:
        assert pair[0].shape == a.shape and pair[1].shape == b.shape
        in_specs += [a_spec, b_spec]
        args += list(pair)
    if res is not None:
        in_specs.append(o_spec)
        args.append(res)
    out_specs, out_shape = o_spec, o_shape
    scratch = [pltpu.VMEM((tm, tn), F32)] if nk > 1 else []
    sem = ("parallel", "parallel", "arbitrary")
    if side:
        hbm = pl.BlockSpec(memory_space=pl.ANY)
        in_specs += [hbm] * n_side_in
        args += side.ins
        out_specs, out_shape = [o_spec] + [hbm] * n_side_out, [out_shape] + side.out_shapes
        scratch += side.sems()
        sem = ("arbitrary",) * 3
    return pl.pallas_call(
        body, name=name, grid=(nj, ni, nk), in_specs=in_specs, out_specs=out_specs, out_shape=out_shape,
        scratch_shapes=scratch, compiler_params=_params(sem),
    )(*args)


def _mm_res_norm(a, b, res, w, *, name, tm):
    m, k = a.shape
    n = b.shape[1]
    tm = min(tm, m)

    def body(a_ref, b_ref, r_ref, w_ref, h_ref, hn_ref):
        h = jnp.dot(a_ref[...], b_ref[...], preferred_element_type=F32) + r_ref[...]
        h_ref[...] = h
        r = lax.rsqrt(jnp.mean(h * h, axis=-1, keepdims=True) + EPS)
        hn_ref[...] = (h * r * w_ref[...]).astype(BF16)

    row = pl.BlockSpec((tm, n), lambda i: (i, 0))
    return pl.pallas_call(
        body, name=name, grid=(m // tm,),
        in_specs=[pl.BlockSpec((tm, k), lambda i: (i, 0)), pl.BlockSpec((k, n), lambda i: (0, 0)), row,
                  pl.BlockSpec((1, n), lambda i: (0, 0))],
        out_specs=[row, row],
        out_shape=[jax.ShapeDtypeStruct((m, n), F32), jax.ShapeDtypeStruct((m, n), BF16)],
        compiler_params=_params(("parallel",)),
    )(a, b, res, w)


def _rms_fwd(x, w, *, name, w_narrow=None):
    t, d = x.shape
    tt = _pick(t, (512, 256))

    def body(*refs):
        x_ref, w_ref, o_ref = refs[0], refs[1], refs[-2 if w_narrow is not None else -1]
        xv = x_ref[...]
        r = lax.rsqrt(jnp.mean(xv * xv, axis=-1, keepdims=True) + EPS)
        u = (xv * r * w_ref[...]).astype(BF16)
        o_ref[...] = u
        if w_narrow is not None:
            refs[-1][...] = jnp.dot(u, refs[2][...], preferred_element_type=F32)

    blk = pl.BlockSpec((tt, d), lambda i: (i, 0))
    in_specs, args = [blk, pl.BlockSpec((1, d), lambda i: (0, 0))], [x, w]
    out_specs, out_shape = [blk], [jax.ShapeDtypeStruct((t, d), BF16)]
    if w_narrow is not None:
        in_specs.append(pl.BlockSpec(w_narrow.shape, lambda i: (0, 0)))
        args.append(w_narrow)
        out_specs.append(pl.BlockSpec((tt, LANES), lambda i: (i, 0)))
        out_shape.append(jax.ShapeDtypeStruct((t, LANES), F32))
    out = pl.pallas_call(
        body, name=name, grid=(t // tt,), in_specs=in_specs, out_specs=out_specs, out_shape=out_shape,
        compiler_params=_params(("parallel",)),
    )(*args)
    return out if w_narrow is not None else out[0]


def _rms_bwd(dn, x, w, dres, *, name, narrow=None):
    t, d = x.shape
    tt = _pick(t, (256,))

    def body(*refs):
        dn_ref, x_ref, w_ref, dres_ref = refs[:4]
        dx_ref, dxb_ref, dw_ref = refs[-3:]
        xv, g = x_ref[...], dn_ref[...].astype(F32)
        if narrow is not None:
            g = g + lax.dot_general(refs[4][...], refs[5][...], (((1,), (1,)), ((), ())), preferred_element_type=F32)
        r = lax.rsqrt(jnp.mean(xv * xv, axis=-1, keepdims=True) + EPS)
        xh = xv * r
        gw = g * w_ref[...]
        dx = dres_ref[...] + r * (gw - xh * jnp.mean(gw * xh, axis=-1, keepdims=True))
        dx_ref[...] = dx
        dxb_ref[...] = dx.astype(BF16)

        @pl.when(pl.program_id(0) == 0)
        def _():
            dw_ref[...] = jnp.zeros_like(dw_ref)

        dw_ref[...] += jnp.sum(g * xh, axis=0, keepdims=True)

    blk = pl.BlockSpec((tt, d), lambda i: (i, 0))
    row = pl.BlockSpec((1, d), lambda i: (0, 0))
    in_specs, args = [blk, blk, row, blk], [dn, x, w, dres]
    if narrow is not None:
        in_specs += [pl.BlockSpec((tt, LANES), lambda i: (i, 0)), pl.BlockSpec(narrow[1].shape, lambda i: (0, 0))]
        args += list(narrow)
    return pl.pallas_call(
        body, name=name, grid=(t // tt,), in_specs=in_specs, out_specs=[blk, blk, row],
        out_shape=[jax.ShapeDtypeStruct((t, d), F32), jax.ShapeDtypeStruct((t, d), BF16),
                   jax.ShapeDtypeStruct((1, d), F32)],
        compiler_params=_params(("arbitrary",)),
    )(*args)


def _loss_head(h2, tgt, wf, *, name):
    t, d = h2.shape
    tt = _pick(t, (256,))

    def body(x_ref, t_ref, w_ref, loss_ref, dx_ref, dxb_ref, dw_ref):
        xv = x_ref[...]
        r = lax.rsqrt(jnp.mean(xv * xv, axis=-1, keepdims=True) + EPS)
        xh = xv * r
        err = xh * w_ref[...] - t_ref[...]
        lpart = 0.5 * jnp.sum(jnp.mean(err * err, axis=-1, keepdims=True), axis=0, keepdims=True)
        dy = err * (1.0 / d)
        gw = dy * w_ref[...]
        dx = r * (gw - xh * jnp.mean(gw * xh, axis=-1, keepdims=True))
        dx_ref[...] = dx
        dxb_ref[...] = dx.astype(BF16)

        @pl.when(pl.program_id(0) == 0)
        def _():
            dw_ref[...] = jnp.zeros_like(dw_ref)
            loss_ref[...] = jnp.zeros_like(loss_ref)

        dw_ref[...] += jnp.sum(dy * xh, axis=0, keepdims=True)
        loss_ref[...] += jnp.broadcast_to(lpart, loss_ref.shape)

    blk = pl.BlockSpec((tt, d), lambda i: (i, 0))
    row = pl.BlockSpec((1, d), lambda i: (0, 0))
    lrow = pl.BlockSpec((1, LANES), lambda i: (0, 0))
    return pl.pallas_call(
        body, name=name, grid=(t // tt,),
        in_specs=[blk, blk, row], out_specs=[lrow, blk, blk, row],
        out_shape=[jax.ShapeDtypeStruct((1, LANES), F32), jax.ShapeDtypeStruct((t, d), F32),
                   jax.ShapeDtypeStruct((t, d), BF16), jax.ShapeDtypeStruct((1, d), F32)],
        compiler_params=_params(("arbitrary",)),
    )(h2, tgt, wf)


def _ffn_in(hn, w_g, w_u, *, name, tm):
    t, d = hn.shape
    tn = w_g.shape[2]
    f = w_g.shape[0] * tn
    tm = min(tm, t)

    def body(a_ref, g_ref, u_ref, gt_ref, up_ref, act_ref):
        a = a_ref[...]
        g = jnp.dot(a, g_ref[...], preferred_element_type=F32)
        u = jnp.dot(a, u_ref[...], preferred_element_type=F32)
        gt_ref[...] = g.astype(BF16)
        up_ref[...] = u.astype(BF16)
        act_ref[...] = (g * _sigmoid(g) * u).astype(BF16)

    wblk = pl.BlockSpec((None, d, tn), lambda j, i: (j, 0, 0))
    oblk = pl.BlockSpec((tm, tn), lambda j, i: (i, j))
    return pl.pallas_call(
        body, name=name, grid=(f // tn, t // tm),
        in_specs=[pl.BlockSpec((tm, d), lambda j, i: (i, 0)), wblk, wblk], out_specs=[oblk] * 3,
        out_shape=[jax.ShapeDtypeStruct((t, f), BF16)] * 3,
        compiler_params=_params(("parallel", "parallel")),
    )(hn, w_g, w_u)


def _ffn_back(dh2b, w_d, gt, up, *, name, tm, tn):
    t, d = dh2b.shape
    f = w_d.shape[0]
    tm = min(tm, t)

    n_sub = 2 if tm % 32 == 0 else 1

    def body(a_ref, w_ref, g_ref, u_ref, dg_ref, du_ref):
        w = w_ref[...]
        subs = [slice(i * (tm // n_sub), (i + 1) * (tm // n_sub)) for i in range(n_sub)]
        das = [lax.dot_general(a_ref[rs, :], w, (((1,), (1,)), ((), ())), preferred_element_type=F32) for rs in subs]
        for rs, da in zip(subs, das):
            g = g_ref[rs, :].astype(F32)
            sg = _sigmoid(g)
            dg_ref[rs, :] = (da * u_ref[rs, :].astype(F32) * sg * (1.0 + g * (1.0 - sg))).astype(BF16)
            du_ref[rs, :] = (da * g * sg).astype(BF16)

    oblk = pl.BlockSpec((tm, tn), lambda j, i: (i, j))
    return pl.pallas_call(
        body, name=name, grid=(f // tn, t // tm),
        in_specs=[pl.BlockSpec((tm, d), lambda j, i: (i, 0)), pl.BlockSpec((tn, d), lambda j, i: (j, 0)), oblk, oblk],
        out_specs=[oblk] * 2,
        out_shape=[jax.ShapeDtypeStruct((t, f), BF16)] * 2,
        compiler_params=_params(("parallel", "parallel")),
    )(dh2b, w_d, gt, up)


CONV_HEAD = 2 * HALO


def _causal_conv(x_ref, xp_ref, cw_ref, xs, tt):
    first = pl.program_id(0) == 0
    xs[0:HALO, :] = jnp.where(first, 0.0, xp_ref[...])
    xs[HALO:HALO + CONV_HEAD, :] = x_ref[0:CONV_HEAD, :]
    xb = x_ref[...].astype(BF16)
    r, c = _iota2(tt)
    moved = [None] + [jnp.dot((c == r - sh).astype(BF16), xb, preferred_element_type=F32)
                      for sh in range(1, CONV_W)]

    def cols(cb):
        cs = slice(cb * LANES, (cb + 1) * LANES)
        c_head = c_main = None
        for j in range(CONV_W):
            sh = CONV_W - 1 - j
            w_j = cw_ref[j:j + 1, cs]
            x_m = x_ref[CONV_HEAD:tt, cs] if sh == 0 else moved[sh][CONV_HEAD:tt, cs]
            x_h = xs[HALO - sh:HALO - sh + CONV_HEAD, cs]
            c_main = x_m * w_j if c_main is None else c_main + x_m * w_j
            c_head = x_h * w_j if c_head is None else c_head + x_h * w_j
        return jnp.concatenate([c_head, c_main], axis=0)

    return cols


def _conv_specs(tt, ch):
    cur = pl.BlockSpec((tt, ch), lambda i: (i, 0))
    prev = pl.BlockSpec((HALO, ch), lambda i: (jnp.maximum(i * (tt // HALO) - 1, 0), 0))
    return cur, prev


def _conv_fwd(proj, conv_w, *, name):
    t = proj.shape[0]
    ch = conv_w.shape[1]
    nqk = HEADS * DK
    tt = _pick(t, (256,))
    cur, prev = _conv_specs(tt, ch)

    def body(x_ref, xp_ref, cw_ref, q_ref, k_ref, v_ref, xs):
        conv = _causal_conv(x_ref, xp_ref, cw_ref, xs, tt)
        for cb in range(ch // LANES):
            c = conv(cb)
            s = c * _sigmoid(c)
            if cb < 2 * HEADS:
                s = s * lax.rsqrt(jnp.sum(s * s, axis=-1, keepdims=True) + EPS)
                if cb < HEADS:
                    q_ref[:, cb * LANES:(cb + 1) * LANES] = (s * (DK ** -0.5)).astype(BF16)
                else:
                    k_ref[:, (cb - HEADS) * LANES:(cb - HEADS + 1) * LANES] = s.astype(BF16)
            else:
                v_ref[:, (cb - 2 * HEADS) * LANES:(cb - 2 * HEADS + 1) * LANES] = s.astype(BF16)

    return pl.pallas_call(
        body, name=name, grid=(t // tt,),
        in_specs=[cur, prev, pl.BlockSpec((CONV_W, ch), lambda i: (0, 0))],
        out_specs=[pl.BlockSpec((tt, nqk), lambda i: (i, 0)), pl.BlockSpec((tt, nqk), lambda i: (i, 0)),
                   pl.BlockSpec((tt, ch - 2 * nqk), lambda i: (i, 0))],
        out_shape=[jax.ShapeDtypeStruct((t, nqk), BF16), jax.ShapeDtypeStruct((t, nqk), BF16),
                   jax.ShapeDtypeStruct((t, ch - 2 * nqk), BF16)],
        scratch_shapes=[pltpu.VMEM((HALO + CONV_HEAD, ch), F32)],
        compiler_params=_params(("arbitrary",)),
    )(proj, proj, conv_w)


def _conv_bwd_pre(proj, conv_w, dq, dk, dv, *, name):
    t = proj.shape[0]
    ch = conv_w.shape[1]
    nqk = HEADS * DK
    tt = _pick(t, (256,))
    cur, prev = _conv_specs(tt, ch)

    def body(x_ref, xp_ref, cw_ref, dq_ref, dk_ref, dv_ref, dc_ref, xs):
        conv = _causal_conv(x_ref, xp_ref, cw_ref, xs, tt)
        for cb in range(ch // LANES):
            c = conv(cb)
            sg = _sigmoid(c)
            s = c * sg
            if cb < 2 * HEADS:
                if cb < HEADS:
                    d, scale = dq_ref[:, cb * LANES:(cb + 1) * LANES], DK ** -0.5
                else:
                    d, scale = dk_ref[:, (cb - HEADS) * LANES:(cb - HEADS + 1) * LANES], 1.0
                rinv = lax.rsqrt(jnp.sum(s * s, axis=-1, keepdims=True) + EPS)
                ds = scale * rinv * (d - s * (rinv * rinv) * jnp.sum(d * s, axis=-1, keepdims=True))
            else:
                ds = dv_ref[:, (cb - 2 * HEADS) * LANES:(cb - 2 * HEADS + 1) * LANES]
            dc_ref[:, cb * LANES:(cb + 1) * LANES] = ds * sg * (1.0 + c * (1.0 - sg))

    return pl.pallas_call(
        body, name=name, grid=(t // tt,),
        in_specs=[cur, prev, pl.BlockSpec((CONV_W, ch), lambda i: (0, 0)),
                  pl.BlockSpec((tt, nqk), lambda i: (i, 0)), pl.BlockSpec((tt, nqk), lambda i: (i, 0)),
                  pl.BlockSpec((tt, ch - 2 * nqk), lambda i: (i, 0))],
        out_specs=pl.BlockSpec((tt, ch), lambda i: (i, 0)),
        out_shape=jax.ShapeDtypeStruct((t, ch), F32),
        scratch_shapes=[pltpu.VMEM((HALO + CONV_HEAD, ch), F32)],
        compiler_params=_params(("arbitrary",)),
    )(proj, proj, conv_w, dq, dk, dv)


def _conv_bwd(proj, dc, conv_w, dproj, *, name):
    t = proj.shape[0]
    ch = conv_w.shape[1]
    tt = _pick(t, (256,))
    nt = t // tt
    cur = pl.BlockSpec((tt, ch), lambda i: (i, 0))
    nxt = pl.BlockSpec((HALO, ch), lambda i: (jnp.minimum((i + 1) * (tt // HALO), t // HALO - 1), 0))

    tail = 2 * HALO
    main = tt - tail

    def body(x_ref, d_ref, dn_ref, cw_ref, alias_ref, dx_ref, dw_ref, ds):
        last = pl.program_id(0) == nt - 1
        ds[0:tail, :] = d_ref[main:tt, :]
        ds[tail:tail + HALO, :] = jnp.where(last, 0.0, dn_ref[...])

        @pl.when(pl.program_id(0) == 0)
        def _():
            dw_ref[...] = jnp.zeros_like(dw_ref)

        dcb = d_ref[...].astype(BF16)
        r, c = _iota2(tt)
        moved = [None] + [jnp.dot((c == r + sh).astype(BF16), dcb, preferred_element_type=F32)
                          for sh in range(1, CONV_W)]
        for cb in range(ch // LANES):
            cols = slice(cb * LANES, (cb + 1) * LANES)
            x_m, x_t = x_ref[0:main, cols], x_ref[main:tt, cols]
            acc_m = acc_t = None
            for j in range(CONV_W):
                sh = CONV_W - 1 - j
                w_j = cw_ref[j:j + 1, cols]
                dj_m = d_ref[0:main, cols] if sh == 0 else moved[sh][0:main, cols]
                dj_t = ds[sh:sh + tail, cols]
                acc_m = dj_m * w_j if acc_m is None else acc_m + dj_m * w_j
                acc_t = dj_t * w_j if acc_t is None else acc_t + dj_t * w_j
                dw_ref[j:j + 1, cols] += (jnp.sum(dj_m * x_m, axis=0, keepdims=True)
                                          + jnp.sum(dj_t * x_t, axis=0, keepdims=True))
            dx_ref[0:main, cols] = acc_m.astype(BF16)
            dx_ref[main:tt, cols] = acc_t.astype(BF16)

    return pl.pallas_call(
        body, name=name, grid=(nt,),
        in_specs=[cur, cur, nxt, pl.BlockSpec((CONV_W, ch), lambda i: (0, 0)), pl.BlockSpec(memory_space=pl.ANY)],
        out_specs=[pl.BlockSpec((tt, ch), lambda i: (i, 0)), pl.BlockSpec((HALO, ch), lambda i: (0, 0))],
        out_shape=[jax.ShapeDtypeStruct(dproj.shape, BF16), jax.ShapeDtypeStruct((HALO, ch), F32)],
        scratch_shapes=[pltpu.VMEM((tail + HALO, ch), F32)],
        input_output_aliases={4: 0},
        compiler_params=_params(("arbitrary",)),
    )(proj, dc, dc, conv_w, dproj)


def _bg_fwd(projs, alog_row, dtb_row, *, name):
    t = projs.shape[0]
    tt = _pick(t, (512, 256))

    def body(p_ref, al_ref, db_ref, o_ref):
        p = p_ref[...]
        lane = lax.broadcasted_iota(jnp.int32, p.shape, 1)
        z = p + db_ref[...]
        sp = jnp.maximum(z, 0.0) + jnp.log1p(jnp.exp(-jnp.abs(z)))
        g = -jnp.exp(al_ref[...]) * sp
        o_ref[...] = jnp.where(lane < HEADS, _sigmoid(p), jnp.where(lane < 2 * HEADS, g, 0.0))

    blk = pl.BlockSpec((tt, LANES), lambda i: (i, 0))
    row = pl.BlockSpec((1, LANES), lambda i: (0, 0))
    return pl.pallas_call(
        body, name=name, grid=(t // tt,), in_specs=[blk, row, row], out_specs=blk,
        out_shape=jax.ShapeDtypeStruct((t, LANES), F32), compiler_params=_params(("parallel",)),
    )(projs, alog_row, dtb_row)


def _bg_bwd(projs, dbg, alog_row, dtb_row, *, name):
    t = projs.shape[0]
    tt = _pick(t, (512, 256))

    def body(p_ref, d_ref, al_ref, db_ref, o_ref, dal_ref, ddb_ref):
        p, d = p_ref[...], d_ref[...]
        lane = lax.broadcasted_iota(jnp.int32, p.shape, 1)
        isg = (lane >= HEADS) & (lane < 2 * HEADS)
        be = _sigmoid(p)
        z = p + db_ref[...]
        sp = jnp.maximum(z, 0.0) + jnp.log1p(jnp.exp(-jnp.abs(z)))
        ea = jnp.exp(al_ref[...])
        d_aa = jnp.where(isg, d * (-ea) * _sigmoid(z), 0.0)
        o_ref[...] = jnp.where(lane < HEADS, d * be * (1.0 - be), d_aa).astype(BF16)

        @pl.when(pl.program_id(0) == 0)
        def _():
            dal_ref[...] = jnp.zeros_like(dal_ref)
            ddb_ref[...] = jnp.zeros_like(ddb_ref)

        dal_ref[...] += jnp.sum(jnp.where(isg, d * (-ea) * sp, 0.0), axis=0, keepdims=True)
        ddb_ref[...] += jnp.sum(d_aa, axis=0, keepdims=True)

    blk = pl.BlockSpec((tt, LANES), lambda i: (i, 0))
    row = pl.BlockSpec((1, LANES), lambda i: (0, 0))
    return pl.pallas_call(
        body, name=name, grid=(t // tt,), in_specs=[blk, blk, row, row], out_specs=[blk, row, row],
        out_shape=[jax.ShapeDtypeStruct((t, LANES), BF16), jax.ShapeDtypeStruct((1, LANES), F32),
                   jax.ShapeDtypeStruct((1, LANES), F32)],
        compiler_params=_params(("arbitrary",)),
    )(projs, dbg, alog_row, dtb_row)


def _hn_specs(n_chunks, rev=False):
    def nn(n):
        return n_chunks - 1 - n if rev else n
    tok = lambda w: pl.BlockSpec((CHUNK, HB * w), lambda g, n: (nn(n), g))
    per = lambda a, b: pl.BlockSpec((HB, None, a, b), lambda g, n: (g, nn(n), 0, 0))
    return tok, per


def _hcols(hh, w):
    return slice(hh * w, (hh + 1) * w)


def _decay(gc_col, gc_row):
    r, c = _iota2(CHUNK)
    d = jnp.exp(jnp.minimum(gc_col - gc_row, 0.0))
    return jnp.where(r >= c, d, 0.0), jnp.where(r > c, d, 0.0)


def _gdn_prep(q, k, v, beta_r, g_r, *, name):
    t = q.shape[0]
    n_chunks = t // CHUNK
    cps = 2 if n_chunks % 2 == 0 else 1
    tok = lambda w: pl.BlockSpec((cps * CHUNK, HB * w), lambda g, n: (n, g))
    per = lambda a, b: pl.BlockSpec((HB, cps, a, b), lambda g, n: (g, n, 0, 0))

    def body(q_ref, k_ref, v_ref, b_ref, g_ref, gc_ref, ti_ref, u_ref, w_ref, p_ref):
        r, c = _iota2(CHUNK)
        its = [(h, cc) for cc in range(cps) for h in range(HB)]
        rows = lambda cc: slice(cc * CHUNK, (cc + 1) * CHUNK)
        n = range(len(its))
        kk = [k_ref[rows(cc), _hcols(h, DK)] for h, cc in its]
        gc_col = [jnp.sum(jnp.where(c <= r, jnp.broadcast_to(g_ref[h, cc], (CHUNK, CHUNK)), 0.0), axis=1,
                          keepdims=True) for h, cc in its]
        gc_row = [_col_to_row(gc_col[i]) for i in n]
        beta_col = [_row_to_col(b_ref[h, cc]) for h, cc in its]
        dec = [_decay(gc_col[i], gc_row[i]) for i in n]
        kb = [kk[i] * beta_col[i] for i in n]
        for i, (h, cc) in enumerate(its):
            gc_ref[h, cc] = gc_row[i]
            p_ref[h, cc] = (_dot_nt(q_ref[rows(cc), _hcols(h, DK)], kk[i]) * dec[i][0]).astype(BF16)
        pw = [_dot_nt(kb[i], kk[i]) * dec[i][1] for i in n]
        y = [-pw[i] for i in n]
        for _ in range(5):
            pw = [_dot(pw[i], pw[i]) for i in n]
            yp = [_dot(y[i], pw[i]) for i in n]
            y = [y[i] + pw[i] + yp[i] for i in n]
        vb = [v_ref[rows(cc), _hcols(h, DV)] * beta_col[i] for i, (h, cc) in enumerate(its)]
        kbe = [kb[i] * jnp.exp(gc_col[i]) for i in n]
        yv = [_dot(y[i], vb[i]) for i in n]
        yk = [_dot(y[i], kbe[i]) for i in n]
        for i, (h, cc) in enumerate(its):
            ti_ref[h, cc] = y[i].astype(BF16)
            u_ref[rows(cc), _hcols(h, DV)] = vb[i] + yv[i]
            w_ref[rows(cc), _hcols(h, DK)] = (kbe[i] + yk[i]).astype(BF16)

    return pl.pallas_call(
        body, name=name, grid=(HEADS // HB, n_chunks // cps),
        in_specs=[tok(DK), tok(DK), tok(DV), per(1, CHUNK), per(1, CHUNK)],
        out_specs=[per(1, CHUNK), per(CHUNK, CHUNK), tok(DV), tok(DK), per(CHUNK, CHUNK)],
        out_shape=[jax.ShapeDtypeStruct((HEADS, n_chunks, 1, CHUNK), F32),
                   jax.ShapeDtypeStruct((HEADS, n_chunks, CHUNK, CHUNK), BF16),
                   jax.ShapeDtypeStruct((t, HEADS * DV), F32), jax.ShapeDtypeStruct((t, HEADS * DK), BF16),
                   jax.ShapeDtypeStruct((HEADS, n_chunks, CHUNK, CHUNK), BF16)],
        compiler_params=_params(("parallel", "parallel")),
    )(q, k, v, beta_r, g_r)


def _interleave(*step_gens):
    live = list(step_gens)
    while live:
        for g in list(live):
            try:
                next(g)
            except StopIteration:
                live.remove(g)


def _gdn_scan_steps(q_ref, k_ref, u_ref, w_ref, p_ref, gc_ref, o_ref, s_ref, st):
    hs = range(HB)
    s = [st[h] for h in hs]
    gc_row = [gc_ref[h] for h in hs]
    gc_col = [_row_to_col(gc_row[h]) for h in hs]
    glast = [gc_row[h][:, CHUNK - 1:CHUNK] for h in hs]
    for h in hs:
        s_ref[h] = s[h].astype(BF16)
    yield
    ws = [_dot(w_ref[:, _hcols(h, DK)], s[h]) for h in hs]
    yield
    qs = [_dot(q_ref[:, _hcols(h, DK)] * jnp.exp(gc_col[h]), s[h]) for h in hs]
    vn = [u_ref[:, _hcols(h, DV)] - ws[h] for h in hs]
    yield
    pv = [_dot(p_ref[h], vn[h]) for h in hs]
    yield
    kv = [_dot_tn(k_ref[:, _hcols(h, DK)] * jnp.exp(glast[h] - gc_col[h]), vn[h]) for h in hs]
    yield
    for h in hs:
        o_ref[:, _hcols(h, DV)] = qs[h] + pv[h]
        st[h] = s[h] * jnp.exp(glast[h]) + kv[h]


def _gdn_bwd_steps(q_ref, k_ref, v_ref, b_ref, gc_ref, ti_ref, u_ref, w_ref, s_ref, do_ref,
                   dq_ref, dk_ref, dv_ref, db_ref, dg_ref, dst):
    r, c = _iota2(CHUNK)
    rows = lax.broadcasted_iota(jnp.int32, (CHUNK, 1), 0)
    rsum = lambda a: jnp.sum(a, axis=1, keepdims=True)
    hs = range(HB)
    qq = [q_ref[:, _hcols(h, DK)] for h in hs]
    kk = [k_ref[:, _hcols(h, DK)] for h in hs]
    ww = [w_ref[:, _hcols(h, DK)] for h in hs]
    uu = [u_ref[:, _hcols(h, DV)] for h in hs]
    d_o = [do_ref[:, _hcols(h, DV)] for h in hs]
    s = [s_ref[h].astype(F32) for h in hs]
    d_s = [dst[h] for h in hs]
    gc_row = [gc_ref[h] for h in hs]
    gc_col = [_row_to_col(gc_row[h]) for h in hs]
    beta_col = [_row_to_col(b_ref[h]) for h in hs]
    eg = [jnp.exp(gc_col[h]) for h in hs]
    glast = [gc_row[h][:, CHUNK - 1:CHUNK] for h in hs]
    eglast = [jnp.exp(glast[h]) for h in hs]
    e2 = [jnp.exp(glast[h] - gc_col[h]) for h in hs]
    dec = [_decay(gc_col[h], gc_row[h]) for h in hs]
    kb = [kk[h] * beta_col[h] for h in hs]
    ke = [kk[h] * e2[h] for h in hs]
    qe = [qq[h] * eg[h] for h in hs]
    yield
    kkt = [_dot_nt(kb[h], kk[h]) for h in hs]
    qk = [_dot_nt(qq[h], kk[h]) for h in hs]
    ws = [_dot(ww[h], s[h]) for h in hs]
    yield
    a = [kkt[h] * dec[h][1] for h in hs]
    pp = [qk[h] * dec[h][0] for h in hs]
    vn = [uu[h] - ws[h] for h in hs]
    t1 = [_dot_tn(pp[h], d_o[h]) for h in hs]
    t2 = [_dot(ke[h], d_s[h]) for h in hs]
    yield
    dqe = [_dot_nt(d_o[h], s[h]) for h in hs]
    dke = [_dot_nt(vn[h], d_s[h]) for h in hs]
    dqk = [_dot_nt(d_o[h], vn[h]) * dec[h][0] for h in hs]
    yield
    dvn = [t1[h] + t2[h] for h in hs]
    t3 = [_dot_tn(qe[h], d_o[h]) for h in hs]
    t4 = [_dot_tn(ww[h], dvn[h]) for h in hs]
    dw = [-_dot_nt(dvn[h], s[h]) for h in hs]
    yield
    de2 = [rsum(dke[h] * ke[h]) for h in hs]
    dglast = [jnp.sum(de2[h], axis=0, keepdims=True)
              + eglast[h] * jnp.sum(rsum(d_s[h] * s[h]), axis=0, keepdims=True) for h in hs]
    for h in hs:
        dst[h] = d_s[h] * eglast[h] + t3[h] - t4[h]
    yy = [ti_ref[h] for h in hs]
    t5 = [_dot_tn(yy[h], dvn[h]) for h in hs]
    t6 = [_dot_tn(yy[h], dw[h]) for h in hs]
    yield
    dvb = [dvn[h] + t5[h] for h in hs]
    dkbe = [dw[h] + t6[h] for h in hs]
    t7 = [_dot_nt(dvb[h], uu[h]) for h in hs]
    t8 = [_dot_nt(dkbe[h], ww[h]) for h in hs]
    yield
    d_a = [jnp.where(r > c, -(t7[h] + t8[h]), 0.0) for h in hs]
    dm = [d_a[h] * dec[h][1] for h in hs]
    t9 = [_dot(dm[h], kk[h]) for h in hs]
    t10 = [_dot(dqk[h], kk[h]) for h in hs]
    yield
    t11 = [_dot_tn(dqk[h], qq[h]) for h in hs]
    t12 = [_dot_tn(dm[h], kb[h]) for h in hs]
    yield
    for h in hs:
        dkb = t9[h] + dkbe[h] * eg[h]
        e_mat = d_a[h] * a[h] + dqk[h] * qk[h]
        dgc = (rsum(dqe[h] * qe[h]) - de2[h] + rsum(dkbe[h] * kb[h] * eg[h]) + rsum(e_mat)
               - _row_to_col(jnp.sum(e_mat, axis=0, keepdims=True)))
        dgc = dgc + jnp.where(rows == CHUNK - 1, dglast[h], 0.0)
        dq_ref[:, _hcols(h, DK)] = dqe[h] * eg[h] + t10[h]
        dk_ref[:, _hcols(h, DK)] = t11[h] + dke[h] * e2[h] + t12[h] + dkb * beta_col[h]
        dv_ref[:, _hcols(h, DV)] = dvb[h] * beta_col[h]
        dbeta = rsum(dkb * kk[h]) + rsum(dvb[h] * v_ref[:, _hcols(h, DV)])
        db_ref[h] = _col_to_row(dbeta)
        dg_ref[h] = jnp.sum(jnp.where(r >= c, jnp.broadcast_to(dgc, (CHUNK, CHUNK)), 0.0), axis=0, keepdims=True)


def _rot(x, cos2, sin2):
    return x * cos2 + pltpu.roll(x, DK // 2, 1) * sin2


def _unrot(d, cos2, sin2):
    return d * cos2 + pltpu.roll(d * sin2, DK // 2, 1)


def _ret_consts(lg):
    r, c = _iota2(CHUNK)
    dm = jnp.where(r >= c, jnp.exp((r - c).astype(F32) * lg), 0.0)
    pos = lax.broadcasted_iota(jnp.int32, (CHUNK, 1), 0).astype(F32)
    return dm, jnp.exp((pos + 1.0) * lg), jnp.exp((CHUNK - 1.0 - pos) * lg), jnp.exp(CHUNK * lg)


def _ret_fwd_steps(q_ref, k_ref, v_ref, cos_ref, sin_ref, lg_ref, o_ref, s_ref, st):
    cos2v, sin2v = cos_ref[...], sin_ref[...]
    hs = range(HB)
    s = [st[h] for h in hs]
    for h in hs:
        s_ref[h] = s[h].astype(BF16)
    cst = [_ret_consts(lg_ref[h][:, 0:1]) for h in hs]
    qq = [_rot(q_ref[:, _hcols(h, DK)], cos2v, sin2v) for h in hs]
    kk = [_rot(k_ref[:, _hcols(h, DK)], cos2v, sin2v) * (DK ** -0.5) for h in hs]
    vv = [v_ref[:, _hcols(h, DV)] for h in hs]
    yield
    sc = [_dot_nt(qq[h], kk[h]) * cst[h][0] for h in hs]
    yield
    qs = [_dot(qq[h], s[h]) for h in hs]
    yield
    kv = [_dot_tn(kk[h] * cst[h][2], vv[h]) for h in hs]
    yield
    sv = [_dot(sc[h], vv[h]) for h in hs]
    yield
    for h in hs:
        o_ref[:, _hcols(h, DV)] = sv[h] + qs[h] * cst[h][1]
        st[h] = s[h] * cst[h][3] + kv[h]


def _mix_fwd(q, k, u, w, p, gc_r, proj, cos2, sin2, lg_tab, col_q, col_k, col_v, *, name):
    t = q.shape[0]
    n_chunks = t // CHUNK
    tok, per = _hn_specs(n_chunks)
    bq, bk, bv = col_q // (HB * DK), col_k // (HB * DK), col_v // (HB * DV)

    def body(*refs):
        st_a, st_b = refs[-2:]

        @pl.when(pl.program_id(1) == 0)
        def _():
            st_a[...] = jnp.zeros_like(st_a)
            st_b[...] = jnp.zeros_like(st_b)

        _interleave(_gdn_scan_steps(*refs[0:6], refs[12], refs[13], st_a),
                    _ret_fwd_steps(*refs[6:12], refs[14], refs[15], st_b))

    state = jax.ShapeDtypeStruct((HEADS, n_chunks, DK, DV), BF16)
    return pl.pallas_call(
        body, name=name, grid=(HEADS // HB, n_chunks),
        in_specs=[tok(DK), tok(DK), tok(DV), tok(DK), per(CHUNK, CHUNK), per(1, CHUNK),
                  pl.BlockSpec((CHUNK, HB * DK), lambda g, n: (n, bq + g)),
                  pl.BlockSpec((CHUNK, HB * DK), lambda g, n: (n, bk + g)),
                  pl.BlockSpec((CHUNK, HB * DV), lambda g, n: (n, bv + g)),
                  pl.BlockSpec((CHUNK, DK), lambda g, n: (n, 0)), pl.BlockSpec((CHUNK, DK), lambda g, n: (n, 0)),
                  pl.BlockSpec((HB, 1, LANES), lambda g, n: (g, 0, 0))],
        out_specs=[tok(DV), per(DK, DV), tok(DV), per(DK, DV)],
        out_shape=[jax.ShapeDtypeStruct((t, HEADS * DV), F32), state, jax.ShapeDtypeStruct((t, HEADS * DV), F32), state],
        scratch_shapes=[pltpu.VMEM((HB, DK, DV), F32)] * 2,
        compiler_params=_params(("arbitrary", "arbitrary")),
    )(q, k, u, w, p, gc_r, proj, proj, proj, cos2, sin2, lg_tab)


def _ret_bwd_steps(q_ref, k_ref, v_ref, cos_ref, sin_ref, lg_ref, s_ref, do_ref, d_ref, dst):
    nqk = HEADS * DK
    cos2v, sin2v = cos_ref[...], sin_ref[...]
    hs = range(HB)
    s = [s_ref[h].astype(F32) for h in hs]
    d_s = [dst[h] for h in hs]
    d_o = [do_ref[:, _hcols(h, DV)] for h in hs]
    cst = [_ret_consts(lg_ref[h][:, 0:1]) for h in hs]
    qq = [_rot(q_ref[:, _hcols(h, DK)], cos2v, sin2v) for h in hs]
    kk = [_rot(k_ref[:, _hcols(h, DK)], cos2v, sin2v) * (DK ** -0.5) for h in hs]
    vv = [v_ref[:, _hcols(h, DV)] for h in hs]
    dxo = [d_o[h] * cst[h][1] for h in hs]
    yield
    sc = [_dot_nt(qq[h], kk[h]) * cst[h][0] for h in hs]
    dsc = [_dot_nt(d_o[h], vv[h]) * cst[h][0] for h in hs]
    yield
    t1 = [_dot(kk[h] * cst[h][2], d_s[h]) for h in hs]
    t2 = [_dot_nt(dxo[h], s[h]) for h in hs]
    yield
    t3 = [_dot_nt(vv[h], d_s[h]) for h in hs]
    t4 = [_dot_tn(qq[h], dxo[h]) for h in hs]
    yield
    t5 = [_dot_tn(sc[h], d_o[h]) for h in hs]
    t6 = [_dot(dsc[h], kk[h]) for h in hs]
    yield
    t7 = [_dot_tn(dsc[h], qq[h]) for h in hs]
    yield
    for h in hs:
        dst[h] = d_s[h] * cst[h][3] + t4[h]
        d_ref[:, 2 * nqk + h * DV:2 * nqk + (h + 1) * DV] = (t5[h] + t1[h]).astype(BF16)
        d_ref[:, _hcols(h, DK)] = _unrot(t6[h] + t2[h], cos2v, sin2v).astype(BF16)
        d_ref[:, nqk + h * DK:nqk + (h + 1) * DK] = (
            _unrot(t7[h] + t3[h] * cst[h][2], cos2v, sin2v) * (DK ** -0.5)).astype(BF16)


def _mix_bwd(q, k, v, beta_r, gc_r, ti, u, w, s_a, do_a, proj, cos2, sin2, lg_tab, s_b, do_b, dproj,
             col_q, col_k, col_v, *, name):
    t = q.shape[0]
    n_chunks = t // CHUNK
    tok, per = _hn_specs(n_chunks, rev=True)
    bq, bk, bv = col_q // (HB * DK), col_k // (HB * DK), col_v // (HB * DV)
    rv = lambda n: n_chunks - 1 - n
    nqk, wid = HEADS * DK, HEADS * (2 * DK + DV)
    assert HB == HEADS and col_k == col_q + nqk and col_v == col_k + nqk and col_q % wid == 0
    n_a, n_b = 10, 8

    def body(*refs):
        ins_a, ins_b = refs[:n_a], refs[n_a:n_a + n_b]
        outs = refs[n_a + n_b + 1:n_a + n_b + 7]
        dst_a, dst_b = refs[-2:]

        @pl.when(pl.program_id(1) == 0)
        def _():
            dst_a[...] = jnp.zeros_like(dst_a)
            dst_b[...] = jnp.zeros_like(dst_b)

        _interleave(_gdn_bwd_steps(*ins_a, *outs[:5], dst_a), _ret_bwd_steps(*ins_b, outs[5], dst_b))

    rows = jax.ShapeDtypeStruct((HEADS, n_chunks, 1, CHUNK), F32)
    return pl.pallas_call(
        body, name=name, grid=(HEADS // HB, n_chunks),
        in_specs=[tok(DK), tok(DK), tok(DV), per(1, CHUNK), per(1, CHUNK), per(CHUNK, CHUNK), tok(DV), tok(DK),
                  per(DK, DV), tok(DV),
                  pl.BlockSpec((CHUNK, HB * DK), lambda g, n: (rv(n), bq + g)),
                  pl.BlockSpec((CHUNK, HB * DK), lambda g, n: (rv(n), bk + g)),
                  pl.BlockSpec((CHUNK, HB * DV), lambda g, n: (rv(n), bv + g)),
                  pl.BlockSpec((CHUNK, DK), lambda g, n: (rv(n), 0)),
                  pl.BlockSpec((CHUNK, DK), lambda g, n: (rv(n), 0)),
                  pl.BlockSpec((HB, 1, LANES), lambda g, n: (g, 0, 0)),
                  per(DK, DV), tok(DV), pl.BlockSpec(memory_space=pl.ANY)],
        out_specs=[tok(DK), tok(DK), tok(DV), per(1, CHUNK), per(1, CHUNK),
                   pl.BlockSpec((CHUNK, wid), lambda g, n: (rv(n), col_q // wid))],
        out_shape=[jax.ShapeDtypeStruct((t, HEADS * DK), F32), jax.ShapeDtypeStruct((t, HEADS * DK), F32),
                   jax.ShapeDtypeStruct((t, HEADS * DV), F32), rows, rows, jax.ShapeDtypeStruct(dproj.shape, BF16)],
        scratch_shapes=[pltpu.VMEM((HB, DK, DV), F32)] * 2,
        input_output_aliases={n_a + n_b: 5},
        compiler_params=_params(("arbitrary", "arbitrary")),
    )(q, k, v, beta_r, gc_r, ti, u, w, s_a, do_a, proj, proj, proj, cos2, sin2, lg_tab, s_b, do_b, dproj)


def _merge_parts(oa, ob, z, rg, ga, gb):
    ra = lax.rsqrt(jnp.mean(oa * oa, axis=-1, keepdims=True) + EPS)
    xa = oa * ra
    mu = jnp.mean(ob, axis=-1, keepdims=True)
    cen = ob - mu
    rb = lax.rsqrt(jnp.mean(cen * cen, axis=-1, keepdims=True) + EPS)
    xb = cen * rb
    sz, sr = _sigmoid(z), _sigmoid(rg)
    return ra, xa, rb, xb, sz, sr, _sigmoid(ga), _sigmoid(gb)


def _merge_specs(tt, d, cz, crg, cga, cgb):
    blk = pl.BlockSpec((tt, d), lambda i: (i, 0))
    pcol = lambda c: pl.BlockSpec((tt, d), lambda i: (i, c // d))
    row = pl.BlockSpec((1, d), lambda i: (0, 0))
    return blk, [blk, blk, pcol(cz), pcol(crg), pcol(cga), pcol(cgb), row, row], row


def _merge_fwd(oa, ob, proj, wa, wb, cz, crg, cga, cgb, *, name):
    t, d = oa.shape
    tt = _pick(t, (256,))
    blk, in_specs, _ = _merge_specs(tt, d, cz, crg, cga, cgb)

    def body(oa_ref, ob_ref, z_ref, rg_ref, ga_ref, gb_ref, wa_ref, wb_ref, o_ref):
        for h in range(HEADS):
            cols = slice(h * DV, (h + 1) * DV)
            z, rg = z_ref[:, cols], rg_ref[:, cols]
            _, xa, _, xb, sz, sr, sga, sgb = _merge_parts(
                oa_ref[:, cols], ob_ref[:, cols], z, rg, ga_ref[:, cols], gb_ref[:, cols])
            o_a = xa * wa_ref[:, cols] * (z * sz)
            o_b = xb * wb_ref[:, cols] * (rg * sr)
            o_ref[:, cols] = (sga * o_a + sgb * o_b).astype(BF16)

    return pl.pallas_call(
        body, name=name, grid=(t // tt,), in_specs=in_specs, out_specs=blk,
        out_shape=jax.ShapeDtypeStruct((t, d), BF16), compiler_params=_params(("parallel",)),
    )(oa, ob, proj, proj, proj, proj, wa, wb)


def _merge_bwd(dmix, oa, ob, proj, wa, wb, cz, crg, cga, cgb, *, name, side):
    t, d = oa.shape
    tt = _pick(t, (256,))
    nt = t // tt
    blk, in_specs, row = _merge_specs(tt, d, cz, crg, cga, cgb)
    assert (crg, cga, cgb) == (cz + d, cz + 2 * d, cz + 3 * d) and cz % (4 * d) == 0
    n_si, n_so = len(side.ins), len(side.out_shapes)
    assert len(side.phases) == 2

    def body(*refs):
        dm_ref, oa_ref, ob_ref, z_ref, rg_ref, ga_ref, gb_ref, wa_ref, wb_ref = refs[:9]
        s_in = refs[9:9 + n_si]
        doa_ref, dob_ref, dp_ref, dwa_ref, dwb_ref = refs[9 + n_si:14 + n_si]
        s_out, ssem, rsem = refs[14 + n_si:14 + n_si + n_so], refs[-2], refs[-1]
        dz_ref, drg_ref, dga_ref, dgb_ref = [dp_ref.at[:, i * d:(i + 1) * d] for i in range(4)]

        @pl.when(pl.program_id(0) == 0)
        def _():
            side.phases[0](s_in, s_out, ssem, rsem)
            dwa_ref[...] = jnp.zeros_like(dwa_ref)
            dwb_ref[...] = jnp.zeros_like(dwb_ref)

        for h in range(HEADS):
            cols = slice(h * DV, (h + 1) * DV)
            z, rg, wa_h, wb_h, dmx = z_ref[:, cols], rg_ref[:, cols], wa_ref[:, cols], wb_ref[:, cols], dm_ref[:, cols]
            ra, xa, rb, xb, sz, sr, sga, sgb = _merge_parts(
                oa_ref[:, cols], ob_ref[:, cols], z, rg, ga_ref[:, cols], gb_ref[:, cols])
            na, nb = xa * wa_h, xb * wb_h
            sil_z, sil_r = z * sz, rg * sr
            o_a, o_b = na * sil_z, nb * sil_r
            dga_ref[:, cols] = (dmx * o_a * sga * (1.0 - sga)).astype(BF16)
            dgb_ref[:, cols] = (dmx * o_b * sgb * (1.0 - sgb)).astype(BF16)
            d_oa, d_ob = dmx * sga, dmx * sgb
            dz_ref[:, cols] = (d_oa * na * sz * (1.0 + z * (1.0 - sz))).astype(BF16)
            drg_ref[:, cols] = (d_ob * nb * sr * (1.0 + rg * (1.0 - sr))).astype(BF16)
            dna, dnb = d_oa * sil_z, d_ob * sil_r
            dwa_ref[:, cols] += jnp.sum(dna * xa, axis=0, keepdims=True)
            dwb_ref[:, cols] += jnp.sum(dnb * xb, axis=0, keepdims=True)
            gwa, gwb = dna * wa_h, dnb * wb_h
            doa_ref[:, cols] = ra * (gwa - xa * jnp.mean(gwa * xa, axis=-1, keepdims=True))
            dob_ref[:, cols] = rb * (gwb - jnp.mean(gwb, axis=-1, keepdims=True)
                                     - xb * jnp.mean(gwb * xb, axis=-1, keepdims=True))

        @pl.when(pl.program_id(0) == nt - 1)
        def _():
            side.phases[1](s_in, s_out, ssem, rsem)

    hbm = pl.BlockSpec(memory_space=pl.ANY)
    return pl.pallas_call(
        body, name=name, grid=(nt,), in_specs=[blk] + in_specs + [hbm] * n_si,
        out_specs=[blk, blk, pl.BlockSpec((tt, 4 * d), lambda i: (i, cz // (4 * d))), row, row] + [hbm] * n_so,
        out_shape=[jax.ShapeDtypeStruct((t, d), F32)] * 2 + [jax.ShapeDtypeStruct(proj.shape, BF16)]
        + [jax.ShapeDtypeStruct((1, d), F32)] * 2 + side.out_shapes,
        scratch_shapes=side.sems(),
        compiler_params=_params(("arbitrary",)),
    )(dmix, oa, ob, proj, proj, proj, proj, wa, wb, *side.ins)


def _row_block(rows, cols, itemsize=4, target=1 << 20):
    for rb in (512, 256, 128, 64, 32, 16, 8):
        if rows % rb == 0 and rb * cols * itemsize <= target:
            return rb
    return rows


def _adamw(w, g, m, v, *, name):
    rows, cols = w.shape
    rb = _row_block(rows, cols)

    def body(w_ref, g_ref, m_ref, v_ref, d_ref, nm_ref, nv_ref):
        gg = g_ref[...]
        mm = ADAM_B1 * m_ref[...] + (1.0 - ADAM_B1) * gg
        vv = ADAM_B2 * v_ref[...] + (1.0 - ADAM_B2) * (gg * gg)
        m_hat = mm / (1.0 - ADAM_B1 ** ADAM_STEP)
        v_hat = vv / (1.0 - ADAM_B2 ** ADAM_STEP)
        d_ref[...] = -ADAM_LR * (m_hat / (jnp.sqrt(v_hat) + ADAM_EPS) + ADAM_WD * w_ref[...])
        nm_ref[...] = mm
        nv_ref[...] = vv

    blk = pl.BlockSpec((rb, cols), lambda i: (i, 0))
    return pl.pallas_call(
        body, name=name, grid=(rows // rb,), in_specs=[blk] * 4, out_specs=[blk] * 3,
        out_shape=[jax.ShapeDtypeStruct((rows, cols), F32)] * 3, compiler_params=_params(("parallel",)),
    )(w, g, m, v)


def _adamw_halves(w, mine, other, c_idx, m, v, *, name):
    rows, cols = w.shape
    hr = rows // 2
    rb = _row_block(hr, cols, 4, 2 << 20)
    nb = hr // rb

    def body(c_ref, w_ref, a_ref, b_ref, m_ref, v_ref, g_ref, d_ref, nm_ref, nv_ref):
        gg = jnp.where(pl.program_id(0) // nb == c_ref[0], a_ref[...], b_ref[...])
        mm = ADAM_B1 * m_ref[...] + (1.0 - ADAM_B1) * gg
        vv = ADAM_B2 * v_ref[...] + (1.0 - ADAM_B2) * (gg * gg)
        m_hat = mm / (1.0 - ADAM_B1 ** ADAM_STEP)
        v_hat = vv / (1.0 - ADAM_B2 ** ADAM_STEP)
        g_ref[...] = gg
        d_ref[...] = -ADAM_LR * (m_hat / (jnp.sqrt(v_hat) + ADAM_EPS) + ADAM_WD * w_ref[...])
        nm_ref[...] = mm
        nv_ref[...] = vv

    blk = pl.BlockSpec((rb, cols), lambda i, c: (i, 0))
    half = pl.BlockSpec((rb, cols), lambda i, c: (i % nb, 0))
    return pl.pallas_call(
        body, name=name,
        grid_spec=pltpu.PrefetchScalarGridSpec(
            num_scalar_prefetch=1, grid=(rows // rb,), in_specs=[blk, half, half, blk, blk], out_specs=[blk] * 4),
        out_shape=[jax.ShapeDtypeStruct((rows, cols), F32)] * 4, compiler_params=_params(("parallel",)),
    )(c_idx, w, mine, other, m, v)


def _pair_add(g, rsib, c_idx, *, name):
    _, _, hr, cols = g.shape
    rb = _row_block(hr, cols, 2, 4 << 20)

    def body(c_ref, g_ref, r_ref, o_ref):
        o_ref[...] = (g_ref[...].astype(F32) + r_ref[...].astype(F32)).astype(BF16)

    return pl.pallas_call(
        body, name=name,
        grid_spec=pltpu.PrefetchScalarGridSpec(
            num_scalar_prefetch=1, grid=(N_CHIPS, hr // rb),
            in_specs=[pl.BlockSpec((None, None, rb, cols), lambda j, i, c: (j, c[0], i, 0)),
                      pl.BlockSpec((None, rb, cols), lambda j, i, c: (j, i, 0))],
            out_specs=pl.BlockSpec((None, rb, cols), lambda j, i, c: (j, i, 0))),
        out_shape=jax.ShapeDtypeStruct((N_CHIPS, hr, cols), BF16),
        compiler_params=_params(("parallel", "parallel")),
    )(c_idx, g, rsib)


def _chip_sum(qb, *, name):
    _, hr, cols = qb.shape
    rb = _row_block(hr, cols, 2, 1 << 20)

    def body(q_ref, o_ref):
        acc = q_ref[0].astype(F32)
        for i in range(1, N_CHIPS):
            acc = acc + q_ref[i].astype(F32)
        o_ref[...] = acc

    return pl.pallas_call(
        body, name=name, grid=(hr // rb,),
        in_specs=[pl.BlockSpec((N_CHIPS, rb, cols), lambda i: (0, i, 0))],
        out_specs=pl.BlockSpec((rb, cols), lambda i: (i, 0)),
        out_shape=jax.ShapeDtypeStruct((hr, cols), F32), compiler_params=_params(("parallel",)),
    )(qb)


def _place():
    x, y, c = lax.axis_index("x"), lax.axis_index("y"), lax.axis_index("c")
    return x, y, c, [(1 - x, y), (x, 1 - y), (1 - x, 1 - y)]


ANY = pl.BlockSpec(memory_space=pl.ANY)


def _gather_side(shards):
    nw, n_k = len(shards), 8

    def plan(ins, outs, ssem, rsem):
        x, y, c, _ = _place()
        nbr_x, nbr_y, sib = (1 - x, y, c), (x, 1 - y, c), (x, y, 1 - c)
        s_me, s_x, s_y, s_d = 2 * x + y, 2 * (1 - x) + y, 2 * x + 1 - y, 2 * (1 - x) + 1 - y

        def rows(w, half, quarter=None):
            hr = shards[w].shape[0] // 2
            if quarter is None:
                return pl.ds(pl.multiple_of(half * hr, 16), hr)
            return pl.ds(pl.multiple_of(half * hr + quarter * (hr // 2), 16), hr // 2)

        def rcopy(w, k, src, slot, rws, to):
            return pltpu.make_async_remote_copy(
                src_ref=src, dst_ref=outs[w].at[slot, rws], send_sem=ssem.at[w * n_k + k],
                recv_sem=rsem.at[w * n_k + k], device_id=to, device_id_type=MESH)

        out = []
        for w in range(nw):
            mine = ins[w].at[rows(w, c)]
            q = [rows(w, c, 0), rows(w, c, 1)]
            p = functools.partial
            out.append([
                (p(rcopy, w, 0, mine, s_me, rows(w, c), nbr_x), p(rcopy, w, 0, mine, s_x, rows(w, c), nbr_x)),
                (p(rcopy, w, 1, mine, s_me, rows(w, c), nbr_y), p(rcopy, w, 1, mine, s_y, rows(w, c), nbr_y)),
                (p(rcopy, w, 2, outs[w].at[s_x, q[0]], s_x, q[0], nbr_y),
                 p(rcopy, w, 2, outs[w].at[s_x, q[0]], s_d, q[0], nbr_y)),
                (p(rcopy, w, 3, outs[w].at[s_y, q[1]], s_y, q[1], nbr_x),
                 p(rcopy, w, 3, outs[w].at[s_y, q[1]], s_d, q[1], nbr_x)),
                (p(rcopy, w, 4, outs[w].at[s_x, rows(w, c)], s_x, rows(w, c), sib),
                 p(rcopy, w, 4, mine, s_x, rows(w, 1 - c), sib)),
                (p(rcopy, w, 5, outs[w].at[s_y, rows(w, c)], s_y, rows(w, c), sib),
                 p(rcopy, w, 5, mine, s_y, rows(w, 1 - c), sib)),
                (p(rcopy, w, 6, outs[w].at[s_d, q[0]], s_d, q[0], sib),
                 p(rcopy, w, 6, outs[w].at[s_d, q[0]], s_d, rows(w, 1 - c, 0), sib)),
                (p(rcopy, w, 7, outs[w].at[s_d, q[1]], s_d, q[1], sib),
                 p(rcopy, w, 7, outs[w].at[s_d, q[1]], s_d, rows(w, 1 - c, 1), sib)),
            ])
        return out

    def send_own(*refs):
        for cps in plan(*refs):
            cps[0][0]().start()
            cps[1][0]().start()

    def relay(*refs):
        for cps in plan(*refs):
            cps[0][1]().wait_recv()
            cps[2][0]().start()
            cps[4][0]().start()
            cps[1][1]().wait_recv()
            cps[3][0]().start()
            cps[5][0]().start()

    def pass_diagonal(*refs):
        for cps in plan(*refs):
            cps[2][1]().wait_recv()
            cps[6][0]().start()
            cps[3][1]().wait_recv()
            cps[7][0]().start()

    def finish(*refs):
        for cps in plan(*refs):
            for k in (4, 5, 6, 7):
                cps[k][1]().wait_recv()
            for k in range(n_k):
                cps[k][0]().wait_send()

    return _Side(shards, [jax.ShapeDtypeStruct((N_CHIPS,) + s.shape, s.dtype) for s in shards], nw * n_k,
                 [send_own, relay, pass_diagonal, finish])


def _run_side(side, *, name):
    n_in, n_out = len(side.ins), len(side.out_shapes)

    def body(*refs):
        ins, outs = refs[:n_in], refs[n_in:n_in + n_out]
        for phase in side.phases:
            phase(ins, outs, refs[-2], refs[-1])

    return pl.pallas_call(
        body, name=name, in_specs=[ANY] * n_in, out_specs=[ANY] * n_out, out_shape=side.out_shapes,
        scratch_shapes=side.sems(), compiler_params=pltpu.CompilerParams(has_side_effects=True),
    )(*side.ins)


def _exchange_side(gs):
    nw = len(gs)

    def copies(ins, outs, ssem, rsem):
        x, y, c, _ = _place()
        return [pltpu.make_async_remote_copy(
            src_ref=ins[w].at[:, 1 - c], dst_ref=outs[w], send_sem=ssem.at[w], recv_sem=rsem.at[w],
            device_id=(x, y, 1 - c), device_id_type=MESH) for w in range(nw)]

    def start(*refs):
        for cp in copies(*refs):
            cp.start()

    def finish(*refs):
        for cp in copies(*refs):
            cp.wait()

    return _Side(gs, [jax.ShapeDtypeStruct((g.shape[0],) + g.shape[2:], g.dtype) for g in gs], nw, [start, finish])


def _scatter_side(ps):
    nw = len(ps)

    def copies(ins, outs, ssem, rsem):
        x, y, c, chips = _place()
        s_me = 2 * x + y
        return [pltpu.make_async_remote_copy(
            src_ref=ins[w].at[2 * chip[0] + chip[1]], dst_ref=outs[w].at[s_me],
            send_sem=ssem.at[w * 3 + j], recv_sem=rsem.at[w * 3 + j],
            device_id=(chip[0], chip[1], c), device_id_type=MESH)
            for w in range(nw) for j, chip in enumerate(chips)]

    def start(*refs):
        for cp in copies(*refs):
            cp.start()

    def finish(*refs):
        for cp in copies(*refs):
            cp.wait()

    return _Side(ps, [jax.ShapeDtypeStruct(p.shape, p.dtype) for p in ps], nw * 3, [start, finish])


def _pair_share(hs, *, name):
    nw = len(hs)

    def body(*refs):
        ins, outs = refs[:nw], refs[nw:2 * nw]
        ssem, rsem = refs[2 * nw:]
        x, y, c, _ = _place()
        cps = []
        for w in range(nw):
            cp = pltpu.make_async_remote_copy(
                src_ref=ins[w], dst_ref=outs[w], send_sem=ssem.at[w], recv_sem=rsem.at[w],
                device_id=(x, y, 1 - c), device_id_type=MESH)
            cp.start()
            cps.append(cp)
        for cp in cps:
            cp.wait()

    return pl.pallas_call(
        body, name=name, in_specs=[ANY] * nw, out_specs=[ANY] * nw,
        out_shape=[jax.ShapeDtypeStruct(h.shape, h.dtype) for h in hs],
        scratch_shapes=[pltpu.SemaphoreType.DMA((nw,)), pltpu.SemaphoreType.DMA((nw,))],
        compiler_params=pltpu.CompilerParams(has_side_effects=True),
    )(*hs)


def _gather_small(a, *, name):
    def body(a_ref, o_ref, ssem, rsem):
        x, y, c, chips = _place()
        s_me = 2 * x + y
        o_ref[s_me] = a_ref[...]
        cps = []
        for j, chip in enumerate(chips):
            cp = pltpu.make_async_remote_copy(
                src_ref=a_ref, dst_ref=o_ref.at[s_me], send_sem=ssem.at[j], recv_sem=rsem.at[j],
                device_id=(chip[0], chip[1], c), device_id_type=MESH)
            cp.start()
            cps.append(cp)
        for cp in cps:
            cp.wait()

    vm = pl.BlockSpec(memory_space=pltpu.VMEM)
    return pl.pallas_call(
        body, name=name, in_specs=[vm], out_specs=vm,
        out_shape=jax.ShapeDtypeStruct((N_CHIPS,) + a.shape, a.dtype),
        scratch_shapes=[pltpu.SemaphoreType.DMA((3,)), pltpu.SemaphoreType.DMA((3,))],
    )(a)


def _allreduce_small(p, *, name):
    def body(p_ref, o_ref, buf, ssem, rsem):
        x, y, c, _ = _place()
        me = 4 * x + 2 * y + c
        buf[me] = p_ref[...]
        cps = []
        for k in range(1, N_DEV):
            fx, fy, fc = (k >> 2) & 1, (k >> 1) & 1, k & 1
            peer = (x + fx - 2 * x * fx, y + fy - 2 * y * fy, c + fc - 2 * c * fc)
            cp = pltpu.make_async_remote_copy(
                src_ref=buf.at[me], dst_ref=buf.at[me], send_sem=ssem.at[k - 1], recv_sem=rsem.at[k - 1],
                device_id=peer, device_id_type=MESH)
            cp.start()
            cps.append(cp)
        for cp in cps:
            cp.wait()
        acc = buf[0]
        for i in range(1, N_DEV):
            acc = acc + buf[i]
        o_ref[...] = acc

    vm = pl.BlockSpec(memory_space=pltpu.VMEM)
    return pl.pallas_call(
        body, name=name, in_specs=[vm], out_specs=vm,
        out_shape=jax.ShapeDtypeStruct(p.shape, p.dtype),
        scratch_shapes=[pltpu.VMEM((N_DEV,) + p.shape, p.dtype), pltpu.SemaphoreType.DMA((N_DEV - 1,)),
                        pltpu.SemaphoreType.DMA((N_DEV - 1,))],
    )(p)


def _rows_to_tokens(r):
    h, n = r.shape[0], r.shape[1]
    return r.reshape(h, n * CHUNK).T


def _tokens_to_rows(a):
    t, h = a.shape
    return a.T.reshape(h, t // CHUNK, 1, CHUNK)


def kernel(x, norm1_w, w_in, conv_w, a_log, dt_bias, gdn_norm_w, ret_norm_w, w_out, norm2_w, w_gate, w_up, w_down, norm_f_w, loss_target, m_norm1_w, m_w_in, m_conv_w, m_a_log, m_dt_bias, m_gdn_norm_w, m_ret_norm_w, m_w_out, m_norm2_w, m_w_gate, m_w_up, m_w_down, m_norm_f_w, v_norm1_w, v_w_in, v_conv_w, v_a_log, v_dt_bias, v_gdn_norm_w, v_ret_norm_w, v_w_out, v_norm2_w, v_w_gate, v_w_up, v_w_down, v_norm_f_w):
    t, d = x.shape[1], x.shape[2]
    f = w_gate.shape[2] * N_CHIPS
    nqk, nv = HEADS * DK, HEADS * DV
    ncs = w_in.shape[2]
    c_idx = lax.axis_index("c")
    s_idx = 2 * lax.axis_index("x") + lax.axis_index("y")
    xs = x[0]
    tgt = loss_target[0]

    n_small = 2 * HEADS
    widths = [2 * nqk + nv, nv, n_small, nqk, nqk, nv, nv, d, d]
    g_off = np.concatenate([[0], np.cumsum(widths)])
    order = [0, 3, 4, 5, 1, 6, 7, 8]
    m_off = np.concatenate([[0], np.cumsum([widths[i] for i in order])])
    o_rq, o_rk, o_rv, o_az, o_rg, o_ga, o_gb = [int(m_off[i]) for i in range(1, 8)]
    segs = [(int(g_off[i]), int(g_off[i + 1]), int(m_off[order.index(i)]) if i != 2 else None) for i in range(9)]

    def shard_pieces(s):
        out = []
        for a, b, dst in segs:
            lo, hi = max(a, s * ncs), min(b, (s + 1) * ncs)
            if lo < hi:
                out.append((lo - s * ncs, hi - s * ncs, None if dst is None else dst + lo - a))
        return out

    own = [w_in[0].astype(BF16), w_out[0].astype(BF16), w_gate[0].astype(BF16), w_up[0].astype(BF16),
           w_down[0].astype(BF16)]
    put_own = lambda full, mine: lax.dynamic_update_slice(full, mine[None], (s_idx, 0, 0))
    wg_in = put_own(_run_side(_gather_side(own[:1]), name="gather_w_in")[0], own[0])
    conv_full = _gather_small(conv_w[0], name="gather_conv_w")
    conv_full = jnp.concatenate([conv_full[i] for i in range(N_CHIPS)], axis=1)
    cuts = [(dst, s, a, b) for s in range(N_CHIPS) for a, b, dst in shard_pieces(s)]
    w_main = jnp.concatenate([wg_in[s][:, a:b] for dst, s, a, b in sorted(c for c in cuts if c[0] is not None)],
                             axis=1)
    w_small = jnp.concatenate([wg_in[s][:, a:b] for dst, s, a, b in cuts if dst is None], axis=1)
    w_small = jnp.pad(w_small, ((0, 0), (0, LANES - n_small)))

    pad16 = lambda a: jnp.pad(a, ((0, 0), (HEADS, LANES - 2 * HEADS)))
    alog_row, dtb_row = pad16(a_log), pad16(dt_bias)
    wa_row = jnp.tile(gdn_norm_w, (1, HEADS))
    inv = ROPE_BASE ** (-jnp.arange(0, DK, 2, dtype=F32) / DK)
    ang = jnp.arange(t, dtype=F32)[:, None] * inv[None, :]
    cos2 = jnp.concatenate([jnp.cos(ang), jnp.cos(ang)], axis=1)
    sin2 = jnp.concatenate([-jnp.sin(ang), jnp.sin(ang)], axis=1)
    lg = jnp.log1p(-jnp.exp2(-5.0 - jnp.arange(HEADS, dtype=F32)))
    lg_tab = jnp.broadcast_to(lg[:, None, None], (HEADS, 1, LANES))

    fq = f // N_CHIPS
    u1, projs = _rms_fwd(xs, norm1_w, w_narrow=w_small, name="rms1_fwd")
    proj, *rest = _mm(u1, w_main, tm=512, tn=2048, tk=d, side=_gather_side(own[1:]), name="mm_proj")
    wg_out, wg_gate, wg_up, wg_down = [put_own(g, o) for g, o in zip(rest, own[1:])]
    w_o = wg_out.reshape(d, d)
    w_g, w_u = wg_gate, wg_up
    w_d = wg_down.reshape(f, d)
    q_a, k_a, v_a = _conv_fwd(proj, conv_full, name="conv_fwd")
    bg = _bg_fwd(projs, alog_row, dtb_row, name="bg_fwd")
    beta_r = _tokens_to_rows(bg[:, :HEADS])
    g_r = _tokens_to_rows(bg[:, HEADS:2 * HEADS])
    gc_r, tinv, u_a, w_a, p_a = _gdn_prep(q_a, k_a, v_a, beta_r, g_r, name="gdn_prep")
    o_a, s_a, o_b, s_b = _mix_fwd(q_a, k_a, u_a, w_a, p_a, gc_r, proj, cos2, sin2, lg_tab, o_rq, o_rk, o_rv,
                                  name="mix_fwd")
    mixed = _merge_fwd(o_a, o_b, proj, wa_row, ret_norm_w, o_az, o_rg, o_ga, o_gb, name="merge_fwd")
    h1, hn = _mm_res_norm(mixed, w_o, xs, norm2_w, tm=512, name="mm_out")
    gt, up, act = _ffn_in(hn, w_g, w_u, tm=512, name="ffn_in")
    h2 = _mm(act, w_d, res=h1, tm=512, tn=1024, tk=f, name="mm_down")
    loss_row, dh2, dh2b, d_nf = _loss_head(h2, tgt, norm_f_w.reshape(1, d), name="loss_head")

    g_down = _mm(act, dh2b, ta=True, out_dtype=BF16, tm=fq, tn=d, tk=1024, name="mm_dw_down")
    dgt, dup = _ffn_back(dh2b, w_d, gt, up, tm=512, tn=fq, name="ffn_back")
    dhn = _mm(dgt, w_g, tb=True, pair=(dup, w_u), tm=512, tn=d, tk=fq, name="mm_dhn")
    g_gate = _mm(hn, dgt, ta=True, out_dtype=BF16, out_stacked=True, tm=1024, tn=fq, tk=2048, name="mm_dw_gate")
    g_up = _mm(hn, dup, ta=True, out_dtype=BF16, out_stacked=True, tm=1024, tn=fq, tk=2048, name="mm_dw_up")
    dh1, dh1b, d_n2 = _rms_bwd(dhn, h1, norm2_w, dh2, name="rms2_bwd")
    dmix = _mm(dh1b, w_o, tb=True, tm=512, tn=d, tk=d, name="mm_dmix")
    g_out = _mm(mixed, dh1b, ta=True, out_dtype=BF16, tm=1024, tn=d, tk=1024, name="mm_dw_out")
    halves = lambda g: g.reshape(N_CHIPS, 2, g.shape[1] // 2, g.shape[2])
    c_arr = jnp.reshape(c_idx, (1,)).astype(jnp.int32)
    gs_ffn = [halves(g_out.reshape(N_CHIPS, d // N_CHIPS, d)), halves(g_gate), halves(g_up),
              halves(g_down.reshape(N_CHIPS, fq, d))]
    do_a, do_b, dproj, d_wa, d_wb, *rsib = _merge_bwd(
        dmix, o_a, o_b, proj, wa_row, ret_norm_w, o_az, o_rg, o_ga, o_gb, side=_exchange_side(gs_ffn),
        name="merge_bwd")
    ps_ffn = [_pair_add(g, r, c_arr, name=f"grad_pair_add_{nm}")
              for g, r, nm in zip(gs_ffn, rsib, ["w_out", "w_gate", "w_up", "w_down"])]
    dq_a, dk_a, dv_a, dbeta_r, dg_r, dproj = _mix_bwd(
        q_a, k_a, v_a, beta_r, gc_r, tinv, u_a, w_a, s_a, do_a, proj, cos2, sin2, lg_tab, s_b, do_b, dproj,
        o_rq, o_rk, o_rv, name="mix_bwd")
    dc = _conv_bwd_pre(proj, conv_full, dq_a, dk_a, dv_a, name="conv_bwd_pre")
    dproj, d_cw = _conv_bwd(proj, dc, conv_full, dproj, name="conv_bwd")
    dbg = jnp.pad(jnp.concatenate([_rows_to_tokens(dbeta_r), _rows_to_tokens(dg_r)], axis=1),
                  ((0, 0), (0, LANES - 2 * HEADS)))
    dprojs, d_alog, d_dtb = _bg_bwd(projs, dbg, alog_row, dtb_row, name="bg_bwd")
    g_main, *qs_ffn = _mm(u1, dproj, ta=True, out_dtype=BF16, tm=1024, tn=2048, tk=2048,
                          side=_scatter_side(ps_ffn), name="mm_dw_in")
    g_small = _mm(u1, dprojs, ta=True, out_dtype=BF16, tm=1024, tn=LANES, tk=1024, name="mm_dw_in_small")
    pieces = []
    for s in range(N_CHIPS):
        seen, parts = 0, []
        for a, b, dst in shard_pieces(s):
            parts.append(g_small[:, seen:seen + b - a] if dst is None else g_main[:, dst:dst + b - a])
            seen += (b - a) if dst is None else 0
        pieces.append(jnp.concatenate(parts, axis=1))
    gs_in = [halves(jnp.stack(pieces))]
    rsib = _run_side(_exchange_side(gs_in), name="grad_pair_exchange_in")
    ps_in = [_pair_add(gs_in[0], rsib[0], c_arr, name="grad_pair_add_w_in")]
    du, *qs_in = _mm(dproj, w_main, tb=True, out_dtype=BF16, tm=1024, tn=d, tk=2048,
                     side=_scatter_side(ps_in), name="mm_du")
    dx, _, d_n1 = _rms_bwd(du, xs, norm1_w, dh1, narrow=(dprojs, w_small), name="rms1_bwd")

    names = ["w_in", "w_out", "w_gate", "w_up", "w_down"]
    qs = [lax.dynamic_update_slice(q, lax.dynamic_slice(p, (s_idx, 0, 0), (1,) + p.shape[1:]), (s_idx, 0, 0))
          for q, p in zip(qs_in + qs_ffn, ps_in + ps_ffn)]
    hs = [_chip_sum(q, name=f"grad_chip_sum_{nm}") for q, nm in zip(qs, names)]
    theirs = _pair_share(hs, name="grad_pair_share")
    big_w = [w_in[0], w_out[0], w_gate[0], w_up[0], w_down[0]]
    big_m = [m_w_in[0], m_w_out[0], m_w_gate[0], m_w_up[0], m_w_down[0]]
    big_v = [v_w_in[0], v_w_out[0], v_w_gate[0], v_w_up[0], v_w_down[0]]
    big = {}
    for nm, mine, other, w_, m_, v_ in zip(names, hs, theirs, big_w, big_m, big_v):
        big[nm] = tuple(a[None] for a in _adamw_halves(w_, mine, other, c_arr, m_, v_, name=f"adamw_{nm}"))

    d_wa_h = jnp.sum(d_wa.reshape(HEADS, DV), axis=0, keepdims=True)
    small = [d_n1, d_alog[:, HEADS:2 * HEADS], d_dtb[:, HEADS:2 * HEADS], d_wa_h, d_wb, d_n2, d_nf,
             d_cw[:CONV_W].reshape(1, -1)]
    sizes = [a.shape[1] for a in small]
    packed = jnp.concatenate(small, axis=1)
    n_pack = packed.shape[1]
    n_rows = -(-n_pack // LANES)
    n_rows = -(-n_rows // 8) * 8
    packed = jnp.pad(packed, ((0, 0), (0, n_rows * LANES - n_pack))).reshape(n_rows, LANES)
    red = _allreduce_small(packed, name="allreduce_small").reshape(1, -1)
    offs = np.cumsum([0] + sizes)
    g_n1, g_alog, g_dtb, g_wa, g_wb, g_n2, g_nf, g_cw = [red[:, offs[i]:offs[i + 1]] for i in range(len(sizes))]
    ncw = conv_w.shape[2]
    g_cw = lax.dynamic_slice(g_cw.reshape(CONV_W, -1), (0, s_idx * ncw), (CONV_W, ncw))

    def small_update(w_, g, m_, v_, nm):
        shape = w_.shape
        pad = (-w_.size) % LANES
        to2 = lambda a: jnp.pad(a.reshape(1, -1), ((0, 0), (0, pad)))
        outs = _adamw(to2(w_), to2(g), to2(m_), to2(v_), name=f"adamw_{nm}")
        return (g.reshape(shape),) + tuple(a[:, :w_.size].reshape(shape) for a in outs)

    res = {
        "norm1_w": small_update(norm1_w, g_n1, m_norm1_w, v_norm1_w, "norm1_w"),
        "w_in": big["w_in"],
        "conv_w": small_update(conv_w, g_cw, m_conv_w, v_conv_w, "conv_w"),
        "a_log": small_update(a_log, g_alog, m_a_log, v_a_log, "a_log"),
        "dt_bias": small_update(dt_bias, g_dtb, m_dt_bias, v_dt_bias, "dt_bias"),
        "gdn_norm_w": small_update(gdn_norm_w, g_wa, m_gdn_norm_w, v_gdn_norm_w, "gdn_norm_w"),
        "ret_norm_w": small_update(ret_norm_w, g_wb, m_ret_norm_w, v_ret_norm_w, "ret_norm_w"),
        "w_out": big["w_out"],
        "norm2_w": small_update(norm2_w, g_n2, m_norm2_w, v_norm2_w, "norm2_w"),
        "w_gate": big["w_gate"],
        "w_up": big["w_up"],
        "w_down": big["w_down"],
        "norm_f_w": small_update(norm_f_w, g_nf, m_norm_f_w, v_norm_f_w, "norm_f_w"),
    }
    order = ["norm1_w", "w_in", "conv_w", "a_log", "dt_bias", "gdn_norm_w", "ret_norm_w", "w_out", "norm2_w",
             "w_gate", "w_up", "w_down", "norm_f_w"]
    loss = lax.psum(loss_row[0, 0], ("x", "y", "c"))
    return (loss, dx[None], *[res[n][0] for n in order], *[res[n][1] for n in order],
            *[res[n][2] for n in order], *[res[n][3] for n in order])
```

```python
import functools

import jax
import jax.numpy as jnp
import numpy as np
from jax import lax
from jax.experimental import pallas as pl
from jax.experimental.pallas import tpu as pltpu

F32 = jnp.float32
BF16 = jnp.bfloat16
MESH = pl.DeviceIdType.MESH

HEADS = 8
DK = 128
DV = 256
CHUNK = 64
CONV_W = 4
EPS = 1e-6
ROPE_BASE = 10000.0
ADAM_LR, ADAM_B1, ADAM_B2, ADAM_EPS, ADAM_WD, ADAM_STEP = 0.001, 0.9, 0.999, 1e-08, 0.01, 10
N_CHIPS = 4
N_DEV = 8
LANES = 128
HALO = 8
VMEM_LIMIT = 56 * 1024 * 1024
HB = 8


def _pick(n, cands):
    for c in cands:
        if n % c == 0:
            return c
    raise ValueError(f"no tile for {n} in {cands}")


def _params(sem=None):
    return pltpu.CompilerParams(dimension_semantics=sem, vmem_limit_bytes=VMEM_LIMIT)


def _dot(a, b):
    return jnp.dot(a.astype(BF16), b.astype(BF16), preferred_element_type=F32)


def _dot_nt(a, b):
    return lax.dot_general(a.astype(BF16), b.astype(BF16), (((1,), (1,)), ((), ())), preferred_element_type=F32)


def _dot_tn(a, b):
    return lax.dot_general(a.astype(BF16), b.astype(BF16), (((0,), (0,)), ((), ())), preferred_element_type=F32)


def _sigmoid(x):
    return 1.0 / (1.0 + jnp.exp(-x))


def _iota2(n):
    return lax.broadcasted_iota(jnp.int32, (n, n), 0), lax.broadcasted_iota(jnp.int32, (n, n), 1)


def _row_to_col(row):
    n = row.shape[1]
    r, c = _iota2(n)
    return jnp.sum(jnp.where(r == c, jnp.broadcast_to(row, (n, n)), 0.0), axis=1, keepdims=True)


def _col_to_row(col):
    n = col.shape[0]
    r, c = _iota2(n)
    return jnp.sum(jnp.where(r == c, jnp.broadcast_to(col, (n, n)), 0.0), axis=0, keepdims=True)


class _Side:
    def __init__(self, ins, out_shapes, n_sem, phases):
        self.ins, self.out_shapes, self.n_sem, self.phases = list(ins), list(out_shapes), n_sem, phases

    def sems(self):
        return [pltpu.SemaphoreType.DMA((self.n_sem,)), pltpu.SemaphoreType.DMA((self.n_sem,))]


def _mm(a, b, *, name, tm, tn, tk, ta=False, tb=False, out_dtype=F32, res=None, side=None, pair=None,
        out_stacked=False):
    m, k = (a.shape[1], a.shape[0]) if ta else a.shape
    b_slots = b.ndim == 3
    if b_slots:
        assert tb and tk == b.shape[2]
        n = b.shape[1]
    else:
        n = b.shape[0] if tb else b.shape[1]
    tm, tn, tk = min(tm, m), min(tn, n), min(tk, k)
    assert m % tm == 0 and n % tn == 0 and k % tk == 0, (name, m, n, k)
    nk = k // tk
    nj, ni = n // tn, m // tm
    dn = (((0 if ta else 1,), (1 if tb else 0,)), ((), ()))
    n_ab = 4 if pair else 2
    n_in = n_ab + (res is not None)
    n_side_in = len(side.ins) if side else 0
    n_side_out = len(side.out_shapes) if side else 0

    def body(*refs):
        a_ref, b_ref = refs[0], refs[1]
        r_ref = refs[n_ab] if res is not None else None
        o_ref = refs[n_in + n_side_in]
        if side:
            s_in = refs[n_in:n_in + n_side_in]
            s_out = refs[n_in + n_side_in + 1:n_in + n_side_in + 1 + n_side_out]
            ssem, rsem = refs[-2], refs[-1]
            j_, i_, k_ = pl.program_id(0), pl.program_id(1), pl.program_id(2)
            n_mid = len(side.phases) - 2
            assert n_mid == 0 or nj >= 2 * n_mid
            when = [(j_ == 0) & (i_ == 0) & (k_ == 0)]
            when += [(j_ == nj // 2 + (p * (nj // 2)) // n_mid) & (i_ == 0) & (k_ == 0) for p in range(n_mid)]
            when.append((j_ == nj - 1) & (i_ == ni - 1) & (k_ == nk - 1))

        def run_phase(p):
            @pl.when(when[p])
            def _():
                side.phases[p](s_in, s_out, ssem, rsem)

        if side:
            for p in range(len(side.phases) - 1):
                run_phase(p)

        def finish(r):
            if res is not None:
                r = r + r_ref[...]
            o_ref[...] = r.astype(out_dtype)

        part = lax.dot_general(a_ref[...], b_ref[...], dn, preferred_element_type=F32)
        if pair:
            part = part + lax.dot_general(refs[2][...], refs[3][...], dn, preferred_element_type=F32)
        if nk == 1:
            finish(part)
        else:
            acc = refs[n_in + n_side_in + 1 + n_side_out]
            kk = pl.program_id(2)

            @pl.when(kk == 0)
            def _():
                acc[...] = part

            @pl.when((kk > 0) & (kk < nk - 1))
            def _():
                acc[...] += part

            @pl.when(kk == nk - 1)
            def _():
                finish(acc[...] + part)

        if side:
            run_phase(len(side.phases) - 1)

    a_spec = pl.BlockSpec((tk, tm), lambda j, i, kk: (kk, i)) if ta else pl.BlockSpec((tm, tk), lambda j, i, kk: (i, kk))
    b_spec = pl.BlockSpec((tn, tk), lambda j, i, kk: (j, kk)) if tb else pl.BlockSpec((tk, tn), lambda j, i, kk: (kk, j))
    if b_slots:
        b_spec = pl.BlockSpec((None, tn, tk), lambda j, i, kk: (kk, j, 0))
    o_spec = pl.BlockSpec((tm, tn), lambda j, i, kk: (i, j))
    o_shape = jax.ShapeDtypeStruct((m, n), out_dtype)
    if out_stacked:
        assert res is None
        o_spec = pl.BlockSpec((None, tm, tn), lambda j, i, kk: (j, i, 0))
        o_shape = jax.ShapeDtypeStruct((nj, m, tn), out_dtype)
    in_specs, args = [a_spec, b_spec], [a, b]
    if pair:
        assert pair[0].shape == a.shape and pair[1].shape == b.shape
        in_specs += [a_spec, b_spec]
        args += list(pair)
    if res is not None:
        in_specs.append(o_spec)
        args.append(res)
    out_specs, out_shape = o_spec, o_shape
    scratch = [pltpu.VMEM((tm, tn), F32)] if nk > 1 else []
    sem = ("parallel", "parallel", "arbitrary")
    if side:
        hbm = pl.BlockSpec(memory_space=pl.ANY)
        in_specs += [hbm] * n_side_in
        args += side.ins
        out_specs, out_shape = [o_spec] + [hbm] * n_side_out, [out_shape] + side.out_shapes
        scratch += side.sems()
        sem = ("arbitrary",) * 3
    return pl.pallas_call(
        body, name=name, grid=(nj, ni, nk), in_specs=in_specs, out_specs=out_specs, out_shape=out_shape,
        scratch_shapes=scratch, compiler_params=_params(sem),
    )(*args)


def _mm_res_norm(a, b, res, w, *, name, tm):
    m, k = a.shape
    n = b.shape[1]
    tm = min(tm, m)

    def body(a_ref, b_ref, r_ref, w_ref, h_ref, hn_ref):
        h = jnp.dot(a_ref[...], b_ref[...], preferred_element_type=F32) + r_ref[...]
        h_ref[...] = h
        r = lax.rsqrt(jnp.mean(h * h, axis=-1, keepdims=True) + EPS)
        hn_ref[...] = (h * r * w_ref[...]).astype(BF16)

    row = pl.BlockSpec((tm, n), lambda i: (i, 0))
    return pl.pallas_call(
        body, name=name, grid=(m // tm,),
        in_specs=[pl.BlockSpec((tm, k), lambda i: (i, 0)), pl.BlockSpec((k, n), lambda i: (0, 0)), row,
                  pl.BlockSpec((1, n), lambda i: (0, 0))],
        out_specs=[row, row],
        out_shape=[jax.ShapeDtypeStruct((m, n), F32), jax.ShapeDtypeStruct((m, n), BF16)],
        compiler_params=_params(("parallel",)),
    )(a, b, res, w)


def _rms_fwd(x, w, *, name, w_narrow=None):
    t, d = x.shape
    tt = _pick(t, (512, 256))

    def body(*refs):
        x_ref, w_ref, o_ref = refs[0], refs[1], refs[-2 if w_narrow is not None else -1]
        xv = x_ref[...]
        r = lax.rsqrt(jnp.mean(xv * xv, axis=-1, keepdims=True) + EPS)
        u = (xv * r * w_ref[...]).astype(BF16)
        o_ref[...] = u
        if w_narrow is not None:
            refs[-1][...] = jnp.dot(u, refs[2][...], preferred_element_type=F32)

    blk = pl.BlockSpec((tt, d), lambda i: (i, 0))
    in_specs, args = [blk, pl.BlockSpec((1, d), lambda i: (0, 0))], [x, w]
    out_specs, out_shape = [blk], [jax.ShapeDtypeStruct((t, d), BF16)]
    if w_narrow is not None:
        in_specs.append(pl.BlockSpec(w_narrow.shape, lambda i: (0, 0)))
        args.append(w_narrow)
        out_specs.append(pl.BlockSpec((tt, LANES), lambda i: (i, 0)))
        out_shape.append(jax.ShapeDtypeStruct((t, LANES), F32))
    out = pl.pallas_call(
        body, name=name, grid=(t // tt,), in_specs=in_specs, out_specs=out_specs, out_shape=out_shape,
        compiler_params=_params(("parallel",)),
    )(*args)
    return out if w_narrow is not None else out[0]


def _rms_bwd(dn, x, w, dres, *, name, bf16_copy, narrow=None):
    t, d = x.shape
    tt = _pick(t, (256,))

    def body(*refs):
        dn_ref, x_ref, w_ref, dres_ref = refs[:4]
        dx_ref, dw_ref = refs[-3 if bf16_copy else -2], refs[-1]
        xv, g = x_ref[...], dn_ref[...].astype(F32)
        if narrow is not None:
            g = g + lax.dot_general(refs[4][...], refs[5][...], (((1,), (1,)), ((), ())), preferred_element_type=F32)
        r = lax.rsqrt(jnp.mean(xv * xv, axis=-1, keepdims=True) + EPS)
        xh = xv * r
        gw = g * w_ref[...]
        dx = dres_ref[...] + r * (gw - xh * jnp.mean(gw * xh, axis=-1, keepdims=True))
        dx_ref[...] = dx
        if bf16_copy:
            refs[-2][...] = dx.astype(BF16)

        @pl.when(pl.program_id(0) == 0)
        def _():
            dw_ref[...] = jnp.zeros_like(dw_ref)

        dw_ref[...] += jnp.sum(g * xh, axis=0, keepdims=True)

    blk = pl.BlockSpec((tt, d), lambda i: (i, 0))
    row = pl.BlockSpec((1, d), lambda i: (0, 0))
    in_specs, args = [blk, blk, row, blk], [dn, x, w, dres]
    if narrow is not None:
        in_specs += [pl.BlockSpec((tt, LANES), lambda i: (i, 0)), pl.BlockSpec(narrow[1].shape, lambda i: (0, 0))]
        args += list(narrow)
    copy_spec, copy_shape = ([blk], [jax.ShapeDtypeStruct((t, d), BF16)]) if bf16_copy else ([], [])
    return pl.pallas_call(
        body, name=name, grid=(t // tt,), in_specs=in_specs, out_specs=[blk] + copy_spec + [row],
        out_shape=[jax.ShapeDtypeStruct((t, d), F32)] + copy_shape + [jax.ShapeDtypeStruct((1, d), F32)],
        compiler_params=_params(("arbitrary",)),
    )(*args)


def _loss_head(h2, tgt, wf, *, name):
    t, d = h2.shape
    tt = _pick(t, (256,))

    def body(x_ref, t_ref, w_ref, loss_ref, dx_ref, dxb_ref, dw_ref):
        xv = x_ref[...]
        r = lax.rsqrt(jnp.mean(xv * xv, axis=-1, keepdims=True) + EPS)
        xh = xv * r
        err = xh * w_ref[...] - t_ref[...]
        lpart = 0.5 * jnp.sum(jnp.mean(err * err, axis=-1, keepdims=True), axis=0, keepdims=True)
        dy = err * (1.0 / d)
        gw = dy * w_ref[...]
        dx = r * (gw - xh * jnp.mean(gw * xh, axis=-1, keepdims=True))
        dx_ref[...] = dx
        dxb_ref[...] = dx.astype(BF16)

        @pl.when(pl.program_id(0) == 0)
        def _():
            dw_ref[...] = jnp.zeros_like(dw_ref)
            loss_ref[...] = jnp.zeros_like(loss_ref)

        dw_ref[...] += jnp.sum(dy * xh, axis=0, keepdims=True)
        loss_ref[...] += jnp.broadcast_to(lpart, loss_ref.shape)

    blk = pl.BlockSpec((tt, d), lambda i: (i, 0))
    row = pl.BlockSpec((1, d), lambda i: (0, 0))
    lrow = pl.BlockSpec((1, LANES), lambda i: (0, 0))
    return pl.pallas_call(
        body, name=name, grid=(t // tt,),
        in_specs=[blk, blk, row], out_specs=[lrow, blk, blk, row],
        out_shape=[jax.ShapeDtypeStruct((1, LANES), F32), jax.ShapeDtypeStruct((t, d), F32),
                   jax.ShapeDtypeStruct((t, d), BF16), jax.ShapeDtypeStruct((1, d), F32)],
        compiler_params=_params(("arbitrary",)),
    )(h2, tgt, wf)


def _ffn_in(hn, w_g, w_u, *, name, tm):
    t, d = hn.shape
    tn = w_g.shape[2]
    f = w_g.shape[0] * tn
    tm = min(tm, t)

    def body(a_ref, g_ref, u_ref, gt_ref, up_ref, act_ref):
        a = a_ref[...]
        g = jnp.dot(a, g_ref[...], preferred_element_type=F32)
        u = jnp.dot(a, u_ref[...], preferred_element_type=F32)
        gt_ref[...] = g.astype(BF16)
        up_ref[...] = u.astype(BF16)
        act_ref[...] = (g * _sigmoid(g) * u).astype(BF16)

    wblk = pl.BlockSpec((None, d, tn), lambda j, i: (j, 0, 0))
    oblk = pl.BlockSpec((tm, tn), lambda j, i: (i, j))
    return pl.pallas_call(
        body, name=name, grid=(f // tn, t // tm),
        in_specs=[pl.BlockSpec((tm, d), lambda j, i: (i, 0)), wblk, wblk], out_specs=[oblk] * 3,
        out_shape=[jax.ShapeDtypeStruct((t, f), BF16)] * 3,
        compiler_params=_params(("parallel", "parallel")),
    )(hn, w_g, w_u)


def _ffn_back(dh2b, w_d, gt, up, *, name, tm, tn):
    t, d = dh2b.shape
    f = w_d.shape[0]
    tm = min(tm, t)

    n_sub = 2 if tm % 32 == 0 else 1

    def body(a_ref, w_ref, g_ref, u_ref, dg_ref, du_ref):
        w = w_ref[...]
        subs = [slice(i * (tm // n_sub), (i + 1) * (tm // n_sub)) for i in range(n_sub)]
        das = [lax.dot_general(a_ref[rs, :], w, (((1,), (1,)), ((), ())), preferred_element_type=F32) for rs in subs]
        for rs, da in zip(subs, das):
            g = g_ref[rs, :].astype(F32)
            sg = _sigmoid(g)
            dg_ref[rs, :] = (da * u_ref[rs, :].astype(F32) * sg * (1.0 + g * (1.0 - sg))).astype(BF16)
            du_ref[rs, :] = (da * g * sg).astype(BF16)

    oblk = pl.BlockSpec((tm, tn), lambda j, i: (i, j))
    return pl.pallas_call(
        body, name=name, grid=(f // tn, t // tm),
        in_specs=[pl.BlockSpec((tm, d), lambda j, i: (i, 0)), pl.BlockSpec((tn, d), lambda j, i: (j, 0)), oblk, oblk],
        out_specs=[oblk] * 2,
        out_shape=[jax.ShapeDtypeStruct((t, f), BF16)] * 2,
        compiler_params=_params(("parallel", "parallel")),
    )(dh2b, w_d, gt, up)


CONV_HEAD = 2 * HALO


def _causal_conv(x_ref, xp_ref, cw_ref, xs, tt):
    first = pl.program_id(0) == 0
    xs[0:HALO, :] = jnp.where(first, 0.0, xp_ref[...])
    xs[HALO:HALO + CONV_HEAD, :] = x_ref[0:CONV_HEAD, :]
    xb = x_ref[...].astype(BF16)
    r, c = _iota2(tt)
    moved = [None] + [jnp.dot((c == r - sh).astype(BF16), xb, preferred_element_type=F32)
                      for sh in range(1, CONV_W)]

    def cols(cb):
        cs = slice(cb * LANES, (cb + 1) * LANES)
        c_head = c_main = None
        for j in range(CONV_W):
            sh = CONV_W - 1 - j
            w_j = cw_ref[j:j + 1, cs]
            x_m = x_ref[CONV_HEAD:tt, cs] if sh == 0 else moved[sh][CONV_HEAD:tt, cs]
            x_h = xs[HALO - sh:HALO - sh + CONV_HEAD, cs]
            c_main = x_m * w_j if c_main is None else c_main + x_m * w_j
            c_head = x_h * w_j if c_head is None else c_head + x_h * w_j
        return jnp.concatenate([c_head, c_main], axis=0)

    return cols


def _conv_specs(tt, ch):
    cur = pl.BlockSpec((tt, ch), lambda i: (i, 0))
    prev = pl.BlockSpec((HALO, ch), lambda i: (jnp.maximum(i * (tt // HALO) - 1, 0), 0))
    return cur, prev


def _conv_fwd(proj, conv_w, *, name):
    t = proj.shape[0]
    ch = conv_w.shape[1]
    nqk = HEADS * DK
    tt = _pick(t, (256,))
    cur, prev = _conv_specs(tt, ch)

    def body(x_ref, xp_ref, cw_ref, q_ref, k_ref, v_ref, xs):
        conv = _causal_conv(x_ref, xp_ref, cw_ref, xs, tt)
        for cb in range(ch // LANES):
            c = conv(cb)
            s = c * _sigmoid(c)
            if cb < 2 * HEADS:
                s = s * lax.rsqrt(jnp.sum(s * s, axis=-1, keepdims=True) + EPS)
                if cb < HEADS:
                    q_ref[:, cb * LANES:(cb + 1) * LANES] = (s * (DK ** -0.5)).astype(BF16)
                else:
                    k_ref[:, (cb - HEADS) * LANES:(cb - HEADS + 1) * LANES] = s.astype(BF16)
            else:
                v_ref[:, (cb - 2 * HEADS) * LANES:(cb - 2 * HEADS + 1) * LANES] = s.astype(BF16)

    return pl.pallas_call(
        body, name=name, grid=(t // tt,),
        in_specs=[cur, prev, pl.BlockSpec((CONV_W, ch), lambda i: (0, 0))],
        out_specs=[pl.BlockSpec((tt, nqk), lambda i: (i, 0)), pl.BlockSpec((tt, nqk), lambda i: (i, 0)),
                   pl.BlockSpec((tt, ch - 2 * nqk), lambda i: (i, 0))],
        out_shape=[jax.ShapeDtypeStruct((t, nqk), BF16), jax.ShapeDtypeStruct((t, nqk), BF16),
                   jax.ShapeDtypeStruct((t, ch - 2 * nqk), BF16)],
        scratch_shapes=[pltpu.VMEM((HALO + CONV_HEAD, ch), F32)],
        compiler_params=_params(("arbitrary",)),
    )(proj, proj, conv_w)


def _conv_bwd_pre(proj, conv_w, dq, dk, dv, *, name):
    t = proj.shape[0]
    ch = conv_w.shape[1]
    nqk = HEADS * DK
    tt = _pick(t, (256,))
    cur, prev = _conv_specs(tt, ch)

    def body(x_ref, xp_ref, cw_ref, dq_ref, dk_ref, dv_ref, dc_ref, xs):
        conv = _causal_conv(x_ref, xp_ref, cw_ref, xs, tt)
        for cb in range(ch // LANES):
            c = conv(cb)
            sg = _sigmoid(c)
            s = c * sg
            if cb < 2 * HEADS:
                if cb < HEADS:
                    d, scale = dq_ref[:, cb * LANES:(cb + 1) * LANES], DK ** -0.5
                else:
                    d, scale = dk_ref[:, (cb - HEADS) * LANES:(cb - HEADS + 1) * LANES], 1.0
                rinv = lax.rsqrt(jnp.sum(s * s, axis=-1, keepdims=True) + EPS)
                ds = scale * rinv * (d - s * (rinv * rinv) * jnp.sum(d * s, axis=-1, keepdims=True))
            else:
                ds = dv_ref[:, (cb - 2 * HEADS) * LANES:(cb - 2 * HEADS + 1) * LANES]
            dc_ref[:, cb * LANES:(cb + 1) * LANES] = ds * sg * (1.0 + c * (1.0 - sg))

    return pl.pallas_call(
        body, name=name, grid=(t // tt,),
        in_specs=[cur, prev, pl.BlockSpec((CONV_W, ch), lambda i: (0, 0)),
                  pl.BlockSpec((tt, nqk), lambda i: (i, 0)), pl.BlockSpec((tt, nqk), lambda i: (i, 0)),
                  pl.BlockSpec((tt, ch - 2 * nqk), lambda i: (i, 0))],
        out_specs=pl.BlockSpec((tt, ch), lambda i: (i, 0)),
        out_shape=jax.ShapeDtypeStruct((t, ch), F32),
        scratch_shapes=[pltpu.VMEM((HALO + CONV_HEAD, ch), F32)],
        compiler_params=_params(("arbitrary",)),
    )(proj, proj, conv_w, dq, dk, dv)


def _conv_bwd(proj, dc, conv_w, dproj, *, name):
    t = proj.shape[0]
    ch = conv_w.shape[1]
    tt = _pick(t, (256,))
    nt = t // tt
    cur = pl.BlockSpec((tt, ch), lambda i: (i, 0))
    nxt = pl.BlockSpec((HALO, ch), lambda i: (jnp.minimum((i + 1) * (tt // HALO), t // HALO - 1), 0))

    tail = 2 * HALO
    main = tt - tail

    def body(x_ref, d_ref, dn_ref, cw_ref, alias_ref, dx_ref, dw_ref, ds):
        last = pl.program_id(0) == nt - 1
        ds[0:tail, :] = d_ref[main:tt, :]
        ds[tail:tail + HALO, :] = jnp.where(last, 0.0, dn_ref[...])

        @pl.when(pl.program_id(0) == 0)
        def _():
            dw_ref[...] = jnp.zeros_like(dw_ref)

        dcb = d_ref[...].astype(BF16)
        r, c = _iota2(tt)
        moved = [None] + [jnp.dot((c == r + sh).astype(BF16), dcb, preferred_element_type=F32)
                          for sh in range(1, CONV_W)]
        for cb in range(ch // LANES):
            cols = slice(cb * LANES, (cb + 1) * LANES)
            x_m, x_t = x_ref[0:main, cols], x_ref[main:tt, cols]
            acc_m = acc_t = None
            for j in range(CONV_W):
                sh = CONV_W - 1 - j
                w_j = cw_ref[j:j + 1, cols]
                dj_m = d_ref[0:main, cols] if sh == 0 else moved[sh][0:main, cols]
                dj_t = ds[sh:sh + tail, cols]
                acc_m = dj_m * w_j if acc_m is None else acc_m + dj_m * w_j
                acc_t = dj_t * w_j if acc_t is None else acc_t + dj_t * w_j
                dw_ref[j:j + 1, cols] += (jnp.sum(dj_m * x_m, axis=0, keepdims=True)
                                          + jnp.sum(dj_t * x_t, axis=0, keepdims=True))
            dx_ref[0:main, cols] = acc_m.astype(BF16)
            dx_ref[main:tt, cols] = acc_t.astype(BF16)

    return pl.pallas_call(
        body, name=name, grid=(nt,),
        in_specs=[cur, cur, nxt, pl.BlockSpec((CONV_W, ch), lambda i: (0, 0)), pl.BlockSpec(memory_space=pl.ANY)],
        out_specs=[pl.BlockSpec((tt, ch), lambda i: (i, 0)), pl.BlockSpec((HALO, ch), lambda i: (0, 0))],
        out_shape=[jax.ShapeDtypeStruct(dproj.shape, BF16), jax.ShapeDtypeStruct((HALO, ch), F32)],
        scratch_shapes=[pltpu.VMEM((tail + HALO, ch), F32)],
        input_output_aliases={4: 0},
        compiler_params=_params(("arbitrary",)),
    )(proj, dc, dc, conv_w, dproj)


def _bg_fwd(projs, alog_row, dtb_row, *, name):
    t = projs.shape[0]
    tt = _pick(t, (512, 256))

    def body(p_ref, al_ref, db_ref, o_ref):
        p = p_ref[...]
        lane = lax.broadcasted_iota(jnp.int32, p.shape, 1)
        z = p + db_ref[...]
        sp = jnp.maximum(z, 0.0) + jnp.log1p(jnp.exp(-jnp.abs(z)))
        g = -jnp.exp(al_ref[...]) * sp
        o_ref[...] = jnp.where(lane < HEADS, _sigmoid(p), jnp.where(lane < 2 * HEADS, g, 0.0))

    blk = pl.BlockSpec((tt, LANES), lambda i: (i, 0))
    row = pl.BlockSpec((1, LANES), lambda i: (0, 0))
    return pl.pallas_call(
        body, name=name, grid=(t // tt,), in_specs=[blk, row, row], out_specs=blk,
        out_shape=jax.ShapeDtypeStruct((t, LANES), F32), compiler_params=_params(("parallel",)),
    )(projs, alog_row, dtb_row)


def _bg_bwd(projs, dbg, alog_row, dtb_row, *, name):
    t = projs.shape[0]
    tt = _pick(t, (512, 256))

    def body(p_ref, d_ref, al_ref, db_ref, o_ref, dal_ref, ddb_ref):
        p, d = p_ref[...], d_ref[...]
        lane = lax.broadcasted_iota(jnp.int32, p.shape, 1)
        isg = (lane >= HEADS) & (lane < 2 * HEADS)
        be = _sigmoid(p)
        z = p + db_ref[...]
        sp = jnp.maximum(z, 0.0) + jnp.log1p(jnp.exp(-jnp.abs(z)))
        ea = jnp.exp(al_ref[...])
        d_aa = jnp.where(isg, d * (-ea) * _sigmoid(z), 0.0)
        o_ref[...] = jnp.where(lane < HEADS, d * be * (1.0 - be), d_aa).astype(BF16)

        @pl.when(pl.program_id(0) == 0)
        def _():
            dal_ref[...] = jnp.zeros_like(dal_ref)
            ddb_ref[...] = jnp.zeros_like(ddb_ref)

        dal_ref[...] += jnp.sum(jnp.where(isg, d * (-ea) * sp, 0.0), axis=0, keepdims=True)
        ddb_ref[...] += jnp.sum(d_aa, axis=0, keepdims=True)

    blk = pl.BlockSpec((tt, LANES), lambda i: (i, 0))
    row = pl.BlockSpec((1, LANES), lambda i: (0, 0))
    return pl.pallas_call(
        body, name=name, grid=(t // tt,), in_specs=[blk, blk, row, row], out_specs=[blk, row, row],
        out_shape=[jax.ShapeDtypeStruct((t, LANES), BF16), jax.ShapeDtypeStruct((1, LANES), F32),
                   jax.ShapeDtypeStruct((1, LANES), F32)],
        compiler_params=_params(("arbitrary",)),
    )(projs, dbg, alog_row, dtb_row)


def _hn_specs(n_chunks, rev=False):
    def nn(n):
        return n_chunks - 1 - n if rev else n
    tok = lambda w: pl.BlockSpec((CHUNK, HB * w), lambda g, n: (nn(n), g))
    per = lambda a, b: pl.BlockSpec((HB, None, a, b), lambda g, n: (g, nn(n), 0, 0))
    return tok, per


def _hcols(hh, w):
    return slice(hh * w, (hh + 1) * w)


def _decay(gc_col, gc_row):
    r, c = _iota2(CHUNK)
    d = jnp.exp(jnp.minimum(gc_col - gc_row, 0.0))
    return jnp.where(r >= c, d, 0.0), jnp.where(r > c, d, 0.0)


def _gdn_prep(q, k, v, beta_r, g_r, *, name):
    t = q.shape[0]
    n_chunks = t // CHUNK
    cps = 2 if n_chunks % 2 == 0 else 1
    tok = lambda w: pl.BlockSpec((cps * CHUNK, HB * w), lambda g, n: (n, g))
    per = lambda a, b: pl.BlockSpec((HB, cps, a, b), lambda g, n: (g, n, 0, 0))

    def body(q_ref, k_ref, v_ref, b_ref, g_ref, gc_ref, ti_ref, u_ref, w_ref, p_ref):
        r, c = _iota2(CHUNK)
        its = [(h, cc) for cc in range(cps) for h in range(HB)]
        rows = lambda cc: slice(cc * CHUNK, (cc + 1) * CHUNK)
        n = range(len(its))
        kk = [k_ref[rows(cc), _hcols(h, DK)] for h, cc in its]
        gc_col = [jnp.sum(jnp.where(c <= r, jnp.broadcast_to(g_ref[h, cc], (CHUNK, CHUNK)), 0.0), axis=1,
                          keepdims=True) for h, cc in its]
        gc_row = [_col_to_row(gc_col[i]) for i in n]
        beta_col = [_row_to_col(b_ref[h, cc]) for h, cc in its]
        dec = [_decay(gc_col[i], gc_row[i]) for i in n]
        kb = [kk[i] * beta_col[i] for i in n]
        for i, (h, cc) in enumerate(its):
            gc_ref[h, cc] = gc_row[i]
            p_ref[h, cc] = (_dot_nt(q_ref[rows(cc), _hcols(h, DK)], kk[i]) * dec[i][0]).astype(BF16)
        pw = [_dot_nt(kb[i], kk[i]) * dec[i][1] for i in n]
        y = [-pw[i] for i in n]
        for _ in range(5):
            pw = [_dot(pw[i], pw[i]) for i in n]
            yp = [_dot(y[i], pw[i]) for i in n]
            y = [y[i] + pw[i] + yp[i] for i in n]
        vb = [v_ref[rows(cc), _hcols(h, DV)] * beta_col[i] for i, (h, cc) in enumerate(its)]
        kbe = [kb[i] * jnp.exp(gc_col[i]) for i in n]
        yv = [_dot(y[i], vb[i]) for i in n]
        yk = [_dot(y[i], kbe[i]) for i in n]
        for i, (h, cc) in enumerate(its):
            ti_ref[h, cc] = y[i].astype(BF16)
            u_ref[rows(cc), _hcols(h, DV)] = vb[i] + yv[i]
            w_ref[rows(cc), _hcols(h, DK)] = (kbe[i] + yk[i]).astype(BF16)

    return pl.pallas_call(
        body, name=name, grid=(HEADS // HB, n_chunks // cps),
        in_specs=[tok(DK), tok(DK), tok(DV), per(1, CHUNK), per(1, CHUNK)],
        out_specs=[per(1, CHUNK), per(CHUNK, CHUNK), tok(DV), tok(DK), per(CHUNK, CHUNK)],
        out_shape=[jax.ShapeDtypeStruct((HEADS, n_chunks, 1, CHUNK), F32),
                   jax.ShapeDtypeStruct((HEADS, n_chunks, CHUNK, CHUNK), BF16),
                   jax.ShapeDtypeStruct((t, HEADS * DV), F32), jax.ShapeDtypeStruct((t, HEADS * DK), BF16),
                   jax.ShapeDtypeStruct((HEADS, n_chunks, CHUNK, CHUNK), BF16)],
        compiler_params=_params(("parallel", "parallel")),
    )(q, k, v, beta_r, g_r)


def _interleave(*step_gens):
    live = list(step_gens)
    while live:
        for g in list(live):
            try:
                next(g)
            except StopIteration:
                live.remove(g)


def _gdn_scan_steps(q_ref, k_ref, u_ref, w_ref, p_ref, gc_ref, o_ref, s_ref, st):
    hs = range(HB)
    s = [st[h] for h in hs]
    gc_row = [gc_ref[h] for h in hs]
    gc_col = [_row_to_col(gc_row[h]) for h in hs]
    glast = [gc_row[h][:, CHUNK - 1:CHUNK] for h in hs]
    for h in hs:
        s_ref[h] = s[h].astype(BF16)
    yield
    ws = [_dot(w_ref[:, _hcols(h, DK)], s[h]) for h in hs]
    yield
    qs = [_dot(q_ref[:, _hcols(h, DK)] * jnp.exp(gc_col[h]), s[h]) for h in hs]
    vn = [u_ref[:, _hcols(h, DV)] - ws[h] for h in hs]
    yield
    pv = [_dot(p_ref[h], vn[h]) for h in hs]
    yield
    kv = [_dot_tn(k_ref[:, _hcols(h, DK)] * jnp.exp(glast[h] - gc_col[h]), vn[h]) for h in hs]
    yield
    for h in hs:
        o_ref[:, _hcols(h, DV)] = (qs[h] + pv[h]).astype(BF16)
        st[h] = s[h] * jnp.exp(glast[h]) + kv[h]


def _gdn_bwd_steps(q_ref, k_ref, v_ref, b_ref, gc_ref, ti_ref, u_ref, w_ref, s_ref, do_ref,
                   dq_ref, dk_ref, dv_ref, db_ref, dg_ref, dst):
    r, c = _iota2(CHUNK)
    rows = lax.broadcasted_iota(jnp.int32, (CHUNK, 1), 0)
    rsum = lambda a: jnp.sum(a, axis=1, keepdims=True)
    hs = range(HB)
    qq = [q_ref[:, _hcols(h, DK)] for h in hs]
    kk = [k_ref[:, _hcols(h, DK)] for h in hs]
    ww = [w_ref[:, _hcols(h, DK)] for h in hs]
    uu = [u_ref[:, _hcols(h, DV)] for h in hs]
    d_o = [do_ref[:, _hcols(h, DV)] for h in hs]
    s = [s_ref[h].astype(F32) for h in hs]
    d_s = [dst[h] for h in hs]
    gc_row = [gc_ref[h] for h in hs]
    gc_col = [_row_to_col(gc_row[h]) for h in hs]
    beta_col = [_row_to_col(b_ref[h]) for h in hs]
    eg = [jnp.exp(gc_col[h]) for h in hs]
    glast = [gc_row[h][:, CHUNK - 1:CHUNK] for h in hs]
    eglast = [jnp.exp(glast[h]) for h in hs]
    e2 = [jnp.exp(glast[h] - gc_col[h]) for h in hs]
    dec = [_decay(gc_col[h], gc_row[h]) for h in hs]
    kb = [kk[h] * beta_col[h] for h in hs]
    ke = [kk[h] * e2[h] for h in hs]
    qe = [qq[h] * eg[h] for h in hs]
    yield
    kkt = [_dot_nt(kb[h], kk[h]) for h in hs]
    qk = [_dot_nt(qq[h], kk[h]) for h in hs]
    ws = [_dot(ww[h], s[h]) for h in hs]
    yield
    a = [kkt[h] * dec[h][1] for h in hs]
    pp = [qk[h] * dec[h][0] for h in hs]
    vn = [uu[h] - ws[h] for h in hs]
    t1 = [_dot_tn(pp[h], d_o[h]) for h in hs]
    t2 = [_dot(ke[h], d_s[h]) for h in hs]
    yield
    dqe = [_dot_nt(d_o[h], s[h]) for h in hs]
    dke = [_dot_nt(vn[h], d_s[h]) for h in hs]
    dqk = [_dot_nt(d_o[h], vn[h]) * dec[h][0] for h in hs]
    yield
    dvn = [t1[h] + t2[h] for h in hs]
    t3 = [_dot_tn(qe[h], d_o[h]) for h in hs]
    t4 = [_dot_tn(ww[h], dvn[h]) for h in hs]
    dw = [-_dot_nt(dvn[h], s[h]) for h in hs]
    yield
    de2 = [rsum(dke[h] * ke[h]) for h in hs]
    dglast = [jnp.sum(de2[h], axis=0, keepdims=True)
              + eglast[h] * jnp.sum(rsum(d_s[h] * s[h]), axis=0, keepdims=True) for h in hs]
    for h in hs:
        dst[h] = d_s[h] * eglast[h] + t3[h] - t4[h]
    yy = [ti_ref[h] for h in hs]
    t5 = [_dot_tn(yy[h], dvn[h]) for h in hs]
    t6 = [_dot_tn(yy[h], dw[h]) for h in hs]
    yield
    dvb = [dvn[h] + t5[h] for h in hs]
    dkbe = [dw[h] + t6[h] for h in hs]
    t7 = [_dot_nt(dvb[h], uu[h]) for h in hs]
    t8 = [_dot_nt(dkbe[h], ww[h]) for h in hs]
    yield
    d_a = [jnp.where(r > c, -(t7[h] + t8[h]), 0.0) for h in hs]
    dm = [d_a[h] * dec[h][1] for h in hs]
    t9 = [_dot(dm[h], kk[h]) for h in hs]
    t10 = [_dot(dqk[h], kk[h]) for h in hs]
    yield
    t11 = [_dot_tn(dqk[h], qq[h]) for h in hs]
    t12 = [_dot_tn(dm[h], kb[h]) for h in hs]
    yield
    for h in hs:
        dkb = t9[h] + dkbe[h] * eg[h]
        e_mat = d_a[h] * a[h] + dqk[h] * qk[h]
        dgc = (rsum(dqe[h] * qe[h]) - de2[h] + rsum(dkbe[h] * kb[h] * eg[h]) + rsum(e_mat)
               - _row_to_col(jnp.sum(e_mat, axis=0, keepdims=True)))
        dgc = dgc + jnp.where(rows == CHUNK - 1, dglast[h], 0.0)
        dq_ref[:, _hcols(h, DK)] = dqe[h] * eg[h] + t10[h]
        dk_ref[:, _hcols(h, DK)] = t11[h] + dke[h] * e2[h] + t12[h] + dkb * beta_col[h]
        dv_ref[:, _hcols(h, DV)] = dvb[h] * beta_col[h]
        dbeta = rsum(dkb * kk[h]) + rsum(dvb[h] * v_ref[:, _hcols(h, DV)])
        db_ref[h] = _col_to_row(dbeta)
        dg_ref[h] = jnp.sum(jnp.where(r >= c, jnp.broadcast_to(dgc, (CHUNK, CHUNK)), 0.0), axis=0, keepdims=True)


def _rot(x, cos2, sin2):
    return x * cos2 + pltpu.roll(x, DK // 2, 1) * sin2


def _unrot(d, cos2, sin2):
    return d * cos2 + pltpu.roll(d * sin2, DK // 2, 1)


def _ret_consts(lg):
    r, c = _iota2(CHUNK)
    dm = jnp.where(r >= c, jnp.exp((r - c).astype(F32) * lg), 0.0)
    pos = lax.broadcasted_iota(jnp.int32, (CHUNK, 1), 0).astype(F32)
    return dm, jnp.exp((pos + 1.0) * lg), jnp.exp((CHUNK - 1.0 - pos) * lg), jnp.exp(CHUNK * lg)


def _ret_fwd_steps(q_ref, k_ref, v_ref, cos_ref, sin_ref, lg_ref, o_ref, s_ref, st):
    cos2v, sin2v = cos_ref[...], sin_ref[...]
    hs = range(HB)
    s = [st[h] for h in hs]
    for h in hs:
        s_ref[h] = s[h].astype(BF16)
    cst = [_ret_consts(lg_ref[h][:, 0:1]) for h in hs]
    qq = [_rot(q_ref[:, _hcols(h, DK)], cos2v, sin2v) for h in hs]
    kk = [_rot(k_ref[:, _hcols(h, DK)], cos2v, sin2v) * (DK ** -0.5) for h in hs]
    vv = [v_ref[:, _hcols(h, DV)] for h in hs]
    yield
    sc = [_dot_nt(qq[h], kk[h]) * cst[h][0] for h in hs]
    yield
    qs = [_dot(qq[h], s[h]) for h in hs]
    yield
    kv = [_dot_tn(kk[h] * cst[h][2], vv[h]) for h in hs]
    yield
    sv = [_dot(sc[h], vv[h]) for h in hs]
    yield
    for h in hs:
        o_ref[:, _hcols(h, DV)] = (sv[h] + qs[h] * cst[h][1]).astype(BF16)
        st[h] = s[h] * cst[h][3] + kv[h]


def _mix_fwd(q, k, u, w, p, gc_r, proj, cos2, sin2, lg_tab, col_q, col_k, col_v, *, name):
    t = q.shape[0]
    n_chunks = t // CHUNK
    tok, per = _hn_specs(n_chunks)
    bq, bk, bv = col_q // (HB * DK), col_k // (HB * DK), col_v // (HB * DV)

    def body(*refs):
        st_a, st_b = refs[-2:]

        @pl.when(pl.program_id(1) == 0)
        def _():
            st_a[...] = jnp.zeros_like(st_a)
            st_b[...] = jnp.zeros_like(st_b)

        _interleave(_gdn_scan_steps(*refs[0:6], refs[12], refs[13], st_a),
                    _ret_fwd_steps(*refs[6:12], refs[14], refs[15], st_b))

    state = jax.ShapeDtypeStruct((HEADS, n_chunks, DK, DV), BF16)
    return pl.pallas_call(
        body, name=name, grid=(HEADS // HB, n_chunks),
        in_specs=[tok(DK), tok(DK), tok(DV), tok(DK), per(CHUNK, CHUNK), per(1, CHUNK),
                  pl.BlockSpec((CHUNK, HB * DK), lambda g, n: (n, bq + g)),
                  pl.BlockSpec((CHUNK, HB * DK), lambda g, n: (n, bk + g)),
                  pl.BlockSpec((CHUNK, HB * DV), lambda g, n: (n, bv + g)),
                  pl.BlockSpec((CHUNK, DK), lambda g, n: (n, 0)), pl.BlockSpec((CHUNK, DK), lambda g, n: (n, 0)),
                  pl.BlockSpec((HB, 1, LANES), lambda g, n: (g, 0, 0))],
        out_specs=[tok(DV), per(DK, DV), tok(DV), per(DK, DV)],
        out_shape=[jax.ShapeDtypeStruct((t, HEADS * DV), BF16), state, jax.ShapeDtypeStruct((t, HEADS * DV), BF16), state],
        scratch_shapes=[pltpu.VMEM((HB, DK, DV), F32)] * 2,
        compiler_params=_params(("arbitrary", "arbitrary")),
    )(q, k, u, w, p, gc_r, proj, proj, proj, cos2, sin2, lg_tab)


def _ret_bwd_steps(q_ref, k_ref, v_ref, cos_ref, sin_ref, lg_ref, s_ref, do_ref, d_ref, dst):
    nqk = HEADS * DK
    cos2v, sin2v = cos_ref[...], sin_ref[...]
    hs = range(HB)
    s = [s_ref[h].astype(F32) for h in hs]
    d_s = [dst[h] for h in hs]
    d_o = [do_ref[:, _hcols(h, DV)] for h in hs]
    cst = [_ret_consts(lg_ref[h][:, 0:1]) for h in hs]
    qq = [_rot(q_ref[:, _hcols(h, DK)], cos2v, sin2v) for h in hs]
    kk = [_rot(k_ref[:, _hcols(h, DK)], cos2v, sin2v) * (DK ** -0.5) for h in hs]
    vv = [v_ref[:, _hcols(h, DV)] for h in hs]
    dxo = [d_o[h] * cst[h][1] for h in hs]
    yield
    sc = [_dot_nt(qq[h], kk[h]) * cst[h][0] for h in hs]
    dsc = [_dot_nt(d_o[h], vv[h]) * cst[h][0] for h in hs]
    yield
    t1 = [_dot(kk[h] * cst[h][2], d_s[h]) for h in hs]
    t2 = [_dot_nt(dxo[h], s[h]) for h in hs]
    yield
    t3 = [_dot_nt(vv[h], d_s[h]) for h in hs]
    t4 = [_dot_tn(qq[h], dxo[h]) for h in hs]
    yield
    t5 = [_dot_tn(sc[h], d_o[h]) for h in hs]
    t6 = [_dot(dsc[h], kk[h]) for h in hs]
    yield
    t7 = [_dot_tn(dsc[h], qq[h]) for h in hs]
    yield
    for h in hs:
        dst[h] = d_s[h] * cst[h][3] + t4[h]
        d_ref[:, 2 * nqk + h * DV:2 * nqk + (h + 1) * DV] = (t5[h] + t1[h]).astype(BF16)
        d_ref[:, _hcols(h, DK)] = _unrot(t6[h] + t2[h], cos2v, sin2v).astype(BF16)
        d_ref[:, nqk + h * DK:nqk + (h + 1) * DK] = (
            _unrot(t7[h] + t3[h] * cst[h][2], cos2v, sin2v) * (DK ** -0.5)).astype(BF16)


def _mix_bwd(q, k, v, beta_r, gc_r, ti, u, w, s_a, do_a, proj, cos2, sin2, lg_tab, s_b, do_b, dproj,
             col_q, col_k, col_v, *, name):
    t = q.shape[0]
    n_chunks = t // CHUNK
    tok, per = _hn_specs(n_chunks, rev=True)
    bq, bk, bv = col_q // (HB * DK), col_k // (HB * DK), col_v // (HB * DV)
    rv = lambda n: n_chunks - 1 - n
    nqk, wid = HEADS * DK, HEADS * (2 * DK + DV)
    assert HB == HEADS and col_k == col_q + nqk and col_v == col_k + nqk and col_q % wid == 0
    n_a, n_b = 10, 8

    def body(*refs):
        ins_a, ins_b = refs[:n_a], refs[n_a:n_a + n_b]
        outs = refs[n_a + n_b + 1:n_a + n_b + 7]
        dst_a, dst_b = refs[-2:]

        @pl.when(pl.program_id(1) == 0)
        def _():
            dst_a[...] = jnp.zeros_like(dst_a)
            dst_b[...] = jnp.zeros_like(dst_b)

        _interleave(_gdn_bwd_steps(*ins_a, *outs[:5], dst_a), _ret_bwd_steps(*ins_b, outs[5], dst_b))

    rows = jax.ShapeDtypeStruct((HEADS, n_chunks, 1, CHUNK), F32)
    return pl.pallas_call(
        body, name=name, grid=(HEADS // HB, n_chunks),
        in_specs=[tok(DK), tok(DK), tok(DV), per(1, CHUNK), per(1, CHUNK), per(CHUNK, CHUNK), tok(DV), tok(DK),
                  per(DK, DV), tok(DV),
                  pl.BlockSpec((CHUNK, HB * DK), lambda g, n: (rv(n), bq + g)),
                  pl.BlockSpec((CHUNK, HB * DK), lambda g, n: (rv(n), bk + g)),
                  pl.BlockSpec((CHUNK, HB * DV), lambda g, n: (rv(n), bv + g)),
                  pl.BlockSpec((CHUNK, DK), lambda g, n: (rv(n), 0)),
                  pl.BlockSpec((CHUNK, DK), lambda g, n: (rv(n), 0)),
                  pl.BlockSpec((HB, 1, LANES), lambda g, n: (g, 0, 0)),
                  per(DK, DV), tok(DV), pl.BlockSpec(memory_space=pl.ANY)],
        out_specs=[tok(DK), tok(DK), tok(DV), per(1, CHUNK), per(1, CHUNK),
                   pl.BlockSpec((CHUNK, wid), lambda g, n: (rv(n), col_q // wid))],
        out_shape=[jax.ShapeDtypeStruct((t, HEADS * DK), F32), jax.ShapeDtypeStruct((t, HEADS * DK), F32),
                   jax.ShapeDtypeStruct((t, HEADS * DV), F32), rows, rows, jax.ShapeDtypeStruct(dproj.shape, BF16)],
        scratch_shapes=[pltpu.VMEM((HB, DK, DV), F32)] * 2,
        input_output_aliases={n_a + n_b: 5},
        compiler_params=_params(("arbitrary", "arbitrary")),
    )(q, k, v, beta_r, gc_r, ti, u, w, s_a, do_a, proj, proj, proj, cos2, sin2, lg_tab, s_b, do_b, dproj)


def _merge_parts(oa, ob, z, rg, ga, gb):
    oa, ob = oa.astype(F32), ob.astype(F32)
    ra = lax.rsqrt(jnp.mean(oa * oa, axis=-1, keepdims=True) + EPS)
    xa = oa * ra
    mu = jnp.mean(ob, axis=-1, keepdims=True)
    cen = ob - mu
    rb = lax.rsqrt(jnp.mean(cen * cen, axis=-1, keepdims=True) + EPS)
    xb = cen * rb
    sz, sr = _sigmoid(z), _sigmoid(rg)
    return ra, xa, rb, xb, sz, sr, _sigmoid(ga), _sigmoid(gb)


def _merge_specs(tt, d, cz, crg, cga, cgb):
    blk = pl.BlockSpec((tt, d), lambda i: (i, 0))
    pcol = lambda c: pl.BlockSpec((tt, d), lambda i: (i, c // d))
    row = pl.BlockSpec((1, d), lambda i: (0, 0))
    return blk, [blk, blk, pcol(cz), pcol(crg), pcol(cga), pcol(cgb), row, row], row


def _merge_fwd(oa, ob, proj, wa, wb, cz, crg, cga, cgb, *, name):
    t, d = oa.shape
    tt = _pick(t, (256,))
    blk, in_specs, _ = _merge_specs(tt, d, cz, crg, cga, cgb)

    def body(oa_ref, ob_ref, z_ref, rg_ref, ga_ref, gb_ref, wa_ref, wb_ref, o_ref):
        for h in range(HEADS):
            cols = slice(h * DV, (h + 1) * DV)
            z, rg = z_ref[:, cols], rg_ref[:, cols]
            _, xa, _, xb, sz, sr, sga, sgb = _merge_parts(
                oa_ref[:, cols], ob_ref[:, cols], z, rg, ga_ref[:, cols], gb_ref[:, cols])
            o_a = xa * wa_ref[:, cols] * (z * sz)
            o_b = xb * wb_ref[:, cols] * (rg * sr)
            o_ref[:, cols] = (sga * o_a + sgb * o_b).astype(BF16)

    return pl.pallas_call(
        body, name=name, grid=(t // tt,), in_specs=in_specs, out_specs=blk,
        out_shape=jax.ShapeDtypeStruct((t, d), BF16), compiler_params=_params(("parallel",)),
    )(oa, ob, proj, proj, proj, proj, wa, wb)


def _merge_bwd(dmix, oa, ob, proj, wa, wb, cz, crg, cga, cgb, *, name, side):
    t, d = oa.shape
    tt = _pick(t, (256,))
    nt = t // tt
    blk, in_specs, row = _merge_specs(tt, d, cz, crg, cga, cgb)
    assert (crg, cga, cgb) == (cz + d, cz + 2 * d, cz + 3 * d) and cz % (4 * d) == 0
    n_si, n_so = len(side.ins), len(side.out_shapes)
    assert len(side.phases) == 2

    def body(*refs):
        dm_ref, oa_ref, ob_ref, z_ref, rg_ref, ga_ref, gb_ref, wa_ref, wb_ref = refs[:9]
        s_in = refs[9:9 + n_si]
        doa_ref, dob_ref, dp_ref, dwa_ref, dwb_ref = refs[9 + n_si:14 + n_si]
        s_out, ssem, rsem = refs[14 + n_si:14 + n_si + n_so], refs[-2], refs[-1]
        dz_ref, drg_ref, dga_ref, dgb_ref = [dp_ref.at[:, i * d:(i + 1) * d] for i in range(4)]

        @pl.when(pl.program_id(0) == 0)
        def _():
            side.phases[0](s_in, s_out, ssem, rsem)
            dwa_ref[...] = jnp.zeros_like(dwa_ref)
            dwb_ref[...] = jnp.zeros_like(dwb_ref)

        for h in range(HEADS):
            cols = slice(h * DV, (h + 1) * DV)
            z, rg, wa_h, wb_h, dmx = z_ref[:, cols], rg_ref[:, cols], wa_ref[:, cols], wb_ref[:, cols], dm_ref[:, cols]
            ra, xa, rb, xb, sz, sr, sga, sgb = _merge_parts(
                oa_ref[:, cols], ob_ref[:, cols], z, rg, ga_ref[:, cols], gb_ref[:, cols])
            na, nb = xa * wa_h, xb * wb_h
            sil_z, sil_r = z * sz, rg * sr
            o_a, o_b = na * sil_z, nb * sil_r
            dga_ref[:, cols] = (dmx * o_a * sga * (1.0 - sga)).astype(BF16)
            dgb_ref[:, cols] = (dmx * o_b * sgb * (1.0 - sgb)).astype(BF16)
            d_oa, d_ob = dmx * sga, dmx * sgb
            dz_ref[:, cols] = (d_oa * na * sz * (1.0 + z * (1.0 - sz))).astype(BF16)
            drg_ref[:, cols] = (d_ob * nb * sr * (1.0 + rg * (1.0 - sr))).astype(BF16)
            dna, dnb = d_oa * sil_z, d_ob * sil_r
            dwa_ref[:, cols] += jnp.sum(dna * xa, axis=0, keepdims=True)
            dwb_ref[:, cols] += jnp.sum(dnb * xb, axis=0, keepdims=True)
            gwa, gwb = dna * wa_h, dnb * wb_h
            doa_ref[:, cols] = ra * (gwa - xa * jnp.mean(gwa * xa, axis=-1, keepdims=True))
            dob_ref[:, cols] = rb * (gwb - jnp.mean(gwb, axis=-1, keepdims=True)
                                     - xb * jnp.mean(gwb * xb, axis=-1, keepdims=True))

        @pl.when(pl.program_id(0) == nt - 1)
        def _():
            side.phases[1](s_in, s_out, ssem, rsem)

    hbm = pl.BlockSpec(memory_space=pl.ANY)
    return pl.pallas_call(
        body, name=name, grid=(nt,), in_specs=[blk] + in_specs + [hbm] * n_si,
        out_specs=[blk, blk, pl.BlockSpec((tt, 4 * d), lambda i: (i, cz // (4 * d))), row, row] + [hbm] * n_so,
        out_shape=[jax.ShapeDtypeStruct((t, d), F32)] * 2 + [jax.ShapeDtypeStruct(proj.shape, BF16)]
        + [jax.ShapeDtypeStruct((1, d), F32)] * 2 + side.out_shapes,
        scratch_shapes=side.sems(),
        compiler_params=_params(("arbitrary",)),
    )(dmix, oa, ob, proj, proj, proj, proj, wa, wb, *side.ins)


def _row_block(rows, cols, itemsize=4, target=1 << 20):
    for rb in (512, 256, 128, 64, 32, 16, 8):
        if rows % rb == 0 and rb * cols * itemsize <= target:
            return rb
    return rows


def _adamw(w, g, m, v, *, name):
    rows, cols = w.shape
    rb = _row_block(rows, cols)

    def body(w_ref, g_ref, m_ref, v_ref, d_ref, nm_ref, nv_ref):
        gg = g_ref[...]
        mm = ADAM_B1 * m_ref[...] + (1.0 - ADAM_B1) * gg
        vv = ADAM_B2 * v_ref[...] + (1.0 - ADAM_B2) * (gg * gg)
        m_hat = mm / (1.0 - ADAM_B1 ** ADAM_STEP)
        v_hat = vv / (1.0 - ADAM_B2 ** ADAM_STEP)
        d_ref[...] = -ADAM_LR * (m_hat / (jnp.sqrt(v_hat) + ADAM_EPS) + ADAM_WD * w_ref[...])
        nm_ref[...] = mm
        nv_ref[...] = vv

    blk = pl.BlockSpec((rb, cols), lambda i: (i, 0))
    return pl.pallas_call(
        body, name=name, grid=(rows // rb,), in_specs=[blk] * 4, out_specs=[blk] * 3,
        out_shape=[jax.ShapeDtypeStruct((rows, cols), F32)] * 3, compiler_params=_params(("parallel",)),
    )(w, g, m, v)


def _adamw_halves(w, mine, other, c_idx, m, v, *, name):
    rows, cols = w.shape
    hr = rows // 2
    rb = _row_block(hr, cols, 4, 2 << 20)
    nb = hr // rb

    def body(c_ref, w_ref, a_ref, b_ref, m_ref, v_ref, g_ref, d_ref, nm_ref, nv_ref):
        gg = jnp.where(pl.program_id(0) // nb == c_ref[0], a_ref[...], b_ref[...])
        mm = ADAM_B1 * m_ref[...] + (1.0 - ADAM_B1) * gg
        vv = ADAM_B2 * v_ref[...] + (1.0 - ADAM_B2) * (gg * gg)
        m_hat = mm / (1.0 - ADAM_B1 ** ADAM_STEP)
        v_hat = vv / (1.0 - ADAM_B2 ** ADAM_STEP)
        g_ref[...] = gg
        d_ref[...] = -ADAM_LR * (m_hat / (jnp.sqrt(v_hat) + ADAM_EPS) + ADAM_WD * w_ref[...])
        nm_ref[...] = mm
        nv_ref[...] = vv

    blk = pl.BlockSpec((rb, cols), lambda i, c: (i, 0))
    half = pl.BlockSpec((rb, cols), lambda i, c: (i % nb, 0))
    return pl.pallas_call(
        body, name=name,
        grid_spec=pltpu.PrefetchScalarGridSpec(
            num_scalar_prefetch=1, grid=(rows // rb,), in_specs=[blk, half, half, blk, blk], out_specs=[blk] * 4),
        out_shape=[jax.ShapeDtypeStruct((rows, cols), F32)] * 4, compiler_params=_params(("parallel",)),
    )(c_idx, w, mine, other, m, v)


def _pair_add(g, rsib, c_idx, *, name):
    _, _, hr, cols = g.shape
    rb = _row_block(hr, cols, 2, 4 << 20)

    def body(c_ref, g_ref, r_ref, o_ref):
        o_ref[...] = (g_ref[...].astype(F32) + r_ref[...].astype(F32)).astype(BF16)

    return pl.pallas_call(
        body, name=name,
        grid_spec=pltpu.PrefetchScalarGridSpec(
            num_scalar_prefetch=1, grid=(N_CHIPS, hr // rb),
            in_specs=[pl.BlockSpec((None, None, rb, cols), lambda j, i, c: (j, c[0], i, 0)),
                      pl.BlockSpec((None, rb, cols), lambda j, i, c: (j, i, 0))],
            out_specs=pl.BlockSpec((None, rb, cols), lambda j, i, c: (j, i, 0))),
        out_shape=jax.ShapeDtypeStruct((N_CHIPS, hr, cols), BF16),
        compiler_params=_params(("parallel", "parallel")),
    )(c_idx, g, rsib)


def _chip_sum(qb, *, name):
    _, hr, cols = qb.shape
    rb = _row_block(hr, cols, 2, 1 << 20)

    def body(q_ref, o_ref):
        acc = q_ref[0].astype(F32)
        for i in range(1, N_CHIPS):
            acc = acc + q_ref[i].astype(F32)
        o_ref[...] = acc

    return pl.pallas_call(
        body, name=name, grid=(hr // rb,),
        in_specs=[pl.BlockSpec((N_CHIPS, rb, cols), lambda i: (0, i, 0))],
        out_specs=pl.BlockSpec((rb, cols), lambda i: (i, 0)),
        out_shape=jax.ShapeDtypeStruct((hr, cols), F32), compiler_params=_params(("parallel",)),
    )(qb)


def _place():
    x, y, c = lax.axis_index("x"), lax.axis_index("y"), lax.axis_index("c")
    return x, y, c, [(1 - x, y), (x, 1 - y), (1 - x, 1 - y)]


ANY = pl.BlockSpec(memory_space=pl.ANY)


def _gather_side(shards):
    nw, n_k = len(shards), 8

    def plan(ins, outs, ssem, rsem):
        x, y, c, _ = _place()
        nbr_x, nbr_y, sib = (1 - x, y, c), (x, 1 - y, c), (x, y, 1 - c)
        s_me, s_x, s_y, s_d = 2 * x + y, 2 * (1 - x) + y, 2 * x + 1 - y, 2 * (1 - x) + 1 - y

        def rows(w, half, quarter=None):
            hr = shards[w].shape[0] // 2
            if quarter is None:
                return pl.ds(pl.multiple_of(half * hr, 16), hr)
            return pl.ds(pl.multiple_of(half * hr + quarter * (hr // 2), 16), hr // 2)

        def rcopy(w, k, src, slot, rws, to):
            return pltpu.make_async_remote_copy(
                src_ref=src, dst_ref=outs[w].at[slot, rws], send_sem=ssem.at[w * n_k + k],
                recv_sem=rsem.at[w * n_k + k], device_id=to, device_id_type=MESH)

        out = []
        for w in range(nw):
            mine = ins[w].at[rows(w, c)]
            q = [rows(w, c, 0), rows(w, c, 1)]
            p = functools.partial
            out.append([
                (p(rcopy, w, 0, mine, s_me, rows(w, c), nbr_x), p(rcopy, w, 0, mine, s_x, rows(w, c), nbr_x)),
                (p(rcopy, w, 1, mine, s_me, rows(w, c), nbr_y), p(rcopy, w, 1, mine, s_y, rows(w, c), nbr_y)),
                (p(rcopy, w, 2, outs[w].at[s_x, q[0]], s_x, q[0], nbr_y),
                 p(rcopy, w, 2, outs[w].at[s_x, q[0]], s_d, q[0], nbr_y)),
                (p(rcopy, w, 3, outs[w].at[s_y, q[1]], s_y, q[1], nbr_x),
                 p(rcopy, w, 3, outs[w].at[s_y, q[1]], s_d, q[1], nbr_x)),
                (p(rcopy, w, 4, outs[w].at[s_x, rows(w, c)], s_x, rows(w, c), sib),
                 p(rcopy, w, 4, mine, s_x, rows(w, 1 - c), sib)),
                (p(rcopy, w, 5, outs[w].at[s_y, rows(w, c)], s_y, rows(w, c), sib),
                 p(rcopy, w, 5, mine, s_y, rows(w, 1 - c), sib)),
                (p(rcopy, w, 6, outs[w].at[s_d, q[0]], s_d, q[0], sib),
                 p(rcopy, w, 6, outs[w].at[s_d, q[0]], s_d, rows(w, 1 - c, 0), sib)),
                (p(rcopy, w, 7, outs[w].at[s_d, q[1]], s_d, q[1], sib),
                 p(rcopy, w, 7, outs[w].at[s_d, q[1]], s_d, rows(w, 1 - c, 1), sib)),
            ])
        return out

    def send_own(*refs):
        for cps in plan(*refs):
            cps[0][0]().start()
            cps[1][0]().start()

    def relay(*refs):
        for cps in plan(*refs):
            cps[0][1]().wait_recv()
            cps[2][0]().start()
            cps[4][0]().start()
            cps[1][1]().wait_recv()
            cps[3][0]().start()
            cps[5][0]().start()

    def pass_diagonal(*refs):
        for cps in plan(*refs):
            cps[2][1]().wait_recv()
            cps[6][0]().start()
            cps[3][1]().wait_recv()
            cps[7][0]().start()

    def finish(*refs):
        for cps in plan(*refs):
            for k in (4, 5, 6, 7):
                cps[k][1]().wait_recv()
            for k in range(n_k):
                cps[k][0]().wait_send()

    return _Side(shards, [jax.ShapeDtypeStruct((N_CHIPS,) + s.shape, s.dtype) for s in shards], nw * n_k,
                 [send_own, relay, pass_diagonal, finish])


def _run_side(side, *, name):
    n_in, n_out = len(side.ins), len(side.out_shapes)

    def body(*refs):
        ins, outs = refs[:n_in], refs[n_in:n_in + n_out]
        for phase in side.phases:
            phase(ins, outs, refs[-2], refs[-1])

    return pl.pallas_call(
        body, name=name, in_specs=[ANY] * n_in, out_specs=[ANY] * n_out, out_shape=side.out_shapes,
        scratch_shapes=side.sems(), compiler_params=pltpu.CompilerParams(has_side_effects=True),
    )(*side.ins)


def _exchange_side(gs):
    nw = len(gs)

    def copies(ins, outs, ssem, rsem):
        x, y, c, _ = _place()
        return [pltpu.make_async_remote_copy(
            src_ref=ins[w].at[:, 1 - c], dst_ref=outs[w], send_sem=ssem.at[w], recv_sem=rsem.at[w],
            device_id=(x, y, 1 - c), device_id_type=MESH) for w in range(nw)]

    def start(*refs):
        for cp in copies(*refs):
            cp.start()

    def finish(*refs):
        for cp in copies(*refs):
            cp.wait()

    return _Side(gs, [jax.ShapeDtypeStruct((g.shape[0],) + g.shape[2:], g.dtype) for g in gs], nw, [start, finish])


def _scatter_side(ps):
    nw = len(ps)

    def copies(ins, outs, ssem, rsem):
        x, y, c, chips = _place()
        s_me = 2 * x + y
        return [pltpu.make_async_remote_copy(
            src_ref=ins[w].at[2 * chip[0] + chip[1]], dst_ref=outs[w].at[s_me],
            send_sem=ssem.at[w * 3 + j], recv_sem=rsem.at[w * 3 + j],
            device_id=(chip[0], chip[1], c), device_id_type=MESH)
            for w in range(nw) for j, chip in enumerate(chips)]

    def start(*refs):
        for cp in copies(*refs):
            cp.start()

    def finish(*refs):
        for cp in copies(*refs):
            cp.wait()

    return _Side(ps, [jax.ShapeDtypeStruct(p.shape, p.dtype) for p in ps], nw * 3, [start, finish])


def _pair_share(hs, *, name):
    nw = len(hs)

    def body(*refs):
        ins, outs = refs[:nw], refs[nw:2 * nw]
        ssem, rsem = refs[2 * nw:]
        x, y, c, _ = _place()
        cps = []
        for w in range(nw):
            cp = pltpu.make_async_remote_copy(
                src_ref=ins[w], dst_ref=outs[w], send_sem=ssem.at[w], recv_sem=rsem.at[w],
                device_id=(x, y, 1 - c), device_id_type=MESH)
            cp.start()
            cps.append(cp)
        for cp in cps:
            cp.wait()

    return pl.pallas_call(
        body, name=name, in_specs=[ANY] * nw, out_specs=[ANY] * nw,
        out_shape=[jax.ShapeDtypeStruct(h.shape, h.dtype) for h in hs],
        scratch_shapes=[pltpu.SemaphoreType.DMA((nw,)), pltpu.SemaphoreType.DMA((nw,))],
        compiler_params=pltpu.CompilerParams(has_side_effects=True),
    )(*hs)


def _gather_small(a, *, name):
    def body(a_ref, o_ref, ssem, rsem):
        x, y, c, chips = _place()
        s_me = 2 * x + y
        o_ref[s_me] = a_ref[...]
        cps = []
        for j, chip in enumerate(chips):
            cp = pltpu.make_async_remote_copy(
                src_ref=a_ref, dst_ref=o_ref.at[s_me], send_sem=ssem.at[j], recv_sem=rsem.at[j],
                device_id=(chip[0], chip[1], c), device_id_type=MESH)
            cp.start()
            cps.append(cp)
        for cp in cps:
            cp.wait()

    vm = pl.BlockSpec(memory_space=pltpu.VMEM)
    return pl.pallas_call(
        body, name=name, in_specs=[vm], out_specs=vm,
        out_shape=jax.ShapeDtypeStruct((N_CHIPS,) + a.shape, a.dtype),
        scratch_shapes=[pltpu.SemaphoreType.DMA((3,)), pltpu.SemaphoreType.DMA((3,))],
    )(a)


def _allreduce_small(p, *, name):
    def body(p_ref, o_ref, buf, ssem, rsem):
        x, y, c, _ = _place()
        me = 4 * x + 2 * y + c
        buf[me] = p_ref[...]
        cps = []
        for k in range(1, N_DEV):
            fx, fy, fc = (k >> 2) & 1, (k >> 1) & 1, k & 1
            peer = (x + fx - 2 * x * fx, y + fy - 2 * y * fy, c + fc - 2 * c * fc)
            cp = pltpu.make_async_remote_copy(
                src_ref=buf.at[me], dst_ref=buf.at[me], send_sem=ssem.at[k - 1], recv_sem=rsem.at[k - 1],
                device_id=peer, device_id_type=MESH)
            cp.start()
            cps.append(cp)
        for cp in cps:
            cp.wait()
        acc = buf[0]
        for i in range(1, N_DEV):
            acc = acc + buf[i]
        o_ref[...] = acc

    vm = pl.BlockSpec(memory_space=pltpu.VMEM)
    return pl.pallas_call(
        body, name=name, in_specs=[vm], out_specs=vm,
        out_shape=jax.ShapeDtypeStruct(p.shape, p.dtype),
        scratch_shapes=[pltpu.VMEM((N_DEV,) + p.shape, p.dtype), pltpu.SemaphoreType.DMA((N_DEV - 1,)),
                        pltpu.SemaphoreType.DMA((N_DEV - 1,))],
    )(p)


def _rows_to_tokens(r):
    h, n = r.shape[0], r.shape[1]
    return r.reshape(h, n * CHUNK).T


def _tokens_to_rows(a):
    t, h = a.shape
    return a.T.reshape(h, t // CHUNK, 1, CHUNK)


def kernel(x, norm1_w, w_in, conv_w, a_log, dt_bias, gdn_norm_w, ret_norm_w, w_out, norm2_w, w_gate, w_up, w_down, norm_f_w, loss_target, m_norm1_w, m_w_in, m_conv_w, m_a_log, m_dt_bias, m_gdn_norm_w, m_ret_norm_w, m_w_out, m_norm2_w, m_w_gate, m_w_up, m_w_down, m_norm_f_w, v_norm1_w, v_w_in, v_conv_w, v_a_log, v_dt_bias, v_gdn_norm_w, v_ret_norm_w, v_w_out, v_norm2_w, v_w_gate, v_w_up, v_w_down, v_norm_f_w):
    t, d = x.shape[1], x.shape[2]
    f = w_gate.shape[2] * N_CHIPS
    nqk, nv = HEADS * DK, HEADS * DV
    ncs = w_in.shape[2]
    c_idx = lax.axis_index("c")
    s_idx = 2 * lax.axis_index("x") + lax.axis_index("y")
    xs = x[0]
    tgt = loss_target[0]

    n_small = 2 * HEADS
    widths = [2 * nqk + nv, nv, n_small, nqk, nqk, nv, nv, d, d]
    g_off = np.concatenate([[0], np.cumsum(widths)])
    order = [0, 3, 4, 5, 1, 6, 7, 8]
    m_off = np.concatenate([[0], np.cumsum([widths[i] for i in order])])
    o_rq, o_rk, o_rv, o_az, o_rg, o_ga, o_gb = [int(m_off[i]) for i in range(1, 8)]
    segs = [(int(g_off[i]), int(g_off[i + 1]), int(m_off[order.index(i)]) if i != 2 else None) for i in range(9)]

    def shard_pieces(s):
        out = []
        for a, b, dst in segs:
            lo, hi = max(a, s * ncs), min(b, (s + 1) * ncs)
            if lo < hi:
                out.append((lo - s * ncs, hi - s * ncs, None if dst is None else dst + lo - a))
        return out

    own = [w_in[0].astype(BF16), w_out[0].astype(BF16), w_gate[0].astype(BF16), w_up[0].astype(BF16),
           w_down[0].astype(BF16)]
    put_own = lambda full, mine: lax.dynamic_update_slice(full, mine[None], (s_idx, 0, 0))
    wg_in = put_own(_run_side(_gather_side(own[:1]), name="gather_w_in")[0], own[0])
    conv_full = _gather_small(conv_w[0], name="gather_conv_w")
    conv_full = jnp.concatenate([conv_full[i] for i in range(N_CHIPS)], axis=1)
    cuts = [(dst, s, a, b) for s in range(N_CHIPS) for a, b, dst in shard_pieces(s)]
    w_main = jnp.concatenate([wg_in[s][:, a:b] for dst, s, a, b in sorted(c for c in cuts if c[0] is not None)],
                             axis=1)
    w_small = jnp.concatenate([wg_in[s][:, a:b] for dst, s, a, b in cuts if dst is None], axis=1)
    w_small = jnp.pad(w_small, ((0, 0), (0, LANES - n_small)))

    pad16 = lambda a: jnp.pad(a, ((0, 0), (HEADS, LANES - 2 * HEADS)))
    alog_row, dtb_row = pad16(a_log), pad16(dt_bias)
    wa_row = jnp.tile(gdn_norm_w, (1, HEADS))
    inv = ROPE_BASE ** (-jnp.arange(0, DK, 2, dtype=F32) / DK)
    ang = jnp.arange(t, dtype=F32)[:, None] * inv[None, :]
    cos2 = jnp.concatenate([jnp.cos(ang), jnp.cos(ang)], axis=1)
    sin2 = jnp.concatenate([-jnp.sin(ang), jnp.sin(ang)], axis=1)
    lg = jnp.log1p(-jnp.exp2(-5.0 - jnp.arange(HEADS, dtype=F32)))
    lg_tab = jnp.broadcast_to(lg[:, None, None], (HEADS, 1, LANES))

    fq = f // N_CHIPS
    u1, projs = _rms_fwd(xs, norm1_w, w_narrow=w_small, name="rms1_fwd")
    proj, *rest = _mm(u1, w_main, tm=512, tn=2048, tk=d, side=_gather_side(own[1:]), name="mm_proj")
    wg_out, wg_gate, wg_up, wg_down = [put_own(g, o) for g, o in zip(rest, own[1:])]
    w_o = wg_out.reshape(d, d)
    w_g, w_u = wg_gate, wg_up
    w_d = wg_down.reshape(f, d)
    q_a, k_a, v_a = _conv_fwd(proj, conv_full, name="conv_fwd")
    bg = _bg_fwd(projs, alog_row, dtb_row, name="bg_fwd")
    beta_r = _tokens_to_rows(bg[:, :HEADS])
    g_r = _tokens_to_rows(bg[:, HEADS:2 * HEADS])
    gc_r, tinv, u_a, w_a, p_a = _gdn_prep(q_a, k_a, v_a, beta_r, g_r, name="gdn_prep")
    o_a, s_a, o_b, s_b = _mix_fwd(q_a, k_a, u_a, w_a, p_a, gc_r, proj, cos2, sin2, lg_tab, o_rq, o_rk, o_rv,
                                  name="mix_fwd")
    mixed = _merge_fwd(o_a, o_b, proj, wa_row, ret_norm_w, o_az, o_rg, o_ga, o_gb, name="merge_fwd")
    h1, hn = _mm_res_norm(mixed, w_o, xs, norm2_w, tm=512, name="mm_out")
    gt, up, act = _ffn_in(hn, w_g, w_u, tm=512, name="ffn_in")
    h2 = _mm(act, w_d, res=h1, tm=512, tn=1024, tk=f, name="mm_down")
    loss_row, dh2, dh2b, d_nf = _loss_head(h2, tgt, norm_f_w.reshape(1, d), name="loss_head")

    g_down = _mm(act, dh2b, ta=True, out_dtype=BF16, tm=fq, tn=d, tk=1024, name="mm_dw_down")
    dgt, dup = _ffn_back(dh2b, w_d, gt, up, tm=512, tn=fq, name="ffn_back")
    dhn = _mm(dgt, w_g, tb=True, pair=(dup, w_u), tm=512, tn=d, tk=fq, name="mm_dhn")
    g_gate = _mm(hn, dgt, ta=True, out_dtype=BF16, out_stacked=True, tm=1024, tn=fq, tk=2048, name="mm_dw_gate")
    g_up = _mm(hn, dup, ta=True, out_dtype=BF16, out_stacked=True, tm=1024, tn=fq, tk=2048, name="mm_dw_up")
    dh1, dh1b, d_n2 = _rms_bwd(dhn, h1, norm2_w, dh2, bf16_copy=True, name="rms2_bwd")
    dmix = _mm(dh1b, w_o, tb=True, tm=512, tn=d, tk=d, name="mm_dmix")
    g_out = _mm(mixed, dh1b, ta=True, out_dtype=BF16, tm=1024, tn=d, tk=1024, name="mm_dw_out")
    halves = lambda g: g.reshape(N_CHIPS, 2, g.shape[1] // 2, g.shape[2])
    c_arr = jnp.reshape(c_idx, (1,)).astype(jnp.int32)
    gs_ffn = [halves(g_out.reshape(N_CHIPS, d // N_CHIPS, d)), halves(g_gate), halves(g_up),
              halves(g_down.reshape(N_CHIPS, fq, d))]
    do_a, do_b, dproj, d_wa, d_wb, *rsib = _merge_bwd(
        dmix, o_a, o_b, proj, wa_row, ret_norm_w, o_az, o_rg, o_ga, o_gb, side=_exchange_side(gs_ffn),
        name="merge_bwd")
    ps_ffn = [_pair_add(g, r, c_arr, name=f"grad_pair_add_{nm}")
              for g, r, nm in zip(gs_ffn, rsib, ["w_out", "w_gate", "w_up", "w_down"])]
    dq_a, dk_a, dv_a, dbeta_r, dg_r, dproj = _mix_bwd(
        q_a, k_a, v_a, beta_r, gc_r, tinv, u_a, w_a, s_a, do_a, proj, cos2, sin2, lg_tab, s_b, do_b, dproj,
        o_rq, o_rk, o_rv, name="mix_bwd")
    dc = _conv_bwd_pre(proj, conv_full, dq_a, dk_a, dv_a, name="conv_bwd_pre")
    dproj, d_cw = _conv_bwd(proj, dc, conv_full, dproj, name="conv_bwd")
    dbg = jnp.pad(jnp.concatenate([_rows_to_tokens(dbeta_r), _rows_to_tokens(dg_r)], axis=1),
                  ((0, 0), (0, LANES - 2 * HEADS)))
    dprojs, d_alog, d_dtb = _bg_bwd(projs, dbg, alog_row, dtb_row, name="bg_bwd")
    g_main, *qs_ffn = _mm(u1, dproj, ta=True, out_dtype=BF16, tm=1024, tn=2048, tk=2048,
                          side=_scatter_side(ps_ffn), name="mm_dw_in")
    g_small = _mm(u1, dprojs, ta=True, out_dtype=BF16, tm=1024, tn=LANES, tk=1024, name="mm_dw_in_small")
    pieces = []
    for s in range(N_CHIPS):
        seen, parts = 0, []
        for a, b, dst in shard_pieces(s):
            parts.append(g_small[:, seen:seen + b - a] if dst is None else g_main[:, dst:dst + b - a])
            seen += (b - a) if dst is None else 0
        pieces.append(jnp.concatenate(parts, axis=1))
    gs_in = [halves(jnp.stack(pieces))]
    rsib = _run_side(_exchange_side(gs_in), name="grad_pair_exchange_in")
    ps_in = [_pair_add(gs_in[0], rsib[0], c_arr, name="grad_pair_add_w_in")]
    du, *qs_in = _mm(dproj, w_main, tb=True, out_dtype=BF16, tm=1024, tn=d, tk=2048,
                     side=_scatter_side(ps_in), name="mm_du")
    dx, d_n1 = _rms_bwd(du, xs, norm1_w, dh1, bf16_copy=False, narrow=(dprojs, w_small), name="rms1_bwd")

    names = ["w_in", "w_out", "w_gate", "w_up", "w_down"]
    qs = [lax.dynamic_update_slice(q, lax.dynamic_slice(p, (s_idx, 0, 0), (1,) + p.shape[1:]), (s_idx, 0, 0))
          for q, p in zip(qs_in + qs_ffn, ps_in + ps_ffn)]
    hs = [_chip_sum(q, name=f"grad_chip_sum_{nm}") for q, nm in zip(qs, names)]
    theirs = _pair_share(hs, name="grad_pair_share")
    big_w = [w_in[0], w_out[0], w_gate[0], w_up[0], w_down[0]]
    big_m = [m_w_in[0], m_w_out[0], m_w_gate[0], m_w_up[0], m_w_down[0]]
    big_v = [v_w_in[0], v_w_out[0], v_w_gate[0], v_w_up[0], v_w_down[0]]
    big = {}
    for nm, mine, other, w_, m_, v_ in zip(names, hs, theirs, big_w, big_m, big_v):
        big[nm] = tuple(a[None] for a in _adamw_halves(w_, mine, other, c_arr, m_, v_, name=f"adamw_{nm}"))

    d_wa_h = jnp.sum(d_wa.reshape(HEADS, DV), axis=0, keepdims=True)
    small = [d_n1, d_alog[:, HEADS:2 * HEADS], d_dtb[:, HEADS:2 * HEADS], d_wa_h, d_wb, d_n2, d_nf,
             d_cw[:CONV_W].reshape(1, -1)]
    sizes = [a.shape[1] for a in small]
    packed = jnp.concatenate(small, axis=1)
    n_pack = packed.shape[1]
    n_rows = -(-n_pack // LANES)
    n_rows = -(-n_rows // 8) * 8
    packed = jnp.pad(packed, ((0, 0), (0, n_rows * LANES - n_pack))).reshape(n_rows, LANES)
    red = _allreduce_small(packed, name="allreduce_small").reshape(1, -1)
    offs = np.cumsum([0] + sizes)
    g_n1, g_alog, g_dtb, g_wa, g_wb, g_n2, g_nf, g_cw = [red[:, offs[i]:offs[i + 1]] for i in range(len(sizes))]
    ncw = conv_w.shape[2]
    g_cw = lax.dynamic_slice(g_cw.reshape(CONV_W, -1), (0, s_idx * ncw), (CONV_W, ncw))

    def small_update(w_, g, m_, v_, nm):
        shape = w_.shape
        pad = (-w_.size) % LANES
        to2 = lambda a: jnp.pad(a.reshape(1, -1), ((0, 0), (0, pad)))
        outs = _adamw(to2(w_), to2(g), to2(m_), to2(v_), name=f"adamw_{nm}")
        return (g.reshape(shape),) + tuple(a[:, :w_.size].reshape(shape) for a in outs)

    res = {
        "norm1_w": small_update(norm1_w, g_n1, m_norm1_w, v_norm1_w, "norm1_w"),
        "w_in": big["w_in"],
        "conv_w": small_update(conv_w, g_cw, m_conv_w, v_conv_w, "conv_w"),
        "a_log": small_update(a_log, g_alog, m_a_log, v_a_log, "a_log"),
        "dt_bias": small_update(dt_bias, g_dtb, m_dt_bias, v_dt_bias, "dt_bias"),
        "gdn_norm_w": small_update(gdn_norm_w, g_wa, m_gdn_norm_w, v_gdn_norm_w, "gdn_norm_w"),
        "ret_norm_w": small_update(ret_norm_w, g_wb, m_ret_norm_w, v_ret_norm_w, "ret_norm_w"),
        "w_out": big["w_out"],
        "norm2_w": small_update(norm2_w, g_n2, m_norm2_w, v_norm2_w, "norm2_w"),
        "w_gate": big["w_gate"],
        "w_up": big["w_up"],
        "w_down": big["w_down"],
        "norm_f_w": small_update(norm_f_w, g_nf, m_norm_f_w, v_norm_f_w, "norm_f_w"),
    }
    order = ["norm1_w", "w_in", "conv_w", "a_log", "dt_bias", "gdn_norm_w", "ret_norm_w", "w_out", "norm2_w",
             "w_gate", "w_up", "w_down", "norm_f_w"]
    loss = lax.psum(loss_row[0, 0], ("x", "y", "c"))
    return (loss, dx[None], *[res[n][0] for n in order], *[res[n][1] for n in order],
            *[res[n][2] for n in order], *[res[n][3] for n in order])
```

```python
import functools

import jax
import jax.numpy as jnp
import numpy as np
from jax import lax
from jax.experimental import pallas as pl
from jax.experimental.pallas import tpu as pltpu

F32 = jnp.float32
BF16 = jnp.bfloat16
MESH = pl.DeviceIdType.MESH

HEADS = 8
DK = 128
DV = 256
CHUNK = 64
CONV_W = 4
EPS = 1e-6
ROPE_BASE = 10000.0
ADAM_LR, ADAM_B1, ADAM_B2, ADAM_EPS, ADAM_WD, ADAM_STEP = 0.001, 0.9, 0.999, 1e-08, 0.01, 10
N_CHIPS = 4
N_DEV = 8
LANES = 128
HALO = 8
VMEM_LIMIT = 56 * 1024 * 1024
HB = 8


def _pick(n, cands):
    for c in cands:
        if n % c == 0:
            return c
    raise ValueError(f"no tile for {n} in {cands}")


def _params(sem=None):
    return pltpu.CompilerParams(dimension_semantics=sem, vmem_limit_bytes=VMEM_LIMIT)


def _dot(a, b):
    return jnp.dot(a.astype(BF16), b.astype(BF16), preferred_element_type=F32)


def _dot_nt(a, b):
    return lax.dot_general(a.astype(BF16), b.astype(BF16), (((1,), (1,)), ((), ())), preferred_element_type=F32)


def _dot_tn(a, b):
    return lax.dot_general(a.astype(BF16), b.astype(BF16), (((0,), (0,)), ((), ())), preferred_element_type=F32)


def _sigmoid(x):
    return 1.0 / (1.0 + jnp.exp(-x))


def _iota2(n):
    return lax.broadcasted_iota(jnp.int32, (n, n), 0), lax.broadcasted_iota(jnp.int32, (n, n), 1)


def _row_to_col(row):
    n = row.shape[1]
    r, c = _iota2(n)
    return jnp.sum(jnp.where(r == c, jnp.broadcast_to(row, (n, n)), 0.0), axis=1, keepdims=True)


def _col_to_row(col):
    n = col.shape[0]
    r, c = _iota2(n)
    return jnp.sum(jnp.where(r == c, jnp.broadcast_to(col, (n, n)), 0.0), axis=0, keepdims=True)


class _Side:
    def __init__(self, ins, out_shapes, n_sem, phases):
        self.ins, self.out_shapes, self.n_sem, self.phases = list(ins), list(out_shapes), n_sem, phases

    def sems(self):
        return [pltpu.SemaphoreType.DMA((self.n_sem,)), pltpu.SemaphoreType.DMA((self.n_sem,))]


def _mm(a, b, *, name, tm, tn, tk, ta=False, tb=False, out_dtype=F32, res=None, side=None, pair=None,
        out_stacked=False):
    m, k = (a.shape[1], a.shape[0]) if ta else a.shape
    b_slots = b.ndim == 3
    if b_slots:
        assert tb and tk == b.shape[2]
        n = b.shape[1]
    else:
        n = b.shape[0] if tb else b.shape[1]
    tm, tn, tk = min(tm, m), min(tn, n), min(tk, k)
    assert m % tm == 0 and n % tn == 0 and k % tk == 0, (name, m, n, k)
    nk = k // tk
    nj, ni = n // tn, m // tm
    dn = (((0 if ta else 1,), (1 if tb else 0,)), ((), ()))
    n_ab = 4 if pair else 2
    n_in = n_ab + (res is not None)
    n_side_in = len(side.ins) if side else 0
    n_side_out = len(side.out_shapes) if side else 0

    def body(*refs):
        a_ref, b_ref = refs[0], refs[1]
        r_ref = refs[n_ab] if res is not None else None
        o_ref = refs[n_in + n_side_in]
        if side:
            s_in = refs[n_in:n_in + n_side_in]
            s_out = refs[n_in + n_side_in + 1:n_in + n_side_in + 1 + n_side_out]
            ssem, rsem = refs[-2], refs[-1]
            j_, i_, k_ = pl.program_id(0), pl.program_id(1), pl.program_id(2)
            n_mid = len(side.phases) - 2
            assert n_mid == 0 or nj >= 2 * n_mid
            when = [(j_ == 0) & (i_ == 0) & (k_ == 0)]
            when += [(j_ == nj // 2 + (p * (nj // 2)) // n_mid) & (i_ == 0) & (k_ == 0) for p in range(n_mid)]
            when.append((j_ == nj - 1) & (i_ == ni - 1) & (k_ == nk - 1))

        def run_phase(p):
            @pl.when(when[p])
            def _():
                side.phases[p](s_in, s_out, ssem, rsem)

        if side:
            for p in range(len(side.phases) - 1):
                run_phase(p)

        def finish(r):
            if res is not None:
                r = r + r_ref[...]
            o_ref[...] = r.astype(out_dtype)

        part = lax.dot_general(a_ref[...], b_ref[...], dn, preferred_element_type=F32)
        if pair:
            part = part + lax.dot_general(refs[2][...], refs[3][...], dn, preferred_element_type=F32)
        if nk == 1:
            finish(part)
        else:
            acc = refs[n_in + n_side_in + 1 + n_side_out]
            kk = pl.program_id(2)

            @pl.when(kk == 0)
            def _():
                acc[...] = part

            @pl.when((kk > 0) & (kk < nk - 1))
            def _():
                acc[...] += part

            @pl.when(kk == nk - 1)
            def _():
                finish(acc[...] + part)

        if side:
            run_phase(len(side.phases) - 1)

    a_spec = pl.BlockSpec((tk, tm), lambda j, i, kk: (kk, i)) if ta else pl.BlockSpec((tm, tk), lambda j, i, kk: (i, kk))
    b_spec = pl.BlockSpec((tn, tk), lambda j, i, kk: (j, kk)) if tb else pl.BlockSpec((tk, tn), lambda j, i, kk: (kk, j))
    if b_slots:
        b_spec = pl.BlockSpec((None, tn, tk), lambda j, i, kk: (kk, j, 0))
    o_spec = pl.BlockSpec((tm, tn), lambda j, i, kk: (i, j))
    o_shape = jax.ShapeDtypeStruct((m, n), out_dtype)
    if out_stacked:
        assert res is None
        o_spec = pl.BlockSpec((None, tm, tn), lambda j, i, kk: (j, i, 0))
        o_shape = jax.ShapeDtypeStruct((nj, m, tn), out_dtype)
    in_specs, args = [a_spec, b_spec], [a, b]
    if pair:
        assert pair[0].shape == a.shape and pair[1].shape == b.shape
        in_specs += [a_spec, b_spec]
        args += list(pair)
    if res is not None:
        in_specs.append(o_spec)
        args.append(res)
    out_specs, out_shape = o_spec, o_shape
    scratch = [pltpu.VMEM((tm, tn), F32)] if nk > 1 else []
    sem = ("parallel", "parallel", "arbitrary")
    if side:
        hbm = pl.BlockSpec(memory_space=pl.ANY)
        in_specs += [hbm] * n_side_in
        args += side.ins
        out_specs, out_shape = [o_spec] + [hbm] * n_side_out, [out_shape] + side.out_shapes
        scratch += side.sems()
        sem = ("arbitrary",) * 3
    return pl.pallas_call(
        body, name=name, grid=(nj, ni, nk), in_specs=in_specs, out_specs=out_specs, out_shape=out_shape,
        scratch_shapes=scratch, compiler_params=_params(sem),
    )(*args)


def _mm_res_norm(a, b, res, w, *, name, tm):
    m, k = a.shape
    n = b.shape[1]
    tm = min(tm, m)

    def body(a_ref, b_ref, r_ref, w_ref, h_ref, hn_ref):
        h = jnp.dot(a_ref[...], b_ref[...], preferred_element_type=F32) + r_ref[...]
        h_ref[...] = h
        r = lax.rsqrt(jnp.mean(h * h, axis=-1, keepdims=True) + EPS)
        hn_ref[...] = (h * r * w_ref[...]).astype(BF16)

    row = pl.BlockSpec((tm, n), lambda i: (i, 0))
    return pl.pallas_call(
        body, name=name, grid=(m // tm,),
        in_specs=[pl.BlockSpec((tm, k), lambda i: (i, 0)), pl.BlockSpec((k, n), lambda i: (0, 0)), row,
                  pl.BlockSpec((1, n), lambda i: (0, 0))],
        out_specs=[row, row],
        out_shape=[jax.ShapeDtypeStruct((m, n), F32), jax.ShapeDtypeStruct((m, n), BF16)],
        compiler_params=_params(("parallel",)),
    )(a, b, res, w)


def _rms_fwd(x, w, *, name, w_narrow=None):
    t, d = x.shape
    tt = _pick(t, (512, 256))

    def body(*refs):
        x_ref, w_ref, o_ref = refs[0], refs[1], refs[-2 if w_narrow is not None else -1]
        xv = x_ref[...]
        r = lax.rsqrt(jnp.mean(xv * xv, axis=-1, keepdims=True) + EPS)
        u = (xv * r * w_ref[...]).astype(BF16)
        o_ref[...] = u
        if w_narrow is not None:
            refs[-1][...] = jnp.dot(u, refs[2][...], preferred_element_type=F32)

    blk = pl.BlockSpec((tt, d), lambda i: (i, 0))
    in_specs, args = [blk, pl.BlockSpec((1, d), lambda i: (0, 0))], [x, w]
    out_specs, out_shape = [blk], [jax.ShapeDtypeStruct((t, d), BF16)]
    if w_narrow is not None:
        in_specs.append(pl.BlockSpec(w_narrow.shape, lambda i: (0, 0)))
        args.append(w_narrow)
        out_specs.append(pl.BlockSpec((tt, LANES), lambda i: (i, 0)))
        out_shape.append(jax.ShapeDtypeStruct((t, LANES), F32))
    out = pl.pallas_call(
        body, name=name, grid=(t // tt,), in_specs=in_specs, out_specs=out_specs, out_shape=out_shape,
        compiler_params=_params(("parallel",)),
    )(*args)
    return out if w_narrow is not None else out[0]


def _rms_bwd(dn, x, w, dres, *, name, bf16_copy, narrow=None):
    t, d = x.shape
    tt = _pick(t, (256,))

    def body(*refs):
        dn_ref, x_ref, w_ref, dres_ref = refs[:4]
        dx_ref, dw_ref = refs[-3 if bf16_copy else -2], refs[-1]
        xv, g = x_ref[...], dn_ref[...].astype(F32)
        if narrow is not None:
            g = g + lax.dot_general(refs[4][...], refs[5][...], (((1,), (1,)), ((), ())), preferred_element_type=F32)
        r = lax.rsqrt(jnp.mean(xv * xv, axis=-1, keepdims=True) + EPS)
        xh = xv * r
        gw = g * w_ref[...]
        dx = dres_ref[...] + r * (gw - xh * jnp.mean(gw * xh, axis=-1, keepdims=True))
        dx_ref[...] = dx
        if bf16_copy:
            refs[-2][...] = dx.astype(BF16)

        @pl.when(pl.program_id(0) == 0)
        def _():
            dw_ref[...] = jnp.zeros_like(dw_ref)

        dw_ref[...] += jnp.sum(g * xh, axis=0, keepdims=True)

    blk = pl.BlockSpec((tt, d), lambda i: (i, 0))
    row = pl.BlockSpec((1, d), lambda i: (0, 0))
    in_specs, args = [blk, blk, row, blk], [dn, x, w, dres]
    if narrow is not None:
        in_specs += [pl.BlockSpec((tt, LANES), lambda i: (i, 0)), pl.BlockSpec(narrow[1].shape, lambda i: (0, 0))]
        args += list(narrow)
    copy_spec, copy_shape = ([blk], [jax.ShapeDtypeStruct((t, d), BF16)]) if bf16_copy else ([], [])
    return pl.pallas_call(
        body, name=name, grid=(t // tt,), in_specs=in_specs, out_specs=[blk] + copy_spec + [row],
        out_shape=[jax.ShapeDtypeStruct((t, d), F32)] + copy_shape + [jax.ShapeDtypeStruct((1, d), F32)],
        compiler_params=_params(("arbitrary",)),
    )(*args)


def _loss_head(h2, tgt, wf, *, name):
    t, d = h2.shape
    tt = _pick(t, (256,))

    def body(x_ref, t_ref, w_ref, loss_ref, dx_ref, dxb_ref, dw_ref):
        xv = x_ref[...]
        r = lax.rsqrt(jnp.mean(xv * xv, axis=-1, keepdims=True) + EPS)
        xh = xv * r
        err = xh * w_ref[...] - t_ref[...]
        lpart = 0.5 * jnp.sum(jnp.mean(err * err, axis=-1, keepdims=True), axis=0, keepdims=True)
        dy = err * (1.0 / d)
        gw = dy * w_ref[...]
        dx = r * (gw - xh * jnp.mean(gw * xh, axis=-1, keepdims=True))
        dx_ref[...] = dx
        dxb_ref[...] = dx.astype(BF16)

        @pl.when(pl.program_id(0) == 0)
        def _():
            dw_ref[...] = jnp.zeros_like(dw_ref)
            loss_ref[...] = jnp.zeros_like(loss_ref)

        dw_ref[...] += jnp.sum(dy * xh, axis=0, keepdims=True)
        loss_ref[...] += jnp.broadcast_to(lpart, loss_ref.shape)

    blk = pl.BlockSpec((tt, d), lambda i: (i, 0))
    row = pl.BlockSpec((1, d), lambda i: (0, 0))
    lrow = pl.BlockSpec((1, LANES), lambda i: (0, 0))
    return pl.pallas_call(
        body, name=name, grid=(t // tt,),
        in_specs=[blk, blk, row], out_specs=[lrow, blk, blk, row],
        out_shape=[jax.ShapeDtypeStruct((1, LANES), F32), jax.ShapeDtypeStruct((t, d), F32),
                   jax.ShapeDtypeStruct((t, d), BF16), jax.ShapeDtypeStruct((1, d), F32)],
        compiler_params=_params(("arbitrary",)),
    )(h2, tgt, wf)


def _ffn_in(hn, w_g, w_u, *, name, tm):
    t, d = hn.shape
    tn = w_g.shape[2]
    f = w_g.shape[0] * tn
    tm = min(tm, t)

    def body(a_ref, g_ref, u_ref, gt_ref, up_ref, act_ref):
        a = a_ref[...]
        g = jnp.dot(a, g_ref[...], preferred_element_type=F32)
        u = jnp.dot(a, u_ref[...], preferred_element_type=F32)
        gt_ref[...] = g.astype(BF16)
        up_ref[...] = u.astype(BF16)
        act_ref[...] = (g * _sigmoid(g) * u).astype(BF16)

    wblk = pl.BlockSpec((None, d, tn), lambda j, i: (j, 0, 0))
    oblk = pl.BlockSpec((tm, tn), lambda j, i: (i, j))
    return pl.pallas_call(
        body, name=name, grid=(f // tn, t // tm),
        in_specs=[pl.BlockSpec((tm, d), lambda j, i: (i, 0)), wblk, wblk], out_specs=[oblk] * 3,
        out_shape=[jax.ShapeDtypeStruct((t, f), BF16)] * 3,
        compiler_params=_params(("parallel", "parallel")),
    )(hn, w_g, w_u)


def _ffn_back(dh2b, w_d, gt, up, *, name, tm, tn):
    t, d = dh2b.shape
    f = w_d.shape[0]
    tm = min(tm, t)

    n_sub = 2 if tm % 32 == 0 else 1

    def body(a_ref, w_ref, g_ref, u_ref, dg_ref, du_ref):
        w = w_ref[...]
        subs = [slice(i * (tm // n_sub), (i + 1) * (tm // n_sub)) for i in range(n_sub)]
        das = [lax.dot_general(a_ref[rs, :], w, (((1,), (1,)), ((), ())), preferred_element_type=F32) for rs in subs]
        for rs, da in zip(subs, das):
            g = g_ref[rs, :].astype(F32)
            sg = _sigmoid(g)
            dg_ref[rs, :] = (da * u_ref[rs, :].astype(F32) * sg * (1.0 + g * (1.0 - sg))).astype(BF16)
            du_ref[rs, :] = (da * g * sg).astype(BF16)

    oblk = pl.BlockSpec((tm, tn), lambda j, i: (i, j))
    return pl.pallas_call(
        body, name=name, grid=(f // tn, t // tm),
        in_specs=[pl.BlockSpec((tm, d), lambda j, i: (i, 0)), pl.BlockSpec((tn, d), lambda j, i: (j, 0)), oblk, oblk],
        out_specs=[oblk] * 2,
        out_shape=[jax.ShapeDtypeStruct((t, f), BF16)] * 2,
        compiler_params=_params(("parallel", "parallel")),
    )(dh2b, w_d, gt, up)


CONV_HEAD = 2 * HALO


def _causal_conv(x_ref, xp_ref, cw_ref, xs, tt):
    first = pl.program_id(0) == 0
    xs[0:HALO, :] = jnp.where(first, 0.0, xp_ref[...])
    xs[HALO:HALO + CONV_HEAD, :] = x_ref[0:CONV_HEAD, :]
    xb = x_ref[...].astype(BF16)
    r, c = _iota2(tt)
    moved = [None] + [jnp.dot((c == r - sh).astype(BF16), xb, preferred_element_type=F32)
                      for sh in range(1, CONV_W)]

    def cols(cb):
        cs = slice(cb * LANES, (cb + 1) * LANES)
        c_head = c_main = None
        for j in range(CONV_W):
            sh = CONV_W - 1 - j
            w_j = cw_ref[j:j + 1, cs]
            x_m = x_ref[CONV_HEAD:tt, cs] if sh == 0 else moved[sh][CONV_HEAD:tt, cs]
            x_h = xs[HALO - sh:HALO - sh + CONV_HEAD, cs]
            c_main = x_m * w_j if c_main is None else c_main + x_m * w_j
            c_head = x_h * w_j if c_head is None else c_head + x_h * w_j
        return jnp.concatenate([c_head, c_main], axis=0)

    return cols


def _conv_specs(tt, ch):
    cur = pl.BlockSpec((tt, ch), lambda i: (i, 0))
    prev = pl.BlockSpec((HALO, ch), lambda i: (jnp.maximum(i * (tt // HALO) - 1, 0), 0))
    return cur, prev


def _conv_fwd(proj, conv_w, *, name):
    t = proj.shape[0]
    ch = conv_w.shape[1]
    nqk = HEADS * DK
    tt = _pick(t, (256,))
    cur, prev = _conv_specs(tt, ch)

    def body(x_ref, xp_ref, cw_ref, q_ref, k_ref, v_ref, xs):
        conv = _causal_conv(x_ref, xp_ref, cw_ref, xs, tt)
        for cb in range(ch // LANES):
            c = conv(cb)
            s = c * _sigmoid(c)
            if cb < 2 * HEADS:
                s = s * lax.rsqrt(jnp.sum(s * s, axis=-1, keepdims=True) + EPS)
                if cb < HEADS:
                    q_ref[:, cb * LANES:(cb + 1) * LANES] = (s * (DK ** -0.5)).astype(BF16)
                else:
                    k_ref[:, (cb - HEADS) * LANES:(cb - HEADS + 1) * LANES] = s.astype(BF16)
            else:
                v_ref[:, (cb - 2 * HEADS) * LANES:(cb - 2 * HEADS + 1) * LANES] = s.astype(BF16)

    return pl.pallas_call(
        body, name=name, grid=(t // tt,),
        in_specs=[cur, prev, pl.BlockSpec((CONV_W, ch), lambda i: (0, 0))],
        out_specs=[pl.BlockSpec((tt, nqk), lambda i: (i, 0)), pl.BlockSpec((tt, nqk), lambda i: (i, 0)),
                   pl.BlockSpec((tt, ch - 2 * nqk), lambda i: (i, 0))],
        out_shape=[jax.ShapeDtypeStruct((t, nqk), BF16), jax.ShapeDtypeStruct((t, nqk), BF16),
                   jax.ShapeDtypeStruct((t, ch - 2 * nqk), BF16)],
        scratch_shapes=[pltpu.VMEM((HALO + CONV_HEAD, ch), F32)],
        compiler_params=_params(("arbitrary",)),
    )(proj, proj, conv_w)


def _conv_bwd_pre(proj, conv_w, dq, dk, dv, *, name):
    t = proj.shape[0]
    ch = conv_w.shape[1]
    nqk = HEADS * DK
    tt = _pick(t, (256,))
    cur, prev = _conv_specs(tt, ch)

    def body(x_ref, xp_ref, cw_ref, dq_ref, dk_ref, dv_ref, dc_ref, xs):
        conv = _causal_conv(x_ref, xp_ref, cw_ref, xs, tt)
        for cb in range(ch // LANES):
            c = conv(cb)
            sg = _sigmoid(c)
            s = c * sg
            if cb < 2 * HEADS:
                if cb < HEADS:
                    d, scale = dq_ref[:, cb * LANES:(cb + 1) * LANES], DK ** -0.5
                else:
                    d, scale = dk_ref[:, (cb - HEADS) * LANES:(cb - HEADS + 1) * LANES], 1.0
                rinv = lax.rsqrt(jnp.sum(s * s, axis=-1, keepdims=True) + EPS)
                ds = scale * rinv * (d - s * (rinv * rinv) * jnp.sum(d * s, axis=-1, keepdims=True))
            else:
                ds = dv_ref[:, (cb - 2 * HEADS) * LANES:(cb - 2 * HEADS + 1) * LANES]
            dc_ref[:, cb * LANES:(cb + 1) * LANES] = ds * sg * (1.0 + c * (1.0 - sg))

    return pl.pallas_call(
        body, name=name, grid=(t // tt,),
        in_specs=[cur, prev, pl.BlockSpec((CONV_W, ch), lambda i: (0, 0)),
                  pl.BlockSpec((tt, nqk), lambda i: (i, 0)), pl.BlockSpec((tt, nqk), lambda i: (i, 0)),
                  pl.BlockSpec((tt, ch - 2 * nqk), lambda i: (i, 0))],
        out_specs=pl.BlockSpec((tt, ch), lambda i: (i, 0)),
        out_shape=jax.ShapeDtypeStruct((t, ch), F32),
        scratch_shapes=[pltpu.VMEM((HALO + CONV_HEAD, ch), F32)],
        compiler_params=_params(("arbitrary",)),
    )(proj, proj, conv_w, dq, dk, dv)


def _conv_bwd(proj, dc, conv_w, dproj, *, name):
    t = proj.shape[0]
    ch = conv_w.shape[1]
    tt = _pick(t, (256,))
    nt = t // tt
    cur = pl.BlockSpec((tt, ch), lambda i: (i, 0))
    nxt = pl.BlockSpec((HALO, ch), lambda i: (jnp.minimum((i + 1) * (tt // HALO), t // HALO - 1), 0))

    tail = 2 * HALO
    main = tt - tail

    def body(x_ref, d_ref, dn_ref, cw_ref, alias_ref, dx_ref, dw_ref, ds):
        last = pl.program_id(0) == nt - 1
        ds[0:tail, :] = d_ref[main:tt, :]
        ds[tail:tail + HALO, :] = jnp.where(last, 0.0, dn_ref[...])

        @pl.when(pl.program_id(0) == 0)
        def _():
            dw_ref[...] = jnp.zeros_like(dw_ref)

        dcb = d_ref[...].astype(BF16)
        r, c = _iota2(tt)
        moved = [None] + [jnp.dot((c == r + sh).astype(BF16), dcb, preferred_element_type=F32)
                          for sh in range(1, CONV_W)]
        for cb in range(ch // LANES):
            cols = slice(cb * LANES, (cb + 1) * LANES)
            x_m, x_t = x_ref[0:main, cols], x_ref[main:tt, cols]
            acc_m = acc_t = None
            for j in range(CONV_W):
                sh = CONV_W - 1 - j
                w_j = cw_ref[j:j + 1, cols]
                dj_m = d_ref[0:main, cols] if sh == 0 else moved[sh][0:main, cols]
                dj_t = ds[sh:sh + tail, cols]
                acc_m = dj_m * w_j if acc_m is None else acc_m + dj_m * w_j
                acc_t = dj_t * w_j if acc_t is None else acc_t + dj_t * w_j
                dw_ref[j:j + 1, cols] += (jnp.sum(dj_m * x_m, axis=0, keepdims=True)
                                          + jnp.sum(dj_t * x_t, axis=0, keepdims=True))
            dx_ref[0:main, cols] = acc_m.astype(BF16)
            dx_ref[main:tt, cols] = acc_t.astype(BF16)

    return pl.pallas_call(
        body, name=name, grid=(nt,),
        in_specs=[cur, cur, nxt, pl.BlockSpec((CONV_W, ch), lambda i: (0, 0)), pl.BlockSpec(memory_space=pl.ANY)],
        out_specs=[pl.BlockSpec((tt, ch), lambda i: (i, 0)), pl.BlockSpec((HALO, ch), lambda i: (0, 0))],
        out_shape=[jax.ShapeDtypeStruct(dproj.shape, BF16), jax.ShapeDtypeStruct((HALO, ch), F32)],
        scratch_shapes=[pltpu.VMEM((tail + HALO, ch), F32)],
        input_output_aliases={4: 0},
        compiler_params=_params(("arbitrary",)),
    )(proj, dc, dc, conv_w, dproj)


def _bg_fwd(projs, alog_row, dtb_row, *, name):
    t = projs.shape[0]
    tt = _pick(t, (512, 256))

    def body(p_ref, al_ref, db_ref, o_ref):
        p = p_ref[...]
        lane = lax.broadcasted_iota(jnp.int32, p.shape, 1)
        z = p + db_ref[...]
        sp = jnp.maximum(z, 0.0) + jnp.log1p(jnp.exp(-jnp.abs(z)))
        g = -jnp.exp(al_ref[...]) * sp
        o_ref[...] = jnp.where(lane < HEADS, _sigmoid(p), jnp.where(lane < 2 * HEADS, g, 0.0))

    blk = pl.BlockSpec((tt, LANES), lambda i: (i, 0))
    row = pl.BlockSpec((1, LANES), lambda i: (0, 0))
    return pl.pallas_call(
        body, name=name, grid=(t // tt,), in_specs=[blk, row, row], out_specs=blk,
        out_shape=jax.ShapeDtypeStruct((t, LANES), F32), compiler_params=_params(("parallel",)),
    )(projs, alog_row, dtb_row)


def _bg_bwd(projs, dbg, alog_row, dtb_row, *, name):
    t = projs.shape[0]
    tt = _pick(t, (512, 256))

    def body(p_ref, d_ref, al_ref, db_ref, o_ref, dal_ref, ddb_ref):
        p, d = p_ref[...], d_ref[...]
        lane = lax.broadcasted_iota(jnp.int32, p.shape, 1)
        isg = (lane >= HEADS) & (lane < 2 * HEADS)
        be = _sigmoid(p)
        z = p + db_ref[...]
        sp = jnp.maximum(z, 0.0) + jnp.log1p(jnp.exp(-jnp.abs(z)))
        ea = jnp.exp(al_ref[...])
        d_aa = jnp.where(isg, d * (-ea) * _sigmoid(z), 0.0)
        o_ref[...] = jnp.where(lane < HEADS, d * be * (1.0 - be), d_aa).astype(BF16)

        @pl.when(pl.program_id(0) == 0)
        def _():
            dal_ref[...] = jnp.zeros_like(dal_ref)
            ddb_ref[...] = jnp.zeros_like(ddb_ref)

        dal_ref[...] += jnp.sum(jnp.where(isg, d * (-ea) * sp, 0.0), axis=0, keepdims=True)
        ddb_ref[...] += jnp.sum(d_aa, axis=0, keepdims=True)

    blk = pl.BlockSpec((tt, LANES), lambda i: (i, 0))
    row = pl.BlockSpec((1, LANES), lambda i: (0, 0))
    return pl.pallas_call(
        body, name=name, grid=(t // tt,), in_specs=[blk, blk, row, row], out_specs=[blk, row, row],
        out_shape=[jax.ShapeDtypeStruct((t, LANES), BF16), jax.ShapeDtypeStruct((1, LANES), F32),
                   jax.ShapeDtypeStruct((1, LANES), F32)],
        compiler_params=_params(("arbitrary",)),
    )(projs, dbg, alog_row, dtb_row)


def _hn_specs(n_chunks, rev=False):
    def nn(n):
        return n_chunks - 1 - n if rev else n
    tok = lambda w: pl.BlockSpec((CHUNK, HB * w), lambda g, n: (nn(n), g))
    per = lambda a, b: pl.BlockSpec((HB, None, a, b), lambda g, n: (g, nn(n), 0, 0))
    return tok, per


def _hcols(hh, w):
    return slice(hh * w, (hh + 1) * w)


def _decay(gc_col, gc_row):
    r, c = _iota2(CHUNK)
    d = jnp.exp(jnp.minimum(gc_col - gc_row, 0.0))
    return jnp.where(r >= c, d, 0.0), jnp.where(r > c, d, 0.0)


def _gdn_prep(q, k, v, beta_r, g_r, *, name):
    t = q.shape[0]
    n_chunks = t // CHUNK
    cps = 2 if n_chunks % 2 == 0 else 1
    tok = lambda w: pl.BlockSpec((cps * CHUNK, HB * w), lambda g, n: (n, g))
    per = lambda a, b: pl.BlockSpec((HB, cps, a, b), lambda g, n: (g, n, 0, 0))

    def body(q_ref, k_ref, v_ref, b_ref, g_ref, gc_ref, ti_ref, u_ref, w_ref, p_ref):
        r, c = _iota2(CHUNK)
        its = [(h, cc) for cc in range(cps) for h in range(HB)]
        rows = lambda cc: slice(cc * CHUNK, (cc + 1) * CHUNK)
        n = range(len(its))
        kk = [k_ref[rows(cc), _hcols(h, DK)] for h, cc in its]
        gc_col = [jnp.sum(jnp.where(c <= r, jnp.broadcast_to(g_ref[h, cc], (CHUNK, CHUNK)), 0.0), axis=1,
                          keepdims=True) for h, cc in its]
        gc_row = [_col_to_row(gc_col[i]) for i in n]
        beta_col = [_row_to_col(b_ref[h, cc]) for h, cc in its]
        dec = [_decay(gc_col[i], gc_row[i]) for i in n]
        kb = [kk[i] * beta_col[i] for i in n]
        for i, (h, cc) in enumerate(its):
            gc_ref[h, cc] = gc_row[i]
            p_ref[h, cc] = (_dot_nt(q_ref[rows(cc), _hcols(h, DK)], kk[i]) * dec[i][0]).astype(BF16)
        pw = [_dot_nt(kb[i], kk[i]) * dec[i][1] for i in n]
        y = [-pw[i] for i in n]
        for _ in range(5):
            pw = [_dot(pw[i], pw[i]) for i in n]
            yp = [_dot(y[i], pw[i]) for i in n]
            y = [y[i] + pw[i] + yp[i] for i in n]
        vb = [v_ref[rows(cc), _hcols(h, DV)] * beta_col[i] for i, (h, cc) in enumerate(its)]
        kbe = [kb[i] * jnp.exp(gc_col[i]) for i in n]
        yv = [_dot(y[i], vb[i]) for i in n]
        yk = [_dot(y[i], kbe[i]) for i in n]
        for i, (h, cc) in enumerate(its):
            ti_ref[h, cc] = y[i].astype(BF16)
            u_ref[rows(cc), _hcols(h, DV)] = vb[i] + yv[i]
            w_ref[rows(cc), _hcols(h, DK)] = (kbe[i] + yk[i]).astype(BF16)

    return pl.pallas_call(
        body, name=name, grid=(HEADS // HB, n_chunks // cps),
        in_specs=[tok(DK), tok(DK), tok(DV), per(1, CHUNK), per(1, CHUNK)],
        out_specs=[per(1, CHUNK), per(CHUNK, CHUNK), tok(DV), tok(DK), per(CHUNK, CHUNK)],
        out_shape=[jax.ShapeDtypeStruct((HEADS, n_chunks, 1, CHUNK), F32),
                   jax.ShapeDtypeStruct((HEADS, n_chunks, CHUNK, CHUNK), BF16),
                   jax.ShapeDtypeStruct((t, HEADS * DV), F32), jax.ShapeDtypeStruct((t, HEADS * DK), BF16),
                   jax.ShapeDtypeStruct((HEADS, n_chunks, CHUNK, CHUNK), BF16)],
        compiler_params=_params(("parallel", "parallel")),
    )(q, k, v, beta_r, g_r)


def _interleave(*step_gens):
    live = list(step_gens)
    while live:
        for g in list(live):
            try:
                next(g)
            except StopIteration:
                live.remove(g)


def _gdn_scan_steps(q_ref, k_ref, u_ref, w_ref, p_ref, gc_ref, o_ref, s_ref, st):
    hs = range(HB)
    s = [st[h] for h in hs]
    gc_row = [gc_ref[h] for h in hs]
    gc_col = [_row_to_col(gc_row[h]) for h in hs]
    glast = [gc_row[h][:, CHUNK - 1:CHUNK] for h in hs]
    for h in hs:
        s_ref[h] = s[h].astype(BF16)
    yield
    ws = [_dot(w_ref[:, _hcols(h, DK)], s[h]) for h in hs]
    yield
    qs = [_dot(q_ref[:, _hcols(h, DK)] * jnp.exp(gc_col[h]), s[h]) for h in hs]
    vn = [u_ref[:, _hcols(h, DV)] - ws[h] for h in hs]
    yield
    pv = [_dot(p_ref[h], vn[h]) for h in hs]
    yield
    kv = [_dot_tn(k_ref[:, _hcols(h, DK)] * jnp.exp(glast[h] - gc_col[h]), vn[h]) for h in hs]
    yield
    for h in hs:
        o_ref[:, _hcols(h, DV)] = (qs[h] + pv[h]).astype(BF16)
        st[h] = s[h] * jnp.exp(glast[h]) + kv[h]


def _gdn_bwd_steps(q_ref, k_ref, v_ref, b_ref, gc_ref, ti_ref, u_ref, w_ref, s_ref, do_ref,
                   dq_ref, dk_ref, dv_ref, db_ref, dg_ref, dst):
    r, c = _iota2(CHUNK)
    rows = lax.broadcasted_iota(jnp.int32, (CHUNK, 1), 0)
    rsum = lambda a: jnp.sum(a, axis=1, keepdims=True)
    hs = range(HB)
    qq = [q_ref[:, _hcols(h, DK)] for h in hs]
    kk = [k_ref[:, _hcols(h, DK)] for h in hs]
    ww = [w_ref[:, _hcols(h, DK)] for h in hs]
    uu = [u_ref[:, _hcols(h, DV)] for h in hs]
    d_o = [do_ref[:, _hcols(h, DV)] for h in hs]
    s = [s_ref[h].astype(F32) for h in hs]
    d_s = [dst[h] for h in hs]
    gc_row = [gc_ref[h] for h in hs]
    gc_col = [_row_to_col(gc_row[h]) for h in hs]
    beta_col = [_row_to_col(b_ref[h]) for h in hs]
    eg = [jnp.exp(gc_col[h]) for h in hs]
    glast = [gc_row[h][:, CHUNK - 1:CHUNK] for h in hs]
    eglast = [jnp.exp(glast[h]) for h in hs]
    e2 = [jnp.exp(glast[h] - gc_col[h]) for h in hs]
    dec = [_decay(gc_col[h], gc_row[h]) for h in hs]
    kb = [kk[h] * beta_col[h] for h in hs]
    ke = [kk[h] * e2[h] for h in hs]
    qe = [qq[h] * eg[h] for h in hs]
    yield
    kkt = [_dot_nt(kb[h], kk[h]) for h in hs]
    qk = [_dot_nt(qq[h], kk[h]) for h in hs]
    ws = [_dot(ww[h], s[h]) for h in hs]
    yield
    a = [kkt[h] * dec[h][1] for h in hs]
    pp = [qk[h] * dec[h][0] for h in hs]
    vn = [uu[h] - ws[h] for h in hs]
    t1 = [_dot_tn(pp[h], d_o[h]) for h in hs]
    t2 = [_dot(ke[h], d_s[h]) for h in hs]
    yield
    dqe = [_dot_nt(d_o[h], s[h]) for h in hs]
    dke = [_dot_nt(vn[h], d_s[h]) for h in hs]
    dqk = [_dot_nt(d_o[h], vn[h]) * dec[h][0] for h in hs]
    yield
    dvn = [t1[h] + t2[h] for h in hs]
    t3 = [_dot_tn(qe[h], d_o[h]) for h in hs]
    t4 = [_dot_tn(ww[h], dvn[h]) for h in hs]
    dw = [-_dot_nt(dvn[h], s[h]) for h in hs]
    yield
    de2 = [rsum(dke[h] * ke[h]) for h in hs]
    dglast = [jnp.sum(de2[h], axis=0, keepdims=True)
              + eglast[h] * jnp.sum(rsum(d_s[h] * s[h]), axis=0, keepdims=True) for h in hs]
    for h in hs:
        dst[h] = d_s[h] * eglast[h] + t3[h] - t4[h]
    yy = [ti_ref[h] for h in hs]
    t5 = [_dot_tn(yy[h], dvn[h]) for h in hs]
    t6 = [_dot_tn(yy[h], dw[h]) for h in hs]
    yield
    dvb = [dvn[h] + t5[h] for h in hs]
    dkbe = [dw[h] + t6[h] for h in hs]
    t7 = [_dot_nt(dvb[h], uu[h]) for h in hs]
    t8 = [_dot_nt(dkbe[h], ww[h]) for h in hs]
    yield
    d_a = [jnp.where(r > c, -(t7[h] + t8[h]), 0.0) for h in hs]
    dm = [d_a[h] * dec[h][1] for h in hs]
    t9 = [_dot(dm[h], kk[h]) for h in hs]
    t10 = [_dot(dqk[h], kk[h]) for h in hs]
    yield
    t11 = [_dot_tn(dqk[h], qq[h]) for h in hs]
    t12 = [_dot_tn(dm[h], kb[h]) for h in hs]
    yield
    for h in hs:
        dkb = t9[h] + dkbe[h] * eg[h]
        e_mat = d_a[h] * a[h] + dqk[h] * qk[h]
        dgc = (rsum(dqe[h] * qe[h]) - de2[h] + rsum(dkbe[h] * kb[h] * eg[h]) + rsum(e_mat)
               - _row_to_col(jnp.sum(e_mat, axis=0, keepdims=True)))
        dgc = dgc + jnp.where(rows == CHUNK - 1, dglast[h], 0.0)
        dq_ref[:, _hcols(h, DK)] = dqe[h] * eg[h] + t10[h]
        dk_ref[:, _hcols(h, DK)] = t11[h] + dke[h] * e2[h] + t12[h] + dkb * beta_col[h]
        dv_ref[:, _hcols(h, DV)] = dvb[h] * beta_col[h]
        dbeta = rsum(dkb * kk[h]) + rsum(dvb[h] * v_ref[:, _hcols(h, DV)])
        db_ref[h] = _col_to_row(dbeta)
        dg_ref[h] = jnp.sum(jnp.where(r >= c, jnp.broadcast_to(dgc, (CHUNK, CHUNK)), 0.0), axis=0, keepdims=True)


def _rot(x, cos2, sin2):
    return x * cos2 + pltpu.roll(x, DK // 2, 1) * sin2


def _unrot(d, cos2, sin2):
    return d * cos2 + pltpu.roll(d * sin2, DK // 2, 1)


def _ret_consts(lg):
    r, c = _iota2(CHUNK)
    dm = jnp.where(r >= c, jnp.exp((r - c).astype(F32) * lg), 0.0)
    pos = lax.broadcasted_iota(jnp.int32, (CHUNK, 1), 0).astype(F32)
    return dm, jnp.exp((pos + 1.0) * lg), jnp.exp((CHUNK - 1.0 - pos) * lg), jnp.exp(CHUNK * lg)


def _ret_fwd_steps(q_ref, k_ref, v_ref, cos_ref, sin_ref, lg_ref, o_ref, s_ref, st):
    cos2v, sin2v = cos_ref[...], sin_ref[...]
    hs = range(HB)
    s = [st[h] for h in hs]
    for h in hs:
        s_ref[h] = s[h].astype(BF16)
    cst = [_ret_consts(lg_ref[h][:, 0:1]) for h in hs]
    qq = [_rot(q_ref[:, _hcols(h, DK)], cos2v, sin2v) for h in hs]
    kk = [_rot(k_ref[:, _hcols(h, DK)], cos2v, sin2v) * (DK ** -0.5) for h in hs]
    vv = [v_ref[:, _hcols(h, DV)] for h in hs]
    yield
    sc = [_dot_nt(qq[h], kk[h]) * cst[h][0] for h in hs]
    yield
    qs = [_dot(qq[h], s[h]) for h in hs]
    yield
    kv = [_dot_tn(kk[h] * cst[h][2], vv[h]) for h in hs]
    yield
    sv = [_dot(sc[h], vv[h]) for h in hs]
    yield
    for h in hs:
        o_ref[:, _hcols(h, DV)] = (sv[h] + qs[h] * cst[h][1]).astype(BF16)
        st[h] = s[h] * cst[h][3] + kv[h]


def _mix_fwd(q, k, u, w, p, gc_r, proj, cos2, sin2, lg_tab, col_q, col_k, col_v, *, name):
    t = q.shape[0]
    n_chunks = t // CHUNK
    tok, per = _hn_specs(n_chunks)
    bq, bk, bv = col_q // (HB * DK), col_k // (HB * DK), col_v // (HB * DV)

    def body(*refs):
        st_a, st_b = refs[-2:]

        @pl.when(pl.program_id(1) == 0)
        def _():
            st_a[...] = jnp.zeros_like(st_a)
            st_b[...] = jnp.zeros_like(st_b)

        _interleave(_gdn_scan_steps(*refs[0:6], refs[12], refs[13], st_a),
                    _ret_fwd_steps(*refs[6:12], refs[14], refs[15], st_b))

    state = jax.ShapeDtypeStruct((HEADS, n_chunks, DK, DV), BF16)
    return pl.pallas_call(
        body, name=name, grid=(HEADS // HB, n_chunks),
        in_specs=[tok(DK), tok(DK), tok(DV), tok(DK), per(CHUNK, CHUNK), per(1, CHUNK),
                  pl.BlockSpec((CHUNK, HB * DK), lambda g, n: (n, bq + g)),
                  pl.BlockSpec((CHUNK, HB * DK), lambda g, n: (n, bk + g)),
                  pl.BlockSpec((CHUNK, HB * DV), lambda g, n: (n, bv + g)),
                  pl.BlockSpec((CHUNK, DK), lambda g, n: (n, 0)), pl.BlockSpec((CHUNK, DK), lambda g, n: (n, 0)),
                  pl.BlockSpec((HB, 1, LANES), lambda g, n: (g, 0, 0))],
        out_specs=[tok(DV), per(DK, DV), tok(DV), per(DK, DV)],
        out_shape=[jax.ShapeDtypeStruct((t, HEADS * DV), BF16), state, jax.ShapeDtypeStruct((t, HEADS * DV), BF16), state],
        scratch_shapes=[pltpu.VMEM((HB, DK, DV), F32)] * 2,
        compiler_params=_params(("arbitrary", "arbitrary")),
    )(q, k, u, w, p, gc_r, proj, proj, proj, cos2, sin2, lg_tab)


def _ret_bwd_steps(q_ref, k_ref, v_ref, cos_ref, sin_ref, lg_ref, s_ref, do_ref, d_ref, dst):
    nqk = HEADS * DK
    cos2v, sin2v = cos_ref[...], sin_ref[...]
    hs = range(HB)
    s = [s_ref[h].astype(F32) for h in hs]
    d_s = [dst[h] for h in hs]
    d_o = [do_ref[:, _hcols(h, DV)] for h in hs]
    cst = [_ret_consts(lg_ref[h][:, 0:1]) for h in hs]
    qq = [_rot(q_ref[:, _hcols(h, DK)], cos2v, sin2v) for h in hs]
    kk = [_rot(k_ref[:, _hcols(h, DK)], cos2v, sin2v) * (DK ** -0.5) for h in hs]
    vv = [v_ref[:, _hcols(h, DV)] for h in hs]
    dxo = [d_o[h] * cst[h][1] for h in hs]
    yield
    sc = [_dot_nt(qq[h], kk[h]) * cst[h][0] for h in hs]
    dsc = [_dot_nt(d_o[h], vv[h]) * cst[h][0] for h in hs]
    yield
    t1 = [_dot(kk[h] * cst[h][2], d_s[h]) for h in hs]
    t2 = [_dot_nt(dxo[h], s[h]) for h in hs]
    yield
    t3 = [_dot_nt(vv[h], d_s[h]) for h in hs]
    t4 = [_dot_tn(qq[h], dxo[h]) for h in hs]
    yield
    t5 = [_dot_tn(sc[h], d_o[h]) for h in hs]
    t6 = [_dot(dsc[h], kk[h]) for h in hs]
    yield
    t7 = [_dot_tn(dsc[h], qq[h]) for h in hs]
    yield
    for h in hs:
        dst[h] = d_s[h] * cst[h][3] + t4[h]
        d_ref[:, 2 * nqk + h * DV:2 * nqk + (h + 1) * DV] = (t5[h] + t1[h]).astype(BF16)
        d_ref[:, _hcols(h, DK)] = _unrot(t6[h] + t2[h], cos2v, sin2v).astype(BF16)
        d_ref[:, nqk + h * DK:nqk + (h + 1) * DK] = (
            _unrot(t7[h] + t3[h] * cst[h][2], cos2v, sin2v) * (DK ** -0.5)).astype(BF16)


def _mix_bwd(q, k, v, beta_r, gc_r, ti, u, w, s_a, do_a, proj, cos2, sin2, lg_tab, s_b, do_b, dproj,
             col_q, col_k, col_v, *, name):
    t = q.shape[0]
    n_chunks = t // CHUNK
    tok, per = _hn_specs(n_chunks, rev=True)
    bq, bk, bv = col_q // (HB * DK), col_k // (HB * DK), col_v // (HB * DV)
    rv = lambda n: n_chunks - 1 - n
    nqk, wid = HEADS * DK, HEADS * (2 * DK + DV)
    assert HB == HEADS and col_k == col_q + nqk and col_v == col_k + nqk and col_q % wid == 0
    n_a, n_b = 10, 8

    def body(*refs):
        ins_a, ins_b = refs[:n_a], refs[n_a:n_a + n_b]
        outs = refs[n_a + n_b + 1:n_a + n_b + 7]
        dst_a, dst_b = refs[-2:]

        @pl.when(pl.program_id(1) == 0)
        def _():
            dst_a[...] = jnp.zeros_like(dst_a)
            dst_b[...] = jnp.zeros_like(dst_b)

        _interleave(_gdn_bwd_steps(*ins_a, *outs[:5], dst_a), _ret_bwd_steps(*ins_b, outs[5], dst_b))

    rows = jax.ShapeDtypeStruct((HEADS, n_chunks, 1, CHUNK), F32)
    return pl.pallas_call(
        body, name=name, grid=(HEADS // HB, n_chunks),
        in_specs=[tok(DK), tok(DK), tok(DV), per(1, CHUNK), per(1, CHUNK), per(CHUNK, CHUNK), tok(DV), tok(DK),
                  per(DK, DV), tok(DV),
                  pl.BlockSpec((CHUNK, HB * DK), lambda g, n: (rv(n), bq + g)),
                  pl.BlockSpec((CHUNK, HB * DK), lambda g, n: (rv(n), bk + g)),
                  pl.BlockSpec((CHUNK, HB * DV), lambda g, n: (rv(n), bv + g)),
                  pl.BlockSpec((CHUNK, DK), lambda g, n: (rv(n), 0)),
                  pl.BlockSpec((CHUNK, DK), lambda g, n: (rv(n), 0)),
                  pl.BlockSpec((HB, 1, LANES), lambda g, n: (g, 0, 0)),
                  per(DK, DV), tok(DV), pl.BlockSpec(memory_space=pl.ANY)],
        out_specs=[tok(DK), tok(DK), tok(DV), per(1, CHUNK), per(1, CHUNK),
                   pl.BlockSpec((CHUNK, wid), lambda g, n: (rv(n), col_q // wid))],
        out_shape=[jax.ShapeDtypeStruct((t, HEADS * DK), F32), jax.ShapeDtypeStruct((t, HEADS * DK), F32),
                   jax.ShapeDtypeStruct((t, HEADS * DV), F32), rows, rows, jax.ShapeDtypeStruct(dproj.shape, BF16)],
        scratch_shapes=[pltpu.VMEM((HB, DK, DV), F32)] * 2,
        input_output_aliases={n_a + n_b: 5},
        compiler_params=_params(("arbitrary", "arbitrary")),
    )(q, k, v, beta_r, gc_r, ti, u, w, s_a, do_a, proj, proj, proj, cos2, sin2, lg_tab, s_b, do_b, dproj)


def _merge_parts(oa, ob, z, rg, ga, gb):
    oa, ob = oa.astype(F32), ob.astype(F32)
    ra = lax.rsqrt(jnp.mean(oa * oa, axis=-1, keepdims=True) + EPS)
    xa = oa * ra
    mu = jnp.mean(ob, axis=-1, keepdims=True)
    cen = ob - mu
    rb = lax.rsqrt(jnp.mean(cen * cen, axis=-1, keepdims=True) + EPS)
    xb = cen * rb
    sz, sr = _sigmoid(z), _sigmoid(rg)
    return ra, xa, rb, xb, sz, sr, _sigmoid(ga), _sigmoid(gb)


def _merge_specs(tt, d, cz, crg, cga, cgb):
    blk = pl.BlockSpec((tt, d), lambda i: (i, 0))
    pcol = lambda c: pl.BlockSpec((tt, d), lambda i: (i, c // d))
    row = pl.BlockSpec((1, d), lambda i: (0, 0))
    return blk, [blk, blk, pcol(cz), pcol(crg), pcol(cga), pcol(cgb), row, row], row


def _merge_fwd(oa, ob, proj, wa, wb, cz, crg, cga, cgb, *, name):
    t, d = oa.shape
    tt = _pick(t, (256,))
    blk, in_specs, _ = _merge_specs(tt, d, cz, crg, cga, cgb)

    def body(oa_ref, ob_ref, z_ref, rg_ref, ga_ref, gb_ref, wa_ref, wb_ref, o_ref):
        for h in range(HEADS):
            cols = slice(h * DV, (h + 1) * DV)
            z, rg = z_ref[:, cols], rg_ref[:, cols]
            _, xa, _, xb, sz, sr, sga, sgb = _merge_parts(
                oa_ref[:, cols], ob_ref[:, cols], z, rg, ga_ref[:, cols], gb_ref[:, cols])
            o_a = xa * wa_ref[:, cols] * (z * sz)
            o_b = xb * wb_ref[:, cols] * (rg * sr)
            o_ref[:, cols] = (sga * o_a + sgb * o_b).astype(BF16)

    return pl.pallas_call(
        body, name=name, grid=(t // tt,), in_specs=in_specs, out_specs=blk,
        out_shape=jax.ShapeDtypeStruct((t, d), BF16), compiler_params=_params(("parallel",)),
    )(oa, ob, proj, proj, proj, proj, wa, wb)


def _merge_bwd(dmix, oa, ob, proj, wa, wb, cz, crg, cga, cgb, *, name, side):
    t, d = oa.shape
    tt = _pick(t, (256,))
    nt = t // tt
    blk, in_specs, row = _merge_specs(tt, d, cz, crg, cga, cgb)
    assert (crg, cga, cgb) == (cz + d, cz + 2 * d, cz + 3 * d) and cz % (4 * d) == 0
    n_si, n_so = len(side.ins), len(side.out_shapes)
    assert len(side.phases) == 2

    def body(*refs):
        dm_ref, oa_ref, ob_ref, z_ref, rg_ref, ga_ref, gb_ref, wa_ref, wb_ref = refs[:9]
        s_in = refs[9:9 + n_si]
        doa_ref, dob_ref, dp_ref, dwa_ref, dwb_ref = refs[9 + n_si:14 + n_si]
        s_out, ssem, rsem = refs[14 + n_si:14 + n_si + n_so], refs[-2], refs[-1]
        dz_ref, drg_ref, dga_ref, dgb_ref = [dp_ref.at[:, i * d:(i + 1) * d] for i in range(4)]

        @pl.when(pl.program_id(0) == 0)
        def _():
            side.phases[0](s_in, s_out, ssem, rsem)
            dwa_ref[...] = jnp.zeros_like(dwa_ref)
            dwb_ref[...] = jnp.zeros_like(dwb_ref)

        for h in range(HEADS):
            cols = slice(h * DV, (h + 1) * DV)
            z, rg, wa_h, wb_h, dmx = z_ref[:, cols], rg_ref[:, cols], wa_ref[:, cols], wb_ref[:, cols], dm_ref[:, cols]
            ra, xa, rb, xb, sz, sr, sga, sgb = _merge_parts(
                oa_ref[:, cols], ob_ref[:, cols], z, rg, ga_ref[:, cols], gb_ref[:, cols])
            na, nb = xa * wa_h, xb * wb_h
            sil_z, sil_r = z * sz, rg * sr
            o_a, o_b = na * sil_z, nb * sil_r
            dga_ref[:, cols] = (dmx * o_a * sga * (1.0 - sga)).astype(BF16)
            dgb_ref[:, cols] = (dmx * o_b * sgb * (1.0 - sgb)).astype(BF16)
            d_oa, d_ob = dmx * sga, dmx * sgb
            dz_ref[:, cols] = (d_oa * na * sz * (1.0 + z * (1.0 - sz))).astype(BF16)
            drg_ref[:, cols] = (d_ob * nb * sr * (1.0 + rg * (1.0 - sr))).astype(BF16)
            dna, dnb = d_oa * sil_z, d_ob * sil_r
            dwa_ref[:, cols] += jnp.sum(dna * xa, axis=0, keepdims=True)
            dwb_ref[:, cols] += jnp.sum(dnb * xb, axis=0, keepdims=True)
            gwa, gwb = dna * wa_h, dnb * wb_h
            doa_ref[:, cols] = ra * (gwa - xa * jnp.mean(gwa * xa, axis=-1, keepdims=True))
            dob_ref[:, cols] = rb * (gwb - jnp.mean(gwb, axis=-1, keepdims=True)
                                     - xb * jnp.mean(gwb * xb, axis=-1, keepdims=True))

        @pl.when(pl.program_id(0) == nt - 1)
        def _():
            side.phases[1](s_in, s_out, ssem, rsem)

    hbm = pl.BlockSpec(memory_space=pl.ANY)
    return pl.pallas_call(
        body, name=name, grid=(nt,), in_specs=[blk] + in_specs + [hbm] * n_si,
        out_specs=[blk, blk, pl.BlockSpec((tt, 4 * d), lambda i: (i, cz // (4 * d))), row, row] + [hbm] * n_so,
        out_shape=[jax.ShapeDtypeStruct((t, d), F32)] * 2 + [jax.ShapeDtypeStruct(proj.shape, BF16)]
        + [jax.ShapeDtypeStruct((1, d), F32)] * 2 + side.out_shapes,
        scratch_shapes=side.sems(),
        compiler_params=_params(("arbitrary",)),
    )(dmix, oa, ob, proj, proj, proj, proj, wa, wb, *side.ins)


def _row_block(rows, cols, itemsize=4, target=1 << 20):
    for rb in (512, 256, 128, 64, 32, 16, 8):
        if rows % rb == 0 and rb * cols * itemsize <= target:
            return rb
    return rows


def _adamw(w, g, m, v, *, name):
    rows, cols = w.shape
    rb = _row_block(rows, cols)

    def body(w_ref, g_ref, m_ref, v_ref, d_ref, nm_ref, nv_ref):
        gg = g_ref[...]
        mm = ADAM_B1 * m_ref[...] + (1.0 - ADAM_B1) * gg
        vv = ADAM_B2 * v_ref[...] + (1.0 - ADAM_B2) * (gg * gg)
        m_hat = mm / (1.0 - ADAM_B1 ** ADAM_STEP)
        v_hat = vv / (1.0 - ADAM_B2 ** ADAM_STEP)
        d_ref[...] = -ADAM_LR * (m_hat / (jnp.sqrt(v_hat) + ADAM_EPS) + ADAM_WD * w_ref[...])
        nm_ref[...] = mm
        nv_ref[...] = vv

    blk = pl.BlockSpec((rb, cols), lambda i: (i, 0))
    return pl.pallas_call(
        body, name=name, grid=(rows // rb,), in_specs=[blk] * 4, out_specs=[blk] * 3,
        out_shape=[jax.ShapeDtypeStruct((rows, cols), F32)] * 3, compiler_params=_params(("parallel",)),
    )(w, g, m, v)


def _adamw_halves(w, mine, other, c_idx, m, v, *, name):
    rows, cols = w.shape
    hr = rows // 2
    rb = _row_block(hr, cols, 4, 2 << 20)
    nb = hr // rb

    def body(c_ref, w_ref, a_ref, b_ref, m_ref, v_ref, g_ref, d_ref, nm_ref, nv_ref):
        gg = jnp.where(pl.program_id(0) // nb == c_ref[0], a_ref[...], b_ref[...])
        mm = ADAM_B1 * m_ref[...] + (1.0 - ADAM_B1) * gg
        vv = ADAM_B2 * v_ref[...] + (1.0 - ADAM_B2) * (gg * gg)
        m_hat = mm / (1.0 - ADAM_B1 ** ADAM_STEP)
        v_hat = vv / (1.0 - ADAM_B2 ** ADAM_STEP)
        g_ref[...] = gg
        d_ref[...] = -ADAM_LR * (m_hat / (jnp.sqrt(v_hat) + ADAM_EPS) + ADAM_WD * w_ref[...])
        nm_ref[...] = mm
        nv_ref[...] = vv

    blk = pl.BlockSpec((rb, cols), lambda i, c: (i, 0))
    half = pl.BlockSpec((rb, cols), lambda i, c: (i % nb, 0))
    return pl.pallas_call(
        body, name=name,
        grid_spec=pltpu.PrefetchScalarGridSpec(
            num_scalar_prefetch=1, grid=(rows // rb,), in_specs=[blk, half, half, blk, blk], out_specs=[blk] * 4),
        out_shape=[jax.ShapeDtypeStruct((rows, cols), F32)] * 4, compiler_params=_params(("parallel",)),
    )(c_idx, w, mine, other, m, v)


def _pair_add(g, rsib, c_idx, *, name):
    _, _, hr, cols = g.shape
    rb = _row_block(hr, cols, 2, 4 << 20)

    def body(c_ref, g_ref, r_ref, o_ref):
        o_ref[...] = (g_ref[...].astype(F32) + r_ref[...].astype(F32)).astype(BF16)

    return pl.pallas_call(
        body, name=name,
        grid_spec=pltpu.PrefetchScalarGridSpec(
            num_scalar_prefetch=1, grid=(N_CHIPS, hr // rb),
            in_specs=[pl.BlockSpec((None, None, rb, cols), lambda j, i, c: (j, c[0], i, 0)),
                      pl.BlockSpec((None, rb, cols), lambda j, i, c: (j, i, 0))],
            out_specs=pl.BlockSpec((None, rb, cols), lambda j, i, c: (j, i, 0))),
        out_shape=jax.ShapeDtypeStruct((N_CHIPS, hr, cols), BF16),
        compiler_params=_params(("parallel", "parallel")),
    )(c_idx, g, rsib)


def _chip_sum(qb, *, name):
    _, hr, cols = qb.shape
    rb = _row_block(hr, cols, 2, 1 << 20)

    def body(q_ref, o_ref):
        acc = q_ref[0].astype(F32)
        for i in range(1, N_CHIPS):
            acc = acc + q_ref[i].astype(F32)
        o_ref[...] = acc

    return pl.pallas_call(
        body, name=name, grid=(hr // rb,),
        in_specs=[pl.BlockSpec((N_CHIPS, rb, cols), lambda i: (0, i, 0))],
        out_specs=pl.BlockSpec((rb, cols), lambda i: (i, 0)),
        out_shape=jax.ShapeDtypeStruct((hr, cols), F32), compiler_params=_params(("parallel",)),
    )(qb)


def _place():
    x, y, c = lax.axis_index("x"), lax.axis_index("y"), lax.axis_index("c")
    return x, y, c, [(1 - x, y), (x, 1 - y), (1 - x, 1 - y)]


ANY = pl.BlockSpec(memory_space=pl.ANY)


def _gather_side(shards):
    nw, n_k = len(shards), 8

    def plan(ins, outs, ssem, rsem):
        x, y, c, _ = _place()
        nbr_x, nbr_y, sib = (1 - x, y, c), (x, 1 - y, c), (x, y, 1 - c)
        s_me, s_x, s_y, s_d = 2 * x + y, 2 * (1 - x) + y, 2 * x + 1 - y, 2 * (1 - x) + 1 - y

        def rows(w, half, quarter=None):
            hr = shards[w].shape[0] // 2
            if quarter is None:
                return pl.ds(pl.multiple_of(half * hr, 16), hr)
            return pl.ds(pl.multiple_of(half * hr + quarter * (hr // 2), 16), hr // 2)

        def rcopy(w, k, src, slot, rws, to):
            return pltpu.make_async_remote_copy(
                src_ref=src, dst_ref=outs[w].at[slot, rws], send_sem=ssem.at[w * n_k + k],
                recv_sem=rsem.at[w * n_k + k], device_id=to, device_id_type=MESH)

        out = []
        for w in range(nw):
            mine = ins[w].at[rows(w, c)]
            q = [rows(w, c, 0), rows(w, c, 1)]
            p = functools.partial
            out.append([
                (p(rcopy, w, 0, mine, s_me, rows(w, c), nbr_x), p(rcopy, w, 0, mine, s_x, rows(w, c), nbr_x)),
                (p(rcopy, w, 1, mine, s_me, rows(w, c), nbr_y), p(rcopy, w, 1, mine, s_y, rows(w, c), nbr_y)),
                (p(rcopy, w, 2, outs[w].at[s_x, q[0]], s_x, q[0], nbr_y),
                 p(rcopy, w, 2, outs[w].at[s_x, q[0]], s_d, q[0], nbr_y)),
                (p(rcopy, w, 3, outs[w].at[s_y, q[1]], s_y, q[1], nbr_x),
                 p(rcopy, w, 3, outs[w].at[s_y, q[1]], s_d, q[1], nbr_x)),
                (p(rcopy, w, 4, outs[w].at[s_x, rows(w, c)], s_x, rows(w, c), sib),
                 p(rcopy, w, 4, mine, s_x, rows(w, 1 - c), sib)),
                (p(rcopy, w, 5, outs[w].at[s_y, rows(w, c)], s_y, rows(w, c), sib),
                 p(rcopy, w, 5, mine, s_y, rows(w, 1 - c), sib)),
                (p(rcopy, w, 6, outs[w].at[s_d, q[0]], s_d, q[0], sib),
                 p(rcopy, w, 6, outs[w].at[s_d, q[0]], s_d, rows(w, 1 - c, 0), sib)),
                (p(rcopy, w, 7, outs[w].at[s_d, q[1]], s_d, q[1], sib),
                 p(rcopy, w, 7, outs[w].at[s_d, q[1]], s_d, rows(w, 1 - c, 1), sib)),
            ])
        return out

    def send_own(*refs):
        for cps in plan(*refs):
            cps[0][0]().start()
            cps[1][0]().start()

    def relay(*refs):
        for cps in plan(*refs):
            cps[0][1]().wait_recv()
            cps[2][0]().start()
            cps[4][0]().start()
            cps[1][1]().wait_recv()
            cps[3][0]().start()
            cps[5][0]().start()

    def pass_diagonal(*refs):
        for cps in plan(*refs):
            cps[2][1]().wait_recv()
            cps[6][0]().start()
            cps[3][1]().wait_recv()
            cps[7][0]().start()

    def finish(*refs):
        for cps in plan(*refs):
            for k in (4, 5, 6, 7):
                cps[k][1]().wait_recv()
            for k in range(n_k):
                cps[k][0]().wait_send()

    return _Side(shards, [jax.ShapeDtypeStruct((N_CHIPS,) + s.shape, s.dtype) for s in shards], nw * n_k,
                 [send_own, relay, pass_diagonal, finish])


def _run_side(side, *, name):
    n_in, n_out = len(side.ins), len(side.out_shapes)

    def body(*refs):
        ins, outs = refs[:n_in], refs[n_in:n_in + n_out]
        for phase in side.phases:
            phase(ins, outs, refs[-2], refs[-1])

    return pl.pallas_call(
        body, name=name, in_specs=[ANY] * n_in, out_specs=[ANY] * n_out, out_shape=side.out_shapes,
        scratch_shapes=side.sems(), compiler_params=pltpu.CompilerParams(has_side_effects=True),
    )(*side.ins)


def _exchange_side(gs):
    nw = len(gs)

    def copies(ins, outs, ssem, rsem):
        x, y, c, _ = _place()
        return [pltpu.make_async_remote_copy(
            src_ref=ins[w].at[:, 1 - c], dst_ref=outs[w], send_sem=ssem.at[w], recv_sem=rsem.at[w],
            device_id=(x, y, 1 - c), device_id_type=MESH) for w in range(nw)]

    def start(*refs):
        for cp in copies(*refs):
            cp.start()

    def finish(*refs):
        for cp in copies(*refs):
            cp.wait()

    return _Side(gs, [jax.ShapeDtypeStruct((g.shape[0],) + g.shape[2:], g.dtype) for g in gs], nw, [start, finish])


def _scatter_side(ps):
    nw = len(ps)

    def copies(ins, outs, ssem, rsem):
        x, y, c, chips = _place()
        s_me = 2 * x + y
        return [pltpu.make_async_remote_copy(
            src_ref=ins[w].at[2 * chip[0] + chip[1]], dst_ref=outs[w].at[s_me],
            send_sem=ssem.at[w * 3 + j], recv_sem=rsem.at[w * 3 + j],
            device_id=(chip[0], chip[1], c), device_id_type=MESH)
            for w in range(nw) for j, chip in enumerate(chips)]

    def start(*refs):
        for cp in copies(*refs):
            cp.start()

    def finish(*refs):
        for cp in copies(*refs):
            cp.wait()

    return _Side(ps, [jax.ShapeDtypeStruct(p.shape, p.dtype) for p in ps], nw * 3, [start, finish])


def _pair_share(hs, *, name):
    nw = len(hs)

    def body(*refs):
        ins, outs = refs[:nw], refs[nw:2 * nw]
        ssem, rsem = refs[2 * nw:]
        x, y, c, _ = _place()
        cps = []
        for w in range(nw):
            cp = pltpu.make_async_remote_copy(
                src_ref=ins[w], dst_ref=outs[w], send_sem=ssem.at[w], recv_sem=rsem.at[w],
                device_id=(x, y, 1 - c), device_id_type=MESH)
            cp.start()
            cps.append(cp)
        for cp in cps:
            cp.wait()

    return pl.pallas_call(
        body, name=name, in_specs=[ANY] * nw, out_specs=[ANY] * nw,
        out_shape=[jax.ShapeDtypeStruct(h.shape, h.dtype) for h in hs],
        scratch_shapes=[pltpu.SemaphoreType.DMA((nw,)), pltpu.SemaphoreType.DMA((nw,))],
        compiler_params=pltpu.CompilerParams(has_side_effects=True),
    )(*hs)


def _gather_small(a, *, name):
    def body(a_ref, o_ref, ssem, rsem):
        x, y, c, chips = _place()
        s_me = 2 * x + y
        o_ref[s_me] = a_ref[...]
        cps = []
        for j, chip in enumerate(chips):
            cp = pltpu.make_async_remote_copy(
                src_ref=a_ref, dst_ref=o_ref.at[s_me], send_sem=ssem.at[j], recv_sem=rsem.at[j],
                device_id=(chip[0], chip[1], c), device_id_type=MESH)
            cp.start()
            cps.append(cp)
        for cp in cps:
            cp.wait()

    vm = pl.BlockSpec(memory_space=pltpu.VMEM)
    return pl.pallas_call(
        body, name=name, in_specs=[vm], out_specs=vm,
        out_shape=jax.ShapeDtypeStruct((N_CHIPS,) + a.shape, a.dtype),
        scratch_shapes=[pltpu.SemaphoreType.DMA((3,)), pltpu.SemaphoreType.DMA((3,))],
    )(a)


def _allreduce_small(p, *, name):
    def body(p_ref, o_ref, buf, ssem, rsem):
        x, y, c, _ = _place()
        me = 4 * x + 2 * y + c
        buf[me] = p_ref[...]
        cps = []
        for k in range(1, N_DEV):
            fx, fy, fc = (k >> 2) & 1, (k >> 1) & 1, k & 1
            peer = (x + fx - 2 * x * fx, y + fy - 2 * y * fy, c + fc - 2 * c * fc)
            cp = pltpu.make_async_remote_copy(
                src_ref=buf.at[me], dst_ref=buf.at[me], send_sem=ssem.at[k - 1], recv_sem=rsem.at[k - 1],
                device_id=peer, device_id_type=MESH)
            cp.start()
            cps.append(cp)
        for cp in cps:
            cp.wait()
        acc = buf[0]
        for i in range(1, N_DEV):
            acc = acc + buf[i]
        o_ref[...] = acc

    vm = pl.BlockSpec(memory_space=pltpu.VMEM)
    return pl.pallas_call(
        body, name=name, in_specs=[vm], out_specs=vm,
        out_shape=jax.ShapeDtypeStruct(p.shape, p.dtype),
        scratch_shapes=[pltpu.VMEM((N_DEV,) + p.shape, p.dtype), pltpu.SemaphoreType.DMA((N_DEV - 1,)),
                        pltpu.SemaphoreType.DMA((N_DEV - 1,))],
    )(p)


def _rows_to_tokens(r):
    h, n = r.shape[0], r.shape[1]
    return r.reshape(h, n * CHUNK).T


def _tokens_to_rows(a):
    t, h = a.shape
    return a.T.reshape(h, t // CHUNK, 1, CHUNK)


def kernel(x, norm1_w, w_in, conv_w, a_log, dt_bias, gdn_norm_w, ret_norm_w, w_out, norm2_w, w_gate, w_up, w_down, norm_f_w, loss_target, m_norm1_w, m_w_in, m_conv_w, m_a_log, m_dt_bias, m_gdn_norm_w, m_ret_norm_w, m_w_out, m_norm2_w, m_w_gate, m_w_up, m_w_down, m_norm_f_w, v_norm1_w, v_w_in, v_conv_w, v_a_log, v_dt_bias, v_gdn_norm_w, v_ret_norm_w, v_w_out, v_norm2_w, v_w_gate, v_w_up, v_w_down, v_norm_f_w):
    t, d = x.shape[1], x.shape[2]
    f = w_gate.shape[2] * N_CHIPS
    nqk, nv = HEADS * DK, HEADS * DV
    ncs = w_in.shape[2]
    c_idx = lax.axis_index("c")
    s_idx = 2 * lax.axis_index("x") + lax.axis_index("y")
    xs = x[0]
    tgt = loss_target[0]

    n_small = 2 * HEADS
    widths = [2 * nqk + nv, nv, n_small, nqk, nqk, nv, nv, d, d]
    g_off = np.concatenate([[0], np.cumsum(widths)])
    order = [0, 3, 4, 5, 1, 6, 7, 8]
    m_off = np.concatenate([[0], np.cumsum([widths[i] for i in order])])
    o_rq, o_rk, o_rv, o_az, o_rg, o_ga, o_gb = [int(m_off[i]) for i in range(1, 8)]
    segs = [(int(g_off[i]), int(g_off[i + 1]), int(m_off[order.index(i)]) if i != 2 else None) for i in range(9)]

    def shard_pieces(s):
        out = []
        for a, b, dst in segs:
            lo, hi = max(a, s * ncs), min(b, (s + 1) * ncs)
            if lo < hi:
                out.append((lo - s * ncs, hi - s * ncs, None if dst is None else dst + lo - a))
        return out

    own = [w_in[0].astype(BF16), w_out[0].astype(BF16), w_gate[0].astype(BF16), w_up[0].astype(BF16),
           w_down[0].astype(BF16)]
    put_own = lambda full, mine: lax.dynamic_update_slice(full, mine[None], (s_idx, 0, 0))
    wg_in = put_own(_run_side(_gather_side(own[:1]), name="gather_w_in")[0], own[0])
    conv_full = _gather_small(conv_w[0], name="gather_conv_w")
    conv_full = jnp.concatenate([conv_full[i] for i in range(N_CHIPS)], axis=1)
    cuts = [(dst, s, a, b) for s in range(N_CHIPS) for a, b, dst in shard_pieces(s)]
    w_main = jnp.concatenate([wg_in[s][:, a:b] for dst, s, a, b in sorted(c for c in cuts if c[0] is not None)],
                             axis=1)
    w_small = jnp.concatenate([wg_in[s][:, a:b] for dst, s, a, b in cuts if dst is None], axis=1)
    w_small = jnp.pad(w_small, ((0, 0), (0, LANES - n_small)))

    pad16 = lambda a: jnp.pad(a, ((0, 0), (HEADS, LANES - 2 * HEADS)))
    alog_row, dtb_row = pad16(a_log), pad16(dt_bias)
    wa_row = jnp.tile(gdn_norm_w, (1, HEADS))
    inv = ROPE_BASE ** (-jnp.arange(0, DK, 2, dtype=F32) / DK)
    ang = jnp.arange(t, dtype=F32)[:, None] * inv[None, :]
    cos2 = jnp.concatenate([jnp.cos(ang), jnp.cos(ang)], axis=1)
    sin2 = jnp.concatenate([-jnp.sin(ang), jnp.sin(ang)], axis=1)
    lg = jnp.log1p(-jnp.exp2(-5.0 - jnp.arange(HEADS, dtype=F32)))
    lg_tab = jnp.broadcast_to(lg[:, None, None], (HEADS, 1, LANES))

    fq = f // N_CHIPS
    u1, projs = _rms_fwd(xs, norm1_w, w_narrow=w_small, name="rms1_fwd")
    proj, *rest = _mm(u1, w_main, tm=512, tn=2048, tk=d, side=_gather_side(own[1:]), name="mm_proj")
    wg_out, wg_gate, wg_up, wg_down = [put_own(g, o) for g, o in zip(rest, own[1:])]
    w_o = wg_out.reshape(d, d)
    w_g, w_u = wg_gate, wg_up
    w_d = wg_down.reshape(f, d)
    q_a, k_a, v_a = _conv_fwd(proj, conv_full, name="conv_fwd")
    bg = _bg_fwd(projs, alog_row, dtb_row, name="bg_fwd")
    beta_r = _tokens_to_rows(bg[:, :HEADS])
    g_r = _tokens_to_rows(bg[:, HEADS:2 * HEADS])
    gc_r, tinv, u_a, w_a, p_a = _gdn_prep(q_a, k_a, v_a, beta_r, g_r, name="gdn_prep")
    o_a, s_a, o_b, s_b = _mix_fwd(q_a, k_a, u_a, w_a, p_a, gc_r, proj, cos2, sin2, lg_tab, o_rq, o_rk, o_rv,
                                  name="mix_fwd")
    mixed = _merge_fwd(o_a, o_b, proj, wa_row, ret_norm_w, o_az, o_rg, o_ga, o_gb, name="merge_fwd")
    h1, hn = _mm_res_norm(mixed, w_o, xs, norm2_w, tm=512, name="mm_out")
    gt, up, act = _ffn_in(hn, w_g, w_u, tm=512, name="ffn_in")
    h2 = _mm(act, w_d, res=h1, tm=512, tn=1024, tk=f, name="mm_down")
    loss_row, dh2, dh2b, d_nf = _loss_head(h2, tgt, norm_f_w.reshape(1, d), name="loss_head")

    g_down = _mm(act, dh2b, ta=True, out_dtype=BF16, tm=fq, tn=d, tk=1024, name="mm_dw_down")
    dgt, dup = _ffn_back(dh2b, w_d, gt, up, tm=512, tn=fq, name="ffn_back")
    dhn = _mm(dgt, w_g, tb=True, pair=(dup, w_u), out_dtype=BF16, tm=512, tn=d, tk=fq, name="mm_dhn")
    g_gate = _mm(hn, dgt, ta=True, out_dtype=BF16, out_stacked=True, tm=1024, tn=fq, tk=2048, name="mm_dw_gate")
    g_up = _mm(hn, dup, ta=True, out_dtype=BF16, out_stacked=True, tm=1024, tn=fq, tk=2048, name="mm_dw_up")
    dh1, dh1b, d_n2 = _rms_bwd(dhn, h1, norm2_w, dh2, bf16_copy=True, name="rms2_bwd")
    dmix = _mm(dh1b, w_o, tb=True, tm=512, tn=d, tk=d, name="mm_dmix")
    g_out = _mm(mixed, dh1b, ta=True, out_dtype=BF16, tm=1024, tn=d, tk=1024, name="mm_dw_out")
    halves = lambda g: g.reshape(N_CHIPS, 2, g.shape[1] // 2, g.shape[2])
    c_arr = jnp.reshape(c_idx, (1,)).astype(jnp.int32)
    gs_ffn = [halves(g_out.reshape(N_CHIPS, d // N_CHIPS, d)), halves(g_gate), halves(g_up),
              halves(g_down.reshape(N_CHIPS, fq, d))]
    do_a, do_b, dproj, d_wa, d_wb, *rsib = _merge_bwd(
        dmix, o_a, o_b, proj, wa_row, ret_norm_w, o_az, o_rg, o_ga, o_gb, side=_exchange_side(gs_ffn),
        name="merge_bwd")
    ps_ffn = [_pair_add(g, r, c_arr, name=f"grad_pair_add_{nm}")
              for g, r, nm in zip(gs_ffn, rsib, ["w_out", "w_gate", "w_up", "w_down"])]
    dq_a, dk_a, dv_a, dbeta_r, dg_r, dproj = _mix_bwd(
        q_a, k_a, v_a, beta_r, gc_r, tinv, u_a, w_a, s_a, do_a, proj, cos2, sin2, lg_tab, s_b, do_b, dproj,
        o_rq, o_rk, o_rv, name="mix_bwd")
    dc = _conv_bwd_pre(proj, conv_full, dq_a, dk_a, dv_a, name="conv_bwd_pre")
    dproj, d_cw = _conv_bwd(proj, dc, conv_full, dproj, name="conv_bwd")
    dbg = jnp.pad(jnp.concatenate([_rows_to_tokens(dbeta_r), _rows_to_tokens(dg_r)], axis=1),
                  ((0, 0), (0, LANES - 2 * HEADS)))
    dprojs, d_alog, d_dtb = _bg_bwd(projs, dbg, alog_row, dtb_row, name="bg_bwd")
    g_main, *qs_ffn = _mm(u1, dproj, ta=True, out_dtype=BF16, tm=1024, tn=2048, tk=2048,
                          side=_scatter_side(ps_ffn), name="mm_dw_in")
    g_small = _mm(u1, dprojs, ta=True, out_dtype=BF16, tm=1024, tn=LANES, tk=1024, name="mm_dw_in_small")
    pieces = []
    for s in range(N_CHIPS):
        seen, parts = 0, []
        for a, b, dst in shard_pieces(s):
            parts.append(g_small[:, seen:seen + b - a] if dst is None else g_main[:, dst:dst + b - a])
            seen += (b - a) if dst is None else 0
        pieces.append(jnp.concatenate(parts, axis=1))
    gs_in = [halves(jnp.stack(pieces))]
    rsib = _run_side(_exchange_side(gs_in), name="grad_pair_exchange_in")
    ps_in = [_pair_add(gs_in[0], rsib[0], c_arr, name="grad_pair_add_w_in")]
    du, *qs_in = _mm(dproj, w_main, tb=True, out_dtype=BF16, tm=1024, tn=d, tk=2048,
                     side=_scatter_side(ps_in), name="mm_du")
    dx, d_n1 = _rms_bwd(du, xs, norm1_w, dh1, bf16_copy=False, narrow=(dprojs, w_small), name="rms1_bwd")

    names = ["w_in", "w_out", "w_gate", "w_up", "w_down"]
    qs = [lax.dynamic_update_slice(q, lax.dynamic_slice(p, (s_idx, 0, 0), (1,) + p.shape[1:]), (s_idx, 0, 0))
          for q, p in zip(qs_in + qs_ffn, ps_in + ps_ffn)]
    hs = [_chip_sum(q, name=f"grad_chip_sum_{nm}") for q, nm in zip(qs, names)]
    theirs = _pair_share(hs, name="grad_pair_share")
    big_w = [w_in[0], w_out[0], w_gate[0], w_up[0], w_down[0]]
    big_m = [m_w_in[0], m_w_out[0], m_w_gate[0], m_w_up[0], m_w_down[0]]
    big_v = [v_w_in[0], v_w_out[0], v_w_gate[0], v_w_up[0], v_w_down[0]]
    big = {}
    for nm, mine, other, w_, m_, v_ in zip(names, hs, theirs, big_w, big_m, big_v):
        big[nm] = tuple(a[None] for a in _adamw_halves(w_, mine, other, c_arr, m_, v_, name=f"adamw_{nm}"))

    d_wa_h = jnp.sum(d_wa.reshape(HEADS, DV), axis=0, keepdims=True)
    small = [d_n1, d_alog[:, HEADS:2 * HEADS], d_dtb[:, HEADS:2 * HEADS], d_wa_h, d_wb, d_n2, d_nf,
             d_cw[:CONV_W].reshape(1, -1)]
    sizes = [a.shape[1] for a in small]
    packed = jnp.concatenate(small, axis=1)
    n_pack = packed.shape[1]
    n_rows = -(-n_pack // LANES)
    n_rows = -(-n_rows // 8) * 8
    packed = jnp.pad(packed, ((0, 0), (0, n_rows * LANES - n_pack))).reshape(n_rows, LANES)
    red = _allreduce_small(packed, name="allreduce_small").reshape(1, -1)
    offs = np.cumsum([0] + sizes)
    g_n1, g_alog, g_dtb, g_wa, g_wb, g_n2, g_nf, g_cw = [red[:, offs[i]:offs[i + 1]] for i in range(len(sizes))]
    ncw = conv_w.shape[2]
    g_cw = lax.dynamic_slice(g_cw.reshape(CONV_W, -1), (0, s_idx * ncw), (CONV_W, ncw))

    def small_update(w_, g, m_, v_, nm):
        shape = w_.shape
        pad = (-w_.size) % LANES
        to2 = lambda a: jnp.pad(a.reshape(1, -1), ((0, 0), (0, pad)))
        outs = _adamw(to2(w_), to2(g), to2(m_), to2(v_), name=f"adamw_{nm}")
        return (g.reshape(shape),) + tuple(a[:, :w_.size].reshape(shape) for a in outs)

    res = {
        "norm1_w": small_update(norm1_w, g_n1, m_norm1_w, v_norm1_w, "norm1_w"),
        "w_in": big["w_in"],
        "conv_w": small_update(conv_w, g_cw, m_conv_w, v_conv_w, "conv_w"),
        "a_log": small_update(a_log, g_alog, m_a_log, v_a_log, "a_log"),
        "dt_bias": small_update(dt_bias, g_dtb, m_dt_bias, v_dt_bias, "dt_bias"),
        "gdn_norm_w": small_update(gdn_norm_w, g_wa, m_gdn_norm_w, v_gdn_norm_w, "gdn_norm_w"),
        "ret_norm_w": small_update(ret_norm_w, g_wb, m_ret_norm_w, v_ret_norm_w, "ret_norm_w"),
        "w_out": big["w_out"],
        "norm2_w": small_update(norm2_w, g_n2, m_norm2_w, v_norm2_w, "norm2_w"),
        "w_gate": big["w_gate"],
        "w_up": big["w_up"],
        "w_down": big["w_down"],
        "norm_f_w": small_update(norm_f_w, g_nf, m_norm_f_w, v_norm_f_w, "norm_f_w"),
    }
    order = ["norm1_w", "w_in", "conv_w", "a_log", "dt_bias", "gdn_norm_w", "ret_norm_w", "w_out", "norm2_w",
             "w_gate", "w_up", "w_down", "norm_f_w"]
    loss = lax.psum(loss_row[0, 0], ("x", "y", "c"))
    return (loss, dx[None], *[res[n][0] for n in order], *[res[n][1] for n in order],
            *[res[n][2] for n in order], *[res[n][3] for n in order])
```

```python
import functools

import jax
import jax.numpy as jnp
import numpy as np
from jax import lax
from jax.experimental import pallas as pl
from jax.experimental.pallas import tpu as pltpu

F32 = jnp.float32
BF16 = jnp.bfloat16
MESH = pl.DeviceIdType.MESH

HEADS = 8
DK = 128
DV = 256
CHUNK = 64
CONV_W = 4
EPS = 1e-6
ROPE_BASE = 10000.0
ADAM_LR, ADAM_B1, ADAM_B2, ADAM_EPS, ADAM_WD, ADAM_STEP = 0.001, 0.9, 0.999, 1e-08, 0.01, 10
N_CHIPS = 4
N_DEV = 8
LANES = 128
HALO = 8
VMEM_LIMIT = 56 * 1024 * 1024
HB = 8


def _pick(n, cands):
    for c in cands:
        if n % c == 0:
            return c
    raise ValueError(f"no tile for {n} in {cands}")


def _params(sem=None):
    return pltpu.CompilerParams(dimension_semantics=sem, vmem_limit_bytes=VMEM_LIMIT)


def _dot(a, b):
    return jnp.dot(a.astype(BF16), b.astype(BF16), preferred_element_type=F32)


def _dot_nt(a, b):
    return lax.dot_general(a.astype(BF16), b.astype(BF16), (((1,), (1,)), ((), ())), preferred_element_type=F32)


def _dot_tn(a, b):
    return lax.dot_general(a.astype(BF16), b.astype(BF16), (((0,), (0,)), ((), ())), preferred_element_type=F32)


def _sigmoid(x):
    return 1.0 / (1.0 + jnp.exp(-x))


def _iota2(n):
    return lax.broadcasted_iota(jnp.int32, (n, n), 0), lax.broadcasted_iota(jnp.int32, (n, n), 1)


def _row_to_col(row):
    n = row.shape[1]
    r, c = _iota2(n)
    return jnp.sum(jnp.where(r == c, jnp.broadcast_to(row, (n, n)), 0.0), axis=1, keepdims=True)


def _col_to_row(col):
    n = col.shape[0]
    r, c = _iota2(n)
    return jnp.sum(jnp.where(r == c, jnp.broadcast_to(col, (n, n)), 0.0), axis=0, keepdims=True)


class _Side:
    def __init__(self, ins, out_shapes, n_sem, phases):
        self.ins, self.out_shapes, self.n_sem, self.phases = list(ins), list(out_shapes), n_sem, phases

    def sems(self):
        return [pltpu.SemaphoreType.DMA((self.n_sem,)), pltpu.SemaphoreType.DMA((self.n_sem,))]


def _mm(a, b, *, name, tm, tn, tk, ta=False, tb=False, out_dtype=F32, res=None, side=None, pair=None,
        out_stacked=False):
    m, k = (a.shape[1], a.shape[0]) if ta else a.shape
    b_slots = b.ndim == 3
    if b_slots:
        assert tb and tk == b.shape[2]
        n = b.shape[1]
    else:
        n = b.shape[0] if tb else b.shape[1]
    tm, tn, tk = min(tm, m), min(tn, n), min(tk, k)
    assert m % tm == 0 and n % tn == 0 and k % tk == 0, (name, m, n, k)
    nk = k // tk
    nj, ni = n // tn, m // tm
    dn = (((0 if ta else 1,), (1 if tb else 0,)), ((), ()))
    n_ab = 4 if pair else 2
    n_in = n_ab + (res is not None)
    n_side_in = len(side.ins) if side else 0
    n_side_out = len(side.out_shapes) if side else 0

    def body(*refs):
        a_ref, b_ref = refs[0], refs[1]
        r_ref = refs[n_ab] if res is not None else None
        o_ref = refs[n_in + n_side_in]
        if side:
            s_in = refs[n_in:n_in + n_side_in]
            s_out = refs[n_in + n_side_in + 1:n_in + n_side_in + 1 + n_side_out]
            ssem, rsem = refs[-2], refs[-1]
            j_, i_, k_ = pl.program_id(0), pl.program_id(1), pl.program_id(2)
            n_mid = len(side.phases) - 2
            assert n_mid == 0 or nj >= 2 * n_mid
            when = [(j_ == 0) & (i_ == 0) & (k_ == 0)]
            when += [(j_ == nj // 2 + (p * (nj // 2)) // n_mid) & (i_ == 0) & (k_ == 0) for p in range(n_mid)]
            when.append((j_ == nj - 1) & (i_ == ni - 1) & (k_ == nk - 1))

        def run_phase(p):
            @pl.when(when[p])
            def _():
                side.phases[p](s_in, s_out, ssem, rsem)

        if side:
            for p in range(len(side.phases) - 1):
                run_phase(p)

        def finish(r):
            if res is not None:
                r = r + r_ref[...]
            o_ref[...] = r.astype(out_dtype)

        part = lax.dot_general(a_ref[...], b_ref[...], dn, preferred_element_type=F32)
        if pair:
            part = part + lax.dot_general(refs[2][...], refs[3][...], dn, preferred_element_type=F32)
        if nk == 1:
            finish(part)
        else:
            acc = refs[n_in + n_side_in + 1 + n_side_out]
            kk = pl.program_id(2)

            @pl.when(kk == 0)
            def _():
                acc[...] = part

            @pl.when((kk > 0) & (kk < nk - 1))
            def _():
                acc[...] += part

            @pl.when(kk == nk - 1)
            def _():
                finish(acc[...] + part)

        if side:
            run_phase(len(side.phases) - 1)

    a_spec = pl.BlockSpec((tk, tm), lambda j, i, kk: (kk, i)) if ta else pl.BlockSpec((tm, tk), lambda j, i, kk: (i, kk))
    b_spec = pl.BlockSpec((tn, tk), lambda j, i, kk: (j, kk)) if tb else pl.BlockSpec((tk, tn), lambda j, i, kk: (kk, j))
    if b_slots:
        b_spec = pl.BlockSpec((None, tn, tk), lambda j, i, kk: (kk, j, 0))
    o_spec = pl.BlockSpec((tm, tn), lambda j, i, kk: (i, j))
    o_shape = jax.ShapeDtypeStruct((m, n), out_dtype)
    if out_stacked:
        assert res is None
        o_spec = pl.BlockSpec((None, tm, tn), lambda j, i, kk: (j, i, 0))
        o_shape = jax.ShapeDtypeStruct((nj, m, tn), out_dtype)
    in_specs, args = [a_spec, b_spec], [a, b]
    if pair:
        assert pair[0].shape == a.shape and pair[1].shape == b.shape
        in_specs += [a_spec, b_spec]
        args += list(pair)
    if res is not None:
        in_specs.append(o_spec)
        args.append(res)
    out_specs, out_shape = o_spec, o_shape
    scratch = [pltpu.VMEM((tm, tn), F32)] if nk > 1 else []
    sem = ("parallel", "parallel", "arbitrary")
    if side:
        hbm = pl.BlockSpec(memory_space=pl.ANY)
        in_specs += [hbm] * n_side_in
        args += side.ins
        out_specs, out_shape = [o_spec] + [hbm] * n_side_out, [out_shape] + side.out_shapes
        scratch += side.sems()
        sem = ("arbitrary",) * 3
    return pl.pallas_call(
        body, name=name, grid=(nj, ni, nk), in_specs=in_specs, out_specs=out_specs, out_shape=out_shape,
        scratch_shapes=scratch, compiler_params=_params(sem),
    )(*args)


def _mm_res_norm(a, b, res, w, *, name, tm):
    m, k = a.shape
    n = b.shape[1]
    tm = min(tm, m)

    def body(a_ref, b_ref, r_ref, w_ref, h_ref, hn_ref):
        h = jnp.dot(a_ref[...], b_ref[...], preferred_element_type=F32) + r_ref[...]
        h_ref[...] = h
        r = lax.rsqrt(jnp.mean(h * h, axis=-1, keepdims=True) + EPS)
        hn_ref[...] = (h * r * w_ref[...]).astype(BF16)

    row = pl.BlockSpec((tm, n), lambda i: (i, 0))
    return pl.pallas_call(
        body, name=name, grid=(m // tm,),
        in_specs=[pl.BlockSpec((tm, k), lambda i: (i, 0)), pl.BlockSpec((k, n), lambda i: (0, 0)), row,
                  pl.BlockSpec((1, n), lambda i: (0, 0))],
        out_specs=[row, row],
        out_shape=[jax.ShapeDtypeStruct((m, n), F32), jax.ShapeDtypeStruct((m, n), BF16)],
        compiler_params=_params(("parallel",)),
    )(a, b, res, w)


def _rms_fwd(x, w, *, name, w_narrow=None):
    t, d = x.shape
    tt = _pick(t, (512, 256))

    def body(*refs):
        x_ref, w_ref, o_ref = refs[0], refs[1], refs[-2 if w_narrow is not None else -1]
        xv = x_ref[...]
        r = lax.rsqrt(jnp.mean(xv * xv, axis=-1, keepdims=True) + EPS)
        u = (xv * r * w_ref[...]).astype(BF16)
        o_ref[...] = u
        if w_narrow is not None:
            refs[-1][...] = jnp.dot(u, refs[2][...], preferred_element_type=F32)

    blk = pl.BlockSpec((tt, d), lambda i: (i, 0))
    in_specs, args = [blk, pl.BlockSpec((1, d), lambda i: (0, 0))], [x, w]
    out_specs, out_shape = [blk], [jax.ShapeDtypeStruct((t, d), BF16)]
    if w_narrow is not None:
        in_specs.append(pl.BlockSpec(w_narrow.shape, lambda i: (0, 0)))
        args.append(w_narrow)
        out_specs.append(pl.BlockSpec((tt, LANES), lambda i: (i, 0)))
        out_shape.append(jax.ShapeDtypeStruct((t, LANES), F32))
    out = pl.pallas_call(
        body, name=name, grid=(t // tt,), in_specs=in_specs, out_specs=out_specs, out_shape=out_shape,
        compiler_params=_params(("parallel",)),
    )(*args)
    return out if w_narrow is not None else out[0]


def _rms_bwd(dn, x, w, dres, *, name, bf16_copy, narrow=None):
    t, d = x.shape
    tt = _pick(t, (256,))

    def body(*refs):
        dn_ref, x_ref, w_ref, dres_ref = refs[:4]
        dx_ref, dw_ref = refs[-3 if bf16_copy else -2], refs[-1]
        xv, g = x_ref[...], dn_ref[...].astype(F32)
        if narrow is not None:
            g = g + lax.dot_general(refs[4][...], refs[5][...], (((1,), (1,)), ((), ())), preferred_element_type=F32)
        r = lax.rsqrt(jnp.mean(xv * xv, axis=-1, keepdims=True) + EPS)
        xh = xv * r
        gw = g * w_ref[...]
        dx = dres_ref[...] + r * (gw - xh * jnp.mean(gw * xh, axis=-1, keepdims=True))
        dx_ref[...] = dx
        if bf16_copy:
            refs[-2][...] = dx.astype(BF16)

        @pl.when(pl.program_id(0) == 0)
        def _():
            dw_ref[...] = jnp.zeros_like(dw_ref)

        dw_ref[...] += jnp.sum(g * xh, axis=0, keepdims=True)

    blk = pl.BlockSpec((tt, d), lambda i: (i, 0))
    row = pl.BlockSpec((1, d), lambda i: (0, 0))
    in_specs, args = [blk, blk, row, blk], [dn, x, w, dres]
    if narrow is not None:
        in_specs += [pl.BlockSpec((tt, LANES), lambda i: (i, 0)), pl.BlockSpec(narrow[1].shape, lambda i: (0, 0))]
        args += list(narrow)
    copy_spec, copy_shape = ([blk], [jax.ShapeDtypeStruct((t, d), BF16)]) if bf16_copy else ([], [])
    return pl.pallas_call(
        body, name=name, grid=(t // tt,), in_specs=in_specs, out_specs=[blk] + copy_spec + [row],
        out_shape=[jax.ShapeDtypeStruct((t, d), F32)] + copy_shape + [jax.ShapeDtypeStruct((1, d), F32)],
        compiler_params=_params(("arbitrary",)),
    )(*args)


def _loss_head(h2, tgt, wf, *, name):
    t, d = h2.shape
    tt = _pick(t, (256,))

    def body(x_ref, t_ref, w_ref, loss_ref, dx_ref, dxb_ref, dw_ref):
        xv = x_ref[...]
        r = lax.rsqrt(jnp.mean(xv * xv, axis=-1, keepdims=True) + EPS)
        xh = xv * r
        err = xh * w_ref[...] - t_ref[...]
        lpart = 0.5 * jnp.sum(jnp.mean(err * err, axis=-1, keepdims=True), axis=0, keepdims=True)
        dy = err * (1.0 / d)
        gw = dy * w_ref[...]
        dx = r * (gw - xh * jnp.mean(gw * xh, axis=-1, keepdims=True))
        dx_ref[...] = dx
        dxb_ref[...] = dx.astype(BF16)

        @pl.when(pl.program_id(0) == 0)
        def _():
            dw_ref[...] = jnp.zeros_like(dw_ref)
            loss_ref[...] = jnp.zeros_like(loss_ref)

        dw_ref[...] += jnp.sum(dy * xh, axis=0, keepdims=True)
        loss_ref[...] += jnp.broadcast_to(lpart, loss_ref.shape)

    blk = pl.BlockSpec((tt, d), lambda i: (i, 0))
    row = pl.BlockSpec((1, d), lambda i: (0, 0))
    lrow = pl.BlockSpec((1, LANES), lambda i: (0, 0))
    return pl.pallas_call(
        body, name=name, grid=(t // tt,),
        in_specs=[blk, blk, row], out_specs=[lrow, blk, blk, row],
        out_shape=[jax.ShapeDtypeStruct((1, LANES), F32), jax.ShapeDtypeStruct((t, d), F32),
                   jax.ShapeDtypeStruct((t, d), BF16), jax.ShapeDtypeStruct((1, d), F32)],
        compiler_params=_params(("arbitrary",)),
    )(h2, tgt, wf)


def _ffn_in(hn, w_g, w_u, *, name, tm):
    t, d = hn.shape
    tn = w_g.shape[2]
    f = w_g.shape[0] * tn
    tm = min(tm, t)

    def body(a_ref, g_ref, u_ref, gt_ref, up_ref, act_ref):
        a = a_ref[...]
        g = jnp.dot(a, g_ref[...], preferred_element_type=F32)
        u = jnp.dot(a, u_ref[...], preferred_element_type=F32)
        gt_ref[...] = g.astype(BF16)
        up_ref[...] = u.astype(BF16)
        act_ref[...] = (g * _sigmoid(g) * u).astype(BF16)

    wblk = pl.BlockSpec((None, d, tn), lambda j, i: (j, 0, 0))
    oblk = pl.BlockSpec((tm, tn), lambda j, i: (i, j))
    return pl.pallas_call(
        body, name=name, grid=(f // tn, t // tm),
        in_specs=[pl.BlockSpec((tm, d), lambda j, i: (i, 0)), wblk, wblk], out_specs=[oblk] * 3,
        out_shape=[jax.ShapeDtypeStruct((t, f), BF16)] * 3,
        compiler_params=_params(("parallel", "parallel")),
    )(hn, w_g, w_u)


def _ffn_back(dh2b, w_d, gt, up, *, name, tm, tn):
    t, d = dh2b.shape
    f = w_d.shape[0]
    tm = min(tm, t)

    n_sub = 2 if tm % 32 == 0 else 1

    def body(a_ref, w_ref, g_ref, u_ref, dg_ref, du_ref):
        w = w_ref[...]
        subs = [slice(i * (tm // n_sub), (i + 1) * (tm // n_sub)) for i in range(n_sub)]
        das = [lax.dot_general(a_ref[rs, :], w, (((1,), (1,)), ((), ())), preferred_element_type=F32) for rs in subs]
        for rs, da in zip(subs, das):
            g = g_ref[rs, :].astype(F32)
            sg = _sigmoid(g)
            dg_ref[rs, :] = (da * u_ref[rs, :].astype(F32) * sg * (1.0 + g * (1.0 - sg))).astype(BF16)
            du_ref[rs, :] = (da * g * sg).astype(BF16)

    oblk = pl.BlockSpec((tm, tn), lambda j, i: (i, j))
    return pl.pallas_call(
        body, name=name, grid=(f // tn, t // tm),
        in_specs=[pl.BlockSpec((tm, d), lambda j, i: (i, 0)), pl.BlockSpec((tn, d), lambda j, i: (j, 0)), oblk, oblk],
        out_specs=[oblk] * 2,
        out_shape=[jax.ShapeDtypeStruct((t, f), BF16)] * 2,
        compiler_params=_params(("parallel", "parallel")),
    )(dh2b, w_d, gt, up)


CONV_HEAD = 2 * HALO


def _causal_conv(x_ref, xp_ref, cw_ref, xs, tt):
    first = pl.program_id(0) == 0
    xs[0:HALO, :] = jnp.where(first, 0.0, xp_ref[...])
    xs[HALO:HALO + CONV_HEAD, :] = x_ref[0:CONV_HEAD, :]
    xb = x_ref[...].astype(BF16)
    r, c = _iota2(tt)
    moved = [None] + [jnp.dot((c == r - sh).astype(BF16), xb, preferred_element_type=F32)
                      for sh in range(1, CONV_W)]

    def cols(cb):
        cs = slice(cb * LANES, (cb + 1) * LANES)
        c_head = c_main = None
        for j in range(CONV_W):
            sh = CONV_W - 1 - j
            w_j = cw_ref[j:j + 1, cs]
            x_m = x_ref[CONV_HEAD:tt, cs] if sh == 0 else moved[sh][CONV_HEAD:tt, cs]
            x_h = xs[HALO - sh:HALO - sh + CONV_HEAD, cs]
            c_main = x_m * w_j if c_main is None else c_main + x_m * w_j
            c_head = x_h * w_j if c_head is None else c_head + x_h * w_j
        return jnp.concatenate([c_head, c_main], axis=0)

    return cols


def _conv_specs(tt, ch):
    cur = pl.BlockSpec((tt, ch), lambda i: (i, 0))
    prev = pl.BlockSpec((HALO, ch), lambda i: (jnp.maximum(i * (tt // HALO) - 1, 0), 0))
    return cur, prev


def _conv_fwd(proj, conv_w, *, name):
    t = proj.shape[0]
    ch = conv_w.shape[1]
    nqk = HEADS * DK
    tt = _pick(t, (256,))
    cur, prev = _conv_specs(tt, ch)

    def body(x_ref, xp_ref, cw_ref, q_ref, k_ref, v_ref, xs):
        conv = _causal_conv(x_ref, xp_ref, cw_ref, xs, tt)
        for cb in range(ch // LANES):
            c = conv(cb)
            s = c * _sigmoid(c)
            if cb < 2 * HEADS:
                s = s * lax.rsqrt(jnp.sum(s * s, axis=-1, keepdims=True) + EPS)
                if cb < HEADS:
                    q_ref[:, cb * LANES:(cb + 1) * LANES] = (s * (DK ** -0.5)).astype(BF16)
                else:
                    k_ref[:, (cb - HEADS) * LANES:(cb - HEADS + 1) * LANES] = s.astype(BF16)
            else:
                v_ref[:, (cb - 2 * HEADS) * LANES:(cb - 2 * HEADS + 1) * LANES] = s.astype(BF16)

    return pl.pallas_call(
        body, name=name, grid=(t // tt,),
        in_specs=[cur, prev, pl.BlockSpec((CONV_W, ch), lambda i: (0, 0))],
        out_specs=[pl.BlockSpec((tt, nqk), lambda i: (i, 0)), pl.BlockSpec((tt, nqk), lambda i: (i, 0)),
                   pl.BlockSpec((tt, ch - 2 * nqk), lambda i: (i, 0))],
        out_shape=[jax.ShapeDtypeStruct((t, nqk), BF16), jax.ShapeDtypeStruct((t, nqk), BF16),
                   jax.ShapeDtypeStruct((t, ch - 2 * nqk), BF16)],
        scratch_shapes=[pltpu.VMEM((HALO + CONV_HEAD, ch), F32)],
        compiler_params=_params(("arbitrary",)),
    )(proj, proj, conv_w)


def _conv_bwd_pre(proj, conv_w, dq, dk, dv, *, name):
    t = proj.shape[0]
    ch = conv_w.shape[1]
    nqk = HEADS * DK
    tt = _pick(t, (256,))
    cur, prev = _conv_specs(tt, ch)

    def body(x_ref, xp_ref, cw_ref, dq_ref, dk_ref, dv_ref, dc_ref, xs):
        conv = _causal_conv(x_ref, xp_ref, cw_ref, xs, tt)
        for cb in range(ch // LANES):
            c = conv(cb)
            sg = _sigmoid(c)
            s = c * sg
            if cb < 2 * HEADS:
                if cb < HEADS:
                    d, scale = dq_ref[:, cb * LANES:(cb + 1) * LANES], DK ** -0.5
                else:
                    d, scale = dk_ref[:, (cb - HEADS) * LANES:(cb - HEADS + 1) * LANES], 1.0
                rinv = lax.rsqrt(jnp.sum(s * s, axis=-1, keepdims=True) + EPS)
                ds = scale * rinv * (d - s * (rinv * rinv) * jnp.sum(d * s, axis=-1, keepdims=True))
            else:
                ds = dv_ref[:, (cb - 2 * HEADS) * LANES:(cb - 2 * HEADS + 1) * LANES]
            dc_ref[:, cb * LANES:(cb + 1) * LANES] = ds * sg * (1.0 + c * (1.0 - sg))

    return pl.pallas_call(
        body, name=name, grid=(t // tt,),
        in_specs=[cur, prev, pl.BlockSpec((CONV_W, ch), lambda i: (0, 0)),
                  pl.BlockSpec((tt, nqk), lambda i: (i, 0)), pl.BlockSpec((tt, nqk), lambda i: (i, 0)),
                  pl.BlockSpec((tt, ch - 2 * nqk), lambda i: (i, 0))],
        out_specs=pl.BlockSpec((tt, ch), lambda i: (i, 0)),
        out_shape=jax.ShapeDtypeStruct((t, ch), F32),
        scratch_shapes=[pltpu.VMEM((HALO + CONV_HEAD, ch), F32)],
        compiler_params=_params(("arbitrary",)),
    )(proj, proj, conv_w, dq, dk, dv)


def _conv_bwd(proj, dc, conv_w, dproj, *, name):
    t = proj.shape[0]
    ch = conv_w.shape[1]
    tt = _pick(t, (256,))
    nt = t // tt
    cur = pl.BlockSpec((tt, ch), lambda i: (i, 0))
    nxt = pl.BlockSpec((HALO, ch), lambda i: (jnp.minimum((i + 1) * (tt // HALO), t // HALO - 1), 0))

    tail = 2 * HALO
    main = tt - tail

    def body(x_ref, d_ref, dn_ref, cw_ref, alias_ref, dx_ref, dw_ref, ds):
        last = pl.program_id(0) == nt - 1
        ds[0:tail, :] = d_ref[main:tt, :]
        ds[tail:tail + HALO, :] = jnp.where(last, 0.0, dn_ref[...])

        @pl.when(pl.program_id(0) == 0)
        def _():
            dw_ref[...] = jnp.zeros_like(dw_ref)

        dcb = d_ref[...].astype(BF16)
        r, c = _iota2(tt)
        moved = [None] + [jnp.dot((c == r + sh).astype(BF16), dcb, preferred_element_type=F32)
                          for sh in range(1, CONV_W)]
        for cb in range(ch // LANES):
            cols = slice(cb * LANES, (cb + 1) * LANES)
            x_m, x_t = x_ref[0:main, cols], x_ref[main:tt, cols]
            acc_m = acc_t = None
            for j in range(CONV_W):
                sh = CONV_W - 1 - j
                w_j = cw_ref[j:j + 1, cols]
                dj_m = d_ref[0:main, cols] if sh == 0 else moved[sh][0:main, cols]
                dj_t = ds[sh:sh + tail, cols]
                acc_m = dj_m * w_j if acc_m is None else acc_m + dj_m * w_j
                acc_t = dj_t * w_j if acc_t is None else acc_t + dj_t * w_j
                dw_ref[j:j + 1, cols] += (jnp.sum(dj_m * x_m, axis=0, keepdims=True)
                                          + jnp.sum(dj_t * x_t, axis=0, keepdims=True))
            dx_ref[0:main, cols] = acc_m.astype(BF16)
            dx_ref[main:tt, cols] = acc_t.astype(BF16)

    return pl.pallas_call(
        body, name=name, grid=(nt,),
        in_specs=[cur, cur, nxt, pl.BlockSpec((CONV_W, ch), lambda i: (0, 0)), pl.BlockSpec(memory_space=pl.ANY)],
        out_specs=[pl.BlockSpec((tt, ch), lambda i: (i, 0)), pl.BlockSpec((HALO, ch), lambda i: (0, 0))],
        out_shape=[jax.ShapeDtypeStruct(dproj.shape, BF16), jax.ShapeDtypeStruct((HALO, ch), F32)],
        scratch_shapes=[pltpu.VMEM((tail + HALO, ch), F32)],
        input_output_aliases={4: 0},
        compiler_params=_params(("arbitrary",)),
    )(proj, dc, dc, conv_w, dproj)


def _bg_fwd(projs, alog_row, dtb_row, *, name):
    t = projs.shape[0]
    tt = _pick(t, (512, 256))

    def body(p_ref, al_ref, db_ref, o_ref):
        p = p_ref[...]
        lane = lax.broadcasted_iota(jnp.int32, p.shape, 1)
        z = p + db_ref[...]
        sp = jnp.maximum(z, 0.0) + jnp.log1p(jnp.exp(-jnp.abs(z)))
        g = -jnp.exp(al_ref[...]) * sp
        o_ref[...] = jnp.where(lane < HEADS, _sigmoid(p), jnp.where(lane < 2 * HEADS, g, 0.0))

    blk = pl.BlockSpec((tt, LANES), lambda i: (i, 0))
    row = pl.BlockSpec((1, LANES), lambda i: (0, 0))
    return pl.pallas_call(
        body, name=name, grid=(t // tt,), in_specs=[blk, row, row], out_specs=blk,
        out_shape=jax.ShapeDtypeStruct((t, LANES), F32), compiler_params=_params(("parallel",)),
    )(projs, alog_row, dtb_row)


def _bg_bwd(projs, dbg, alog_row, dtb_row, *, name):
    t = projs.shape[0]
    tt = _pick(t, (512, 256))

    def body(p_ref, d_ref, al_ref, db_ref, o_ref, dal_ref, ddb_ref):
        p, d = p_ref[...], d_ref[...]
        lane = lax.broadcasted_iota(jnp.int32, p.shape, 1)
        isg = (lane >= HEADS) & (lane < 2 * HEADS)
        be = _sigmoid(p)
        z = p + db_ref[...]
        sp = jnp.maximum(z, 0.0) + jnp.log1p(jnp.exp(-jnp.abs(z)))
        ea = jnp.exp(al_ref[...])
        d_aa = jnp.where(isg, d * (-ea) * _sigmoid(z), 0.0)
        o_ref[...] = jnp.where(lane < HEADS, d * be * (1.0 - be), d_aa).astype(BF16)

        @pl.when(pl.program_id(0) == 0)
        def _():
            dal_ref[...] = jnp.zeros_like(dal_ref)
            ddb_ref[...] = jnp.zeros_like(ddb_ref)

        dal_ref[...] += jnp.sum(jnp.where(isg, d * (-ea) * sp, 0.0), axis=0, keepdims=True)
        ddb_ref[...] += jnp.sum(d_aa, axis=0, keepdims=True)

    blk = pl.BlockSpec((tt, LANES), lambda i: (i, 0))
    row = pl.BlockSpec((1, LANES), lambda i: (0, 0))
    return pl.pallas_call(
        body, name=name, grid=(t // tt,), in_specs=[blk, blk, row, row], out_specs=[blk, row, row],
        out_shape=[jax.ShapeDtypeStruct((t, LANES), BF16), jax.ShapeDtypeStruct((1, LANES), F32),
                   jax.ShapeDtypeStruct((1, LANES), F32)],
        compiler_params=_params(("arbitrary",)),
    )(projs, dbg, alog_row, dtb_row)


def _hn_specs(n_chunks, rev=False):
    def nn(n):
        return n_chunks - 1 - n if rev else n
    tok = lambda w: pl.BlockSpec((CHUNK, HB * w), lambda g, n: (nn(n), g))
    per = lambda a, b: pl.BlockSpec((HB, None, a, b), lambda g, n: (g, nn(n), 0, 0))
    return tok, per


def _hcols(hh, w):
    return slice(hh * w, (hh + 1) * w)


def _decay(gc_col, gc_row):
    r, c = _iota2(CHUNK)
    d = jnp.exp(jnp.minimum(gc_col - gc_row, 0.0))
    return jnp.where(r >= c, d, 0.0), jnp.where(r > c, d, 0.0)


def _gdn_prep(q, k, v, beta_r, g_r, *, name):
    t = q.shape[0]
    n_chunks = t // CHUNK
    cps = 4 if n_chunks % 4 == 0 else 1
    tok = lambda w: pl.BlockSpec((cps * CHUNK, HB * w), lambda g, n: (n, g))
    per = lambda a, b: pl.BlockSpec((HB, cps, a, b), lambda g, n: (g, n, 0, 0))

    def body(q_ref, k_ref, v_ref, b_ref, g_ref, gc_ref, ti_ref, u_ref, w_ref, p_ref):
        r, c = _iota2(CHUNK)
        its = [(h, cc) for cc in range(cps) for h in range(HB)]
        rows = lambda cc: slice(cc * CHUNK, (cc + 1) * CHUNK)
        n = range(len(its))
        kk = [k_ref[rows(cc), _hcols(h, DK)] for h, cc in its]
        gc_col = [jnp.sum(jnp.where(c <= r, jnp.broadcast_to(g_ref[h, cc], (CHUNK, CHUNK)), 0.0), axis=1,
                          keepdims=True) for h, cc in its]
        gc_row = [_col_to_row(gc_col[i]) for i in n]
        beta_col = [_row_to_col(b_ref[h, cc]) for h, cc in its]
        dec = [_decay(gc_col[i], gc_row[i]) for i in n]
        kb = [kk[i] * beta_col[i] for i in n]
        for i, (h, cc) in enumerate(its):
            gc_ref[h, cc] = gc_row[i]
            p_ref[h, cc] = (_dot_nt(q_ref[rows(cc), _hcols(h, DK)], kk[i]) * dec[i][0]).astype(BF16)
        pw = [_dot_nt(kb[i], kk[i]) * dec[i][1] for i in n]
        y = [-pw[i] for i in n]
        for _ in range(5):
            pw = [_dot(pw[i], pw[i]) for i in n]
            yp = [_dot(y[i], pw[i]) for i in n]
            y = [y[i] + pw[i] + yp[i] for i in n]
        vb = [v_ref[rows(cc), _hcols(h, DV)] * beta_col[i] for i, (h, cc) in enumerate(its)]
        kbe = [kb[i] * jnp.exp(gc_col[i]) for i in n]
        yv = [_dot(y[i], vb[i]) for i in n]
        yk = [_dot(y[i], kbe[i]) for i in n]
        for i, (h, cc) in enumerate(its):
            ti_ref[h, cc] = y[i].astype(BF16)
            u_ref[rows(cc), _hcols(h, DV)] = vb[i] + yv[i]
            w_ref[rows(cc), _hcols(h, DK)] = (kbe[i] + yk[i]).astype(BF16)

    return pl.pallas_call(
        body, name=name, grid=(HEADS // HB, n_chunks // cps),
        in_specs=[tok(DK), tok(DK), tok(DV), per(1, CHUNK), per(1, CHUNK)],
        out_specs=[per(1, CHUNK), per(CHUNK, CHUNK), tok(DV), tok(DK), per(CHUNK, CHUNK)],
        out_shape=[jax.ShapeDtypeStruct((HEADS, n_chunks, 1, CHUNK), F32),
                   jax.ShapeDtypeStruct((HEADS, n_chunks, CHUNK, CHUNK), BF16),
                   jax.ShapeDtypeStruct((t, HEADS * DV), F32), jax.ShapeDtypeStruct((t, HEADS * DK), BF16),
                   jax.ShapeDtypeStruct((HEADS, n_chunks, CHUNK, CHUNK), BF16)],
        compiler_params=_params(("parallel", "parallel")),
    )(q, k, v, beta_r, g_r)


def _interleave(*step_gens):
    live = list(step_gens)
    while live:
        for g in list(live):
            try:
                next(g)
            except StopIteration:
                live.remove(g)


def _gdn_scan_steps(q_ref, k_ref, u_ref, w_ref, p_ref, gc_ref, o_ref, s_ref, st):
    hs = range(HB)
    s = [st[h] for h in hs]
    gc_row = [gc_ref[h] for h in hs]
    gc_col = [_row_to_col(gc_row[h]) for h in hs]
    glast = [gc_row[h][:, CHUNK - 1:CHUNK] for h in hs]
    for h in hs:
        s_ref[h] = s[h].astype(BF16)
    yield
    ws = [_dot(w_ref[:, _hcols(h, DK)], s[h]) for h in hs]
    yield
    qs = [_dot(q_ref[:, _hcols(h, DK)] * jnp.exp(gc_col[h]), s[h]) for h in hs]
    vn = [u_ref[:, _hcols(h, DV)] - ws[h] for h in hs]
    yield
    pv = [_dot(p_ref[h], vn[h]) for h in hs]
    yield
    kv = [_dot_tn(k_ref[:, _hcols(h, DK)] * jnp.exp(glast[h] - gc_col[h]), vn[h]) for h in hs]
    yield
    for h in hs:
        o_ref[:, _hcols(h, DV)] = (qs[h] + pv[h]).astype(BF16)
        st[h] = s[h] * jnp.exp(glast[h]) + kv[h]


def _gdn_bwd_steps(q_ref, k_ref, v_ref, b_ref, gc_ref, ti_ref, u_ref, w_ref, s_ref, do_ref,
                   dq_ref, dk_ref, dv_ref, db_ref, dg_ref, dst):
    r, c = _iota2(CHUNK)
    rows = lax.broadcasted_iota(jnp.int32, (CHUNK, 1), 0)
    rsum = lambda a: jnp.sum(a, axis=1, keepdims=True)
    hs = range(HB)
    qq = [q_ref[:, _hcols(h, DK)] for h in hs]
    kk = [k_ref[:, _hcols(h, DK)] for h in hs]
    ww = [w_ref[:, _hcols(h, DK)] for h in hs]
    uu = [u_ref[:, _hcols(h, DV)] for h in hs]
    d_o = [do_ref[:, _hcols(h, DV)] for h in hs]
    s = [s_ref[h].astype(F32) for h in hs]
    d_s = [dst[h] for h in hs]
    gc_row = [gc_ref[h] for h in hs]
    gc_col = [_row_to_col(gc_row[h]) for h in hs]
    beta_col = [_row_to_col(b_ref[h]) for h in hs]
    eg = [jnp.exp(gc_col[h]) for h in hs]
    glast = [gc_row[h][:, CHUNK - 1:CHUNK] for h in hs]
    eglast = [jnp.exp(glast[h]) for h in hs]
    e2 = [jnp.exp(glast[h] - gc_col[h]) for h in hs]
    dec = [_decay(gc_col[h], gc_row[h]) for h in hs]
    kb = [kk[h] * beta_col[h] for h in hs]
    ke = [kk[h] * e2[h] for h in hs]
    qe = [qq[h] * eg[h] for h in hs]
    yield
    kkt = [_dot_nt(kb[h], kk[h]) for h in hs]
    qk = [_dot_nt(qq[h], kk[h]) for h in hs]
    ws = [_dot(ww[h], s[h]) for h in hs]
    yield
    a = [kkt[h] * dec[h][1] for h in hs]
    pp = [qk[h] * dec[h][0] for h in hs]
    vn = [uu[h] - ws[h] for h in hs]
    t1 = [_dot_tn(pp[h], d_o[h]) for h in hs]
    t2 = [_dot(ke[h], d_s[h]) for h in hs]
    yield
    dqe = [_dot_nt(d_o[h], s[h]) for h in hs]
    dke = [_dot_nt(vn[h], d_s[h]) for h in hs]
    dqk = [_dot_nt(d_o[h], vn[h]) * dec[h][0] for h in hs]
    yield
    dvn = [t1[h] + t2[h] for h in hs]
    t3 = [_dot_tn(qe[h], d_o[h]) for h in hs]
    t4 = [_dot_tn(ww[h], dvn[h]) for h in hs]
    dw = [-_dot_nt(dvn[h], s[h]) for h in hs]
    yield
    de2 = [rsum(dke[h] * ke[h]) for h in hs]
    dglast = [jnp.sum(de2[h], axis=0, keepdims=True)
              + eglast[h] * jnp.sum(rsum(d_s[h] * s[h]), axis=0, keepdims=True) for h in hs]
    for h in hs:
        dst[h] = d_s[h] * eglast[h] + t3[h] - t4[h]
    yy = [ti_ref[h] for h in hs]
    t5 = [_dot_tn(yy[h], dvn[h]) for h in hs]
    t6 = [_dot_tn(yy[h], dw[h]) for h in hs]
    yield
    dvb = [dvn[h] + t5[h] for h in hs]
    dkbe = [dw[h] + t6[h] for h in hs]
    t7 = [_dot_nt(dvb[h], uu[h]) for h in hs]
    t8 = [_dot_nt(dkbe[h], ww[h]) for h in hs]
    yield
    d_a = [jnp.where(r > c, -(t7[h] + t8[h]), 0.0) for h in hs]
    dm = [d_a[h] * dec[h][1] for h in hs]
    t9 = [_dot(dm[h], kk[h]) for h in hs]
    t10 = [_dot(dqk[h], kk[h]) for h in hs]
    yield
    t11 = [_dot_tn(dqk[h], qq[h]) for h in hs]
    t12 = [_dot_tn(dm[h], kb[h]) for h in hs]
    yield
    for h in hs:
        dkb = t9[h] + dkbe[h] * eg[h]
        e_mat = d_a[h] * a[h] + dqk[h] * qk[h]
        dgc = (rsum(dqe[h] * qe[h]) - de2[h] + rsum(dkbe[h] * kb[h] * eg[h]) + rsum(e_mat)
               - _row_to_col(jnp.sum(e_mat, axis=0, keepdims=True)))
        dgc = dgc + jnp.where(rows == CHUNK - 1, dglast[h], 0.0)
        dq_ref[:, _hcols(h, DK)] = dqe[h] * eg[h] + t10[h]
        dk_ref[:, _hcols(h, DK)] = t11[h] + dke[h] * e2[h] + t12[h] + dkb * beta_col[h]
        dv_ref[:, _hcols(h, DV)] = dvb[h] * beta_col[h]
        dbeta = rsum(dkb * kk[h]) + rsum(dvb[h] * v_ref[:, _hcols(h, DV)])
        db_ref[h] = _col_to_row(dbeta)
        dg_ref[h] = jnp.sum(jnp.where(r >= c, jnp.broadcast_to(dgc, (CHUNK, CHUNK)), 0.0), axis=0, keepdims=True)


def _rot(x, cos2, sin2):
    return x * cos2 + pltpu.roll(x, DK // 2, 1) * sin2


def _unrot(d, cos2, sin2):
    return d * cos2 + pltpu.roll(d * sin2, DK // 2, 1)


def _ret_consts(lg):
    r, c = _iota2(CHUNK)
    dm = jnp.where(r >= c, jnp.exp((r - c).astype(F32) * lg), 0.0)
    pos = lax.broadcasted_iota(jnp.int32, (CHUNK, 1), 0).astype(F32)
    return dm, jnp.exp((pos + 1.0) * lg), jnp.exp((CHUNK - 1.0 - pos) * lg), jnp.exp(CHUNK * lg)


def _ret_fwd_steps(q_ref, k_ref, v_ref, cos_ref, sin_ref, lg_ref, o_ref, s_ref, st):
    cos2v, sin2v = cos_ref[...], sin_ref[...]
    hs = range(HB)
    s = [st[h] for h in hs]
    for h in hs:
        s_ref[h] = s[h].astype(BF16)
    cst = [_ret_consts(lg_ref[h][:, 0:1]) for h in hs]
    qq = [_rot(q_ref[:, _hcols(h, DK)], cos2v, sin2v) for h in hs]
    kk = [_rot(k_ref[:, _hcols(h, DK)], cos2v, sin2v) * (DK ** -0.5) for h in hs]
    vv = [v_ref[:, _hcols(h, DV)] for h in hs]
    yield
    sc = [_dot_nt(qq[h], kk[h]) * cst[h][0] for h in hs]
    yield
    qs = [_dot(qq[h], s[h]) for h in hs]
    yield
    kv = [_dot_tn(kk[h] * cst[h][2], vv[h]) for h in hs]
    yield
    sv = [_dot(sc[h], vv[h]) for h in hs]
    yield
    for h in hs:
        o_ref[:, _hcols(h, DV)] = (sv[h] + qs[h] * cst[h][1]).astype(BF16)
        st[h] = s[h] * cst[h][3] + kv[h]


def _mix_fwd(q, k, u, w, p, gc_r, proj, cos2, sin2, lg_tab, col_q, col_k, col_v, *, name):
    t = q.shape[0]
    n_chunks = t // CHUNK
    tok, per = _hn_specs(n_chunks)
    bq, bk, bv = col_q // (HB * DK), col_k // (HB * DK), col_v // (HB * DV)

    def body(*refs):
        st_a, st_b = refs[-2:]

        @pl.when(pl.program_id(1) == 0)
        def _():
            st_a[...] = jnp.zeros_like(st_a)
            st_b[...] = jnp.zeros_like(st_b)

        _interleave(_gdn_scan_steps(*refs[0:6], refs[12], refs[13], st_a),
                    _ret_fwd_steps(*refs[6:12], refs[14], refs[15], st_b))

    state = jax.ShapeDtypeStruct((HEADS, n_chunks, DK, DV), BF16)
    return pl.pallas_call(
        body, name=name, grid=(HEADS // HB, n_chunks),
        in_specs=[tok(DK), tok(DK), tok(DV), tok(DK), per(CHUNK, CHUNK), per(1, CHUNK),
                  pl.BlockSpec((CHUNK, HB * DK), lambda g, n: (n, bq + g)),
                  pl.BlockSpec((CHUNK, HB * DK), lambda g, n: (n, bk + g)),
                  pl.BlockSpec((CHUNK, HB * DV), lambda g, n: (n, bv + g)),
                  pl.BlockSpec((CHUNK, DK), lambda g, n: (n, 0)), pl.BlockSpec((CHUNK, DK), lambda g, n: (n, 0)),
                  pl.BlockSpec((HB, 1, LANES), lambda g, n: (g, 0, 0))],
        out_specs=[tok(DV), per(DK, DV), tok(DV), per(DK, DV)],
        out_shape=[jax.ShapeDtypeStruct((t, HEADS * DV), BF16), state, jax.ShapeDtypeStruct((t, HEADS * DV), BF16), state],
        scratch_shapes=[pltpu.VMEM((HB, DK, DV), F32)] * 2,
        compiler_params=_params(("arbitrary", "arbitrary")),
    )(q, k, u, w, p, gc_r, proj, proj, proj, cos2, sin2, lg_tab)


def _ret_bwd_steps(q_ref, k_ref, v_ref, cos_ref, sin_ref, lg_ref, s_ref, do_ref, d_ref, dst):
    nqk = HEADS * DK
    cos2v, sin2v = cos_ref[...], sin_ref[...]
    hs = range(HB)
    s = [s_ref[h].astype(F32) for h in hs]
    d_s = [dst[h] for h in hs]
    d_o = [do_ref[:, _hcols(h, DV)] for h in hs]
    cst = [_ret_consts(lg_ref[h][:, 0:1]) for h in hs]
    qq = [_rot(q_ref[:, _hcols(h, DK)], cos2v, sin2v) for h in hs]
    kk = [_rot(k_ref[:, _hcols(h, DK)], cos2v, sin2v) * (DK ** -0.5) for h in hs]
    vv = [v_ref[:, _hcols(h, DV)] for h in hs]
    dxo = [d_o[h] * cst[h][1] for h in hs]
    yield
    sc = [_dot_nt(qq[h], kk[h]) * cst[h][0] for h in hs]
    dsc = [_dot_nt(d_o[h], vv[h]) * cst[h][0] for h in hs]
    yield
    t1 = [_dot(kk[h] * cst[h][2], d_s[h]) for h in hs]
    t2 = [_dot_nt(dxo[h], s[h]) for h in hs]
    yield
    t3 = [_dot_nt(vv[h], d_s[h]) for h in hs]
    t4 = [_dot_tn(qq[h], dxo[h]) for h in hs]
    yield
    t5 = [_dot_tn(sc[h], d_o[h]) for h in hs]
    t6 = [_dot(dsc[h], kk[h]) for h in hs]
    yield
    t7 = [_dot_tn(dsc[h], qq[h]) for h in hs]
    yield
    for h in hs:
        dst[h] = d_s[h] * cst[h][3] + t4[h]
        d_ref[:, 2 * nqk + h * DV:2 * nqk + (h + 1) * DV] = (t5[h] + t1[h]).astype(BF16)
        d_ref[:, _hcols(h, DK)] = _unrot(t6[h] + t2[h], cos2v, sin2v).astype(BF16)
        d_ref[:, nqk + h * DK:nqk + (h + 1) * DK] = (
            _unrot(t7[h] + t3[h] * cst[h][2], cos2v, sin2v) * (DK ** -0.5)).astype(BF16)


def _mix_bwd(q, k, v, beta_r, gc_r, ti, u, w, s_a, do_a, proj, cos2, sin2, lg_tab, s_b, do_b, dproj,
             col_q, col_k, col_v, *, name):
    t = q.shape[0]
    n_chunks = t // CHUNK
    tok, per = _hn_specs(n_chunks, rev=True)
    bq, bk, bv = col_q // (HB * DK), col_k // (HB * DK), col_v // (HB * DV)
    rv = lambda n: n_chunks - 1 - n
    nqk, wid = HEADS * DK, HEADS * (2 * DK + DV)
    assert HB == HEADS and col_k == col_q + nqk and col_v == col_k + nqk and col_q % wid == 0
    n_a, n_b = 10, 8

    def body(*refs):
        ins_a, ins_b = refs[:n_a], refs[n_a:n_a + n_b]
        outs = refs[n_a + n_b + 1:n_a + n_b + 7]
        dst_a, dst_b = refs[-2:]

        @pl.when(pl.program_id(1) == 0)
        def _():
            dst_a[...] = jnp.zeros_like(dst_a)
            dst_b[...] = jnp.zeros_like(dst_b)

        _interleave(_gdn_bwd_steps(*ins_a, *outs[:5], dst_a), _ret_bwd_steps(*ins_b, outs[5], dst_b))

    rows = jax.ShapeDtypeStruct((HEADS, n_chunks, 1, CHUNK), F32)
    return pl.pallas_call(
        body, name=name, grid=(HEADS // HB, n_chunks),
        in_specs=[tok(DK), tok(DK), tok(DV), per(1, CHUNK), per(1, CHUNK), per(CHUNK, CHUNK), tok(DV), tok(DK),
                  per(DK, DV), tok(DV),
                  pl.BlockSpec((CHUNK, HB * DK), lambda g, n: (rv(n), bq + g)),
                  pl.BlockSpec((CHUNK, HB * DK), lambda g, n: (rv(n), bk + g)),
                  pl.BlockSpec((CHUNK, HB * DV), lambda g, n: (rv(n), bv + g)),
                  pl.BlockSpec((CHUNK, DK), lambda g, n: (rv(n), 0)),
                  pl.BlockSpec((CHUNK, DK), lambda g, n: (rv(n), 0)),
                  pl.BlockSpec((HB, 1, LANES), lambda g, n: (g, 0, 0)),
                  per(DK, DV), tok(DV), pl.BlockSpec(memory_space=pl.ANY)],
        out_specs=[tok(DK), tok(DK), tok(DV), per(1, CHUNK), per(1, CHUNK),
                   pl.BlockSpec((CHUNK, wid), lambda g, n: (rv(n), col_q // wid))],
        out_shape=[jax.ShapeDtypeStruct((t, HEADS * DK), F32), jax.ShapeDtypeStruct((t, HEADS * DK), F32),
                   jax.ShapeDtypeStruct((t, HEADS * DV), F32), rows, rows, jax.ShapeDtypeStruct(dproj.shape, BF16)],
        scratch_shapes=[pltpu.VMEM((HB, DK, DV), F32)] * 2,
        input_output_aliases={n_a + n_b: 5},
        compiler_params=_params(("arbitrary", "arbitrary")),
    )(q, k, v, beta_r, gc_r, ti, u, w, s_a, do_a, proj, proj, proj, cos2, sin2, lg_tab, s_b, do_b, dproj)


def _merge_parts(oa, ob, z, rg, ga, gb):
    oa, ob = oa.astype(F32), ob.astype(F32)
    ra = lax.rsqrt(jnp.mean(oa * oa, axis=-1, keepdims=True) + EPS)
    xa = oa * ra
    mu = jnp.mean(ob, axis=-1, keepdims=True)
    cen = ob - mu
    rb = lax.rsqrt(jnp.mean(cen * cen, axis=-1, keepdims=True) + EPS)
    xb = cen * rb
    sz, sr = _sigmoid(z), _sigmoid(rg)
    return ra, xa, rb, xb, sz, sr, _sigmoid(ga), _sigmoid(gb)


def _merge_specs(tt, d, cz, crg, cga, cgb):
    blk = pl.BlockSpec((tt, d), lambda i: (i, 0))
    pcol = lambda c: pl.BlockSpec((tt, d), lambda i: (i, c // d))
    row = pl.BlockSpec((1, d), lambda i: (0, 0))
    return blk, [blk, blk, pcol(cz), pcol(crg), pcol(cga), pcol(cgb), row, row], row


def _merge_fwd(oa, ob, proj, wa, wb, cz, crg, cga, cgb, *, name):
    t, d = oa.shape
    tt = _pick(t, (256,))
    blk, in_specs, _ = _merge_specs(tt, d, cz, crg, cga, cgb)

    def body(oa_ref, ob_ref, z_ref, rg_ref, ga_ref, gb_ref, wa_ref, wb_ref, o_ref):
        for h in range(HEADS):
            cols = slice(h * DV, (h + 1) * DV)
            z, rg = z_ref[:, cols], rg_ref[:, cols]
            _, xa, _, xb, sz, sr, sga, sgb = _merge_parts(
                oa_ref[:, cols], ob_ref[:, cols], z, rg, ga_ref[:, cols], gb_ref[:, cols])
            o_a = xa * wa_ref[:, cols] * (z * sz)
            o_b = xb * wb_ref[:, cols] * (rg * sr)
            o_ref[:, cols] = (sga * o_a + sgb * o_b).astype(BF16)

    return pl.pallas_call(
        body, name=name, grid=(t // tt,), in_specs=in_specs, out_specs=blk,
        out_shape=jax.ShapeDtypeStruct((t, d), BF16), compiler_params=_params(("parallel",)),
    )(oa, ob, proj, proj, proj, proj, wa, wb)


def _merge_bwd(dmix, oa, ob, proj, wa, wb, cz, crg, cga, cgb, *, name, side):
    t, d = oa.shape
    tt = _pick(t, (256,))
    nt = t // tt
    blk, in_specs, row = _merge_specs(tt, d, cz, crg, cga, cgb)
    assert (crg, cga, cgb) == (cz + d, cz + 2 * d, cz + 3 * d) and cz % (4 * d) == 0
    n_si, n_so = len(side.ins), len(side.out_shapes)
    assert len(side.phases) == 2

    def body(*refs):
        dm_ref, oa_ref, ob_ref, z_ref, rg_ref, ga_ref, gb_ref, wa_ref, wb_ref = refs[:9]
        s_in = refs[9:9 + n_si]
        doa_ref, dob_ref, dp_ref, dwa_ref, dwb_ref = refs[9 + n_si:14 + n_si]
        s_out, ssem, rsem = refs[14 + n_si:14 + n_si + n_so], refs[-2], refs[-1]
        dz_ref, drg_ref, dga_ref, dgb_ref = [dp_ref.at[:, i * d:(i + 1) * d] for i in range(4)]

        @pl.when(pl.program_id(0) == 0)
        def _():
            side.phases[0](s_in, s_out, ssem, rsem)
            dwa_ref[...] = jnp.zeros_like(dwa_ref)
            dwb_ref[...] = jnp.zeros_like(dwb_ref)

        for h in range(HEADS):
            cols = slice(h * DV, (h + 1) * DV)
            z, rg, wa_h, wb_h, dmx = z_ref[:, cols], rg_ref[:, cols], wa_ref[:, cols], wb_ref[:, cols], dm_ref[:, cols]
            ra, xa, rb, xb, sz, sr, sga, sgb = _merge_parts(
                oa_ref[:, cols], ob_ref[:, cols], z, rg, ga_ref[:, cols], gb_ref[:, cols])
            na, nb = xa * wa_h, xb * wb_h
            sil_z, sil_r = z * sz, rg * sr
            o_a, o_b = na * sil_z, nb * sil_r
            dga_ref[:, cols] = (dmx * o_a * sga * (1.0 - sga)).astype(BF16)
            dgb_ref[:, cols] = (dmx * o_b * sgb * (1.0 - sgb)).astype(BF16)
            d_oa, d_ob = dmx * sga, dmx * sgb
            dz_ref[:, cols] = (d_oa * na * sz * (1.0 + z * (1.0 - sz))).astype(BF16)
            drg_ref[:, cols] = (d_ob * nb * sr * (1.0 + rg * (1.0 - sr))).astype(BF16)
            dna, dnb = d_oa * sil_z, d_ob * sil_r
            dwa_ref[:, cols] += jnp.sum(dna * xa, axis=0, keepdims=True)
            dwb_ref[:, cols] += jnp.sum(dnb * xb, axis=0, keepdims=True)
            gwa, gwb = dna * wa_h, dnb * wb_h
            doa_ref[:, cols] = ra * (gwa - xa * jnp.mean(gwa * xa, axis=-1, keepdims=True))
            dob_ref[:, cols] = rb * (gwb - jnp.mean(gwb, axis=-1, keepdims=True)
                                     - xb * jnp.mean(gwb * xb, axis=-1, keepdims=True))

        @pl.when(pl.program_id(0) == nt - 1)
        def _():
            side.phases[1](s_in, s_out, ssem, rsem)

    hbm = pl.BlockSpec(memory_space=pl.ANY)
    return pl.pallas_call(
        body, name=name, grid=(nt,), in_specs=[blk] + in_specs + [hbm] * n_si,
        out_specs=[blk, blk, pl.BlockSpec((tt, 4 * d), lambda i: (i, cz // (4 * d))), row, row] + [hbm] * n_so,
        out_shape=[jax.ShapeDtypeStruct((t, d), F32)] * 2 + [jax.ShapeDtypeStruct(proj.shape, BF16)]
        + [jax.ShapeDtypeStruct((1, d), F32)] * 2 + side.out_shapes,
        scratch_shapes=side.sems(),
        compiler_params=_params(("arbitrary",)),
    )(dmix, oa, ob, proj, proj, proj, proj, wa, wb, *side.ins)


def _row_block(rows, cols, itemsize=4, target=1 << 20):
    for rb in (512, 256, 128, 64, 32, 16, 8):
        if rows % rb == 0 and rb * cols * itemsize <= target:
            return rb
    return rows


def _adamw(w, g, m, v, *, name):
    rows, cols = w.shape
    rb = _row_block(rows, cols)

    def body(w_ref, g_ref, m_ref, v_ref, d_ref, nm_ref, nv_ref):
        gg = g_ref[...]
        mm = ADAM_B1 * m_ref[...] + (1.0 - ADAM_B1) * gg
        vv = ADAM_B2 * v_ref[...] + (1.0 - ADAM_B2) * (gg * gg)
        m_hat = mm / (1.0 - ADAM_B1 ** ADAM_STEP)
        v_hat = vv / (1.0 - ADAM_B2 ** ADAM_STEP)
        d_ref[...] = -ADAM_LR * (m_hat / (jnp.sqrt(v_hat) + ADAM_EPS) + ADAM_WD * w_ref[...])
        nm_ref[...] = mm
        nv_ref[...] = vv

    blk = pl.BlockSpec((rb, cols), lambda i: (i, 0))
    return pl.pallas_call(
        body, name=name, grid=(rows // rb,), in_specs=[blk] * 4, out_specs=[blk] * 3,
        out_shape=[jax.ShapeDtypeStruct((rows, cols), F32)] * 3, compiler_params=_params(("parallel",)),
    )(w, g, m, v)


def _adamw_halves(w, mine, other, c_idx, m, v, *, name):
    rows, cols = w.shape
    hr = rows // 2
    rb = _row_block(hr, cols, 4, 2 << 20)
    nb = hr // rb

    def body(c_ref, w_ref, a_ref, b_ref, m_ref, v_ref, g_ref, d_ref, nm_ref, nv_ref):
        gg = jnp.where(pl.program_id(0) // nb == c_ref[0], a_ref[...], b_ref[...])
        mm = ADAM_B1 * m_ref[...] + (1.0 - ADAM_B1) * gg
        vv = ADAM_B2 * v_ref[...] + (1.0 - ADAM_B2) * (gg * gg)
        m_hat = mm / (1.0 - ADAM_B1 ** ADAM_STEP)
        v_hat = vv / (1.0 - ADAM_B2 ** ADAM_STEP)
        g_ref[...] = gg
        d_ref[...] = -ADAM_LR * (m_hat / (jnp.sqrt(v_hat) + ADAM_EPS) + ADAM_WD * w_ref[...])
        nm_ref[...] = mm
        nv_ref[...] = vv

    blk = pl.BlockSpec((rb, cols), lambda i, c: (i, 0))
    half = pl.BlockSpec((rb, cols), lambda i, c: (i % nb, 0))
    return pl.pallas_call(
        body, name=name,
        grid_spec=pltpu.PrefetchScalarGridSpec(
            num_scalar_prefetch=1, grid=(rows // rb,), in_specs=[blk, half, half, blk, blk], out_specs=[blk] * 4),
        out_shape=[jax.ShapeDtypeStruct((rows, cols), F32)] * 4, compiler_params=_params(("parallel",)),
    )(c_idx, w, mine, other, m, v)


def _pair_add(g, rsib, c_idx, *, name):
    _, _, hr, cols = g.shape
    rb = _row_block(hr, cols, 2, 4 << 20)

    def body(c_ref, g_ref, r_ref, o_ref):
        o_ref[...] = (g_ref[...].astype(F32) + r_ref[...].astype(F32)).astype(BF16)

    return pl.pallas_call(
        body, name=name,
        grid_spec=pltpu.PrefetchScalarGridSpec(
            num_scalar_prefetch=1, grid=(N_CHIPS, hr // rb),
            in_specs=[pl.BlockSpec((None, None, rb, cols), lambda j, i, c: (j, c[0], i, 0)),
                      pl.BlockSpec((None, rb, cols), lambda j, i, c: (j, i, 0))],
            out_specs=pl.BlockSpec((None, rb, cols), lambda j, i, c: (j, i, 0))),
        out_shape=jax.ShapeDtypeStruct((N_CHIPS, hr, cols), BF16),
        compiler_params=_params(("parallel", "parallel")),
    )(c_idx, g, rsib)


def _chip_sum(qb, *, name):
    _, hr, cols = qb.shape
    rb = _row_block(hr, cols, 2, 1 << 20)

    def body(q_ref, o_ref):
        acc = q_ref[0].astype(F32)
        for i in range(1, N_CHIPS):
            acc = acc + q_ref[i].astype(F32)
        o_ref[...] = acc

    return pl.pallas_call(
        body, name=name, grid=(hr // rb,),
        in_specs=[pl.BlockSpec((N_CHIPS, rb, cols), lambda i: (0, i, 0))],
        out_specs=pl.BlockSpec((rb, cols), lambda i: (i, 0)),
        out_shape=jax.ShapeDtypeStruct((hr, cols), F32), compiler_params=_params(("parallel",)),
    )(qb)


def _place():
    x, y, c = lax.axis_index("x"), lax.axis_index("y"), lax.axis_index("c")
    return x, y, c, [(1 - x, y), (x, 1 - y), (1 - x, 1 - y)]


ANY = pl.BlockSpec(memory_space=pl.ANY)


def _gather_side(shards):
    nw, n_k = len(shards), 8

    def plan(ins, outs, ssem, rsem):
        x, y, c, _ = _place()
        nbr_x, nbr_y, sib = (1 - x, y, c), (x, 1 - y, c), (x, y, 1 - c)
        s_me, s_x, s_y, s_d = 2 * x + y, 2 * (1 - x) + y, 2 * x + 1 - y, 2 * (1 - x) + 1 - y

        def rows(w, half, quarter=None):
            hr = shards[w].shape[0] // 2
            if quarter is None:
                return pl.ds(pl.multiple_of(half * hr, 16), hr)
            return pl.ds(pl.multiple_of(half * hr + quarter * (hr // 2), 16), hr // 2)

        def rcopy(w, k, src, slot, rws, to):
            return pltpu.make_async_remote_copy(
                src_ref=src, dst_ref=outs[w].at[slot, rws], send_sem=ssem.at[w * n_k + k],
                recv_sem=rsem.at[w * n_k + k], device_id=to, device_id_type=MESH)

        out = []
        for w in range(nw):
            mine = ins[w].at[rows(w, c)]
            q = [rows(w, c, 0), rows(w, c, 1)]
            p = functools.partial
            out.append([
                (p(rcopy, w, 0, mine, s_me, rows(w, c), nbr_x), p(rcopy, w, 0, mine, s_x, rows(w, c), nbr_x)),
                (p(rcopy, w, 1, mine, s_me, rows(w, c), nbr_y), p(rcopy, w, 1, mine, s_y, rows(w, c), nbr_y)),
                (p(rcopy, w, 2, outs[w].at[s_x, q[0]], s_x, q[0], nbr_y),
                 p(rcopy, w, 2, outs[w].at[s_x, q[0]], s_d, q[0], nbr_y)),
                (p(rcopy, w, 3, outs[w].at[s_y, q[1]], s_y, q[1], nbr_x),
                 p(rcopy, w, 3, outs[w].at[s_y, q[1]], s_d, q[1], nbr_x)),
                (p(rcopy, w, 4, outs[w].at[s_x, rows(w, c)], s_x, rows(w, c), sib),
                 p(rcopy, w, 4, mine, s_x, rows(w, 1 - c), sib)),
                (p(rcopy, w, 5, outs[w].at[s_y, rows(w, c)], s_y, rows(w, c), sib),
                 p(rcopy, w, 5, mine, s_y, rows(w, 1 - c), sib)),
                (p(rcopy, w, 6, outs[w].at[s_d, q[0]], s_d, q[0], sib),
                 p(rcopy, w, 6, outs[w].at[s_d, q[0]], s_d, rows(w, 1 - c, 0), sib)),
                (p(rcopy, w, 7, outs[w].at[s_d, q[1]], s_d, q[1], sib),
                 p(rcopy, w, 7, outs[w].at[s_d, q[1]], s_d, rows(w, 1 - c, 1), sib)),
            ])
        return out

    def send_own(*refs):
        for cps in plan(*refs):
            cps[0][0]().start()
            cps[1][0]().start()

    def relay(*refs):
        for cps in plan(*refs):
            cps[0][1]().wait_recv()
            cps[2][0]().start()
            cps[4][0]().start()
            cps[1][1]().wait_recv()
            cps[3][0]().start()
            cps[5][0]().start()

    def pass_diagonal(*refs):
        for cps in plan(*refs):
            cps[2][1]().wait_recv()
            cps[6][0]().start()
            cps[3][1]().wait_recv()
            cps[7][0]().start()

    def finish(*refs):
        for cps in plan(*refs):
            for k in (4, 5, 6, 7):
                cps[k][1]().wait_recv()
            for k in range(n_k):
                cps[k][0]().wait_send()

    return _Side(shards, [jax.ShapeDtypeStruct((N_CHIPS,) + s.shape, s.dtype) for s in shards], nw * n_k,
                 [send_own, relay, pass_diagonal, finish])


def _run_side(side, *, name):
    n_in, n_out = len(side.ins), len(side.out_shapes)

    def body(*refs):
        ins, outs = refs[:n_in], refs[n_in:n_in + n_out]
        for phase in side.phases:
            phase(ins, outs, refs[-2], refs[-1])

    return pl.pallas_call(
        body, name=name, in_specs=[ANY] * n_in, out_specs=[ANY] * n_out, out_shape=side.out_shapes,
        scratch_shapes=side.sems(), compiler_params=pltpu.CompilerParams(has_side_effects=True),
    )(*side.ins)


def _exchange_side(gs):
    nw = len(gs)

    def copies(ins, outs, ssem, rsem):
        x, y, c, _ = _place()
        return [pltpu.make_async_remote_copy(
            src_ref=ins[w].at[:, 1 - c], dst_ref=outs[w], send_sem=ssem.at[w], recv_sem=rsem.at[w],
            device_id=(x, y, 1 - c), device_id_type=MESH) for w in range(nw)]

    def start(*refs):
        for cp in copies(*refs):
            cp.start()

    def finish(*refs):
        for cp in copies(*refs):
            cp.wait()

    return _Side(gs, [jax.ShapeDtypeStruct((g.shape[0],) + g.shape[2:], g.dtype) for g in gs], nw, [start, finish])


def _scatter_side(ps):
    nw = len(ps)

    def copies(ins, outs, ssem, rsem):
        x, y, c, chips = _place()
        s_me = 2 * x + y
        return [pltpu.make_async_remote_copy(
            src_ref=ins[w].at[2 * chip[0] + chip[1]], dst_ref=outs[w].at[s_me],
            send_sem=ssem.at[w * 3 + j], recv_sem=rsem.at[w * 3 + j],
            device_id=(chip[0], chip[1], c), device_id_type=MESH)
            for w in range(nw) for j, chip in enumerate(chips)]

    def start(*refs):
        for cp in copies(*refs):
            cp.start()

    def finish(*refs):
        for cp in copies(*refs):
            cp.wait()

    return _Side(ps, [jax.ShapeDtypeStruct(p.shape, p.dtype) for p in ps], nw * 3, [start, finish])


def _pair_share(hs, *, name):
    nw = len(hs)

    def body(*refs):
        ins, outs = refs[:nw], refs[nw:2 * nw]
        ssem, rsem = refs[2 * nw:]
        x, y, c, _ = _place()
        cps = []
        for w in range(nw):
            cp = pltpu.make_async_remote_copy(
                src_ref=ins[w], dst_ref=outs[w], send_sem=ssem.at[w], recv_sem=rsem.at[w],
                device_id=(x, y, 1 - c), device_id_type=MESH)
            cp.start()
            cps.append(cp)
        for cp in cps:
            cp.wait()

    return pl.pallas_call(
        body, name=name, in_specs=[ANY] * nw, out_specs=[ANY] * nw,
        out_shape=[jax.ShapeDtypeStruct(h.shape, h.dtype) for h in hs],
        scratch_shapes=[pltpu.SemaphoreType.DMA((nw,)), pltpu.SemaphoreType.DMA((nw,))],
        compiler_params=pltpu.CompilerParams(has_side_effects=True),
    )(*hs)


def _gather_small(a, *, name):
    def body(a_ref, o_ref, ssem, rsem):
        x, y, c, chips = _place()
        s_me = 2 * x + y
        o_ref[s_me] = a_ref[...]
        cps = []
        for j, chip in enumerate(chips):
            cp = pltpu.make_async_remote_copy(
                src_ref=a_ref, dst_ref=o_ref.at[s_me], send_sem=ssem.at[j], recv_sem=rsem.at[j],
                device_id=(chip[0], chip[1], c), device_id_type=MESH)
            cp.start()
            cps.append(cp)
        for cp in cps:
            cp.wait()

    vm = pl.BlockSpec(memory_space=pltpu.VMEM)
    return pl.pallas_call(
        body, name=name, in_specs=[vm], out_specs=vm,
        out_shape=jax.ShapeDtypeStruct((N_CHIPS,) + a.shape, a.dtype),
        scratch_shapes=[pltpu.SemaphoreType.DMA((3,)), pltpu.SemaphoreType.DMA((3,))],
    )(a)


def _allreduce_small(p, *, name):
    def body(p_ref, o_ref, buf, ssem, rsem):
        x, y, c, _ = _place()
        me = 4 * x + 2 * y + c
        buf[me] = p_ref[...]
        cps = []
        for k in range(1, N_DEV):
            fx, fy, fc = (k >> 2) & 1, (k >> 1) & 1, k & 1
            peer = (x + fx - 2 * x * fx, y + fy - 2 * y * fy, c + fc - 2 * c * fc)
            cp = pltpu.make_async_remote_copy(
                src_ref=buf.at[me], dst_ref=buf.at[me], send_sem=ssem.at[k - 1], recv_sem=rsem.at[k - 1],
                device_id=peer, device_id_type=MESH)
            cp.start()
            cps.append(cp)
        for cp in cps:
            cp.wait()
        acc = buf[0]
        for i in range(1, N_DEV):
            acc = acc + buf[i]
        o_ref[...] = acc

    vm = pl.BlockSpec(memory_space=pltpu.VMEM)
    return pl.pallas_call(
        body, name=name, in_specs=[vm], out_specs=vm,
        out_shape=jax.ShapeDtypeStruct(p.shape, p.dtype),
        scratch_shapes=[pltpu.VMEM((N_DEV,) + p.shape, p.dtype), pltpu.SemaphoreType.DMA((N_DEV - 1,)),
                        pltpu.SemaphoreType.DMA((N_DEV - 1,))],
    )(p)


def _rows_to_tokens(r):
    h, n = r.shape[0], r.shape[1]
    return r.reshape(h, n * CHUNK).T


def _tokens_to_rows(a):
    t, h = a.shape
    return a.T.reshape(h, t // CHUNK, 1, CHUNK)


def kernel(x, norm1_w, w_in, conv_w, a_log, dt_bias, gdn_norm_w, ret_norm_w, w_out, norm2_w, w_gate, w_up, w_down, norm_f_w, loss_target, m_norm1_w, m_w_in, m_conv_w, m_a_log, m_dt_bias, m_gdn_norm_w, m_ret_norm_w, m_w_out, m_norm2_w, m_w_gate, m_w_up, m_w_down, m_norm_f_w, v_norm1_w, v_w_in, v_conv_w, v_a_log, v_dt_bias, v_gdn_norm_w, v_ret_norm_w, v_w_out, v_norm2_w, v_w_gate, v_w_up, v_w_down, v_norm_f_w):
    t, d = x.shape[1], x.shape[2]
    f = w_gate.shape[2] * N_CHIPS
    nqk, nv = HEADS * DK, HEADS * DV
    ncs = w_in.shape[2]
    c_idx = lax.axis_index("c")
    s_idx = 2 * lax.axis_index("x") + lax.axis_index("y")
    xs = x[0]
    tgt = loss_target[0]

    n_small = 2 * HEADS
    widths = [2 * nqk + nv, nv, n_small, nqk, nqk, nv, nv, d, d]
    g_off = np.concatenate([[0], np.cumsum(widths)])
    order = [0, 3, 4, 5, 1, 6, 7, 8]
    m_off = np.concatenate([[0], np.cumsum([widths[i] for i in order])])
    o_rq, o_rk, o_rv, o_az, o_rg, o_ga, o_gb = [int(m_off[i]) for i in range(1, 8)]
    segs = [(int(g_off[i]), int(g_off[i + 1]), int(m_off[order.index(i)]) if i != 2 else None) for i in range(9)]

    def shard_pieces(s):
        out = []
        for a, b, dst in segs:
            lo, hi = max(a, s * ncs), min(b, (s + 1) * ncs)
            if lo < hi:
                out.append((lo - s * ncs, hi - s * ncs, None if dst is None else dst + lo - a))
        return out

    own = [w_in[0].astype(BF16), w_out[0].astype(BF16), w_gate[0].astype(BF16), w_up[0].astype(BF16),
           w_down[0].astype(BF16)]
    put_own = lambda full, mine: lax.dynamic_update_slice(full, mine[None], (s_idx, 0, 0))
    wg_in = put_own(_run_side(_gather_side(own[:1]), name="gather_w_in")[0], own[0])
    conv_full = _gather_small(conv_w[0], name="gather_conv_w")
    conv_full = jnp.concatenate([conv_full[i] for i in range(N_CHIPS)], axis=1)
    cuts = [(dst, s, a, b) for s in range(N_CHIPS) for a, b, dst in shard_pieces(s)]
    w_main = jnp.concatenate([wg_in[s][:, a:b] for dst, s, a, b in sorted(c for c in cuts if c[0] is not None)],
                             axis=1)
    w_small = jnp.concatenate([wg_in[s][:, a:b] for dst, s, a, b in cuts if dst is None], axis=1)
    w_small = jnp.pad(w_small, ((0, 0), (0, LANES - n_small)))

    pad16 = lambda a: jnp.pad(a, ((0, 0), (HEADS, LANES - 2 * HEADS)))
    alog_row, dtb_row = pad16(a_log), pad16(dt_bias)
    wa_row = jnp.tile(gdn_norm_w, (1, HEADS))
    inv = ROPE_BASE ** (-jnp.arange(0, DK, 2, dtype=F32) / DK)
    ang = jnp.arange(t, dtype=F32)[:, None] * inv[None, :]
    cos2 = jnp.concatenate([jnp.cos(ang), jnp.cos(ang)], axis=1)
    sin2 = jnp.concatenate([-jnp.sin(ang), jnp.sin(ang)], axis=1)
    lg = jnp.log1p(-jnp.exp2(-5.0 - jnp.arange(HEADS, dtype=F32)))
    lg_tab = jnp.broadcast_to(lg[:, None, None], (HEADS, 1, LANES))

    fq = f // N_CHIPS
    u1, projs = _rms_fwd(xs, norm1_w, w_narrow=w_small, name="rms1_fwd")
    proj, *rest = _mm(u1, w_main, tm=1024, tn=2048, tk=d, side=_gather_side(own[1:]), name="mm_proj")
    wg_out, wg_gate, wg_up, wg_down = [put_own(g, o) for g, o in zip(rest, own[1:])]
    w_o = wg_out.reshape(d, d)
    w_g, w_u = wg_gate, wg_up
    w_d = wg_down.reshape(f, d)
    q_a, k_a, v_a = _conv_fwd(proj, conv_full, name="conv_fwd")
    bg = _bg_fwd(projs, alog_row, dtb_row, name="bg_fwd")
    beta_r = _tokens_to_rows(bg[:, :HEADS])
    g_r = _tokens_to_rows(bg[:, HEADS:2 * HEADS])
    gc_r, tinv, u_a, w_a, p_a = _gdn_prep(q_a, k_a, v_a, beta_r, g_r, name="gdn_prep")
    o_a, s_a, o_b, s_b = _mix_fwd(q_a, k_a, u_a, w_a, p_a, gc_r, proj, cos2, sin2, lg_tab, o_rq, o_rk, o_rv,
                                  name="mix_fwd")
    mixed = _merge_fwd(o_a, o_b, proj, wa_row, ret_norm_w, o_az, o_rg, o_ga, o_gb, name="merge_fwd")
    h1, hn = _mm_res_norm(mixed, w_o, xs, norm2_w, tm=512, name="mm_out")
    gt, up, act = _ffn_in(hn, w_g, w_u, tm=512, name="ffn_in")
    h2 = _mm(act, w_d, res=h1, tm=512, tn=1024, tk=f, name="mm_down")
    loss_row, dh2, dh2b, d_nf = _loss_head(h2, tgt, norm_f_w.reshape(1, d), name="loss_head")

    g_down = _mm(act, dh2b, ta=True, out_dtype=BF16, tm=fq, tn=d, tk=1024, name="mm_dw_down")
    dgt, dup = _ffn_back(dh2b, w_d, gt, up, tm=512, tn=fq, name="ffn_back")
    dhn = _mm(dgt, w_g, tb=True, pair=(dup, w_u), out_dtype=BF16, tm=512, tn=d, tk=fq, name="mm_dhn")
    g_gate = _mm(hn, dgt, ta=True, out_dtype=BF16, out_stacked=True, tm=1024, tn=fq, tk=2048, name="mm_dw_gate")
    g_up = _mm(hn, dup, ta=True, out_dtype=BF16, out_stacked=True, tm=1024, tn=fq, tk=2048, name="mm_dw_up")
    dh1, dh1b, d_n2 = _rms_bwd(dhn, h1, norm2_w, dh2, bf16_copy=True, name="rms2_bwd")
    dmix = _mm(dh1b, w_o, tb=True, tm=512, tn=d, tk=d, name="mm_dmix")
    g_out = _mm(mixed, dh1b, ta=True, out_dtype=BF16, tm=1024, tn=d, tk=1024, name="mm_dw_out")
    halves = lambda g: g.reshape(N_CHIPS, 2, g.shape[1] // 2, g.shape[2])
    c_arr = jnp.reshape(c_idx, (1,)).astype(jnp.int32)
    gs_ffn = [halves(g_out.reshape(N_CHIPS, d // N_CHIPS, d)), halves(g_gate), halves(g_up),
              halves(g_down.reshape(N_CHIPS, fq, d))]
    do_a, do_b, dproj, d_wa, d_wb, *rsib = _merge_bwd(
        dmix, o_a, o_b, proj, wa_row, ret_norm_w, o_az, o_rg, o_ga, o_gb, side=_exchange_side(gs_ffn),
        name="merge_bwd")
    ps_ffn = [_pair_add(g, r, c_arr, name=f"grad_pair_add_{nm}")
              for g, r, nm in zip(gs_ffn, rsib, ["w_out", "w_gate", "w_up", "w_down"])]
    dq_a, dk_a, dv_a, dbeta_r, dg_r, dproj = _mix_bwd(
        q_a, k_a, v_a, beta_r, gc_r, tinv, u_a, w_a, s_a, do_a, proj, cos2, sin2, lg_tab, s_b, do_b, dproj,
        o_rq, o_rk, o_rv, name="mix_bwd")
    dc = _conv_bwd_pre(proj, conv_full, dq_a, dk_a, dv_a, name="conv_bwd_pre")
    dproj, d_cw = _conv_bwd(proj, dc, conv_full, dproj, name="conv_bwd")
    dbg = jnp.pad(jnp.concatenate([_rows_to_tokens(dbeta_r), _rows_to_tokens(dg_r)], axis=1),
                  ((0, 0), (0, LANES - 2 * HEADS)))
    dprojs, d_alog, d_dtb = _bg_bwd(projs, dbg, alog_row, dtb_row, name="bg_bwd")
    g_main, *qs_ffn = _mm(u1, dproj, ta=True, out_dtype=BF16, tm=1024, tn=2048, tk=2048,
                          side=_scatter_side(ps_ffn), name="mm_dw_in")
    g_small = _mm(u1, dprojs, ta=True, out_dtype=BF16, tm=1024, tn=LANES, tk=1024, name="mm_dw_in_small")
    pieces = []
    for s in range(N_CHIPS):
        seen, parts = 0, []
        for a, b, dst in shard_pieces(s):
            parts.append(g_small[:, seen:seen + b - a] if dst is None else g_main[:, dst:dst + b - a])
            seen += (b - a) if dst is None else 0
        pieces.append(jnp.concatenate(parts, axis=1))
    gs_in = [halves(jnp.stack(pieces))]
    rsib = _run_side(_exchange_side(gs_in), name="grad_pair_exchange_in")
    ps_in = [_pair_add(gs_in[0], rsib[0], c_arr, name="grad_pair_add_w_in")]
    du, *qs_in = _mm(dproj, w_main, tb=True, out_dtype=BF16, tm=1024, tn=d, tk=2048,
                     side=_scatter_side(ps_in), name="mm_du")
    dx, d_n1 = _rms_bwd(du, xs, norm1_w, dh1, bf16_copy=False, narrow=(dprojs, w_small), name="rms1_bwd")

    names = ["w_in", "w_out", "w_gate", "w_up", "w_down"]
    qs = [lax.dynamic_update_slice(q, lax.dynamic_slice(p, (s_idx, 0, 0), (1,) + p.shape[1:]), (s_idx, 0, 0))
          for q, p in zip(qs_in + qs_ffn, ps_in + ps_ffn)]
    hs = [_chip_sum(q, name=f"grad_chip_sum_{nm}") for q, nm in zip(qs, names)]
    theirs = _pair_share(hs, name="grad_pair_share")
    big_w = [w_in[0], w_out[0], w_gate[0], w_up[0], w_down[0]]
    big_m = [m_w_in[0], m_w_out[0], m_w_gate[0], m_w_up[0], m_w_down[0]]
    big_v = [v_w_in[0], v_w_out[0], v_w_gate[0], v_w_up[0], v_w_down[0]]
    big = {}
    for nm, mine, other, w_, m_, v_ in zip(names, hs, theirs, big_w, big_m, big_v):
        big[nm] = tuple(a[None] for a in _adamw_halves(w_, mine, other, c_arr, m_, v_, name=f"adamw_{nm}"))

    d_wa_h = jnp.sum(d_wa.reshape(HEADS, DV), axis=0, keepdims=True)
    small = [d_n1, d_alog[:, HEADS:2 * HEADS], d_dtb[:, HEADS:2 * HEADS], d_wa_h, d_wb, d_n2, d_nf,
             d_cw[:CONV_W].reshape(1, -1)]
    sizes = [a.shape[1] for a in small]
    packed = jnp.concatenate(small, axis=1)
    n_pack = packed.shape[1]
    n_rows = -(-n_pack // LANES)
    n_rows = -(-n_rows // 8) * 8
    packed = jnp.pad(packed, ((0, 0), (0, n_rows * LANES - n_pack))).reshape(n_rows, LANES)
    red = _allreduce_small(packed, name="allreduce_small").reshape(1, -1)
    offs = np.cumsum([0] + sizes)
    g_n1, g_alog, g_dtb, g_wa, g_wb, g_n2, g_nf, g_cw = [red[:, offs[i]:offs[i + 1]] for i in range(len(sizes))]
    ncw = conv_w.shape[2]
    g_cw = lax.dynamic_slice(g_cw.reshape(CONV_W, -1), (0, s_idx * ncw), (CONV_W, ncw))

    def small_update(w_, g, m_, v_, nm):
        shape = w_.shape
        pad = (-w_.size) % LANES
        to2 = lambda a: jnp.pad(a.reshape(1, -1), ((0, 0), (0, pad)))
        outs = _adamw(to2(w_), to2(g), to2(m_), to2(v_), name=f"adamw_{nm}")
        return (g.reshape(shape),) + tuple(a[:, :w_.size].reshape(shape) for a in outs)

    res = {
        "norm1_w": small_update(norm1_w, g_n1, m_norm1_w, v_norm1_w, "norm1_w"),
        "w_in": big["w_in"],
        "conv_w": small_update(conv_w, g_cw, m_conv_w, v_conv_w, "conv_w"),
        "a_log": small_update(a_log, g_alog, m_a_log, v_a_log, "a_log"),
        "dt_bias": small_update(dt_bias, g_dtb, m_dt_bias, v_dt_bias, "dt_bias"),
        "gdn_norm_w": small_update(gdn_norm_w, g_wa, m_gdn_norm_w, v_gdn_norm_w, "gdn_norm_w"),
        "ret_norm_w": small_update(ret_norm_w, g_wb, m_ret_norm_w, v_ret_norm_w, "ret_norm_w"),
        "w_out": big["w_out"],
        "norm2_w": small_update(norm2_w, g_n2, m_norm2_w, v_norm2_w, "norm2_w"),
        "w_gate": big["w_gate"],
        "w_up": big["w_up"],
        "w_down": big["w_down"],
        "norm_f_w": small_update(norm_f_w, g_nf, m_norm_f_w, v_norm_f_w, "norm_f_w"),
    }
    order = ["norm1_w", "w_in", "conv_w", "a_log", "dt_bias", "gdn_norm_w", "ret_norm_w", "w_out", "norm2_w",
             "w_gate", "w_up", "w_down", "norm_f_w"]
    loss = lax.psum(loss_row[0, 0], ("x", "y", "c"))
    return (loss, dx[None], *[res[n][0] for n in order], *[res[n][1] for n in order],
            *[res[n][2] for n in order], *[res[n][3] for n in order])
```

```python
import functools

import jax
import jax.numpy as jnp
import numpy as np
from jax import lax
from jax.experimental import pallas as pl
from jax.experimental.pallas import tpu as pltpu

F32 = jnp.float32
BF16 = jnp.bfloat16
MESH = pl.DeviceIdType.MESH

HEADS = 8
DK = 128
DV = 256
CHUNK = 64
CONV_W = 4
EPS = 1e-6
ROPE_BASE = 10000.0
ADAM_LR, ADAM_B1, ADAM_B2, ADAM_EPS, ADAM_WD, ADAM_STEP = 0.001, 0.9, 0.999, 1e-08, 0.01, 10
N_CHIPS = 4
N_DEV = 8
LANES = 128
HALO = 8
VMEM_LIMIT = 56 * 1024 * 1024
HB = 8


def _pick(n, cands):
    for c in cands:
        if n % c == 0:
            return c
    raise ValueError(f"no tile for {n} in {cands}")


def _params(sem=None):
    return pltpu.CompilerParams(dimension_semantics=sem, vmem_limit_bytes=VMEM_LIMIT)


def _dot(a, b):
    return jnp.dot(a.astype(BF16), b.astype(BF16), preferred_element_type=F32)


def _dot_nt(a, b):
    return lax.dot_general(a.astype(BF16), b.astype(BF16), (((1,), (1,)), ((), ())), preferred_element_type=F32)


def _dot_tn(a, b):
    return lax.dot_general(a.astype(BF16), b.astype(BF16), (((0,), (0,)), ((), ())), preferred_element_type=F32)


def _sigmoid(x):
    return 1.0 / (1.0 + jnp.exp(-x))


def _iota2(n):
    return lax.broadcasted_iota(jnp.int32, (n, n), 0), lax.broadcasted_iota(jnp.int32, (n, n), 1)


def _row_to_col(row):
    n = row.shape[1]
    r, c = _iota2(n)
    return jnp.sum(jnp.where(r == c, jnp.broadcast_to(row, (n, n)), 0.0), axis=1, keepdims=True)


def _col_to_row(col):
    n = col.shape[0]
    r, c = _iota2(n)
    return jnp.sum(jnp.where(r == c, jnp.broadcast_to(col, (n, n)), 0.0), axis=0, keepdims=True)


class _Side:
    def __init__(self, ins, out_shapes, n_sem, phases):
        self.ins, self.out_shapes, self.n_sem, self.phases = list(ins), list(out_shapes), n_sem, phases

    def sems(self):
        return [pltpu.SemaphoreType.DMA((self.n_sem,)), pltpu.SemaphoreType.DMA((self.n_sem,))]


def _mm(a, b, *, name, tm, tn, tk, ta=False, tb=False, out_dtype=F32, res=None, side=None, pair=None,
        out_stacked=False):
    m, k = (a.shape[1], a.shape[0]) if ta else a.shape
    b_slots = b.ndim == 3
    if b_slots:
        assert tb and tk == b.shape[2]
        n = b.shape[1]
    else:
        n = b.shape[0] if tb else b.shape[1]
    tm, tn, tk = min(tm, m), min(tn, n), min(tk, k)
    assert m % tm == 0 and n % tn == 0 and k % tk == 0, (name, m, n, k)
    nk = k // tk
    nj, ni = n // tn, m // tm
    dn = (((0 if ta else 1,), (1 if tb else 0,)), ((), ()))
    n_ab = 4 if pair else 2
    n_in = n_ab + (res is not None)
    n_side_in = len(side.ins) if side else 0
    n_side_out = len(side.out_shapes) if side else 0

    def body(*refs):
        a_ref, b_ref = refs[0], refs[1]
        r_ref = refs[n_ab] if res is not None else None
        o_ref = refs[n_in + n_side_in]
        if side:
            s_in = refs[n_in:n_in + n_side_in]
            s_out = refs[n_in + n_side_in + 1:n_in + n_side_in + 1 + n_side_out]
            ssem, rsem = refs[-2], refs[-1]
            j_, i_, k_ = pl.program_id(0), pl.program_id(1), pl.program_id(2)
            n_mid = len(side.phases) - 2
            assert n_mid == 0 or nj >= 2 * n_mid
            when = [(j_ == 0) & (i_ == 0) & (k_ == 0)]
            when += [(j_ == nj // 2 + (p * (nj // 2)) // n_mid) & (i_ == 0) & (k_ == 0) for p in range(n_mid)]
            when.append((j_ == nj - 1) & (i_ == ni - 1) & (k_ == nk - 1))

        def run_phase(p):
            @pl.when(when[p])
            def _():
                side.phases[p](s_in, s_out, ssem, rsem)

        if side:
            for p in range(len(side.phases) - 1):
                run_phase(p)

        def finish(r):
            if res is not None:
                r = r + r_ref[...]
            o_ref[...] = r.astype(out_dtype)

        part = lax.dot_general(a_ref[...], b_ref[...], dn, preferred_element_type=F32)
        if pair:
            part = part + lax.dot_general(refs[2][...], refs[3][...], dn, preferred_element_type=F32)
        if nk == 1:
            finish(part)
        else:
            acc = refs[n_in + n_side_in + 1 + n_side_out]
            kk = pl.program_id(2)

            @pl.when(kk == 0)
            def _():
                acc[...] = part

            @pl.when((kk > 0) & (kk < nk - 1))
            def _():
                acc[...] += part

            @pl.when(kk == nk - 1)
            def _():
                finish(acc[...] + part)

        if side:
            run_phase(len(side.phases) - 1)

    a_spec = pl.BlockSpec((tk, tm), lambda j, i, kk: (kk, i)) if ta else pl.BlockSpec((tm, tk), lambda j, i, kk: (i, kk))
    b_spec = pl.BlockSpec((tn, tk), lambda j, i, kk: (j, kk)) if tb else pl.BlockSpec((tk, tn), lambda j, i, kk: (kk, j))
    if b_slots:
        b_spec = pl.BlockSpec((None, tn, tk), lambda j, i, kk: (kk, j, 0))
    o_spec = pl.BlockSpec((tm, tn), lambda j, i, kk: (i, j))
    o_shape = jax.ShapeDtypeStruct((m, n), out_dtype)
    if out_stacked:
        assert res is None
        o_spec = pl.BlockSpec((None, tm, tn), lambda j, i, kk: (j, i, 0))
        o_shape = jax.ShapeDtypeStruct((nj, m, tn), out_dtype)
    in_specs, args = [a_spec, b_spec], [a, b]
    if pair:
        assert pair[0].shape == a.shape and pair[1].shape == b.shape
        in_specs += [a_spec, b_spec]
        args += list(pair)
    if res is not None:
        in_specs.append(o_spec)
        args.append(res)
    out_specs, out_shape = o_spec, o_shape
    scratch = [pltpu.VMEM((tm, tn), F32)] if nk > 1 else []
    sem = ("parallel", "parallel", "arbitrary")
    if side:
        hbm = pl.BlockSpec(memory_space=pl.ANY)
        in_specs += [hbm] * n_side_in
        args += side.ins
        out_specs, out_shape = [o_spec] + [hbm] * n_side_out, [out_shape] + side.out_shapes
        scratch += side.sems()
        sem = ("arbitrary",) * 3
    return pl.pallas_call(
        body, name=name, grid=(nj, ni, nk), in_specs=in_specs, out_specs=out_specs, out_shape=out_shape,
        scratch_shapes=scratch, compiler_params=_params(sem),
    )(*args)


def _mm_res_norm(a, b, res, w, *, name, tm):
    m, k = a.shape
    n = b.shape[1]
    tm = min(tm, m)

    def body(a_ref, b_ref, r_ref, w_ref, h_ref, hn_ref):
        h = jnp.dot(a_ref[...], b_ref[...], preferred_element_type=F32) + r_ref[...]
        h_ref[...] = h
        r = lax.rsqrt(jnp.mean(h * h, axis=-1, keepdims=True) + EPS)
        hn_ref[...] = (h * r * w_ref[...]).astype(BF16)

    row = pl.BlockSpec((tm, n), lambda i: (i, 0))
    return pl.pallas_call(
        body, name=name, grid=(m // tm,),
        in_specs=[pl.BlockSpec((tm, k), lambda i: (i, 0)), pl.BlockSpec((k, n), lambda i: (0, 0)), row,
                  pl.BlockSpec((1, n), lambda i: (0, 0))],
        out_specs=[row, row],
        out_shape=[jax.ShapeDtypeStruct((m, n), F32), jax.ShapeDtypeStruct((m, n), BF16)],
        compiler_params=_params(("parallel",)),
    )(a, b, res, w)


def _rms_fwd(x, w, *, name, w_narrow=None):
    t, d = x.shape
    tt = _pick(t, (512, 256))

    def body(*refs):
        x_ref, w_ref, o_ref = refs[0], refs[1], refs[-2 if w_narrow is not None else -1]
        xv = x_ref[...]
        r = lax.rsqrt(jnp.mean(xv * xv, axis=-1, keepdims=True) + EPS)
        u = (xv * r * w_ref[...]).astype(BF16)
        o_ref[...] = u
        if w_narrow is not None:
            refs[-1][...] = jnp.dot(u, refs[2][...], preferred_element_type=F32)

    blk = pl.BlockSpec((tt, d), lambda i: (i, 0))
    in_specs, args = [blk, pl.BlockSpec((1, d), lambda i: (0, 0))], [x, w]
    out_specs, out_shape = [blk], [jax.ShapeDtypeStruct((t, d), BF16)]
    if w_narrow is not None:
        in_specs.append(pl.BlockSpec(w_narrow.shape, lambda i: (0, 0)))
        args.append(w_narrow)
        out_specs.append(pl.BlockSpec((tt, LANES), lambda i: (i, 0)))
        out_shape.append(jax.ShapeDtypeStruct((t, LANES), F32))
    out = pl.pallas_call(
        body, name=name, grid=(t // tt,), in_specs=in_specs, out_specs=out_specs, out_shape=out_shape,
        compiler_params=_params(("parallel",)),
    )(*args)
    return out if w_narrow is not None else out[0]


def _rms_bwd(dn, x, w, dres, *, name, bf16_copy, narrow=None):
    t, d = x.shape
    tt = _pick(t, (256,))

    def body(*refs):
        dn_ref, x_ref, w_ref, dres_ref = refs[:4]
        dx_ref, dw_ref = refs[-3 if bf16_copy else -2], refs[-1]
        xv, g = x_ref[...], dn_ref[...].astype(F32)
        if narrow is not None:
            g = g + lax.dot_general(refs[4][...], refs[5][...], (((1,), (1,)), ((), ())), preferred_element_type=F32)
        r = lax.rsqrt(jnp.mean(xv * xv, axis=-1, keepdims=True) + EPS)
        xh = xv * r
        gw = g * w_ref[...]
        dx = dres_ref[...] + r * (gw - xh * jnp.mean(gw * xh, axis=-1, keepdims=True))
        dx_ref[...] = dx
        if bf16_copy:
            refs[-2][...] = dx.astype(BF16)

        @pl.when(pl.program_id(0) == 0)
        def _():
            dw_ref[...] = jnp.zeros_like(dw_ref)

        dw_ref[...] += jnp.sum(g * xh, axis=0, keepdims=True)

    blk = pl.BlockSpec((tt, d), lambda i: (i, 0))
    row = pl.BlockSpec((1, d), lambda i: (0, 0))
    in_specs, args = [blk, blk, row, blk], [dn, x, w, dres]
    if narrow is not None:
        in_specs += [pl.BlockSpec((tt, LANES), lambda i: (i, 0)), pl.BlockSpec(narrow[1].shape, lambda i: (0, 0))]
        args += list(narrow)
    copy_spec, copy_shape = ([blk], [jax.ShapeDtypeStruct((t, d), BF16)]) if bf16_copy else ([], [])
    return pl.pallas_call(
        body, name=name, grid=(t // tt,), in_specs=in_specs, out_specs=[blk] + copy_spec + [row],
        out_shape=[jax.ShapeDtypeStruct((t, d), F32)] + copy_shape + [jax.ShapeDtypeStruct((1, d), F32)],
        compiler_params=_params(("arbitrary",)),
    )(*args)


def _loss_head(h2, tgt, wf, *, name):
    t, d = h2.shape
    tt = _pick(t, (256,))

    def body(x_ref, t_ref, w_ref, loss_ref, dx_ref, dxb_ref, dw_ref):
        xv = x_ref[...]
        r = lax.rsqrt(jnp.mean(xv * xv, axis=-1, keepdims=True) + EPS)
        xh = xv * r
        err = xh * w_ref[...] - t_ref[...]
        lpart = 0.5 * jnp.sum(jnp.mean(err * err, axis=-1, keepdims=True), axis=0, keepdims=True)
        dy = err * (1.0 / d)
        gw = dy * w_ref[...]
        dx = r * (gw - xh * jnp.mean(gw * xh, axis=-1, keepdims=True))
        dx_ref[...] = dx
        dxb_ref[...] = dx.astype(BF16)

        @pl.when(pl.program_id(0) == 0)
        def _():
            dw_ref[...] = jnp.zeros_like(dw_ref)
            loss_ref[...] = jnp.zeros_like(loss_ref)

        dw_ref[...] += jnp.sum(dy * xh, axis=0, keepdims=True)
        loss_ref[...] += jnp.broadcast_to(lpart, loss_ref.shape)

    blk = pl.BlockSpec((tt, d), lambda i: (i, 0))
    row = pl.BlockSpec((1, d), lambda i: (0, 0))
    lrow = pl.BlockSpec((1, LANES), lambda i: (0, 0))
    return pl.pallas_call(
        body, name=name, grid=(t // tt,),
        in_specs=[blk, blk, row], out_specs=[lrow, blk, blk, row],
        out_shape=[jax.ShapeDtypeStruct((1, LANES), F32), jax.ShapeDtypeStruct((t, d), F32),
                   jax.ShapeDtypeStruct((t, d), BF16), jax.ShapeDtypeStruct((1, d), F32)],
        compiler_params=_params(("arbitrary",)),
    )(h2, tgt, wf)


def _ffn_in(hn, w_g, w_u, *, name, tm):
    t, d = hn.shape
    tn = w_g.shape[2]
    f = w_g.shape[0] * tn
    tm = min(tm, t)

    def body(a_ref, g_ref, u_ref, gt_ref, up_ref, act_ref):
        a = a_ref[...]
        g = jnp.dot(a, g_ref[...], preferred_element_type=F32)
        u = jnp.dot(a, u_ref[...], preferred_element_type=F32)
        gt_ref[...] = g.astype(BF16)
        up_ref[...] = u.astype(BF16)
        act_ref[...] = (g * _sigmoid(g) * u).astype(BF16)

    wblk = pl.BlockSpec((None, d, tn), lambda j, i: (j, 0, 0))
    oblk = pl.BlockSpec((tm, tn), lambda j, i: (i, j))
    return pl.pallas_call(
        body, name=name, grid=(f // tn, t // tm),
        in_specs=[pl.BlockSpec((tm, d), lambda j, i: (i, 0)), wblk, wblk], out_specs=[oblk] * 3,
        out_shape=[jax.ShapeDtypeStruct((t, f), BF16)] * 3,
        compiler_params=_params(("parallel", "parallel")),
    )(hn, w_g, w_u)


def _ffn_back(dh2b, w_d, gt, up, *, name, tm, tn):
    t, d = dh2b.shape
    f = w_d.shape[0]
    tm = min(tm, t)

    n_sub = 2 if tm % 32 == 0 else 1

    def body(a_ref, w_ref, g_ref, u_ref, dg_ref, du_ref):
        w = w_ref[...]
        subs = [slice(i * (tm // n_sub), (i + 1) * (tm // n_sub)) for i in range(n_sub)]
        das = [lax.dot_general(a_ref[rs, :], w, (((1,), (1,)), ((), ())), preferred_element_type=F32) for rs in subs]
        for rs, da in zip(subs, das):
            g = g_ref[rs, :].astype(F32)
            sg = _sigmoid(g)
            dg_ref[rs, :] = (da * u_ref[rs, :].astype(F32) * sg * (1.0 + g * (1.0 - sg))).astype(BF16)
            du_ref[rs, :] = (da * g * sg).astype(BF16)

    oblk = pl.BlockSpec((tm, tn), lambda j, i: (i, j))
    return pl.pallas_call(
        body, name=name, grid=(f // tn, t // tm),
        in_specs=[pl.BlockSpec((tm, d), lambda j, i: (i, 0)), pl.BlockSpec((tn, d), lambda j, i: (j, 0)), oblk, oblk],
        out_specs=[oblk] * 2,
        out_shape=[jax.ShapeDtypeStruct((t, f), BF16)] * 2,
        compiler_params=_params(("parallel", "parallel")),
    )(dh2b, w_d, gt, up)


CONV_HEAD = 2 * HALO


def _causal_conv(x_ref, xp_ref, cw_ref, xs, tt):
    first = pl.program_id(0) == 0
    xs[0:HALO, :] = jnp.where(first, 0.0, xp_ref[...])
    xs[HALO:HALO + CONV_HEAD, :] = x_ref[0:CONV_HEAD, :]
    xb = x_ref[...].astype(BF16)
    r, c = _iota2(tt)
    moved = [None] + [jnp.dot((c == r - sh).astype(BF16), xb, preferred_element_type=F32)
                      for sh in range(1, CONV_W)]

    def cols(cb):
        cs = slice(cb * LANES, (cb + 1) * LANES)
        c_head = c_main = None
        for j in range(CONV_W):
            sh = CONV_W - 1 - j
            w_j = cw_ref[j:j + 1, cs]
            x_m = x_ref[CONV_HEAD:tt, cs] if sh == 0 else moved[sh][CONV_HEAD:tt, cs]
            x_h = xs[HALO - sh:HALO - sh + CONV_HEAD, cs]
            c_main = x_m * w_j if c_main is None else c_main + x_m * w_j
            c_head = x_h * w_j if c_head is None else c_head + x_h * w_j
        return jnp.concatenate([c_head, c_main], axis=0)

    return cols


def _conv_specs(tt, ch):
    cur = pl.BlockSpec((tt, ch), lambda i: (i, 0))
    prev = pl.BlockSpec((HALO, ch), lambda i: (jnp.maximum(i * (tt // HALO) - 1, 0), 0))
    return cur, prev


def _conv_fwd(proj, conv_w, *, name):
    t = proj.shape[0]
    ch = conv_w.shape[1]
    nqk = HEADS * DK
    tt = _pick(t, (256,))
    cur, prev = _conv_specs(tt, ch)

    def body(x_ref, xp_ref, cw_ref, q_ref, k_ref, v_ref, xs):
        conv = _causal_conv(x_ref, xp_ref, cw_ref, xs, tt)
        for cb in range(ch // LANES):
            c = conv(cb)
            s = c * _sigmoid(c)
            if cb < 2 * HEADS:
                s = s * lax.rsqrt(jnp.sum(s * s, axis=-1, keepdims=True) + EPS)
                if cb < HEADS:
                    q_ref[:, cb * LANES:(cb + 1) * LANES] = (s * (DK ** -0.5)).astype(BF16)
                else:
                    k_ref[:, (cb - HEADS) * LANES:(cb - HEADS + 1) * LANES] = s.astype(BF16)
            else:
                v_ref[:, (cb - 2 * HEADS) * LANES:(cb - 2 * HEADS + 1) * LANES] = s.astype(BF16)

    return pl.pallas_call(
        body, name=name, grid=(t // tt,),
        in_specs=[cur, prev, pl.BlockSpec((CONV_W, ch), lambda i: (0, 0))],
        out_specs=[pl.BlockSpec((tt, nqk), lambda i: (i, 0)), pl.BlockSpec((tt, nqk), lambda i: (i, 0)),
                   pl.BlockSpec((tt, ch - 2 * nqk), lambda i: (i, 0))],
        out_shape=[jax.ShapeDtypeStruct((t, nqk), BF16), jax.ShapeDtypeStruct((t, nqk), BF16),
                   jax.ShapeDtypeStruct((t, ch - 2 * nqk), BF16)],
        scratch_shapes=[pltpu.VMEM((HALO + CONV_HEAD, ch), F32)],
        compiler_params=_params(("arbitrary",)),
    )(proj, proj, conv_w)


def _conv_bwd_pre(proj, conv_w, dq, dk, dv, *, name):
    t = proj.shape[0]
    ch = conv_w.shape[1]
    nqk = HEADS * DK
    tt = _pick(t, (256,))
    cur, prev = _conv_specs(tt, ch)

    def body(x_ref, xp_ref, cw_ref, dq_ref, dk_ref, dv_ref, dc_ref, xs):
        conv = _causal_conv(x_ref, xp_ref, cw_ref, xs, tt)
        for cb in range(ch // LANES):
            c = conv(cb)
            sg = _sigmoid(c)
            s = c * sg
            if cb < 2 * HEADS:
                if cb < HEADS:
                    d, scale = dq_ref[:, cb * LANES:(cb + 1) * LANES], DK ** -0.5
                else:
                    d, scale = dk_ref[:, (cb - HEADS) * LANES:(cb - HEADS + 1) * LANES], 1.0
                rinv = lax.rsqrt(jnp.sum(s * s, axis=-1, keepdims=True) + EPS)
                ds = scale * rinv * (d - s * (rinv * rinv) * jnp.sum(d * s, axis=-1, keepdims=True))
            else:
                ds = dv_ref[:, (cb - 2 * HEADS) * LANES:(cb - 2 * HEADS + 1) * LANES]
            dc_ref[:, cb * LANES:(cb + 1) * LANES] = ds * sg * (1.0 + c * (1.0 - sg))

    return pl.pallas_call(
        body, name=name, grid=(t // tt,),
        in_specs=[cur, prev, pl.BlockSpec((CONV_W, ch), lambda i: (0, 0)),
                  pl.BlockSpec((tt, nqk), lambda i: (i, 0)), pl.BlockSpec((tt, nqk), lambda i: (i, 0)),
                  pl.BlockSpec((tt, ch - 2 * nqk), lambda i: (i, 0))],
        out_specs=pl.BlockSpec((tt, ch), lambda i: (i, 0)),
        out_shape=jax.ShapeDtypeStruct((t, ch), F32),
        scratch_shapes=[pltpu.VMEM((HALO + CONV_HEAD, ch), F32)],
        compiler_params=_params(("arbitrary",)),
    )(proj, proj, conv_w, dq, dk, dv)


def _conv_bwd(proj, dc, conv_w, dproj, *, name):
    t = proj.shape[0]
    ch = conv_w.shape[1]
    tt = _pick(t, (256,))
    nt = t // tt
    cur = pl.BlockSpec((tt, ch), lambda i: (i, 0))
    nxt = pl.BlockSpec((HALO, ch), lambda i: (jnp.minimum((i + 1) * (tt // HALO), t // HALO - 1), 0))

    tail = 2 * HALO
    main = tt - tail

    def body(x_ref, d_ref, dn_ref, cw_ref, alias_ref, dx_ref, dw_ref, ds):
        last = pl.program_id(0) == nt - 1
        ds[0:tail, :] = d_ref[main:tt, :]
        ds[tail:tail + HALO, :] = jnp.where(last, 0.0, dn_ref[...])

        @pl.when(pl.program_id(0) == 0)
        def _():
            dw_ref[...] = jnp.zeros_like(dw_ref)

        dcb = d_ref[...].astype(BF16)
        r, c = _iota2(tt)
        moved = [None] + [jnp.dot((c == r + sh).astype(BF16), dcb, preferred_element_type=F32)
                          for sh in range(1, CONV_W)]
        for cb in range(ch // LANES):
            cols = slice(cb * LANES, (cb + 1) * LANES)
            x_m, x_t = x_ref[0:main, cols], x_ref[main:tt, cols]
            acc_m = acc_t = None
            for j in range(CONV_W):
                sh = CONV_W - 1 - j
                w_j = cw_ref[j:j + 1, cols]
                dj_m = d_ref[0:main, cols] if sh == 0 else moved[sh][0:main, cols]
                dj_t = ds[sh:sh + tail, cols]
                acc_m = dj_m * w_j if acc_m is None else acc_m + dj_m * w_j
                acc_t = dj_t * w_j if acc_t is None else acc_t + dj_t * w_j
                dw_ref[j:j + 1, cols] += (jnp.sum(dj_m * x_m, axis=0, keepdims=True)
                                          + jnp.sum(dj_t * x_t, axis=0, keepdims=True))
            dx_ref[0:main, cols] = acc_m.astype(BF16)
            dx_ref[main:tt, cols] = acc_t.astype(BF16)

    return pl.pallas_call(
        body, name=name, grid=(nt,),
        in_specs=[cur, cur, nxt, pl.BlockSpec((CONV_W, ch), lambda i: (0, 0)), pl.BlockSpec(memory_space=pl.ANY)],
        out_specs=[pl.BlockSpec((tt, ch), lambda i: (i, 0)), pl.BlockSpec((HALO, ch), lambda i: (0, 0))],
        out_shape=[jax.ShapeDtypeStruct(dproj.shape, BF16), jax.ShapeDtypeStruct((HALO, ch), F32)],
        scratch_shapes=[pltpu.VMEM((tail + HALO, ch), F32)],
        input_output_aliases={4: 0},
        compiler_params=_params(("arbitrary",)),
    )(proj, dc, dc, conv_w, dproj)


def _bg_fwd(projs, alog_row, dtb_row, *, name):
    t = projs.shape[0]
    tt = _pick(t, (512, 256))

    def body(p_ref, al_ref, db_ref, o_ref):
        p = p_ref[...]
        lane = lax.broadcasted_iota(jnp.int32, p.shape, 1)
        z = p + db_ref[...]
        sp = jnp.maximum(z, 0.0) + jnp.log1p(jnp.exp(-jnp.abs(z)))
        g = -jnp.exp(al_ref[...]) * sp
        o_ref[...] = jnp.where(lane < HEADS, _sigmoid(p), jnp.where(lane < 2 * HEADS, g, 0.0))

    blk = pl.BlockSpec((tt, LANES), lambda i: (i, 0))
    row = pl.BlockSpec((1, LANES), lambda i: (0, 0))
    return pl.pallas_call(
        body, name=name, grid=(t // tt,), in_specs=[blk, row, row], out_specs=blk,
        out_shape=jax.ShapeDtypeStruct((t, LANES), F32), compiler_params=_params(("parallel",)),
    )(projs, alog_row, dtb_row)


def _bg_bwd(projs, dbg, alog_row, dtb_row, *, name):
    t = projs.shape[0]
    tt = _pick(t, (512, 256))

    def body(p_ref, d_ref, al_ref, db_ref, o_ref, dal_ref, ddb_ref):
        p, d = p_ref[...], d_ref[...]
        lane = lax.broadcasted_iota(jnp.int32, p.shape, 1)
        isg = (lane >= HEADS) & (lane < 2 * HEADS)
        be = _sigmoid(p)
        z = p + db_ref[...]
        sp = jnp.maximum(z, 0.0) + jnp.log1p(jnp.exp(-jnp.abs(z)))
        ea = jnp.exp(al_ref[...])
        d_aa = jnp.where(isg, d * (-ea) * _sigmoid(z), 0.0)
        o_ref[...] = jnp.where(lane < HEADS, d * be * (1.0 - be), d_aa).astype(BF16)

        @pl.when(pl.program_id(0) == 0)
        def _():
            dal_ref[...] = jnp.zeros_like(dal_ref)
            ddb_ref[...] = jnp.zeros_like(ddb_ref)

        dal_ref[...] += jnp.sum(jnp.where(isg, d * (-ea) * sp, 0.0), axis=0, keepdims=True)
        ddb_ref[...] += jnp.sum(d_aa, axis=0, keepdims=True)

    blk = pl.BlockSpec((tt, LANES), lambda i: (i, 0))
    row = pl.BlockSpec((1, LANES), lambda i: (0, 0))
    return pl.pallas_call(
        body, name=name, grid=(t // tt,), in_specs=[blk, blk, row, row], out_specs=[blk, row, row],
        out_shape=[jax.ShapeDtypeStruct((t, LANES), BF16), jax.ShapeDtypeStruct((1, LANES), F32),
                   jax.ShapeDtypeStruct((1, LANES), F32)],
        compiler_params=_params(("arbitrary",)),
    )(projs, dbg, alog_row, dtb_row)


def _hn_specs(n_chunks, rev=False):
    def nn(n):
        return n_chunks - 1 - n if rev else n
    tok = lambda w: pl.BlockSpec((CHUNK, HB * w), lambda g, n: (nn(n), g))
    per = lambda a, b: pl.BlockSpec((HB, None, a, b), lambda g, n: (g, nn(n), 0, 0))
    return tok, per


def _hcols(hh, w):
    return slice(hh * w, (hh + 1) * w)


def _decay(gc_col, gc_row):
    r, c = _iota2(CHUNK)
    d = jnp.exp(jnp.minimum(gc_col - gc_row, 0.0))
    return jnp.where(r >= c, d, 0.0), jnp.where(r > c, d, 0.0)


def _gdn_prep(q, k, v, beta_r, g_r, *, name):
    t = q.shape[0]
    n_chunks = t // CHUNK
    cps = 4 if n_chunks % 4 == 0 else 1
    tok = lambda w: pl.BlockSpec((cps * CHUNK, HB * w), lambda g, n: (n, g))
    per = lambda a, b: pl.BlockSpec((HB, cps, a, b), lambda g, n: (g, n, 0, 0))

    def body(q_ref, k_ref, v_ref, b_ref, g_ref, gc_ref, ti_ref, u_ref, w_ref, p_ref):
        r, c = _iota2(CHUNK)
        its = [(h, cc) for cc in range(cps) for h in range(HB)]
        rows = lambda cc: slice(cc * CHUNK, (cc + 1) * CHUNK)
        n = range(len(its))
        kk = [k_ref[rows(cc), _hcols(h, DK)] for h, cc in its]
        gc_col = [jnp.sum(jnp.where(c <= r, jnp.broadcast_to(g_ref[h, cc], (CHUNK, CHUNK)), 0.0), axis=1,
                          keepdims=True) for h, cc in its]
        gc_row = [_col_to_row(gc_col[i]) for i in n]
        beta_col = [_row_to_col(b_ref[h, cc]) for h, cc in its]
        dec = [_decay(gc_col[i], gc_row[i]) for i in n]
        kb = [kk[i] * beta_col[i] for i in n]
        for i, (h, cc) in enumerate(its):
            gc_ref[h, cc] = gc_row[i]
            p_ref[h, cc] = (_dot_nt(q_ref[rows(cc), _hcols(h, DK)], kk[i]) * dec[i][0]).astype(BF16)
        pw = [_dot_nt(kb[i], kk[i]) * dec[i][1] for i in n]
        y = [-pw[i] for i in n]
        for _ in range(5):
            pw = [_dot(pw[i], pw[i]) for i in n]
            yp = [_dot(y[i], pw[i]) for i in n]
            y = [y[i] + pw[i] + yp[i] for i in n]
        vb = [v_ref[rows(cc), _hcols(h, DV)] * beta_col[i] for i, (h, cc) in enumerate(its)]
        kbe = [kb[i] * jnp.exp(gc_col[i]) for i in n]
        yv = [_dot(y[i], vb[i]) for i in n]
        yk = [_dot(y[i], kbe[i]) for i in n]
        for i, (h, cc) in enumerate(its):
            ti_ref[h, cc] = y[i].astype(BF16)
            u_ref[rows(cc), _hcols(h, DV)] = vb[i] + yv[i]
            w_ref[rows(cc), _hcols(h, DK)] = (kbe[i] + yk[i]).astype(BF16)

    return pl.pallas_call(
        body, name=name, grid=(HEADS // HB, n_chunks // cps),
        in_specs=[tok(DK), tok(DK), tok(DV), per(1, CHUNK), per(1, CHUNK)],
        out_specs=[per(1, CHUNK), per(CHUNK, CHUNK), tok(DV), tok(DK), per(CHUNK, CHUNK)],
        out_shape=[jax.ShapeDtypeStruct((HEADS, n_chunks, 1, CHUNK), F32),
                   jax.ShapeDtypeStruct((HEADS, n_chunks, CHUNK, CHUNK), BF16),
                   jax.ShapeDtypeStruct((t, HEADS * DV), F32), jax.ShapeDtypeStruct((t, HEADS * DK), BF16),
                   jax.ShapeDtypeStruct((HEADS, n_chunks, CHUNK, CHUNK), BF16)],
        compiler_params=_params(("parallel", "parallel")),
    )(q, k, v, beta_r, g_r)


def _interleave(*step_gens):
    live = list(step_gens)
    while live:
        for g in list(live):
            try:
                next(g)
            except StopIteration:
                live.remove(g)


def _gdn_scan_steps(q_ref, k_ref, u_ref, w_ref, p_ref, gc_ref, o_ref, s_ref, st):
    hs = range(HB)
    s = [st[h] for h in hs]
    gc_row = [gc_ref[h] for h in hs]
    gc_col = [_row_to_col(gc_row[h]) for h in hs]
    glast = [gc_row[h][:, CHUNK - 1:CHUNK] for h in hs]
    for h in hs:
        s_ref[h] = s[h].astype(BF16)
    yield
    ws = [_dot(w_ref[:, _hcols(h, DK)], s[h]) for h in hs]
    yield
    qs = [_dot(q_ref[:, _hcols(h, DK)] * jnp.exp(gc_col[h]), s[h]) for h in hs]
    vn = [u_ref[:, _hcols(h, DV)] - ws[h] for h in hs]
    yield
    pv = [_dot(p_ref[h], vn[h]) for h in hs]
    yield
    kv = [_dot_tn(k_ref[:, _hcols(h, DK)] * jnp.exp(glast[h] - gc_col[h]), vn[h]) for h in hs]
    yield
    for h in hs:
        o_ref[:, _hcols(h, DV)] = (qs[h] + pv[h]).astype(BF16)
        st[h] = s[h] * jnp.exp(glast[h]) + kv[h]


def _gdn_bwd_steps(q_ref, k_ref, v_ref, b_ref, gc_ref, ti_ref, u_ref, w_ref, s_ref, do_ref,
                   dq_ref, dk_ref, dv_ref, db_ref, dg_ref, dst):
    r, c = _iota2(CHUNK)
    rows = lax.broadcasted_iota(jnp.int32, (CHUNK, 1), 0)
    rsum = lambda a: jnp.sum(a, axis=1, keepdims=True)
    hs = range(HB)
    qq = [q_ref[:, _hcols(h, DK)] for h in hs]
    kk = [k_ref[:, _hcols(h, DK)] for h in hs]
    ww = [w_ref[:, _hcols(h, DK)] for h in hs]
    uu = [u_ref[:, _hcols(h, DV)] for h in hs]
    d_o = [do_ref[:, _hcols(h, DV)] for h in hs]
    s = [s_ref[h].astype(F32) for h in hs]
    d_s = [dst[h] for h in hs]
    gc_row = [gc_ref[h] for h in hs]
    gc_col = [_row_to_col(gc_row[h]) for h in hs]
    beta_col = [_row_to_col(b_ref[h]) for h in hs]
    eg = [jnp.exp(gc_col[h]) for h in hs]
    glast = [gc_row[h][:, CHUNK - 1:CHUNK] for h in hs]
    eglast = [jnp.exp(glast[h]) for h in hs]
    e2 = [jnp.exp(glast[h] - gc_col[h]) for h in hs]
    dec = [_decay(gc_col[h], gc_row[h]) for h in hs]
    kb = [kk[h] * beta_col[h] for h in hs]
    ke = [kk[h] * e2[h] for h in hs]
    qe = [qq[h] * eg[h] for h in hs]
    yield
    kkt = [_dot_nt(kb[h], kk[h]) for h in hs]
    qk = [_dot_nt(qq[h], kk[h]) for h in hs]
    ws = [_dot(ww[h], s[h]) for h in hs]
    yield
    a = [kkt[h] * dec[h][1] for h in hs]
    pp = [qk[h] * dec[h][0] for h in hs]
    vn = [uu[h] - ws[h] for h in hs]
    t1 = [_dot_tn(pp[h], d_o[h]) for h in hs]
    t2 = [_dot(ke[h], d_s[h]) for h in hs]
    yield
    dqe = [_dot_nt(d_o[h], s[h]) for h in hs]
    dke = [_dot_nt(vn[h], d_s[h]) for h in hs]
    dqk = [_dot_nt(d_o[h], vn[h]) * dec[h][0] for h in hs]
    yield
    dvn = [t1[h] + t2[h] for h in hs]
    t3 = [_dot_tn(qe[h], d_o[h]) for h in hs]
    t4 = [_dot_tn(ww[h], dvn[h]) for h in hs]
    dw = [-_dot_nt(dvn[h], s[h]) for h in hs]
    yield
    de2 = [rsum(dke[h] * ke[h]) for h in hs]
    dglast = [jnp.sum(de2[h], axis=0, keepdims=True)
              + eglast[h] * jnp.sum(rsum(d_s[h] * s[h]), axis=0, keepdims=True) for h in hs]
    for h in hs:
        dst[h] = d_s[h] * eglast[h] + t3[h] - t4[h]
    yy = [ti_ref[h] for h in hs]
    t5 = [_dot_tn(yy[h], dvn[h]) for h in hs]
    t6 = [_dot_tn(yy[h], dw[h]) for h in hs]
    yield
    dvb = [dvn[h] + t5[h] for h in hs]
    dkbe = [dw[h] + t6[h] for h in hs]
    t7 = [_dot_nt(dvb[h], uu[h]) for h in hs]
    t8 = [_dot_nt(dkbe[h], ww[h]) for h in hs]
    yield
    d_a = [jnp.where(r > c, -(t7[h] + t8[h]), 0.0) for h in hs]
    dm = [d_a[h] * dec[h][1] for h in hs]
    t9 = [_dot(dm[h], kk[h]) for h in hs]
    t10 = [_dot(dqk[h], kk[h]) for h in hs]
    yield
    t11 = [_dot_tn(dqk[h], qq[h]) for h in hs]
    t12 = [_dot_tn(dm[h], kb[h]) for h in hs]
    yield
    for h in hs:
        dkb = t9[h] + dkbe[h] * eg[h]
        e_mat = d_a[h] * a[h] + dqk[h] * qk[h]
        dgc = (rsum(dqe[h] * qe[h]) - de2[h] + rsum(dkbe[h] * kb[h] * eg[h]) + rsum(e_mat)
               - _row_to_col(jnp.sum(e_mat, axis=0, keepdims=True)))
        dgc = dgc + jnp.where(rows == CHUNK - 1, dglast[h], 0.0)
        dq_ref[:, _hcols(h, DK)] = dqe[h] * eg[h] + t10[h]
        dk_ref[:, _hcols(h, DK)] = t11[h] + dke[h] * e2[h] + t12[h] + dkb * beta_col[h]
        dv_ref[:, _hcols(h, DV)] = dvb[h] * beta_col[h]
        dbeta = rsum(dkb * kk[h]) + rsum(dvb[h] * v_ref[:, _hcols(h, DV)])
        db_ref[h] = _col_to_row(dbeta)
        dg_ref[h] = jnp.sum(jnp.where(r >= c, jnp.broadcast_to(dgc, (CHUNK, CHUNK)), 0.0), axis=0, keepdims=True)


def _rot(x, cos2, sin2):
    return x * cos2 + pltpu.roll(x, DK // 2, 1) * sin2


def _unrot(d, cos2, sin2):
    return d * cos2 + pltpu.roll(d * sin2, DK // 2, 1)


def _ret_consts(lg):
    r, c = _iota2(CHUNK)
    dm = jnp.where(r >= c, jnp.exp((r - c).astype(F32) * lg), 0.0)
    pos = lax.broadcasted_iota(jnp.int32, (CHUNK, 1), 0).astype(F32)
    return dm, jnp.exp((pos + 1.0) * lg), jnp.exp((CHUNK - 1.0 - pos) * lg), jnp.exp(CHUNK * lg)


def _ret_fwd_steps(q_ref, k_ref, v_ref, cos_ref, sin_ref, lg_ref, o_ref, s_ref, st):
    cos2v, sin2v = cos_ref[...], sin_ref[...]
    hs = range(HB)
    s = [st[h] for h in hs]
    for h in hs:
        s_ref[h] = s[h].astype(BF16)
    cst = [_ret_consts(lg_ref[h][:, 0:1]) for h in hs]
    qq = [_rot(q_ref[:, _hcols(h, DK)], cos2v, sin2v) for h in hs]
    kk = [_rot(k_ref[:, _hcols(h, DK)], cos2v, sin2v) * (DK ** -0.5) for h in hs]
    vv = [v_ref[:, _hcols(h, DV)] for h in hs]
    yield
    sc = [_dot_nt(qq[h], kk[h]) * cst[h][0] for h in hs]
    yield
    qs = [_dot(qq[h], s[h]) for h in hs]
    yield
    kv = [_dot_tn(kk[h] * cst[h][2], vv[h]) for h in hs]
    yield
    sv = [_dot(sc[h], vv[h]) for h in hs]
    yield
    for h in hs:
        o_ref[:, _hcols(h, DV)] = (sv[h] + qs[h] * cst[h][1]).astype(BF16)
        st[h] = s[h] * cst[h][3] + kv[h]


def _mix_fwd(q, k, u, w, p, gc_r, proj, cos2, sin2, lg_tab, col_q, col_k, col_v, *, name):
    t = q.shape[0]
    n_chunks = t // CHUNK
    tok, per = _hn_specs(n_chunks)
    bq, bk, bv = col_q // (HB * DK), col_k // (HB * DK), col_v // (HB * DV)

    def body(*refs):
        st_a, st_b = refs[-2:]

        @pl.when(pl.program_id(1) == 0)
        def _():
            st_a[...] = jnp.zeros_like(st_a)
            st_b[...] = jnp.zeros_like(st_b)

        _interleave(_gdn_scan_steps(*refs[0:6], refs[12], refs[13], st_a),
                    _ret_fwd_steps(*refs[6:12], refs[14], refs[15], st_b))

    state = jax.ShapeDtypeStruct((HEADS, n_chunks, DK, DV), BF16)
    return pl.pallas_call(
        body, name=name, grid=(HEADS // HB, n_chunks),
        in_specs=[tok(DK), tok(DK), tok(DV), tok(DK), per(CHUNK, CHUNK), per(1, CHUNK),
                  pl.BlockSpec((CHUNK, HB * DK), lambda g, n: (n, bq + g)),
                  pl.BlockSpec((CHUNK, HB * DK), lambda g, n: (n, bk + g)),
                  pl.BlockSpec((CHUNK, HB * DV), lambda g, n: (n, bv + g)),
                  pl.BlockSpec((CHUNK, DK), lambda g, n: (n, 0)), pl.BlockSpec((CHUNK, DK), lambda g, n: (n, 0)),
                  pl.BlockSpec((HB, 1, LANES), lambda g, n: (g, 0, 0))],
        out_specs=[tok(DV), per(DK, DV), tok(DV), per(DK, DV)],
        out_shape=[jax.ShapeDtypeStruct((t, HEADS * DV), BF16), state, jax.ShapeDtypeStruct((t, HEADS * DV), BF16), state],
        scratch_shapes=[pltpu.VMEM((HB, DK, DV), F32)] * 2,
        compiler_params=_params(("arbitrary", "arbitrary")),
    )(q, k, u, w, p, gc_r, proj, proj, proj, cos2, sin2, lg_tab)


def _ret_bwd_steps(q_ref, k_ref, v_ref, cos_ref, sin_ref, lg_ref, s_ref, do_ref, d_ref, dst):
    nqk = HEADS * DK
    cos2v, sin2v = cos_ref[...], sin_ref[...]
    hs = range(HB)
    s = [s_ref[h].astype(F32) for h in hs]
    d_s = [dst[h] for h in hs]
    d_o = [do_ref[:, _hcols(h, DV)] for h in hs]
    cst = [_ret_consts(lg_ref[h][:, 0:1]) for h in hs]
    qq = [_rot(q_ref[:, _hcols(h, DK)], cos2v, sin2v) for h in hs]
    kk = [_rot(k_ref[:, _hcols(h, DK)], cos2v, sin2v) * (DK ** -0.5) for h in hs]
    vv = [v_ref[:, _hcols(h, DV)] for h in hs]
    dxo = [d_o[h] * cst[h][1] for h in hs]
    yield
    sc = [_dot_nt(qq[h], kk[h]) * cst[h][0] for h in hs]
    dsc = [_dot_nt(d_o[h], vv[h]) * cst[h][0] for h in hs]
    yield
    t1 = [_dot(kk[h] * cst[h][2], d_s[h]) for h in hs]
    t2 = [_dot_nt(dxo[h], s[h]) for h in hs]
    yield
    t3 = [_dot_nt(vv[h], d_s[h]) for h in hs]
    t4 = [_dot_tn(qq[h], dxo[h]) for h in hs]
    yield
    t5 = [_dot_tn(sc[h], d_o[h]) for h in hs]
    t6 = [_dot(dsc[h], kk[h]) for h in hs]
    yield
    t7 = [_dot_tn(dsc[h], qq[h]) for h in hs]
    yield
    for h in hs:
        dst[h] = d_s[h] * cst[h][3] + t4[h]
        d_ref[:, 2 * nqk + h * DV:2 * nqk + (h + 1) * DV] = (t5[h] + t1[h]).astype(BF16)
        d_ref[:, _hcols(h, DK)] = _unrot(t6[h] + t2[h], cos2v, sin2v).astype(BF16)
        d_ref[:, nqk + h * DK:nqk + (h + 1) * DK] = (
            _unrot(t7[h] + t3[h] * cst[h][2], cos2v, sin2v) * (DK ** -0.5)).astype(BF16)


def _mix_bwd(q, k, v, beta_r, gc_r, ti, u, w, s_a, do_a, proj, cos2, sin2, lg_tab, s_b, do_b, dproj,
             col_q, col_k, col_v, *, name):
    t = q.shape[0]
    n_chunks = t // CHUNK
    tok, per = _hn_specs(n_chunks, rev=True)
    bq, bk, bv = col_q // (HB * DK), col_k // (HB * DK), col_v // (HB * DV)
    rv = lambda n: n_chunks - 1 - n
    nqk, wid = HEADS * DK, HEADS * (2 * DK + DV)
    assert HB == HEADS and col_k == col_q + nqk and col_v == col_k + nqk and col_q % wid == 0
    n_a, n_b = 10, 8

    def body(*refs):
        ins_a, ins_b = refs[:n_a], refs[n_a:n_a + n_b]
        outs = refs[n_a + n_b + 1:n_a + n_b + 7]
        dst_a, dst_b = refs[-2:]

        @pl.when(pl.program_id(1) == 0)
        def _():
            dst_a[...] = jnp.zeros_like(dst_a)
            dst_b[...] = jnp.zeros_like(dst_b)

        _interleave(_gdn_bwd_steps(*ins_a, *outs[:5], dst_a), _ret_bwd_steps(*ins_b, outs[5], dst_b))

    rows = jax.ShapeDtypeStruct((HEADS, n_chunks, 1, CHUNK), F32)
    return pl.pallas_call(
        body, name=name, grid=(HEADS // HB, n_chunks),
        in_specs=[tok(DK), tok(DK), tok(DV), per(1, CHUNK), per(1, CHUNK), per(CHUNK, CHUNK), tok(DV), tok(DK),
                  per(DK, DV), tok(DV),
                  pl.BlockSpec((CHUNK, HB * DK), lambda g, n: (rv(n), bq + g)),
                  pl.BlockSpec((CHUNK, HB * DK), lambda g, n: (rv(n), bk + g)),
                  pl.BlockSpec((CHUNK, HB * DV), lambda g, n: (rv(n), bv + g)),
                  pl.BlockSpec((CHUNK, DK), lambda g, n: (rv(n), 0)),
                  pl.BlockSpec((CHUNK, DK), lambda g, n: (rv(n), 0)),
                  pl.BlockSpec((HB, 1, LANES), lambda g, n: (g, 0, 0)),
                  per(DK, DV), tok(DV), pl.BlockSpec(memory_space=pl.ANY)],
        out_specs=[tok(DK), tok(DK), tok(DV), per(1, CHUNK), per(1, CHUNK),
                   pl.BlockSpec((CHUNK, wid), lambda g, n: (rv(n), col_q // wid))],
        out_shape=[jax.ShapeDtypeStruct((t, HEADS * DK), F32), jax.ShapeDtypeStruct((t, HEADS * DK), F32),
                   jax.ShapeDtypeStruct((t, HEADS * DV), F32), rows, rows, jax.ShapeDtypeStruct(dproj.shape, BF16)],
        scratch_shapes=[pltpu.VMEM((HB, DK, DV), F32)] * 2,
        input_output_aliases={n_a + n_b: 5},
        compiler_params=_params(("arbitrary", "arbitrary")),
    )(q, k, v, beta_r, gc_r, ti, u, w, s_a, do_a, proj, proj, proj, cos2, sin2, lg_tab, s_b, do_b, dproj)


def _merge_parts(oa, ob, z, rg, ga, gb):
    oa, ob = oa.astype(F32), ob.astype(F32)
    ra = lax.rsqrt(jnp.mean(oa * oa, axis=-1, keepdims=True) + EPS)
    xa = oa * ra
    mu = jnp.mean(ob, axis=-1, keepdims=True)
    cen = ob - mu
    rb = lax.rsqrt(jnp.mean(cen * cen, axis=-1, keepdims=True) + EPS)
    xb = cen * rb
    sz, sr = _sigmoid(z), _sigmoid(rg)
    return ra, xa, rb, xb, sz, sr, _sigmoid(ga), _sigmoid(gb)


def _merge_specs(tt, d, cz, crg, cga, cgb):
    blk = pl.BlockSpec((tt, d), lambda i: (i, 0))
    pcol = lambda c: pl.BlockSpec((tt, d), lambda i: (i, c // d))
    row = pl.BlockSpec((1, d), lambda i: (0, 0))
    return blk, [blk, blk, pcol(cz), pcol(crg), pcol(cga), pcol(cgb), row, row], row


def _merge_fwd(oa, ob, proj, wa, wb, cz, crg, cga, cgb, *, name):
    t, d = oa.shape
    tt = _pick(t, (256,))
    blk, in_specs, _ = _merge_specs(tt, d, cz, crg, cga, cgb)

    def body(oa_ref, ob_ref, z_ref, rg_ref, ga_ref, gb_ref, wa_ref, wb_ref, o_ref):
        for h in range(HEADS):
            cols = slice(h * DV, (h + 1) * DV)
            z, rg = z_ref[:, cols], rg_ref[:, cols]
            _, xa, _, xb, sz, sr, sga, sgb = _merge_parts(
                oa_ref[:, cols], ob_ref[:, cols], z, rg, ga_ref[:, cols], gb_ref[:, cols])
            o_a = xa * wa_ref[:, cols] * (z * sz)
            o_b = xb * wb_ref[:, cols] * (rg * sr)
            o_ref[:, cols] = (sga * o_a + sgb * o_b).astype(BF16)

    return pl.pallas_call(
        body, name=name, grid=(t // tt,), in_specs=in_specs, out_specs=blk,
        out_shape=jax.ShapeDtypeStruct((t, d), BF16), compiler_params=_params(("parallel",)),
    )(oa, ob, proj, proj, proj, proj, wa, wb)


def _merge_bwd(dmix, oa, ob, proj, wa, wb, cz, crg, cga, cgb, *, name, side):
    t, d = oa.shape
    tt = _pick(t, (256,))
    nt = t // tt
    blk, in_specs, row = _merge_specs(tt, d, cz, crg, cga, cgb)
    assert (crg, cga, cgb) == (cz + d, cz + 2 * d, cz + 3 * d) and cz % (4 * d) == 0
    n_si, n_so = len(side.ins), len(side.out_shapes)
    assert len(side.phases) == 2

    def body(*refs):
        dm_ref, oa_ref, ob_ref, z_ref, rg_ref, ga_ref, gb_ref, wa_ref, wb_ref = refs[:9]
        s_in = refs[9:9 + n_si]
        doa_ref, dob_ref, dp_ref, dwa_ref, dwb_ref = refs[9 + n_si:14 + n_si]
        s_out, ssem, rsem = refs[14 + n_si:14 + n_si + n_so], refs[-2], refs[-1]
        dz_ref, drg_ref, dga_ref, dgb_ref = [dp_ref.at[:, i * d:(i + 1) * d] for i in range(4)]

        @pl.when(pl.program_id(0) == 0)
        def _():
            side.phases[0](s_in, s_out, ssem, rsem)
            dwa_ref[...] = jnp.zeros_like(dwa_ref)
            dwb_ref[...] = jnp.zeros_like(dwb_ref)

        for h in range(HEADS):
            cols = slice(h * DV, (h + 1) * DV)
            z, rg, wa_h, wb_h, dmx = z_ref[:, cols], rg_ref[:, cols], wa_ref[:, cols], wb_ref[:, cols], dm_ref[:, cols]
            ra, xa, rb, xb, sz, sr, sga, sgb = _merge_parts(
                oa_ref[:, cols], ob_ref[:, cols], z, rg, ga_ref[:, cols], gb_ref[:, cols])
            na, nb = xa * wa_h, xb * wb_h
            sil_z, sil_r = z * sz, rg * sr
            o_a, o_b = na * sil_z, nb * sil_r
            dga_ref[:, cols] = (dmx * o_a * sga * (1.0 - sga)).astype(BF16)
            dgb_ref[:, cols] = (dmx * o_b * sgb * (1.0 - sgb)).astype(BF16)
            d_oa, d_ob = dmx * sga, dmx * sgb
            dz_ref[:, cols] = (d_oa * na * sz * (1.0 + z * (1.0 - sz))).astype(BF16)
            drg_ref[:, cols] = (d_ob * nb * sr * (1.0 + rg * (1.0 - sr))).astype(BF16)
            dna, dnb = d_oa * sil_z, d_ob * sil_r
            dwa_ref[:, cols] += jnp.sum(dna * xa, axis=0, keepdims=True)
            dwb_ref[:, cols] += jnp.sum(dnb * xb, axis=0, keepdims=True)
            gwa, gwb = dna * wa_h, dnb * wb_h
            doa_ref[:, cols] = ra * (gwa - xa * jnp.mean(gwa * xa, axis=-1, keepdims=True))
            dob_ref[:, cols] = rb * (gwb - jnp.mean(gwb, axis=-1, keepdims=True)
                                     - xb * jnp.mean(gwb * xb, axis=-1, keepdims=True))

        @pl.when(pl.program_id(0) == nt - 1)
        def _():
            side.phases[1](s_in, s_out, ssem, rsem)

    hbm = pl.BlockSpec(memory_space=pl.ANY)
    return pl.pallas_call(
        body, name=name, grid=(nt,), in_specs=[blk] + in_specs + [hbm] * n_si,
        out_specs=[blk, blk, pl.BlockSpec((tt, 4 * d), lambda i: (i, cz // (4 * d))), row, row] + [hbm] * n_so,
        out_shape=[jax.ShapeDtypeStruct((t, d), F32)] * 2 + [jax.ShapeDtypeStruct(proj.shape, BF16)]
        + [jax.ShapeDtypeStruct((1, d), F32)] * 2 + side.out_shapes,
        scratch_shapes=side.sems(),
        compiler_params=_params(("arbitrary",)),
    )(dmix, oa, ob, proj, proj, proj, proj, wa, wb, *side.ins)


def _row_block(rows, cols, itemsize=4, target=1 << 20):
    for rb in (512, 256, 128, 64, 32, 16, 8):
        if rows % rb == 0 and rb * cols * itemsize <= target:
            return rb
    return rows


def _adamw(w, g, m, v, *, name):
    rows, cols = w.shape
    rb = _row_block(rows, cols)

    def body(w_ref, g_ref, m_ref, v_ref, d_ref, nm_ref, nv_ref):
        gg = g_ref[...]
        mm = ADAM_B1 * m_ref[...] + (1.0 - ADAM_B1) * gg
        vv = ADAM_B2 * v_ref[...] + (1.0 - ADAM_B2) * (gg * gg)
        m_hat = mm / (1.0 - ADAM_B1 ** ADAM_STEP)
        v_hat = vv / (1.0 - ADAM_B2 ** ADAM_STEP)
        d_ref[...] = -ADAM_LR * (m_hat / (jnp.sqrt(v_hat) + ADAM_EPS) + ADAM_WD * w_ref[...])
        nm_ref[...] = mm
        nv_ref[...] = vv

    blk = pl.BlockSpec((rb, cols), lambda i: (i, 0))
    return pl.pallas_call(
        body, name=name, grid=(rows // rb,), in_specs=[blk] * 4, out_specs=[blk] * 3,
        out_shape=[jax.ShapeDtypeStruct((rows, cols), F32)] * 3, compiler_params=_params(("parallel",)),
    )(w, g, m, v)


def _adamw_halves(w, mine, other, c_idx, m, v, *, name):
    rows, cols = w.shape
    hr = rows // 2
    rb = _row_block(hr, cols, 4, 2 << 20)
    nb = hr // rb

    def body(c_ref, w_ref, a_ref, b_ref, m_ref, v_ref, g_ref, d_ref, nm_ref, nv_ref):
        gg = jnp.where(pl.program_id(0) // nb == c_ref[0], a_ref[...], b_ref[...])
        mm = ADAM_B1 * m_ref[...] + (1.0 - ADAM_B1) * gg
        vv = ADAM_B2 * v_ref[...] + (1.0 - ADAM_B2) * (gg * gg)
        m_hat = mm / (1.0 - ADAM_B1 ** ADAM_STEP)
        v_hat = vv / (1.0 - ADAM_B2 ** ADAM_STEP)
        g_ref[...] = gg
        d_ref[...] = -ADAM_LR * (m_hat / (jnp.sqrt(v_hat) + ADAM_EPS) + ADAM_WD * w_ref[...])
        nm_ref[...] = mm
        nv_ref[...] = vv

    blk = pl.BlockSpec((rb, cols), lambda i, c: (i, 0))
    half = pl.BlockSpec((rb, cols), lambda i, c: (i % nb, 0))
    return pl.pallas_call(
        body, name=name,
        grid_spec=pltpu.PrefetchScalarGridSpec(
            num_scalar_prefetch=1, grid=(rows // rb,), in_specs=[blk, half, half, blk, blk], out_specs=[blk] * 4),
        out_shape=[jax.ShapeDtypeStruct((rows, cols), F32)] * 4, compiler_params=_params(("parallel",)),
    )(c_idx, w, mine, other, m, v)


def _pair_add(g, rsib, c_idx, *, name):
    _, _, hr, cols = g.shape
    rb = _row_block(hr, cols, 2, 4 << 20)

    def body(c_ref, g_ref, r_ref, o_ref):
        o_ref[...] = (g_ref[...].astype(F32) + r_ref[...].astype(F32)).astype(BF16)

    return pl.pallas_call(
        body, name=name,
        grid_spec=pltpu.PrefetchScalarGridSpec(
            num_scalar_prefetch=1, grid=(N_CHIPS, hr // rb),
            in_specs=[pl.BlockSpec((None, None, rb, cols), lambda j, i, c: (j, c[0], i, 0)),
                      pl.BlockSpec((None, rb, cols), lambda j, i, c: (j, i, 0))],
            out_specs=pl.BlockSpec((None, rb, cols), lambda j, i, c: (j, i, 0))),
        out_shape=jax.ShapeDtypeStruct((N_CHIPS, hr, cols), BF16),
        compiler_params=_params(("parallel", "parallel")),
    )(c_idx, g, rsib)


def _chip_sum(qb, *, name):
    _, hr, cols = qb.shape
    rb = _row_block(hr, cols, 2, 1 << 20)

    def body(q_ref, o_ref):
        acc = q_ref[0].astype(F32)
        for i in range(1, N_CHIPS):
            acc = acc + q_ref[i].astype(F32)
        o_ref[...] = acc

    return pl.pallas_call(
        body, name=name, grid=(hr // rb,),
        in_specs=[pl.BlockSpec((N_CHIPS, rb, cols), lambda i: (0, i, 0))],
        out_specs=pl.BlockSpec((rb, cols), lambda i: (i, 0)),
        out_shape=jax.ShapeDtypeStruct((hr, cols), F32), compiler_params=_params(("parallel",)),
    )(qb)


def _place():
    x, y, c = lax.axis_index("x"), lax.axis_index("y"), lax.axis_index("c")
    return x, y, c, [(1 - x, y), (x, 1 - y), (1 - x, 1 - y)]


ANY = pl.BlockSpec(memory_space=pl.ANY)


def _gather_side(shards):
    nw, n_k = len(shards), 8

    def plan(ins, outs, ssem, rsem):
        x, y, c, _ = _place()
        nbr_x, nbr_y, sib = (1 - x, y, c), (x, 1 - y, c), (x, y, 1 - c)
        s_me, s_x, s_y, s_d = 2 * x + y, 2 * (1 - x) + y, 2 * x + 1 - y, 2 * (1 - x) + 1 - y

        def rows(w, half, quarter=None):
            hr = shards[w].shape[0] // 2
            if quarter is None:
                return pl.ds(pl.multiple_of(half * hr, 16), hr)
            return pl.ds(pl.multiple_of(half * hr + quarter * (hr // 2), 16), hr // 2)

        def rcopy(w, k, src, slot, rws, to):
            return pltpu.make_async_remote_copy(
                src_ref=src, dst_ref=outs[w].at[slot, rws], send_sem=ssem.at[w * n_k + k],
                recv_sem=rsem.at[w * n_k + k], device_id=to, device_id_type=MESH)

        out = []
        for w in range(nw):
            mine = ins[w].at[rows(w, c)]
            q = [rows(w, c, 0), rows(w, c, 1)]
            p = functools.partial
            out.append([
                (p(rcopy, w, 0, mine, s_me, rows(w, c), nbr_x), p(rcopy, w, 0, mine, s_x, rows(w, c), nbr_x)),
                (p(rcopy, w, 1, mine, s_me, rows(w, c), nbr_y), p(rcopy, w, 1, mine, s_y, rows(w, c), nbr_y)),
                (p(rcopy, w, 2, outs[w].at[s_x, q[0]], s_x, q[0], nbr_y),
                 p(rcopy, w, 2, outs[w].at[s_x, q[0]], s_d, q[0], nbr_y)),
                (p(rcopy, w, 3, outs[w].at[s_y, q[1]], s_y, q[1], nbr_x),
                 p(rcopy, w, 3, outs[w].at[s_y, q[1]], s_d, q[1], nbr_x)),
                (p(rcopy, w, 4, outs[w].at[s_x, rows(w, c)], s_x, rows(w, c), sib),
                 p(rcopy, w, 4, mine, s_x, rows(w, 1 - c), sib)),
                (p(rcopy, w, 5, outs[w].at[s_y, rows(w, c)], s_y, rows(w, c), sib),
                 p(rcopy, w, 5, mine, s_y, rows(w, 1 - c), sib)),
                (p(rcopy, w, 6, outs[w].at[s_d, q[0]], s_d, q[0], sib),
                 p(rcopy, w, 6, outs[w].at[s_d, q[0]], s_d, rows(w, 1 - c, 0), sib)),
                (p(rcopy, w, 7, outs[w].at[s_d, q[1]], s_d, q[1], sib),
                 p(rcopy, w, 7, outs[w].at[s_d, q[1]], s_d, rows(w, 1 - c, 1), sib)),
            ])
        return out

    def send_own(*refs):
        for cps in plan(*refs):
            cps[0][0]().start()
            cps[1][0]().start()

    def relay(*refs):
        for cps in plan(*refs):
            cps[0][1]().wait_recv()
            cps[2][0]().start()
            cps[4][0]().start()
            cps[1][1]().wait_recv()
            cps[3][0]().start()
            cps[5][0]().start()

    def pass_diagonal(*refs):
        for cps in plan(*refs):
            cps[2][1]().wait_recv()
            cps[6][0]().start()
            cps[3][1]().wait_recv()
            cps[7][0]().start()

    def finish(*refs):
        for cps in plan(*refs):
            for k in (4, 5, 6, 7):
                cps[k][1]().wait_recv()
            for k in range(n_k):
                cps[k][0]().wait_send()

    return _Side(shards, [jax.ShapeDtypeStruct((N_CHIPS,) + s.shape, s.dtype) for s in shards], nw * n_k,
                 [send_own, relay, pass_diagonal, finish])


def _run_side(side, *, name):
    n_in, n_out = len(side.ins), len(side.out_shapes)

    def body(*refs):
        ins, outs = refs[:n_in], refs[n_in:n_in + n_out]
        for phase in side.phases:
            phase(ins, outs, refs[-2], refs[-1])

    return pl.pallas_call(
        body, name=name, in_specs=[ANY] * n_in, out_specs=[ANY] * n_out, out_shape=side.out_shapes,
        scratch_shapes=side.sems(), compiler_params=pltpu.CompilerParams(has_side_effects=True),
    )(*side.ins)


def _exchange_side(gs):
    nw = len(gs)

    def copies(ins, outs, ssem, rsem):
        x, y, c, _ = _place()
        return [pltpu.make_async_remote_copy(
            src_ref=ins[w].at[:, 1 - c], dst_ref=outs[w], send_sem=ssem.at[w], recv_sem=rsem.at[w],
            device_id=(x, y, 1 - c), device_id_type=MESH) for w in range(nw)]

    def start(*refs):
        for cp in copies(*refs):
            cp.start()

    def finish(*refs):
        for cp in copies(*refs):
            cp.wait()

    return _Side(gs, [jax.ShapeDtypeStruct((g.shape[0],) + g.shape[2:], g.dtype) for g in gs], nw, [start, finish])


def _scatter_side(ps):
    nw = len(ps)

    def copies(ins, outs, ssem, rsem):
        x, y, c, chips = _place()
        s_me = 2 * x + y
        return [pltpu.make_async_remote_copy(
            src_ref=ins[w].at[2 * chip[0] + chip[1]], dst_ref=outs[w].at[s_me],
            send_sem=ssem.at[w * 3 + j], recv_sem=rsem.at[w * 3 + j],
            device_id=(chip[0], chip[1], c), device_id_type=MESH)
            for w in range(nw) for j, chip in enumerate(chips)]

    def start(*refs):
        for cp in copies(*refs):
            cp.start()

    def finish(*refs):
        for cp in copies(*refs):
            cp.wait()

    return _Side(ps, [jax.ShapeDtypeStruct(p.shape, p.dtype) for p in ps], nw * 3, [start, finish])


def _share_side(hs, sem0=0):
    nw = len(hs)

    def copies(ins, outs, ssem, rsem):
        x, y, c, _ = _place()
        return [pltpu.make_async_remote_copy(
            src_ref=ins[w], dst_ref=outs[w], send_sem=ssem.at[sem0 + w], recv_sem=rsem.at[sem0 + w],
            device_id=(x, y, 1 - c), device_id_type=MESH) for w in range(nw)]

    def start(*refs):
        for cp in copies(*refs):
            cp.start()

    def finish(*refs):
        for cp in copies(*refs):
            cp.wait()

    return _Side(hs, [jax.ShapeDtypeStruct(h.shape, h.dtype) for h in hs], sem0 + nw, [start, finish])


def _join_sides(a, b):
    na_in, na_out = len(a.ins), len(a.out_shapes)
    assert len(a.phases) == len(b.phases) == 2 and b.n_sem > a.n_sem

    def phase(p):
        def run(ins, outs, ssem, rsem):
            a.phases[p](ins[:na_in], outs[:na_out], ssem, rsem)
            b.phases[p](ins[na_in:], outs[na_out:], ssem, rsem)
        return run

    return _Side(a.ins + b.ins, a.out_shapes + b.out_shapes, b.n_sem, [phase(0), phase(1)])


def _gather_small(a, *, name):
    def body(a_ref, o_ref, ssem, rsem):
        x, y, c, chips = _place()
        s_me = 2 * x + y
        o_ref[s_me] = a_ref[...]
        cps = []
        for j, chip in enumerate(chips):
            cp = pltpu.make_async_remote_copy(
                src_ref=a_ref, dst_ref=o_ref.at[s_me], send_sem=ssem.at[j], recv_sem=rsem.at[j],
                device_id=(chip[0], chip[1], c), device_id_type=MESH)
            cp.start()
            cps.append(cp)
        for cp in cps:
            cp.wait()

    vm = pl.BlockSpec(memory_space=pltpu.VMEM)
    return pl.pallas_call(
        body, name=name, in_specs=[vm], out_specs=vm,
        out_shape=jax.ShapeDtypeStruct((N_CHIPS,) + a.shape, a.dtype),
        scratch_shapes=[pltpu.SemaphoreType.DMA((3,)), pltpu.SemaphoreType.DMA((3,))],
    )(a)


def _allreduce_small(p, *, name):
    def body(p_ref, o_ref, buf, ssem, rsem):
        x, y, c, _ = _place()
        me = 4 * x + 2 * y + c
        buf[me] = p_ref[...]
        cps = []
        for k in range(1, N_DEV):
            fx, fy, fc = (k >> 2) & 1, (k >> 1) & 1, k & 1
            peer = (x + fx - 2 * x * fx, y + fy - 2 * y * fy, c + fc - 2 * c * fc)
            cp = pltpu.make_async_remote_copy(
                src_ref=buf.at[me], dst_ref=buf.at[me], send_sem=ssem.at[k - 1], recv_sem=rsem.at[k - 1],
                device_id=peer, device_id_type=MESH)
            cp.start()
            cps.append(cp)
        for cp in cps:
            cp.wait()
        acc = buf[0]
        for i in range(1, N_DEV):
            acc = acc + buf[i]
        o_ref[...] = acc

    vm = pl.BlockSpec(memory_space=pltpu.VMEM)
    return pl.pallas_call(
        body, name=name, in_specs=[vm], out_specs=vm,
        out_shape=jax.ShapeDtypeStruct(p.shape, p.dtype),
        scratch_shapes=[pltpu.VMEM((N_DEV,) + p.shape, p.dtype), pltpu.SemaphoreType.DMA((N_DEV - 1,)),
                        pltpu.SemaphoreType.DMA((N_DEV - 1,))],
    )(p)


def _rows_to_tokens(r):
    h, n = r.shape[0], r.shape[1]
    return r.reshape(h, n * CHUNK).T


def _tokens_to_rows(a):
    t, h = a.shape
    return a.T.reshape(h, t // CHUNK, 1, CHUNK)


def kernel(x, norm1_w, w_in, conv_w, a_log, dt_bias, gdn_norm_w, ret_norm_w, w_out, norm2_w, w_gate, w_up, w_down, norm_f_w, loss_target, m_norm1_w, m_w_in, m_conv_w, m_a_log, m_dt_bias, m_gdn_norm_w, m_ret_norm_w, m_w_out, m_norm2_w, m_w_gate, m_w_up, m_w_down, m_norm_f_w, v_norm1_w, v_w_in, v_conv_w, v_a_log, v_dt_bias, v_gdn_norm_w, v_ret_norm_w, v_w_out, v_norm2_w, v_w_gate, v_w_up, v_w_down, v_norm_f_w):
    t, d = x.shape[1], x.shape[2]
    f = w_gate.shape[2] * N_CHIPS
    nqk, nv = HEADS * DK, HEADS * DV
    ncs = w_in.shape[2]
    c_idx = lax.axis_index("c")
    s_idx = 2 * lax.axis_index("x") + lax.axis_index("y")
    xs = x[0]
    tgt = loss_target[0]

    n_small = 2 * HEADS
    widths = [2 * nqk + nv, nv, n_small, nqk, nqk, nv, nv, d, d]
    g_off = np.concatenate([[0], np.cumsum(widths)])
    order = [0, 3, 4, 5, 1, 6, 7, 8]
    m_off = np.concatenate([[0], np.cumsum([widths[i] for i in order])])
    o_rq, o_rk, o_rv, o_az, o_rg, o_ga, o_gb = [int(m_off[i]) for i in range(1, 8)]
    segs = [(int(g_off[i]), int(g_off[i + 1]), int(m_off[order.index(i)]) if i != 2 else None) for i in range(9)]

    def shard_pieces(s):
        out = []
        for a, b, dst in segs:
            lo, hi = max(a, s * ncs), min(b, (s + 1) * ncs)
            if lo < hi:
                out.append((lo - s * ncs, hi - s * ncs, None if dst is None else dst + lo - a))
        return out

    own = [w_in[0].astype(BF16), w_out[0].astype(BF16), w_gate[0].astype(BF16), w_up[0].astype(BF16),
           w_down[0].astype(BF16)]
    put_own = lambda full, mine: lax.dynamic_update_slice(full, mine[None], (s_idx, 0, 0))
    wg_in = put_own(_run_side(_gather_side(own[:1]), name="gather_w_in")[0], own[0])
    conv_full = _gather_small(conv_w[0], name="gather_conv_w")
    conv_full = jnp.concatenate([conv_full[i] for i in range(N_CHIPS)], axis=1)
    cuts = [(dst, s, a, b) for s in range(N_CHIPS) for a, b, dst in shard_pieces(s)]
    w_main = jnp.concatenate([wg_in[s][:, a:b] for dst, s, a, b in sorted(c for c in cuts if c[0] is not None)],
                             axis=1)
    w_small = jnp.concatenate([wg_in[s][:, a:b] for dst, s, a, b in cuts if dst is None], axis=1)
    w_small = jnp.pad(w_small, ((0, 0), (0, LANES - n_small)))

    pad16 = lambda a: jnp.pad(a, ((0, 0), (HEADS, LANES - 2 * HEADS)))
    alog_row, dtb_row = pad16(a_log), pad16(dt_bias)
    wa_row = jnp.tile(gdn_norm_w, (1, HEADS))
    inv = ROPE_BASE ** (-jnp.arange(0, DK, 2, dtype=F32) / DK)
    ang = jnp.arange(t, dtype=F32)[:, None] * inv[None, :]
    cos2 = jnp.concatenate([jnp.cos(ang), jnp.cos(ang)], axis=1)
    sin2 = jnp.concatenate([-jnp.sin(ang), jnp.sin(ang)], axis=1)
    lg = jnp.log1p(-jnp.exp2(-5.0 - jnp.arange(HEADS, dtype=F32)))
    lg_tab = jnp.broadcast_to(lg[:, None, None], (HEADS, 1, LANES))

    fq = f // N_CHIPS
    u1, projs = _rms_fwd(xs, norm1_w, w_narrow=w_small, name="rms1_fwd")
    proj, *rest = _mm(u1, w_main, tm=1024, tn=2048, tk=d, side=_gather_side(own[1:]), name="mm_proj")
    wg_out, wg_gate, wg_up, wg_down = [put_own(g, o) for g, o in zip(rest, own[1:])]
    w_o = wg_out.reshape(d, d)
    w_g, w_u = wg_gate, wg_up
    w_d = wg_down.reshape(f, d)
    q_a, k_a, v_a = _conv_fwd(proj, conv_full, name="conv_fwd")
    bg = _bg_fwd(projs, alog_row, dtb_row, name="bg_fwd")
    beta_r = _tokens_to_rows(bg[:, :HEADS])
    g_r = _tokens_to_rows(bg[:, HEADS:2 * HEADS])
    gc_r, tinv, u_a, w_a, p_a = _gdn_prep(q_a, k_a, v_a, beta_r, g_r, name="gdn_prep")
    o_a, s_a, o_b, s_b = _mix_fwd(q_a, k_a, u_a, w_a, p_a, gc_r, proj, cos2, sin2, lg_tab, o_rq, o_rk, o_rv,
                                  name="mix_fwd")
    mixed = _merge_fwd(o_a, o_b, proj, wa_row, ret_norm_w, o_az, o_rg, o_ga, o_gb, name="merge_fwd")
    h1, hn = _mm_res_norm(mixed, w_o, xs, norm2_w, tm=512, name="mm_out")
    gt, up, act = _ffn_in(hn, w_g, w_u, tm=512, name="ffn_in")
    h2 = _mm(act, w_d, res=h1, tm=512, tn=1024, tk=f, name="mm_down")
    loss_row, dh2, dh2b, d_nf = _loss_head(h2, tgt, norm_f_w.reshape(1, d), name="loss_head")

    g_down = _mm(act, dh2b, ta=True, out_dtype=BF16, tm=fq, tn=d, tk=1024, name="mm_dw_down")
    dgt, dup = _ffn_back(dh2b, w_d, gt, up, tm=512, tn=fq, name="ffn_back")
    dhn = _mm(dgt, w_g, tb=True, pair=(dup, w_u), out_dtype=BF16, tm=512, tn=d, tk=fq, name="mm_dhn")
    g_gate = _mm(hn, dgt, ta=True, out_dtype=BF16, out_stacked=True, tm=1024, tn=fq, tk=2048, name="mm_dw_gate")
    g_up = _mm(hn, dup, ta=True, out_dtype=BF16, out_stacked=True, tm=1024, tn=fq, tk=2048, name="mm_dw_up")
    dh1, dh1b, d_n2 = _rms_bwd(dhn, h1, norm2_w, dh2, bf16_copy=True, name="rms2_bwd")
    dmix = _mm(dh1b, w_o, tb=True, tm=512, tn=d, tk=d, name="mm_dmix")
    g_out = _mm(mixed, dh1b, ta=True, out_dtype=BF16, tm=1024, tn=d, tk=1024, name="mm_dw_out")
    halves = lambda g: g.reshape(N_CHIPS, 2, g.shape[1] // 2, g.shape[2])
    c_arr = jnp.reshape(c_idx, (1,)).astype(jnp.int32)
    gs_ffn = [halves(g_out.reshape(N_CHIPS, d // N_CHIPS, d)), halves(g_gate), halves(g_up),
              halves(g_down.reshape(N_CHIPS, fq, d))]
    do_a, do_b, dproj, d_wa, d_wb, *rsib = _merge_bwd(
        dmix, o_a, o_b, proj, wa_row, ret_norm_w, o_az, o_rg, o_ga, o_gb, side=_exchange_side(gs_ffn),
        name="merge_bwd")
    ps_ffn = [_pair_add(g, r, c_arr, name=f"grad_pair_add_{nm}")
              for g, r, nm in zip(gs_ffn, rsib, ["w_out", "w_gate", "w_up", "w_down"])]
    dq_a, dk_a, dv_a, dbeta_r, dg_r, dproj = _mix_bwd(
        q_a, k_a, v_a, beta_r, gc_r, tinv, u_a, w_a, s_a, do_a, proj, cos2, sin2, lg_tab, s_b, do_b, dproj,
        o_rq, o_rk, o_rv, name="mix_bwd")
    dc = _conv_bwd_pre(proj, conv_full, dq_a, dk_a, dv_a, name="conv_bwd_pre")
    dproj, d_cw = _conv_bwd(proj, dc, conv_full, dproj, name="conv_bwd")
    dbg = jnp.pad(jnp.concatenate([_rows_to_tokens(dbeta_r), _rows_to_tokens(dg_r)], axis=1),
                  ((0, 0), (0, LANES - 2 * HEADS)))
    dprojs, d_alog, d_dtb = _bg_bwd(projs, dbg, alog_row, dtb_row, name="bg_bwd")
    g_main, *qs_ffn = _mm(u1, dproj, ta=True, out_dtype=BF16, tm=1024, tn=2048, tk=2048,
                          side=_scatter_side(ps_ffn), name="mm_dw_in")
    g_small = _mm(u1, dprojs, ta=True, out_dtype=BF16, tm=1024, tn=LANES, tk=1024, name="mm_dw_in_small")
    pieces = []
    for s in range(N_CHIPS):
        seen, parts = 0, []
        for a, b, dst in shard_pieces(s):
            parts.append(g_small[:, seen:seen + b - a] if dst is None else g_main[:, dst:dst + b - a])
            seen += (b - a) if dst is None else 0
        pieces.append(jnp.concatenate(parts, axis=1))
    gs_in = [halves(jnp.stack(pieces))]
    rsib = _run_side(_exchange_side(gs_in), name="grad_pair_exchange_in")
    ps_in = [_pair_add(gs_in[0], rsib[0], c_arr, name="grad_pair_add_w_in")]
    names = ["w_in", "w_out", "w_gate", "w_up", "w_down"]
    own_slot = lambda q, p: lax.dynamic_update_slice(
        q, lax.dynamic_slice(p, (s_idx, 0, 0), (1,) + p.shape[1:]), (s_idx, 0, 0))
    hs_ffn = [_chip_sum(own_slot(q, p), name=f"grad_chip_sum_{nm}") for q, p, nm in zip(qs_ffn, ps_ffn, names[1:])]
    scatter_in = _scatter_side(ps_in)
    du, q_in, *theirs_ffn = _mm(dproj, w_main, tb=True, out_dtype=BF16, tm=1024, tn=d, tk=2048,
                                side=_join_sides(scatter_in, _share_side(hs_ffn, sem0=scatter_in.n_sem)),
                                name="mm_du")
    dx, d_n1 = _rms_bwd(du, xs, norm1_w, dh1, bf16_copy=False, narrow=(dprojs, w_small), name="rms1_bwd")

    hs_in = [_chip_sum(own_slot(q_in, ps_in[0]), name="grad_chip_sum_w_in")]
    hs = hs_in + hs_ffn
    theirs = list(_run_side(_share_side(hs_in), name="grad_pair_share_in")) + list(theirs_ffn)
    big_w = [w_in[0], w_out[0], w_gate[0], w_up[0], w_down[0]]
    big_m = [m_w_in[0], m_w_out[0], m_w_gate[0], m_w_up[0], m_w_down[0]]
    big_v = [v_w_in[0], v_w_out[0], v_w_gate[0], v_w_up[0], v_w_down[0]]
    big = {}
    for nm, mine, other, w_, m_, v_ in zip(names, hs, theirs, big_w, big_m, big_v):
        big[nm] = tuple(a[None] for a in _adamw_halves(w_, mine, other, c_arr, m_, v_, name=f"adamw_{nm}"))

    d_wa_h = jnp.sum(d_wa.reshape(HEADS, DV), axis=0, keepdims=True)
    small = [d_n1, d_alog[:, HEADS:2 * HEADS], d_dtb[:, HEADS:2 * HEADS], d_wa_h, d_wb, d_n2, d_nf,
             d_cw[:CONV_W].reshape(1, -1)]
    sizes = [a.shape[1] for a in small]
    packed = jnp.concatenate(small, axis=1)
    n_pack = packed.shape[1]
    n_rows = -(-n_pack // LANES)
    n_rows = -(-n_rows // 8) * 8
    packed = jnp.pad(packed, ((0, 0), (0, n_rows * LANES - n_pack))).reshape(n_rows, LANES)
    red = _allreduce_small(packed, name="allreduce_small").reshape(1, -1)
    offs = np.cumsum([0] + sizes)
    g_n1, g_alog, g_dtb, g_wa, g_wb, g_n2, g_nf, g_cw = [red[:, offs[i]:offs[i + 1]] for i in range(len(sizes))]
    ncw = conv_w.shape[2]
    g_cw = lax.dynamic_slice(g_cw.reshape(CONV_W, -1), (0, s_idx * ncw), (CONV_W, ncw))

    def small_update(w_, g, m_, v_, nm):
        shape = w_.shape
        pad = (-w_.size) % LANES
        to2 = lambda a: jnp.pad(a.reshape(1, -1), ((0, 0), (0, pad)))
        outs = _adamw(to2(w_), to2(g), to2(m_), to2(v_), name=f"adamw_{nm}")
        return (g.reshape(shape),) + tuple(a[:, :w_.size].reshape(shape) for a in outs)

    res = {
        "norm1_w": small_update(norm1_w, g_n1, m_norm1_w, v_norm1_w, "norm1_w"),
        "w_in": big["w_in"],
        "conv_w": small_update(conv_w, g_cw, m_conv_w, v_conv_w, "conv_w"),
        "a_log": small_update(a_log, g_alog, m_a_log, v_a_log, "a_log"),
        "dt_bias": small_update(dt_bias, g_dtb, m_dt_bias, v_dt_bias, "dt_bias"),
        "gdn_norm_w": small_update(gdn_norm_w, g_wa, m_gdn_norm_w, v_gdn_norm_w, "gdn_norm_w"),
        "ret_norm_w": small_update(ret_norm_w, g_wb, m_ret_norm_w, v_ret_norm_w, "ret_norm_w"),
        "w_out": big["w_out"],
        "norm2_w": small_update(norm2_w, g_n2, m_norm2_w, v_norm2_w, "norm2_w"),
        "w_gate": big["w_gate"],
        "w_up": big["w_up"],
        "w_down": big["w_down"],
        "norm_f_w": small_update(norm_f_w, g_nf, m_norm_f_w, v_norm_f_w, "norm_f_w"),
    }
    order = ["norm1_w", "w_in", "conv_w", "a_log", "dt_bias", "gdn_norm_w", "ret_norm_w", "w_out", "norm2_w",
             "w_gate", "w_up", "w_down", "norm_f_w"]
    loss = lax.psum(loss_row[0, 0], ("x", "y", "c"))
    return (loss, dx[None], *[res[n][0] for n in order], *[res[n][1] for n in order],
            *[res[n][2] for n in order], *[res[n][3] for n in order])
```
